```python
import jax, jax.numpy as jnp
from jax import lax
import numpy as np

D_MODEL = 1024
BATCH = 8
SEQ = 8192
DEPTH = 2

N_MIXERS = 2
S5_GROUP = 16
S5_GROUPS = D_MODEL // S5_GROUP
S5_STATE = 64
S5_CHUNK = 128
DT_MIN = 1e-3
DT_MAX = 1e-1
FOX_HEADS = 16
FOX_HEAD_DIM = D_MODEL // FOX_HEADS
Q_BLOCK = 128
FGATE_B_LO = 2.0
FGATE_B_HI = 6.0
D_FF = 2816
CONV_W = 3
EPS = 1e-6
N_S5 = (DEPTH + 1) // 2
N_FOX = DEPTH // 2

kernel_name = "hybrid_s5_fox_convffn_adaln"


def rmsnorm(x, g):
    x32 = x.astype(jnp.float32)
    y = x32 * lax.rsqrt(jnp.mean(x32 * x32, axis=-1, keepdims=True) + EPS)
    return (y * g.astype(jnp.float32)).astype(x.dtype)


def adaln_params(c, w, b):
    mod = jax.nn.silu(c) @ w + b
    shift, scale, gate = jnp.split(mod, 3, axis=-1)
    return shift[:, None, :], scale[:, None, :], gate[:, None, :]


def modulate(x, g, shift, scale):
    return rmsnorm(x, g) * (1.0 + scale) + shift


def s5_mixer(h, w_in, lam_re, lam_im, log_dt, b_re, b_im, c_re, c_im, d_skip, w_glu, w_out):
    f32 = jnp.float32
    Bsz, S, _ = h.shape
    u = h @ w_in
    ug = u.reshape(Bsz, S, S5_GROUPS, S5_GROUP)
    lre = lam_re.astype(f32)
    lim = lam_im.astype(f32)
    dt = jnp.exp(log_dt.astype(f32))[:, None]
    mag = jnp.exp(lre * dt)
    lb_re = mag * jnp.cos(lim * dt)
    lb_im = mag * jnp.sin(lim * dt)
    num_re = lb_re - 1.0
    den = lre * lre + lim * lim
    k_re = (num_re * lre + lb_im * lim) / den
    k_im = (lb_im * lre - num_re * lim) / den
    br = b_re.astype(f32)
    bi = b_im.astype(f32)
    bb_re = k_re[..., None] * br - k_im[..., None] * bi
    bb_im = k_re[..., None] * bi + k_im[..., None] * br
    cr = c_re.astype(f32)
    ci = c_im.astype(f32)

    def combine(left, right):
        a1r, a1i, b1r, b1i = left
        a2r, a2i, b2r, b2i = right
        ar = a2r * a1r - a2i * a1i
        ai = a2r * a1i + a2i * a1r
        b_r = a2r * b1r - a2i * b1i + b2r
        b_i = a2r * b1i + a2i * b1r + b2i
        return ar, ai, b_r, b_i

    def chunk_step(carry, u_c):
        h_re, h_im = carry
        bu_re = jnp.einsum('blgc,gpc->blgp', u_c, bb_re)
        bu_im = jnp.einsum('blgc,gpc->blgp', u_c, bb_im)
        bu_re = bu_re.at[:, 0].add(lb_re * h_re - lb_im * h_im)
        bu_im = bu_im.at[:, 0].add(lb_re * h_im + lb_im * h_re)
        a_re = jnp.broadcast_to(lb_re, bu_re.shape)
        a_im = jnp.broadcast_to(lb_im, bu_im.shape)
        _, _, hs_re, hs_im = lax.associative_scan(combine, (a_re, a_im, bu_re, bu_im), axis=1)
        y = jnp.einsum('blgp,gcp->blgc', hs_re, cr) - jnp.einsum('blgp,gcp->blgc', hs_im, ci)
        return (hs_re[:, -1], hs_im[:, -1]), y

    n_chunks = S // S5_CHUNK
    u_chunks = ug.reshape(Bsz, n_chunks, S5_CHUNK, S5_GROUPS, S5_GROUP).transpose(1, 0, 2, 3, 4)
    h0 = (jnp.zeros((Bsz, S5_GROUPS, S5_STATE), f32), jnp.zeros((Bsz, S5_GROUPS, S5_STATE), f32))
    _, ys = lax.scan(chunk_step, h0, u_chunks)
    y = ys.transpose(1, 0, 2, 3, 4).reshape(Bsz, S, D_MODEL)
    y = y + d_skip.astype(f32) * u.astype(f32)
    y = jax.nn.gelu(y)
    y = y * jax.nn.sigmoid(y @ w_glu.astype(f32))
    return (y @ w_out.astype(f32)).astype(h.dtype)


def fox_mixer(h, w_in, b_f, w_out):
    f32 = jnp.float32
    Bsz, S, _ = h.shape
    proj = h @ w_in
    q = proj[..., :D_MODEL].reshape(Bsz, S, FOX_HEADS, FOX_HEAD_DIM) * (FOX_HEAD_DIM ** -0.5)
    k = proj[..., D_MODEL:2 * D_MODEL].reshape(Bsz, S, FOX_HEADS, FOX_HEAD_DIM)
    v = proj[..., 2 * D_MODEL:3 * D_MODEL].reshape(Bsz, S, FOX_HEADS, FOX_HEAD_DIM)
    f_logit = proj[..., 3 * D_MODEL:]
    log_f = jax.nn.log_sigmoid((f_logit + b_f).astype(f32))
    F = lax.cumsum(log_f, axis=1).transpose(0, 2, 1)
    n_q = S // Q_BLOCK
    q_blocks = q.reshape(Bsz, n_q, Q_BLOCK, FOX_HEADS, FOX_HEAD_DIM).transpose(1, 0, 3, 2, 4)
    F_blocks = F.reshape(Bsz, FOX_HEADS, n_q, Q_BLOCK).transpose(2, 0, 1, 3)
    k_pos = jnp.arange(S)

    def attend_block(args):
        qi, q_blk, Fq = args
        s = jnp.einsum('bhqd,bshd->bhqs', q_blk, k).astype(f32)
        s = s + Fq[..., None] - F[:, :, None, :]
        q_pos = qi * Q_BLOCK + jnp.arange(Q_BLOCK)
        causal = k_pos[None, :] <= q_pos[:, None]
        s = jnp.where(causal, s, -jnp.inf)
        p = jax.nn.softmax(s, axis=-1)
        return jnp.einsum('bhqs,bshd->bqhd', p.astype(v.dtype), v)

    o = lax.map(attend_block, (jnp.arange(n_q), q_blocks, F_blocks))
    o = o.transpose(1, 0, 2, 3, 4).reshape(Bsz, S, D_MODEL)
    return o @ w_out


def conv_ffn(h, w_up, conv_w, conv_b, w_down):
    up = h @ w_up
    a, b = jnp.split(up, 2, axis=-1)
    S = a.shape[1]
    ap = jnp.pad(a, ((0, 0), (CONV_W - 1, 0), (0, 0)))
    a_conv = conv_b
    for i in range(CONV_W):
        a_conv = a_conv + ap[:, i:i + S] * conv_w[i]
    return (jax.nn.silu(a_conv) * b) @ w_down


def _fwd_setup_inputs(seed: int = 0) -> dict:
    key = jax.random.key(seed)
    ks = jax.random.split(key, 32)
    f32 = jnp.float32
    D, G, P, Cg, H, F = D_MODEL, S5_GROUPS, S5_STATE, S5_GROUP, FOX_HEADS, D_FF
    nrm = lambda k, shape, s: jax.random.normal(k, shape, f32) * s
    x = nrm(ks[0], (BATCH, SEQ, D), 1.0)
    c = nrm(ks[1], (BATCH, D), 1.0)
    norm_g = 1.0 + nrm(ks[2], (DEPTH, 2, D), 0.01)
    ada_w = nrm(ks[3], (DEPTH, 2, D, 3 * D), 0.5 * D ** -0.5)
    ada_b = nrm(ks[4], (DEPTH, 2, 3 * D), 0.01)
    s5_w_in = nrm(ks[5], (N_S5, D, D), D ** -0.5)
    s5_lam_re = -0.5 + nrm(ks[6], (N_S5, G, P), 0.01)
    s5_lam_im = jnp.pi * jnp.arange(P, dtype=f32) + nrm(ks[7], (N_S5, G, P), 0.01)
    s5_log_dt = jax.random.uniform(ks[8], (N_S5, G), f32, np.log(DT_MIN), np.log(DT_MAX))
    s5_b_re = nrm(ks[9], (N_S5, G, P, Cg), (2.0 * Cg) ** -0.5)
    s5_b_im = nrm(ks[10], (N_S5, G, P, Cg), (2.0 * Cg) ** -0.5)
    s5_c_re = nrm(ks[11], (N_S5, G, Cg, P), (2.0 * P) ** -0.5)
    s5_c_im = nrm(ks[12], (N_S5, G, Cg, P), (2.0 * P) ** -0.5)
    s5_d = nrm(ks[13], (N_S5, D), 1.0)
    s5_w_glu = nrm(ks[14], (N_S5, D, D), D ** -0.5)
    s5_w_out = nrm(ks[15], (N_S5, D, D), D ** -0.5)
    fox_w_in = nrm(ks[16], (N_FOX, D, 3 * D + H), D ** -0.5)
    fox_b_f = jax.random.uniform(ks[17], (N_FOX, H), f32, FGATE_B_LO, FGATE_B_HI)
    fox_w_out = nrm(ks[18], (N_FOX, D, D), D ** -0.5)
    ffn_w_up = nrm(ks[19], (DEPTH, D, 2 * F), D ** -0.5)
    ffn_conv_w = nrm(ks[20], (DEPTH, CONV_W, F), CONV_W ** -0.5)
    ffn_conv_b = nrm(ks[21], (DEPTH, F), 0.01)
    ffn_w_down = nrm(ks[22], (DEPTH, F, D), F ** -0.5)
    final_g = 1.0 + nrm(ks[23], (D,), 0.01)
    return {"x": x, "c": c, "norm_g": norm_g, "ada_w": ada_w, "ada_b": ada_b,
            "s5_w_in": s5_w_in, "s5_lam_re": s5_lam_re, "s5_lam_im": s5_lam_im,
            "s5_log_dt": s5_log_dt, "s5_b_re": s5_b_re, "s5_b_im": s5_b_im,
            "s5_c_re": s5_c_re, "s5_c_im": s5_c_im, "s5_d": s5_d,
            "s5_w_glu": s5_w_glu, "s5_w_out": s5_w_out,
            "fox_w_in": fox_w_in, "fox_b_f": fox_b_f, "fox_w_out": fox_w_out,
            "ffn_w_up": ffn_w_up, "ffn_conv_w": ffn_conv_w, "ffn_conv_b": ffn_conv_b,
            "ffn_w_down": ffn_w_down, "final_g": final_g}


def _fwd_reference(x, c, norm_g, ada_w, ada_b,
              s5_w_in, s5_lam_re, s5_lam_im, s5_log_dt, s5_b_re, s5_b_im,
              s5_c_re, s5_c_im, s5_d, s5_w_glu, s5_w_out,
              fox_w_in, fox_b_f, fox_w_out,
              ffn_w_up, ffn_conv_w, ffn_conv_b, ffn_w_down, final_g):
    h = x
    for i in range(DEPTH):
        j = i // N_MIXERS
        shift, scale, gate = adaln_params(c, ada_w[i, 0], ada_b[i, 0])
        hn = modulate(h, norm_g[i, 0], shift, scale)
        if i % N_MIXERS == 0:
            m = s5_mixer(hn, s5_w_in[j], s5_lam_re[j], s5_lam_im[j], s5_log_dt[j],
                         s5_b_re[j], s5_b_im[j], s5_c_re[j], s5_c_im[j], s5_d[j],
                         s5_w_glu[j], s5_w_out[j])
        else:
            m = fox_mixer(hn, fox_w_in[j], fox_b_f[j], fox_w_out[j])
        h = (h + gate * m).astype(x.dtype)
        shift, scale, gate = adaln_params(c, ada_w[i, 1], ada_b[i, 1])
        hn = modulate(h, norm_g[i, 1], shift, scale)
        f = conv_ffn(hn, ffn_w_up[i], ffn_conv_w[i], ffn_conv_b[i], ffn_w_down[i])
        h = (h + gate * f).astype(x.dtype)
    return rmsnorm(h, final_g)


import jax as _jax
import jax.numpy as _jnp

TWIN_FORMAT = 'train_step'
FWD_PARAMS = ['x', 'c', 'norm_g', 'ada_w', 'ada_b', 's5_w_in', 's5_lam_re', 's5_lam_im', 's5_log_dt', 's5_b_re', 's5_b_im', 's5_c_re', 's5_c_im', 's5_d', 's5_w_glu', 's5_w_out', 'fox_w_in', 'fox_b_f', 'fox_w_out', 'ffn_w_up', 'ffn_conv_w', 'ffn_conv_b', 'ffn_w_down', 'final_g']
TWIN_WEIGHTS = ['norm_g', 'ada_w', 'ada_b', 's5_w_in', 's5_lam_re', 's5_lam_im', 's5_log_dt', 's5_b_re', 's5_b_im', 's5_c_re', 's5_c_im', 's5_d', 's5_w_glu', 's5_w_out', 'fox_w_in', 'fox_b_f', 'fox_w_out', 'ffn_w_up', 'ffn_conv_w', 'ffn_conv_b', 'ffn_w_down', 'final_g']
TWIN_DIFF_INPUT = 'x'
TWIN_INPUTS = ['x', 'c', 'norm_g', 'ada_w', 'ada_b', 's5_w_in', 's5_lam_re', 's5_lam_im', 's5_log_dt', 's5_b_re', 's5_b_im', 's5_c_re', 's5_c_im', 's5_d', 's5_w_glu', 's5_w_out', 'fox_w_in', 'fox_b_f', 'fox_w_out', 'ffn_w_up', 'ffn_conv_w', 'ffn_conv_b', 'ffn_w_down', 'final_g', 'loss_target', 'm_norm_g', 'm_ada_w', 'm_ada_b', 'm_s5_w_in', 'm_s5_lam_re', 'm_s5_lam_im', 'm_s5_log_dt', 'm_s5_b_re', 'm_s5_b_im', 'm_s5_c_re', 'm_s5_c_im', 'm_s5_d', 'm_s5_w_glu', 'm_s5_w_out', 'm_fox_w_in', 'm_fox_b_f', 'm_fox_w_out', 'm_ffn_w_up', 'm_ffn_conv_w', 'm_ffn_conv_b', 'm_ffn_w_down', 'm_final_g', 'v_norm_g', 'v_ada_w', 'v_ada_b', 'v_s5_w_in', 'v_s5_lam_re', 'v_s5_lam_im', 'v_s5_log_dt', 'v_s5_b_re', 'v_s5_b_im', 'v_s5_c_re', 'v_s5_c_im', 'v_s5_d', 'v_s5_w_glu', 'v_s5_w_out', 'v_fox_w_in', 'v_fox_b_f', 'v_fox_w_out', 'v_ffn_w_up', 'v_ffn_conv_w', 'v_ffn_conv_b', 'v_ffn_w_down', 'v_final_g']
TWIN_OUTPUTS = ['loss', 'grad_x', 'grad_norm_g', 'grad_ada_w', 'grad_ada_b', 'grad_s5_w_in', 'grad_s5_lam_re', 'grad_s5_lam_im', 'grad_s5_log_dt', 'grad_s5_b_re', 'grad_s5_b_im', 'grad_s5_c_re', 'grad_s5_c_im', 'grad_s5_d', 'grad_s5_w_glu', 'grad_s5_w_out', 'grad_fox_w_in', 'grad_fox_b_f', 'grad_fox_w_out', 'grad_ffn_w_up', 'grad_ffn_conv_w', 'grad_ffn_conv_b', 'grad_ffn_w_down', 'grad_final_g', 'delta_norm_g', 'delta_ada_w', 'delta_ada_b', 'delta_s5_w_in', 'delta_s5_lam_re', 'delta_s5_lam_im', 'delta_s5_log_dt', 'delta_s5_b_re', 'delta_s5_b_im', 'delta_s5_c_re', 'delta_s5_c_im', 'delta_s5_d', 'delta_s5_w_glu', 'delta_s5_w_out', 'delta_fox_w_in', 'delta_fox_b_f', 'delta_fox_w_out', 'delta_ffn_w_up', 'delta_ffn_conv_w', 'delta_ffn_conv_b', 'delta_ffn_w_down', 'delta_final_g', 'new_m_norm_g', 'new_m_ada_w', 'new_m_ada_b', 'new_m_s5_w_in', 'new_m_s5_lam_re', 'new_m_s5_lam_im', 'new_m_s5_log_dt', 'new_m_s5_b_re', 'new_m_s5_b_im', 'new_m_s5_c_re', 'new_m_s5_c_im', 'new_m_s5_d', 'new_m_s5_w_glu', 'new_m_s5_w_out', 'new_m_fox_w_in', 'new_m_fox_b_f', 'new_m_fox_w_out', 'new_m_ffn_w_up', 'new_m_ffn_conv_w', 'new_m_ffn_conv_b', 'new_m_ffn_w_down', 'new_m_final_g', 'new_v_norm_g', 'new_v_ada_w', 'new_v_ada_b', 'new_v_s5_w_in', 'new_v_s5_lam_re', 'new_v_s5_lam_im', 'new_v_s5_log_dt', 'new_v_s5_b_re', 'new_v_s5_b_im', 'new_v_s5_c_re', 'new_v_s5_c_im', 'new_v_s5_d', 'new_v_s5_w_glu', 'new_v_s5_w_out', 'new_v_fox_w_in', 'new_v_fox_b_f', 'new_v_fox_w_out', 'new_v_ffn_w_up', 'new_v_ffn_conv_w', 'new_v_ffn_conv_b', 'new_v_ffn_w_down', 'new_v_final_g']
TWIN_LEAF_KINDS = {'loss': 'loss', 'grad_x': 'grad_x', 'grad_norm_g': 'grad_w', 'grad_ada_w': 'grad_w', 'grad_ada_b': 'grad_w', 'grad_s5_w_in': 'grad_w', 'grad_s5_lam_re': 'grad_w', 'grad_s5_lam_im': 'grad_w', 'grad_s5_log_dt': 'grad_w', 'grad_s5_b_re': 'grad_w', 'grad_s5_b_im': 'grad_w', 'grad_s5_c_re': 'grad_w', 'grad_s5_c_im': 'grad_w', 'grad_s5_d': 'grad_w', 'grad_s5_w_glu': 'grad_w', 'grad_s5_w_out': 'grad_w', 'grad_fox_w_in': 'grad_w', 'grad_fox_b_f': 'grad_w', 'grad_fox_w_out': 'grad_w', 'grad_ffn_w_up': 'grad_w', 'grad_ffn_conv_w': 'grad_w', 'grad_ffn_conv_b': 'grad_w', 'grad_ffn_w_down': 'grad_w', 'grad_final_g': 'grad_w', 'delta_norm_g': 'delta_w', 'delta_ada_w': 'delta_w', 'delta_ada_b': 'delta_w', 'delta_s5_w_in': 'delta_w', 'delta_s5_lam_re': 'delta_w', 'delta_s5_lam_im': 'delta_w', 'delta_s5_log_dt': 'delta_w', 'delta_s5_b_re': 'delta_w', 'delta_s5_b_im': 'delta_w', 'delta_s5_c_re': 'delta_w', 'delta_s5_c_im': 'delta_w', 'delta_s5_d': 'delta_w', 'delta_s5_w_glu': 'delta_w', 'delta_s5_w_out': 'delta_w', 'delta_fox_w_in': 'delta_w', 'delta_fox_b_f': 'delta_w', 'delta_fox_w_out': 'delta_w', 'delta_ffn_w_up': 'delta_w', 'delta_ffn_conv_w': 'delta_w', 'delta_ffn_conv_b': 'delta_w', 'delta_ffn_w_down': 'delta_w', 'delta_final_g': 'delta_w', 'new_m_norm_g': 'new_m', 'new_m_ada_w': 'new_m', 'new_m_ada_b': 'new_m', 'new_m_s5_w_in': 'new_m', 'new_m_s5_lam_re': 'new_m', 'new_m_s5_lam_im': 'new_m', 'new_m_s5_log_dt': 'new_m', 'new_m_s5_b_re': 'new_m', 'new_m_s5_b_im': 'new_m', 'new_m_s5_c_re': 'new_m', 'new_m_s5_c_im': 'new_m', 'new_m_s5_d': 'new_m', 'new_m_s5_w_glu': 'new_m', 'new_m_s5_w_out': 'new_m', 'new_m_fox_w_in': 'new_m', 'new_m_fox_b_f': 'new_m', 'new_m_fox_w_out': 'new_m', 'new_m_ffn_w_up': 'new_m', 'new_m_ffn_conv_w': 'new_m', 'new_m_ffn_conv_b': 'new_m', 'new_m_ffn_w_down': 'new_m', 'new_m_final_g': 'new_m', 'new_v_norm_g': 'new_v', 'new_v_ada_w': 'new_v', 'new_v_ada_b': 'new_v', 'new_v_s5_w_in': 'new_v', 'new_v_s5_lam_re': 'new_v', 'new_v_s5_lam_im': 'new_v', 'new_v_s5_log_dt': 'new_v', 'new_v_s5_b_re': 'new_v', 'new_v_s5_b_im': 'new_v', 'new_v_s5_c_re': 'new_v', 'new_v_s5_c_im': 'new_v', 'new_v_s5_d': 'new_v', 'new_v_s5_w_glu': 'new_v', 'new_v_s5_w_out': 'new_v', 'new_v_fox_w_in': 'new_v', 'new_v_fox_b_f': 'new_v', 'new_v_fox_w_out': 'new_v', 'new_v_ffn_w_up': 'new_v', 'new_v_ffn_conv_w': 'new_v', 'new_v_ffn_conv_b': 'new_v', 'new_v_ffn_w_down': 'new_v', 'new_v_final_g': 'new_v'}


def _forward(args):
    return _fwd_reference(*[args[k] for k in FWD_PARAMS])


def _output_shape():
    def fwd():
        inp = _fwd_setup_inputs(0)
        return _fwd_reference(*[inp[k] for k in FWD_PARAMS])
    out = _jax.eval_shape(fwd)
    return out.shape, out.dtype

N_MICROBATCH = 1
ADAM_LR = 0.001
ADAM_B1 = 0.9
ADAM_B2 = 0.999
ADAM_EPS = 1e-08
ADAM_WD = 0.01
ADAM_STEP = 10
PER_EXAMPLE_BATCH_AXIS = {'x': 0, 'c': 0, 'loss_target': 0}
SHARED_INPUTS = []
_WEIGHT_DTYPES = {'norm_g': _jnp.float32, 'ada_w': _jnp.float32, 'ada_b': _jnp.float32, 's5_w_in': _jnp.float32, 's5_lam_re': _jnp.float32, 's5_lam_im': _jnp.float32, 's5_log_dt': _jnp.float32, 's5_b_re': _jnp.float32, 's5_b_im': _jnp.float32, 's5_c_re': _jnp.float32, 's5_c_im': _jnp.float32, 's5_d': _jnp.float32, 's5_w_glu': _jnp.float32, 's5_w_out': _jnp.float32, 'fox_w_in': _jnp.float32, 'fox_b_f': _jnp.float32, 'fox_w_out': _jnp.float32, 'ffn_w_up': _jnp.float32, 'ffn_conv_w': _jnp.float32, 'ffn_conv_b': _jnp.float32, 'ffn_w_down': _jnp.float32, 'final_g': _jnp.float32}
MOMENT_SCALE = {'norm_g': 5.791076e-02, 'ada_w': 5.977746e-02, 'ada_b': 9.904269e-02, 's5_w_in': 3.388042e-02, 's5_lam_re': 2.108980e-03, 's5_lam_im': 2.444553e-03, 's5_log_dt': 1.297718e+00, 's5_b_re': 1.459717e-03, 's5_b_im': 1.337457e-03, 's5_c_re': 2.664146e-03, 's5_c_im': 2.745323e-03, 's5_d': 3.599403e-02, 's5_w_glu': 1.038588e-02, 's5_w_out': 3.228579e-02, 'fox_w_in': 2.568578e-02, 'fox_b_f': 1.042988e-01, 'fox_w_out': 3.310536e-02, 'ffn_w_up': 3.333749e-02, 'ffn_conv_w': 3.364955e-02, 'ffn_conv_b': 3.027746e-02, 'ffn_w_down': 5.442539e-02, 'final_g': 6.399265e+01}


def _to_microbatches(a, axis):
    t = _jnp.moveaxis(a, axis, 0)
    t = t.reshape((N_MICROBATCH, t.shape[0] // N_MICROBATCH) + t.shape[1:])
    return _jnp.moveaxis(t, 1, axis + 1)


def setup_inputs(seed: int = 0) -> dict:
    inp = _fwd_setup_inputs(seed)
    key = _jax.random.fold_in(_jax.random.key(seed), 7919)
    shape, _ = _output_shape()
    out = dict(inp)
    out["loss_target"] = _jax.random.normal(_jax.random.fold_in(key, 0), shape, _jnp.float32)
    for i, name in enumerate(TWIN_WEIGHTS):
        w = inp[name].astype(_jnp.float32)
        if MOMENT_SCALE is None:
            s = _jnp.sqrt(_jnp.mean(_jnp.square(w)) + 1e-30)
        else:
            s = MOMENT_SCALE[name]
        km, kv = _jax.random.split(_jax.random.fold_in(key, i + 1))
        out[name] = w
        out["m_" + name] = s * _jax.random.normal(km, w.shape, _jnp.float32)
        out["v_" + name] = (s * s) * _jax.random.uniform(kv, w.shape, _jnp.float32, 0.5, 1.5)
    if N_MICROBATCH > 1:
        for name, axis in PER_EXAMPLE_BATCH_AXIS.items():
            out[name] = _to_microbatches(out[name], axis)
    return {'x': out['x'], 'c': out['c'], 'norm_g': out['norm_g'], 'ada_w': out['ada_w'], 'ada_b': out['ada_b'], 's5_w_in': out['s5_w_in'], 's5_lam_re': out['s5_lam_re'], 's5_lam_im': out['s5_lam_im'], 's5_log_dt': out['s5_log_dt'], 's5_b_re': out['s5_b_re'], 's5_b_im': out['s5_b_im'], 's5_c_re': out['s5_c_re'], 's5_c_im': out['s5_c_im'], 's5_d': out['s5_d'], 's5_w_glu': out['s5_w_glu'], 's5_w_out': out['s5_w_out'], 'fox_w_in': out['fox_w_in'], 'fox_b_f': out['fox_b_f'], 'fox_w_out': out['fox_w_out'], 'ffn_w_up': out['ffn_w_up'], 'ffn_conv_w': out['ffn_conv_w'], 'ffn_conv_b': out['ffn_conv_b'], 'ffn_w_down': out['ffn_w_down'], 'final_g': out['final_g'], 'loss_target': out['loss_target'], 'm_norm_g': out['m_norm_g'], 'm_ada_w': out['m_ada_w'], 'm_ada_b': out['m_ada_b'], 'm_s5_w_in': out['m_s5_w_in'], 'm_s5_lam_re': out['m_s5_lam_re'], 'm_s5_lam_im': out['m_s5_lam_im'], 'm_s5_log_dt': out['m_s5_log_dt'], 'm_s5_b_re': out['m_s5_b_re'], 'm_s5_b_im': out['m_s5_b_im'], 'm_s5_c_re': out['m_s5_c_re'], 'm_s5_c_im': out['m_s5_c_im'], 'm_s5_d': out['m_s5_d'], 'm_s5_w_glu': out['m_s5_w_glu'], 'm_s5_w_out': out['m_s5_w_out'], 'm_fox_w_in': out['m_fox_w_in'], 'm_fox_b_f': out['m_fox_b_f'], 'm_fox_w_out': out['m_fox_w_out'], 'm_ffn_w_up': out['m_ffn_w_up'], 'm_ffn_conv_w': out['m_ffn_conv_w'], 'm_ffn_conv_b': out['m_ffn_conv_b'], 'm_ffn_w_down': out['m_ffn_w_down'], 'm_final_g': out['m_final_g'], 'v_norm_g': out['v_norm_g'], 'v_ada_w': out['v_ada_w'], 'v_ada_b': out['v_ada_b'], 'v_s5_w_in': out['v_s5_w_in'], 'v_s5_lam_re': out['v_s5_lam_re'], 'v_s5_lam_im': out['v_s5_lam_im'], 'v_s5_log_dt': out['v_s5_log_dt'], 'v_s5_b_re': out['v_s5_b_re'], 'v_s5_b_im': out['v_s5_b_im'], 'v_s5_c_re': out['v_s5_c_re'], 'v_s5_c_im': out['v_s5_c_im'], 'v_s5_d': out['v_s5_d'], 'v_s5_w_glu': out['v_s5_w_glu'], 'v_s5_w_out': out['v_s5_w_out'], 'v_fox_w_in': out['v_fox_w_in'], 'v_fox_b_f': out['v_fox_b_f'], 'v_fox_w_out': out['v_fox_w_out'], 'v_ffn_w_up': out['v_ffn_w_up'], 'v_ffn_conv_w': out['v_ffn_conv_w'], 'v_ffn_conv_b': out['v_ffn_conv_b'], 'v_ffn_w_down': out['v_ffn_w_down'], 'v_final_g': out['v_final_g']}


def _loss(weights, diff, rest, loss_target):
    with _jax.named_scope("forward"):
        args = {**rest, TWIN_DIFF_INPUT: diff, **{k: w.astype(_WEIGHT_DTYPES[k]) for k, w in weights.items()}}
        y = _forward(args)
    with _jax.named_scope("loss_head"):
        err = _jnp.square(y.astype(_jnp.float32) - loss_target)
        return 0.5 * _jnp.sum(_jnp.mean(err, axis=-1)) if err.ndim else 0.5 * err


def _adamw(w, g, m, v):
    m = ADAM_B1 * m + (1.0 - ADAM_B1) * g
    v = ADAM_B2 * v + (1.0 - ADAM_B2) * _jnp.square(g)
    m_hat = m / (1.0 - ADAM_B1 ** ADAM_STEP)
    v_hat = v / (1.0 - ADAM_B2 ** ADAM_STEP)
    delta = -ADAM_LR * (m_hat / (_jnp.sqrt(v_hat) + ADAM_EPS) + ADAM_WD * w)
    return delta, m, v


def reference(x, c, norm_g, ada_w, ada_b, s5_w_in, s5_lam_re, s5_lam_im, s5_log_dt, s5_b_re, s5_b_im, s5_c_re, s5_c_im, s5_d, s5_w_glu, s5_w_out, fox_w_in, fox_b_f, fox_w_out, ffn_w_up, ffn_conv_w, ffn_conv_b, ffn_w_down, final_g, loss_target, m_norm_g, m_ada_w, m_ada_b, m_s5_w_in, m_s5_lam_re, m_s5_lam_im, m_s5_log_dt, m_s5_b_re, m_s5_b_im, m_s5_c_re, m_s5_c_im, m_s5_d, m_s5_w_glu, m_s5_w_out, m_fox_w_in, m_fox_b_f, m_fox_w_out, m_ffn_w_up, m_ffn_conv_w, m_ffn_conv_b, m_ffn_w_down, m_final_g, v_norm_g, v_ada_w, v_ada_b, v_s5_w_in, v_s5_lam_re, v_s5_lam_im, v_s5_log_dt, v_s5_b_re, v_s5_b_im, v_s5_c_re, v_s5_c_im, v_s5_d, v_s5_w_glu, v_s5_w_out, v_fox_w_in, v_fox_b_f, v_fox_w_out, v_ffn_w_up, v_ffn_conv_w, v_ffn_conv_b, v_ffn_w_down, v_final_g):
    given = dict(x=x, c=c, norm_g=norm_g, ada_w=ada_w, ada_b=ada_b, s5_w_in=s5_w_in, s5_lam_re=s5_lam_re, s5_lam_im=s5_lam_im, s5_log_dt=s5_log_dt, s5_b_re=s5_b_re, s5_b_im=s5_b_im, s5_c_re=s5_c_re, s5_c_im=s5_c_im, s5_d=s5_d, s5_w_glu=s5_w_glu, s5_w_out=s5_w_out, fox_w_in=fox_w_in, fox_b_f=fox_b_f, fox_w_out=fox_w_out, ffn_w_up=ffn_w_up, ffn_conv_w=ffn_conv_w, ffn_conv_b=ffn_conv_b, ffn_w_down=ffn_w_down, final_g=final_g, loss_target=loss_target, m_norm_g=m_norm_g, m_ada_w=m_ada_w, m_ada_b=m_ada_b, m_s5_w_in=m_s5_w_in, m_s5_lam_re=m_s5_lam_re, m_s5_lam_im=m_s5_lam_im, m_s5_log_dt=m_s5_log_dt, m_s5_b_re=m_s5_b_re, m_s5_b_im=m_s5_b_im, m_s5_c_re=m_s5_c_re, m_s5_c_im=m_s5_c_im, m_s5_d=m_s5_d, m_s5_w_glu=m_s5_w_glu, m_s5_w_out=m_s5_w_out, m_fox_w_in=m_fox_w_in, m_fox_b_f=m_fox_b_f, m_fox_w_out=m_fox_w_out, m_ffn_w_up=m_ffn_w_up, m_ffn_conv_w=m_ffn_conv_w, m_ffn_conv_b=m_ffn_conv_b, m_ffn_w_down=m_ffn_w_down, m_final_g=m_final_g, v_norm_g=v_norm_g, v_ada_w=v_ada_w, v_ada_b=v_ada_b, v_s5_w_in=v_s5_w_in, v_s5_lam_re=v_s5_lam_re, v_s5_lam_im=v_s5_lam_im, v_s5_log_dt=v_s5_log_dt, v_s5_b_re=v_s5_b_re, v_s5_b_im=v_s5_b_im, v_s5_c_re=v_s5_c_re, v_s5_c_im=v_s5_c_im, v_s5_d=v_s5_d, v_s5_w_glu=v_s5_w_glu, v_s5_w_out=v_s5_w_out, v_fox_w_in=v_fox_w_in, v_fox_b_f=v_fox_b_f, v_fox_w_out=v_fox_w_out, v_ffn_w_up=v_ffn_w_up, v_ffn_conv_w=v_ffn_conv_w, v_ffn_conv_b=v_ffn_conv_b, v_ffn_w_down=v_ffn_w_down, v_final_g=v_final_g)
    weights = {n: given[n] for n in TWIN_WEIGHTS}
    shared = {n: given[n] for n in SHARED_INPUTS}
    per_example = {n: given[n] for n in ['x', 'c']}
    grad_fn = _jax.value_and_grad(_loss, argnums=(0, 1))

    def one_microbatch(ex, loss_target):
        ex = dict(ex)
        diff = ex.pop(TWIN_DIFF_INPUT)
        return grad_fn(weights, diff, {**shared, **ex}, loss_target)

    if N_MICROBATCH == 1:
        loss, (grad_w, grad_x) = one_microbatch(per_example, given["loss_target"])
    else:
        def body(carry, xs):
            loss_sum, grad_sum = carry
            l_k, (gw_k, gx_k) = one_microbatch(xs[0], xs[1])
            with _jax.named_scope("update"):
                return (loss_sum + l_k, _jax.tree.map(_jnp.add, grad_sum, gw_k)), gx_k

        init = (_jnp.zeros((), _jnp.float32), _jax.tree.map(_jnp.zeros_like, weights))
        (loss, grad_w), grad_x = _jax.lax.scan(body, init, (per_example, given["loss_target"]))
    with _jax.named_scope("update"):
        delta_w, new_m, new_v = {}, {}, {}
        for n in TWIN_WEIGHTS:
            delta_w[n], new_m[n], new_v[n] = _adamw(weights[n], grad_w[n], given["m_" + n], given["v_" + n])
    return (loss, grad_x, *[grad_w[n] for n in TWIN_WEIGHTS], *[delta_w[n] for n in TWIN_WEIGHTS],
            *[new_m[n] for n in TWIN_WEIGHTS], *[new_v[n] for n in TWIN_WEIGHTS])
```

```python
import functools
import math

import jax
import jax.numpy as jnp
from jax import lax
from jax.experimental import pallas as pl
from jax.experimental.pallas import tpu as pltpu

F32, BF16 = jnp.float32, jnp.bfloat16
EPS = 1e-6
N_DEV = 8
D_MODEL = 1024
D_FF = 2816
HEADS = 16
HEAD_DIM = 64
S5_G, S5_P, S5_C = 64, 64, 16
S5_SUB = 8
V7X_VMEM_LIMIT = 56 * 1024 * 1024
NEG = -1e30
ADAM_LR, ADAM_B1, ADAM_B2, ADAM_EPS, ADAM_WD, ADAM_STEP = 1e-3, 0.9, 0.999, 1e-8, 0.01, 10
GELU_K = math.sqrt(2.0 / math.pi)
MESH_ID = pl.DeviceIdType.MESH


def _cp(*sem):
    return pltpu.CompilerParams(dimension_semantics=sem, vmem_limit_bytes=V7X_VMEM_LIMIT)


def _tile(n, target, mult=128):
    if n <= target:
        return n
    t = (target // mult) * mult
    while t >= mult:
        if n % t == 0:
            return t
        t -= mult
    return n


def _row(v):
    return v.reshape(1, -1).astype(F32)


def _mm(a, b, *, name, ta=False, tb=False, out_dtype=F32, tm=1024, tn=512, tk=None, res=None, gate=None,
        a_h=False, b_h=False, o_h=False):
    if a_h:
        M, K = a.shape[1], 2 * a.shape[2]
    elif ta:
        K, M = a.shape
    else:
        M, K = a.shape
    if b_h:
        N = 2 * b.shape[2]
    else:
        N = b.shape[0] if tb else b.shape[1]
    half_n = N // 2
    tm = _tile(M, tm, 128 if ta else 8)
    tn = _tile(half_n if (b_h or o_h) else N, tn)
    tk = K if tk is None else _tile(K // 2 if a_h else K, tk)
    nk = K // tk
    nkh, nnh = (K // 2) // tk if a_h else 1, half_n // tn
    if a_h:
        a_spec = pl.BlockSpec((None, tm, tk), lambda i, j, k: (k // nkh, i, k % nkh))
    elif ta:
        a_spec = pl.BlockSpec((tk, tm), lambda i, j, k: (k, i))
    else:
        a_spec = pl.BlockSpec((tm, tk), lambda i, j, k: (i, k))
    if b_h:
        b_spec = pl.BlockSpec((None, tk, tn), lambda i, j, k: (j // nnh, k, j % nnh))
    elif tb:
        b_spec = pl.BlockSpec((tn, tk), lambda i, j, k: (j, k))
    else:
        b_spec = pl.BlockSpec((tk, tn), lambda i, j, k: (k, j))
    if o_h:
        o_spec = pl.BlockSpec((None, tm, tn), lambda i, j, k: (j // nnh, i, j % nnh))
    else:
        o_spec = pl.BlockSpec((tm, tn), lambda i, j, k: (i, j))
    dn = (((0 if ta else 1,), (1 if tb else 0,)), ((), ()))
    fused = res is not None

    def body(*refs):
        if fused:
            a_ref, b_ref, r_ref, g_ref, m_ref, o_ref, acc_ref = refs
        else:
            a_ref, b_ref, o_ref, acc_ref = refs
        p = lax.dot_general(a_ref[...].astype(BF16), b_ref[...].astype(BF16), dn, preferred_element_type=F32)

        def finish(acc):
            if fused:
                m_ref[...] = acc.astype(m_ref.dtype)
                o_ref[...] = r_ref[...] + g_ref[...] * acc
            else:
                o_ref[...] = acc.astype(o_ref.dtype)

        if nk == 1:
            finish(p)
        else:
            k = pl.program_id(2)

            @pl.when(k == 0)
            def _():
                acc_ref[...] = p

            @pl.when(k > 0)
            def _():
                acc_ref[...] += p

            @pl.when(k == nk - 1)
            def _():
                finish(acc_ref[...])

    in_specs = [a_spec, b_spec]
    args = [a, b]
    if fused:
        in_specs += [o_spec, pl.BlockSpec((1, tn), lambda i, j, k: (0, j))]
        args += [res, gate]
        out_shape = [jax.ShapeDtypeStruct((M, N), BF16), jax.ShapeDtypeStruct((M, N), F32)]
        out_specs = [o_spec, o_spec]
    else:
        out_shape = jax.ShapeDtypeStruct((2, M, half_n) if o_h else (M, N), out_dtype)
        out_specs = o_spec
    return pl.pallas_call(
        body, name=name, grid=(M // tm, N // tn, nk), in_specs=in_specs, out_specs=out_specs, out_shape=out_shape,
        scratch_shapes=[pltpu.VMEM((tm, tn) if nk > 1 else (8, 128), F32)],
        compiler_params=_cp("parallel", "parallel", "arbitrary"),
    )(*args)


def _modulate(h, g, shift, scale, *, name):
    S, D = h.shape
    tm = _tile(S, 512, 8)
    vec = pl.BlockSpec((1, D), lambda i: (0, 0))
    blk = pl.BlockSpec((tm, D), lambda i: (i, 0))

    def body(h_ref, g_ref, sh_ref, sc_ref, o_ref):
        x = h_ref[...]
        r = lax.rsqrt(jnp.mean(x * x, axis=-1, keepdims=True) + EPS)
        o_ref[...] = ((x * r * g_ref[...]) * (1.0 + sc_ref[...]) + sh_ref[...]).astype(o_ref.dtype)

    return pl.pallas_call(body, name=name, grid=(S // tm,), in_specs=[blk, vec, vec, vec], out_specs=blk,
                          out_shape=jax.ShapeDtypeStruct((S, D), BF16), compiler_params=_cp("parallel"))(h, g, shift, scale)


def _gate_bwd(dh, m, gate, *, name):
    S, D = dh.shape
    tm = _tile(S, 512, 8)
    vec = pl.BlockSpec((1, D), lambda i: (0, 0))
    blk = pl.BlockSpec((tm, D), lambda i: (i, 0))

    def body(dh_ref, m_ref, g_ref, dm_ref, dg_ref):
        @pl.when(pl.program_id(0) == 0)
        def _():
            dg_ref[...] = jnp.zeros_like(dg_ref)

        d = dh_ref[...]
        dm_ref[...] = (d * g_ref[...]).astype(dm_ref.dtype)
        dg_ref[...] += jnp.sum(d * m_ref[...].astype(F32), axis=0, keepdims=True)

    return pl.pallas_call(body, name=name, grid=(S // tm,), in_specs=[blk, blk, vec], out_specs=[blk, vec],
                          out_shape=[jax.ShapeDtypeStruct((S, D), BF16), jax.ShapeDtypeStruct((1, D), F32)],
                          compiler_params=_cp("arbitrary"))(dh, m, gate)


def _modulate_bwd(h, dhn, dh_in, g, scale, *, name):
    S, D = h.shape
    tm = _tile(S, 512, 8)
    vec = pl.BlockSpec((1, D), lambda i: (0, 0))
    blk = pl.BlockSpec((tm, D), lambda i: (i, 0))
    sums = pl.BlockSpec((8, D), lambda i: (0, 0))

    def body(h_ref, dhn_ref, dhi_ref, g_ref, sc_ref, dh_ref, s_ref):
        @pl.when(pl.program_id(0) == 0)
        def _():
            s_ref[...] = jnp.zeros_like(s_ref)

        x = h_ref[...]
        r = lax.rsqrt(jnp.mean(x * x, axis=-1, keepdims=True) + EPS)
        xhat = x * r
        gv = g_ref[...]
        dhn_v = dhn_ref[...].astype(F32)
        dn = dhn_v * (1.0 + sc_ref[...])
        s_ref[0:1, :] += jnp.sum(dhn_v, axis=0, keepdims=True)
        s_ref[1:2, :] += jnp.sum(dhn_v * (xhat * gv), axis=0, keepdims=True)
        s_ref[2:3, :] += jnp.sum(dn * xhat, axis=0, keepdims=True)
        dxh = dn * gv
        dh_ref[...] = dhi_ref[...] + r * (dxh - xhat * jnp.mean(dxh * xhat, axis=-1, keepdims=True))

    return pl.pallas_call(body, name=name, grid=(S // tm,), in_specs=[blk, blk, blk, vec, vec], out_specs=[blk, sums],
                          out_shape=[jax.ShapeDtypeStruct((S, D), F32), jax.ShapeDtypeStruct((8, D), F32)],
                          compiler_params=_cp("arbitrary"))(h, dhn, dh_in, g, scale)


def _loss_head(h, g, tgt, *, name):
    S, D = h.shape
    tm = _tile(S, 512, 8)
    vec = pl.BlockSpec((1, D), lambda i: (0, 0))
    blk = pl.BlockSpec((tm, D), lambda i: (i, 0))
    lss = pl.BlockSpec((8, 128), lambda i: (0, 0))

    def body(h_ref, g_ref, t_ref, l_ref, dh_ref, dg_ref):
        @pl.when(pl.program_id(0) == 0)
        def _():
            l_ref[...] = jnp.zeros_like(l_ref)
            dg_ref[...] = jnp.zeros_like(dg_ref)

        x = h_ref[...]
        r = lax.rsqrt(jnp.mean(x * x, axis=-1, keepdims=True) + EPS)
        xhat = x * r
        gv = g_ref[...]
        e = xhat * gv - t_ref[...]
        l_ref[...] += 0.5 * jnp.sum(jnp.mean(e * e, axis=-1, keepdims=True))
        dy = e * (1.0 / D)
        dg_ref[...] += jnp.sum(dy * xhat, axis=0, keepdims=True)
        dxh = dy * gv
        dh_ref[...] = r * (dxh - xhat * jnp.mean(dxh * xhat, axis=-1, keepdims=True))

    return pl.pallas_call(body, name=name, grid=(S // tm,), in_specs=[blk, vec, blk], out_specs=[lss, blk, vec],
                          out_shape=[jax.ShapeDtypeStruct((8, 128), F32), jax.ShapeDtypeStruct((S, D), F32),
                                     jax.ShapeDtypeStruct((1, D), F32)],
                          compiler_params=_cp("arbitrary"))(h, g, tgt)


def _shift_down(x, k, edge):
    tm = x.shape[0]
    rows = lax.broadcasted_iota(jnp.int32, x.shape, 0)
    out = pltpu.roll(x, k, 0)
    for j in range(k):
        out = jnp.where(rows == j, edge[8 - k + j:8 - k + j + 1, :], out)
    return out


def _shift_up(x, k, edge):
    tm = x.shape[0]
    rows = lax.broadcasted_iota(jnp.int32, x.shape, 0)
    out = pltpu.roll(x, tm - k, 0)
    for j in range(k):
        out = jnp.where(rows == tm - k + j, edge[j:j + 1, :], out)
    return out


def _conv_gate_fwd(up, cw, cb, *, name):
    _, S, F = up.shape
    tf = _tile(F, 1408)
    nf = F // tf
    tm = _tile(S, 512, 8)

    def body(a_ref, b_ref, w_ref, cb_ref, z_ref, edge_ref):
        @pl.when(pl.program_id(1) == 0)
        def _():
            edge_ref[...] = jnp.zeros_like(edge_ref)

        a = a_ref[...].astype(F32)
        edge = edge_ref[...]
        w = w_ref[...]
        ac = cb_ref[...] + w[2:3, :] * a + w[1:2, :] * _shift_down(a, 1, edge) + w[0:1, :] * _shift_down(a, 2, edge)
        edge_ref[...] = a[tm - 8:tm, :]
        z_ref[...] = (ac * jax.nn.sigmoid(ac) * b_ref[...].astype(F32)).astype(z_ref.dtype)

    return pl.pallas_call(
        body, name=name, grid=(nf, S // tm),
        in_specs=[pl.BlockSpec((None, tm, tf), lambda j, i: (0, i, j)), pl.BlockSpec((None, tm, tf), lambda j, i: (1, i, j)),
                  pl.BlockSpec((3, tf), lambda j, i: (0, j)), pl.BlockSpec((1, tf), lambda j, i: (0, j))],
        out_specs=pl.BlockSpec((tm, tf), lambda j, i: (i, j)),
        out_shape=jax.ShapeDtypeStruct((S, F), BF16), scratch_shapes=[pltpu.VMEM((8, tf), F32)],
        compiler_params=_cp("parallel", "arbitrary"))(up, up, cw, cb)


def _conv_gate_bwd(up, dz, cw, cb, *, name):
    _, S, F = up.shape
    tf = _tile(F, 1408)
    nf = F // tf
    tm = _tile(S, 512, 8)
    nt = S // tm
    hb = tm // 8

    def body(a_ref, ah_ref, b_ref, dz_ref, w_ref, cb_ref, d_ref, s_ref, edge_ref):
        i = pl.program_id(1)

        @pl.when(i == 0)
        def _():
            edge_ref[...] = jnp.zeros_like(edge_ref)
            s_ref[...] = jnp.zeros_like(s_ref)

        a = a_ref[...].astype(F32)
        halo = jnp.where(i == nt - 1, 0.0, ah_ref[...].astype(F32))
        w = w_ref[...]
        a1 = _shift_down(a, 1, halo)
        a2 = _shift_down(a, 2, halo)
        ac = cb_ref[...] + w[2:3, :] * a + w[1:2, :] * a1 + w[0:1, :] * a2
        sg = jax.nn.sigmoid(ac)
        dzv = dz_ref[...].astype(F32)
        d_ref[1] = (dzv * ac * sg).astype(d_ref.dtype)
        dac = dzv * b_ref[...].astype(F32) * (sg * (1.0 + ac * (1.0 - sg)))
        s_ref[0:1, :] += jnp.sum(dac * a2, axis=0, keepdims=True)
        s_ref[1:2, :] += jnp.sum(dac * a1, axis=0, keepdims=True)
        s_ref[2:3, :] += jnp.sum(dac * a, axis=0, keepdims=True)
        s_ref[3:4, :] += jnp.sum(dac, axis=0, keepdims=True)
        edge = edge_ref[...]
        da = w[2:3, :] * dac + w[1:2, :] * _shift_up(dac, 1, edge) + w[0:1, :] * _shift_up(dac, 2, edge)
        edge_ref[...] = dac[0:8, :]
        d_ref[0] = da.astype(d_ref.dtype)

    tile = lambda hlf: pl.BlockSpec((None, tm, tf), lambda j, i: (hlf, nt - 1 - i, j))
    d_up, sums = pl.pallas_call(
        body, name=name, grid=(nf, nt),
        in_specs=[tile(0),
                  pl.BlockSpec((None, 8, tf), lambda j, i: (0, jnp.maximum((nt - 1 - i) * hb - 1, 0), j)),
                  tile(1), pl.BlockSpec((tm, tf), lambda j, i: (nt - 1 - i, j)),
                  pl.BlockSpec((3, tf), lambda j, i: (0, j)), pl.BlockSpec((1, tf), lambda j, i: (0, j))],
        out_specs=[pl.BlockSpec((2, tm, tf), lambda j, i: (0, nt - 1 - i, j)), pl.BlockSpec((8, tf), lambda j, i: (0, j))],
        out_shape=[jax.ShapeDtypeStruct((2, S, F), BF16), jax.ShapeDtypeStruct((8, F), F32)],
        scratch_shapes=[pltpu.VMEM((8, tf), F32)],
        compiler_params=_cp("parallel", "arbitrary"))(up, up, up, dz, cw, cb)
    return d_up, sums


def _log_sigmoid(x):
    return jnp.minimum(x, 0.0) - jnp.log(1.0 + jnp.exp(-jnp.abs(x)))


def _tri_ones(n, upper):
    r = lax.broadcasted_iota(jnp.int32, (n, n), 0)
    c = lax.broadcasted_iota(jnp.int32, (n, n), 1)
    return jnp.where((c >= r) if upper else (c <= r), 1.0, 0.0).astype(F32)


def _fgate_fwd(fl, bf, *, name):
    S, W = fl.shape
    tb = _tile(S, 256, 8)

    def body(fl_ref, b_ref, o_ref, carry_ref):
        @pl.when(pl.program_id(0) == 0)
        def _():
            carry_ref[...] = jnp.zeros_like(carry_ref)

        lf = _log_sigmoid(fl_ref[...] + b_ref[...])
        cs = jnp.dot(_tri_ones(tb, False), lf, precision=lax.Precision.HIGHEST, preferred_element_type=F32) + carry_ref[0:1, :]
        o_ref[...] = cs
        carry_ref[...] = jnp.broadcast_to(cs[tb - 1:tb, :], carry_ref.shape)

    blk = pl.BlockSpec((tb, W), lambda i: (i, 0))
    return pl.pallas_call(body, name=name, grid=(S // tb,), in_specs=[blk, pl.BlockSpec((1, W), lambda i: (0, 0))], out_specs=blk,
                          out_shape=jax.ShapeDtypeStruct((S, W), F32), scratch_shapes=[pltpu.VMEM((8, W), F32)],
                          compiler_params=_cp("arbitrary"))(fl, bf)


def _fgate_bwd(dF, fl, bf, *, name):
    S, W = fl.shape
    tb = _tile(S, 256, 8)
    nb = S // tb

    def body(d_ref, fl_ref, b_ref, o_ref, s_ref, carry_ref):
        @pl.when(pl.program_id(0) == 0)
        def _():
            carry_ref[...] = jnp.zeros_like(carry_ref)
            s_ref[...] = jnp.zeros_like(s_ref)

        rc = jnp.dot(_tri_ones(tb, True), d_ref[...], precision=lax.Precision.HIGHEST, preferred_element_type=F32) + carry_ref[0:1, :]
        carry_ref[...] = jnp.broadcast_to(rc[0:1, :], carry_ref.shape)
        dfl = rc * jax.nn.sigmoid(-(fl_ref[...] + b_ref[...]))
        o_ref[...] = dfl
        s_ref[...] += jnp.sum(dfl, axis=0, keepdims=True)

    blk = pl.BlockSpec((tb, W), lambda i: (nb - 1 - i, 0))
    vec = pl.BlockSpec((1, W), lambda i: (0, 0))
    return pl.pallas_call(body, name=name, grid=(nb,), in_specs=[blk, blk, vec], out_specs=[blk, vec],
                          out_shape=[jax.ShapeDtypeStruct((S, W), F32), jax.ShapeDtypeStruct((1, W), F32)],
                          scratch_shapes=[pltpu.VMEM((8, W), F32)], compiler_params=_cp("arbitrary"))(dF, fl, bf)


NT_DIMS = (((1,), (1,)), ((), ()))
TN_DIMS = (((0,), (0,)), ((), ()))
HP = HEADS // 2
Q_SCALE = HEAD_DIM ** -0.5


def _lane_pick(x, idx):
    lanes = lax.broadcasted_iota(jnp.int32, x.shape, 1)
    return jnp.sum(jnp.where(lanes == idx, x, 0.0), axis=1, keepdims=True)


def _causal_logits(q, k, fq, fk, row0, col0):
    s = lax.dot_general(q, k, NT_DIMS, preferred_element_type=F32) + fq - fk
    rows = row0 + lax.broadcasted_iota(jnp.int32, s.shape, 0)
    cols = col0 + lax.broadcasted_iota(jnp.int32, s.shape, 1)
    return jnp.where(cols <= rows, s, NEG)


def _attn_fwd(qkv, fcol, frow, *, tq, tk, name):
    S = qkv.shape[0]
    nq, nk = S // tq, S // tk
    last = lambda qi: (qi * tq + tq - 1) // tk

    def body(q_ref, k_ref, v_ref, fc_ref, fr_ref, o_ref, lse_ref, m_sc, l_sc, acc_sc):
        qi, hp, kj = pl.program_id(0), pl.program_id(1), pl.program_id(2)

        @pl.when(kj == 0)
        def _():
            m_sc[...] = jnp.full_like(m_sc, NEG)
            l_sc[...] = jnp.zeros_like(l_sc)
            acc_sc[...] = jnp.zeros_like(acc_sc)

            @pl.when(hp == 0)
            def _():
                lse_ref[...] = jnp.zeros_like(lse_ref)

        @pl.when(kj <= last(qi))
        def _():
            for hh in range(2):
                sl = slice(hh * HEAD_DIM, (hh + 1) * HEAD_DIM)
                q = q_ref[:, sl] * Q_SCALE
                fq = _lane_pick(fc_ref[...], 2 * hp + hh)
                s = _causal_logits(q, k_ref[:, sl], fq, fr_ref[hh:hh + 1, :], qi * tq, kj * tk)
                m_old = m_sc[hh]
                m_new = jnp.maximum(m_old, jnp.max(s, axis=1, keepdims=True))
                alpha = jnp.exp(m_old - m_new)
                p = jnp.exp(s - m_new)
                l_sc[hh] = alpha * l_sc[hh] + jnp.sum(p, axis=1, keepdims=True)
                acc_sc[hh] = alpha * acc_sc[hh] + jnp.dot(p.astype(BF16), v_ref[:, sl], preferred_element_type=F32)
                m_sc[hh] = m_new

        @pl.when(kj == nk - 1)
        def _():
            lanes = lax.broadcasted_iota(jnp.int32, lse_ref.shape, 1)
            cur = lse_ref[...]
            for hh in range(2):
                l = l_sc[hh]
                o_ref[:, hh * HEAD_DIM:(hh + 1) * HEAD_DIM] = (acc_sc[hh] / l).astype(o_ref.dtype)
                cur = jnp.where(lanes == 2 * hp + hh, m_sc[hh] + jnp.log(l), cur)
            lse_ref[...] = cur

    kmap = lambda off: (lambda qi, hp, kj: (jnp.minimum(kj, last(qi)), off + hp))
    return pl.pallas_call(
        body, name=name, grid=(nq, HP, nk),
        in_specs=[pl.BlockSpec((tq, 128), lambda qi, hp, kj: (qi, hp)),
                  pl.BlockSpec((tk, 128), kmap(HP)), pl.BlockSpec((tk, 128), kmap(2 * HP)),
                  pl.BlockSpec((tq, 128), lambda qi, hp, kj: (qi, 0)),
                  pl.BlockSpec((None, 8, tk), lambda qi, hp, kj: (hp, 0, jnp.minimum(kj, last(qi))))],
        out_specs=[pl.BlockSpec((tq, 128), lambda qi, hp, kj: (qi, hp)), pl.BlockSpec((tq, 128), lambda qi, hp, kj: (qi, 0))],
        out_shape=[jax.ShapeDtypeStruct((S, D_MODEL), BF16), jax.ShapeDtypeStruct((S, 128), F32)],
        scratch_shapes=[pltpu.VMEM((2, tq, 1), F32), pltpu.VMEM((2, tq, 1), F32), pltpu.VMEM((2, tq, HEAD_DIM), F32)],
        compiler_params=_cp("parallel", "arbitrary", "arbitrary"))(qkv, qkv, qkv, fcol, frow)


def _attn_bwd_delta(qkv, fcol, frow, do, lse, *, tq, tk, name):
    S = qkv.shape[0]
    nq, nk = S // tq, S // tk
    last = lambda qi: (qi * tq + tq - 1) // tk

    def body(q_ref, k_ref, v_ref, fc_ref, fr_ref, do_ref, lse_ref, dl_ref, acc_sc):
        qi, hp, kj = pl.program_id(0), pl.program_id(1), pl.program_id(2)

        @pl.when(kj == 0)
        def _():
            acc_sc[...] = jnp.zeros_like(acc_sc)

            @pl.when(hp == 0)
            def _():
                dl_ref[...] = jnp.zeros_like(dl_ref)

        @pl.when(kj <= last(qi))
        def _():
            for hh in range(2):
                sl = slice(hh * HEAD_DIM, (hh + 1) * HEAD_DIM)
                q = q_ref[:, sl] * Q_SCALE
                fq = _lane_pick(fc_ref[...], 2 * hp + hh)
                s = _causal_logits(q, k_ref[:, sl], fq, fr_ref[hh:hh + 1, :], qi * tq, kj * tk)
                p = jnp.exp(s - _lane_pick(lse_ref[...], 2 * hp + hh))
                dp = lax.dot_general(do_ref[:, sl], v_ref[:, sl], NT_DIMS, preferred_element_type=F32)
                acc_sc[hh] += jnp.sum(p * dp, axis=1, keepdims=True)

        @pl.when(kj == nk - 1)
        def _():
            lanes = lax.broadcasted_iota(jnp.int32, dl_ref.shape, 1)
            cur = dl_ref[...]
            for hh in range(2):
                cur = jnp.where(lanes == 2 * hp + hh, acc_sc[hh], cur)
            dl_ref[...] = cur

    kmap = lambda off: (lambda qi, hp, kj: (jnp.minimum(kj, last(qi)), off + hp))
    qblk = pl.BlockSpec((tq, 128), lambda qi, hp, kj: (qi, hp))
    qall = pl.BlockSpec((tq, 128), lambda qi, hp, kj: (qi, 0))
    return pl.pallas_call(
        body, name=name, grid=(nq, HP, nk),
        in_specs=[qblk, pl.BlockSpec((tk, 128), kmap(HP)), pl.BlockSpec((tk, 128), kmap(2 * HP)), qall,
                  pl.BlockSpec((None, 8, tk), lambda qi, hp, kj: (hp, 0, jnp.minimum(kj, last(qi)))), qblk, qall],
        out_specs=qall, out_shape=jax.ShapeDtypeStruct((S, 128), F32),
        scratch_shapes=[pltpu.VMEM((2, tq, 1), F32)],
        compiler_params=_cp("parallel", "arbitrary", "arbitrary"))(qkv, qkv, qkv, fcol, frow, do, lse)


def _attn_bwd_dq(qkv, fcol, frow, delta, do, lse, *, tq, tk, name):
    S = qkv.shape[0]
    nq, nk = S // tq, S // tk
    last = lambda qi: (qi * tq + tq - 1) // tk

    def body(q_ref, k_ref, v_ref, fc_ref, fr_ref, dl_ref, do_ref, lse_ref, dq_ref, acc_sc, col_sc):
        qi, hp, kj = pl.program_id(0), pl.program_id(1), pl.program_id(2)

        @pl.when(kj == 0)
        def _():
            acc_sc[...] = jnp.zeros_like(acc_sc)
            for hh in range(2):
                col_sc[hh, 0] = _lane_pick(fc_ref[...], 2 * hp + hh)
                col_sc[hh, 1] = _lane_pick(lse_ref[...], 2 * hp + hh)
                col_sc[hh, 2] = _lane_pick(dl_ref[...], 2 * hp + hh)

        @pl.when(kj <= last(qi))
        def _():
            for hh in range(2):
                sl = slice(hh * HEAD_DIM, (hh + 1) * HEAD_DIM)
                q = q_ref[:, sl] * Q_SCALE
                k = k_ref[:, sl]
                s = _causal_logits(q, k, col_sc[hh, 0], fr_ref[hh:hh + 1, :], qi * tq, kj * tk)
                p = jnp.exp(s - col_sc[hh, 1])
                dp = lax.dot_general(do_ref[:, sl], v_ref[:, sl], NT_DIMS, preferred_element_type=F32)
                ds = p * (dp - col_sc[hh, 2])
                acc_sc[hh] += jnp.dot(ds.astype(BF16), k, preferred_element_type=F32)

        @pl.when(kj == nk - 1)
        def _():
            for hh in range(2):
                dq_ref[:, hh * HEAD_DIM:(hh + 1) * HEAD_DIM] = (acc_sc[hh] * Q_SCALE).astype(dq_ref.dtype)

    kmap = lambda off: (lambda qi, hp, kj: (jnp.minimum(kj, last(qi)), off + hp))
    qblk = pl.BlockSpec((tq, 128), lambda qi, hp, kj: (qi, hp))
    qall = pl.BlockSpec((tq, 128), lambda qi, hp, kj: (qi, 0))
    return pl.pallas_call(
        body, name=name, grid=(nq, HP, nk),
        in_specs=[qblk, pl.BlockSpec((tk, 128), kmap(HP)), pl.BlockSpec((tk, 128), kmap(2 * HP)), qall,
                  pl.BlockSpec((None, 8, tk), lambda qi, hp, kj: (hp, 0, jnp.minimum(kj, last(qi)))), qall, qblk, qall],
        out_specs=qblk, out_shape=jax.ShapeDtypeStruct((S, D_MODEL), BF16),
        scratch_shapes=[pltpu.VMEM((2, tq, HEAD_DIM), F32), pltpu.VMEM((2, 3, tq, 1), F32)],
        compiler_params=_cp("parallel", "parallel", "arbitrary"))(qkv, qkv, qkv, fcol, frow, delta, do, lse)


def _attn_bwd_dkv(qkv, fcol, frow, delta, do, lse, *, tq, tk, name):
    S = qkv.shape[0]
    nq, nk = S // tq, S // tk
    first = lambda kj: (kj * tk) // tq

    def body(q_ref, k_ref, v_ref, fc_ref, fr_ref, dl_ref, do_ref, lse_ref, dk_ref, dv_ref, df_ref, dk_sc, dv_sc, df_sc):
        kj, hp, qi = pl.program_id(0), pl.program_id(1), pl.program_id(2)

        @pl.when(qi == 0)
        def _():
            dk_sc[...] = jnp.zeros_like(dk_sc)
            dv_sc[...] = jnp.zeros_like(dv_sc)
            df_sc[...] = jnp.zeros_like(df_sc)

        @pl.when(qi >= first(kj))
        def _():
            for hh in range(2):
                sl = slice(hh * HEAD_DIM, (hh + 1) * HEAD_DIM)
                q = q_ref[:, sl] * Q_SCALE
                do_h = do_ref[:, sl]
                fq = _lane_pick(fc_ref[...], 2 * hp + hh)
                lse_c = _lane_pick(lse_ref[...], 2 * hp + hh)
                delta = _lane_pick(dl_ref[...], 2 * hp + hh)
                s = _causal_logits(q, k_ref[:, sl], fq, fr_ref[hh:hh + 1, :], qi * tq, kj * tk)
                p = jnp.exp(s - lse_c)
                dp = lax.dot_general(do_h, v_ref[:, sl], NT_DIMS, preferred_element_type=F32)
                ds = p * (dp - delta)
                dv_sc[hh] += lax.dot_general(p.astype(BF16), do_h, TN_DIMS, preferred_element_type=F32)
                dk_sc[hh] += lax.dot_general(ds.astype(BF16), q, TN_DIMS, preferred_element_type=F32)
                df_sc[hh:hh + 1, :] -= jnp.sum(ds, axis=0, keepdims=True)

        @pl.when(qi == nq - 1)
        def _():
            for hh in range(2):
                sl = slice(hh * HEAD_DIM, (hh + 1) * HEAD_DIM)
                dk_ref[:, sl] = dk_sc[hh].astype(dk_ref.dtype)
                dv_ref[:, sl] = dv_sc[hh].astype(dv_ref.dtype)
            df_ref[...] = df_sc[...]

    qeff = lambda kj, qi: jnp.maximum(qi, first(kj))
    qblk = pl.BlockSpec((tq, 128), lambda kj, hp, qi: (qeff(kj, qi), hp))
    qall = pl.BlockSpec((tq, 128), lambda kj, hp, qi: (qeff(kj, qi), 0))
    kblk = lambda off: pl.BlockSpec((tk, 128), lambda kj, hp, qi: (kj, off + hp))
    oblk = pl.BlockSpec((tk, 128), lambda kj, hp, qi: (kj, hp))
    fblk = pl.BlockSpec((None, 8, tk), lambda kj, hp, qi: (hp, 0, kj))
    return pl.pallas_call(
        body, name=name, grid=(nk, HP, nq),
        in_specs=[qblk, kblk(HP), kblk(2 * HP), qall, fblk, qall, qblk, qall],
        out_specs=[oblk, oblk, fblk],
        out_shape=[jax.ShapeDtypeStruct((S, D_MODEL), BF16), jax.ShapeDtypeStruct((S, D_MODEL), BF16),
                   jax.ShapeDtypeStruct((HP, 8, S), F32)],
        scratch_shapes=[pltpu.VMEM((2, tk, HEAD_DIM), F32), pltpu.VMEM((2, tk, HEAD_DIM), F32), pltpu.VMEM((8, tk), F32)],
        compiler_params=_cp("parallel", "parallel", "arbitrary"))(qkv, qkv, qkv, fcol, frow, delta, do, lse)


HALF = S5_SUB * S5_P
NCB = 2 * HALF // 128
RE, IM = slice(0, NCB // 2), slice(NCB // 2, NCB)


def _gelu(x):
    return 0.5 * x * (1.0 + jnp.tanh(GELU_K * (x + 0.044715 * x * x * x)))


def _gelu_grad(x):
    t = jnp.tanh(GELU_K * (x + 0.044715 * x * x * x))
    return 0.5 * (1.0 + t) + 0.5 * x * (1.0 - t * t) * GELU_K * (1.0 + 3.0 * 0.044715 * x * x)


def _s5_put(buf, s, val, lt):
    for cb in range(NCB):
        buf[cb, pl.ds(s, lt, stride=S5_SUB), :] = val[:, 128 * cb:128 * (cb + 1)]


def _s5_get(buf, s, lt):
    return jnp.concatenate([buf[cb, pl.ds(s, lt, stride=S5_SUB), :] for cb in range(NCB)], axis=1)


def _s5_project_in(u_ref, wb_ref, buf, lt):
    for s in range(S5_SUB):
        _s5_put(buf, s, jnp.dot(u_ref[:, 128 * s:128 * (s + 1)], wb_ref[s], preferred_element_type=F32), lt)


def _s5_scan(buf, lam_ref, h0, lt):
    a_re, a_im = lam_ref[RE], lam_ref[IM]

    def step(t, carry):
        hr, hi = carry
        r0 = pl.multiple_of(t * S5_SUB, S5_SUB)
        nr = a_re * hr - a_im * hi + buf[RE, pl.ds(r0, S5_SUB), :]
        ni = a_re * hi + a_im * hr + buf[IM, pl.ds(r0, S5_SUB), :]
        buf[RE, pl.ds(r0, S5_SUB), :] = nr
        buf[IM, pl.ds(r0, S5_SUB), :] = ni
        return nr, ni

    return lax.fori_loop(0, lt, step, (h0[RE], h0[IM]), unroll=4)


def _s5_fwd(u, wb, wc, lam, dskip, *, lt, name):
    S, D = u.shape
    nt = S // lt

    def body(u_ref, wb_ref, wc_ref, lam_ref, d_ref, yp_ref, yg_ref, st_ref, buf, h_sc):
        @pl.when(pl.program_id(0) == 0)
        def _():
            h_sc[...] = jnp.zeros_like(h_sc)

        st_ref[...] = h_sc[...]
        _s5_project_in(u_ref, wb_ref, buf, lt)
        hr, hi = _s5_scan(buf, lam_ref, h_sc[...], lt)
        h_sc[RE] = hr
        h_sc[IM] = hi
        for s in range(S5_SUB):
            cols = slice(128 * s, 128 * (s + 1))
            hs = _s5_get(buf, s, lt).astype(BF16)
            yp = jnp.dot(hs, wc_ref[s], preferred_element_type=F32) + d_ref[:, cols] * u_ref[:, cols].astype(F32)
            yp_ref[:, cols] = yp.astype(yp_ref.dtype)
            yg_ref[:, cols] = _gelu(yp).astype(yg_ref.dtype)

    blk = pl.BlockSpec((lt, D), lambda i: (i, 0))
    full = lambda shp: pl.BlockSpec(shp, lambda i: (0,) * len(shp))
    state = (NCB, S5_SUB, 128)
    return pl.pallas_call(
        body, name=name, grid=(nt,),
        in_specs=[blk, full(wb.shape), full(wc.shape), full(lam.shape), full(dskip.shape)],
        out_specs=[blk, blk, pl.BlockSpec((None,) + state, lambda i: (i, 0, 0, 0))],
        out_shape=[jax.ShapeDtypeStruct((S, D), BF16), jax.ShapeDtypeStruct((S, D), BF16), jax.ShapeDtypeStruct((nt,) + state, F32)],
        scratch_shapes=[pltpu.VMEM((NCB, lt * S5_SUB, 128), F32), pltpu.VMEM(state, F32)],
        compiler_params=_cp("arbitrary"))(u, wb, wc, lam, dskip)


def _s5_bwd(u, dyp, st, wb, wc, lam, dskip, *, lt, name):
    S, D = u.shape
    nt = S // lt

    def body(u_ref, dy_ref, st_ref, wb_ref, wc_ref, lam_ref, d_ref, du_ref, dwb_ref, dwc_ref, dlam_ref, hbuf, gbuf, g_sc):
        @pl.when(pl.program_id(0) == 0)
        def _():
            g_sc[...] = jnp.zeros_like(g_sc)
            dwb_ref[...] = jnp.zeros_like(dwb_ref)
            dwc_ref[...] = jnp.zeros_like(dwc_ref)
            dlam_ref[...] = jnp.zeros_like(dlam_ref)

        _s5_project_in(u_ref, wb_ref, hbuf, lt)
        _s5_scan(hbuf, lam_ref, st_ref[...], lt)
        for s in range(S5_SUB):
            dys = dy_ref[:, 128 * s:128 * (s + 1)]
            _s5_put(gbuf, s, lax.dot_general(dys, wc_ref[s], NT_DIMS, preferred_element_type=F32), lt)
        a_re, a_im = lam_ref[RE], lam_ref[IM]

        def one(t, carry, hp_re, hp_im):
            gr, gi, dar, dai = carry
            r0 = pl.multiple_of(t * S5_SUB, S5_SUB)
            nr = gbuf[RE, pl.ds(r0, S5_SUB), :] + a_re * gr + a_im * gi
            ni = gbuf[IM, pl.ds(r0, S5_SUB), :] + a_re * gi - a_im * gr
            gbuf[RE, pl.ds(r0, S5_SUB), :] = nr
            gbuf[IM, pl.ds(r0, S5_SUB), :] = ni
            return nr, ni, dar + nr * hp_re + ni * hp_im, dai + ni * hp_re - nr * hp_im

        def step(k, carry):
            t = lt - 1 - k
            p0 = pl.multiple_of((t - 1) * S5_SUB, S5_SUB)
            return one(t, carry, hbuf[RE, pl.ds(p0, S5_SUB), :], hbuf[IM, pl.ds(p0, S5_SUB), :])

        init = (g_sc[RE], g_sc[IM], dlam_ref[RE], dlam_ref[IM])
        carry = lax.fori_loop(0, lt - 1, step, init, unroll=2)
        gr, gi, dar, dai = one(0, carry, st_ref[RE], st_ref[IM])
        g_sc[RE] = gr
        g_sc[IM] = gi
        dlam_ref[RE] = dar
        dlam_ref[IM] = dai
        for s in range(S5_SUB):
            cols = slice(128 * s, 128 * (s + 1))
            gs = _s5_get(gbuf, s, lt).astype(BF16)
            hs = _s5_get(hbuf, s, lt).astype(BF16)
            us, dys = u_ref[:, cols], dy_ref[:, cols]
            du = lax.dot_general(gs, wb_ref[s], NT_DIMS, preferred_element_type=F32) + d_ref[:, cols] * dys.astype(F32)
            du_ref[:, cols] = du.astype(du_ref.dtype)
            dwb_ref[s] += lax.dot_general(us, gs, TN_DIMS, preferred_element_type=F32)
            dwc_ref[s] += lax.dot_general(hs, dys, TN_DIMS, preferred_element_type=F32)

    blk = pl.BlockSpec((lt, D), lambda i: (nt - 1 - i, 0))
    full = lambda shp: pl.BlockSpec(shp, lambda i: (0,) * len(shp))
    state = (NCB, S5_SUB, 128)
    words = pltpu.VMEM((NCB, lt * S5_SUB, 128), F32)
    return pl.pallas_call(
        body, name=name, grid=(nt,),
        in_specs=[blk, blk, pl.BlockSpec((None,) + state, lambda i: (nt - 1 - i, 0, 0, 0)),
                  full(wb.shape), full(wc.shape), full(lam.shape), full(dskip.shape)],
        out_specs=[blk, full(wb.shape), full(wc.shape), full(lam.shape)],
        out_shape=[jax.ShapeDtypeStruct((S, D), BF16), jax.ShapeDtypeStruct(wb.shape, F32), jax.ShapeDtypeStruct(wc.shape, F32),
                   jax.ShapeDtypeStruct(lam.shape, F32)],
        scratch_shapes=[words, words, pltpu.VMEM(state, F32)],
        compiler_params=_cp("arbitrary"))(u, dyp, st, wb, wc, lam, dskip)


def _s5_disc(lam_re, lam_im, log_dt, b_re, b_im):
    dt = jnp.exp(log_dt)[:, None]
    mag = jnp.exp(lam_re * dt)
    lb_re, lb_im = mag * jnp.cos(lam_im * dt), mag * jnp.sin(lam_im * dt)
    num_re = lb_re - 1.0
    den = lam_re * lam_re + lam_im * lam_im
    k_re = (num_re * lam_re + lb_im * lam_im) / den
    k_im = (lb_im * lam_re - num_re * lam_im) / den
    bb_re = k_re[..., None] * b_re - k_im[..., None] * b_im
    bb_im = k_re[..., None] * b_im + k_im[..., None] * b_re
    return lb_re, lb_im, bb_re, bb_im


def _s5_pack(lb_re, lb_im, bb_re, bb_im, c_re, c_im):
    eye = jnp.eye(S5_SUB, dtype=F32)
    tb = lambda x: jnp.einsum("sgcp,gh->sgchp", x.reshape(S5_SUB, S5_SUB, S5_P, S5_C).transpose(0, 1, 3, 2), eye).reshape(S5_SUB, 128, HALF)
    tc = lambda x: jnp.einsum("sgpc,gh->sgphc", x.reshape(S5_SUB, S5_SUB, S5_C, S5_P).transpose(0, 1, 3, 2), eye).reshape(S5_SUB, HALF, 128)
    wb = jnp.concatenate([tb(bb_re), tb(bb_im)], axis=2).astype(BF16)
    wc = jnp.concatenate([tc(c_re), -tc(c_im)], axis=1).astype(BF16)
    lam = jnp.concatenate([lb_re.reshape(S5_SUB, HALF), lb_im.reshape(S5_SUB, HALF)], axis=1)
    return wb, wc, lam.reshape(S5_SUB, NCB, 128).transpose(1, 0, 2)


def _s5_unpack(dwb, dwc, dlam):
    db = jnp.einsum("sgcrgp->rsgpc", dwb.reshape(S5_SUB, S5_SUB, S5_C, 2, S5_SUB, S5_P)).reshape(2, S5_G, S5_P, S5_C)
    dc = jnp.einsum("srgpgc->rsgcp", dwc.reshape(S5_SUB, 2, S5_SUB, S5_P, S5_SUB, S5_C)).reshape(2, S5_G, S5_C, S5_P)
    dlam = dlam.transpose(1, 0, 2).reshape(S5_SUB, 2 * HALF)
    return (dlam[:, :HALF].reshape(S5_G, S5_P), dlam[:, HALF:].reshape(S5_G, S5_P), db[0], db[1], dc[0], -dc[1])


def _glu_fwd(yg, gl, *, name):
    S, D = yg.shape
    tm = _tile(S, 512, 8)
    blk = pl.BlockSpec((tm, D), lambda i: (i, 0))

    def body(y_ref, g_ref, o_ref):
        o_ref[...] = (y_ref[...].astype(F32) * jax.nn.sigmoid(g_ref[...].astype(F32))).astype(o_ref.dtype)

    return pl.pallas_call(body, name=name, grid=(S // tm,), in_specs=[blk, blk], out_specs=blk,
                          out_shape=jax.ShapeDtypeStruct((S, D), BF16), compiler_params=_cp("parallel"))(yg, gl)


def _glu_bwd(dy2, yg, gl, *, name):
    S, D = yg.shape
    tm = _tile(S, 512, 8)
    blk = pl.BlockSpec((tm, D), lambda i: (i, 0))

    def body(d_ref, y_ref, g_ref, da_ref, dg_ref):
        d = d_ref[...].astype(F32)
        sg = jax.nn.sigmoid(g_ref[...].astype(F32))
        da_ref[...] = d * sg
        dg_ref[...] = (d * y_ref[...].astype(F32) * sg * (1.0 - sg)).astype(dg_ref.dtype)

    return pl.pallas_call(body, name=name, grid=(S // tm,), in_specs=[blk, blk, blk], out_specs=[blk, blk],
                          out_shape=[jax.ShapeDtypeStruct((S, D), F32), jax.ShapeDtypeStruct((S, D), BF16)],
                          compiler_params=_cp("parallel"))(dy2, yg, gl)


def _gelu_bwd(da, db, ypre, u, *, name):
    S, D = ypre.shape
    tm = _tile(S, 512, 8)
    blk = pl.BlockSpec((tm, D), lambda i: (i, 0))
    vec = pl.BlockSpec((1, D), lambda i: (0, 0))

    def body(a_ref, b_ref, y_ref, u_ref, o_ref, s_ref):
        @pl.when(pl.program_id(0) == 0)
        def _():
            s_ref[...] = jnp.zeros_like(s_ref)

        dy = (a_ref[...] + b_ref[...]) * _gelu_grad(y_ref[...].astype(F32))
        o_ref[...] = dy.astype(o_ref.dtype)
        s_ref[...] += jnp.sum(dy * u_ref[...].astype(F32), axis=0, keepdims=True)

    return pl.pallas_call(body, name=name, grid=(S // tm,), in_specs=[blk, blk, blk, blk], out_specs=[blk, vec],
                          out_shape=[jax.ShapeDtypeStruct((S, D), BF16), jax.ShapeDtypeStruct((1, D), F32)],
                          compiler_params=_cp("arbitrary"))(da, db, ypre, u)


def _my_index():
    return 4 * lax.axis_index("x") + 2 * lax.axis_index("y") + lax.axis_index("c")


def _exchange(x, *, gather, name):
    shp = x.shape if gather else x.shape[1:]

    def body(x_ref, o_ref, send_sems, recv_sems, local_sem):
        mx, my, mc = lax.axis_index("x"), lax.axis_index("y"), lax.axis_index("c")
        me = 4 * mx + 2 * my + mc
        src = (lambda j: x_ref) if gather else (lambda j: x_ref.at[j])
        local = pltpu.make_async_copy(src(me), o_ref.at[me], local_sem)
        local.start()
        copies = []
        for k in range(1, N_DEV):
            px, py, pc = mx ^ (k >> 2), my ^ ((k >> 1) & 1), mc ^ (k & 1)
            cp = pltpu.make_async_remote_copy(src(4 * px + 2 * py + pc), o_ref.at[me], send_sems.at[k - 1], recv_sems.at[k - 1],
                                              device_id=(px, py, pc), device_id_type=MESH_ID)
            cp.start()
            copies.append(cp)
        for cp in copies:
            cp.wait()
        local.wait()

    return pl.pallas_call(
        body, name=name, in_specs=[pl.BlockSpec(memory_space=pl.ANY)], out_specs=pl.BlockSpec(memory_space=pl.ANY),
        out_shape=jax.ShapeDtypeStruct((N_DEV,) + shp, x.dtype),
        scratch_shapes=[pltpu.SemaphoreType.DMA((N_DEV - 1,)), pltpu.SemaphoreType.DMA((N_DEV - 1,)), pltpu.SemaphoreType.DMA(())],
        compiler_params=pltpu.CompilerParams(has_side_effects=True))(x)


def _sum8(x8, *, name):
    _, R, C = x8.shape
    tr = _tile(R, 256, 8)

    def body(x_ref, o_ref):
        acc = x_ref[0].astype(F32)
        for j in range(1, N_DEV):
            acc = acc + x_ref[j].astype(F32)
        o_ref[...] = acc

    return pl.pallas_call(body, name=name, grid=(R // tr,), in_specs=[pl.BlockSpec((N_DEV, tr, C), lambda i: (0, i, 0))],
                          out_specs=pl.BlockSpec((tr, C), lambda i: (i, 0)), out_shape=jax.ShapeDtypeStruct((R, C), F32),
                          compiler_params=_cp("parallel"))(x8)


def _pack(arrs, dtype, row_mult):
    flat = jnp.concatenate([a.reshape(-1).astype(dtype) for a in arrs])
    n = flat.shape[0]
    rows = -(-n // 1024)
    rows = -(-rows // row_mult) * row_mult
    return jnp.pad(flat, (0, rows * 1024 - n)).reshape(rows, 1024)


def _unpack(buf, shapes):
    lead = buf.shape[:-2]
    flat = buf.reshape(lead + (-1,))
    out, off = [], 0
    for s in shapes:
        n = math.prod(s)
        out.append(flat[..., off:off + n].reshape(lead + tuple(s)))
        off += n
    return out


def _adaln_fwd(c_all, w, b, *, name):
    nk, D, n = w.shape

    def body(c_ref, w_ref, b_ref, o_ref):
        cv = c_ref[...]
        sc = (cv * jax.nn.sigmoid(cv)).astype(BF16)
        o_ref[...] = jnp.dot(sc, w_ref[...].astype(BF16), preferred_element_type=F32) + b_ref[...]

    return pl.pallas_call(body, name=name, grid=(nk,),
                          in_specs=[pl.BlockSpec((N_DEV, D), lambda k: (0, 0)), pl.BlockSpec((None, D, n), lambda k: (k, 0, 0)),
                                    pl.BlockSpec((None, 1, n), lambda k: (k, 0, 0))],
                          out_specs=pl.BlockSpec((None, N_DEV, n), lambda k: (k, 0, 0)),
                          out_shape=jax.ShapeDtypeStruct((nk, N_DEV, n), F32), compiler_params=_cp("parallel"))(c_all, w, b)


def _adaln_bwd(c_all, dmod, *, name):
    nk, _, n = dmod.shape
    D = c_all.shape[1]

    def body(c_ref, d_ref, dw_ref, db_ref):
        cv = c_ref[...]
        sc = (cv * jax.nn.sigmoid(cv)).astype(BF16).astype(F32)
        dm = d_ref[...]
        dw_ref[...] = lax.dot_general(sc, dm.astype(BF16).astype(F32), TN_DIMS, precision=lax.Precision.HIGHEST,
                                      preferred_element_type=F32)
        db_ref[...] = jnp.sum(dm, axis=0, keepdims=True)

    return pl.pallas_call(body, name=name, grid=(nk,),
                          in_specs=[pl.BlockSpec((N_DEV, D), lambda k: (0, 0)), pl.BlockSpec((None, N_DEV, n), lambda k: (k, 0, 0))],
                          out_specs=[pl.BlockSpec((None, D, n), lambda k: (k, 0, 0)), pl.BlockSpec((None, 1, n), lambda k: (k, 0, 0))],
                          out_shape=[jax.ShapeDtypeStruct((nk, D, n), F32), jax.ShapeDtypeStruct((nk, 1, n), F32)],
                          compiler_params=_cp("parallel"))(c_all, dmod)


def _adamw(w, g, m, v, *, name):
    R, C = w.shape
    tr = _tile(R, 256, 8)
    blk = pl.BlockSpec((tr, C), lambda i: (i, 0))
    c1 = 1.0 / (1.0 - ADAM_B1 ** ADAM_STEP)
    c2 = 1.0 / (1.0 - ADAM_B2 ** ADAM_STEP)

    def body(w_ref, g_ref, m_ref, v_ref, d_ref, nm_ref, nv_ref):
        gv = g_ref[...]
        nm = ADAM_B1 * m_ref[...] + (1.0 - ADAM_B1) * gv
        nv = ADAM_B2 * v_ref[...] + (1.0 - ADAM_B2) * (gv * gv)
        nm_ref[...] = nm
        nv_ref[...] = nv
        d_ref[...] = -ADAM_LR * ((nm * c1) / (jnp.sqrt(nv * c2) + ADAM_EPS) + ADAM_WD * w_ref[...])

    sh = jax.ShapeDtypeStruct((R, C), F32)
    return pl.pallas_call(body, name=name, grid=(R // tr,), in_specs=[blk] * 4, out_specs=[blk] * 3, out_shape=[sh] * 3,
                          compiler_params=_cp("parallel"))(w, g, m, v)


def _adamw_nd(w, g, m, v, *, name):
    shp = w.shape
    two = (-1, shp[-1]) if w.ndim > 1 else (1, -1)
    outs = _adamw(w.reshape(two), g.reshape(two), m.reshape(two), v.reshape(two), name=name)
    return [o.reshape(shp) for o in outs]


S5_TILE = 256
ATTN_TQ = 256
ATTN_TK = 256
BIG = ("s5_w_in", "s5_w_glu", "s5_w_out", "fox_w_in", "fox_w_out", "ffn_w_up", "ffn_w_down")
WEIGHTS = ("norm_g", "ada_w", "ada_b", "s5_w_in", "s5_lam_re", "s5_lam_im", "s5_log_dt", "s5_b_re", "s5_b_im", "s5_c_re",
           "s5_c_im", "s5_d", "s5_w_glu", "s5_w_out", "fox_w_in", "fox_b_f", "fox_w_out", "ffn_w_up", "ffn_conv_w",
           "ffn_conv_b", "ffn_w_down", "final_g")
REPLICATED = ("s5_lam_re", "s5_lam_im", "s5_log_dt", "s5_b_re", "s5_b_im", "s5_c_re", "s5_c_im", "s5_d", "fox_b_f",
              "ffn_conv_b", "final_g")


def _full_from_gathered(name, g):
    if name in ("s5_w_in", "s5_w_glu", "s5_w_out", "fox_w_out"):
        return g.reshape(D_MODEL, D_MODEL)
    if name == "fox_w_in":
        return g[:, 0].transpose(1, 0, 2).reshape(D_MODEL, -1)
    if name == "ffn_w_up":
        return g.transpose(1, 2, 0, 3).reshape(2, D_MODEL, 2 * D_FF)
    if name == "ffn_w_down":
        return g.transpose(1, 0, 2, 3).reshape(2, D_FF, D_MODEL)
    raise ValueError(name)


def _chunks_of_full(name, full):
    if name in ("s5_w_in", "s5_w_glu", "s5_w_out", "fox_w_out"):
        return full.reshape(N_DEV, 1, D_MODEL // N_DEV, D_MODEL)
    if name == "fox_w_in":
        return full.reshape(D_MODEL, N_DEV, -1).transpose(1, 0, 2)[:, None]
    if name == "ffn_w_up":
        return full.reshape(2, D_MODEL, N_DEV, -1).transpose(2, 0, 1, 3)
    if name == "ffn_w_down":
        return full.reshape(2, N_DEV, -1, D_MODEL).transpose(1, 0, 2, 3)
    raise ValueError(name)


def kernel(x, c, norm_g, ada_w, ada_b, s5_w_in, s5_lam_re, s5_lam_im, s5_log_dt, s5_b_re, s5_b_im, s5_c_re, s5_c_im, s5_d, s5_w_glu, s5_w_out, fox_w_in, fox_b_f, fox_w_out, ffn_w_up, ffn_conv_w, ffn_conv_b, ffn_w_down, final_g, loss_target, m_norm_g, m_ada_w, m_ada_b, m_s5_w_in, m_s5_lam_re, m_s5_lam_im, m_s5_log_dt, m_s5_b_re, m_s5_b_im, m_s5_c_re, m_s5_c_im, m_s5_d, m_s5_w_glu, m_s5_w_out, m_fox_w_in, m_fox_b_f, m_fox_w_out, m_ffn_w_up, m_ffn_conv_w, m_ffn_conv_b, m_ffn_w_down, m_final_g, v_norm_g, v_ada_w, v_ada_b, v_s5_w_in, v_s5_lam_re, v_s5_lam_im, v_s5_log_dt, v_s5_b_re, v_s5_b_im, v_s5_c_re, v_s5_c_im, v_s5_d, v_s5_w_glu, v_s5_w_out, v_fox_w_in, v_fox_b_f, v_fox_w_out, v_ffn_w_up, v_ffn_conv_w, v_ffn_conv_b, v_ffn_w_down, v_final_g):
    args = dict(locals())
    W = {n: args[n] for n in WEIGHTS}
    M = {n: args["m_" + n] for n in WEIGHTS}
    V = {n: args["v_" + n] for n in WEIGHTS}
    D, F = D_MODEL, D_FF
    me = _my_index()
    h0 = x[0]
    S = h0.shape[0]
    lt = min(S5_TILE, S)
    tq, tk = min(ATTN_TQ, S), min(ATTN_TK, S)
    n_g, n_cw = norm_g.size, ffn_conv_w.size

    g0 = _exchange(_pack([c, norm_g, ffn_conv_w], F32, 8), gather=True, name="gather_small_in")
    c_all, ng_all, cw_all = _unpack(g0, [(D,), norm_g.shape, ffn_conv_w.shape])
    ng_full = ng_all.transpose(1, 2, 0, 3).reshape(2, 2, D)
    cw_full = cw_all.transpose(1, 2, 0, 3).reshape(2, 3, F)

    ncol = ada_w.shape[-1]
    modp = _adaln_fwd(c_all, ada_w.reshape(4, D, ncol), ada_b.reshape(4, 1, ncol), name="adaln_fwd")
    g1 = _exchange(_pack([modp], F32, 8), gather=True, name="gather_adaln")
    (mod_all,) = _unpack(g1, [modp.shape])
    mod = lax.dynamic_index_in_dim(mod_all, me, axis=2, keepdims=False).transpose(1, 0, 2).reshape(4, 3 * D)
    shift = [mod[k:k + 1, :D] for k in range(4)]
    scale = [mod[k:k + 1, D:2 * D] for k in range(4)]
    gate = [mod[k:k + 1, 2 * D:] for k in range(4)]
    gain = [_row(ng_full[k // 2, k % 2]) for k in range(4)]

    gw = _exchange(_pack([W[n] for n in BIG], BF16, 16), gather=True, name="gather_weights")
    full = {n: _full_from_gathered(n, g) for n, g in zip(BIG, _unpack(gw, [W[n].shape for n in BIG]))}
    w_proj = jnp.pad(full["fox_w_in"], ((0, 0), (0, 3 * D + 128 - full["fox_w_in"].shape[1])))
    w_qkv, w_f = w_proj[:, :3 * D], w_proj[:, 3 * D:]
    bf_pad = jnp.pad(fox_b_f, ((0, 0), (0, 128 - HEADS)))

    def ffn_fwd(h, k, layer):
        hn = _modulate(h, gain[k], shift[k], scale[k], name=f"modulate{k}")
        up = _mm(hn, full["ffn_w_up"][layer], name=f"ffn_up{layer}", o_h=True, out_dtype=BF16, tn=1408)
        z = _conv_gate_fwd(up, cw_full[layer], ffn_conv_b[layer:layer + 1], name=f"conv_gate{layer}")
        m, h_out = _mm(z, full["ffn_w_down"][layer], name=f"ffn_down{layer}", res=h, gate=gate[k])
        return h_out, (hn, up, z, m)

    lb_re, lb_im, bb_re, bb_im = _s5_disc(s5_lam_re[0], s5_lam_im[0], s5_log_dt[0], s5_b_re[0], s5_b_im[0])
    wb, wc, lam = _s5_pack(lb_re, lb_im, bb_re, bb_im, s5_c_re[0], s5_c_im[0])
    hn0 = _modulate(h0, gain[0], shift[0], scale[0], name="modulate0")
    u = _mm(hn0, full["s5_w_in"], name="s5_in", out_dtype=BF16)
    ypre, yg, st = _s5_fwd(u, wb, wc, lam, s5_d, lt=lt, name="s5_scan")
    gl = _mm(yg, full["s5_w_glu"], name="s5_glu", out_dtype=BF16)
    y2 = _glu_fwd(yg, gl, name="s5_glu_gate")
    m0, h1 = _mm(y2, full["s5_w_out"], name="s5_out", res=h0, gate=gate[0])
    h2, (hn1, up0, z0, m1) = ffn_fwd(h1, 1, 0)

    hn2 = _modulate(h2, gain[2], shift[2], scale[2], name="modulate2")
    qkv = _mm(hn2, w_qkv, name="fox_qkv", out_dtype=BF16)
    fl = _mm(hn2, w_f, name="fox_f", out_dtype=F32)
    fcol = _fgate_fwd(fl, bf_pad, name="fox_fgate")
    frow = jnp.pad(fcol[:, :HEADS].T.reshape(HP, 2, S), ((0, 0), (0, 6), (0, 0)))
    o, lse = _attn_fwd(qkv, fcol, frow, tq=tq, tk=tk, name="fox_attn")
    m2, h3 = _mm(o, full["fox_w_out"], name="fox_out", res=h2, gate=gate[2])
    h4, (hn3, up1, z1, m3) = ffn_fwd(h3, 3, 1)

    lblk, dh, d_final_g = _loss_head(h4, _row(final_g), loss_target[0], name="loss_head")
    loss = lax.psum(lblk[0, 0], ("x", "y", "c"))

    G = {}
    dmod = [None] * 4

    def norm_bwd(h, dhn, dh_in, k):
        dh_out, sums = _modulate_bwd(h, dhn, dh_in, gain[k], scale[k], name=f"modulate_bwd{k}")
        return dh_out, sums[0:1], sums[1:2], sums[2]

    def ffn_bwd(dh_in, h, k, layer, saved):
        hn, up, z, m = saved
        dm, dgate = _gate_bwd(dh_in, m, gate[k], name=f"gate_bwd{k}")
        dz = _mm(dm, full["ffn_w_down"][layer], name=f"ffn_down_dx{layer}", tb=True, out_dtype=BF16, tn=1408)
        dw_down = _mm(z, dm, name=f"ffn_down_dw{layer}", ta=True, tm=1408, tn=512, tk=512)
        d_up, cs = _conv_gate_bwd(up, dz, cw_full[layer], ffn_conv_b[layer:layer + 1], name=f"conv_gate_bwd{layer}")
        dhn = _mm(d_up, full["ffn_w_up"][layer], name=f"ffn_up_dx{layer}", tb=True, a_h=True, tk=1408)
        dw_up = _mm(hn, d_up, name=f"ffn_up_dw{layer}", ta=True, b_h=True, tm=1024, tn=1408, tk=512)
        dh_out, dshift, dscale, dg = norm_bwd(h, dhn, dh_in, k)
        dmod[k] = jnp.concatenate([dshift, dscale, dgate], axis=1)
        return dh_out, dg, dw_up, dw_down, cs[0:3], cs[3]

    dh, dg3, dw_up1, dw_down1, dcw1, dcb1 = ffn_bwd(dh, h3, 3, 1, (hn3, up1, z1, m3))

    dm, dgate2 = _gate_bwd(dh, m2, gate[2], name="gate_bwd2")
    do = _mm(dm, full["fox_w_out"], name="fox_out_dx", tb=True, out_dtype=BF16)
    G["fox_w_out"] = _mm(o, dm, name="fox_out_dw", ta=True, tk=512)
    delta = _attn_bwd_delta(qkv, fcol, frow, do, lse, tq=tq, tk=tk, name="fox_attn_delta")
    dq = _attn_bwd_dq(qkv, fcol, frow, delta, do, lse, tq=tq, tk=tk, name="fox_attn_dq")
    dk, dv, dfrow = _attn_bwd_dkv(qkv, fcol, frow, delta, do, lse, tq=tq, tk=tk, name="fox_attn_dkv")
    dF = jnp.pad(dfrow[:, :2, :].reshape(HEADS, S).T, ((0, 0), (0, 128 - HEADS)))
    dfl, dbf = _fgate_bwd(dF, fl, bf_pad, name="fox_fgate_bwd")
    dproj = jnp.concatenate([dq, dk, dv, dfl.astype(BF16)], axis=1)
    dhn = _mm(dproj, w_proj, name="fox_in_dx", tb=True, tk=640)
    dw_proj = _mm(hn2, dproj, name="fox_in_dw", ta=True, tn=640, tk=512)
    G["fox_w_in"] = dw_proj[:, :full["fox_w_in"].shape[1]]
    dh, dshift, dscale, dg2 = norm_bwd(h2, dhn, dh, 2)
    dmod[2] = jnp.concatenate([dshift, dscale, dgate2], axis=1)

    dh, dg1, dw_up0, dw_down0, dcw0, dcb0 = ffn_bwd(dh, h1, 1, 0, (hn1, up0, z0, m1))
    G["ffn_w_up"] = jnp.stack([dw_up0, dw_up1])
    G["ffn_w_down"] = jnp.stack([dw_down0, dw_down1])

    dm, dgate0 = _gate_bwd(dh, m0, gate[0], name="gate_bwd0")
    dy2 = _mm(dm, full["s5_w_out"], name="s5_out_dx", tb=True, out_dtype=BF16)
    G["s5_w_out"] = _mm(y2, dm, name="s5_out_dw", ta=True, tk=512)
    da, dgl = _glu_bwd(dy2, yg, gl, name="s5_glu_bwd")
    dyg_b = _mm(dgl, full["s5_w_glu"], name="s5_glu_dx", tb=True)
    G["s5_w_glu"] = _mm(yg, dgl, name="s5_glu_dw", ta=True, tk=512)
    dyp, dd = _gelu_bwd(da, dyg_b, ypre, u, name="s5_gelu_bwd")
    du, dwb, dwc, dlam = _s5_bwd(u, dyp, st, wb, wc, lam, s5_d, lt=lt, name="s5_scan_bwd")
    dhn = _mm(du, full["s5_w_in"], name="s5_in_dx", tb=True)
    G["s5_w_in"] = _mm(hn0, du, name="s5_in_dw", ta=True, tk=512)
    dh, dshift, dscale, dg0 = norm_bwd(h0, dhn, dh, 0)
    dmod[0] = jnp.concatenate([dshift, dscale, dgate0], axis=1)
    grad_x = dh[None]

    dlb_re, dlb_im, dbb_re, dbb_im, dc_re, dc_im = _s5_unpack(dwb, dwc, dlam)
    _, disc_vjp = jax.vjp(_s5_disc, s5_lam_re[0], s5_lam_im[0], s5_log_dt[0], s5_b_re[0], s5_b_im[0])
    dlam_re, dlam_im, dlog_dt, db_re, db_im = disc_vjp((dlb_re, dlb_im, dbb_re, dbb_im))

    g2 = _exchange(_pack([jnp.concatenate(dmod, axis=0)], F32, 8), gather=True, name="gather_dmod")
    (dmod_all,) = _unpack(g2, [(4, 3 * D)])
    dmod_mine = _columns_of_mod(dmod_all, me, ncol)
    d_ada_w, d_ada_b = _adaln_bwd(c_all, dmod_mine, name="adaln_bwd")
    G["ada_w"] = d_ada_w.reshape(ada_w.shape)
    G["ada_b"] = d_ada_b.reshape(ada_b.shape)

    send = jnp.concatenate([_chunks_of_full(n, G[n]).reshape(N_DEV, -1).astype(BF16) for n in BIG], axis=1)
    rows = gw.shape[1]
    send = jnp.pad(send, ((0, 0), (0, rows * 1024 - send.shape[1]))).reshape(N_DEV, rows, 1024)
    got = _exchange(send, gather=False, name="scatter_grads")
    for n, g in zip(BIG, _unpack(_sum8(got, name="sum_grads"), [W[n].shape for n in BIG])):
        G[n] = g

    small_full = {
        "norm_g": jnp.stack([dg0, dg1, dg2, dg3]).reshape(2, 2, D),
        "ffn_conv_w": jnp.stack([dcw0, dcw1]),
        "s5_lam_re": dlam_re[None], "s5_lam_im": dlam_im[None], "s5_log_dt": dlog_dt[None],
        "s5_b_re": db_re[None], "s5_b_im": db_im[None], "s5_c_re": dc_re[None], "s5_c_im": dc_im[None],
        "s5_d": dd, "fox_b_f": dbf[:, :HEADS], "ffn_conv_b": jnp.stack([dcb0, dcb1]), "final_g": d_final_g[0],
    }
    names = tuple(small_full)
    g3 = _exchange(_pack([small_full[n] for n in names], F32, 8), gather=True, name="gather_small_grads")
    summed = dict(zip(names, _unpack(_sum8(g3, name="sum_small_grads"), [small_full[n].shape for n in names])))
    G["norm_g"] = lax.dynamic_slice_in_dim(summed["norm_g"], me * norm_g.shape[-1], norm_g.shape[-1], axis=2)
    G["ffn_conv_w"] = lax.dynamic_slice_in_dim(summed["ffn_conv_w"], me * ffn_conv_w.shape[-1], ffn_conv_w.shape[-1], axis=2)
    for n in REPLICATED:
        G[n] = summed[n]

    delta, new_m, new_v = {}, {}, {}
    small = tuple(n for n in WEIGHTS if n not in BIG and n != "ada_w")
    for n in WEIGHTS:
        if n not in small:
            delta[n], new_m[n], new_v[n] = _adamw_nd(W[n], G[n], M[n], V[n], name=f"adamw_{n}")
    packed = [_pack([src[n] for n in small], F32, 8) for src in (W, G, M, V)]
    for dst, buf in zip((delta, new_m, new_v), _adamw(*packed, name="adamw_small")):
        dst.update(zip(small, _unpack(buf, [W[n].shape for n in small])))

    return (loss, grad_x, *[G[n] for n in WEIGHTS], *[delta[n] for n in WEIGHTS], *[new_m[n] for n in WEIGHTS],
            *[new_v[n] for n in WEIGHTS])


def _columns_of_mod(dmod_all, me, ncol):
    flat = lax.dynamic_slice_in_dim(dmod_all, me * ncol, ncol, axis=2)
    return flat.transpose(1, 0, 2)
```

```python
import functools
import math

import jax
import jax.numpy as jnp
from jax import lax
from jax.experimental import pallas as pl
from jax.experimental.pallas import tpu as pltpu

F32, BF16 = jnp.float32, jnp.bfloat16
EPS = 1e-6
N_DEV = 8
D_MODEL = 1024
D_FF = 2816
HEADS = 16
HEAD_DIM = 64
S5_G, S5_P, S5_C = 64, 64, 16
S5_SUB = 8
V7X_VMEM_LIMIT = 56 * 1024 * 1024
NEG = -1e30
ADAM_LR, ADAM_B1, ADAM_B2, ADAM_EPS, ADAM_WD, ADAM_STEP = 1e-3, 0.9, 0.999, 1e-8, 0.01, 10
GELU_K = math.sqrt(2.0 / math.pi)
MESH_ID = pl.DeviceIdType.MESH


def _cp(*sem):
    return pltpu.CompilerParams(dimension_semantics=sem, vmem_limit_bytes=V7X_VMEM_LIMIT)


def _tile(n, target, mult=128):
    if n <= target:
        return n
    t = (target // mult) * mult
    while t >= mult:
        if n % t == 0:
            return t
        t -= mult
    return n


def _row(v):
    return v.reshape(1, -1).astype(F32)


def _mm(a, b, *, name, ta=False, tb=False, out_dtype=F32, tm=1024, tn=512, tk=None, res=None, gate=None,
        a_h=False, b_h=False, o_h=False):
    if a_h:
        M, K = a.shape[1], 2 * a.shape[2]
    elif ta:
        K, M = a.shape
    else:
        M, K = a.shape
    if b_h:
        N = 2 * b.shape[2]
    else:
        N = b.shape[0] if tb else b.shape[1]
    half_n = N // 2
    tm = _tile(M, tm, 128 if ta else 8)
    tn = _tile(half_n if (b_h or o_h) else N, tn)
    tk = K if tk is None else _tile(K // 2 if a_h else K, tk)
    nk = K // tk
    nkh, nnh = (K // 2) // tk if a_h else 1, half_n // tn
    if a_h:
        a_spec = pl.BlockSpec((None, tm, tk), lambda i, j, k: (k // nkh, i, k % nkh))
    elif ta:
        a_spec = pl.BlockSpec((tk, tm), lambda i, j, k: (k, i))
    else:
        a_spec = pl.BlockSpec((tm, tk), lambda i, j, k: (i, k))
    if b_h:
        b_spec = pl.BlockSpec((None, tk, tn), lambda i, j, k: (j // nnh, k, j % nnh))
    elif tb:
        b_spec = pl.BlockSpec((tn, tk), lambda i, j, k: (j, k))
    else:
        b_spec = pl.BlockSpec((tk, tn), lambda i, j, k: (k, j))
    if o_h:
        o_spec = pl.BlockSpec((None, tm, tn), lambda i, j, k: (j // nnh, i, j % nnh))
    else:
        o_spec = pl.BlockSpec((tm, tn), lambda i, j, k: (i, j))
    dn = (((0 if ta else 1,), (1 if tb else 0,)), ((), ()))
    fused = res is not None

    def body(*refs):
        if fused:
            a_ref, b_ref, r_ref, g_ref, m_ref, o_ref, acc_ref = refs
        else:
            a_ref, b_ref, o_ref, acc_ref = refs
        p = lax.dot_general(a_ref[...].astype(BF16), b_ref[...].astype(BF16), dn, preferred_element_type=F32)

        def finish(acc):
            if fused:
                m_ref[...] = acc.astype(m_ref.dtype)
                o_ref[...] = r_ref[...] + g_ref[...] * acc
            else:
                o_ref[...] = acc.astype(o_ref.dtype)

        if nk == 1:
            finish(p)
        else:
            k = pl.program_id(2)

            @pl.when(k == 0)
            def _():
                acc_ref[...] = p

            @pl.when(k > 0)
            def _():
                acc_ref[...] += p

            @pl.when(k == nk - 1)
            def _():
                finish(acc_ref[...])

    in_specs = [a_spec, b_spec]
    args = [a, b]
    if fused:
        in_specs += [o_spec, pl.BlockSpec((1, tn), lambda i, j, k: (0, j))]
        args += [res, gate]
        out_shape = [jax.ShapeDtypeStruct((M, N), BF16), jax.ShapeDtypeStruct((M, N), F32)]
        out_specs = [o_spec, o_spec]
    else:
        out_shape = jax.ShapeDtypeStruct((2, M, half_n) if o_h else (M, N), out_dtype)
        out_specs = o_spec
    return pl.pallas_call(
        body, name=name, grid=(M // tm, N // tn, nk), in_specs=in_specs, out_specs=out_specs, out_shape=out_shape,
        scratch_shapes=[pltpu.VMEM((tm, tn) if nk > 1 else (8, 128), F32)],
        compiler_params=_cp("parallel", "parallel", "arbitrary"),
    )(*args)


def _modulate(h, g, shift, scale, *, name):
    S, D = h.shape
    tm = _tile(S, 512, 8)
    vec = pl.BlockSpec((1, D), lambda i: (0, 0))
    blk = pl.BlockSpec((tm, D), lambda i: (i, 0))

    def body(h_ref, g_ref, sh_ref, sc_ref, o_ref):
        x = h_ref[...]
        r = lax.rsqrt(jnp.mean(x * x, axis=-1, keepdims=True) + EPS)
        o_ref[...] = ((x * r * g_ref[...]) * (1.0 + sc_ref[...]) + sh_ref[...]).astype(o_ref.dtype)

    return pl.pallas_call(body, name=name, grid=(S // tm,), in_specs=[blk, vec, vec, vec], out_specs=blk,
                          out_shape=jax.ShapeDtypeStruct((S, D), BF16), compiler_params=_cp("parallel"))(h, g, shift, scale)


def _gate_bwd(dh, m, gate, *, name):
    S, D = dh.shape
    tm = _tile(S, 512, 8)
    vec = pl.BlockSpec((1, D), lambda i: (0, 0))
    blk = pl.BlockSpec((tm, D), lambda i: (i, 0))

    def body(dh_ref, m_ref, g_ref, dm_ref, dg_ref):
        @pl.when(pl.program_id(0) == 0)
        def _():
            dg_ref[...] = jnp.zeros_like(dg_ref)

        d = dh_ref[...]
        dm_ref[...] = (d * g_ref[...]).astype(dm_ref.dtype)
        dg_ref[...] += jnp.sum(d * m_ref[...].astype(F32), axis=0, keepdims=True)

    return pl.pallas_call(body, name=name, grid=(S // tm,), in_specs=[blk, blk, vec], out_specs=[blk, vec],
                          out_shape=[jax.ShapeDtypeStruct((S, D), BF16), jax.ShapeDtypeStruct((1, D), F32)],
                          compiler_params=_cp("arbitrary"))(dh, m, gate)


def _modulate_bwd(h, dhn, dh_in, g, scale, *, name):
    S, D = h.shape
    tm = _tile(S, 512, 8)
    vec = pl.BlockSpec((1, D), lambda i: (0, 0))
    blk = pl.BlockSpec((tm, D), lambda i: (i, 0))
    sums = pl.BlockSpec((8, D), lambda i: (0, 0))

    def body(h_ref, dhn_ref, dhi_ref, g_ref, sc_ref, dh_ref, s_ref):
        @pl.when(pl.program_id(0) == 0)
        def _():
            s_ref[...] = jnp.zeros_like(s_ref)

        x = h_ref[...]
        r = lax.rsqrt(jnp.mean(x * x, axis=-1, keepdims=True) + EPS)
        xhat = x * r
        gv = g_ref[...]
        dhn_v = dhn_ref[...].astype(F32)
        dn = dhn_v * (1.0 + sc_ref[...])
        s_ref[0:1, :] += jnp.sum(dhn_v, axis=0, keepdims=True)
        s_ref[1:2, :] += jnp.sum(dhn_v * (xhat * gv), axis=0, keepdims=True)
        s_ref[2:3, :] += jnp.sum(dn * xhat, axis=0, keepdims=True)
        dxh = dn * gv
        dh_ref[...] = dhi_ref[...] + r * (dxh - xhat * jnp.mean(dxh * xhat, axis=-1, keepdims=True))

    return pl.pallas_call(body, name=name, grid=(S // tm,), in_specs=[blk, blk, blk, vec, vec], out_specs=[blk, sums],
                          out_shape=[jax.ShapeDtypeStruct((S, D), F32), jax.ShapeDtypeStruct((8, D), F32)],
                          compiler_params=_cp("arbitrary"))(h, dhn, dh_in, g, scale)


def _loss_head(h, g, tgt, *, name):
    S, D = h.shape
    tm = _tile(S, 512, 8)
    vec = pl.BlockSpec((1, D), lambda i: (0, 0))
    blk = pl.BlockSpec((tm, D), lambda i: (i, 0))
    lss = pl.BlockSpec((8, 128), lambda i: (0, 0))

    def body(h_ref, g_ref, t_ref, l_ref, dh_ref, dg_ref):
        @pl.when(pl.program_id(0) == 0)
        def _():
            l_ref[...] = jnp.zeros_like(l_ref)
            dg_ref[...] = jnp.zeros_like(dg_ref)

        x = h_ref[...]
        r = lax.rsqrt(jnp.mean(x * x, axis=-1, keepdims=True) + EPS)
        xhat = x * r
        gv = g_ref[...]
        e = xhat * gv - t_ref[...]
        l_ref[...] += 0.5 * jnp.sum(jnp.mean(e * e, axis=-1, keepdims=True))
        dy = e * (1.0 / D)
        dg_ref[...] += jnp.sum(dy * xhat, axis=0, keepdims=True)
        dxh = dy * gv
        dh_ref[...] = r * (dxh - xhat * jnp.mean(dxh * xhat, axis=-1, keepdims=True))

    return pl.pallas_call(body, name=name, grid=(S // tm,), in_specs=[blk, vec, blk], out_specs=[lss, blk, vec],
                          out_shape=[jax.ShapeDtypeStruct((8, 128), F32), jax.ShapeDtypeStruct((S, D), F32),
                                     jax.ShapeDtypeStruct((1, D), F32)],
                          compiler_params=_cp("arbitrary"))(h, g, tgt)


def _shift_down(x, k, edge):
    tm = x.shape[0]
    rows = lax.broadcasted_iota(jnp.int32, x.shape, 0)
    out = pltpu.roll(x, k, 0)
    for j in range(k):
        out = jnp.where(rows == j, edge[8 - k + j:8 - k + j + 1, :], out)
    return out


def _shift_up(x, k, edge):
    tm = x.shape[0]
    rows = lax.broadcasted_iota(jnp.int32, x.shape, 0)
    out = pltpu.roll(x, tm - k, 0)
    for j in range(k):
        out = jnp.where(rows == tm - k + j, edge[j:j + 1, :], out)
    return out


def _conv_gate_fwd(up, cw, cb, *, name):
    _, S, F = up.shape
    tf = _tile(F, 1408)
    nf = F // tf
    tm = _tile(S, 512, 8)

    def body(a_ref, b_ref, w_ref, cb_ref, z_ref, edge_ref):
        @pl.when(pl.program_id(1) == 0)
        def _():
            edge_ref[...] = jnp.zeros_like(edge_ref)

        a = a_ref[...].astype(F32)
        edge = edge_ref[...]
        w = w_ref[...]
        ac = cb_ref[...] + w[2:3, :] * a + w[1:2, :] * _shift_down(a, 1, edge) + w[0:1, :] * _shift_down(a, 2, edge)
        edge_ref[...] = a[tm - 8:tm, :]
        z_ref[...] = (ac * jax.nn.sigmoid(ac) * b_ref[...].astype(F32)).astype(z_ref.dtype)

    return pl.pallas_call(
        body, name=name, grid=(nf, S // tm),
        in_specs=[pl.BlockSpec((None, tm, tf), lambda j, i: (0, i, j)), pl.BlockSpec((None, tm, tf), lambda j, i: (1, i, j)),
                  pl.BlockSpec((3, tf), lambda j, i: (0, j)), pl.BlockSpec((1, tf), lambda j, i: (0, j))],
        out_specs=pl.BlockSpec((tm, tf), lambda j, i: (i, j)),
        out_shape=jax.ShapeDtypeStruct((S, F), BF16), scratch_shapes=[pltpu.VMEM((8, tf), F32)],
        compiler_params=_cp("parallel", "arbitrary"))(up, up, cw, cb)


def _conv_gate_bwd(up, dz, cw, cb, *, name):
    _, S, F = up.shape
    tf = _tile(F, 1408)
    nf = F // tf
    tm = _tile(S, 512, 8)
    nt = S // tm
    hb = tm // 8

    def body(a_ref, ah_ref, b_ref, dz_ref, w_ref, cb_ref, d_ref, s_ref, edge_ref):
        i = pl.program_id(1)

        @pl.when(i == 0)
        def _():
            edge_ref[...] = jnp.zeros_like(edge_ref)
            s_ref[...] = jnp.zeros_like(s_ref)

        a = a_ref[...].astype(F32)
        halo = jnp.where(i == nt - 1, 0.0, ah_ref[...].astype(F32))
        w = w_ref[...]
        a1 = _shift_down(a, 1, halo)
        a2 = _shift_down(a, 2, halo)
        ac = cb_ref[...] + w[2:3, :] * a + w[1:2, :] * a1 + w[0:1, :] * a2
        sg = jax.nn.sigmoid(ac)
        dzv = dz_ref[...].astype(F32)
        d_ref[1] = (dzv * ac * sg).astype(d_ref.dtype)
        dac = dzv * b_ref[...].astype(F32) * (sg * (1.0 + ac * (1.0 - sg)))
        s_ref[0:1, :] += jnp.sum(dac * a2, axis=0, keepdims=True)
        s_ref[1:2, :] += jnp.sum(dac * a1, axis=0, keepdims=True)
        s_ref[2:3, :] += jnp.sum(dac * a, axis=0, keepdims=True)
        s_ref[3:4, :] += jnp.sum(dac, axis=0, keepdims=True)
        edge = edge_ref[...]
        da = w[2:3, :] * dac + w[1:2, :] * _shift_up(dac, 1, edge) + w[0:1, :] * _shift_up(dac, 2, edge)
        edge_ref[...] = dac[0:8, :]
        d_ref[0] = da.astype(d_ref.dtype)

    tile = lambda hlf: pl.BlockSpec((None, tm, tf), lambda j, i: (hlf, nt - 1 - i, j))
    d_up, sums = pl.pallas_call(
        body, name=name, grid=(nf, nt),
        in_specs=[tile(0),
                  pl.BlockSpec((None, 8, tf), lambda j, i: (0, jnp.maximum((nt - 1 - i) * hb - 1, 0), j)),
                  tile(1), pl.BlockSpec((tm, tf), lambda j, i: (nt - 1 - i, j)),
                  pl.BlockSpec((3, tf), lambda j, i: (0, j)), pl.BlockSpec((1, tf), lambda j, i: (0, j))],
        out_specs=[pl.BlockSpec((2, tm, tf), lambda j, i: (0, nt - 1 - i, j)), pl.BlockSpec((8, tf), lambda j, i: (0, j))],
        out_shape=[jax.ShapeDtypeStruct((2, S, F), BF16), jax.ShapeDtypeStruct((8, F), F32)],
        scratch_shapes=[pltpu.VMEM((8, tf), F32)],
        compiler_params=_cp("parallel", "arbitrary"))(up, up, up, dz, cw, cb)
    return d_up, sums


def _log_sigmoid(x):
    return jnp.minimum(x, 0.0) - jnp.log(1.0 + jnp.exp(-jnp.abs(x)))


def _tri_ones(n, upper):
    r = lax.broadcasted_iota(jnp.int32, (n, n), 0)
    c = lax.broadcasted_iota(jnp.int32, (n, n), 1)
    return jnp.where((c >= r) if upper else (c <= r), 1.0, 0.0).astype(F32)


def _fgate_fwd(fl, bf, *, name):
    S, W = fl.shape
    tb = _tile(S, 256, 8)

    def body(fl_ref, b_ref, o_ref, carry_ref):
        @pl.when(pl.program_id(0) == 0)
        def _():
            carry_ref[...] = jnp.zeros_like(carry_ref)

        lf = _log_sigmoid(fl_ref[...] + b_ref[...])
        cs = jnp.dot(_tri_ones(tb, False), lf, precision=lax.Precision.HIGHEST, preferred_element_type=F32) + carry_ref[0:1, :]
        o_ref[...] = cs
        carry_ref[...] = jnp.broadcast_to(cs[tb - 1:tb, :], carry_ref.shape)

    blk = pl.BlockSpec((tb, W), lambda i: (i, 0))
    return pl.pallas_call(body, name=name, grid=(S // tb,), in_specs=[blk, pl.BlockSpec((1, W), lambda i: (0, 0))], out_specs=blk,
                          out_shape=jax.ShapeDtypeStruct((S, W), F32), scratch_shapes=[pltpu.VMEM((8, W), F32)],
                          compiler_params=_cp("arbitrary"))(fl, bf)


def _fgate_bwd(dF, fl, bf, *, name):
    S, W = fl.shape
    tb = _tile(S, 256, 8)
    nb = S // tb

    def body(d_ref, fl_ref, b_ref, o_ref, s_ref, carry_ref):
        @pl.when(pl.program_id(0) == 0)
        def _():
            carry_ref[...] = jnp.zeros_like(carry_ref)
            s_ref[...] = jnp.zeros_like(s_ref)

        rc = jnp.dot(_tri_ones(tb, True), d_ref[...], precision=lax.Precision.HIGHEST, preferred_element_type=F32) + carry_ref[0:1, :]
        carry_ref[...] = jnp.broadcast_to(rc[0:1, :], carry_ref.shape)
        dfl = rc * jax.nn.sigmoid(-(fl_ref[...] + b_ref[...]))
        o_ref[...] = dfl
        s_ref[...] += jnp.sum(dfl, axis=0, keepdims=True)

    blk = pl.BlockSpec((tb, W), lambda i: (nb - 1 - i, 0))
    vec = pl.BlockSpec((1, W), lambda i: (0, 0))
    return pl.pallas_call(body, name=name, grid=(nb,), in_specs=[blk, blk, vec], out_specs=[blk, vec],
                          out_shape=[jax.ShapeDtypeStruct((S, W), F32), jax.ShapeDtypeStruct((1, W), F32)],
                          scratch_shapes=[pltpu.VMEM((8, W), F32)], compiler_params=_cp("arbitrary"))(dF, fl, bf)


NT_DIMS = (((1,), (1,)), ((), ()))
TN_DIMS = (((0,), (0,)), ((), ()))
HP = HEADS // 2
Q_SCALE = HEAD_DIM ** -0.5


def _lane_pick(x, idx):
    lanes = lax.broadcasted_iota(jnp.int32, x.shape, 1)
    return jnp.sum(jnp.where(lanes == idx, x, 0.0), axis=1, keepdims=True)


def _causal_logits(q, k, fq, fk, row0, col0):
    s = lax.dot_general(q, k, NT_DIMS, preferred_element_type=F32) + fq - fk
    rows = row0 + lax.broadcasted_iota(jnp.int32, s.shape, 0)
    cols = col0 + lax.broadcasted_iota(jnp.int32, s.shape, 1)
    return jnp.where(cols <= rows, s, NEG)


def _attn_fwd(qkv, fcol, frow, *, tq, tk, name):
    S = qkv.shape[0]
    nq, nk = S // tq, S // tk
    last = lambda qi: (qi * tq + tq - 1) // tk

    def body(q_ref, k_ref, v_ref, fc_ref, fr_ref, o_ref, lse_ref, m_sc, l_sc, acc_sc):
        qi, hp, kj = pl.program_id(0), pl.program_id(1), pl.program_id(2)

        @pl.when(kj == 0)
        def _():
            m_sc[...] = jnp.full_like(m_sc, NEG)
            l_sc[...] = jnp.zeros_like(l_sc)
            acc_sc[...] = jnp.zeros_like(acc_sc)

            @pl.when(hp == 0)
            def _():
                lse_ref[...] = jnp.zeros_like(lse_ref)

        @pl.when(kj <= last(qi))
        def _():
            for hh in range(2):
                sl = slice(hh * HEAD_DIM, (hh + 1) * HEAD_DIM)
                q = q_ref[:, sl] * Q_SCALE
                fq = _lane_pick(fc_ref[...], 2 * hp + hh)
                s = _causal_logits(q, k_ref[:, sl], fq, fr_ref[hh:hh + 1, :], qi * tq, kj * tk)
                m_old = m_sc[hh]
                m_new = jnp.maximum(m_old, jnp.max(s, axis=1, keepdims=True))
                alpha = jnp.exp(m_old - m_new)
                p = jnp.exp(s - m_new)
                l_sc[hh] = alpha * l_sc[hh] + jnp.sum(p, axis=1, keepdims=True)
                acc_sc[hh] = alpha * acc_sc[hh] + jnp.dot(p.astype(BF16), v_ref[:, sl], preferred_element_type=F32)
                m_sc[hh] = m_new

        @pl.when(kj == nk - 1)
        def _():
            lanes = lax.broadcasted_iota(jnp.int32, lse_ref.shape, 1)
            cur = lse_ref[...]
            for hh in range(2):
                l = l_sc[hh]
                o_ref[:, hh * HEAD_DIM:(hh + 1) * HEAD_DIM] = (acc_sc[hh] / l).astype(o_ref.dtype)
                cur = jnp.where(lanes == 2 * hp + hh, m_sc[hh] + jnp.log(l), cur)
            lse_ref[...] = cur

    kmap = lambda off: (lambda qi, hp, kj: (jnp.minimum(kj, last(qi)), off + hp))
    return pl.pallas_call(
        body, name=name, grid=(nq, HP, nk),
        in_specs=[pl.BlockSpec((tq, 128), lambda qi, hp, kj: (qi, hp)),
                  pl.BlockSpec((tk, 128), kmap(HP)), pl.BlockSpec((tk, 128), kmap(2 * HP)),
                  pl.BlockSpec((tq, 128), lambda qi, hp, kj: (qi, 0)),
                  pl.BlockSpec((None, 8, tk), lambda qi, hp, kj: (hp, 0, jnp.minimum(kj, last(qi))))],
        out_specs=[pl.BlockSpec((tq, 128), lambda qi, hp, kj: (qi, hp)), pl.BlockSpec((tq, 128), lambda qi, hp, kj: (qi, 0))],
        out_shape=[jax.ShapeDtypeStruct((S, D_MODEL), BF16), jax.ShapeDtypeStruct((S, 128), F32)],
        scratch_shapes=[pltpu.VMEM((2, tq, 1), F32), pltpu.VMEM((2, tq, 1), F32), pltpu.VMEM((2, tq, HEAD_DIM), F32)],
        compiler_params=_cp("parallel", "arbitrary", "arbitrary"))(qkv, qkv, qkv, fcol, frow)


def _attn_bwd_delta(qkv, fcol, frow, do, lse, *, tq, tk, name):
    S = qkv.shape[0]
    nq, nk = S // tq, S // tk
    last = lambda qi: (qi * tq + tq - 1) // tk

    def body(q_ref, k_ref, v_ref, fc_ref, fr_ref, do_ref, lse_ref, dl_ref, acc_sc):
        qi, hp, kj = pl.program_id(0), pl.program_id(1), pl.program_id(2)

        @pl.when(kj == 0)
        def _():
            acc_sc[...] = jnp.zeros_like(acc_sc)

            @pl.when(hp == 0)
            def _():
                dl_ref[...] = jnp.zeros_like(dl_ref)

        @pl.when(kj <= last(qi))
        def _():
            for hh in range(2):
                sl = slice(hh * HEAD_DIM, (hh + 1) * HEAD_DIM)
                q = q_ref[:, sl] * Q_SCALE
                fq = _lane_pick(fc_ref[...], 2 * hp + hh)
                s = _causal_logits(q, k_ref[:, sl], fq, fr_ref[hh:hh + 1, :], qi * tq, kj * tk)
                p = jnp.exp(s - _lane_pick(lse_ref[...], 2 * hp + hh))
                dp = lax.dot_general(do_ref[:, sl], v_ref[:, sl], NT_DIMS, preferred_element_type=F32)
                acc_sc[hh] += jnp.sum(p * dp, axis=1, keepdims=True)

        @pl.when(kj == nk - 1)
        def _():
            lanes = lax.broadcasted_iota(jnp.int32, dl_ref.shape, 1)
            cur = dl_ref[...]
            for hh in range(2):
                cur = jnp.where(lanes == 2 * hp + hh, acc_sc[hh], cur)
            dl_ref[...] = cur

    kmap = lambda off: (lambda qi, hp, kj: (jnp.minimum(kj, last(qi)), off + hp))
    qblk = pl.BlockSpec((tq, 128), lambda qi, hp, kj: (qi, hp))
    qall = pl.BlockSpec((tq, 128), lambda qi, hp, kj: (qi, 0))
    return pl.pallas_call(
        body, name=name, grid=(nq, HP, nk),
        in_specs=[qblk, pl.BlockSpec((tk, 128), kmap(HP)), pl.BlockSpec((tk, 128), kmap(2 * HP)), qall,
                  pl.BlockSpec((None, 8, tk), lambda qi, hp, kj: (hp, 0, jnp.minimum(kj, last(qi)))), qblk, qall],
        out_specs=qall, out_shape=jax.ShapeDtypeStruct((S, 128), F32),
        scratch_shapes=[pltpu.VMEM((2, tq, 1), F32)],
        compiler_params=_cp("parallel", "arbitrary", "arbitrary"))(qkv, qkv, qkv, fcol, frow, do, lse)


def _attn_bwd_dq(qkv, fcol, frow, delta, do, lse, *, tq, tk, name):
    S = qkv.shape[0]
    nq, nk = S // tq, S // tk
    last = lambda qi: (qi * tq + tq - 1) // tk

    def body(q_ref, k_ref, v_ref, fc_ref, fr_ref, dl_ref, do_ref, lse_ref, dq_ref, acc_sc, col_sc):
        qi, hp, kj = pl.program_id(0), pl.program_id(1), pl.program_id(2)

        @pl.when(kj == 0)
        def _():
            acc_sc[...] = jnp.zeros_like(acc_sc)
            for hh in range(2):
                col_sc[hh, 0] = _lane_pick(fc_ref[...], 2 * hp + hh)
                col_sc[hh, 1] = _lane_pick(lse_ref[...], 2 * hp + hh)
                col_sc[hh, 2] = _lane_pick(dl_ref[...], 2 * hp + hh)

        @pl.when(kj <= last(qi))
        def _():
            for hh in range(2):
                sl = slice(hh * HEAD_DIM, (hh + 1) * HEAD_DIM)
                q = q_ref[:, sl] * Q_SCALE
                k = k_ref[:, sl]
                s = _causal_logits(q, k, col_sc[hh, 0], fr_ref[hh:hh + 1, :], qi * tq, kj * tk)
                p = jnp.exp(s - col_sc[hh, 1])
                dp = lax.dot_general(do_ref[:, sl], v_ref[:, sl], NT_DIMS, preferred_element_type=F32)
                ds = p * (dp - col_sc[hh, 2])
                acc_sc[hh] += jnp.dot(ds.astype(BF16), k, preferred_element_type=F32)

        @pl.when(kj == nk - 1)
        def _():
            for hh in range(2):
                dq_ref[:, hh * HEAD_DIM:(hh + 1) * HEAD_DIM] = (acc_sc[hh] * Q_SCALE).astype(dq_ref.dtype)

    kmap = lambda off: (lambda qi, hp, kj: (jnp.minimum(kj, last(qi)), off + hp))
    qblk = pl.BlockSpec((tq, 128), lambda qi, hp, kj: (qi, hp))
    qall = pl.BlockSpec((tq, 128), lambda qi, hp, kj: (qi, 0))
    return pl.pallas_call(
        body, name=name, grid=(nq, HP, nk),
        in_specs=[qblk, pl.BlockSpec((tk, 128), kmap(HP)), pl.BlockSpec((tk, 128), kmap(2 * HP)), qall,
                  pl.BlockSpec((None, 8, tk), lambda qi, hp, kj: (hp, 0, jnp.minimum(kj, last(qi)))), qall, qblk, qall],
        out_specs=qblk, out_shape=jax.ShapeDtypeStruct((S, D_MODEL), BF16),
        scratch_shapes=[pltpu.VMEM((2, tq, HEAD_DIM), F32), pltpu.VMEM((2, 3, tq, 1), F32)],
        compiler_params=_cp("parallel", "parallel", "arbitrary"))(qkv, qkv, qkv, fcol, frow, delta, do, lse)


def _attn_bwd_dkv(qkv, fcol, frow, delta, do, lse, *, tq, tk, name):
    S = qkv.shape[0]
    nq, nk = S // tq, S // tk
    first = lambda kj: (kj * tk) // tq

    def body(q_ref, k_ref, v_ref, fc_ref, fr_ref, dl_ref, do_ref, lse_ref, dk_ref, dv_ref, df_ref, dk_sc, dv_sc, df_sc):
        kj, hp, qi = pl.program_id(0), pl.program_id(1), pl.program_id(2)

        @pl.when(qi == 0)
        def _():
            dk_sc[...] = jnp.zeros_like(dk_sc)
            dv_sc[...] = jnp.zeros_like(dv_sc)
            df_sc[...] = jnp.zeros_like(df_sc)

        @pl.when(qi >= first(kj))
        def _():
            for hh in range(2):
                sl = slice(hh * HEAD_DIM, (hh + 1) * HEAD_DIM)
                q = q_ref[:, sl] * Q_SCALE
                do_h = do_ref[:, sl]
                fq = _lane_pick(fc_ref[...], 2 * hp + hh)
                lse_c = _lane_pick(lse_ref[...], 2 * hp + hh)
                delta = _lane_pick(dl_ref[...], 2 * hp + hh)
                s = _causal_logits(q, k_ref[:, sl], fq, fr_ref[hh:hh + 1, :], qi * tq, kj * tk)
                p = jnp.exp(s - lse_c)
                dp = lax.dot_general(do_h, v_ref[:, sl], NT_DIMS, preferred_element_type=F32)
                ds = p * (dp - delta)
                dv_sc[hh] += lax.dot_general(p.astype(BF16), do_h, TN_DIMS, preferred_element_type=F32)
                dk_sc[hh] += lax.dot_general(ds.astype(BF16), q, TN_DIMS, preferred_element_type=F32)
                df_sc[hh:hh + 1, :] -= jnp.sum(ds, axis=0, keepdims=True)

        @pl.when(qi == nq - 1)
        def _():
            for hh in range(2):
                sl = slice(hh * HEAD_DIM, (hh + 1) * HEAD_DIM)
                dk_ref[:, sl] = dk_sc[hh].astype(dk_ref.dtype)
                dv_ref[:, sl] = dv_sc[hh].astype(dv_ref.dtype)
            df_ref[...] = df_sc[...]

    qeff = lambda kj, qi: jnp.maximum(qi, first(kj))
    qblk = pl.BlockSpec((tq, 128), lambda kj, hp, qi: (qeff(kj, qi), hp))
    qall = pl.BlockSpec((tq, 128), lambda kj, hp, qi: (qeff(kj, qi), 0))
    kblk = lambda off: pl.BlockSpec((tk, 128), lambda kj, hp, qi: (kj, off + hp))
    oblk = pl.BlockSpec((tk, 128), lambda kj, hp, qi: (kj, hp))
    fblk = pl.BlockSpec((None, 8, tk), lambda kj, hp, qi: (hp, 0, kj))
    return pl.pallas_call(
        body, name=name, grid=(nk, HP, nq),
        in_specs=[qblk, kblk(HP), kblk(2 * HP), qall, fblk, qall, qblk, qall],
        out_specs=[oblk, oblk, fblk],
        out_shape=[jax.ShapeDtypeStruct((S, D_MODEL), BF16), jax.ShapeDtypeStruct((S, D_MODEL), BF16),
                   jax.ShapeDtypeStruct((HP, 8, S), F32)],
        scratch_shapes=[pltpu.VMEM((2, tk, HEAD_DIM), F32), pltpu.VMEM((2, tk, HEAD_DIM), F32), pltpu.VMEM((8, tk), F32)],
        compiler_params=_cp("parallel", "parallel", "arbitrary"))(qkv, qkv, qkv, fcol, frow, delta, do, lse)


def _head_mask(x, hh):
    lanes = lax.broadcasted_iota(jnp.int32, x.shape, 1)
    return jnp.where((lanes >= hh * HEAD_DIM) & (lanes < (hh + 1) * HEAD_DIM), x, jnp.zeros_like(x))


def _diag_mask(s):
    rows = lax.broadcasted_iota(jnp.int32, s.shape, 0)
    cols = lax.broadcasted_iota(jnp.int32, s.shape, 1)
    return jnp.where(cols <= rows, s, NEG)


def _lanes_from(cols, shape):
    lanes = lax.broadcasted_iota(jnp.int32, shape, 1)
    out = jnp.zeros(shape, F32)
    for i, cvec in enumerate(cols):
        out = jnp.where(lanes == i, cvec, out)
    return out


def _fox_fwd(qkv, fcol, frow4, *, tb, name):
    S = qkv.shape[0]
    nb = S // tb

    def body(q_ref, k_ref, v_ref, fc_ref, fr_ref, o_ref, st_ref, m_sc, l_sc, acc_sc):
        hp, qi = pl.program_id(0), pl.program_id(1)
        q2 = q_ref[...] * Q_SCALE
        qh = [_head_mask(q2, hh) for hh in range(2)]
        fq = [_lane_pick(fc_ref[...], 2 * hp + hh) for hh in range(2)]
        m_sc[...] = jnp.full_like(m_sc, NEG)
        l_sc[...] = jnp.zeros_like(l_sc)
        acc_sc[...] = jnp.zeros_like(acc_sc)

        def block(kj, masked):
            k0 = pl.multiple_of(kj * tb, tb)
            k2, v2 = k_ref[pl.ds(k0, tb), :], v_ref[pl.ds(k0, tb), :]
            for hh in range(2):
                s = lax.dot_general(qh[hh], k2, NT_DIMS, preferred_element_type=F32) + (fq[hh] - fr_ref[kj, hh:hh + 1, :])
                if masked:
                    s = _diag_mask(s)
                m_old = m_sc[hh]
                m_new = jnp.maximum(m_old, jnp.max(s, axis=1, keepdims=True))
                alpha = jnp.exp(m_old - m_new)
                p = jnp.exp(s - m_new)
                l_sc[hh] = alpha * l_sc[hh] + jnp.sum(p, axis=1, keepdims=True)
                acc_sc[hh] = alpha * acc_sc[hh] + jnp.dot(p.astype(BF16), v2, preferred_element_type=F32)
                m_sc[hh] = m_new

        def off_diagonal(kj, c):
            block(kj, False)
            return c

        lax.fori_loop(0, qi, off_diagonal, 0)
        block(qi, True)
        lanes = lax.broadcasted_iota(jnp.int32, (tb, 128), 1)
        o_ref[...] = jnp.where(lanes < HEAD_DIM, acc_sc[0] / l_sc[0], acc_sc[1] / l_sc[1]).astype(o_ref.dtype)
        st_ref[...] = _lanes_from([m_sc[hh] + jnp.log(l_sc[hh]) for hh in range(2)], (tb, 128))

    whole = lambda off: pl.BlockSpec((S, 128), lambda hp, qi: (0, off + hp))
    return pl.pallas_call(
        body, name=name, grid=(HP, nb),
        in_specs=[pl.BlockSpec((tb, 128), lambda hp, qi: (qi, hp)), whole(HP), whole(2 * HP),
                  pl.BlockSpec((tb, 128), lambda hp, qi: (qi, 0)),
                  pl.BlockSpec((None, nb, 8, tb), lambda hp, qi: (hp, 0, 0, 0))],
        out_specs=[pl.BlockSpec((tb, 128), lambda hp, qi: (qi, hp)), pl.BlockSpec((None, tb, 128), lambda hp, qi: (hp, qi, 0))],
        out_shape=[jax.ShapeDtypeStruct((S, D_MODEL), BF16), jax.ShapeDtypeStruct((HP, S, 128), F32)],
        scratch_shapes=[pltpu.VMEM((2, tb, 1), F32), pltpu.VMEM((2, tb, 1), F32), pltpu.VMEM((2, tb, 128), F32)],
        compiler_params=_cp("parallel", "parallel"))(qkv, qkv, qkv, fcol, frow4)


def _fox_stats(qkv, fcol, frow4, do, lse, *, tb, name):
    S = qkv.shape[0]
    nb = S // tb

    def body(q_ref, k_ref, v_ref, fc_ref, fr_ref, do_ref, st_ref, rs_ref, acc_sc):
        hp, qi = pl.program_id(0), pl.program_id(1)
        q2 = q_ref[...] * Q_SCALE
        do2 = do_ref[...]
        lse = st_ref[...]
        qh = [_head_mask(q2, hh) for hh in range(2)]
        doh = [_head_mask(do2, hh) for hh in range(2)]
        a = [_lane_pick(fc_ref[...], 2 * hp + hh) - lse[:, hh:hh + 1] for hh in range(2)]
        acc_sc[...] = jnp.zeros_like(acc_sc)

        def block(kj, masked):
            k0 = pl.multiple_of(kj * tb, tb)
            k2, v2 = k_ref[pl.ds(k0, tb), :], v_ref[pl.ds(k0, tb), :]
            for hh in range(2):
                s = lax.dot_general(qh[hh], k2, NT_DIMS, preferred_element_type=F32) + (a[hh] - fr_ref[kj, hh:hh + 1, :])
                if masked:
                    s = _diag_mask(s)
                dp = lax.dot_general(doh[hh], v2, NT_DIMS, preferred_element_type=F32)
                acc_sc[hh] += jnp.sum(jnp.exp(s) * dp, axis=1, keepdims=True)

        def off_diagonal(kj, c):
            block(kj, False)
            return c

        lax.fori_loop(0, qi, off_diagonal, 0)
        block(qi, True)
        rs_ref[...] = _lanes_from([a[0], a[1], acc_sc[0], acc_sc[1]], (tb, 128))

    whole = lambda off: pl.BlockSpec((S, 128), lambda hp, qi: (0, off + hp))
    qblk = pl.BlockSpec((tb, 128), lambda hp, qi: (qi, hp))
    stat = pl.BlockSpec((None, tb, 128), lambda hp, qi: (hp, qi, 0))
    return pl.pallas_call(
        body, name=name, grid=(HP, nb),
        in_specs=[qblk, whole(HP), whole(2 * HP), pl.BlockSpec((tb, 128), lambda hp, qi: (qi, 0)),
                  pl.BlockSpec((None, nb, 8, tb), lambda hp, qi: (hp, 0, 0, 0)), qblk, stat],
        out_specs=stat, out_shape=jax.ShapeDtypeStruct((HP, S, 128), F32),
        scratch_shapes=[pltpu.VMEM((2, tb, 1), F32)],
        compiler_params=_cp("parallel", "parallel"))(qkv, qkv, qkv, fcol, frow4, do, lse)


def _fox_bwd(qkv, frow4, do, rs, *, tb, name):
    S = qkv.shape[0]
    nb = S // tb

    def body(q_ref, k_ref, v_ref, fr_ref, do_ref, rs_ref, dq_ref, dk_ref, dv_ref, df_ref, dq_sc, dk_sc, dv_sc, df_sc):
        hp, kj = pl.program_id(0), pl.program_id(1)

        @pl.when(kj == 0)
        def _():
            dq_sc[...] = jnp.zeros_like(dq_sc)

        dk_sc[...] = jnp.zeros_like(dk_sc)
        dv_sc[...] = jnp.zeros_like(dv_sc)
        df_sc[...] = jnp.zeros_like(df_sc)
        k2, v2 = k_ref[...], v_ref[...]
        kh = [_head_mask(k2, hh) for hh in range(2)]

        def block(qi, masked):
            q0 = pl.multiple_of(qi * tb, tb)
            q2 = q_ref[pl.ds(q0, tb), :] * Q_SCALE
            do2 = do_ref[pl.ds(q0, tb), :]
            st = rs_ref[pl.ds(q0, tb), :]
            dq_new = dq_sc[pl.ds(q0, tb), :]
            for hh in range(2):
                qh, doh = _head_mask(q2, hh), _head_mask(do2, hh)
                s = lax.dot_general(qh, k2, NT_DIMS, preferred_element_type=F32) + (st[:, hh:hh + 1] - fr_ref[hh:hh + 1, :])
                if masked:
                    s = _diag_mask(s)
                p = jnp.exp(s)
                dp = lax.dot_general(doh, v2, NT_DIMS, preferred_element_type=F32)
                ds = p * (dp - st[:, 2 + hh:3 + hh])
                dsb = ds.astype(BF16)
                dv_sc[...] += lax.dot_general(p.astype(BF16), doh, TN_DIMS, preferred_element_type=F32)
                dk_sc[...] += lax.dot_general(dsb, qh, TN_DIMS, preferred_element_type=F32)
                dq_new = dq_new + jnp.dot(dsb, kh[hh], preferred_element_type=F32)
                df_sc[hh:hh + 1, :] -= jnp.sum(ds, axis=0, keepdims=True)
            dq_sc[pl.ds(q0, tb), :] = dq_new

        def off_diagonal(qi, c):
            block(qi, False)
            return c

        block(kj, True)
        lax.fori_loop(kj + 1, nb, off_diagonal, 0)
        dk_ref[...] = dk_sc[...].astype(dk_ref.dtype)
        dv_ref[...] = dv_sc[...].astype(dv_ref.dtype)
        df_ref[...] = df_sc[...]

        @pl.when(kj == nb - 1)
        def _():
            dq_ref[...] = (dq_sc[...] * Q_SCALE).astype(dq_ref.dtype)

    whole = pl.BlockSpec((S, 128), lambda hp, kj: (0, hp))
    kblk = lambda off: pl.BlockSpec((tb, 128), lambda hp, kj: (kj, off + hp))
    return pl.pallas_call(
        body, name=name, grid=(HP, nb),
        in_specs=[whole, kblk(HP), kblk(2 * HP), pl.BlockSpec((None, None, 8, tb), lambda hp, kj: (hp, kj, 0, 0)), whole,
                  pl.BlockSpec((None, S, 128), lambda hp, kj: (hp, 0, 0))],
        out_specs=[whole, kblk(0), kblk(0), pl.BlockSpec((None, 8, tb), lambda hp, kj: (hp, 0, kj))],
        out_shape=[jax.ShapeDtypeStruct((S, D_MODEL), BF16)] * 3 + [jax.ShapeDtypeStruct((HP, 8, S), F32)],
        scratch_shapes=[pltpu.VMEM((S, 128), F32), pltpu.VMEM((tb, 128), F32), pltpu.VMEM((tb, 128), F32), pltpu.VMEM((8, tb), F32)],
        compiler_params=_cp("parallel", "arbitrary"))(qkv, qkv, qkv, frow4, do, rs)


AUG_ROWS = 16


def _qa_pair(qt, aug):
    fill = jnp.zeros((HEAD_DIM - AUG_ROWS, qt.shape[1]), qt.dtype)
    return [jnp.concatenate([qt[0:HEAD_DIM], aug[0], fill], axis=0), jnp.concatenate([aug[1], fill, qt[HEAD_DIM:]], axis=0)]


def _rows_of_head(xt, hh):
    rows = lax.broadcasted_iota(jnp.int32, xt.shape, 0)
    return jnp.where((rows >= hh * HEAD_DIM) & (rows < (hh + 1) * HEAD_DIM), xt, jnp.zeros_like(xt))


def _put_rows(ref, rows):
    for i, r in enumerate(rows):
        ref[i:i + 1, :] = r
    ref[len(rows):, :] = jnp.zeros((ref.shape[0] - len(rows), ref.shape[1]), ref.dtype)


def _diag_mask_t(s):
    rows = lax.broadcasted_iota(jnp.int32, s.shape, 0)
    cols = lax.broadcasted_iota(jnp.int32, s.shape, 1)
    return jnp.where(rows <= cols, s, NEG)


def _foxt_fwd(qkvT, ka, vtb, augq, *, tb, name):
    S = qkvT.shape[1]
    nb = S // tb

    def body(q_ref, ka_ref, v_ref, aq_ref, o_ref, lse_ref, m_sc, l_sc, acc_sc):
        qi = pl.program_id(1)
        qa = _qa_pair(q_ref[...], aq_ref[...])
        m_sc[...] = jnp.full_like(m_sc, NEG)
        l_sc[...] = jnp.zeros_like(l_sc)
        acc_sc[...] = jnp.zeros_like(acc_sc)

        def block(kj, masked):
            k0 = pl.multiple_of(kj * tb, tb)
            vt = v_ref[kj]
            for hh in range(2):
                s = jnp.dot(ka_ref[hh, pl.ds(k0, tb), :], qa[hh], preferred_element_type=F32)
                if masked:
                    s = _diag_mask_t(s)
                m_old = m_sc[hh]
                m_new = jnp.maximum(m_old, jnp.max(s, axis=0, keepdims=True))
                alpha = jnp.exp(m_old - m_new)
                p = jnp.exp(s - m_new)
                l_sc[hh] = alpha * l_sc[hh] + jnp.sum(p, axis=0, keepdims=True)
                acc_sc[hh] = alpha * acc_sc[hh] + jnp.dot(vt, p.astype(BF16), preferred_element_type=F32)
                m_sc[hh] = m_new

        def off_diagonal(kj, c):
            block(kj, False)
            return c

        lax.fori_loop(0, qi, off_diagonal, 0)
        block(qi, True)
        o_ref[0:HEAD_DIM, :] = (acc_sc[0, 0:HEAD_DIM, :] / l_sc[0]).astype(o_ref.dtype)
        o_ref[HEAD_DIM:, :] = (acc_sc[1, HEAD_DIM:, :] / l_sc[1]).astype(o_ref.dtype)
        _put_rows(lse_ref, [m_sc[hh] + jnp.log(l_sc[hh]) for hh in range(2)])

    return pl.pallas_call(
        body, name=name, grid=(HP, nb),
        in_specs=[pl.BlockSpec((128, tb), lambda hp, qi: (hp, qi)),
                  pl.BlockSpec((None, 2, S, 128), lambda hp, qi: (hp, 0, 0, 0)),
                  pl.BlockSpec((None, nb, 128, tb), lambda hp, qi: (hp, 0, 0, 0)),
                  pl.BlockSpec((None, 2, AUG_ROWS, tb), lambda hp, qi: (hp, 0, 0, qi))],
        out_specs=[pl.BlockSpec((128, tb), lambda hp, qi: (hp, qi)), pl.BlockSpec((None, 8, tb), lambda hp, qi: (hp, 0, qi))],
        out_shape=[jax.ShapeDtypeStruct((D_MODEL, S), BF16), jax.ShapeDtypeStruct((HP, 8, S), F32)],
        scratch_shapes=[pltpu.VMEM((2, 1, tb), F32), pltpu.VMEM((2, 1, tb), F32), pltpu.VMEM((2, 128, tb), F32)],
        compiler_params=_cp("parallel", "parallel"))(qkvT, ka, vtb, augq)


def _foxt_stats(qkvT, qkv, ka, augq, doT, lse, *, tb, name):
    S = qkvT.shape[1]
    nb = S // tb

    def body(q_ref, ka_ref, v_ref, aq_ref, do_ref, ls_ref, dl_ref, acc_sc):
        qi = pl.program_id(1)
        qa = _qa_pair(q_ref[...], aq_ref[...])
        doth = [_rows_of_head(do_ref[...], hh) for hh in range(2)]
        lse_row = [ls_ref[hh:hh + 1, :] for hh in range(2)]
        acc_sc[...] = jnp.zeros_like(acc_sc)

        def block(kj, masked):
            k0 = pl.multiple_of(kj * tb, tb)
            v2 = v_ref[pl.ds(k0, tb), :]
            for hh in range(2):
                s = jnp.dot(ka_ref[hh, pl.ds(k0, tb), :], qa[hh], preferred_element_type=F32)
                if masked:
                    s = _diag_mask_t(s)
                dp = jnp.dot(v2, doth[hh], preferred_element_type=F32)
                acc_sc[hh] += jnp.sum(jnp.exp(s - lse_row[hh]) * dp, axis=0, keepdims=True)

        def off_diagonal(kj, c):
            block(kj, False)
            return c

        lax.fori_loop(0, qi, off_diagonal, 0)
        block(qi, True)
        _put_rows(dl_ref, [acc_sc[0], acc_sc[1]])

    qblk = pl.BlockSpec((128, tb), lambda hp, qi: (hp, qi))
    return pl.pallas_call(
        body, name=name, grid=(HP, nb),
        in_specs=[qblk, pl.BlockSpec((None, 2, S, 128), lambda hp, qi: (hp, 0, 0, 0)),
                  pl.BlockSpec((S, 128), lambda hp, qi: (0, 2 * HP + hp)),
                  pl.BlockSpec((None, 2, AUG_ROWS, tb), lambda hp, qi: (hp, 0, 0, qi)), qblk,
                  pl.BlockSpec((None, 8, tb), lambda hp, qi: (hp, 0, qi))],
        out_specs=pl.BlockSpec((None, 8, tb), lambda hp, qi: (hp, 0, qi)),
        out_shape=jax.ShapeDtypeStruct((HP, 8, S), F32),
        scratch_shapes=[pltpu.VMEM((2, 1, tb), F32)],
        compiler_params=_cp("parallel", "parallel"))(qkvT, ka, qkv, augq, doT, lse)


def _foxt_bwd(qkvT, qkv, ka, qtb, dotb, do, augqb, lseb, deltab, *, tb, name):
    S = qkv.shape[0]
    nb = S // tb

    def body(ka_ref, v_ref, kt_ref, qt_ref, dot_ref, qn_ref, dn_ref, aq_ref, ls_ref, dl_ref,
             dq_ref, dk_ref, dv_ref, df_ref, dq_sc, dk_sc, dv_sc, ds_sc):
        kj = pl.program_id(1)

        @pl.when(kj == 0)
        def _():
            dq_sc[...] = jnp.zeros_like(dq_sc)

        dk_sc[...] = jnp.zeros_like(dk_sc)
        dv_sc[...] = jnp.zeros_like(dv_sc)
        ds_sc[...] = jnp.zeros_like(ds_sc)
        v2 = v_ref[...]
        kth = [_rows_of_head(kt_ref[...], hh) for hh in range(2)]

        def block(qi, masked):
            q0 = pl.multiple_of(qi * tb, tb)
            qa = _qa_pair(qt_ref[qi], aq_ref[qi])
            dot, dl, ls = dot_ref[qi], dl_ref[qi], ls_ref[qi]
            qn, dn = qn_ref[pl.ds(q0, tb), :], dn_ref[pl.ds(q0, tb), :]
            dq_new = dq_sc[qi]
            for hh in range(2):
                s = jnp.dot(ka_ref[hh], qa[hh], preferred_element_type=F32)
                if masked:
                    s = _diag_mask_t(s)
                p = jnp.exp(s - ls[hh:hh + 1, :])
                dp = jnp.dot(v2, _rows_of_head(dot, hh), preferred_element_type=F32)
                ds = p * (dp - dl[hh:hh + 1, :])
                dsb = ds.astype(BF16)
                dv_sc[...] += jnp.dot(p.astype(BF16), _head_mask(dn, hh), preferred_element_type=F32)
                dk_sc[...] += jnp.dot(dsb, _head_mask(qn, hh), preferred_element_type=F32)
                dq_new = dq_new + jnp.dot(kth[hh], dsb, preferred_element_type=F32)
                part = ds[:, 0:128]
                for j in range(1, tb // 128):
                    part = part + ds[:, 128 * j:128 * (j + 1)]
                ds_sc[hh] += part
            dq_sc[qi] = dq_new

        def off_diagonal(i, c):
            block(kj + 1 + i, False)
            return c

        block(kj, True)
        lax.fori_loop(0, nb - 1 - kj, off_diagonal, 0)
        dk_ref[...] = dk_sc[...].astype(dk_ref.dtype)
        dv_ref[...] = dv_sc[...].astype(dv_ref.dtype)
        df_ref[...] = _lanes_from([-jnp.sum(ds_sc[hh], axis=1, keepdims=True) for hh in range(2)], (tb, 128))

        @pl.when(kj == nb - 1)
        def _():
            dq_ref[...] = (dq_sc[...] * Q_SCALE).astype(dq_ref.dtype)

    resident = lambda rows: pl.BlockSpec((None, nb) + rows, lambda hp, kj: (hp,) + (0,) * (len(rows) + 1))
    whole = pl.BlockSpec((S, 128), lambda hp, kj: (0, hp))
    kblk = lambda off: pl.BlockSpec((tb, 128), lambda hp, kj: (kj, off + hp))
    return pl.pallas_call(
        body, name=name, grid=(HP, nb),
        in_specs=[pl.BlockSpec((None, 2, tb, 128), lambda hp, kj: (hp, 0, kj, 0)), kblk(2 * HP),
                  pl.BlockSpec((128, tb), lambda hp, kj: (HP + hp, kj)),
                  resident((128, tb)), resident((128, tb)), whole, whole, resident((2, AUG_ROWS, tb)), resident((8, tb)),
                  resident((8, tb))],
        out_specs=[resident((128, tb)), kblk(0), kblk(0), pl.BlockSpec((None, tb, 128), lambda hp, kj: (hp, kj, 0))],
        out_shape=[jax.ShapeDtypeStruct((HP, nb, 128, tb), BF16), jax.ShapeDtypeStruct((S, D_MODEL), BF16),
                   jax.ShapeDtypeStruct((S, D_MODEL), BF16), jax.ShapeDtypeStruct((HP, S, 128), F32)],
        scratch_shapes=[pltpu.VMEM((nb, 128, tb), F32), pltpu.VMEM((tb, 128), F32), pltpu.VMEM((tb, 128), F32),
                        pltpu.VMEM((2, tb, 128), F32)],
        compiler_params=_cp("parallel", "arbitrary"))(ka, qkv, qkvT, qtb, dotb, qkv, do, augqb, lseb, deltab)


def _split3(x):
    rnd = lambda v: lax.reduce_precision(v, exponent_bits=8, mantissa_bits=7)
    hi = rnd(x)
    mid = rnd(x - hi)
    lo = rnd(x - hi - mid)
    return hi.astype(BF16), mid.astype(BF16), lo.astype(BF16)


def _blocked(xt, nb, tb):
    lead = xt.shape[:-1]
    x = xt.reshape(lead + (nb, tb))
    return jnp.moveaxis(x, -2, 1)


def _aug_rows(bias):
    ones = jnp.ones(bias.shape, BF16)
    zeros = jnp.zeros(bias.shape, BF16)
    return jnp.stack(list(_split3(bias)) + [ones] * 3 + [zeros] * (AUG_ROWS - 6), axis=2)


def _key_operand(k, f_heads):
    S = k.shape[0]
    k4 = k.reshape(S, HP, 2, HEAD_DIM).transpose(1, 2, 0, 3)
    ones = jnp.ones(f_heads.shape, BF16)
    zeros = jnp.zeros(f_heads.shape + (HEAD_DIM - 6,), BF16)
    aug = jnp.concatenate([jnp.stack([ones] * 3 + list(_split3(-f_heads)), axis=-1), zeros], axis=-1)
    return jnp.stack([jnp.concatenate([k4[:, 0], aug[:, 0]], axis=-1), jnp.concatenate([aug[:, 1], k4[:, 1]], axis=-1)], axis=1)


HALF = S5_SUB * S5_P
NCB = 2 * HALF // 128
RE, IM = slice(0, NCB // 2), slice(NCB // 2, NCB)


def _gelu(x):
    return 0.5 * x * (1.0 + jnp.tanh(GELU_K * (x + 0.044715 * x * x * x)))


def _gelu_grad(x):
    t = jnp.tanh(GELU_K * (x + 0.044715 * x * x * x))
    return 0.5 * (1.0 + t) + 0.5 * x * (1.0 - t * t) * GELU_K * (1.0 + 3.0 * 0.044715 * x * x)


def _s5_put(buf, s, val, lt):
    for cb in range(NCB):
        buf[cb, pl.ds(s, lt, stride=S5_SUB), :] = val[:, 128 * cb:128 * (cb + 1)]


def _s5_get(buf, s, lt):
    return jnp.concatenate([buf[cb, pl.ds(s, lt, stride=S5_SUB), :] for cb in range(NCB)], axis=1)


def _s5_project_in(u_ref, wb_ref, buf, lt):
    for s in range(S5_SUB):
        _s5_put(buf, s, jnp.dot(u_ref[:, 128 * s:128 * (s + 1)], wb_ref[s], preferred_element_type=F32), lt)


def _s5_scan(buf, lam_ref, h0, lt):
    a_re, a_im = lam_ref[RE], lam_ref[IM]

    def step(t, carry):
        hr, hi = carry
        r0 = pl.multiple_of(t * S5_SUB, S5_SUB)
        nr = a_re * hr - a_im * hi + buf[RE, pl.ds(r0, S5_SUB), :]
        ni = a_re * hi + a_im * hr + buf[IM, pl.ds(r0, S5_SUB), :]
        buf[RE, pl.ds(r0, S5_SUB), :] = nr
        buf[IM, pl.ds(r0, S5_SUB), :] = ni
        return nr, ni

    return lax.fori_loop(0, lt, step, (h0[RE], h0[IM]), unroll=4)


def _s5_fwd(u, wb, wc, lam, dskip, *, lt, name):
    S, D = u.shape
    nt = S // lt

    def body(u_ref, wb_ref, wc_ref, lam_ref, d_ref, yp_ref, yg_ref, st_ref, buf, h_sc):
        @pl.when(pl.program_id(0) == 0)
        def _():
            h_sc[...] = jnp.zeros_like(h_sc)

        st_ref[...] = h_sc[...]
        _s5_project_in(u_ref, wb_ref, buf, lt)
        hr, hi = _s5_scan(buf, lam_ref, h_sc[...], lt)
        h_sc[RE] = hr
        h_sc[IM] = hi
        for s in range(S5_SUB):
            cols = slice(128 * s, 128 * (s + 1))
            hs = _s5_get(buf, s, lt).astype(BF16)
            yp = jnp.dot(hs, wc_ref[s], preferred_element_type=F32) + d_ref[:, cols] * u_ref[:, cols].astype(F32)
            yp_ref[:, cols] = yp.astype(yp_ref.dtype)
            yg_ref[:, cols] = _gelu(yp).astype(yg_ref.dtype)

    blk = pl.BlockSpec((lt, D), lambda i: (i, 0))
    full = lambda shp: pl.BlockSpec(shp, lambda i: (0,) * len(shp))
    state = (NCB, S5_SUB, 128)
    return pl.pallas_call(
        body, name=name, grid=(nt,),
        in_specs=[blk, full(wb.shape), full(wc.shape), full(lam.shape), full(dskip.shape)],
        out_specs=[blk, blk, pl.BlockSpec((None,) + state, lambda i: (i, 0, 0, 0))],
        out_shape=[jax.ShapeDtypeStruct((S, D), BF16), jax.ShapeDtypeStruct((S, D), BF16), jax.ShapeDtypeStruct((nt,) + state, F32)],
        scratch_shapes=[pltpu.VMEM((NCB, lt * S5_SUB, 128), F32), pltpu.VMEM(state, F32)],
        compiler_params=_cp("arbitrary"))(u, wb, wc, lam, dskip)


def _s5_bwd(u, dyp, st, wb, wc, lam, dskip, *, lt, name):
    S, D = u.shape
    nt = S // lt

    def body(u_ref, dy_ref, st_ref, wb_ref, wc_ref, lam_ref, d_ref, du_ref, dwb_ref, dwc_ref, dlam_ref, hbuf, gbuf, g_sc):
        @pl.when(pl.program_id(0) == 0)
        def _():
            g_sc[...] = jnp.zeros_like(g_sc)
            dwb_ref[...] = jnp.zeros_like(dwb_ref)
            dwc_ref[...] = jnp.zeros_like(dwc_ref)
            dlam_ref[...] = jnp.zeros_like(dlam_ref)

        _s5_project_in(u_ref, wb_ref, hbuf, lt)
        _s5_scan(hbuf, lam_ref, st_ref[...], lt)
        for s in range(S5_SUB):
            dys = dy_ref[:, 128 * s:128 * (s + 1)]
            _s5_put(gbuf, s, lax.dot_general(dys, wc_ref[s], NT_DIMS, preferred_element_type=F32), lt)
        a_re, a_im = lam_ref[RE], lam_ref[IM]

        def one(t, carry, hp_re, hp_im):
            gr, gi, dar, dai = carry
            r0 = pl.multiple_of(t * S5_SUB, S5_SUB)
            nr = gbuf[RE, pl.ds(r0, S5_SUB), :] + a_re * gr + a_im * gi
            ni = gbuf[IM, pl.ds(r0, S5_SUB), :] + a_re * gi - a_im * gr
            gbuf[RE, pl.ds(r0, S5_SUB), :] = nr
            gbuf[IM, pl.ds(r0, S5_SUB), :] = ni
            return nr, ni, dar + nr * hp_re + ni * hp_im, dai + ni * hp_re - nr * hp_im

        def step(k, carry):
            t = lt - 1 - k
            p0 = pl.multiple_of((t - 1) * S5_SUB, S5_SUB)
            return one(t, carry, hbuf[RE, pl.ds(p0, S5_SUB), :], hbuf[IM, pl.ds(p0, S5_SUB), :])

        init = (g_sc[RE], g_sc[IM], dlam_ref[RE], dlam_ref[IM])
        carry = lax.fori_loop(0, lt - 1, step, init, unroll=2)
        gr, gi, dar, dai = one(0, carry, st_ref[RE], st_ref[IM])
        g_sc[RE] = gr
        g_sc[IM] = gi
        dlam_ref[RE] = dar
        dlam_ref[IM] = dai
        for s in range(S5_SUB):
            cols = slice(128 * s, 128 * (s + 1))
            gs = _s5_get(gbuf, s, lt).astype(BF16)
            hs = _s5_get(hbuf, s, lt).astype(BF16)
            us, dys = u_ref[:, cols], dy_ref[:, cols]
            du = lax.dot_general(gs, wb_ref[s], NT_DIMS, preferred_element_type=F32) + d_ref[:, cols] * dys.astype(F32)
            du_ref[:, cols] = du.astype(du_ref.dtype)
            dwb_ref[s] += lax.dot_general(us, gs, TN_DIMS, preferred_element_type=F32)
            dwc_ref[s] += lax.dot_general(hs, dys, TN_DIMS, preferred_element_type=F32)

    blk = pl.BlockSpec((lt, D), lambda i: (nt - 1 - i, 0))
    full = lambda shp: pl.BlockSpec(shp, lambda i: (0,) * len(shp))
    state = (NCB, S5_SUB, 128)
    words = pltpu.VMEM((NCB, lt * S5_SUB, 128), F32)
    return pl.pallas_call(
        body, name=name, grid=(nt,),
        in_specs=[blk, blk, pl.BlockSpec((None,) + state, lambda i: (nt - 1 - i, 0, 0, 0)),
                  full(wb.shape), full(wc.shape), full(lam.shape), full(dskip.shape)],
        out_specs=[blk, full(wb.shape), full(wc.shape), full(lam.shape)],
        out_shape=[jax.ShapeDtypeStruct((S, D), BF16), jax.ShapeDtypeStruct(wb.shape, F32), jax.ShapeDtypeStruct(wc.shape, F32),
                   jax.ShapeDtypeStruct(lam.shape, F32)],
        scratch_shapes=[words, words, pltpu.VMEM(state, F32)],
        compiler_params=_cp("arbitrary"))(u, dyp, st, wb, wc, lam, dskip)


def _s5_disc(lam_re, lam_im, log_dt, b_re, b_im):
    dt = jnp.exp(log_dt)[:, None]
    mag = jnp.exp(lam_re * dt)
    lb_re, lb_im = mag * jnp.cos(lam_im * dt), mag * jnp.sin(lam_im * dt)
    num_re = lb_re - 1.0
    den = lam_re * lam_re + lam_im * lam_im
    k_re = (num_re * lam_re + lb_im * lam_im) / den
    k_im = (lb_im * lam_re - num_re * lam_im) / den
    bb_re = k_re[..., None] * b_re - k_im[..., None] * b_im
    bb_im = k_re[..., None] * b_im + k_im[..., None] * b_re
    return lb_re, lb_im, bb_re, bb_im


def _s5_pack(lb_re, lb_im, bb_re, bb_im, c_re, c_im):
    eye = jnp.eye(S5_SUB, dtype=F32)
    tb = lambda x: jnp.einsum("sgcp,gh->sgchp", x.reshape(S5_SUB, S5_SUB, S5_P, S5_C).transpose(0, 1, 3, 2), eye).reshape(S5_SUB, 128, HALF)
    tc = lambda x: jnp.einsum("sgpc,gh->sgphc", x.reshape(S5_SUB, S5_SUB, S5_C, S5_P).transpose(0, 1, 3, 2), eye).reshape(S5_SUB, HALF, 128)
    wb = jnp.concatenate([tb(bb_re), tb(bb_im)], axis=2).astype(BF16)
    wc = jnp.concatenate([tc(c_re), -tc(c_im)], axis=1).astype(BF16)
    lam = jnp.concatenate([lb_re.reshape(S5_SUB, HALF), lb_im.reshape(S5_SUB, HALF)], axis=1)
    return wb, wc, lam.reshape(S5_SUB, NCB, 128).transpose(1, 0, 2)


def _s5_unpack(dwb, dwc, dlam):
    db = jnp.einsum("sgcrgp->rsgpc", dwb.reshape(S5_SUB, S5_SUB, S5_C, 2, S5_SUB, S5_P)).reshape(2, S5_G, S5_P, S5_C)
    dc = jnp.einsum("srgpgc->rsgcp", dwc.reshape(S5_SUB, 2, S5_SUB, S5_P, S5_SUB, S5_C)).reshape(2, S5_G, S5_C, S5_P)
    dlam = dlam.transpose(1, 0, 2).reshape(S5_SUB, 2 * HALF)
    return (dlam[:, :HALF].reshape(S5_G, S5_P), dlam[:, HALF:].reshape(S5_G, S5_P), db[0], db[1], dc[0], -dc[1])


def _glu_fwd(yg, gl, *, name):
    S, D = yg.shape
    tm = _tile(S, 512, 8)
    blk = pl.BlockSpec((tm, D), lambda i: (i, 0))

    def body(y_ref, g_ref, o_ref):
        o_ref[...] = (y_ref[...].astype(F32) * jax.nn.sigmoid(g_ref[...].astype(F32))).astype(o_ref.dtype)

    return pl.pallas_call(body, name=name, grid=(S // tm,), in_specs=[blk, blk], out_specs=blk,
                          out_shape=jax.ShapeDtypeStruct((S, D), BF16), compiler_params=_cp("parallel"))(yg, gl)


def _glu_bwd(dy2, yg, gl, *, name):
    S, D = yg.shape
    tm = _tile(S, 512, 8)
    blk = pl.BlockSpec((tm, D), lambda i: (i, 0))

    def body(d_ref, y_ref, g_ref, da_ref, dg_ref):
        d = d_ref[...].astype(F32)
        sg = jax.nn.sigmoid(g_ref[...].astype(F32))
        da_ref[...] = d * sg
        dg_ref[...] = (d * y_ref[...].astype(F32) * sg * (1.0 - sg)).astype(dg_ref.dtype)

    return pl.pallas_call(body, name=name, grid=(S // tm,), in_specs=[blk, blk, blk], out_specs=[blk, blk],
                          out_shape=[jax.ShapeDtypeStruct((S, D), F32), jax.ShapeDtypeStruct((S, D), BF16)],
                          compiler_params=_cp("parallel"))(dy2, yg, gl)


def _gelu_bwd(da, db, ypre, u, *, name):
    S, D = ypre.shape
    tm = _tile(S, 512, 8)
    blk = pl.BlockSpec((tm, D), lambda i: (i, 0))
    vec = pl.BlockSpec((1, D), lambda i: (0, 0))

    def body(a_ref, b_ref, y_ref, u_ref, o_ref, s_ref):
        @pl.when(pl.program_id(0) == 0)
        def _():
            s_ref[...] = jnp.zeros_like(s_ref)

        dy = (a_ref[...] + b_ref[...]) * _gelu_grad(y_ref[...].astype(F32))
        o_ref[...] = dy.astype(o_ref.dtype)
        s_ref[...] += jnp.sum(dy * u_ref[...].astype(F32), axis=0, keepdims=True)

    return pl.pallas_call(body, name=name, grid=(S // tm,), in_specs=[blk, blk, blk, blk], out_specs=[blk, vec],
                          out_shape=[jax.ShapeDtypeStruct((S, D), BF16), jax.ShapeDtypeStruct((1, D), F32)],
                          compiler_params=_cp("arbitrary"))(da, db, ypre, u)


def _my_index():
    return 4 * lax.axis_index("x") + 2 * lax.axis_index("y") + lax.axis_index("c")


def _exchange(x, *, gather, name):
    shp = x.shape if gather else x.shape[1:]

    def body(x_ref, o_ref, send_sems, recv_sems, local_sem):
        mx, my, mc = lax.axis_index("x"), lax.axis_index("y"), lax.axis_index("c")
        me = 4 * mx + 2 * my + mc
        src = (lambda j: x_ref) if gather else (lambda j: x_ref.at[j])
        local = pltpu.make_async_copy(src(me), o_ref.at[me], local_sem)
        local.start()
        copies = []
        for k in range(1, N_DEV):
            px, py, pc = mx ^ (k >> 2), my ^ ((k >> 1) & 1), mc ^ (k & 1)
            cp = pltpu.make_async_remote_copy(src(4 * px + 2 * py + pc), o_ref.at[me], send_sems.at[k - 1], recv_sems.at[k - 1],
                                              device_id=(px, py, pc), device_id_type=MESH_ID)
            cp.start()
            copies.append(cp)
        for cp in copies:
            cp.wait()
        local.wait()

    return pl.pallas_call(
        body, name=name, in_specs=[pl.BlockSpec(memory_space=pl.ANY)], out_specs=pl.BlockSpec(memory_space=pl.ANY),
        out_shape=jax.ShapeDtypeStruct((N_DEV,) + shp, x.dtype),
        scratch_shapes=[pltpu.SemaphoreType.DMA((N_DEV - 1,)), pltpu.SemaphoreType.DMA((N_DEV - 1,)), pltpu.SemaphoreType.DMA(())],
        compiler_params=pltpu.CompilerParams(has_side_effects=True))(x)


def _sum8(x8, *, name):
    _, R, C = x8.shape
    tr = _tile(R, 256, 16)

    def body(x_ref, o_ref):
        acc = x_ref[0].astype(F32)
        for j in range(1, N_DEV):
            acc = acc + x_ref[j].astype(F32)
        o_ref[...] = acc

    return pl.pallas_call(body, name=name, grid=(R // tr,), in_specs=[pl.BlockSpec((N_DEV, tr, C), lambda i: (0, i, 0))],
                          out_specs=pl.BlockSpec((tr, C), lambda i: (i, 0)), out_shape=jax.ShapeDtypeStruct((R, C), F32),
                          compiler_params=_cp("parallel"))(x8)


def _pack(arrs, dtype, row_mult):
    flat = jnp.concatenate([a.reshape(-1).astype(dtype) for a in arrs])
    n = flat.shape[0]
    rows = -(-n // 1024)
    rows = -(-rows // row_mult) * row_mult
    return jnp.pad(flat, (0, rows * 1024 - n)).reshape(rows, 1024)


def _unpack(buf, shapes):
    lead = buf.shape[:-2]
    flat = buf.reshape(lead + (-1,))
    out, off = [], 0
    for s in shapes:
        n = math.prod(s)
        out.append(flat[..., off:off + n].reshape(lead + tuple(s)))
        off += n
    return out


def _adaln_fwd(c_all, w, b, *, name):
    nk, D, n = w.shape

    def body(c_ref, w_ref, b_ref, o_ref):
        cv = c_ref[...]
        sc = (cv * jax.nn.sigmoid(cv)).astype(BF16)
        o_ref[...] = jnp.dot(sc, w_ref[...].astype(BF16), preferred_element_type=F32) + b_ref[...]

    return pl.pallas_call(body, name=name, grid=(nk,),
                          in_specs=[pl.BlockSpec((N_DEV, D), lambda k: (0, 0)), pl.BlockSpec((None, D, n), lambda k: (k, 0, 0)),
                                    pl.BlockSpec((None, 1, n), lambda k: (k, 0, 0))],
                          out_specs=pl.BlockSpec((None, N_DEV, n), lambda k: (k, 0, 0)),
                          out_shape=jax.ShapeDtypeStruct((nk, N_DEV, n), F32), compiler_params=_cp("parallel"))(c_all, w, b)


def _adaln_bwd(c_all, dmod, *, name):
    nk, _, n = dmod.shape
    D = c_all.shape[1]

    def body(c_ref, d_ref, dw_ref, db_ref):
        cv = c_ref[...]
        sc = (cv * jax.nn.sigmoid(cv)).astype(BF16).astype(F32)
        dm = d_ref[...]
        dw_ref[...] = lax.dot_general(sc, dm.astype(BF16).astype(F32), TN_DIMS, precision=lax.Precision.HIGHEST,
                                      preferred_element_type=F32)
        db_ref[...] = jnp.sum(dm, axis=0, keepdims=True)

    return pl.pallas_call(body, name=name, grid=(nk,),
                          in_specs=[pl.BlockSpec((N_DEV, D), lambda k: (0, 0)), pl.BlockSpec((None, N_DEV, n), lambda k: (k, 0, 0))],
                          out_specs=[pl.BlockSpec((None, D, n), lambda k: (k, 0, 0)), pl.BlockSpec((None, 1, n), lambda k: (k, 0, 0))],
                          out_shape=[jax.ShapeDtypeStruct((nk, D, n), F32), jax.ShapeDtypeStruct((nk, 1, n), F32)],
                          compiler_params=_cp("parallel"))(c_all, dmod)


def _adamw(w, g, m, v, *, name):
    R, C = w.shape
    tr = _tile(R, 256, 8)
    blk = pl.BlockSpec((tr, C), lambda i: (i, 0))
    c1 = 1.0 / (1.0 - ADAM_B1 ** ADAM_STEP)
    c2 = 1.0 / (1.0 - ADAM_B2 ** ADAM_STEP)

    def body(w_ref, g_ref, m_ref, v_ref, d_ref, nm_ref, nv_ref):
        gv = g_ref[...]
        nm = ADAM_B1 * m_ref[...] + (1.0 - ADAM_B1) * gv
        nv = ADAM_B2 * v_ref[...] + (1.0 - ADAM_B2) * (gv * gv)
        nm_ref[...] = nm
        nv_ref[...] = nv
        d_ref[...] = -ADAM_LR * ((nm * c1) / (jnp.sqrt(nv * c2) + ADAM_EPS) + ADAM_WD * w_ref[...])

    sh = jax.ShapeDtypeStruct((R, C), F32)
    return pl.pallas_call(body, name=name, grid=(R // tr,), in_specs=[blk] * 4, out_specs=[blk] * 3, out_shape=[sh] * 3,
                          compiler_params=_cp("parallel"))(w, g, m, v)


def _adamw_nd(w, g, m, v, *, name):
    shp = w.shape
    two = (-1, shp[-1]) if w.ndim > 1 else (1, -1)
    outs = _adamw(w.reshape(two), g.reshape(two), m.reshape(two), v.reshape(two), name=name)
    return [o.reshape(shp) for o in outs]


S5_TILE = 256
ATTN_BLOCK = 512
BIG = ("s5_w_in", "s5_w_glu", "s5_w_out", "fox_w_in", "fox_w_out", "ffn_w_up", "ffn_w_down")
WEIGHTS = ("norm_g", "ada_w", "ada_b", "s5_w_in", "s5_lam_re", "s5_lam_im", "s5_log_dt", "s5_b_re", "s5_b_im", "s5_c_re",
           "s5_c_im", "s5_d", "s5_w_glu", "s5_w_out", "fox_w_in", "fox_b_f", "fox_w_out", "ffn_w_up", "ffn_conv_w",
           "ffn_conv_b", "ffn_w_down", "final_g")
REPLICATED = ("s5_lam_re", "s5_lam_im", "s5_log_dt", "s5_b_re", "s5_b_im", "s5_c_re", "s5_c_im", "s5_d", "fox_b_f",
              "ffn_conv_b", "final_g")


ROW_SPLIT = ("s5_w_in", "s5_w_glu", "s5_w_out", "fox_w_out", "ffn_w_down")


def _row_group(arrs):
    return jnp.concatenate([a.reshape(a.shape[:-3] + (-1, D_MODEL)) if a.ndim > 2 else a for a in arrs], axis=-2)


def _row_ungroup(buf, shapes):
    out, off = [], 0
    for s in shapes:
        n = math.prod(s[:-1])
        out.append(buf[..., off:off + n, :].reshape(buf.shape[:-2] + tuple(s)))
        off += n
    return out


def _cols_join(g):
    return jnp.concatenate([g[j] for j in range(N_DEV)], axis=1)


def _cols_split(full):
    n = full.shape[1] // N_DEV
    return jnp.stack([full[:, j * n:(j + 1) * n] for j in range(N_DEV)])


def kernel(x, c, norm_g, ada_w, ada_b, s5_w_in, s5_lam_re, s5_lam_im, s5_log_dt, s5_b_re, s5_b_im, s5_c_re, s5_c_im, s5_d, s5_w_glu, s5_w_out, fox_w_in, fox_b_f, fox_w_out, ffn_w_up, ffn_conv_w, ffn_conv_b, ffn_w_down, final_g, loss_target, m_norm_g, m_ada_w, m_ada_b, m_s5_w_in, m_s5_lam_re, m_s5_lam_im, m_s5_log_dt, m_s5_b_re, m_s5_b_im, m_s5_c_re, m_s5_c_im, m_s5_d, m_s5_w_glu, m_s5_w_out, m_fox_w_in, m_fox_b_f, m_fox_w_out, m_ffn_w_up, m_ffn_conv_w, m_ffn_conv_b, m_ffn_w_down, m_final_g, v_norm_g, v_ada_w, v_ada_b, v_s5_w_in, v_s5_lam_re, v_s5_lam_im, v_s5_log_dt, v_s5_b_re, v_s5_b_im, v_s5_c_re, v_s5_c_im, v_s5_d, v_s5_w_glu, v_s5_w_out, v_fox_w_in, v_fox_b_f, v_fox_w_out, v_ffn_w_up, v_ffn_conv_w, v_ffn_conv_b, v_ffn_w_down, v_final_g):
    args = dict(locals())
    W = {n: args[n] for n in WEIGHTS}
    M = {n: args["m_" + n] for n in WEIGHTS}
    V = {n: args["v_" + n] for n in WEIGHTS}
    D, F = D_MODEL, D_FF
    me = _my_index()
    h0 = x[0]
    S = h0.shape[0]
    lt = min(S5_TILE, S)
    tb = min(ATTN_BLOCK, S)
    n_g, n_cw = norm_g.size, ffn_conv_w.size

    g0 = _exchange(_pack([c, norm_g, ffn_conv_w], F32, 8), gather=True, name="gather_small_in")
    c_all, ng_all, cw_all = _unpack(g0, [(D,), norm_g.shape, ffn_conv_w.shape])
    ng_full = ng_all.transpose(1, 2, 0, 3).reshape(2, 2, D)
    cw_full = cw_all.transpose(1, 2, 0, 3).reshape(2, 3, F)

    ncol = ada_w.shape[-1]
    modp = _adaln_fwd(c_all, ada_w.reshape(4, D, ncol), ada_b.reshape(4, 1, ncol), name="adaln_fwd")
    g1 = _exchange(_pack([modp], F32, 8), gather=True, name="gather_adaln")
    (mod_all,) = _unpack(g1, [modp.shape])
    mod = lax.dynamic_index_in_dim(mod_all, me, axis=2, keepdims=False).transpose(1, 0, 2).reshape(4, 3 * D)
    shift = [mod[k:k + 1, :D] for k in range(4)]
    scale = [mod[k:k + 1, D:2 * D] for k in range(4)]
    gate = [mod[k:k + 1, 2 * D:] for k in range(4)]
    gain = [_row(ng_full[k // 2, k % 2]) for k in range(4)]

    g_rows = _exchange(_row_group([W[n].astype(BF16) for n in ROW_SPLIT]), gather=True, name="gather_row_weights")
    g_fox = _exchange(fox_w_in[0].astype(BF16), gather=True, name="gather_fox_w_in")
    g_up = _exchange(ffn_w_up.astype(BF16).reshape(2 * D, -1), gather=True, name="gather_ffn_w_up")
    full = {n: g.reshape(D, D) for n, g in zip(ROW_SPLIT[:4], _row_ungroup(g_rows, [W[n].shape for n in ROW_SPLIT])[:4])}
    full["ffn_w_down"] = _row_ungroup(g_rows, [W[n].shape for n in ROW_SPLIT])[4].transpose(1, 0, 2, 3).reshape(2, F, D)
    full["fox_w_in"] = _cols_join(g_fox)
    full["ffn_w_up"] = jnp.stack([_cols_join(g_up[:, l * D:(l + 1) * D]) for l in range(2)])
    w_proj = jnp.pad(full["fox_w_in"], ((0, 0), (0, 3 * D + 128 - full["fox_w_in"].shape[1])))
    w_qkv, w_f = w_proj[:, :3 * D], w_proj[:, 3 * D:]
    bf_pad = jnp.pad(fox_b_f, ((0, 0), (0, 128 - HEADS)))

    def ffn_fwd(h, k, layer):
        hn = _modulate(h, gain[k], shift[k], scale[k], name=f"modulate{k}")
        up = _mm(hn, full["ffn_w_up"][layer], name=f"ffn_up{layer}", o_h=True, out_dtype=BF16, tn=1408)
        z = _conv_gate_fwd(up, cw_full[layer], ffn_conv_b[layer:layer + 1], name=f"conv_gate{layer}")
        m, h_out = _mm(z, full["ffn_w_down"][layer], name=f"ffn_down{layer}", res=h, gate=gate[k])
        return h_out, (hn, up, z, m)

    lb_re, lb_im, bb_re, bb_im = _s5_disc(s5_lam_re[0], s5_lam_im[0], s5_log_dt[0], s5_b_re[0], s5_b_im[0])
    wb, wc, lam = _s5_pack(lb_re, lb_im, bb_re, bb_im, s5_c_re[0], s5_c_im[0])
    hn0 = _modulate(h0, gain[0], shift[0], scale[0], name="modulate0")
    u = _mm(hn0, full["s5_w_in"], name="s5_in", out_dtype=BF16)
    ypre, yg, st = _s5_fwd(u, wb, wc, lam, s5_d, lt=lt, name="s5_scan")
    gl = _mm(yg, full["s5_w_glu"], name="s5_glu", out_dtype=BF16)
    y2 = _glu_fwd(yg, gl, name="s5_glu_gate")
    m0, h1 = _mm(y2, full["s5_w_out"], name="s5_out", res=h0, gate=gate[0])
    h2, (hn1, up0, z0, m1) = ffn_fwd(h1, 1, 0)

    hn2 = _modulate(h2, gain[2], shift[2], scale[2], name="modulate2")
    nb = S // tb
    w_qkv_s = jnp.concatenate([w_qkv[:, :D] * Q_SCALE, w_qkv[:, D:]], axis=1)
    qkv = _mm(hn2, w_qkv_s, name="fox_qkv", out_dtype=BF16)
    qkvT = _mm(w_qkv_s, hn2, name="fox_qkv_t", ta=True, tb=True, out_dtype=BF16)
    fl = _mm(hn2, w_f, name="fox_f", out_dtype=F32)
    fcol = _fgate_fwd(fl, bf_pad, name="fox_fgate")
    f_heads = fcol[:, :HEADS].T.reshape(HP, 2, S)
    ka = _key_operand(qkv[:, D:2 * D], f_heads)
    vtb = _blocked(qkvT[2 * D:].reshape(HP, 128, S), nb, tb)
    aug = _aug_rows(f_heads)
    oT, lse = _foxt_fwd(qkvT, ka, vtb, aug, tb=tb, name="fox_attn")
    m2, h3 = _mm(oT, full["fox_w_out"], name="fox_out", ta=True, res=h2, gate=gate[2])
    h4, (hn3, up1, z1, m3) = ffn_fwd(h3, 3, 1)

    lblk, dh, d_final_g = _loss_head(h4, _row(final_g), loss_target[0], name="loss_head")
    loss = lax.psum(lblk[0, 0], ("x", "y", "c"))

    G = {}
    dmod = [None] * 4

    def norm_bwd(h, dhn, dh_in, k):
        dh_out, sums = _modulate_bwd(h, dhn, dh_in, gain[k], scale[k], name=f"modulate_bwd{k}")
        return dh_out, sums[0:1], sums[1:2], sums[2]

    def ffn_bwd(dh_in, h, k, layer, saved):
        hn, up, z, m = saved
        dm, dgate = _gate_bwd(dh_in, m, gate[k], name=f"gate_bwd{k}")
        dz = _mm(dm, full["ffn_w_down"][layer], name=f"ffn_down_dx{layer}", tb=True, out_dtype=BF16, tn=1408)
        dw_down = _mm(z, dm, name=f"ffn_down_dw{layer}", ta=True, tm=1408, tn=512, tk=512)
        d_up, cs = _conv_gate_bwd(up, dz, cw_full[layer], ffn_conv_b[layer:layer + 1], name=f"conv_gate_bwd{layer}")
        dhn = _mm(d_up, full["ffn_w_up"][layer], name=f"ffn_up_dx{layer}", tb=True, a_h=True, tk=1408)
        dw_up = _mm(hn, d_up, name=f"ffn_up_dw{layer}", ta=True, b_h=True, tm=1024, tn=1408, tk=512)
        dh_out, dshift, dscale, dg = norm_bwd(h, dhn, dh_in, k)
        dmod[k] = jnp.concatenate([dshift, dscale, dgate], axis=1)
        return dh_out, dg, dw_up, dw_down, cs[0:3], cs[3]

    dh, dg3, dw_up1, dw_down1, dcw1, dcb1 = ffn_bwd(dh, h3, 3, 1, (hn3, up1, z1, m3))

    dm, dgate2 = _gate_bwd(dh, m2, gate[2], name="gate_bwd2")
    do = _mm(dm, full["fox_w_out"], name="fox_out_dx", tb=True, out_dtype=BF16)
    doT = _mm(full["fox_w_out"], dm, name="fox_out_dx_t", tb=True, out_dtype=BF16)
    G["fox_w_out"] = _mm(oT, dm, name="fox_out_dw", tk=512)
    delta = _foxt_stats(qkvT, qkv, ka, aug, doT, lse, tb=tb, name="fox_attn_stats")
    dqtb, dk, dv, df = _foxt_bwd(qkvT, qkv, ka, _blocked(qkvT[:D].reshape(HP, 128, S), nb, tb),
                                 _blocked(doT.reshape(HP, 128, S), nb, tb), do, _blocked(aug, nb, tb),
                                 _blocked(lse, nb, tb), _blocked(delta, nb, tb), tb=tb, name="fox_attn_bwd")
    dq = dqtb.transpose(1, 3, 0, 2).reshape(S, D)
    dF = jnp.pad(df[:, :, :2].transpose(1, 0, 2).reshape(S, HEADS), ((0, 0), (0, 128 - HEADS)))
    dfl, dbf = _fgate_bwd(dF, fl, bf_pad, name="fox_fgate_bwd")
    dproj = jnp.concatenate([dq, dk, dv, dfl.astype(BF16)], axis=1)
    dhn = _mm(dproj, w_proj, name="fox_in_dx", tb=True, tk=640)
    dw_proj = _mm(hn2, dproj, name="fox_in_dw", ta=True, tn=640, tk=512)
    G["fox_w_in"] = dw_proj[:, :full["fox_w_in"].shape[1]]
    dh, dshift, dscale, dg2 = norm_bwd(h2, dhn, dh, 2)
    dmod[2] = jnp.concatenate([dshift, dscale, dgate2], axis=1)

    dh, dg1, dw_up0, dw_down0, dcw0, dcb0 = ffn_bwd(dh, h1, 1, 0, (hn1, up0, z0, m1))
    G["ffn_w_up"] = jnp.stack([dw_up0, dw_up1])
    G["ffn_w_down"] = jnp.stack([dw_down0, dw_down1])

    dm, dgate0 = _gate_bwd(dh, m0, gate[0], name="gate_bwd0")
    dy2 = _mm(dm, full["s5_w_out"], name="s5_out_dx", tb=True, out_dtype=BF16)
    G["s5_w_out"] = _mm(y2, dm, name="s5_out_dw", ta=True, tk=512)
    da, dgl = _glu_bwd(dy2, yg, gl, name="s5_glu_bwd")
    dyg_b = _mm(dgl, full["s5_w_glu"], name="s5_glu_dx", tb=True)
    G["s5_w_glu"] = _mm(yg, dgl, name="s5_glu_dw", ta=True, tk=512)
    dyp, dd = _gelu_bwd(da, dyg_b, ypre, u, name="s5_gelu_bwd")
    du, dwb, dwc, dlam = _s5_bwd(u, dyp, st, wb, wc, lam, s5_d, lt=lt, name="s5_scan_bwd")
    dhn = _mm(du, full["s5_w_in"], name="s5_in_dx", tb=True)
    G["s5_w_in"] = _mm(hn0, du, name="s5_in_dw", ta=True, tk=512)
    dh, dshift, dscale, dg0 = norm_bwd(h0, dhn, dh, 0)
    dmod[0] = jnp.concatenate([dshift, dscale, dgate0], axis=1)
    grad_x = dh[None]

    dlb_re, dlb_im, dbb_re, dbb_im, dc_re, dc_im = _s5_unpack(dwb, dwc, dlam)
    _, disc_vjp = jax.vjp(_s5_disc, s5_lam_re[0], s5_lam_im[0], s5_log_dt[0], s5_b_re[0], s5_b_im[0])
    dlam_re, dlam_im, dlog_dt, db_re, db_im = disc_vjp((dlb_re, dlb_im, dbb_re, dbb_im))

    g2 = _exchange(_pack([jnp.concatenate(dmod, axis=0)], F32, 8), gather=True, name="gather_dmod")
    (dmod_all,) = _unpack(g2, [(4, 3 * D)])
    dmod_mine = _columns_of_mod(dmod_all, me, ncol)
    d_ada_w, d_ada_b = _adaln_bwd(c_all, dmod_mine, name="adaln_bwd")
    G["ada_w"] = d_ada_w.reshape(ada_w.shape)
    G["ada_b"] = d_ada_b.reshape(ada_b.shape)

    chunks = [G[n].astype(BF16).reshape(N_DEV, D // N_DEV, D) for n in ROW_SPLIT[:4]]
    chunks.append(G["ffn_w_down"].astype(BF16).reshape(2, N_DEV, F // N_DEV, D).transpose(1, 0, 2, 3).reshape(N_DEV, -1, D))
    got = _exchange(jnp.concatenate(chunks, axis=1), gather=False, name="scatter_row_grads")
    for n, g in zip(ROW_SPLIT, _row_ungroup(_sum8(got, name="sum_row_grads"), [W[n].shape for n in ROW_SPLIT])):
        G[n] = g
    got = _exchange(_cols_split(G["fox_w_in"].astype(BF16)), gather=False, name="scatter_fox_w_in")
    G["fox_w_in"] = _sum8(got, name="sum_fox_w_in")[None]
    up_chunks = jnp.concatenate([_cols_split(G["ffn_w_up"][l].astype(BF16)) for l in range(2)], axis=1)
    got = _exchange(up_chunks, gather=False, name="scatter_ffn_w_up")
    G["ffn_w_up"] = _sum8(got, name="sum_ffn_w_up").reshape(ffn_w_up.shape)

    small_full = {
        "norm_g": jnp.stack([dg0, dg1, dg2, dg3]).reshape(2, 2, D),
        "ffn_conv_w": jnp.stack([dcw0, dcw1]),
        "s5_lam_re": dlam_re[None], "s5_lam_im": dlam_im[None], "s5_log_dt": dlog_dt[None],
        "s5_b_re": db_re[None], "s5_b_im": db_im[None], "s5_c_re": dc_re[None], "s5_c_im": dc_im[None],
        "s5_d": dd, "fox_b_f": dbf[:, :HEADS], "ffn_conv_b": jnp.stack([dcb0, dcb1]), "final_g": d_final_g[0],
    }
    names = tuple(small_full)
    g3 = _exchange(_pack([small_full[n] for n in names], F32, 8), gather=True, name="gather_small_grads")
    summed = dict(zip(names, _unpack(_sum8(g3, name="sum_small_grads"), [small_full[n].shape for n in names])))
    G["norm_g"] = lax.dynamic_slice_in_dim(summed["norm_g"], me * norm_g.shape[-1], norm_g.shape[-1], axis=2)
    G["ffn_conv_w"] = lax.dynamic_slice_in_dim(summed["ffn_conv_w"], me * ffn_conv_w.shape[-1], ffn_conv_w.shape[-1], axis=2)
    for n in REPLICATED:
        G[n] = summed[n]

    delta, new_m, new_v = {}, {}, {}
    small = tuple(n for n in WEIGHTS if n not in BIG and n != "ada_w")
    for n in WEIGHTS:
        if n not in small:
            delta[n], new_m[n], new_v[n] = _adamw_nd(W[n], G[n], M[n], V[n], name=f"adamw_{n}")
    packed = [_pack([src[n] for n in small], F32, 8) for src in (W, G, M, V)]
    for dst, buf in zip((delta, new_m, new_v), _adamw(*packed, name="adamw_small")):
        dst.update(zip(small, _unpack(buf, [W[n].shape for n in small])))

    return (loss, grad_x, *[G[n] for n in WEIGHTS], *[delta[n] for n in WEIGHTS], *[new_m[n] for n in WEIGHTS],
            *[new_v[n] for n in WEIGHTS])


def _columns_of_mod(dmod_all, me, ncol):
    flat = lax.dynamic_slice_in_dim(dmod_all, me * ncol, ncol, axis=2)
    return flat.transpose(1, 0, 2)
```

```python
import functools
import math

import jax
import jax.numpy as jnp
from jax import lax
from jax.experimental import pallas as pl
from jax.experimental.pallas import tpu as pltpu

F32, BF16 = jnp.float32, jnp.bfloat16
EPS = 1e-6
N_DEV = 8
D_MODEL = 1024
D_FF = 2816
HEADS = 16
HEAD_DIM = 64
S5_G, S5_P, S5_C = 64, 64, 16
S5_SUB = 8
V7X_VMEM_LIMIT = 56 * 1024 * 1024
NEG = -1e30
ADAM_LR, ADAM_B1, ADAM_B2, ADAM_EPS, ADAM_WD, ADAM_STEP = 1e-3, 0.9, 0.999, 1e-8, 0.01, 10
GELU_K = math.sqrt(2.0 / math.pi)
MESH_ID = pl.DeviceIdType.MESH


def _cp(*sem):
    return pltpu.CompilerParams(dimension_semantics=sem, vmem_limit_bytes=V7X_VMEM_LIMIT)


def _tile(n, target, mult=128):
    if n <= target:
        return n
    t = (target // mult) * mult
    while t >= mult:
        if n % t == 0:
            return t
        t -= mult
    return n


def _row(v):
    return v.reshape(1, -1).astype(F32)


def _mm(a, b, *, name, ta=False, tb=False, out_dtype=F32, tm=1024, tn=512, tk=None, res=None, gate=None,
        a_h=False, b_h=False, o_h=False):
    if a_h:
        M, K = a.shape[1], 2 * a.shape[2]
    elif ta:
        K, M = a.shape
    else:
        M, K = a.shape
    if b_h:
        N = 2 * b.shape[2]
    else:
        N = b.shape[0] if tb else b.shape[1]
    half_n = N // 2
    tm = _tile(M, tm, 128 if ta else 8)
    tn = _tile(half_n if (b_h or o_h) else N, tn)
    tk = K if tk is None else _tile(K // 2 if a_h else K, tk)
    nk = K // tk
    nkh, nnh = (K // 2) // tk if a_h else 1, half_n // tn
    if a_h:
        a_spec = pl.BlockSpec((None, tm, tk), lambda i, j, k: (k // nkh, i, k % nkh))
    elif ta:
        a_spec = pl.BlockSpec((tk, tm), lambda i, j, k: (k, i))
    else:
        a_spec = pl.BlockSpec((tm, tk), lambda i, j, k: (i, k))
    if b_h:
        b_spec = pl.BlockSpec((None, tk, tn), lambda i, j, k: (j // nnh, k, j % nnh))
    elif tb:
        b_spec = pl.BlockSpec((tn, tk), lambda i, j, k: (j, k))
    else:
        b_spec = pl.BlockSpec((tk, tn), lambda i, j, k: (k, j))
    if o_h:
        o_spec = pl.BlockSpec((None, tm, tn), lambda i, j, k: (j // nnh, i, j % nnh))
    else:
        o_spec = pl.BlockSpec((tm, tn), lambda i, j, k: (i, j))
    dn = (((0 if ta else 1,), (1 if tb else 0,)), ((), ()))
    fused = res is not None

    def body(*refs):
        if fused:
            a_ref, b_ref, r_ref, g_ref, m_ref, o_ref, acc_ref = refs
        else:
            a_ref, b_ref, o_ref, acc_ref = refs
        p = lax.dot_general(a_ref[...].astype(BF16), b_ref[...].astype(BF16), dn, preferred_element_type=F32)

        def finish(acc):
            if fused:
                m_ref[...] = acc.astype(m_ref.dtype)
                o_ref[...] = r_ref[...] + g_ref[...] * acc
            else:
                o_ref[...] = acc.astype(o_ref.dtype)

        if nk == 1:
            finish(p)
        else:
            k = pl.program_id(2)

            @pl.when(k == 0)
            def _():
                acc_ref[...] = p

            @pl.when(k > 0)
            def _():
                acc_ref[...] += p

            @pl.when(k == nk - 1)
            def _():
                finish(acc_ref[...])

    in_specs = [a_spec, b_spec]
    args = [a, b]
    if fused:
        in_specs += [o_spec, pl.BlockSpec((1, tn), lambda i, j, k: (0, j))]
        args += [res, gate]
        out_shape = [jax.ShapeDtypeStruct((M, N), BF16), jax.ShapeDtypeStruct((M, N), F32)]
        out_specs = [o_spec, o_spec]
    else:
        out_shape = jax.ShapeDtypeStruct((2, M, half_n) if o_h else (M, N), out_dtype)
        out_specs = o_spec
    return pl.pallas_call(
        body, name=name, grid=(M // tm, N // tn, nk), in_specs=in_specs, out_specs=out_specs, out_shape=out_shape,
        scratch_shapes=[pltpu.VMEM((tm, tn) if nk > 1 else (8, 128), F32)],
        compiler_params=_cp("parallel", "parallel", "arbitrary"),
    )(*args)


def _modulate(h, g, shift, scale, *, name):
    S, D = h.shape
    tm = _tile(S, 512, 8)
    vec = pl.BlockSpec((1, D), lambda i: (0, 0))
    blk = pl.BlockSpec((tm, D), lambda i: (i, 0))

    def body(h_ref, g_ref, sh_ref, sc_ref, o_ref):
        x = h_ref[...]
        r = lax.rsqrt(jnp.mean(x * x, axis=-1, keepdims=True) + EPS)
        o_ref[...] = ((x * r * g_ref[...]) * (1.0 + sc_ref[...]) + sh_ref[...]).astype(o_ref.dtype)

    return pl.pallas_call(body, name=name, grid=(S // tm,), in_specs=[blk, vec, vec, vec], out_specs=blk,
                          out_shape=jax.ShapeDtypeStruct((S, D), BF16), compiler_params=_cp("parallel"))(h, g, shift, scale)


def _gate_bwd(dh, m, gate, *, name):
    S, D = dh.shape
    tm = _tile(S, 512, 8)
    vec = pl.BlockSpec((1, D), lambda i: (0, 0))
    blk = pl.BlockSpec((tm, D), lambda i: (i, 0))

    def body(dh_ref, m_ref, g_ref, dm_ref, dg_ref):
        @pl.when(pl.program_id(0) == 0)
        def _():
            dg_ref[...] = jnp.zeros_like(dg_ref)

        d = dh_ref[...]
        dm_ref[...] = (d * g_ref[...]).astype(dm_ref.dtype)
        dg_ref[...] += jnp.sum(d * m_ref[...].astype(F32), axis=0, keepdims=True)

    return pl.pallas_call(body, name=name, grid=(S // tm,), in_specs=[blk, blk, vec], out_specs=[blk, vec],
                          out_shape=[jax.ShapeDtypeStruct((S, D), BF16), jax.ShapeDtypeStruct((1, D), F32)],
                          compiler_params=_cp("arbitrary"))(dh, m, gate)


def _modulate_bwd(h, dhn, dh_in, g, scale, *, name):
    S, D = h.shape
    tm = _tile(S, 512, 8)
    vec = pl.BlockSpec((1, D), lambda i: (0, 0))
    blk = pl.BlockSpec((tm, D), lambda i: (i, 0))
    sums = pl.BlockSpec((8, D), lambda i: (0, 0))

    def body(h_ref, dhn_ref, dhi_ref, g_ref, sc_ref, dh_ref, s_ref):
        @pl.when(pl.program_id(0) == 0)
        def _():
            s_ref[...] = jnp.zeros_like(s_ref)

        x = h_ref[...]
        r = lax.rsqrt(jnp.mean(x * x, axis=-1, keepdims=True) + EPS)
        xhat = x * r
        gv = g_ref[...]
        dhn_v = dhn_ref[...].astype(F32)
        dn = dhn_v * (1.0 + sc_ref[...])
        s_ref[0:1, :] += jnp.sum(dhn_v, axis=0, keepdims=True)
        s_ref[1:2, :] += jnp.sum(dhn_v * (xhat * gv), axis=0, keepdims=True)
        s_ref[2:3, :] += jnp.sum(dn * xhat, axis=0, keepdims=True)
        dxh = dn * gv
        dh_ref[...] = dhi_ref[...] + r * (dxh - xhat * jnp.mean(dxh * xhat, axis=-1, keepdims=True))

    return pl.pallas_call(body, name=name, grid=(S // tm,), in_specs=[blk, blk, blk, vec, vec], out_specs=[blk, sums],
                          out_shape=[jax.ShapeDtypeStruct((S, D), F32), jax.ShapeDtypeStruct((8, D), F32)],
                          compiler_params=_cp("arbitrary"))(h, dhn, dh_in, g, scale)


def _loss_head(h, g, tgt, *, name):
    S, D = h.shape
    tm = _tile(S, 512, 8)
    vec = pl.BlockSpec((1, D), lambda i: (0, 0))
    blk = pl.BlockSpec((tm, D), lambda i: (i, 0))
    lss = pl.BlockSpec((8, 128), lambda i: (0, 0))

    def body(h_ref, g_ref, t_ref, l_ref, dh_ref, dg_ref):
        @pl.when(pl.program_id(0) == 0)
        def _():
            l_ref[...] = jnp.zeros_like(l_ref)
            dg_ref[...] = jnp.zeros_like(dg_ref)

        x = h_ref[...]
        r = lax.rsqrt(jnp.mean(x * x, axis=-1, keepdims=True) + EPS)
        xhat = x * r
        gv = g_ref[...]
        e = xhat * gv - t_ref[...]
        l_ref[...] += 0.5 * jnp.sum(jnp.mean(e * e, axis=-1, keepdims=True))
        dy = e * (1.0 / D)
        dg_ref[...] += jnp.sum(dy * xhat, axis=0, keepdims=True)
        dxh = dy * gv
        dh_ref[...] = r * (dxh - xhat * jnp.mean(dxh * xhat, axis=-1, keepdims=True))

    return pl.pallas_call(body, name=name, grid=(S // tm,), in_specs=[blk, vec, blk], out_specs=[lss, blk, vec],
                          out_shape=[jax.ShapeDtypeStruct((8, 128), F32), jax.ShapeDtypeStruct((S, D), F32),
                                     jax.ShapeDtypeStruct((1, D), F32)],
                          compiler_params=_cp("arbitrary"))(h, g, tgt)


def _shift_down(x, k, edge):
    tm = x.shape[0]
    rows = lax.broadcasted_iota(jnp.int32, x.shape, 0)
    out = pltpu.roll(x, k, 0)
    for j in range(k):
        out = jnp.where(rows == j, edge[8 - k + j:8 - k + j + 1, :], out)
    return out


def _shift_up(x, k, edge):
    tm = x.shape[0]
    rows = lax.broadcasted_iota(jnp.int32, x.shape, 0)
    out = pltpu.roll(x, tm - k, 0)
    for j in range(k):
        out = jnp.where(rows == tm - k + j, edge[j:j + 1, :], out)
    return out


def _conv_gate_fwd(up, cw, cb, *, name):
    _, S, F = up.shape
    tf = _tile(F, 1408)
    nf = F // tf
    tm = _tile(S, 512, 8)

    def body(a_ref, b_ref, w_ref, cb_ref, z_ref, edge_ref):
        @pl.when(pl.program_id(1) == 0)
        def _():
            edge_ref[...] = jnp.zeros_like(edge_ref)

        a = a_ref[...].astype(F32)
        edge = edge_ref[...]
        w = w_ref[...]
        ac = cb_ref[...] + w[2:3, :] * a + w[1:2, :] * _shift_down(a, 1, edge) + w[0:1, :] * _shift_down(a, 2, edge)
        edge_ref[...] = a[tm - 8:tm, :]
        z_ref[...] = (ac * jax.nn.sigmoid(ac) * b_ref[...].astype(F32)).astype(z_ref.dtype)

    return pl.pallas_call(
        body, name=name, grid=(nf, S // tm),
        in_specs=[pl.BlockSpec((None, tm, tf), lambda j, i: (0, i, j)), pl.BlockSpec((None, tm, tf), lambda j, i: (1, i, j)),
                  pl.BlockSpec((3, tf), lambda j, i: (0, j)), pl.BlockSpec((1, tf), lambda j, i: (0, j))],
        out_specs=pl.BlockSpec((tm, tf), lambda j, i: (i, j)),
        out_shape=jax.ShapeDtypeStruct((S, F), BF16), scratch_shapes=[pltpu.VMEM((8, tf), F32)],
        compiler_params=_cp("parallel", "arbitrary"))(up, up, cw, cb)


def _conv_gate_bwd(up, dz, cw, cb, *, name):
    _, S, F = up.shape
    tf = _tile(F, 1408)
    nf = F // tf
    tm = _tile(S, 512, 8)
    nt = S // tm
    hb = tm // 8

    def body(a_ref, ah_ref, b_ref, dz_ref, w_ref, cb_ref, d_ref, s_ref, edge_ref):
        i = pl.program_id(1)

        @pl.when(i == 0)
        def _():
            edge_ref[...] = jnp.zeros_like(edge_ref)
            s_ref[...] = jnp.zeros_like(s_ref)

        a = a_ref[...].astype(F32)
        halo = jnp.where(i == nt - 1, 0.0, ah_ref[...].astype(F32))
        w = w_ref[...]
        a1 = _shift_down(a, 1, halo)
        a2 = _shift_down(a, 2, halo)
        ac = cb_ref[...] + w[2:3, :] * a + w[1:2, :] * a1 + w[0:1, :] * a2
        sg = jax.nn.sigmoid(ac)
        dzv = dz_ref[...].astype(F32)
        d_ref[1] = (dzv * ac * sg).astype(d_ref.dtype)
        dac = dzv * b_ref[...].astype(F32) * (sg * (1.0 + ac * (1.0 - sg)))
        s_ref[0:1, :] += jnp.sum(dac * a2, axis=0, keepdims=True)
        s_ref[1:2, :] += jnp.sum(dac * a1, axis=0, keepdims=True)
        s_ref[2:3, :] += jnp.sum(dac * a, axis=0, keepdims=True)
        s_ref[3:4, :] += jnp.sum(dac, axis=0, keepdims=True)
        edge = edge_ref[...]
        da = w[2:3, :] * dac + w[1:2, :] * _shift_up(dac, 1, edge) + w[0:1, :] * _shift_up(dac, 2, edge)
        edge_ref[...] = dac[0:8, :]
        d_ref[0] = da.astype(d_ref.dtype)

    tile = lambda hlf: pl.BlockSpec((None, tm, tf), lambda j, i: (hlf, nt - 1 - i, j))
    d_up, sums = pl.pallas_call(
        body, name=name, grid=(nf, nt),
        in_specs=[tile(0),
                  pl.BlockSpec((None, 8, tf), lambda j, i: (0, jnp.maximum((nt - 1 - i) * hb - 1, 0), j)),
                  tile(1), pl.BlockSpec((tm, tf), lambda j, i: (nt - 1 - i, j)),
                  pl.BlockSpec((3, tf), lambda j, i: (0, j)), pl.BlockSpec((1, tf), lambda j, i: (0, j))],
        out_specs=[pl.BlockSpec((2, tm, tf), lambda j, i: (0, nt - 1 - i, j)), pl.BlockSpec((8, tf), lambda j, i: (0, j))],
        out_shape=[jax.ShapeDtypeStruct((2, S, F), BF16), jax.ShapeDtypeStruct((8, F), F32)],
        scratch_shapes=[pltpu.VMEM((8, tf), F32)],
        compiler_params=_cp("parallel", "arbitrary"))(up, up, up, dz, cw, cb)
    return d_up, sums


def _log_sigmoid(x):
    return jnp.minimum(x, 0.0) - jnp.log(1.0 + jnp.exp(-jnp.abs(x)))


def _tri_ones(n, upper):
    r = lax.broadcasted_iota(jnp.int32, (n, n), 0)
    c = lax.broadcasted_iota(jnp.int32, (n, n), 1)
    return jnp.where((c >= r) if upper else (c <= r), 1.0, 0.0).astype(F32)


def _fgate_fwd(fl, bf, *, name):
    S, W = fl.shape
    tb = _tile(S, 256, 8)

    def body(fl_ref, b_ref, o_ref, carry_ref):
        @pl.when(pl.program_id(0) == 0)
        def _():
            carry_ref[...] = jnp.zeros_like(carry_ref)

        lf = _log_sigmoid(fl_ref[...] + b_ref[...])
        cs = jnp.dot(_tri_ones(tb, False), lf, precision=lax.Precision.HIGHEST, preferred_element_type=F32) + carry_ref[0:1, :]
        o_ref[...] = cs
        carry_ref[...] = jnp.broadcast_to(cs[tb - 1:tb, :], carry_ref.shape)

    blk = pl.BlockSpec((tb, W), lambda i: (i, 0))
    return pl.pallas_call(body, name=name, grid=(S // tb,), in_specs=[blk, pl.BlockSpec((1, W), lambda i: (0, 0))], out_specs=blk,
                          out_shape=jax.ShapeDtypeStruct((S, W), F32), scratch_shapes=[pltpu.VMEM((8, W), F32)],
                          compiler_params=_cp("arbitrary"))(fl, bf)


def _fgate_bwd(dF, fl, bf, *, name):
    S, W = fl.shape
    tb = _tile(S, 256, 8)
    nb = S // tb

    def body(d_ref, fl_ref, b_ref, o_ref, s_ref, carry_ref):
        @pl.when(pl.program_id(0) == 0)
        def _():
            carry_ref[...] = jnp.zeros_like(carry_ref)
            s_ref[...] = jnp.zeros_like(s_ref)

        rc = jnp.dot(_tri_ones(tb, True), d_ref[...], precision=lax.Precision.HIGHEST, preferred_element_type=F32) + carry_ref[0:1, :]
        carry_ref[...] = jnp.broadcast_to(rc[0:1, :], carry_ref.shape)
        dfl = rc * jax.nn.sigmoid(-(fl_ref[...] + b_ref[...]))
        o_ref[...] = dfl
        s_ref[...] += jnp.sum(dfl, axis=0, keepdims=True)

    blk = pl.BlockSpec((tb, W), lambda i: (nb - 1 - i, 0))
    vec = pl.BlockSpec((1, W), lambda i: (0, 0))
    return pl.pallas_call(body, name=name, grid=(nb,), in_specs=[blk, blk, vec], out_specs=[blk, vec],
                          out_shape=[jax.ShapeDtypeStruct((S, W), F32), jax.ShapeDtypeStruct((1, W), F32)],
                          scratch_shapes=[pltpu.VMEM((8, W), F32)], compiler_params=_cp("arbitrary"))(dF, fl, bf)


NT_DIMS = (((1,), (1,)), ((), ()))
TN_DIMS = (((0,), (0,)), ((), ()))
HP = HEADS // 2
Q_SCALE = HEAD_DIM ** -0.5


def _head_mask(x, hh):
    lanes = lax.broadcasted_iota(jnp.int32, x.shape, 1)
    return jnp.where((lanes >= hh * HEAD_DIM) & (lanes < (hh + 1) * HEAD_DIM), x, jnp.zeros_like(x))


def _lanes_from(cols, shape):
    lanes = lax.broadcasted_iota(jnp.int32, shape, 1)
    out = jnp.zeros(shape, F32)
    for i, cvec in enumerate(cols):
        out = jnp.where(lanes == i, cvec, out)
    return out


AUG_ROWS = 16


def _qa_pair(qt, aug):
    fill = jnp.zeros((HEAD_DIM - AUG_ROWS, qt.shape[1]), qt.dtype)
    return [jnp.concatenate([qt[0:HEAD_DIM], aug[0], fill], axis=0), jnp.concatenate([aug[1], fill, qt[HEAD_DIM:]], axis=0)]


def _rows_of_head(xt, hh):
    rows = lax.broadcasted_iota(jnp.int32, xt.shape, 0)
    return jnp.where((rows >= hh * HEAD_DIM) & (rows < (hh + 1) * HEAD_DIM), xt, jnp.zeros_like(xt))


def _put_rows(ref, rows):
    for i, r in enumerate(rows):
        ref[i:i + 1, :] = r
    ref[len(rows):, :] = jnp.zeros((ref.shape[0] - len(rows), ref.shape[1]), ref.dtype)


def _diag_mask_t(s):
    rows = lax.broadcasted_iota(jnp.int32, s.shape, 0)
    cols = lax.broadcasted_iota(jnp.int32, s.shape, 1)
    return jnp.where(rows <= cols, s, NEG)


def _foxt_fwd(qkvT, ka, vtb, augq, *, tb, name, comm=()):
    S = qkvT.shape[1]
    nb = S // tb

    def body(q_ref, ka_ref, v_ref, aq_ref, o_ref, lse_ref, m_sc, l_sc, acc_sc):
        qi = pl.program_id(1)
        qa = _qa_pair(q_ref[...], aq_ref[...])
        m_sc[...] = jnp.full_like(m_sc, NEG)
        l_sc[...] = jnp.zeros_like(l_sc)
        acc_sc[...] = jnp.zeros_like(acc_sc)

        def block(kj, masked):
            k0 = pl.multiple_of(kj * tb, tb)
            vt = v_ref[kj]
            for hh in range(2):
                s = jnp.dot(ka_ref[hh, pl.ds(k0, tb), :], qa[hh], preferred_element_type=F32)
                if masked:
                    s = _diag_mask_t(s)
                m_old = m_sc[hh]
                m_new = jnp.maximum(m_old, jnp.max(s, axis=0, keepdims=True))
                alpha = jnp.exp(m_old - m_new)
                p = jnp.exp(s - m_new)
                l_sc[hh] = alpha * l_sc[hh] + jnp.sum(p, axis=0, keepdims=True)
                acc_sc[hh] = alpha * acc_sc[hh] + jnp.dot(vt, p.astype(BF16), preferred_element_type=F32)
                m_sc[hh] = m_new

        def off_diagonal(kj, c):
            block(kj, False)
            return c

        lax.fori_loop(0, qi, off_diagonal, 0)
        block(qi, True)
        o_ref[0:HEAD_DIM, :] = (acc_sc[0, 0:HEAD_DIM, :] / l_sc[0]).astype(o_ref.dtype)
        o_ref[HEAD_DIM:, :] = (acc_sc[1, HEAD_DIM:, :] / l_sc[1]).astype(o_ref.dtype)
        _put_rows(lse_ref, [m_sc[hh] + jnp.log(l_sc[hh]) for hh in range(2)])

    return _call(
        body, name=name, grid=(HP, nb), comm=comm, sem=("parallel", "parallel"), args=(qkvT, ka, vtb, augq),
        in_specs=[pl.BlockSpec((128, tb), lambda hp, qi: (hp, qi)),
                  pl.BlockSpec((None, 2, S, 128), lambda hp, qi: (hp, 0, 0, 0)),
                  pl.BlockSpec((None, nb, 128, tb), lambda hp, qi: (hp, 0, 0, 0)),
                  pl.BlockSpec((None, 2, AUG_ROWS, tb), lambda hp, qi: (hp, 0, 0, qi))],
        out_specs=[pl.BlockSpec((128, tb), lambda hp, qi: (hp, qi)), pl.BlockSpec((None, 8, tb), lambda hp, qi: (hp, 0, qi))],
        out_shape=[jax.ShapeDtypeStruct((D_MODEL, S), BF16), jax.ShapeDtypeStruct((HP, 8, S), F32)],
        scratch_shapes=[pltpu.VMEM((2, 1, tb), F32), pltpu.VMEM((2, 1, tb), F32), pltpu.VMEM((2, 128, tb), F32)])


def _foxt_stats(qkvT, qkv, ka, augq, doT, lse, *, tb, name, comm=()):
    S = qkvT.shape[1]
    nb = S // tb

    def body(q_ref, ka_ref, v_ref, aq_ref, do_ref, ls_ref, dl_ref, acc_sc):
        qi = pl.program_id(1)
        qa = _qa_pair(q_ref[...], aq_ref[...])
        doth = [_rows_of_head(do_ref[...], hh) for hh in range(2)]
        lse_row = [ls_ref[hh:hh + 1, :] for hh in range(2)]
        acc_sc[...] = jnp.zeros_like(acc_sc)

        def block(kj, masked):
            k0 = pl.multiple_of(kj * tb, tb)
            v2 = v_ref[pl.ds(k0, tb), :]
            for hh in range(2):
                s = jnp.dot(ka_ref[hh, pl.ds(k0, tb), :], qa[hh], preferred_element_type=F32)
                if masked:
                    s = _diag_mask_t(s)
                dp = jnp.dot(v2, doth[hh], preferred_element_type=F32)
                acc_sc[hh] += jnp.sum(jnp.exp(s - lse_row[hh]) * dp, axis=0, keepdims=True)

        def off_diagonal(kj, c):
            block(kj, False)
            return c

        lax.fori_loop(0, qi, off_diagonal, 0)
        block(qi, True)
        _put_rows(dl_ref, [acc_sc[0], acc_sc[1]])

    qblk = pl.BlockSpec((128, tb), lambda hp, qi: (hp, qi))
    return _call(
        body, name=name, grid=(HP, nb), comm=comm, sem=("parallel", "parallel"), args=(qkvT, ka, qkv, augq, doT, lse),
        in_specs=[qblk, pl.BlockSpec((None, 2, S, 128), lambda hp, qi: (hp, 0, 0, 0)),
                  pl.BlockSpec((S, 128), lambda hp, qi: (0, 2 * HP + hp)),
                  pl.BlockSpec((None, 2, AUG_ROWS, tb), lambda hp, qi: (hp, 0, 0, qi)), qblk,
                  pl.BlockSpec((None, 8, tb), lambda hp, qi: (hp, 0, qi))],
        out_specs=[pl.BlockSpec((None, 8, tb), lambda hp, qi: (hp, 0, qi))],
        out_shape=[jax.ShapeDtypeStruct((HP, 8, S), F32)],
        scratch_shapes=[pltpu.VMEM((2, 1, tb), F32)])


def _foxt_bwd(qkvT, qkv, ka, qtb, dotb, do, augqb, lseb, deltab, *, tb, name):
    S = qkv.shape[0]
    nb = S // tb

    def body(ka_ref, v_ref, kt_ref, qt_ref, dot_ref, qn_ref, dn_ref, aq_ref, ls_ref, dl_ref,
             dq_ref, dk_ref, dv_ref, df_ref, dq_sc, dk_sc, dv_sc, ds_sc):
        kj = pl.program_id(1)

        @pl.when(kj == 0)
        def _():
            dq_sc[...] = jnp.zeros_like(dq_sc)

        dk_sc[...] = jnp.zeros_like(dk_sc)
        dv_sc[...] = jnp.zeros_like(dv_sc)
        ds_sc[...] = jnp.zeros_like(ds_sc)
        v2 = v_ref[...]
        kth = [_rows_of_head(kt_ref[...], hh) for hh in range(2)]

        def block(qi, masked):
            q0 = pl.multiple_of(qi * tb, tb)
            qa = _qa_pair(qt_ref[qi], aq_ref[qi])
            dot, dl, ls = dot_ref[qi], dl_ref[qi], ls_ref[qi]
            qn, dn = qn_ref[pl.ds(q0, tb), :], dn_ref[pl.ds(q0, tb), :]
            dq_new = dq_sc[qi]
            for hh in range(2):
                s = jnp.dot(ka_ref[hh], qa[hh], preferred_element_type=F32)
                if masked:
                    s = _diag_mask_t(s)
                p = jnp.exp(s - ls[hh:hh + 1, :])
                dp = jnp.dot(v2, _rows_of_head(dot, hh), preferred_element_type=F32)
                ds = p * (dp - dl[hh:hh + 1, :])
                dsb = ds.astype(BF16)
                dv_sc[...] += jnp.dot(p.astype(BF16), _head_mask(dn, hh), preferred_element_type=F32)
                dk_sc[...] += jnp.dot(dsb, _head_mask(qn, hh), preferred_element_type=F32)
                dq_new = dq_new + jnp.dot(kth[hh], dsb, preferred_element_type=F32)
                part = ds[:, 0:128]
                for j in range(1, tb // 128):
                    part = part + ds[:, 128 * j:128 * (j + 1)]
                ds_sc[hh] += part
            dq_sc[qi] = dq_new

        def off_diagonal(i, c):
            block(kj + 1 + i, False)
            return c

        block(kj, True)
        lax.fori_loop(0, nb - 1 - kj, off_diagonal, 0)
        dk_ref[...] = dk_sc[...].astype(dk_ref.dtype)
        dv_ref[...] = dv_sc[...].astype(dv_ref.dtype)
        df_ref[...] = _lanes_from([-jnp.sum(ds_sc[hh], axis=1, keepdims=True) for hh in range(2)], (tb, 128))

        @pl.when(kj == nb - 1)
        def _():
            dq_ref[...] = (dq_sc[...] * Q_SCALE).astype(dq_ref.dtype)

    resident = lambda rows: pl.BlockSpec((None, nb) + rows, lambda hp, kj: (hp,) + (0,) * (len(rows) + 1))
    whole = pl.BlockSpec((S, 128), lambda hp, kj: (0, hp))
    kblk = lambda off: pl.BlockSpec((tb, 128), lambda hp, kj: (kj, off + hp))
    return pl.pallas_call(
        body, name=name, grid=(HP, nb),
        in_specs=[pl.BlockSpec((None, 2, tb, 128), lambda hp, kj: (hp, 0, kj, 0)), kblk(2 * HP),
                  pl.BlockSpec((128, tb), lambda hp, kj: (HP + hp, kj)),
                  resident((128, tb)), resident((128, tb)), whole, whole, resident((2, AUG_ROWS, tb)), resident((8, tb)),
                  resident((8, tb))],
        out_specs=[resident((128, tb)), kblk(0), kblk(0), pl.BlockSpec((None, tb, 128), lambda hp, kj: (hp, kj, 0))],
        out_shape=[jax.ShapeDtypeStruct((HP, nb, 128, tb), BF16), jax.ShapeDtypeStruct((S, D_MODEL), BF16),
                   jax.ShapeDtypeStruct((S, D_MODEL), BF16), jax.ShapeDtypeStruct((HP, S, 128), F32)],
        scratch_shapes=[pltpu.VMEM((nb, 128, tb), F32), pltpu.VMEM((tb, 128), F32), pltpu.VMEM((tb, 128), F32),
                        pltpu.VMEM((2, tb, 128), F32)],
        compiler_params=_cp("parallel", "arbitrary"))(ka, qkv, qkvT, qtb, dotb, qkv, do, augqb, lseb, deltab)


def _split3(x):
    rnd = lambda v: lax.reduce_precision(v, exponent_bits=8, mantissa_bits=7)
    hi = rnd(x)
    mid = rnd(x - hi)
    lo = rnd(x - hi - mid)
    return hi.astype(BF16), mid.astype(BF16), lo.astype(BF16)


def _blocked(xt, nb, tb):
    lead = xt.shape[:-1]
    x = xt.reshape(lead + (nb, tb))
    return jnp.moveaxis(x, -2, 1)


def _aug_rows(bias):
    ones = jnp.ones(bias.shape, BF16)
    zeros = jnp.zeros(bias.shape, BF16)
    return jnp.stack(list(_split3(bias)) + [ones] * 3 + [zeros] * (AUG_ROWS - 6), axis=2)


def _fox_keys(qkv, fcol, *, name):
    S = qkv.shape[0]
    tr = _tile(S, 512, 16)

    def body(k_ref, f_ref, o_ref):
        hp = pl.program_id(0)
        k2 = k_ref[...]
        lanes = lax.broadcasted_iota(jnp.int32, (tr, 128), 1)
        for hh in range(2):
            nf = -jnp.sum(jnp.where(lanes == 2 * hp + hh, f_ref[...], 0.0), axis=1, keepdims=True)
            hi = nf.astype(BF16).astype(F32)
            mid = (nf - hi).astype(BF16).astype(F32)
            lo = nf - hi - mid
            base = HEAD_DIM * (1 - hh)
            aug = jnp.where((lanes >= base) & (lanes < base + 3), 1.0, 0.0)
            for j, piece in enumerate((hi, mid, lo)):
                aug = jnp.where(lanes == base + 3 + j, piece, aug)
            mine = (lanes >= HEAD_DIM * hh) & (lanes < HEAD_DIM * (hh + 1))
            o_ref[hh] = jnp.where(mine, k2, aug.astype(BF16))

    return pl.pallas_call(
        body, name=name, grid=(HP, S // tr),
        in_specs=[pl.BlockSpec((tr, 128), lambda hp, i: (i, HP + hp)), pl.BlockSpec((tr, 128), lambda hp, i: (i, 0))],
        out_specs=pl.BlockSpec((None, 2, tr, 128), lambda hp, i: (hp, 0, i, 0)),
        out_shape=jax.ShapeDtypeStruct((HP, 2, S, 128), BF16), compiler_params=_cp("parallel", "parallel"))(qkv, fcol)


HALF = S5_SUB * S5_P
NCB = 2 * HALF // 128
RE, IM = slice(0, NCB // 2), slice(NCB // 2, NCB)


def _gelu(x):
    return 0.5 * x * (1.0 + jnp.tanh(GELU_K * (x + 0.044715 * x * x * x)))


def _gelu_grad(x):
    t = jnp.tanh(GELU_K * (x + 0.044715 * x * x * x))
    return 0.5 * (1.0 + t) + 0.5 * x * (1.0 - t * t) * GELU_K * (1.0 + 3.0 * 0.044715 * x * x)


def _s5_put(buf, s, val, lt):
    for cb in range(NCB):
        buf[cb, pl.ds(s, lt, stride=S5_SUB), :] = val[:, 128 * cb:128 * (cb + 1)]


def _s5_get(buf, s, lt):
    return jnp.concatenate([buf[cb, pl.ds(s, lt, stride=S5_SUB), :] for cb in range(NCB)], axis=1)


def _s5_project_in(u_ref, wb_ref, buf, lt):
    for s in range(S5_SUB):
        _s5_put(buf, s, jnp.dot(u_ref[:, 128 * s:128 * (s + 1)], wb_ref[s], preferred_element_type=F32), lt)


def _s5_scan(buf, lam_ref, h0, lt):
    a_re, a_im = lam_ref[RE], lam_ref[IM]

    def step(t, carry):
        hr, hi = carry
        r0 = pl.multiple_of(t * S5_SUB, S5_SUB)
        nr = a_re * hr - a_im * hi + buf[RE, pl.ds(r0, S5_SUB), :]
        ni = a_re * hi + a_im * hr + buf[IM, pl.ds(r0, S5_SUB), :]
        buf[RE, pl.ds(r0, S5_SUB), :] = nr
        buf[IM, pl.ds(r0, S5_SUB), :] = ni
        return nr, ni

    return lax.fori_loop(0, lt, step, (h0[RE], h0[IM]), unroll=4)


def _s5_fwd(u, wb, wc, lam, dskip, *, lt, name, comm=()):
    S, D = u.shape
    nt = S // lt

    def body(u_ref, wb_ref, wc_ref, lam_ref, d_ref, yp_ref, yg_ref, st_ref, buf, h_sc):
        @pl.when(pl.program_id(0) == 0)
        def _():
            h_sc[...] = jnp.zeros_like(h_sc)

        st_ref[...] = h_sc[...]
        _s5_project_in(u_ref, wb_ref, buf, lt)
        hr, hi = _s5_scan(buf, lam_ref, h_sc[...], lt)
        h_sc[RE] = hr
        h_sc[IM] = hi
        for s in range(S5_SUB):
            cols = slice(128 * s, 128 * (s + 1))
            hs = _s5_get(buf, s, lt).astype(BF16)
            yp = jnp.dot(hs, wc_ref[s], preferred_element_type=F32) + d_ref[:, cols] * u_ref[:, cols].astype(F32)
            yp_ref[:, cols] = yp.astype(yp_ref.dtype)
            yg_ref[:, cols] = _gelu(yp).astype(yg_ref.dtype)

    blk = pl.BlockSpec((lt, D), lambda i: (i, 0))
    full = lambda shp: pl.BlockSpec(shp, lambda i: (0,) * len(shp))
    state = (NCB, S5_SUB, 128)
    return _call(
        body, name=name, grid=(nt,), comm=comm, sem=("arbitrary",), args=(u, wb, wc, lam, dskip),
        in_specs=[blk, full(wb.shape), full(wc.shape), full(lam.shape), full(dskip.shape)],
        out_specs=[blk, blk, pl.BlockSpec((None,) + state, lambda i: (i, 0, 0, 0))],
        out_shape=[jax.ShapeDtypeStruct((S, D), BF16), jax.ShapeDtypeStruct((S, D), BF16), jax.ShapeDtypeStruct((nt,) + state, F32)],
        scratch_shapes=[pltpu.VMEM((NCB, lt * S5_SUB, 128), F32), pltpu.VMEM(state, F32)])


def _s5_bwd(u, dyp, st, wb, wc, lam, dskip, *, lt, name, comm=()):
    S, D = u.shape
    nt = S // lt

    def body(u_ref, dy_ref, st_ref, wb_ref, wc_ref, lam_ref, d_ref, du_ref, dwb_ref, dwc_ref, dlam_ref, hbuf, gbuf, g_sc):
        @pl.when(pl.program_id(0) == 0)
        def _():
            g_sc[...] = jnp.zeros_like(g_sc)
            dwb_ref[...] = jnp.zeros_like(dwb_ref)
            dwc_ref[...] = jnp.zeros_like(dwc_ref)
            dlam_ref[...] = jnp.zeros_like(dlam_ref)

        _s5_project_in(u_ref, wb_ref, hbuf, lt)
        _s5_scan(hbuf, lam_ref, st_ref[...], lt)
        for s in range(S5_SUB):
            dys = dy_ref[:, 128 * s:128 * (s + 1)]
            _s5_put(gbuf, s, lax.dot_general(dys, wc_ref[s], NT_DIMS, preferred_element_type=F32), lt)
        a_re, a_im = lam_ref[RE], lam_ref[IM]

        def one(t, carry, hp_re, hp_im):
            gr, gi, dar, dai = carry
            r0 = pl.multiple_of(t * S5_SUB, S5_SUB)
            nr = gbuf[RE, pl.ds(r0, S5_SUB), :] + a_re * gr + a_im * gi
            ni = gbuf[IM, pl.ds(r0, S5_SUB), :] + a_re * gi - a_im * gr
            gbuf[RE, pl.ds(r0, S5_SUB), :] = nr
            gbuf[IM, pl.ds(r0, S5_SUB), :] = ni
            return nr, ni, dar + nr * hp_re + ni * hp_im, dai + ni * hp_re - nr * hp_im

        def step(k, carry):
            t = lt - 1 - k
            p0 = pl.multiple_of((t - 1) * S5_SUB, S5_SUB)
            return one(t, carry, hbuf[RE, pl.ds(p0, S5_SUB), :], hbuf[IM, pl.ds(p0, S5_SUB), :])

        init = (g_sc[RE], g_sc[IM], dlam_ref[RE], dlam_ref[IM])
        carry = lax.fori_loop(0, lt - 1, step, init, unroll=2)
        gr, gi, dar, dai = one(0, carry, st_ref[RE], st_ref[IM])
        g_sc[RE] = gr
        g_sc[IM] = gi
        dlam_ref[RE] = dar
        dlam_ref[IM] = dai
        for s in range(S5_SUB):
            cols = slice(128 * s, 128 * (s + 1))
            gs = _s5_get(gbuf, s, lt).astype(BF16)
            hs = _s5_get(hbuf, s, lt).astype(BF16)
            us, dys = u_ref[:, cols], dy_ref[:, cols]
            du = lax.dot_general(gs, wb_ref[s], NT_DIMS, preferred_element_type=F32) + d_ref[:, cols] * dys.astype(F32)
            du_ref[:, cols] = du.astype(du_ref.dtype)
            dwb_ref[s] += lax.dot_general(us, gs, TN_DIMS, preferred_element_type=F32)
            dwc_ref[s] += lax.dot_general(hs, dys, TN_DIMS, preferred_element_type=F32)

    blk = pl.BlockSpec((lt, D), lambda i: (nt - 1 - i, 0))
    full = lambda shp: pl.BlockSpec(shp, lambda i: (0,) * len(shp))
    state = (NCB, S5_SUB, 128)
    words = pltpu.VMEM((NCB, lt * S5_SUB, 128), F32)
    return _call(
        body, name=name, grid=(nt,), comm=comm, sem=("arbitrary",), args=(u, dyp, st, wb, wc, lam, dskip),
        in_specs=[blk, blk, pl.BlockSpec((None,) + state, lambda i: (nt - 1 - i, 0, 0, 0)),
                  full(wb.shape), full(wc.shape), full(lam.shape), full(dskip.shape)],
        out_specs=[blk, full(wb.shape), full(wc.shape), full(lam.shape)],
        out_shape=[jax.ShapeDtypeStruct((S, D), BF16), jax.ShapeDtypeStruct(wb.shape, F32), jax.ShapeDtypeStruct(wc.shape, F32),
                   jax.ShapeDtypeStruct(lam.shape, F32)],
        scratch_shapes=[words, words, pltpu.VMEM(state, F32)])


def _s5_disc(lam_re, lam_im, log_dt, b_re, b_im):
    dt = jnp.exp(log_dt)[:, None]
    mag = jnp.exp(lam_re * dt)
    lb_re, lb_im = mag * jnp.cos(lam_im * dt), mag * jnp.sin(lam_im * dt)
    num_re = lb_re - 1.0
    den = lam_re * lam_re + lam_im * lam_im
    k_re = (num_re * lam_re + lb_im * lam_im) / den
    k_im = (lb_im * lam_re - num_re * lam_im) / den
    bb_re = k_re[..., None] * b_re - k_im[..., None] * b_im
    bb_im = k_re[..., None] * b_im + k_im[..., None] * b_re
    return lb_re, lb_im, bb_re, bb_im


def _s5_pack(lb_re, lb_im, bb_re, bb_im, c_re, c_im):
    eye = jnp.eye(S5_SUB, dtype=F32)
    tb = lambda x: jnp.einsum("sgcp,gh->sgchp", x.reshape(S5_SUB, S5_SUB, S5_P, S5_C).transpose(0, 1, 3, 2), eye).reshape(S5_SUB, 128, HALF)
    tc = lambda x: jnp.einsum("sgpc,gh->sgphc", x.reshape(S5_SUB, S5_SUB, S5_C, S5_P).transpose(0, 1, 3, 2), eye).reshape(S5_SUB, HALF, 128)
    wb = jnp.concatenate([tb(bb_re), tb(bb_im)], axis=2).astype(BF16)
    wc = jnp.concatenate([tc(c_re), -tc(c_im)], axis=1).astype(BF16)
    lam = jnp.concatenate([lb_re.reshape(S5_SUB, HALF), lb_im.reshape(S5_SUB, HALF)], axis=1)
    return wb, wc, lam.reshape(S5_SUB, NCB, 128).transpose(1, 0, 2)


def _s5_unpack(dwb, dwc, dlam):
    db = jnp.einsum("sgcrgp->rsgpc", dwb.reshape(S5_SUB, S5_SUB, S5_C, 2, S5_SUB, S5_P)).reshape(2, S5_G, S5_P, S5_C)
    dc = jnp.einsum("srgpgc->rsgcp", dwc.reshape(S5_SUB, 2, S5_SUB, S5_P, S5_SUB, S5_C)).reshape(2, S5_G, S5_C, S5_P)
    dlam = dlam.transpose(1, 0, 2).reshape(S5_SUB, 2 * HALF)
    return (dlam[:, :HALF].reshape(S5_G, S5_P), dlam[:, HALF:].reshape(S5_G, S5_P), db[0], db[1], dc[0], -dc[1])


def _glu_fwd(yg, gl, *, name):
    S, D = yg.shape
    tm = _tile(S, 512, 8)
    blk = pl.BlockSpec((tm, D), lambda i: (i, 0))

    def body(y_ref, g_ref, o_ref):
        o_ref[...] = (y_ref[...].astype(F32) * jax.nn.sigmoid(g_ref[...].astype(F32))).astype(o_ref.dtype)

    return pl.pallas_call(body, name=name, grid=(S // tm,), in_specs=[blk, blk], out_specs=blk,
                          out_shape=jax.ShapeDtypeStruct((S, D), BF16), compiler_params=_cp("parallel"))(yg, gl)


def _glu_bwd(dy2, yg, gl, *, name):
    S, D = yg.shape
    tm = _tile(S, 512, 8)
    blk = pl.BlockSpec((tm, D), lambda i: (i, 0))

    def body(d_ref, y_ref, g_ref, da_ref, dg_ref):
        d = d_ref[...].astype(F32)
        sg = jax.nn.sigmoid(g_ref[...].astype(F32))
        da_ref[...] = d * sg
        dg_ref[...] = (d * y_ref[...].astype(F32) * sg * (1.0 - sg)).astype(dg_ref.dtype)

    return pl.pallas_call(body, name=name, grid=(S // tm,), in_specs=[blk, blk, blk], out_specs=[blk, blk],
                          out_shape=[jax.ShapeDtypeStruct((S, D), F32), jax.ShapeDtypeStruct((S, D), BF16)],
                          compiler_params=_cp("parallel"))(dy2, yg, gl)


def _gelu_bwd(da, db, ypre, u, *, name):
    S, D = ypre.shape
    tm = _tile(S, 512, 8)
    blk = pl.BlockSpec((tm, D), lambda i: (i, 0))
    vec = pl.BlockSpec((1, D), lambda i: (0, 0))

    def body(a_ref, b_ref, y_ref, u_ref, o_ref, s_ref):
        @pl.when(pl.program_id(0) == 0)
        def _():
            s_ref[...] = jnp.zeros_like(s_ref)

        dy = (a_ref[...] + b_ref[...]) * _gelu_grad(y_ref[...].astype(F32))
        o_ref[...] = dy.astype(o_ref.dtype)
        s_ref[...] += jnp.sum(dy * u_ref[...].astype(F32), axis=0, keepdims=True)

    return pl.pallas_call(body, name=name, grid=(S // tm,), in_specs=[blk, blk, blk, blk], out_specs=[blk, vec],
                          out_shape=[jax.ShapeDtypeStruct((S, D), BF16), jax.ShapeDtypeStruct((1, D), F32)],
                          compiler_params=_cp("arbitrary"))(da, db, ypre, u)


def _my_index():
    return 4 * lax.axis_index("x") + 2 * lax.axis_index("y") + lax.axis_index("c")


def _comm_plan(payloads):
    n = len(payloads)
    hbm = pl.BlockSpec(memory_space=pl.ANY)
    shapes = [jax.ShapeDtypeStruct((N_DEV,) + (x.shape if g else x.shape[1:]), x.dtype) for x, g in payloads]
    sems = [pltpu.SemaphoreType.DMA((n, N_DEV - 1)), pltpu.SemaphoreType.DMA((n, N_DEV - 1)), pltpu.SemaphoreType.DMA((n,))] if n else []
    return [hbm] * n, [hbm] * n, shapes, sems


def _comm_copies(x_refs, o_refs, gathers, send_sems, recv_sems, local_sems):
    mx, my, mc = lax.axis_index("x"), lax.axis_index("y"), lax.axis_index("c")
    me = 4 * mx + 2 * my + mc
    copies = []
    for i, (x_ref, o_ref, gather) in enumerate(zip(x_refs, o_refs, gathers)):
        src = (lambda j, r=x_ref: r) if gather else (lambda j, r=x_ref: r.at[j])
        copies.append(pltpu.make_async_copy(src(me), o_ref.at[me], local_sems.at[i]))
        for k in range(1, N_DEV):
            px, py, pc = mx ^ (k >> 2), my ^ ((k >> 1) & 1), mc ^ (k & 1)
            copies.append(pltpu.make_async_remote_copy(src(4 * px + 2 * py + pc), o_ref.at[me], send_sems.at[i, k - 1],
                                                       recv_sems.at[i, k - 1], device_id=(px, py, pc), device_id_type=MESH_ID))
    return copies


def _call(body, *, name, grid, in_specs, out_specs, out_shape, scratch_shapes, sem, args, comm=()):
    if not comm:
        outs = pl.pallas_call(body, name=name, grid=grid, in_specs=in_specs, out_specs=out_specs, out_shape=out_shape,
                              scratch_shapes=scratch_shapes, compiler_params=_cp(*sem))(*args)
        return outs, ()
    n, n_in, n_out = len(comm), len(in_specs), len(out_specs)
    c_in, c_out, c_shapes, c_sems = _comm_plan(comm)
    gathers = [g for _, g in comm]

    def wrapped(*refs):
        own_in, x_refs = refs[:n_in], refs[n_in:n_in + n]
        own_out, o_refs = refs[n_in + n:n_in + n + n_out], refs[n_in + n + n_out:n_in + 2 * n + n_out]
        own_scratch, sems = refs[n_in + 2 * n + n_out:-3], refs[-3:]
        ids = [pl.program_id(a) for a in range(len(grid))]
        first = functools.reduce(jnp.logical_and, [i == 0 for i in ids])
        last = functools.reduce(jnp.logical_and, [i == g - 1 for i, g in zip(ids, grid)])

        @pl.when(first)
        def _():
            for cp in _comm_copies(x_refs, o_refs, gathers, *sems):
                cp.start()

        body(*own_in, *own_out, *own_scratch)

        @pl.when(last)
        def _():
            for cp in _comm_copies(x_refs, o_refs, gathers, *sems):
                cp.wait()

    outs = pl.pallas_call(wrapped, name=name, grid=grid, in_specs=list(in_specs) + c_in, out_specs=list(out_specs) + c_out,
                          out_shape=list(out_shape) + c_shapes, scratch_shapes=list(scratch_shapes) + c_sems,
                          compiler_params=_cp(*["arbitrary"] * len(grid)))(*args, *[x for x, _ in comm])
    return outs[:n_out], outs[n_out:]


def _exchange(x, *, gather, name):
    in_specs, out_specs, shapes, sems = _comm_plan([(x, gather)])

    def body(x_ref, o_ref, send_sems, recv_sems, local_sems):
        copies = _comm_copies([x_ref], [o_ref], [gather], send_sems, recv_sems, local_sems)
        for cp in copies:
            cp.start()
        for cp in copies:
            cp.wait()

    return pl.pallas_call(body, name=name, in_specs=in_specs, out_specs=out_specs[0], out_shape=shapes[0], scratch_shapes=sems,
                          compiler_params=pltpu.CompilerParams(has_side_effects=True))(x)


def _sum8(x8, *, name):
    _, R, C = x8.shape
    tr = _tile(R, 256, 16)

    def body(x_ref, o_ref):
        acc = x_ref[0].astype(F32)
        for j in range(1, N_DEV):
            acc = acc + x_ref[j].astype(F32)
        o_ref[...] = acc

    return pl.pallas_call(body, name=name, grid=(R // tr,), in_specs=[pl.BlockSpec((N_DEV, tr, C), lambda i: (0, i, 0))],
                          out_specs=pl.BlockSpec((tr, C), lambda i: (i, 0)), out_shape=jax.ShapeDtypeStruct((R, C), F32),
                          compiler_params=_cp("parallel"))(x8)


def _pack(arrs, dtype, row_mult):
    flat = jnp.concatenate([a.reshape(-1).astype(dtype) for a in arrs])
    n = flat.shape[0]
    rows = -(-n // 1024)
    rows = -(-rows // row_mult) * row_mult
    return jnp.pad(flat, (0, rows * 1024 - n)).reshape(rows, 1024)


def _unpack(buf, shapes):
    lead = buf.shape[:-2]
    flat = buf.reshape(lead + (-1,))
    out, off = [], 0
    for s in shapes:
        n = math.prod(s)
        out.append(flat[..., off:off + n].reshape(lead + tuple(s)))
        off += n
    return out


def _adaln_fwd(c_all, w, b, *, name):
    nk, D, n = w.shape

    def body(c_ref, w_ref, b_ref, o_ref):
        cv = c_ref[...]
        sc = (cv * jax.nn.sigmoid(cv)).astype(BF16)
        o_ref[...] = jnp.dot(sc, w_ref[...].astype(BF16), preferred_element_type=F32) + b_ref[...]

    return pl.pallas_call(body, name=name, grid=(nk,),
                          in_specs=[pl.BlockSpec((N_DEV, D), lambda k: (0, 0)), pl.BlockSpec((None, D, n), lambda k: (k, 0, 0)),
                                    pl.BlockSpec((None, 1, n), lambda k: (k, 0, 0))],
                          out_specs=pl.BlockSpec((None, N_DEV, n), lambda k: (k, 0, 0)),
                          out_shape=jax.ShapeDtypeStruct((nk, N_DEV, n), F32), compiler_params=_cp("parallel"))(c_all, w, b)


def _adaln_bwd(c_all, dmod, *, name):
    nk, _, n = dmod.shape
    D = c_all.shape[1]

    def body(c_ref, d_ref, dw_ref, db_ref):
        cv = c_ref[...]
        sc = (cv * jax.nn.sigmoid(cv)).astype(BF16).astype(F32)
        dm = d_ref[...]
        dw_ref[...] = lax.dot_general(sc, dm.astype(BF16).astype(F32), TN_DIMS, precision=lax.Precision.HIGHEST,
                                      preferred_element_type=F32)
        db_ref[...] = jnp.sum(dm, axis=0, keepdims=True)

    return pl.pallas_call(body, name=name, grid=(nk,),
                          in_specs=[pl.BlockSpec((N_DEV, D), lambda k: (0, 0)), pl.BlockSpec((None, N_DEV, n), lambda k: (k, 0, 0))],
                          out_specs=[pl.BlockSpec((None, D, n), lambda k: (k, 0, 0)), pl.BlockSpec((None, 1, n), lambda k: (k, 0, 0))],
                          out_shape=[jax.ShapeDtypeStruct((nk, D, n), F32), jax.ShapeDtypeStruct((nk, 1, n), F32)],
                          compiler_params=_cp("parallel"))(c_all, dmod)


def _adamw(w, g, m, v, *, name):
    R, C = w.shape
    tr = _tile(R, 256, 8)
    blk = pl.BlockSpec((tr, C), lambda i: (i, 0))
    c1 = 1.0 / (1.0 - ADAM_B1 ** ADAM_STEP)
    c2 = 1.0 / (1.0 - ADAM_B2 ** ADAM_STEP)

    def body(w_ref, g_ref, m_ref, v_ref, d_ref, nm_ref, nv_ref):
        gv = g_ref[...]
        nm = ADAM_B1 * m_ref[...] + (1.0 - ADAM_B1) * gv
        nv = ADAM_B2 * v_ref[...] + (1.0 - ADAM_B2) * (gv * gv)
        nm_ref[...] = nm
        nv_ref[...] = nv
        d_ref[...] = -ADAM_LR * ((nm * c1) / (jnp.sqrt(nv * c2) + ADAM_EPS) + ADAM_WD * w_ref[...])

    sh = jax.ShapeDtypeStruct((R, C), F32)
    return pl.pallas_call(body, name=name, grid=(R // tr,), in_specs=[blk] * 4, out_specs=[blk] * 3, out_shape=[sh] * 3,
                          compiler_params=_cp("parallel"))(w, g, m, v)


def _adamw_nd(w, g, m, v, *, name):
    shp = w.shape
    two = (-1, shp[-1]) if w.ndim > 1 else (1, -1)
    outs = _adamw(w.reshape(two), g.reshape(two), m.reshape(two), v.reshape(two), name=name)
    return [o.reshape(shp) for o in outs]


S5_TILE = 256
ATTN_BLOCK = 512
BIG = ("s5_w_in", "s5_w_glu", "s5_w_out", "fox_w_in", "fox_w_out", "ffn_w_up", "ffn_w_down")
WEIGHTS = ("norm_g", "ada_w", "ada_b", "s5_w_in", "s5_lam_re", "s5_lam_im", "s5_log_dt", "s5_b_re", "s5_b_im", "s5_c_re",
           "s5_c_im", "s5_d", "s5_w_glu", "s5_w_out", "fox_w_in", "fox_b_f", "fox_w_out", "ffn_w_up", "ffn_conv_w",
           "ffn_conv_b", "ffn_w_down", "final_g")
REPLICATED = ("s5_lam_re", "s5_lam_im", "s5_log_dt", "s5_b_re", "s5_b_im", "s5_c_re", "s5_c_im", "s5_d", "fox_b_f",
              "ffn_conv_b", "final_g")


S5_MATS = ("s5_w_in", "s5_w_glu", "s5_w_out")


def _row_group(arrs):
    return jnp.concatenate([a.reshape(-1, D_MODEL) for a in arrs], axis=0)


def _cols_join(g):
    return jnp.concatenate([g[j] for j in range(N_DEV)], axis=1)


def _cols_split(full):
    n = full.shape[1] // N_DEV
    return jnp.stack([full[:, j * n:(j + 1) * n] for j in range(N_DEV)])


def kernel(x, c, norm_g, ada_w, ada_b, s5_w_in, s5_lam_re, s5_lam_im, s5_log_dt, s5_b_re, s5_b_im, s5_c_re, s5_c_im, s5_d, s5_w_glu, s5_w_out, fox_w_in, fox_b_f, fox_w_out, ffn_w_up, ffn_conv_w, ffn_conv_b, ffn_w_down, final_g, loss_target, m_norm_g, m_ada_w, m_ada_b, m_s5_w_in, m_s5_lam_re, m_s5_lam_im, m_s5_log_dt, m_s5_b_re, m_s5_b_im, m_s5_c_re, m_s5_c_im, m_s5_d, m_s5_w_glu, m_s5_w_out, m_fox_w_in, m_fox_b_f, m_fox_w_out, m_ffn_w_up, m_ffn_conv_w, m_ffn_conv_b, m_ffn_w_down, m_final_g, v_norm_g, v_ada_w, v_ada_b, v_s5_w_in, v_s5_lam_re, v_s5_lam_im, v_s5_log_dt, v_s5_b_re, v_s5_b_im, v_s5_c_re, v_s5_c_im, v_s5_d, v_s5_w_glu, v_s5_w_out, v_fox_w_in, v_fox_b_f, v_fox_w_out, v_ffn_w_up, v_ffn_conv_w, v_ffn_conv_b, v_ffn_w_down, v_final_g):
    args = dict(locals())
    W = {n: args[n] for n in WEIGHTS}
    M = {n: args["m_" + n] for n in WEIGHTS}
    V = {n: args["v_" + n] for n in WEIGHTS}
    D, F = D_MODEL, D_FF
    me = _my_index()
    h0 = x[0]
    S = h0.shape[0]
    lt = min(S5_TILE, S)
    tb = min(ATTN_BLOCK, S)
    n_g, n_cw = norm_g.size, ffn_conv_w.size

    g0 = _exchange(_pack([c, norm_g, ffn_conv_w], F32, 8), gather=True, name="gather_small_in")
    c_all, ng_all, cw_all = _unpack(g0, [(D,), norm_g.shape, ffn_conv_w.shape])
    ng_full = ng_all.transpose(1, 2, 0, 3).reshape(2, 2, D)
    cw_full = cw_all.transpose(1, 2, 0, 3).reshape(2, 3, F)

    ncol = ada_w.shape[-1]
    modp = _adaln_fwd(c_all, ada_w.reshape(4, D, ncol), ada_b.reshape(4, 1, ncol), name="adaln_fwd")
    g1 = _exchange(_pack([modp], F32, 8), gather=True, name="gather_adaln")
    (mod_all,) = _unpack(g1, [modp.shape])
    mod = lax.dynamic_index_in_dim(mod_all, me, axis=2, keepdims=False).transpose(1, 0, 2).reshape(4, 3 * D)
    shift = [mod[k:k + 1, :D] for k in range(4)]
    scale = [mod[k:k + 1, D:2 * D] for k in range(4)]
    gate = [mod[k:k + 1, 2 * D:] for k in range(4)]
    gain = [_row(ng_full[k // 2, k % 2]) for k in range(4)]

    b16 = lambda a: a.astype(BF16)
    rows = D // N_DEV
    g_s5 = _exchange(_row_group([b16(W[n]) for n in S5_MATS]), gather=True, name="gather_s5_weights")
    full = {n: g_s5[:, i * rows:(i + 1) * rows].reshape(D, D) for i, n in enumerate(S5_MATS)}
    bf_pad = jnp.pad(fox_b_f, ((0, 0), (0, 128 - HEADS)))

    def ffn_fwd(h, k, layer):
        hn = _modulate(h, gain[k], shift[k], scale[k], name=f"modulate{k}")
        up = _mm(hn, full[f"ffn_w_up{layer}"], name=f"ffn_up{layer}", o_h=True, out_dtype=BF16, tn=1408)
        z = _conv_gate_fwd(up, cw_full[layer], ffn_conv_b[layer:layer + 1], name=f"conv_gate{layer}")
        m, h_out = _mm(z, full[f"ffn_w_down{layer}"], name=f"ffn_down{layer}", res=h, gate=gate[k])
        return h_out, (hn, up, z, m)

    lb_re, lb_im, bb_re, bb_im = _s5_disc(s5_lam_re[0], s5_lam_im[0], s5_log_dt[0], s5_b_re[0], s5_b_im[0])
    wb, wc, lam = _s5_pack(lb_re, lb_im, bb_re, bb_im, s5_c_re[0], s5_c_im[0])
    hn0 = _modulate(h0, gain[0], shift[0], scale[0], name="modulate0")
    u = _mm(hn0, full["s5_w_in"], name="s5_in", out_dtype=BF16)
    (ypre, yg, st), (g_up0, g_rows0, g_fox) = _s5_fwd(
        u, wb, wc, lam, s5_d, lt=lt, name="s5_scan",
        comm=[(b16(ffn_w_up[0]), True), (_row_group([b16(fox_w_out), b16(ffn_w_down[0])]), True), (b16(fox_w_in[0]), True)])
    full["ffn_w_up0"] = _cols_join(g_up0)
    full["fox_w_out"] = g_rows0[:, :rows].reshape(D, D)
    full["ffn_w_down0"] = g_rows0[:, rows:].reshape(F, D)
    full["fox_w_in"] = _cols_join(g_fox)
    w_proj = jnp.pad(full["fox_w_in"], ((0, 0), (0, 3 * D + 128 - full["fox_w_in"].shape[1])))
    w_qkv, w_f = w_proj[:, :3 * D], w_proj[:, 3 * D:]
    gl = _mm(yg, full["s5_w_glu"], name="s5_glu", out_dtype=BF16)
    y2 = _glu_fwd(yg, gl, name="s5_glu_gate")
    m0, h1 = _mm(y2, full["s5_w_out"], name="s5_out", res=h0, gate=gate[0])
    h2, (hn1, up0, z0, m1) = ffn_fwd(h1, 1, 0)

    hn2 = _modulate(h2, gain[2], shift[2], scale[2], name="modulate2")
    nb = S // tb
    w_qkv_s = jnp.concatenate([w_qkv[:, :D] * Q_SCALE, w_qkv[:, D:]], axis=1)
    qkv = _mm(hn2, w_qkv_s, name="fox_qkv", out_dtype=BF16)
    qkvT = _mm(w_qkv_s, hn2, name="fox_qkv_t", ta=True, tb=True, out_dtype=BF16)
    fl = _mm(hn2, w_f, name="fox_f", out_dtype=F32)
    fcol = _fgate_fwd(fl, bf_pad, name="fox_fgate")
    f_heads = fcol[:, :HEADS].T.reshape(HP, 2, S)
    ka = _fox_keys(qkv, fcol, name="fox_keys")
    vtb = _blocked(qkvT[2 * D:].reshape(HP, 128, S), nb, tb)
    aug = _aug_rows(f_heads)
    (oT, lse), (g_up1, g_down1) = _foxt_fwd(qkvT, ka, vtb, aug, tb=tb, name="fox_attn",
                                            comm=[(b16(ffn_w_up[1]), True), (b16(ffn_w_down[1]), True)])
    full["ffn_w_up1"] = _cols_join(g_up1)
    full["ffn_w_down1"] = g_down1.reshape(F, D)
    m2, h3 = _mm(oT, full["fox_w_out"], name="fox_out", ta=True, res=h2, gate=gate[2])
    h4, (hn3, up1, z1, m3) = ffn_fwd(h3, 3, 1)

    lblk, dh, d_final_g = _loss_head(h4, _row(final_g), loss_target[0], name="loss_head")
    loss = lax.psum(lblk[0, 0], ("x", "y", "c"))

    G = {}
    dmod = [None] * 4

    def norm_bwd(h, dhn, dh_in, k):
        dh_out, sums = _modulate_bwd(h, dhn, dh_in, gain[k], scale[k], name=f"modulate_bwd{k}")
        return dh_out, sums[0:1], sums[1:2], sums[2]

    def ffn_bwd(dh_in, h, k, layer, saved):
        hn, up, z, m = saved
        dm, dgate = _gate_bwd(dh_in, m, gate[k], name=f"gate_bwd{k}")
        dz = _mm(dm, full[f"ffn_w_down{layer}"], name=f"ffn_down_dx{layer}", tb=True, out_dtype=BF16, tn=1408)
        dw_down = _mm(z, dm, name=f"ffn_down_dw{layer}", ta=True, tm=1408, tn=512, tk=512)
        d_up, cs = _conv_gate_bwd(up, dz, cw_full[layer], ffn_conv_b[layer:layer + 1], name=f"conv_gate_bwd{layer}")
        dhn = _mm(d_up, full[f"ffn_w_up{layer}"], name=f"ffn_up_dx{layer}", tb=True, a_h=True, tk=1408)
        dw_up = _mm(hn, d_up, name=f"ffn_up_dw{layer}", ta=True, b_h=True, tm=1024, tn=1408, tk=512)
        dh_out, dshift, dscale, dg = norm_bwd(h, dhn, dh_in, k)
        dmod[k] = jnp.concatenate([dshift, dscale, dgate], axis=1)
        return dh_out, dg, dw_up, dw_down, cs[0:3], cs[3]

    dh, dg3, dw_up1, dw_down1, dcw1, dcb1 = ffn_bwd(dh, h3, 3, 1, (hn3, up1, z1, m3))

    dm, dgate2 = _gate_bwd(dh, m2, gate[2], name="gate_bwd2")
    do = _mm(dm, full["fox_w_out"], name="fox_out_dx", tb=True, out_dtype=BF16)
    doT = _mm(full["fox_w_out"], dm, name="fox_out_dx_t", tb=True, out_dtype=BF16)
    dw_fox_out = _mm(oT, dm, name="fox_out_dw", tk=512)
    to_rows = lambda g: b16(g).reshape(N_DEV, -1, D)
    (delta,), (r_up1, r_down1) = _foxt_stats(qkvT, qkv, ka, aug, doT, lse, tb=tb, name="fox_attn_stats",
                                             comm=[(_cols_split(b16(dw_up1)), False), (to_rows(dw_down1), False)])
    dqtb, dk, dv, df = _foxt_bwd(qkvT, qkv, ka, _blocked(qkvT[:D].reshape(HP, 128, S), nb, tb),
                                 _blocked(doT.reshape(HP, 128, S), nb, tb), do, _blocked(aug, nb, tb),
                                 _blocked(lse, nb, tb), _blocked(delta, nb, tb), tb=tb, name="fox_attn_bwd")
    dq = dqtb.transpose(1, 3, 0, 2).reshape(S, D)
    dF = jnp.pad(df[:, :, :2].transpose(1, 0, 2).reshape(S, HEADS), ((0, 0), (0, 128 - HEADS)))
    dfl, dbf = _fgate_bwd(dF, fl, bf_pad, name="fox_fgate_bwd")
    dproj = jnp.concatenate([dq, dk, dv, dfl.astype(BF16)], axis=1)
    dhn = _mm(dproj, w_proj, name="fox_in_dx", tb=True, tk=640)
    dw_proj = _mm(hn2, dproj, name="fox_in_dw", ta=True, tn=640, tk=512)
    dw_fox_in = dw_proj[:, :full["fox_w_in"].shape[1]]
    dh, dshift, dscale, dg2 = norm_bwd(h2, dhn, dh, 2)
    dmod[2] = jnp.concatenate([dshift, dscale, dgate2], axis=1)

    dh, dg1, dw_up0, dw_down0, dcw0, dcb0 = ffn_bwd(dh, h1, 1, 0, (hn1, up0, z0, m1))

    dm, dgate0 = _gate_bwd(dh, m0, gate[0], name="gate_bwd0")
    dy2 = _mm(dm, full["s5_w_out"], name="s5_out_dx", tb=True, out_dtype=BF16)
    G["s5_w_out"] = _mm(y2, dm, name="s5_out_dw", ta=True, tk=512)
    da, dgl = _glu_bwd(dy2, yg, gl, name="s5_glu_bwd")
    dyg_b = _mm(dgl, full["s5_w_glu"], name="s5_glu_dx", tb=True)
    G["s5_w_glu"] = _mm(yg, dgl, name="s5_glu_dw", ta=True, tk=512)
    dyp, dd = _gelu_bwd(da, dyg_b, ypre, u, name="s5_gelu_bwd")
    (du, dwb, dwc, dlam), (r_up0, r_rows0, r_fox) = _s5_bwd(
        u, dyp, st, wb, wc, lam, s5_d, lt=lt, name="s5_scan_bwd",
        comm=[(_cols_split(b16(dw_up0)), False), (jnp.concatenate([to_rows(dw_fox_out), to_rows(dw_down0)], axis=1), False),
              (_cols_split(b16(dw_fox_in)), False)])
    dhn = _mm(du, full["s5_w_in"], name="s5_in_dx", tb=True)
    G["s5_w_in"] = _mm(hn0, du, name="s5_in_dw", ta=True, tk=512)
    dh, dshift, dscale, dg0 = norm_bwd(h0, dhn, dh, 0)
    dmod[0] = jnp.concatenate([dshift, dscale, dgate0], axis=1)
    grad_x = dh[None]

    dlb_re, dlb_im, dbb_re, dbb_im, dc_re, dc_im = _s5_unpack(dwb, dwc, dlam)
    _, disc_vjp = jax.vjp(_s5_disc, s5_lam_re[0], s5_lam_im[0], s5_log_dt[0], s5_b_re[0], s5_b_im[0])
    dlam_re, dlam_im, dlog_dt, db_re, db_im = disc_vjp((dlb_re, dlb_im, dbb_re, dbb_im))

    g2 = _exchange(_pack([jnp.concatenate(dmod, axis=0)], F32, 8), gather=True, name="gather_dmod")
    (dmod_all,) = _unpack(g2, [(4, 3 * D)])
    dmod_mine = _columns_of_mod(dmod_all, me, ncol)
    d_ada_w, d_ada_b = _adaln_bwd(c_all, dmod_mine, name="adaln_bwd")
    G["ada_w"] = d_ada_w.reshape(ada_w.shape)
    G["ada_b"] = d_ada_b.reshape(ada_b.shape)

    r_s5 = _exchange(jnp.concatenate([to_rows(G[n]) for n in S5_MATS], axis=1), gather=False, name="scatter_s5_grads")
    s5_sum = _sum8(r_s5, name="sum_s5_grads")
    for i, n in enumerate(S5_MATS):
        G[n] = s5_sum[i * rows:(i + 1) * rows][None]
    rows0_sum = _sum8(r_rows0, name="sum_rows0_grads")
    G["fox_w_out"] = rows0_sum[:rows][None]
    G["ffn_w_down"] = jnp.stack([rows0_sum[rows:], _sum8(r_down1, name="sum_down1_grads")])
    G["ffn_w_up"] = jnp.stack([_sum8(r_up0, name="sum_up0_grads"), _sum8(r_up1, name="sum_up1_grads")])
    G["fox_w_in"] = _sum8(r_fox, name="sum_fox_w_in")[None]

    small_full = {
        "norm_g": jnp.stack([dg0, dg1, dg2, dg3]).reshape(2, 2, D),
        "ffn_conv_w": jnp.stack([dcw0, dcw1]),
        "s5_lam_re": dlam_re[None], "s5_lam_im": dlam_im[None], "s5_log_dt": dlog_dt[None],
        "s5_b_re": db_re[None], "s5_b_im": db_im[None], "s5_c_re": dc_re[None], "s5_c_im": dc_im[None],
        "s5_d": dd, "fox_b_f": dbf[:, :HEADS], "ffn_conv_b": jnp.stack([dcb0, dcb1]), "final_g": d_final_g[0],
    }
    names = tuple(small_full)
    g3 = _exchange(_pack([small_full[n] for n in names], F32, 8), gather=True, name="gather_small_grads")
    summed = dict(zip(names, _unpack(_sum8(g3, name="sum_small_grads"), [small_full[n].shape for n in names])))
    G["norm_g"] = lax.dynamic_slice_in_dim(summed["norm_g"], me * norm_g.shape[-1], norm_g.shape[-1], axis=2)
    G["ffn_conv_w"] = lax.dynamic_slice_in_dim(summed["ffn_conv_w"], me * ffn_conv_w.shape[-1], ffn_conv_w.shape[-1], axis=2)
    for n in REPLICATED:
        G[n] = summed[n]

    delta, new_m, new_v = {}, {}, {}
    small = tuple(n for n in WEIGHTS if n not in BIG and n != "ada_w")
    for n in WEIGHTS:
        if n not in small:
            delta[n], new_m[n], new_v[n] = _adamw_nd(W[n], G[n], M[n], V[n], name=f"adamw_{n}")
    packed = [_pack([src[n] for n in small], F32, 8) for src in (W, G, M, V)]
    for dst, buf in zip((delta, new_m, new_v), _adamw(*packed, name="adamw_small")):
        dst.update(zip(small, _unpack(buf, [W[n].shape for n in small])))

    return (loss, grad_x, *[G[n] for n in WEIGHTS], *[delta[n] for n in WEIGHTS], *[new_m[n] for n in WEIGHTS],
            *[new_v[n] for n in WEIGHTS])


def _columns_of_mod(dmod_all, me, ncol):
    flat = lax.dynamic_slice_in_dim(dmod_all, me * ncol, ncol, axis=2)
    return flat.transpose(1, 0, 2)
```

```python
import functools
import math

import jax
import jax.numpy as jnp
from jax import lax
from jax.experimental import pallas as pl
from jax.experimental.pallas import tpu as pltpu

F32, BF16 = jnp.float32, jnp.bfloat16
EPS = 1e-6
N_DEV = 8
D_MODEL = 1024
D_FF = 2816
HEADS = 16
HEAD_DIM = 64
S5_G, S5_P, S5_C = 64, 64, 16
S5_SUB = 8
V7X_VMEM_LIMIT = 56 * 1024 * 1024
NEG = -1e30
ADAM_LR, ADAM_B1, ADAM_B2, ADAM_EPS, ADAM_WD, ADAM_STEP = 1e-3, 0.9, 0.999, 1e-8, 0.01, 10
GELU_K = math.sqrt(2.0 / math.pi)
MESH_ID = pl.DeviceIdType.MESH


def _cp(*sem):
    return pltpu.CompilerParams(dimension_semantics=sem, vmem_limit_bytes=V7X_VMEM_LIMIT)


def _tile(n, target, mult=128):
    if n <= target:
        return n
    t = (target // mult) * mult
    while t >= mult:
        if n % t == 0:
            return t
        t -= mult
    return n


def _row(v):
    return v.reshape(1, -1).astype(F32)


def _mm(a, b, *, name, ta=False, tb=False, out_dtype=F32, tm=1024, tn=512, tk=None, res=None, gate=None,
        a_h=False, b_h=False, o_h=False):
    if a_h:
        M, K = a.shape[1], 2 * a.shape[2]
    elif ta:
        K, M = a.shape
    else:
        M, K = a.shape
    if b_h:
        N = 2 * b.shape[2]
    else:
        N = b.shape[0] if tb else b.shape[1]
    half_n = N // 2
    tm = _tile(M, tm, 128 if ta else 8)
    tn = _tile(half_n if (b_h or o_h) else N, tn)
    tk = K if tk is None else _tile(K // 2 if a_h else K, tk)
    nk = K // tk
    nkh, nnh = (K // 2) // tk if a_h else 1, half_n // tn
    if a_h:
        a_spec = pl.BlockSpec((None, tm, tk), lambda i, j, k: (k // nkh, i, k % nkh))
    elif ta:
        a_spec = pl.BlockSpec((tk, tm), lambda i, j, k: (k, i))
    else:
        a_spec = pl.BlockSpec((tm, tk), lambda i, j, k: (i, k))
    if b_h:
        b_spec = pl.BlockSpec((None, tk, tn), lambda i, j, k: (j // nnh, k, j % nnh))
    elif tb:
        b_spec = pl.BlockSpec((tn, tk), lambda i, j, k: (j, k))
    else:
        b_spec = pl.BlockSpec((tk, tn), lambda i, j, k: (k, j))
    if o_h:
        o_spec = pl.BlockSpec((None, tm, tn), lambda i, j, k: (j // nnh, i, j % nnh))
    else:
        o_spec = pl.BlockSpec((tm, tn), lambda i, j, k: (i, j))
    dn = (((0 if ta else 1,), (1 if tb else 0,)), ((), ()))
    fused = res is not None

    def body(*refs):
        if fused:
            a_ref, b_ref, r_ref, g_ref, m_ref, o_ref, acc_ref = refs
        else:
            a_ref, b_ref, o_ref, acc_ref = refs
        p = lax.dot_general(a_ref[...].astype(BF16), b_ref[...].astype(BF16), dn, preferred_element_type=F32)

        def finish(acc):
            if fused:
                m_ref[...] = acc.astype(m_ref.dtype)
                o_ref[...] = r_ref[...] + g_ref[...] * acc
            else:
                o_ref[...] = acc.astype(o_ref.dtype)

        if nk == 1:
            finish(p)
        else:
            k = pl.program_id(2)

            @pl.when(k == 0)
            def _():
                acc_ref[...] = p

            @pl.when(k > 0)
            def _():
                acc_ref[...] += p

            @pl.when(k == nk - 1)
            def _():
                finish(acc_ref[...])

    in_specs = [a_spec, b_spec]
    args = [a, b]
    if fused:
        in_specs += [o_spec, pl.BlockSpec((1, tn), lambda i, j, k: (0, j))]
        args += [res, gate]
        out_shape = [jax.ShapeDtypeStruct((M, N), BF16), jax.ShapeDtypeStruct((M, N), F32)]
        out_specs = [o_spec, o_spec]
    else:
        out_shape = jax.ShapeDtypeStruct((2, M, half_n) if o_h else (M, N), out_dtype)
        out_specs = o_spec
    return pl.pallas_call(
        body, name=name, grid=(M // tm, N // tn, nk), in_specs=in_specs, out_specs=out_specs, out_shape=out_shape,
        scratch_shapes=[pltpu.VMEM((tm, tn) if nk > 1 else (8, 128), F32)],
        compiler_params=_cp("parallel", "parallel", "arbitrary"),
    )(*args)


def _modulate(h, g, shift, scale, *, name):
    S, D = h.shape
    tm = _tile(S, 512, 8)
    vec = pl.BlockSpec((1, D), lambda i: (0, 0))
    blk = pl.BlockSpec((tm, D), lambda i: (i, 0))

    def body(h_ref, g_ref, sh_ref, sc_ref, o_ref):
        x = h_ref[...]
        r = lax.rsqrt(jnp.mean(x * x, axis=-1, keepdims=True) + EPS)
        o_ref[...] = ((x * r * g_ref[...]) * (1.0 + sc_ref[...]) + sh_ref[...]).astype(o_ref.dtype)

    return pl.pallas_call(body, name=name, grid=(S // tm,), in_specs=[blk, vec, vec, vec], out_specs=blk,
                          out_shape=jax.ShapeDtypeStruct((S, D), BF16), compiler_params=_cp("parallel"))(h, g, shift, scale)


def _gate_bwd(dh, m, gate, *, name):
    S, D = dh.shape
    tm = _tile(S, 512, 8)
    vec = pl.BlockSpec((1, D), lambda i: (0, 0))
    blk = pl.BlockSpec((tm, D), lambda i: (i, 0))

    def body(dh_ref, m_ref, g_ref, dm_ref, dg_ref):
        @pl.when(pl.program_id(0) == 0)
        def _():
            dg_ref[...] = jnp.zeros_like(dg_ref)

        d = dh_ref[...]
        dm_ref[...] = (d * g_ref[...]).astype(dm_ref.dtype)
        dg_ref[...] += jnp.sum(d * m_ref[...].astype(F32), axis=0, keepdims=True)

    return pl.pallas_call(body, name=name, grid=(S // tm,), in_specs=[blk, blk, vec], out_specs=[blk, vec],
                          out_shape=[jax.ShapeDtypeStruct((S, D), BF16), jax.ShapeDtypeStruct((1, D), F32)],
                          compiler_params=_cp("arbitrary"))(dh, m, gate)


def _modulate_bwd(h, dhn, dh_in, g, scale, *, name):
    S, D = h.shape
    tm = _tile(S, 512, 8)
    vec = pl.BlockSpec((1, D), lambda i: (0, 0))
    blk = pl.BlockSpec((tm, D), lambda i: (i, 0))
    sums = pl.BlockSpec((8, D), lambda i: (0, 0))

    def body(h_ref, dhn_ref, dhi_ref, g_ref, sc_ref, dh_ref, s_ref):
        @pl.when(pl.program_id(0) == 0)
        def _():
            s_ref[...] = jnp.zeros_like(s_ref)

        x = h_ref[...]
        r = lax.rsqrt(jnp.mean(x * x, axis=-1, keepdims=True) + EPS)
        xhat = x * r
        gv = g_ref[...]
        dhn_v = dhn_ref[...].astype(F32)
        dn = dhn_v * (1.0 + sc_ref[...])
        s_ref[0:1, :] += jnp.sum(dhn_v, axis=0, keepdims=True)
        s_ref[1:2, :] += jnp.sum(dhn_v * (xhat * gv), axis=0, keepdims=True)
        s_ref[2:3, :] += jnp.sum(dn * xhat, axis=0, keepdims=True)
        dxh = dn * gv
        dh_ref[...] = dhi_ref[...] + r * (dxh - xhat * jnp.mean(dxh * xhat, axis=-1, keepdims=True))

    return pl.pallas_call(body, name=name, grid=(S // tm,), in_specs=[blk, blk, blk, vec, vec], out_specs=[blk, sums],
                          out_shape=[jax.ShapeDtypeStruct((S, D), F32), jax.ShapeDtypeStruct((8, D), F32)],
                          compiler_params=_cp("arbitrary"))(h, dhn, dh_in, g, scale)


def _loss_head(h, g, tgt, *, name):
    S, D = h.shape
    tm = _tile(S, 512, 8)
    vec = pl.BlockSpec((1, D), lambda i: (0, 0))
    blk = pl.BlockSpec((tm, D), lambda i: (i, 0))
    lss = pl.BlockSpec((8, 128), lambda i: (0, 0))

    def body(h_ref, g_ref, t_ref, l_ref, dh_ref, dg_ref):
        @pl.when(pl.program_id(0) == 0)
        def _():
            l_ref[...] = jnp.zeros_like(l_ref)
            dg_ref[...] = jnp.zeros_like(dg_ref)

        x = h_ref[...]
        r = lax.rsqrt(jnp.mean(x * x, axis=-1, keepdims=True) + EPS)
        xhat = x * r
        gv = g_ref[...]
        e = xhat * gv - t_ref[...]
        l_ref[...] += 0.5 * jnp.sum(jnp.mean(e * e, axis=-1, keepdims=True))
        dy = e * (1.0 / D)
        dg_ref[...] += jnp.sum(dy * xhat, axis=0, keepdims=True)
        dxh = dy * gv
        dh_ref[...] = r * (dxh - xhat * jnp.mean(dxh * xhat, axis=-1, keepdims=True))

    return pl.pallas_call(body, name=name, grid=(S // tm,), in_specs=[blk, vec, blk], out_specs=[lss, blk, vec],
                          out_shape=[jax.ShapeDtypeStruct((8, 128), F32), jax.ShapeDtypeStruct((S, D), F32),
                                     jax.ShapeDtypeStruct((1, D), F32)],
                          compiler_params=_cp("arbitrary"))(h, g, tgt)


def _shift_down(x, k, edge):
    tm = x.shape[0]
    rows = lax.broadcasted_iota(jnp.int32, x.shape, 0)
    out = pltpu.roll(x, k, 0)
    for j in range(k):
        out = jnp.where(rows == j, edge[8 - k + j:8 - k + j + 1, :], out)
    return out


def _shift_up(x, k, edge):
    tm = x.shape[0]
    rows = lax.broadcasted_iota(jnp.int32, x.shape, 0)
    out = pltpu.roll(x, tm - k, 0)
    for j in range(k):
        out = jnp.where(rows == tm - k + j, edge[j:j + 1, :], out)
    return out


def _conv_gate_fwd(up, cw, cb, *, name):
    _, S, F = up.shape
    tf = _tile(F, 1408)
    nf = F // tf
    tm = _tile(S, 512, 8)

    def body(a_ref, b_ref, w_ref, cb_ref, z_ref, edge_ref):
        @pl.when(pl.program_id(1) == 0)
        def _():
            edge_ref[...] = jnp.zeros_like(edge_ref)

        a = a_ref[...].astype(F32)
        edge = edge_ref[...]
        w = w_ref[...]
        ac = cb_ref[...] + w[2:3, :] * a + w[1:2, :] * _shift_down(a, 1, edge) + w[0:1, :] * _shift_down(a, 2, edge)
        edge_ref[...] = a[tm - 8:tm, :]
        z_ref[...] = (ac * jax.nn.sigmoid(ac) * b_ref[...].astype(F32)).astype(z_ref.dtype)

    return pl.pallas_call(
        body, name=name, grid=(nf, S // tm),
        in_specs=[pl.BlockSpec((None, tm, tf), lambda j, i: (0, i, j)), pl.BlockSpec((None, tm, tf), lambda j, i: (1, i, j)),
                  pl.BlockSpec((3, tf), lambda j, i: (0, j)), pl.BlockSpec((1, tf), lambda j, i: (0, j))],
        out_specs=pl.BlockSpec((tm, tf), lambda j, i: (i, j)),
        out_shape=jax.ShapeDtypeStruct((S, F), BF16), scratch_shapes=[pltpu.VMEM((8, tf), F32)],
        compiler_params=_cp("parallel", "arbitrary"))(up, up, cw, cb)


def _conv_gate_bwd(up, dz, cw, cb, *, name):
    _, S, F = up.shape
    tf = _tile(F, 1408)
    nf = F // tf
    tm = _tile(S, 512, 8)
    nt = S // tm
    hb = tm // 8

    def body(a_ref, ah_ref, b_ref, dz_ref, w_ref, cb_ref, d_ref, s_ref, edge_ref):
        i = pl.program_id(1)

        @pl.when(i == 0)
        def _():
            edge_ref[...] = jnp.zeros_like(edge_ref)
            s_ref[...] = jnp.zeros_like(s_ref)

        a = a_ref[...].astype(F32)
        halo = jnp.where(i == nt - 1, 0.0, ah_ref[...].astype(F32))
        w = w_ref[...]
        a1 = _shift_down(a, 1, halo)
        a2 = _shift_down(a, 2, halo)
        ac = cb_ref[...] + w[2:3, :] * a + w[1:2, :] * a1 + w[0:1, :] * a2
        sg = jax.nn.sigmoid(ac)
        dzv = dz_ref[...].astype(F32)
        d_ref[1] = (dzv * ac * sg).astype(d_ref.dtype)
        dac = dzv * b_ref[...].astype(F32) * (sg * (1.0 + ac * (1.0 - sg)))
        s_ref[0:1, :] += jnp.sum(dac * a2, axis=0, keepdims=True)
        s_ref[1:2, :] += jnp.sum(dac * a1, axis=0, keepdims=True)
        s_ref[2:3, :] += jnp.sum(dac * a, axis=0, keepdims=True)
        s_ref[3:4, :] += jnp.sum(dac, axis=0, keepdims=True)
        edge = edge_ref[...]
        da = w[2:3, :] * dac + w[1:2, :] * _shift_up(dac, 1, edge) + w[0:1, :] * _shift_up(dac, 2, edge)
        edge_ref[...] = dac[0:8, :]
        d_ref[0] = da.astype(d_ref.dtype)

    tile = lambda hlf: pl.BlockSpec((None, tm, tf), lambda j, i: (hlf, nt - 1 - i, j))
    d_up, sums = pl.pallas_call(
        body, name=name, grid=(nf, nt),
        in_specs=[tile(0),
                  pl.BlockSpec((None, 8, tf), lambda j, i: (0, jnp.maximum((nt - 1 - i) * hb - 1, 0), j)),
                  tile(1), pl.BlockSpec((tm, tf), lambda j, i: (nt - 1 - i, j)),
                  pl.BlockSpec((3, tf), lambda j, i: (0, j)), pl.BlockSpec((1, tf), lambda j, i: (0, j))],
        out_specs=[pl.BlockSpec((2, tm, tf), lambda j, i: (0, nt - 1 - i, j)), pl.BlockSpec((8, tf), lambda j, i: (0, j))],
        out_shape=[jax.ShapeDtypeStruct((2, S, F), BF16), jax.ShapeDtypeStruct((8, F), F32)],
        scratch_shapes=[pltpu.VMEM((8, tf), F32)],
        compiler_params=_cp("parallel", "arbitrary"))(up, up, up, dz, cw, cb)
    return d_up, sums


def _log_sigmoid(x):
    return jnp.minimum(x, 0.0) - jnp.log(1.0 + jnp.exp(-jnp.abs(x)))


def _tri_ones(n, upper):
    r = lax.broadcasted_iota(jnp.int32, (n, n), 0)
    c = lax.broadcasted_iota(jnp.int32, (n, n), 1)
    return jnp.where((c >= r) if upper else (c <= r), 1.0, 0.0).astype(F32)


def _fgate_fwd(fl, bf, *, name):
    S, W = fl.shape
    tb = _tile(S, 256, 8)

    def body(fl_ref, b_ref, o_ref, carry_ref):
        @pl.when(pl.program_id(0) == 0)
        def _():
            carry_ref[...] = jnp.zeros_like(carry_ref)

        lf = _log_sigmoid(fl_ref[...] + b_ref[...])
        cs = jnp.dot(_tri_ones(tb, False), lf, precision=lax.Precision.HIGHEST, preferred_element_type=F32) + carry_ref[0:1, :]
        o_ref[...] = cs
        carry_ref[...] = jnp.broadcast_to(cs[tb - 1:tb, :], carry_ref.shape)

    blk = pl.BlockSpec((tb, W), lambda i: (i, 0))
    return pl.pallas_call(body, name=name, grid=(S // tb,), in_specs=[blk, pl.BlockSpec((1, W), lambda i: (0, 0))], out_specs=blk,
                          out_shape=jax.ShapeDtypeStruct((S, W), F32), scratch_shapes=[pltpu.VMEM((8, W), F32)],
                          compiler_params=_cp("arbitrary"))(fl, bf)


def _fgate_bwd(dF, fl, bf, *, name):
    S, W = fl.shape
    tb = _tile(S, 256, 8)
    nb = S // tb

    def body(d_ref, fl_ref, b_ref, o_ref, s_ref, carry_ref):
        @pl.when(pl.program_id(0) == 0)
        def _():
            carry_ref[...] = jnp.zeros_like(carry_ref)
            s_ref[...] = jnp.zeros_like(s_ref)

        rc = jnp.dot(_tri_ones(tb, True), d_ref[...], precision=lax.Precision.HIGHEST, preferred_element_type=F32) + carry_ref[0:1, :]
        carry_ref[...] = jnp.broadcast_to(rc[0:1, :], carry_ref.shape)
        dfl = rc * jax.nn.sigmoid(-(fl_ref[...] + b_ref[...]))
        o_ref[...] = dfl
        s_ref[...] += jnp.sum(dfl, axis=0, keepdims=True)

    blk = pl.BlockSpec((tb, W), lambda i: (nb - 1 - i, 0))
    vec = pl.BlockSpec((1, W), lambda i: (0, 0))
    return pl.pallas_call(body, name=name, grid=(nb,), in_specs=[blk, blk, vec], out_specs=[blk, vec],
                          out_shape=[jax.ShapeDtypeStruct((S, W), F32), jax.ShapeDtypeStruct((1, W), F32)],
                          scratch_shapes=[pltpu.VMEM((8, W), F32)], compiler_params=_cp("arbitrary"))(dF, fl, bf)


NT_DIMS = (((1,), (1,)), ((), ()))
TN_DIMS = (((0,), (0,)), ((), ()))
HP = HEADS // 2
Q_SCALE = HEAD_DIM ** -0.5


def _head_mask(x, hh):
    lanes = lax.broadcasted_iota(jnp.int32, x.shape, 1)
    return jnp.where((lanes >= hh * HEAD_DIM) & (lanes < (hh + 1) * HEAD_DIM), x, jnp.zeros_like(x))


def _lanes_from(cols, shape):
    lanes = lax.broadcasted_iota(jnp.int32, shape, 1)
    out = jnp.zeros(shape, F32)
    for i, cvec in enumerate(cols):
        out = jnp.where(lanes == i, cvec, out)
    return out


AUG_ROWS = 16


def _qa_pair(qt, aug):
    fill = jnp.zeros((HEAD_DIM - AUG_ROWS, qt.shape[1]), qt.dtype)
    return [jnp.concatenate([qt[0:HEAD_DIM], aug[0], fill], axis=0), jnp.concatenate([aug[1], fill, qt[HEAD_DIM:]], axis=0)]


def _rows_of_head(xt, hh):
    rows = lax.broadcasted_iota(jnp.int32, xt.shape, 0)
    return jnp.where((rows >= hh * HEAD_DIM) & (rows < (hh + 1) * HEAD_DIM), xt, jnp.zeros_like(xt))


def _put_rows(ref, rows):
    for i, r in enumerate(rows):
        ref[i:i + 1, :] = r
    ref[len(rows):, :] = jnp.zeros((ref.shape[0] - len(rows), ref.shape[1]), ref.dtype)


def _diag_mask_t(s):
    rows = lax.broadcasted_iota(jnp.int32, s.shape, 0)
    cols = lax.broadcasted_iota(jnp.int32, s.shape, 1)
    return jnp.where(rows <= cols, s, NEG)


def _foxt_fwd(qkvT, ka, vtb, augq, *, tb, name, comm=()):
    S = qkvT.shape[1]
    nb = S // tb

    def body(q_ref, ka_ref, v_ref, aq_ref, o_ref, lse_ref, m_sc, l_sc, acc_sc):
        qi = pl.program_id(1)
        qa = _qa_pair(q_ref[...], aq_ref[...])
        m_sc[...] = jnp.full_like(m_sc, NEG)
        l_sc[...] = jnp.zeros_like(l_sc)
        acc_sc[...] = jnp.zeros_like(acc_sc)

        def block(kj, masked):
            k0 = pl.multiple_of(kj * tb, tb)
            vt = v_ref[kj]
            for hh in range(2):
                s = jnp.dot(ka_ref[hh, pl.ds(k0, tb), :], qa[hh], preferred_element_type=F32)
                if masked:
                    s = _diag_mask_t(s)
                m_old = m_sc[hh]
                m_new = jnp.maximum(m_old, jnp.max(s, axis=0, keepdims=True))
                alpha = jnp.exp(m_old - m_new)
                p = jnp.exp(s - m_new)
                l_sc[hh] = alpha * l_sc[hh] + jnp.sum(p, axis=0, keepdims=True)
                acc_sc[hh] = alpha * acc_sc[hh] + jnp.dot(vt, p.astype(BF16), preferred_element_type=F32)
                m_sc[hh] = m_new

        def off_diagonal(kj, c):
            block(kj, False)
            return c

        lax.fori_loop(0, qi, off_diagonal, 0)
        block(qi, True)
        o_ref[0:HEAD_DIM, :] = (acc_sc[0, 0:HEAD_DIM, :] / l_sc[0]).astype(o_ref.dtype)
        o_ref[HEAD_DIM:, :] = (acc_sc[1, HEAD_DIM:, :] / l_sc[1]).astype(o_ref.dtype)
        _put_rows(lse_ref, [m_sc[hh] + jnp.log(l_sc[hh]) for hh in range(2)])

    return _call(
        body, name=name, grid=(HP, nb), comm=comm, sem=("parallel", "parallel"), args=(qkvT, ka, vtb, augq),
        in_specs=[pl.BlockSpec((128, tb), lambda hp, qi: (hp, qi)),
                  pl.BlockSpec((None, 2, S, 128), lambda hp, qi: (hp, 0, 0, 0)),
                  pl.BlockSpec((None, nb, 128, tb), lambda hp, qi: (hp, 0, 0, 0)),
                  pl.BlockSpec((None, 2, AUG_ROWS, tb), lambda hp, qi: (hp, 0, 0, qi))],
        out_specs=[pl.BlockSpec((128, tb), lambda hp, qi: (hp, qi)), pl.BlockSpec((None, 8, tb), lambda hp, qi: (hp, 0, qi))],
        out_shape=[jax.ShapeDtypeStruct((D_MODEL, S), BF16), jax.ShapeDtypeStruct((HP, 8, S), F32)],
        scratch_shapes=[pltpu.VMEM((2, 1, tb), F32), pltpu.VMEM((2, 1, tb), F32), pltpu.VMEM((2, 128, tb), F32)])


def _fox_delta(oT, doT, *, tb, name):
    S = oT.shape[1]

    def body(o_ref, do_ref, dl_ref):
        prod = o_ref[...].astype(F32) * do_ref[...].astype(F32)
        _put_rows(dl_ref, [jnp.sum(prod[HEAD_DIM * hh:HEAD_DIM * (hh + 1)], axis=0, keepdims=True) for hh in range(2)])

    blk = pl.BlockSpec((128, tb), lambda hp, qi: (hp, qi))
    return pl.pallas_call(body, name=name, grid=(HP, S // tb), in_specs=[blk, blk],
                          out_specs=pl.BlockSpec((None, 8, tb), lambda hp, qi: (hp, 0, qi)),
                          out_shape=jax.ShapeDtypeStruct((HP, 8, S), F32), compiler_params=_cp("parallel", "parallel"))(oT, doT)


def _foxt_bwd(qkvT, qkv, ka, qtb, dotb, do, augqb, lseb, deltab, *, tb, name, comm=()):
    S = qkv.shape[0]
    nb = S // tb

    def body(ka_ref, v_ref, kt_ref, qt_ref, dot_ref, qn_ref, dn_ref, aq_ref, ls_ref, dl_ref,
             dq_ref, dk_ref, dv_ref, df_ref, dr_ref, dq_sc, dk_sc, dv_sc, ds_sc, dr_sc):
        kj = pl.program_id(1)

        @pl.when(kj == 0)
        def _():
            dq_sc[...] = jnp.zeros_like(dq_sc)
            dr_sc[...] = jnp.zeros_like(dr_sc)

        dk_sc[...] = jnp.zeros_like(dk_sc)
        dv_sc[...] = jnp.zeros_like(dv_sc)
        ds_sc[...] = jnp.zeros_like(ds_sc)
        v2 = v_ref[...]
        kth = [_rows_of_head(kt_ref[...], hh) for hh in range(2)]

        def block(qi, masked):
            q0 = pl.multiple_of(qi * tb, tb)
            qa = _qa_pair(qt_ref[qi], aq_ref[qi])
            dot, dl, ls = dot_ref[qi], dl_ref[qi], ls_ref[qi]
            qn, dn = qn_ref[pl.ds(q0, tb), :], dn_ref[pl.ds(q0, tb), :]
            dq_new = dq_sc[qi]
            for hh in range(2):
                s = jnp.dot(ka_ref[hh], qa[hh], preferred_element_type=F32)
                if masked:
                    s = _diag_mask_t(s)
                p = jnp.exp(s - ls[hh:hh + 1, :])
                dp = jnp.dot(v2, _rows_of_head(dot, hh), preferred_element_type=F32)
                ds = p * (dp - dl[hh:hh + 1, :])
                dsb = ds.astype(BF16)
                dv_sc[...] += jnp.dot(p.astype(BF16), _head_mask(dn, hh), preferred_element_type=F32)
                dk_sc[...] += jnp.dot(dsb, _head_mask(qn, hh), preferred_element_type=F32)
                dq_new = dq_new + jnp.dot(kth[hh], dsb, preferred_element_type=F32)
                part = ds[:, 0:128]
                for j in range(1, tb // 128):
                    part = part + ds[:, 128 * j:128 * (j + 1)]
                ds_sc[hh] += part
                dr_sc[qi, hh:hh + 1, :] += jnp.sum(ds, axis=0, keepdims=True)
            dq_sc[qi] = dq_new

        def off_diagonal(i, c):
            block(kj + 1 + i, False)
            return c

        block(kj, True)
        lax.fori_loop(0, nb - 1 - kj, off_diagonal, 0)
        dk_ref[...] = dk_sc[...].astype(dk_ref.dtype)
        dv_ref[...] = dv_sc[...].astype(dv_ref.dtype)
        df_ref[...] = _lanes_from([-jnp.sum(ds_sc[hh], axis=1, keepdims=True) for hh in range(2)], (tb, 128))

        @pl.when(kj == nb - 1)
        def _():
            dq_ref[...] = (dq_sc[...] * Q_SCALE).astype(dq_ref.dtype)
            dr_ref[...] = dr_sc[...]

    resident = lambda rows: pl.BlockSpec((None, nb) + rows, lambda hp, kj: (hp,) + (0,) * (len(rows) + 1))
    whole = pl.BlockSpec((S, 128), lambda hp, kj: (0, hp))
    kblk = lambda off: pl.BlockSpec((tb, 128), lambda hp, kj: (kj, off + hp))
    return _call(
        body, name=name, grid=(HP, nb), comm=comm, sem=("parallel", "arbitrary"),
        args=(ka, qkv, qkvT, qtb, dotb, qkv, do, augqb, lseb, deltab),
        in_specs=[pl.BlockSpec((None, 2, tb, 128), lambda hp, kj: (hp, 0, kj, 0)), kblk(2 * HP),
                  pl.BlockSpec((128, tb), lambda hp, kj: (HP + hp, kj)),
                  resident((128, tb)), resident((128, tb)), whole, whole, resident((2, AUG_ROWS, tb)), resident((8, tb)),
                  resident((8, tb))],
        out_specs=[resident((128, tb)), kblk(0), kblk(0), pl.BlockSpec((None, tb, 128), lambda hp, kj: (hp, kj, 0)),
                   resident((8, tb))],
        out_shape=[jax.ShapeDtypeStruct((HP, nb, 128, tb), BF16), jax.ShapeDtypeStruct((S, D_MODEL), BF16),
                   jax.ShapeDtypeStruct((S, D_MODEL), BF16), jax.ShapeDtypeStruct((HP, S, 128), F32),
                   jax.ShapeDtypeStruct((HP, nb, 8, tb), F32)],
        scratch_shapes=[pltpu.VMEM((nb, 128, tb), F32), pltpu.VMEM((tb, 128), F32), pltpu.VMEM((tb, 128), F32),
                        pltpu.VMEM((2, tb, 128), F32), pltpu.VMEM((nb, 8, tb), F32)])


def _split3(x):
    rnd = lambda v: lax.reduce_precision(v, exponent_bits=8, mantissa_bits=7)
    hi = rnd(x)
    mid = rnd(x - hi)
    lo = rnd(x - hi - mid)
    return hi.astype(BF16), mid.astype(BF16), lo.astype(BF16)


def _blocked(xt, nb, tb):
    lead = xt.shape[:-1]
    x = xt.reshape(lead + (nb, tb))
    return jnp.moveaxis(x, -2, 1)


def _aug_rows(bias):
    ones = jnp.ones(bias.shape, BF16)
    zeros = jnp.zeros(bias.shape, BF16)
    return jnp.stack(list(_split3(bias)) + [ones] * 3 + [zeros] * (AUG_ROWS - 6), axis=2)


def _fox_keys(qkv, fcol, *, name):
    S = qkv.shape[0]
    tr = _tile(S, 512, 16)

    def body(k_ref, f_ref, o_ref):
        hp = pl.program_id(0)
        k2 = k_ref[...]
        lanes = lax.broadcasted_iota(jnp.int32, (tr, 128), 1)
        for hh in range(2):
            nf = -jnp.sum(jnp.where(lanes == 2 * hp + hh, f_ref[...], 0.0), axis=1, keepdims=True)
            hi = nf.astype(BF16).astype(F32)
            mid = (nf - hi).astype(BF16).astype(F32)
            lo = nf - hi - mid
            base = HEAD_DIM * (1 - hh)
            aug = jnp.where((lanes >= base) & (lanes < base + 3), 1.0, 0.0)
            for j, piece in enumerate((hi, mid, lo)):
                aug = jnp.where(lanes == base + 3 + j, piece, aug)
            mine = (lanes >= HEAD_DIM * hh) & (lanes < HEAD_DIM * (hh + 1))
            o_ref[hh] = jnp.where(mine, k2, aug.astype(BF16))

    return pl.pallas_call(
        body, name=name, grid=(HP, S // tr),
        in_specs=[pl.BlockSpec((tr, 128), lambda hp, i: (i, HP + hp)), pl.BlockSpec((tr, 128), lambda hp, i: (i, 0))],
        out_specs=pl.BlockSpec((None, 2, tr, 128), lambda hp, i: (hp, 0, i, 0)),
        out_shape=jax.ShapeDtypeStruct((HP, 2, S, 128), BF16), compiler_params=_cp("parallel", "parallel"))(qkv, fcol)


HALF = S5_SUB * S5_P
NCB = 2 * HALF // 128
RE, IM = slice(0, NCB // 2), slice(NCB // 2, NCB)


def _gelu(x):
    return 0.5 * x * (1.0 + jnp.tanh(GELU_K * (x + 0.044715 * x * x * x)))


def _gelu_grad(x):
    t = jnp.tanh(GELU_K * (x + 0.044715 * x * x * x))
    return 0.5 * (1.0 + t) + 0.5 * x * (1.0 - t * t) * GELU_K * (1.0 + 3.0 * 0.044715 * x * x)


def _s5_put(buf, s, val, lt):
    for cb in range(NCB):
        buf[cb, pl.ds(s, lt, stride=S5_SUB), :] = val[:, 128 * cb:128 * (cb + 1)]


def _s5_get(buf, s, lt):
    return jnp.concatenate([buf[cb, pl.ds(s, lt, stride=S5_SUB), :] for cb in range(NCB)], axis=1)


def _s5_project_in(u_ref, wb_ref, buf, lt):
    for s in range(S5_SUB):
        _s5_put(buf, s, jnp.dot(u_ref[:, 128 * s:128 * (s + 1)], wb_ref[s], preferred_element_type=F32), lt)


def _s5_scan(buf, lam_ref, h0, lt):
    a_re, a_im = lam_ref[RE], lam_ref[IM]

    def step(t, carry):
        hr, hi = carry
        r0 = pl.multiple_of(t * S5_SUB, S5_SUB)
        nr = a_re * hr - a_im * hi + buf[RE, pl.ds(r0, S5_SUB), :]
        ni = a_re * hi + a_im * hr + buf[IM, pl.ds(r0, S5_SUB), :]
        buf[RE, pl.ds(r0, S5_SUB), :] = nr
        buf[IM, pl.ds(r0, S5_SUB), :] = ni
        return nr, ni

    return lax.fori_loop(0, lt, step, (h0[RE], h0[IM]), unroll=4)


def _s5_fwd(u, wb, wc, lam, dskip, *, lt, name, comm=()):
    S, D = u.shape
    nt = S // lt

    def body(u_ref, wb_ref, wc_ref, lam_ref, d_ref, yp_ref, yg_ref, st_ref, buf, h_sc):
        @pl.when(pl.program_id(0) == 0)
        def _():
            h_sc[...] = jnp.zeros_like(h_sc)

        st_ref[...] = h_sc[...]
        _s5_project_in(u_ref, wb_ref, buf, lt)
        hr, hi = _s5_scan(buf, lam_ref, h_sc[...], lt)
        h_sc[RE] = hr
        h_sc[IM] = hi
        for s in range(S5_SUB):
            cols = slice(128 * s, 128 * (s + 1))
            hs = _s5_get(buf, s, lt).astype(BF16)
            yp = jnp.dot(hs, wc_ref[s], preferred_element_type=F32) + d_ref[:, cols] * u_ref[:, cols].astype(F32)
            yp_ref[:, cols] = yp.astype(yp_ref.dtype)
            yg_ref[:, cols] = _gelu(yp).astype(yg_ref.dtype)

    blk = pl.BlockSpec((lt, D), lambda i: (i, 0))
    full = lambda shp: pl.BlockSpec(shp, lambda i: (0,) * len(shp))
    state = (NCB, S5_SUB, 128)
    return _call(
        body, name=name, grid=(nt,), comm=comm, sem=("arbitrary",), args=(u, wb, wc, lam, dskip),
        in_specs=[blk, full(wb.shape), full(wc.shape), full(lam.shape), full(dskip.shape)],
        out_specs=[blk, blk, pl.BlockSpec((None,) + state, lambda i: (i, 0, 0, 0))],
        out_shape=[jax.ShapeDtypeStruct((S, D), BF16), jax.ShapeDtypeStruct((S, D), BF16), jax.ShapeDtypeStruct((nt,) + state, F32)],
        scratch_shapes=[pltpu.VMEM((NCB, lt * S5_SUB, 128), F32), pltpu.VMEM(state, F32)])


def _s5_bwd(u, dyp, st, wb, wc, lam, dskip, *, lt, name, comm=()):
    S, D = u.shape
    nt = S // lt

    def body(u_ref, dy_ref, st_ref, wb_ref, wc_ref, lam_ref, d_ref, du_ref, dwb_ref, dwc_ref, dlam_ref, hbuf, gbuf, g_sc):
        @pl.when(pl.program_id(0) == 0)
        def _():
            g_sc[...] = jnp.zeros_like(g_sc)
            dwb_ref[...] = jnp.zeros_like(dwb_ref)
            dwc_ref[...] = jnp.zeros_like(dwc_ref)
            dlam_ref[...] = jnp.zeros_like(dlam_ref)

        _s5_project_in(u_ref, wb_ref, hbuf, lt)
        _s5_scan(hbuf, lam_ref, st_ref[...], lt)
        for s in range(S5_SUB):
            dys = dy_ref[:, 128 * s:128 * (s + 1)]
            _s5_put(gbuf, s, lax.dot_general(dys, wc_ref[s], NT_DIMS, preferred_element_type=F32), lt)
        a_re, a_im = lam_ref[RE], lam_ref[IM]

        def one(t, carry, hp_re, hp_im):
            gr, gi, dar, dai = carry
            r0 = pl.multiple_of(t * S5_SUB, S5_SUB)
            nr = gbuf[RE, pl.ds(r0, S5_SUB), :] + a_re * gr + a_im * gi
            ni = gbuf[IM, pl.ds(r0, S5_SUB), :] + a_re * gi - a_im * gr
            gbuf[RE, pl.ds(r0, S5_SUB), :] = nr
            gbuf[IM, pl.ds(r0, S5_SUB), :] = ni
            return nr, ni, dar + nr * hp_re + ni * hp_im, dai + ni * hp_re - nr * hp_im

        def step(k, carry):
            t = lt - 1 - k
            p0 = pl.multiple_of((t - 1) * S5_SUB, S5_SUB)
            return one(t, carry, hbuf[RE, pl.ds(p0, S5_SUB), :], hbuf[IM, pl.ds(p0, S5_SUB), :])

        init = (g_sc[RE], g_sc[IM], dlam_ref[RE], dlam_ref[IM])
        carry = lax.fori_loop(0, lt - 1, step, init, unroll=2)
        gr, gi, dar, dai = one(0, carry, st_ref[RE], st_ref[IM])
        g_sc[RE] = gr
        g_sc[IM] = gi
        dlam_ref[RE] = dar
        dlam_ref[IM] = dai
        for s in range(S5_SUB):
            cols = slice(128 * s, 128 * (s + 1))
            gs = _s5_get(gbuf, s, lt).astype(BF16)
            hs = _s5_get(hbuf, s, lt).astype(BF16)
            us, dys = u_ref[:, cols], dy_ref[:, cols]
            du = lax.dot_general(gs, wb_ref[s], NT_DIMS, preferred_element_type=F32) + d_ref[:, cols] * dys.astype(F32)
            du_ref[:, cols] = du.astype(du_ref.dtype)
            dwb_ref[s] += lax.dot_general(us, gs, TN_DIMS, preferred_element_type=F32)
            dwc_ref[s] += lax.dot_general(hs, dys, TN_DIMS, preferred_element_type=F32)

    blk = pl.BlockSpec((lt, D), lambda i: (nt - 1 - i, 0))
    full = lambda shp: pl.BlockSpec(shp, lambda i: (0,) * len(shp))
    state = (NCB, S5_SUB, 128)
    words = pltpu.VMEM((NCB, lt * S5_SUB, 128), F32)
    return _call(
        body, name=name, grid=(nt,), comm=comm, sem=("arbitrary",), args=(u, dyp, st, wb, wc, lam, dskip),
        in_specs=[blk, blk, pl.BlockSpec((None,) + state, lambda i: (nt - 1 - i, 0, 0, 0)),
                  full(wb.shape), full(wc.shape), full(lam.shape), full(dskip.shape)],
        out_specs=[blk, full(wb.shape), full(wc.shape), full(lam.shape)],
        out_shape=[jax.ShapeDtypeStruct((S, D), BF16), jax.ShapeDtypeStruct(wb.shape, F32), jax.ShapeDtypeStruct(wc.shape, F32),
                   jax.ShapeDtypeStruct(lam.shape, F32)],
        scratch_shapes=[words, words, pltpu.VMEM(state, F32)])


def _s5_disc(lam_re, lam_im, log_dt, b_re, b_im):
    dt = jnp.exp(log_dt)[:, None]
    mag = jnp.exp(lam_re * dt)
    lb_re, lb_im = mag * jnp.cos(lam_im * dt), mag * jnp.sin(lam_im * dt)
    num_re = lb_re - 1.0
    den = lam_re * lam_re + lam_im * lam_im
    k_re = (num_re * lam_re + lb_im * lam_im) / den
    k_im = (lb_im * lam_re - num_re * lam_im) / den
    bb_re = k_re[..., None] * b_re - k_im[..., None] * b_im
    bb_im = k_re[..., None] * b_im + k_im[..., None] * b_re
    return lb_re, lb_im, bb_re, bb_im


def _s5_pack(lb_re, lb_im, bb_re, bb_im, c_re, c_im):
    eye = jnp.eye(S5_SUB, dtype=F32)
    tb = lambda x: jnp.einsum("sgcp,gh->sgchp", x.reshape(S5_SUB, S5_SUB, S5_P, S5_C).transpose(0, 1, 3, 2), eye).reshape(S5_SUB, 128, HALF)
    tc = lambda x: jnp.einsum("sgpc,gh->sgphc", x.reshape(S5_SUB, S5_SUB, S5_C, S5_P).transpose(0, 1, 3, 2), eye).reshape(S5_SUB, HALF, 128)
    wb = jnp.concatenate([tb(bb_re), tb(bb_im)], axis=2).astype(BF16)
    wc = jnp.concatenate([tc(c_re), -tc(c_im)], axis=1).astype(BF16)
    lam = jnp.concatenate([lb_re.reshape(S5_SUB, HALF), lb_im.reshape(S5_SUB, HALF)], axis=1)
    return wb, wc, lam.reshape(S5_SUB, NCB, 128).transpose(1, 0, 2)


def _s5_unpack(dwb, dwc, dlam):
    db = jnp.einsum("sgcrgp->rsgpc", dwb.reshape(S5_SUB, S5_SUB, S5_C, 2, S5_SUB, S5_P)).reshape(2, S5_G, S5_P, S5_C)
    dc = jnp.einsum("srgpgc->rsgcp", dwc.reshape(S5_SUB, 2, S5_SUB, S5_P, S5_SUB, S5_C)).reshape(2, S5_G, S5_C, S5_P)
    dlam = dlam.transpose(1, 0, 2).reshape(S5_SUB, 2 * HALF)
    return (dlam[:, :HALF].reshape(S5_G, S5_P), dlam[:, HALF:].reshape(S5_G, S5_P), db[0], db[1], dc[0], -dc[1])


def _glu_fwd(yg, gl, *, name):
    S, D = yg.shape
    tm = _tile(S, 512, 8)
    blk = pl.BlockSpec((tm, D), lambda i: (i, 0))

    def body(y_ref, g_ref, o_ref):
        o_ref[...] = (y_ref[...].astype(F32) * jax.nn.sigmoid(g_ref[...].astype(F32))).astype(o_ref.dtype)

    return pl.pallas_call(body, name=name, grid=(S // tm,), in_specs=[blk, blk], out_specs=blk,
                          out_shape=jax.ShapeDtypeStruct((S, D), BF16), compiler_params=_cp("parallel"))(yg, gl)


def _glu_bwd(dy2, yg, gl, *, name):
    S, D = yg.shape
    tm = _tile(S, 512, 8)
    blk = pl.BlockSpec((tm, D), lambda i: (i, 0))

    def body(d_ref, y_ref, g_ref, da_ref, dg_ref):
        d = d_ref[...].astype(F32)
        sg = jax.nn.sigmoid(g_ref[...].astype(F32))
        da_ref[...] = d * sg
        dg_ref[...] = (d * y_ref[...].astype(F32) * sg * (1.0 - sg)).astype(dg_ref.dtype)

    return pl.pallas_call(body, name=name, grid=(S // tm,), in_specs=[blk, blk, blk], out_specs=[blk, blk],
                          out_shape=[jax.ShapeDtypeStruct((S, D), F32), jax.ShapeDtypeStruct((S, D), BF16)],
                          compiler_params=_cp("parallel"))(dy2, yg, gl)


def _gelu_bwd(da, db, ypre, u, *, name):
    S, D = ypre.shape
    tm = _tile(S, 512, 8)
    blk = pl.BlockSpec((tm, D), lambda i: (i, 0))
    vec = pl.BlockSpec((1, D), lambda i: (0, 0))

    def body(a_ref, b_ref, y_ref, u_ref, o_ref, s_ref):
        @pl.when(pl.program_id(0) == 0)
        def _():
            s_ref[...] = jnp.zeros_like(s_ref)

        dy = (a_ref[...] + b_ref[...]) * _gelu_grad(y_ref[...].astype(F32))
        o_ref[...] = dy.astype(o_ref.dtype)
        s_ref[...] += jnp.sum(dy * u_ref[...].astype(F32), axis=0, keepdims=True)

    return pl.pallas_call(body, name=name, grid=(S // tm,), in_specs=[blk, blk, blk, blk], out_specs=[blk, vec],
                          out_shape=[jax.ShapeDtypeStruct((S, D), BF16), jax.ShapeDtypeStruct((1, D), F32)],
                          compiler_params=_cp("arbitrary"))(da, db, ypre, u)


def _my_index():
    return 4 * lax.axis_index("x") + 2 * lax.axis_index("y") + lax.axis_index("c")


def _comm_plan(payloads):
    n = len(payloads)
    hbm = pl.BlockSpec(memory_space=pl.ANY)
    shapes = [jax.ShapeDtypeStruct((N_DEV,) + (x.shape if g else x.shape[1:]), x.dtype) for x, g in payloads]
    sems = [pltpu.SemaphoreType.DMA((n, N_DEV - 1)), pltpu.SemaphoreType.DMA((n, N_DEV - 1)), pltpu.SemaphoreType.DMA((n,))] if n else []
    return [hbm] * n, [hbm] * n, shapes, sems


def _comm_copies(x_refs, o_refs, gathers, send_sems, recv_sems, local_sems):
    mx, my, mc = lax.axis_index("x"), lax.axis_index("y"), lax.axis_index("c")
    me = 4 * mx + 2 * my + mc
    copies = []
    for i, (x_ref, o_ref, gather) in enumerate(zip(x_refs, o_refs, gathers)):
        src = (lambda j, r=x_ref: r) if gather else (lambda j, r=x_ref: r.at[j])
        copies.append(pltpu.make_async_copy(src(me), o_ref.at[me], local_sems.at[i]))
        for k in range(1, N_DEV):
            px, py, pc = mx ^ (k >> 2), my ^ ((k >> 1) & 1), mc ^ (k & 1)
            copies.append(pltpu.make_async_remote_copy(src(4 * px + 2 * py + pc), o_ref.at[me], send_sems.at[i, k - 1],
                                                       recv_sems.at[i, k - 1], device_id=(px, py, pc), device_id_type=MESH_ID))
    return copies


def _call(body, *, name, grid, in_specs, out_specs, out_shape, scratch_shapes, sem, args, comm=()):
    if not comm:
        outs = pl.pallas_call(body, name=name, grid=grid, in_specs=in_specs, out_specs=out_specs, out_shape=out_shape,
                              scratch_shapes=scratch_shapes, compiler_params=_cp(*sem))(*args)
        return outs, ()
    n, n_in, n_out = len(comm), len(in_specs), len(out_specs)
    c_in, c_out, c_shapes, c_sems = _comm_plan(comm)
    gathers = [g for _, g in comm]

    def wrapped(*refs):
        own_in, x_refs = refs[:n_in], refs[n_in:n_in + n]
        own_out, o_refs = refs[n_in + n:n_in + n + n_out], refs[n_in + n + n_out:n_in + 2 * n + n_out]
        own_scratch, sems = refs[n_in + 2 * n + n_out:-3], refs[-3:]
        ids = [pl.program_id(a) for a in range(len(grid))]
        first = functools.reduce(jnp.logical_and, [i == 0 for i in ids])
        last = functools.reduce(jnp.logical_and, [i == g - 1 for i, g in zip(ids, grid)])

        @pl.when(first)
        def _():
            for cp in _comm_copies(x_refs, o_refs, gathers, *sems):
                cp.start()

        body(*own_in, *own_out, *own_scratch)

        @pl.when(last)
        def _():
            for cp in _comm_copies(x_refs, o_refs, gathers, *sems):
                cp.wait()

    outs = pl.pallas_call(wrapped, name=name, grid=grid, in_specs=list(in_specs) + c_in, out_specs=list(out_specs) + c_out,
                          out_shape=list(out_shape) + c_shapes, scratch_shapes=list(scratch_shapes) + c_sems,
                          compiler_params=_cp(*["arbitrary"] * len(grid)))(*args, *[x for x, _ in comm])
    return outs[:n_out], outs[n_out:]


def _exchange(x, *, gather, name):
    in_specs, out_specs, shapes, sems = _comm_plan([(x, gather)])

    def body(x_ref, o_ref, send_sems, recv_sems, local_sems):
        copies = _comm_copies([x_ref], [o_ref], [gather], send_sems, recv_sems, local_sems)
        for cp in copies:
            cp.start()
        for cp in copies:
            cp.wait()

    return pl.pallas_call(body, name=name, in_specs=in_specs, out_specs=out_specs[0], out_shape=shapes[0], scratch_shapes=sems,
                          compiler_params=pltpu.CompilerParams(has_side_effects=True))(x)


def _sum8(x8, *, name):
    _, R, C = x8.shape
    tr = _tile(R, 256, 16)

    def body(x_ref, o_ref):
        acc = x_ref[0].astype(F32)
        for j in range(1, N_DEV):
            acc = acc + x_ref[j].astype(F32)
        o_ref[...] = acc

    return pl.pallas_call(body, name=name, grid=(R // tr,), in_specs=[pl.BlockSpec((N_DEV, tr, C), lambda i: (0, i, 0))],
                          out_specs=pl.BlockSpec((tr, C), lambda i: (i, 0)), out_shape=jax.ShapeDtypeStruct((R, C), F32),
                          compiler_params=_cp("parallel"))(x8)


def _pack(arrs, dtype, row_mult):
    flat = jnp.concatenate([a.reshape(-1).astype(dtype) for a in arrs])
    n = flat.shape[0]
    rows = -(-n // 1024)
    rows = -(-rows // row_mult) * row_mult
    return jnp.pad(flat, (0, rows * 1024 - n)).reshape(rows, 1024)


def _unpack(buf, shapes):
    lead = buf.shape[:-2]
    flat = buf.reshape(lead + (-1,))
    out, off = [], 0
    for s in shapes:
        n = math.prod(s)
        out.append(flat[..., off:off + n].reshape(lead + tuple(s)))
        off += n
    return out


def _adaln_fwd(c_all, w, b, *, name):
    nk, D, n = w.shape

    def body(c_ref, w_ref, b_ref, o_ref):
        cv = c_ref[...]
        sc = (cv * jax.nn.sigmoid(cv)).astype(BF16)
        o_ref[...] = jnp.dot(sc, w_ref[...].astype(BF16), preferred_element_type=F32) + b_ref[...]

    return pl.pallas_call(body, name=name, grid=(nk,),
                          in_specs=[pl.BlockSpec((N_DEV, D), lambda k: (0, 0)), pl.BlockSpec((None, D, n), lambda k: (k, 0, 0)),
                                    pl.BlockSpec((None, 1, n), lambda k: (k, 0, 0))],
                          out_specs=pl.BlockSpec((None, N_DEV, n), lambda k: (k, 0, 0)),
                          out_shape=jax.ShapeDtypeStruct((nk, N_DEV, n), F32), compiler_params=_cp("parallel"))(c_all, w, b)


def _adaln_bwd(c_all, dmod, *, name):
    nk, _, n = dmod.shape
    D = c_all.shape[1]

    def body(c_ref, d_ref, dw_ref, db_ref):
        cv = c_ref[...]
        sc = (cv * jax.nn.sigmoid(cv)).astype(BF16).astype(F32)
        dm = d_ref[...]
        dw_ref[...] = lax.dot_general(sc, dm.astype(BF16).astype(F32), TN_DIMS, precision=lax.Precision.HIGHEST,
                                      preferred_element_type=F32)
        db_ref[...] = jnp.sum(dm, axis=0, keepdims=True)

    return pl.pallas_call(body, name=name, grid=(nk,),
                          in_specs=[pl.BlockSpec((N_DEV, D), lambda k: (0, 0)), pl.BlockSpec((None, N_DEV, n), lambda k: (k, 0, 0))],
                          out_specs=[pl.BlockSpec((None, D, n), lambda k: (k, 0, 0)), pl.BlockSpec((None, 1, n), lambda k: (k, 0, 0))],
                          out_shape=[jax.ShapeDtypeStruct((nk, D, n), F32), jax.ShapeDtypeStruct((nk, 1, n), F32)],
                          compiler_params=_cp("parallel"))(c_all, dmod)


def _adamw(w, g, m, v, *, name):
    R, C = w.shape
    tr = _tile(R, 256, 8)
    blk = pl.BlockSpec((tr, C), lambda i: (i, 0))
    c1 = 1.0 / (1.0 - ADAM_B1 ** ADAM_STEP)
    c2 = 1.0 / (1.0 - ADAM_B2 ** ADAM_STEP)

    def body(w_ref, g_ref, m_ref, v_ref, d_ref, nm_ref, nv_ref):
        gv = g_ref[...]
        nm = ADAM_B1 * m_ref[...] + (1.0 - ADAM_B1) * gv
        nv = ADAM_B2 * v_ref[...] + (1.0 - ADAM_B2) * (gv * gv)
        nm_ref[...] = nm
        nv_ref[...] = nv
        d_ref[...] = -ADAM_LR * ((nm * c1) / (jnp.sqrt(nv * c2) + ADAM_EPS) + ADAM_WD * w_ref[...])

    sh = jax.ShapeDtypeStruct((R, C), F32)
    return pl.pallas_call(body, name=name, grid=(R // tr,), in_specs=[blk] * 4, out_specs=[blk] * 3, out_shape=[sh] * 3,
                          compiler_params=_cp("parallel"))(w, g, m, v)


def _adamw_nd(w, g, m, v, *, name):
    shp = w.shape
    two = (-1, shp[-1]) if w.ndim > 1 else (1, -1)
    outs = _adamw(w.reshape(two), g.reshape(two), m.reshape(two), v.reshape(two), name=name)
    return [o.reshape(shp) for o in outs]


S5_TILE = 256
ATTN_BLOCK = 512
BIG = ("s5_w_in", "s5_w_glu", "s5_w_out", "fox_w_in", "fox_w_out", "ffn_w_up", "ffn_w_down")
WEIGHTS = ("norm_g", "ada_w", "ada_b", "s5_w_in", "s5_lam_re", "s5_lam_im", "s5_log_dt", "s5_b_re", "s5_b_im", "s5_c_re",
           "s5_c_im", "s5_d", "s5_w_glu", "s5_w_out", "fox_w_in", "fox_b_f", "fox_w_out", "ffn_w_up", "ffn_conv_w",
           "ffn_conv_b", "ffn_w_down", "final_g")
REPLICATED = ("s5_lam_re", "s5_lam_im", "s5_log_dt", "s5_b_re", "s5_b_im", "s5_c_re", "s5_c_im", "s5_d", "fox_b_f",
              "ffn_conv_b", "final_g")


S5_MATS = ("s5_w_in", "s5_w_glu", "s5_w_out")


def _row_group(arrs):
    return jnp.concatenate([a.reshape(-1, D_MODEL) for a in arrs], axis=0)


def _cols_join(g):
    return jnp.concatenate([g[j] for j in range(N_DEV)], axis=1)


def _cols_split(full):
    n = full.shape[1] // N_DEV
    return jnp.stack([full[:, j * n:(j + 1) * n] for j in range(N_DEV)])


def kernel(x, c, norm_g, ada_w, ada_b, s5_w_in, s5_lam_re, s5_lam_im, s5_log_dt, s5_b_re, s5_b_im, s5_c_re, s5_c_im, s5_d, s5_w_glu, s5_w_out, fox_w_in, fox_b_f, fox_w_out, ffn_w_up, ffn_conv_w, ffn_conv_b, ffn_w_down, final_g, loss_target, m_norm_g, m_ada_w, m_ada_b, m_s5_w_in, m_s5_lam_re, m_s5_lam_im, m_s5_log_dt, m_s5_b_re, m_s5_b_im, m_s5_c_re, m_s5_c_im, m_s5_d, m_s5_w_glu, m_s5_w_out, m_fox_w_in, m_fox_b_f, m_fox_w_out, m_ffn_w_up, m_ffn_conv_w, m_ffn_conv_b, m_ffn_w_down, m_final_g, v_norm_g, v_ada_w, v_ada_b, v_s5_w_in, v_s5_lam_re, v_s5_lam_im, v_s5_log_dt, v_s5_b_re, v_s5_b_im, v_s5_c_re, v_s5_c_im, v_s5_d, v_s5_w_glu, v_s5_w_out, v_fox_w_in, v_fox_b_f, v_fox_w_out, v_ffn_w_up, v_ffn_conv_w, v_ffn_conv_b, v_ffn_w_down, v_final_g):
    args = dict(locals())
    W = {n: args[n] for n in WEIGHTS}
    M = {n: args["m_" + n] for n in WEIGHTS}
    V = {n: args["v_" + n] for n in WEIGHTS}
    D, F = D_MODEL, D_FF
    me = _my_index()
    h0 = x[0]
    S = h0.shape[0]
    lt = min(S5_TILE, S)
    tb = min(ATTN_BLOCK, S)
    n_g, n_cw = norm_g.size, ffn_conv_w.size

    g0 = _exchange(_pack([c, norm_g, ffn_conv_w], F32, 8), gather=True, name="gather_small_in")
    c_all, ng_all, cw_all = _unpack(g0, [(D,), norm_g.shape, ffn_conv_w.shape])
    ng_full = ng_all.transpose(1, 2, 0, 3).reshape(2, 2, D)
    cw_full = cw_all.transpose(1, 2, 0, 3).reshape(2, 3, F)

    ncol = ada_w.shape[-1]
    modp = _adaln_fwd(c_all, ada_w.reshape(4, D, ncol), ada_b.reshape(4, 1, ncol), name="adaln_fwd")
    g1 = _exchange(_pack([modp], F32, 8), gather=True, name="gather_adaln")
    (mod_all,) = _unpack(g1, [modp.shape])
    mod = lax.dynamic_index_in_dim(mod_all, me, axis=2, keepdims=False).transpose(1, 0, 2).reshape(4, 3 * D)
    shift = [mod[k:k + 1, :D] for k in range(4)]
    scale = [mod[k:k + 1, D:2 * D] for k in range(4)]
    gate = [mod[k:k + 1, 2 * D:] for k in range(4)]
    gain = [_row(ng_full[k // 2, k % 2]) for k in range(4)]

    b16 = lambda a: a.astype(BF16)
    rows = D // N_DEV
    g_s5 = _exchange(_row_group([b16(W[n]) for n in S5_MATS]), gather=True, name="gather_s5_weights")
    full = {n: g_s5[:, i * rows:(i + 1) * rows].reshape(D, D) for i, n in enumerate(S5_MATS)}
    bf_pad = jnp.pad(fox_b_f, ((0, 0), (0, 128 - HEADS)))

    def ffn_fwd(h, k, layer):
        hn = _modulate(h, gain[k], shift[k], scale[k], name=f"modulate{k}")
        up = _mm(hn, full[f"ffn_w_up{layer}"], name=f"ffn_up{layer}", o_h=True, out_dtype=BF16, tn=1408)
        z = _conv_gate_fwd(up, cw_full[layer], ffn_conv_b[layer:layer + 1], name=f"conv_gate{layer}")
        m, h_out = _mm(z, full[f"ffn_w_down{layer}"], name=f"ffn_down{layer}", res=h, gate=gate[k])
        return h_out, (hn, up, z, m)

    lb_re, lb_im, bb_re, bb_im = _s5_disc(s5_lam_re[0], s5_lam_im[0], s5_log_dt[0], s5_b_re[0], s5_b_im[0])
    wb, wc, lam = _s5_pack(lb_re, lb_im, bb_re, bb_im, s5_c_re[0], s5_c_im[0])
    hn0 = _modulate(h0, gain[0], shift[0], scale[0], name="modulate0")
    u = _mm(hn0, full["s5_w_in"], name="s5_in", out_dtype=BF16)
    (ypre, yg, st), (g_up0, g_rows0, g_fox) = _s5_fwd(
        u, wb, wc, lam, s5_d, lt=lt, name="s5_scan",
        comm=[(b16(ffn_w_up[0]), True), (_row_group([b16(fox_w_out), b16(ffn_w_down[0])]), True), (b16(fox_w_in[0]), True)])
    full["ffn_w_up0"] = _cols_join(g_up0)
    full["fox_w_out"] = g_rows0[:, :rows].reshape(D, D)
    full["ffn_w_down0"] = g_rows0[:, rows:].reshape(F, D)
    full["fox_w_in"] = _cols_join(g_fox)
    w_proj = jnp.pad(full["fox_w_in"], ((0, 0), (0, 3 * D + 128 - full["fox_w_in"].shape[1])))
    w_qkv, w_f = w_proj[:, :3 * D], w_proj[:, 3 * D:]
    gl = _mm(yg, full["s5_w_glu"], name="s5_glu", out_dtype=BF16)
    y2 = _glu_fwd(yg, gl, name="s5_glu_gate")
    m0, h1 = _mm(y2, full["s5_w_out"], name="s5_out", res=h0, gate=gate[0])
    h2, (hn1, up0, z0, m1) = ffn_fwd(h1, 1, 0)

    hn2 = _modulate(h2, gain[2], shift[2], scale[2], name="modulate2")
    nb = S // tb
    w_qkv_s = jnp.concatenate([w_qkv[:, :D] * Q_SCALE, w_qkv[:, D:]], axis=1)
    qkv = _mm(hn2, w_qkv_s, name="fox_qkv", out_dtype=BF16)
    qkvT = _mm(w_qkv_s, hn2, name="fox_qkv_t", ta=True, tb=True, out_dtype=BF16)
    fl = _mm(hn2, w_f, name="fox_f", out_dtype=F32)
    fcol = _fgate_fwd(fl, bf_pad, name="fox_fgate")
    f_heads = fcol[:, :HEADS].T.reshape(HP, 2, S)
    ka = _fox_keys(qkv, fcol, name="fox_keys")
    vtb = _blocked(qkvT[2 * D:].reshape(HP, 128, S), nb, tb)
    aug = _aug_rows(f_heads)
    (oT, lse), (g_up1, g_down1) = _foxt_fwd(qkvT, ka, vtb, aug, tb=tb, name="fox_attn",
                                            comm=[(b16(ffn_w_up[1]), True), (b16(ffn_w_down[1]), True)])
    full["ffn_w_up1"] = _cols_join(g_up1)
    full["ffn_w_down1"] = g_down1.reshape(F, D)
    m2, h3 = _mm(oT, full["fox_w_out"], name="fox_out", ta=True, res=h2, gate=gate[2])
    h4, (hn3, up1, z1, m3) = ffn_fwd(h3, 3, 1)

    lblk, dh, d_final_g = _loss_head(h4, _row(final_g), loss_target[0], name="loss_head")
    loss = lax.psum(lblk[0, 0], ("x", "y", "c"))

    G = {}
    dmod = [None] * 4

    def norm_bwd(h, dhn, dh_in, k):
        dh_out, sums = _modulate_bwd(h, dhn, dh_in, gain[k], scale[k], name=f"modulate_bwd{k}")
        return dh_out, sums[0:1], sums[1:2], sums[2]

    def ffn_bwd(dh_in, h, k, layer, saved):
        hn, up, z, m = saved
        dm, dgate = _gate_bwd(dh_in, m, gate[k], name=f"gate_bwd{k}")
        dz = _mm(dm, full[f"ffn_w_down{layer}"], name=f"ffn_down_dx{layer}", tb=True, out_dtype=BF16, tn=1408)
        dw_down = _mm(z, dm, name=f"ffn_down_dw{layer}", ta=True, tm=1408, tn=512, tk=512)
        d_up, cs = _conv_gate_bwd(up, dz, cw_full[layer], ffn_conv_b[layer:layer + 1], name=f"conv_gate_bwd{layer}")
        dhn = _mm(d_up, full[f"ffn_w_up{layer}"], name=f"ffn_up_dx{layer}", tb=True, a_h=True, tk=1408)
        dw_up = _mm(hn, d_up, name=f"ffn_up_dw{layer}", ta=True, b_h=True, tm=1024, tn=1408, tk=512)
        dh_out, dshift, dscale, dg = norm_bwd(h, dhn, dh_in, k)
        dmod[k] = jnp.concatenate([dshift, dscale, dgate], axis=1)
        return dh_out, dg, dw_up, dw_down, cs[0:3], cs[3]

    dh, dg3, dw_up1, dw_down1, dcw1, dcb1 = ffn_bwd(dh, h3, 3, 1, (hn3, up1, z1, m3))

    dm, dgate2 = _gate_bwd(dh, m2, gate[2], name="gate_bwd2")
    do = _mm(dm, full["fox_w_out"], name="fox_out_dx", tb=True, out_dtype=BF16)
    doT = _mm(full["fox_w_out"], dm, name="fox_out_dx_t", tb=True, out_dtype=BF16)
    dw_fox_out = _mm(oT, dm, name="fox_out_dw", tk=512)
    to_rows = lambda g: b16(g).reshape(N_DEV, -1, D)
    delta = _fox_delta(oT, doT, tb=tb, name="fox_attn_delta")
    (dqtb, dk, dv, dfk, dfq), (r_up1, r_down1) = _foxt_bwd(
        qkvT, qkv, ka, _blocked(qkvT[:D].reshape(HP, 128, S), nb, tb), _blocked(doT.reshape(HP, 128, S), nb, tb), do,
        _blocked(aug, nb, tb), _blocked(lse, nb, tb), _blocked(delta, nb, tb), tb=tb, name="fox_attn_bwd",
        comm=[(_cols_split(b16(dw_up1)), False), (to_rows(dw_down1), False)])
    dq = dqtb.transpose(1, 3, 0, 2).reshape(S, D)
    dF = dfk[:, :, :2].transpose(1, 0, 2).reshape(S, HEADS) + dfq[:, :, :2, :].transpose(1, 3, 0, 2).reshape(S, HEADS)
    dF = jnp.pad(dF, ((0, 0), (0, 128 - HEADS)))
    dfl, dbf = _fgate_bwd(dF, fl, bf_pad, name="fox_fgate_bwd")
    dproj = jnp.concatenate([dq, dk, dv, dfl.astype(BF16)], axis=1)
    dhn = _mm(dproj, w_proj, name="fox_in_dx", tb=True, tk=640)
    dw_proj = _mm(hn2, dproj, name="fox_in_dw", ta=True, tn=640, tk=512)
    dw_fox_in = dw_proj[:, :full["fox_w_in"].shape[1]]
    dh, dshift, dscale, dg2 = norm_bwd(h2, dhn, dh, 2)
    dmod[2] = jnp.concatenate([dshift, dscale, dgate2], axis=1)

    dh, dg1, dw_up0, dw_down0, dcw0, dcb0 = ffn_bwd(dh, h1, 1, 0, (hn1, up0, z0, m1))

    dm, dgate0 = _gate_bwd(dh, m0, gate[0], name="gate_bwd0")
    dy2 = _mm(dm, full["s5_w_out"], name="s5_out_dx", tb=True, out_dtype=BF16)
    G["s5_w_out"] = _mm(y2, dm, name="s5_out_dw", ta=True, tk=512)
    da, dgl = _glu_bwd(dy2, yg, gl, name="s5_glu_bwd")
    dyg_b = _mm(dgl, full["s5_w_glu"], name="s5_glu_dx", tb=True)
    G["s5_w_glu"] = _mm(yg, dgl, name="s5_glu_dw", ta=True, tk=512)
    dyp, dd = _gelu_bwd(da, dyg_b, ypre, u, name="s5_gelu_bwd")
    (du, dwb, dwc, dlam), (r_up0, r_rows0, r_fox) = _s5_bwd(
        u, dyp, st, wb, wc, lam, s5_d, lt=lt, name="s5_scan_bwd",
        comm=[(_cols_split(b16(dw_up0)), False), (jnp.concatenate([to_rows(dw_fox_out), to_rows(dw_down0)], axis=1), False),
              (_cols_split(b16(dw_fox_in)), False)])
    dhn = _mm(du, full["s5_w_in"], name="s5_in_dx", tb=True)
    G["s5_w_in"] = _mm(hn0, du, name="s5_in_dw", ta=True, tk=512)
    dh, dshift, dscale, dg0 = norm_bwd(h0, dhn, dh, 0)
    dmod[0] = jnp.concatenate([dshift, dscale, dgate0], axis=1)
    grad_x = dh[None]

    dlb_re, dlb_im, dbb_re, dbb_im, dc_re, dc_im = _s5_unpack(dwb, dwc, dlam)
    _, disc_vjp = jax.vjp(_s5_disc, s5_lam_re[0], s5_lam_im[0], s5_log_dt[0], s5_b_re[0], s5_b_im[0])
    dlam_re, dlam_im, dlog_dt, db_re, db_im = disc_vjp((dlb_re, dlb_im, dbb_re, dbb_im))

    g2 = _exchange(_pack([jnp.concatenate(dmod, axis=0)], F32, 8), gather=True, name="gather_dmod")
    (dmod_all,) = _unpack(g2, [(4, 3 * D)])
    dmod_mine = _columns_of_mod(dmod_all, me, ncol)
    d_ada_w, d_ada_b = _adaln_bwd(c_all, dmod_mine, name="adaln_bwd")
    G["ada_w"] = d_ada_w.reshape(ada_w.shape)
    G["ada_b"] = d_ada_b.reshape(ada_b.shape)

    r_s5 = _exchange(jnp.concatenate([to_rows(G[n]) for n in S5_MATS], axis=1), gather=False, name="scatter_s5_grads")
    s5_sum = _sum8(r_s5, name="sum_s5_grads")
    for i, n in enumerate(S5_MATS):
        G[n] = s5_sum[i * rows:(i + 1) * rows][None]
    rows0_sum = _sum8(r_rows0, name="sum_rows0_grads")
    G["fox_w_out"] = rows0_sum[:rows][None]
    G["ffn_w_down"] = jnp.stack([rows0_sum[rows:], _sum8(r_down1, name="sum_down1_grads")])
    G["ffn_w_up"] = jnp.stack([_sum8(r_up0, name="sum_up0_grads"), _sum8(r_up1, name="sum_up1_grads")])
    G["fox_w_in"] = _sum8(r_fox, name="sum_fox_w_in")[None]

    small_full = {
        "norm_g": jnp.stack([dg0, dg1, dg2, dg3]).reshape(2, 2, D),
        "ffn_conv_w": jnp.stack([dcw0, dcw1]),
        "s5_lam_re": dlam_re[None], "s5_lam_im": dlam_im[None], "s5_log_dt": dlog_dt[None],
        "s5_b_re": db_re[None], "s5_b_im": db_im[None], "s5_c_re": dc_re[None], "s5_c_im": dc_im[None],
        "s5_d": dd, "fox_b_f": dbf[:, :HEADS], "ffn_conv_b": jnp.stack([dcb0, dcb1]), "final_g": d_final_g[0],
    }
    names = tuple(small_full)
    g3 = _exchange(_pack([small_full[n] for n in names], F32, 8), gather=True, name="gather_small_grads")
    summed = dict(zip(names, _unpack(_sum8(g3, name="sum_small_grads"), [small_full[n].shape for n in names])))
    G["norm_g"] = lax.dynamic_slice_in_dim(summed["norm_g"], me * norm_g.shape[-1], norm_g.shape[-1], axis=2)
    G["ffn_conv_w"] = lax.dynamic_slice_in_dim(summed["ffn_conv_w"], me * ffn_conv_w.shape[-1], ffn_conv_w.shape[-1], axis=2)
    for n in REPLICATED:
        G[n] = summed[n]

    delta, new_m, new_v = {}, {}, {}
    small = tuple(n for n in WEIGHTS if n not in BIG and n != "ada_w")
    for n in WEIGHTS:
        if n not in small:
            delta[n], new_m[n], new_v[n] = _adamw_nd(W[n], G[n], M[n], V[n], name=f"adamw_{n}")
    packed = [_pack([src[n] for n in small], F32, 8) for src in (W, G, M, V)]
    for dst, buf in zip((delta, new_m, new_v), _adamw(*packed, name="adamw_small")):
        dst.update(zip(small, _unpack(buf, [W[n].shape for n in small])))

    return (loss, grad_x, *[G[n] for n in WEIGHTS], *[delta[n] for n in WEIGHTS], *[new_m[n] for n in WEIGHTS],
            *[new_v[n] for n in WEIGHTS])


def _columns_of_mod(dmod_all, me, ncol):
    flat = lax.dynamic_slice_in_dim(dmod_all, me * ncol, ncol, axis=2)
    return flat.transpose(1, 0, 2)
```

```python
import functools
import math

import jax
import jax.numpy as jnp
from jax import lax
from jax.experimental import pallas as pl
from jax.experimental.pallas import tpu as pltpu

F32, BF16 = jnp.float32, jnp.bfloat16
EPS = 1e-6
N_DEV = 8
D_MODEL = 1024
D_FF = 2816
HEADS = 16
HEAD_DIM = 64
S5_G, S5_P, S5_C = 64, 64, 16
S5_SUB = 8
V7X_VMEM_LIMIT = 56 * 1024 * 1024
NEG = -1e30
ADAM_LR, ADAM_B1, ADAM_B2, ADAM_EPS, ADAM_WD, ADAM_STEP = 1e-3, 0.9, 0.999, 1e-8, 0.01, 10
GELU_K = math.sqrt(2.0 / math.pi)
MESH_ID = pl.DeviceIdType.MESH


def _cp(*sem):
    return pltpu.CompilerParams(dimension_semantics=sem, vmem_limit_bytes=V7X_VMEM_LIMIT)


def _tile(n, target, mult=128):
    if n <= target:
        return n
    t = (target // mult) * mult
    while t >= mult:
        if n % t == 0:
            return t
        t -= mult
    return n


def _row(v):
    return v.reshape(1, -1).astype(F32)


def _mm(a, b, *, name, ta=False, tb=False, out_dtype=F32, tm=1024, tn=512, tk=None, res=None, gate=None,
        a_h=False, b_h=False, o_h=False):
    if a_h:
        M, K = a.shape[1], 2 * a.shape[2]
    elif ta:
        K, M = a.shape
    else:
        M, K = a.shape
    if b_h:
        N = 2 * b.shape[2]
    else:
        N = b.shape[0] if tb else b.shape[1]
    half_n = N // 2
    tm = _tile(M, tm, 128 if ta else 8)
    tn = _tile(half_n if (b_h or o_h) else N, tn)
    tk = K if tk is None else _tile(K // 2 if a_h else K, tk)
    nk = K // tk
    nkh, nnh = (K // 2) // tk if a_h else 1, half_n // tn
    if a_h:
        a_spec = pl.BlockSpec((None, tm, tk), lambda i, j, k: (k // nkh, i, k % nkh))
    elif ta:
        a_spec = pl.BlockSpec((tk, tm), lambda i, j, k: (k, i))
    else:
        a_spec = pl.BlockSpec((tm, tk), lambda i, j, k: (i, k))
    if b_h:
        b_spec = pl.BlockSpec((None, tk, tn), lambda i, j, k: (j // nnh, k, j % nnh))
    elif tb:
        b_spec = pl.BlockSpec((tn, tk), lambda i, j, k: (j, k))
    else:
        b_spec = pl.BlockSpec((tk, tn), lambda i, j, k: (k, j))
    if o_h:
        o_spec = pl.BlockSpec((None, tm, tn), lambda i, j, k: (j // nnh, i, j % nnh))
    else:
        o_spec = pl.BlockSpec((tm, tn), lambda i, j, k: (i, j))
    dn = (((0 if ta else 1,), (1 if tb else 0,)), ((), ()))
    fused = res is not None

    def body(*refs):
        if fused:
            a_ref, b_ref, r_ref, g_ref, m_ref, o_ref, acc_ref = refs
        else:
            a_ref, b_ref, o_ref, acc_ref = refs
        p = lax.dot_general(a_ref[...].astype(BF16), b_ref[...].astype(BF16), dn, preferred_element_type=F32)

        def finish(acc):
            if fused:
                m_ref[...] = acc.astype(m_ref.dtype)
                o_ref[...] = r_ref[...] + g_ref[...] * acc
            else:
                o_ref[...] = acc.astype(o_ref.dtype)

        if nk == 1:
            finish(p)
        else:
            k = pl.program_id(2)

            @pl.when(k == 0)
            def _():
                acc_ref[...] = p

            @pl.when(k > 0)
            def _():
                acc_ref[...] += p

            @pl.when(k == nk - 1)
            def _():
                finish(acc_ref[...])

    in_specs = [a_spec, b_spec]
    args = [a, b]
    if fused:
        in_specs += [o_spec, pl.BlockSpec((1, tn), lambda i, j, k: (0, j))]
        args += [res, gate]
        out_shape = [jax.ShapeDtypeStruct((M, N), BF16), jax.ShapeDtypeStruct((M, N), F32)]
        out_specs = [o_spec, o_spec]
    else:
        out_shape = jax.ShapeDtypeStruct((2, M, half_n) if o_h else (M, N), out_dtype)
        out_specs = o_spec
    return pl.pallas_call(
        body, name=name, grid=(M // tm, N // tn, nk), in_specs=in_specs, out_specs=out_specs, out_shape=out_shape,
        scratch_shapes=[pltpu.VMEM((tm, tn) if nk > 1 else (8, 128), F32)],
        compiler_params=_cp("parallel", "parallel", "arbitrary"),
    )(*args)


def _modulate(h, g, shift, scale, *, name):
    S, D = h.shape
    tm = _tile(S, 512, 8)
    vec = pl.BlockSpec((1, D), lambda i: (0, 0))
    blk = pl.BlockSpec((tm, D), lambda i: (i, 0))

    def body(h_ref, g_ref, sh_ref, sc_ref, o_ref):
        x = h_ref[...]
        r = lax.rsqrt(jnp.mean(x * x, axis=-1, keepdims=True) + EPS)
        o_ref[...] = ((x * r * g_ref[...]) * (1.0 + sc_ref[...]) + sh_ref[...]).astype(o_ref.dtype)

    return pl.pallas_call(body, name=name, grid=(S // tm,), in_specs=[blk, vec, vec, vec], out_specs=blk,
                          out_shape=jax.ShapeDtypeStruct((S, D), BF16), compiler_params=_cp("parallel"))(h, g, shift, scale)


def _gate_bwd(dh, m, gate, *, name):
    S, D = dh.shape
    tm = _tile(S, 512, 8)
    vec = pl.BlockSpec((1, D), lambda i: (0, 0))
    blk = pl.BlockSpec((tm, D), lambda i: (i, 0))

    def body(dh_ref, m_ref, g_ref, dm_ref, dg_ref):
        @pl.when(pl.program_id(0) == 0)
        def _():
            dg_ref[...] = jnp.zeros_like(dg_ref)

        d = dh_ref[...]
        dm_ref[...] = (d * g_ref[...]).astype(dm_ref.dtype)
        dg_ref[...] += jnp.sum(d * m_ref[...].astype(F32), axis=0, keepdims=True)

    return pl.pallas_call(body, name=name, grid=(S // tm,), in_specs=[blk, blk, vec], out_specs=[blk, vec],
                          out_shape=[jax.ShapeDtypeStruct((S, D), BF16), jax.ShapeDtypeStruct((1, D), F32)],
                          compiler_params=_cp("arbitrary"))(dh, m, gate)


def _modulate_bwd(h, dhn, dh_in, g, scale, *, name):
    S, D = h.shape
    tm = _tile(S, 512, 8)
    vec = pl.BlockSpec((1, D), lambda i: (0, 0))
    blk = pl.BlockSpec((tm, D), lambda i: (i, 0))
    sums = pl.BlockSpec((8, D), lambda i: (0, 0))

    def body(h_ref, dhn_ref, dhi_ref, g_ref, sc_ref, dh_ref, s_ref):
        @pl.when(pl.program_id(0) == 0)
        def _():
            s_ref[...] = jnp.zeros_like(s_ref)

        x = h_ref[...]
        r = lax.rsqrt(jnp.mean(x * x, axis=-1, keepdims=True) + EPS)
        xhat = x * r
        gv = g_ref[...]
        dhn_v = dhn_ref[...].astype(F32)
        dn = dhn_v * (1.0 + sc_ref[...])
        s_ref[0:1, :] += jnp.sum(dhn_v, axis=0, keepdims=True)
        s_ref[1:2, :] += jnp.sum(dhn_v * (xhat * gv), axis=0, keepdims=True)
        s_ref[2:3, :] += jnp.sum(dn * xhat, axis=0, keepdims=True)
        dxh = dn * gv
        dh_ref[...] = dhi_ref[...] + r * (dxh - xhat * jnp.mean(dxh * xhat, axis=-1, keepdims=True))

    return pl.pallas_call(body, name=name, grid=(S // tm,), in_specs=[blk, blk, blk, vec, vec], out_specs=[blk, sums],
                          out_shape=[jax.ShapeDtypeStruct((S, D), F32), jax.ShapeDtypeStruct((8, D), F32)],
                          compiler_params=_cp("arbitrary"))(h, dhn, dh_in, g, scale)


def _loss_head(h, g, tgt, *, name):
    S, D = h.shape
    tm = _tile(S, 512, 8)
    vec = pl.BlockSpec((1, D), lambda i: (0, 0))
    blk = pl.BlockSpec((tm, D), lambda i: (i, 0))
    lss = pl.BlockSpec((8, 128), lambda i: (0, 0))

    def body(h_ref, g_ref, t_ref, l_ref, dh_ref, dg_ref):
        @pl.when(pl.program_id(0) == 0)
        def _():
            l_ref[...] = jnp.zeros_like(l_ref)
            dg_ref[...] = jnp.zeros_like(dg_ref)

        x = h_ref[...]
        r = lax.rsqrt(jnp.mean(x * x, axis=-1, keepdims=True) + EPS)
        xhat = x * r
        gv = g_ref[...]
        e = xhat * gv - t_ref[...]
        l_ref[...] += 0.5 * jnp.sum(jnp.mean(e * e, axis=-1, keepdims=True))
        dy = e * (1.0 / D)
        dg_ref[...] += jnp.sum(dy * xhat, axis=0, keepdims=True)
        dxh = dy * gv
        dh_ref[...] = r * (dxh - xhat * jnp.mean(dxh * xhat, axis=-1, keepdims=True))

    return pl.pallas_call(body, name=name, grid=(S // tm,), in_specs=[blk, vec, blk], out_specs=[lss, blk, vec],
                          out_shape=[jax.ShapeDtypeStruct((8, 128), F32), jax.ShapeDtypeStruct((S, D), F32),
                                     jax.ShapeDtypeStruct((1, D), F32)],
                          compiler_params=_cp("arbitrary"))(h, g, tgt)


def _shift_down(x, k, edge):
    tm = x.shape[0]
    rows = lax.broadcasted_iota(jnp.int32, x.shape, 0)
    out = pltpu.roll(x, k, 0)
    for j in range(k):
        out = jnp.where(rows == j, edge[8 - k + j:8 - k + j + 1, :], out)
    return out


def _shift_up(x, k, edge):
    tm = x.shape[0]
    rows = lax.broadcasted_iota(jnp.int32, x.shape, 0)
    out = pltpu.roll(x, tm - k, 0)
    for j in range(k):
        out = jnp.where(rows == tm - k + j, edge[j:j + 1, :], out)
    return out


def _conv_gate_fwd(up, cw, cb, *, name):
    _, S, F = up.shape
    tf = _tile(F, 1408)
    nf = F // tf
    tm = _tile(S, 512, 8)

    def body(a_ref, b_ref, w_ref, cb_ref, z_ref, edge_ref):
        @pl.when(pl.program_id(1) == 0)
        def _():
            edge_ref[...] = jnp.zeros_like(edge_ref)

        a = a_ref[...].astype(F32)
        edge = edge_ref[...]
        w = w_ref[...]
        ac = cb_ref[...] + w[2:3, :] * a + w[1:2, :] * _shift_down(a, 1, edge) + w[0:1, :] * _shift_down(a, 2, edge)
        edge_ref[...] = a[tm - 8:tm, :]
        z_ref[...] = (ac * jax.nn.sigmoid(ac) * b_ref[...].astype(F32)).astype(z_ref.dtype)

    return pl.pallas_call(
        body, name=name, grid=(nf, S // tm),
        in_specs=[pl.BlockSpec((None, tm, tf), lambda j, i: (0, i, j)), pl.BlockSpec((None, tm, tf), lambda j, i: (1, i, j)),
                  pl.BlockSpec((3, tf), lambda j, i: (0, j)), pl.BlockSpec((1, tf), lambda j, i: (0, j))],
        out_specs=pl.BlockSpec((tm, tf), lambda j, i: (i, j)),
        out_shape=jax.ShapeDtypeStruct((S, F), BF16), scratch_shapes=[pltpu.VMEM((8, tf), F32)],
        compiler_params=_cp("parallel", "arbitrary"))(up, up, cw, cb)


def _conv_gate_bwd(up, dz, cw, cb, *, name):
    _, S, F = up.shape
    tf = _tile(F, 1408)
    nf = F // tf
    tm = _tile(S, 512, 8)
    nt = S // tm
    hb = tm // 8

    def body(a_ref, ah_ref, b_ref, dz_ref, w_ref, cb_ref, d_ref, s_ref, edge_ref):
        i = pl.program_id(1)

        @pl.when(i == 0)
        def _():
            edge_ref[...] = jnp.zeros_like(edge_ref)
            s_ref[...] = jnp.zeros_like(s_ref)

        a = a_ref[...].astype(F32)
        halo = jnp.where(i == nt - 1, 0.0, ah_ref[...].astype(F32))
        w = w_ref[...]
        a1 = _shift_down(a, 1, halo)
        a2 = _shift_down(a, 2, halo)
        ac = cb_ref[...] + w[2:3, :] * a + w[1:2, :] * a1 + w[0:1, :] * a2
        sg = jax.nn.sigmoid(ac)
        dzv = dz_ref[...].astype(F32)
        d_ref[1] = (dzv * ac * sg).astype(d_ref.dtype)
        dac = dzv * b_ref[...].astype(F32) * (sg * (1.0 + ac * (1.0 - sg)))
        s_ref[0:1, :] += jnp.sum(dac * a2, axis=0, keepdims=True)
        s_ref[1:2, :] += jnp.sum(dac * a1, axis=0, keepdims=True)
        s_ref[2:3, :] += jnp.sum(dac * a, axis=0, keepdims=True)
        s_ref[3:4, :] += jnp.sum(dac, axis=0, keepdims=True)
        edge = edge_ref[...]
        da = w[2:3, :] * dac + w[1:2, :] * _shift_up(dac, 1, edge) + w[0:1, :] * _shift_up(dac, 2, edge)
        edge_ref[...] = dac[0:8, :]
        d_ref[0] = da.astype(d_ref.dtype)

    tile = lambda hlf: pl.BlockSpec((None, tm, tf), lambda j, i: (hlf, nt - 1 - i, j))
    d_up, sums = pl.pallas_call(
        body, name=name, grid=(nf, nt),
        in_specs=[tile(0),
                  pl.BlockSpec((None, 8, tf), lambda j, i: (0, jnp.maximum((nt - 1 - i) * hb - 1, 0), j)),
                  tile(1), pl.BlockSpec((tm, tf), lambda j, i: (nt - 1 - i, j)),
                  pl.BlockSpec((3, tf), lambda j, i: (0, j)), pl.BlockSpec((1, tf), lambda j, i: (0, j))],
        out_specs=[pl.BlockSpec((2, tm, tf), lambda j, i: (0, nt - 1 - i, j)), pl.BlockSpec((8, tf), lambda j, i: (0, j))],
        out_shape=[jax.ShapeDtypeStruct((2, S, F), BF16), jax.ShapeDtypeStruct((8, F), F32)],
        scratch_shapes=[pltpu.VMEM((8, tf), F32)],
        compiler_params=_cp("parallel", "arbitrary"))(up, up, up, dz, cw, cb)
    return d_up, sums


def _log_sigmoid(x):
    return jnp.minimum(x, 0.0) - jnp.log(1.0 + jnp.exp(-jnp.abs(x)))


def _tri_ones(n, upper):
    r = lax.broadcasted_iota(jnp.int32, (n, n), 0)
    c = lax.broadcasted_iota(jnp.int32, (n, n), 1)
    return jnp.where((c >= r) if upper else (c <= r), 1.0, 0.0).astype(F32)


def _fgate_fwd(fl, bf, *, name):
    S, W = fl.shape
    tb = _tile(S, 256, 8)

    def body(fl_ref, b_ref, o_ref, carry_ref):
        @pl.when(pl.program_id(0) == 0)
        def _():
            carry_ref[...] = jnp.zeros_like(carry_ref)

        lf = _log_sigmoid(fl_ref[...] + b_ref[...])
        cs = jnp.dot(_tri_ones(tb, False), lf, precision=lax.Precision.HIGHEST, preferred_element_type=F32) + carry_ref[0:1, :]
        o_ref[...] = cs
        carry_ref[...] = jnp.broadcast_to(cs[tb - 1:tb, :], carry_ref.shape)

    blk = pl.BlockSpec((tb, W), lambda i: (i, 0))
    return pl.pallas_call(body, name=name, grid=(S // tb,), in_specs=[blk, pl.BlockSpec((1, W), lambda i: (0, 0))], out_specs=blk,
                          out_shape=jax.ShapeDtypeStruct((S, W), F32), scratch_shapes=[pltpu.VMEM((8, W), F32)],
                          compiler_params=_cp("arbitrary"))(fl, bf)


def _fgate_bwd(dF, fl, bf, *, name):
    S, W = fl.shape
    tb = _tile(S, 256, 8)
    nb = S // tb

    def body(d_ref, fl_ref, b_ref, o_ref, s_ref, carry_ref):
        @pl.when(pl.program_id(0) == 0)
        def _():
            carry_ref[...] = jnp.zeros_like(carry_ref)
            s_ref[...] = jnp.zeros_like(s_ref)

        rc = jnp.dot(_tri_ones(tb, True), d_ref[...], precision=lax.Precision.HIGHEST, preferred_element_type=F32) + carry_ref[0:1, :]
        carry_ref[...] = jnp.broadcast_to(rc[0:1, :], carry_ref.shape)
        dfl = rc * jax.nn.sigmoid(-(fl_ref[...] + b_ref[...]))
        o_ref[...] = dfl
        s_ref[...] += jnp.sum(dfl, axis=0, keepdims=True)

    blk = pl.BlockSpec((tb, W), lambda i: (nb - 1 - i, 0))
    vec = pl.BlockSpec((1, W), lambda i: (0, 0))
    return pl.pallas_call(body, name=name, grid=(nb,), in_specs=[blk, blk, vec], out_specs=[blk, vec],
                          out_shape=[jax.ShapeDtypeStruct((S, W), F32), jax.ShapeDtypeStruct((1, W), F32)],
                          scratch_shapes=[pltpu.VMEM((8, W), F32)], compiler_params=_cp("arbitrary"))(dF, fl, bf)


NT_DIMS = (((1,), (1,)), ((), ()))
TN_DIMS = (((0,), (0,)), ((), ()))
HP = HEADS // 2
Q_SCALE = HEAD_DIM ** -0.5


def _head_mask(x, hh):
    lanes = lax.broadcasted_iota(jnp.int32, x.shape, 1)
    return jnp.where((lanes >= hh * HEAD_DIM) & (lanes < (hh + 1) * HEAD_DIM), x, jnp.zeros_like(x))


def _lanes_from(cols, shape):
    lanes = lax.broadcasted_iota(jnp.int32, shape, 1)
    out = jnp.zeros(shape, F32)
    for i, cvec in enumerate(cols):
        out = jnp.where(lanes == i, cvec, out)
    return out


AUG_ROWS = 16


def _qa_pair(qt, aug):
    fill = jnp.zeros((HEAD_DIM - AUG_ROWS, qt.shape[1]), qt.dtype)
    return [jnp.concatenate([qt[0:HEAD_DIM], aug[0], fill], axis=0), jnp.concatenate([aug[1], fill, qt[HEAD_DIM:]], axis=0)]


def _rows_of_head(xt, hh):
    rows = lax.broadcasted_iota(jnp.int32, xt.shape, 0)
    return jnp.where((rows >= hh * HEAD_DIM) & (rows < (hh + 1) * HEAD_DIM), xt, jnp.zeros_like(xt))


def _put_rows(ref, rows):
    for i, r in enumerate(rows):
        ref[i:i + 1, :] = r
    ref[len(rows):, :] = jnp.zeros((ref.shape[0] - len(rows), ref.shape[1]), ref.dtype)


def _diag_mask_t(s):
    rows = lax.broadcasted_iota(jnp.int32, s.shape, 0)
    cols = lax.broadcasted_iota(jnp.int32, s.shape, 1)
    return jnp.where(rows <= cols, s, NEG)


def _foxt_fwd(qkvT, ka, vtb, augq, *, tb, name, comm=()):
    S = qkvT.shape[1]
    nb = S // tb

    def body(q_ref, ka_ref, v_ref, aq_ref, o_ref, lse_ref, m0, m1, l0, l1, acc0, acc1):
        qi = pl.program_id(1)
        qa = _qa_pair(q_ref[...], aq_ref[...])
        state = ((m0, l0, acc0), (m1, l1, acc1))
        for m_sc, l_sc, acc_sc in state:
            m_sc[...] = jnp.full_like(m_sc, NEG)
            l_sc[...] = jnp.zeros_like(l_sc)
            acc_sc[...] = jnp.zeros_like(acc_sc)

        def block(kj, masked):
            k0 = pl.multiple_of(kj * tb, tb)
            logits = [jnp.dot(ka_ref[hh, pl.ds(k0, tb), :], qa[hh], preferred_element_type=F32) for hh in range(2)]
            updates = []
            for hh, (m_sc, l_sc, acc_sc) in enumerate(state):
                s = _diag_mask_t(logits[hh]) if masked else logits[hh]
                m_old = m_sc[...]
                m_new = jnp.maximum(m_old, jnp.max(s, axis=0, keepdims=True))
                alpha = jnp.exp(m_old - m_new)
                p = jnp.exp(s - m_new)
                l_sc[...] = alpha * l_sc[...] + jnp.sum(p, axis=0, keepdims=True)
                m_sc[...] = m_new
                vt = v_ref[kj, HEAD_DIM * hh:HEAD_DIM * (hh + 1), :]
                updates.append((alpha, jnp.dot(vt, p.astype(BF16), preferred_element_type=F32)))
            for (alpha, pv), (_, _, acc_sc) in zip(updates, state):
                acc_sc[...] = alpha * acc_sc[...] + pv

        def off_diagonal(kj, c):
            block(kj, False)
            return c

        lax.fori_loop(0, qi, off_diagonal, 0)
        block(qi, True)
        for hh, (m_sc, l_sc, acc_sc) in enumerate(state):
            o_ref[HEAD_DIM * hh:HEAD_DIM * (hh + 1), :] = (acc_sc[...] / l_sc[...]).astype(o_ref.dtype)
        _put_rows(lse_ref, [m_sc[...] + jnp.log(l_sc[...]) for m_sc, l_sc, _ in state])

    return _call(
        body, name=name, grid=(HP, nb), comm=comm, sem=("parallel", "parallel"), args=(qkvT, ka, vtb, augq),
        in_specs=[pl.BlockSpec((128, tb), lambda hp, qi: (hp, qi)),
                  pl.BlockSpec((None, 2, S, 128), lambda hp, qi: (hp, 0, 0, 0)),
                  pl.BlockSpec((None, nb, 128, tb), lambda hp, qi: (hp, 0, 0, 0)),
                  pl.BlockSpec((None, 2, AUG_ROWS, tb), lambda hp, qi: (hp, 0, 0, qi))],
        out_specs=[pl.BlockSpec((128, tb), lambda hp, qi: (hp, qi)), pl.BlockSpec((None, 8, tb), lambda hp, qi: (hp, 0, qi))],
        out_shape=[jax.ShapeDtypeStruct((D_MODEL, S), BF16), jax.ShapeDtypeStruct((HP, 8, S), F32)],
        scratch_shapes=[pltpu.VMEM((1, tb), F32)] * 4 + [pltpu.VMEM((HEAD_DIM, tb), F32)] * 2)


def _fox_delta(oT, doT, *, tb, name):
    S = oT.shape[1]

    def body(o_ref, do_ref, dl_ref):
        prod = o_ref[...].astype(F32) * do_ref[...].astype(F32)
        _put_rows(dl_ref, [jnp.sum(prod[HEAD_DIM * hh:HEAD_DIM * (hh + 1)], axis=0, keepdims=True) for hh in range(2)])

    blk = pl.BlockSpec((128, tb), lambda hp, qi: (hp, qi))
    return pl.pallas_call(body, name=name, grid=(HP, S // tb), in_specs=[blk, blk],
                          out_specs=pl.BlockSpec((None, 8, tb), lambda hp, qi: (hp, 0, qi)),
                          out_shape=jax.ShapeDtypeStruct((HP, 8, S), F32), compiler_params=_cp("parallel", "parallel"))(oT, doT)


def _foxt_bwd(qkvT, qkv, ka, qtb, dotb, do, augqb, lseb, deltab, *, tb, name, comm=()):
    S = qkv.shape[0]
    nb = S // tb

    def body(ka_ref, v_ref, kt_ref, qt_ref, dot_ref, qn_ref, dn_ref, aq_ref, ls_ref, dl_ref,
             dq_ref, dk_ref, dv_ref, df_ref, dr_ref, dq_sc, dk_sc, dv_sc, ds_sc, dr_sc):
        kj = pl.program_id(1)

        @pl.when(kj == 0)
        def _():
            dq_sc[...] = jnp.zeros_like(dq_sc)
            dr_sc[...] = jnp.zeros_like(dr_sc)

        dk_sc[...] = jnp.zeros_like(dk_sc)
        dv_sc[...] = jnp.zeros_like(dv_sc)
        ds_sc[...] = jnp.zeros_like(ds_sc)
        v2 = v_ref[...]
        kth = [_rows_of_head(kt_ref[...], hh) for hh in range(2)]

        def block(qi, masked):
            q0 = pl.multiple_of(qi * tb, tb)
            qa = _qa_pair(qt_ref[qi], aq_ref[qi])
            dot, dl, ls = dot_ref[qi], dl_ref[qi], ls_ref[qi]
            qn, dn = qn_ref[pl.ds(q0, tb), :], dn_ref[pl.ds(q0, tb), :]
            dq_new = dq_sc[qi]
            for hh in range(2):
                s = jnp.dot(ka_ref[hh], qa[hh], preferred_element_type=F32)
                if masked:
                    s = _diag_mask_t(s)
                p = jnp.exp(s - ls[hh:hh + 1, :])
                dp = jnp.dot(v2, _rows_of_head(dot, hh), preferred_element_type=F32)
                ds = p * (dp - dl[hh:hh + 1, :])
                dsb = ds.astype(BF16)
                dv_sc[...] += jnp.dot(p.astype(BF16), _head_mask(dn, hh), preferred_element_type=F32)
                dk_sc[...] += jnp.dot(dsb, _head_mask(qn, hh), preferred_element_type=F32)
                dq_new = dq_new + jnp.dot(kth[hh], dsb, preferred_element_type=F32)
                part = ds[:, 0:128]
                for j in range(1, tb // 128):
                    part = part + ds[:, 128 * j:128 * (j + 1)]
                ds_sc[hh] += part
                dr_sc[qi, hh:hh + 1, :] += jnp.sum(ds, axis=0, keepdims=True)
            dq_sc[qi] = dq_new

        def off_diagonal(i, c):
            block(kj + 1 + i, False)
            return c

        block(kj, True)
        lax.fori_loop(0, nb - 1 - kj, off_diagonal, 0)
        dk_ref[...] = dk_sc[...].astype(dk_ref.dtype)
        dv_ref[...] = dv_sc[...].astype(dv_ref.dtype)
        df_ref[...] = _lanes_from([-jnp.sum(ds_sc[hh], axis=1, keepdims=True) for hh in range(2)], (tb, 128))

        @pl.when(kj == nb - 1)
        def _():
            dq_ref[...] = (dq_sc[...] * Q_SCALE).astype(dq_ref.dtype)
            dr_ref[...] = dr_sc[...]

    resident = lambda rows: pl.BlockSpec((None, nb) + rows, lambda hp, kj: (hp,) + (0,) * (len(rows) + 1))
    whole = pl.BlockSpec((S, 128), lambda hp, kj: (0, hp))
    kblk = lambda off: pl.BlockSpec((tb, 128), lambda hp, kj: (kj, off + hp))
    return _call(
        body, name=name, grid=(HP, nb), comm=comm, sem=("parallel", "arbitrary"),
        args=(ka, qkv, qkvT, qtb, dotb, qkv, do, augqb, lseb, deltab),
        in_specs=[pl.BlockSpec((None, 2, tb, 128), lambda hp, kj: (hp, 0, kj, 0)), kblk(2 * HP),
                  pl.BlockSpec((128, tb), lambda hp, kj: (HP + hp, kj)),
                  resident((128, tb)), resident((128, tb)), whole, whole, resident((2, AUG_ROWS, tb)), resident((8, tb)),
                  resident((8, tb))],
        out_specs=[resident((128, tb)), kblk(0), kblk(0), pl.BlockSpec((None, tb, 128), lambda hp, kj: (hp, kj, 0)),
                   resident((8, tb))],
        out_shape=[jax.ShapeDtypeStruct((HP, nb, 128, tb), BF16), jax.ShapeDtypeStruct((S, D_MODEL), BF16),
                   jax.ShapeDtypeStruct((S, D_MODEL), BF16), jax.ShapeDtypeStruct((HP, S, 128), F32),
                   jax.ShapeDtypeStruct((HP, nb, 8, tb), F32)],
        scratch_shapes=[pltpu.VMEM((nb, 128, tb), F32), pltpu.VMEM((tb, 128), F32), pltpu.VMEM((tb, 128), F32),
                        pltpu.VMEM((2, tb, 128), F32), pltpu.VMEM((nb, 8, tb), F32)])


def _split3(x):
    rnd = lambda v: lax.reduce_precision(v, exponent_bits=8, mantissa_bits=7)
    hi = rnd(x)
    mid = rnd(x - hi)
    lo = rnd(x - hi - mid)
    return hi.astype(BF16), mid.astype(BF16), lo.astype(BF16)


def _blocked(xt, nb, tb):
    lead = xt.shape[:-1]
    x = xt.reshape(lead + (nb, tb))
    return jnp.moveaxis(x, -2, 1)


def _aug_rows(bias):
    ones = jnp.ones(bias.shape, BF16)
    zeros = jnp.zeros(bias.shape, BF16)
    return jnp.stack(list(_split3(bias)) + [ones] * 3 + [zeros] * (AUG_ROWS - 6), axis=2)


def _fox_keys(qkv, fcol, *, name):
    S = qkv.shape[0]
    tr = _tile(S, 512, 16)

    def body(k_ref, f_ref, o_ref):
        hp = pl.program_id(0)
        k2 = k_ref[...]
        lanes = lax.broadcasted_iota(jnp.int32, (tr, 128), 1)
        for hh in range(2):
            nf = -jnp.sum(jnp.where(lanes == 2 * hp + hh, f_ref[...], 0.0), axis=1, keepdims=True)
            hi = nf.astype(BF16).astype(F32)
            mid = (nf - hi).astype(BF16).astype(F32)
            lo = nf - hi - mid
            base = HEAD_DIM * (1 - hh)
            aug = jnp.where((lanes >= base) & (lanes < base + 3), 1.0, 0.0)
            for j, piece in enumerate((hi, mid, lo)):
                aug = jnp.where(lanes == base + 3 + j, piece, aug)
            mine = (lanes >= HEAD_DIM * hh) & (lanes < HEAD_DIM * (hh + 1))
            o_ref[hh] = jnp.where(mine, k2, aug.astype(BF16))

    return pl.pallas_call(
        body, name=name, grid=(HP, S // tr),
        in_specs=[pl.BlockSpec((tr, 128), lambda hp, i: (i, HP + hp)), pl.BlockSpec((tr, 128), lambda hp, i: (i, 0))],
        out_specs=pl.BlockSpec((None, 2, tr, 128), lambda hp, i: (hp, 0, i, 0)),
        out_shape=jax.ShapeDtypeStruct((HP, 2, S, 128), BF16), compiler_params=_cp("parallel", "parallel"))(qkv, fcol)


HALF = S5_SUB * S5_P
NCB = 2 * HALF // 128
RE, IM = slice(0, NCB // 2), slice(NCB // 2, NCB)


def _gelu(x):
    return 0.5 * x * (1.0 + jnp.tanh(GELU_K * (x + 0.044715 * x * x * x)))


def _gelu_grad(x):
    t = jnp.tanh(GELU_K * (x + 0.044715 * x * x * x))
    return 0.5 * (1.0 + t) + 0.5 * x * (1.0 - t * t) * GELU_K * (1.0 + 3.0 * 0.044715 * x * x)


def _s5_put(buf, s, val, lt):
    for cb in range(NCB):
        buf[cb, pl.ds(s, lt, stride=S5_SUB), :] = val[:, 128 * cb:128 * (cb + 1)]


def _s5_get(buf, s, lt):
    return jnp.concatenate([buf[cb, pl.ds(s, lt, stride=S5_SUB), :] for cb in range(NCB)], axis=1)


def _s5_project_in(u_ref, wb_ref, buf, lt):
    for s in range(S5_SUB):
        _s5_put(buf, s, jnp.dot(u_ref[:, 128 * s:128 * (s + 1)], wb_ref[s], preferred_element_type=F32), lt)


def _s5_scan(buf, lam_ref, h0, lt):
    a_re, a_im = lam_ref[RE], lam_ref[IM]

    def step(t, carry):
        hr, hi = carry
        r0 = pl.multiple_of(t * S5_SUB, S5_SUB)
        nr = a_re * hr - a_im * hi + buf[RE, pl.ds(r0, S5_SUB), :]
        ni = a_re * hi + a_im * hr + buf[IM, pl.ds(r0, S5_SUB), :]
        buf[RE, pl.ds(r0, S5_SUB), :] = nr
        buf[IM, pl.ds(r0, S5_SUB), :] = ni
        return nr, ni

    return lax.fori_loop(0, lt, step, (h0[RE], h0[IM]), unroll=4)


def _s5_fwd(u, wb, wc, lam, dskip, *, lt, name, comm=()):
    S, D = u.shape
    nt = S // lt

    def body(u_ref, wb_ref, wc_ref, lam_ref, d_ref, yp_ref, yg_ref, st_ref, buf, h_sc):
        @pl.when(pl.program_id(0) == 0)
        def _():
            h_sc[...] = jnp.zeros_like(h_sc)

        st_ref[...] = h_sc[...]
        _s5_project_in(u_ref, wb_ref, buf, lt)
        hr, hi = _s5_scan(buf, lam_ref, h_sc[...], lt)
        h_sc[RE] = hr
        h_sc[IM] = hi
        for s in range(S5_SUB):
            cols = slice(128 * s, 128 * (s + 1))
            hs = _s5_get(buf, s, lt).astype(BF16)
            yp = jnp.dot(hs, wc_ref[s], preferred_element_type=F32) + d_ref[:, cols] * u_ref[:, cols].astype(F32)
            yp_ref[:, cols] = yp.astype(yp_ref.dtype)
            yg_ref[:, cols] = _gelu(yp).astype(yg_ref.dtype)

    blk = pl.BlockSpec((lt, D), lambda i: (i, 0))
    full = lambda shp: pl.BlockSpec(shp, lambda i: (0,) * len(shp))
    state = (NCB, S5_SUB, 128)
    return _call(
        body, name=name, grid=(nt,), comm=comm, sem=("arbitrary",), args=(u, wb, wc, lam, dskip),
        in_specs=[blk, full(wb.shape), full(wc.shape), full(lam.shape), full(dskip.shape)],
        out_specs=[blk, blk, pl.BlockSpec((None,) + state, lambda i: (i, 0, 0, 0))],
        out_shape=[jax.ShapeDtypeStruct((S, D), BF16), jax.ShapeDtypeStruct((S, D), BF16), jax.ShapeDtypeStruct((nt,) + state, F32)],
        scratch_shapes=[pltpu.VMEM((NCB, lt * S5_SUB, 128), F32), pltpu.VMEM(state, F32)])


def _s5_bwd(u, dyp, st, wb, wc, lam, dskip, *, lt, name, comm=()):
    S, D = u.shape
    nt = S // lt

    def body(u_ref, dy_ref, st_ref, wb_ref, wc_ref, lam_ref, d_ref, du_ref, dwb_ref, dwc_ref, dlam_ref, hbuf, gbuf, g_sc):
        @pl.when(pl.program_id(0) == 0)
        def _():
            g_sc[...] = jnp.zeros_like(g_sc)
            dwb_ref[...] = jnp.zeros_like(dwb_ref)
            dwc_ref[...] = jnp.zeros_like(dwc_ref)
            dlam_ref[...] = jnp.zeros_like(dlam_ref)

        _s5_project_in(u_ref, wb_ref, hbuf, lt)
        _s5_scan(hbuf, lam_ref, st_ref[...], lt)
        for s in range(S5_SUB):
            dys = dy_ref[:, 128 * s:128 * (s + 1)]
            _s5_put(gbuf, s, lax.dot_general(dys, wc_ref[s], NT_DIMS, preferred_element_type=F32), lt)
        a_re, a_im = lam_ref[RE], lam_ref[IM]

        def one(t, carry, hp_re, hp_im):
            gr, gi, dar, dai = carry
            r0 = pl.multiple_of(t * S5_SUB, S5_SUB)
            nr = gbuf[RE, pl.ds(r0, S5_SUB), :] + a_re * gr + a_im * gi
            ni = gbuf[IM, pl.ds(r0, S5_SUB), :] + a_re * gi - a_im * gr
            gbuf[RE, pl.ds(r0, S5_SUB), :] = nr
            gbuf[IM, pl.ds(r0, S5_SUB), :] = ni
            return nr, ni, dar + nr * hp_re + ni * hp_im, dai + ni * hp_re - nr * hp_im

        def step(k, carry):
            t = lt - 1 - k
            p0 = pl.multiple_of((t - 1) * S5_SUB, S5_SUB)
            return one(t, carry, hbuf[RE, pl.ds(p0, S5_SUB), :], hbuf[IM, pl.ds(p0, S5_SUB), :])

        init = (g_sc[RE], g_sc[IM], dlam_ref[RE], dlam_ref[IM])
        carry = lax.fori_loop(0, lt - 1, step, init, unroll=2)
        gr, gi, dar, dai = one(0, carry, st_ref[RE], st_ref[IM])
        g_sc[RE] = gr
        g_sc[IM] = gi
        dlam_ref[RE] = dar
        dlam_ref[IM] = dai
        for s in range(S5_SUB):
            cols = slice(128 * s, 128 * (s + 1))
            gs = _s5_get(gbuf, s, lt).astype(BF16)
            hs = _s5_get(hbuf, s, lt).astype(BF16)
            us, dys = u_ref[:, cols], dy_ref[:, cols]
            du = lax.dot_general(gs, wb_ref[s], NT_DIMS, preferred_element_type=F32) + d_ref[:, cols] * dys.astype(F32)
            du_ref[:, cols] = du.astype(du_ref.dtype)
            dwb_ref[s] += lax.dot_general(us, gs, TN_DIMS, preferred_element_type=F32)
            dwc_ref[s] += lax.dot_general(hs, dys, TN_DIMS, preferred_element_type=F32)

    blk = pl.BlockSpec((lt, D), lambda i: (nt - 1 - i, 0))
    full = lambda shp: pl.BlockSpec(shp, lambda i: (0,) * len(shp))
    state = (NCB, S5_SUB, 128)
    words = pltpu.VMEM((NCB, lt * S5_SUB, 128), F32)
    return _call(
        body, name=name, grid=(nt,), comm=comm, sem=("arbitrary",), args=(u, dyp, st, wb, wc, lam, dskip),
        in_specs=[blk, blk, pl.BlockSpec((None,) + state, lambda i: (nt - 1 - i, 0, 0, 0)),
                  full(wb.shape), full(wc.shape), full(lam.shape), full(dskip.shape)],
        out_specs=[blk, full(wb.shape), full(wc.shape), full(lam.shape)],
        out_shape=[jax.ShapeDtypeStruct((S, D), BF16), jax.ShapeDtypeStruct(wb.shape, F32), jax.ShapeDtypeStruct(wc.shape, F32),
                   jax.ShapeDtypeStruct(lam.shape, F32)],
        scratch_shapes=[words, words, pltpu.VMEM(state, F32)])


def _s5_disc(lam_re, lam_im, log_dt, b_re, b_im):
    dt = jnp.exp(log_dt)[:, None]
    mag = jnp.exp(lam_re * dt)
    lb_re, lb_im = mag * jnp.cos(lam_im * dt), mag * jnp.sin(lam_im * dt)
    num_re = lb_re - 1.0
    den = lam_re * lam_re + lam_im * lam_im
    k_re = (num_re * lam_re + lb_im * lam_im) / den
    k_im = (lb_im * lam_re - num_re * lam_im) / den
    bb_re = k_re[..., None] * b_re - k_im[..., None] * b_im
    bb_im = k_re[..., None] * b_im + k_im[..., None] * b_re
    return lb_re, lb_im, bb_re, bb_im


def _s5_pack(lb_re, lb_im, bb_re, bb_im, c_re, c_im):
    eye = jnp.eye(S5_SUB, dtype=F32)
    tb = lambda x: jnp.einsum("sgcp,gh->sgchp", x.reshape(S5_SUB, S5_SUB, S5_P, S5_C).transpose(0, 1, 3, 2), eye).reshape(S5_SUB, 128, HALF)
    tc = lambda x: jnp.einsum("sgpc,gh->sgphc", x.reshape(S5_SUB, S5_SUB, S5_C, S5_P).transpose(0, 1, 3, 2), eye).reshape(S5_SUB, HALF, 128)
    wb = jnp.concatenate([tb(bb_re), tb(bb_im)], axis=2).astype(BF16)
    wc = jnp.concatenate([tc(c_re), -tc(c_im)], axis=1).astype(BF16)
    lam = jnp.concatenate([lb_re.reshape(S5_SUB, HALF), lb_im.reshape(S5_SUB, HALF)], axis=1)
    return wb, wc, lam.reshape(S5_SUB, NCB, 128).transpose(1, 0, 2)


def _s5_unpack(dwb, dwc, dlam):
    db = jnp.einsum("sgcrgp->rsgpc", dwb.reshape(S5_SUB, S5_SUB, S5_C, 2, S5_SUB, S5_P)).reshape(2, S5_G, S5_P, S5_C)
    dc = jnp.einsum("srgpgc->rsgcp", dwc.reshape(S5_SUB, 2, S5_SUB, S5_P, S5_SUB, S5_C)).reshape(2, S5_G, S5_C, S5_P)
    dlam = dlam.transpose(1, 0, 2).reshape(S5_SUB, 2 * HALF)
    return (dlam[:, :HALF].reshape(S5_G, S5_P), dlam[:, HALF:].reshape(S5_G, S5_P), db[0], db[1], dc[0], -dc[1])


def _glu_fwd(yg, gl, *, name):
    S, D = yg.shape
    tm = _tile(S, 512, 8)
    blk = pl.BlockSpec((tm, D), lambda i: (i, 0))

    def body(y_ref, g_ref, o_ref):
        o_ref[...] = (y_ref[...].astype(F32) * jax.nn.sigmoid(g_ref[...].astype(F32))).astype(o_ref.dtype)

    return pl.pallas_call(body, name=name, grid=(S // tm,), in_specs=[blk, blk], out_specs=blk,
                          out_shape=jax.ShapeDtypeStruct((S, D), BF16), compiler_params=_cp("parallel"))(yg, gl)


def _glu_bwd(dy2, yg, gl, *, name):
    S, D = yg.shape
    tm = _tile(S, 512, 8)
    blk = pl.BlockSpec((tm, D), lambda i: (i, 0))

    def body(d_ref, y_ref, g_ref, da_ref, dg_ref):
        d = d_ref[...].astype(F32)
        sg = jax.nn.sigmoid(g_ref[...].astype(F32))
        da_ref[...] = d * sg
        dg_ref[...] = (d * y_ref[...].astype(F32) * sg * (1.0 - sg)).astype(dg_ref.dtype)

    return pl.pallas_call(body, name=name, grid=(S // tm,), in_specs=[blk, blk, blk], out_specs=[blk, blk],
                          out_shape=[jax.ShapeDtypeStruct((S, D), F32), jax.ShapeDtypeStruct((S, D), BF16)],
                          compiler_params=_cp("parallel"))(dy2, yg, gl)


def _gelu_bwd(da, db, ypre, u, *, name):
    S, D = ypre.shape
    tm = _tile(S, 512, 8)
    blk = pl.BlockSpec((tm, D), lambda i: (i, 0))
    vec = pl.BlockSpec((1, D), lambda i: (0, 0))

    def body(a_ref, b_ref, y_ref, u_ref, o_ref, s_ref):
        @pl.when(pl.program_id(0) == 0)
        def _():
            s_ref[...] = jnp.zeros_like(s_ref)

        dy = (a_ref[...] + b_ref[...]) * _gelu_grad(y_ref[...].astype(F32))
        o_ref[...] = dy.astype(o_ref.dtype)
        s_ref[...] += jnp.sum(dy * u_ref[...].astype(F32), axis=0, keepdims=True)

    return pl.pallas_call(body, name=name, grid=(S // tm,), in_specs=[blk, blk, blk, blk], out_specs=[blk, vec],
                          out_shape=[jax.ShapeDtypeStruct((S, D), BF16), jax.ShapeDtypeStruct((1, D), F32)],
                          compiler_params=_cp("arbitrary"))(da, db, ypre, u)


def _my_index():
    return 4 * lax.axis_index("x") + 2 * lax.axis_index("y") + lax.axis_index("c")


def _comm_plan(payloads):
    n = len(payloads)
    hbm = pl.BlockSpec(memory_space=pl.ANY)
    shapes = [jax.ShapeDtypeStruct((N_DEV,) + (x.shape if g else x.shape[1:]), x.dtype) for x, g in payloads]
    sems = [pltpu.SemaphoreType.DMA((n, N_DEV - 1)), pltpu.SemaphoreType.DMA((n, N_DEV - 1)), pltpu.SemaphoreType.DMA((n,))] if n else []
    return [hbm] * n, [hbm] * n, shapes, sems


def _comm_copies(x_refs, o_refs, gathers, send_sems, recv_sems, local_sems):
    mx, my, mc = lax.axis_index("x"), lax.axis_index("y"), lax.axis_index("c")
    me = 4 * mx + 2 * my + mc
    copies = []
    for i, (x_ref, o_ref, gather) in enumerate(zip(x_refs, o_refs, gathers)):
        src = (lambda j, r=x_ref: r) if gather else (lambda j, r=x_ref: r.at[j])
        copies.append(pltpu.make_async_copy(src(me), o_ref.at[me], local_sems.at[i]))
        for k in range(1, N_DEV):
            px, py, pc = mx ^ (k >> 2), my ^ ((k >> 1) & 1), mc ^ (k & 1)
            copies.append(pltpu.make_async_remote_copy(src(4 * px + 2 * py + pc), o_ref.at[me], send_sems.at[i, k - 1],
                                                       recv_sems.at[i, k - 1], device_id=(px, py, pc), device_id_type=MESH_ID))
    return copies


def _call(body, *, name, grid, in_specs, out_specs, out_shape, scratch_shapes, sem, args, comm=()):
    if not comm:
        outs = pl.pallas_call(body, name=name, grid=grid, in_specs=in_specs, out_specs=out_specs, out_shape=out_shape,
                              scratch_shapes=scratch_shapes, compiler_params=_cp(*sem))(*args)
        return outs, ()
    n, n_in, n_out = len(comm), len(in_specs), len(out_specs)
    c_in, c_out, c_shapes, c_sems = _comm_plan(comm)
    gathers = [g for _, g in comm]

    def wrapped(*refs):
        own_in, x_refs = refs[:n_in], refs[n_in:n_in + n]
        own_out, o_refs = refs[n_in + n:n_in + n + n_out], refs[n_in + n + n_out:n_in + 2 * n + n_out]
        own_scratch, sems = refs[n_in + 2 * n + n_out:-3], refs[-3:]
        ids = [pl.program_id(a) for a in range(len(grid))]
        first = functools.reduce(jnp.logical_and, [i == 0 for i in ids])
        last = functools.reduce(jnp.logical_and, [i == g - 1 for i, g in zip(ids, grid)])

        @pl.when(first)
        def _():
            for cp in _comm_copies(x_refs, o_refs, gathers, *sems):
                cp.start()

        body(*own_in, *own_out, *own_scratch)

        @pl.when(last)
        def _():
            for cp in _comm_copies(x_refs, o_refs, gathers, *sems):
                cp.wait()

    outs = pl.pallas_call(wrapped, name=name, grid=grid, in_specs=list(in_specs) + c_in, out_specs=list(out_specs) + c_out,
                          out_shape=list(out_shape) + c_shapes, scratch_shapes=list(scratch_shapes) + c_sems,
                          compiler_params=_cp(*["arbitrary"] * len(grid)))(*args, *[x for x, _ in comm])
    return outs[:n_out], outs[n_out:]


def _exchange(x, *, gather, name):
    in_specs, out_specs, shapes, sems = _comm_plan([(x, gather)])

    def body(x_ref, o_ref, send_sems, recv_sems, local_sems):
        copies = _comm_copies([x_ref], [o_ref], [gather], send_sems, recv_sems, local_sems)
        for cp in copies:
            cp.start()
        for cp in copies:
            cp.wait()

    return pl.pallas_call(body, name=name, in_specs=in_specs, out_specs=out_specs[0], out_shape=shapes[0], scratch_shapes=sems,
                          compiler_params=pltpu.CompilerParams(has_side_effects=True))(x)


def _sum8(x8, *, name):
    _, R, C = x8.shape
    tr = _tile(R, 256, 16)

    def body(x_ref, o_ref):
        acc = x_ref[0].astype(F32)
        for j in range(1, N_DEV):
            acc = acc + x_ref[j].astype(F32)
        o_ref[...] = acc

    return pl.pallas_call(body, name=name, grid=(R // tr,), in_specs=[pl.BlockSpec((N_DEV, tr, C), lambda i: (0, i, 0))],
                          out_specs=pl.BlockSpec((tr, C), lambda i: (i, 0)), out_shape=jax.ShapeDtypeStruct((R, C), F32),
                          compiler_params=_cp("parallel"))(x8)


def _pack(arrs, dtype, row_mult):
    flat = jnp.concatenate([a.reshape(-1).astype(dtype) for a in arrs])
    n = flat.shape[0]
    rows = -(-n // 1024)
    rows = -(-rows // row_mult) * row_mult
    return jnp.pad(flat, (0, rows * 1024 - n)).reshape(rows, 1024)


def _unpack(buf, shapes):
    lead = buf.shape[:-2]
    flat = buf.reshape(lead + (-1,))
    out, off = [], 0
    for s in shapes:
        n = math.prod(s)
        out.append(flat[..., off:off + n].reshape(lead + tuple(s)))
        off += n
    return out


def _adaln_fwd(c_all, w, b, *, name):
    nk, D, n = w.shape

    def body(c_ref, w_ref, b_ref, o_ref):
        cv = c_ref[...]
        sc = (cv * jax.nn.sigmoid(cv)).astype(BF16)
        o_ref[...] = jnp.dot(sc, w_ref[...].astype(BF16), preferred_element_type=F32) + b_ref[...]

    return pl.pallas_call(body, name=name, grid=(nk,),
                          in_specs=[pl.BlockSpec((N_DEV, D), lambda k: (0, 0)), pl.BlockSpec((None, D, n), lambda k: (k, 0, 0)),
                                    pl.BlockSpec((None, 1, n), lambda k: (k, 0, 0))],
                          out_specs=pl.BlockSpec((None, N_DEV, n), lambda k: (k, 0, 0)),
                          out_shape=jax.ShapeDtypeStruct((nk, N_DEV, n), F32), compiler_params=_cp("parallel"))(c_all, w, b)


def _adaln_bwd(c_all, dmod, *, name):
    nk, _, n = dmod.shape
    D = c_all.shape[1]

    def body(c_ref, d_ref, dw_ref, db_ref):
        cv = c_ref[...]
        sc = (cv * jax.nn.sigmoid(cv)).astype(BF16).astype(F32)
        dm = d_ref[...]
        dw_ref[...] = lax.dot_general(sc, dm.astype(BF16).astype(F32), TN_DIMS, precision=lax.Precision.HIGHEST,
                                      preferred_element_type=F32)
        db_ref[...] = jnp.sum(dm, axis=0, keepdims=True)

    return pl.pallas_call(body, name=name, grid=(nk,),
                          in_specs=[pl.BlockSpec((N_DEV, D), lambda k: (0, 0)), pl.BlockSpec((None, N_DEV, n), lambda k: (k, 0, 0))],
                          out_specs=[pl.BlockSpec((None, D, n), lambda k: (k, 0, 0)), pl.BlockSpec((None, 1, n), lambda k: (k, 0, 0))],
                          out_shape=[jax.ShapeDtypeStruct((nk, D, n), F32), jax.ShapeDtypeStruct((nk, 1, n), F32)],
                          compiler_params=_cp("parallel"))(c_all, dmod)


def _adamw(w, g, m, v, *, name):
    R, C = w.shape
    tr = _tile(R, 256, 8)
    blk = pl.BlockSpec((tr, C), lambda i: (i, 0))
    c1 = 1.0 / (1.0 - ADAM_B1 ** ADAM_STEP)
    c2 = 1.0 / (1.0 - ADAM_B2 ** ADAM_STEP)

    def body(w_ref, g_ref, m_ref, v_ref, d_ref, nm_ref, nv_ref):
        gv = g_ref[...]
        nm = ADAM_B1 * m_ref[...] + (1.0 - ADAM_B1) * gv
        nv = ADAM_B2 * v_ref[...] + (1.0 - ADAM_B2) * (gv * gv)
        nm_ref[...] = nm
        nv_ref[...] = nv
        d_ref[...] = -ADAM_LR * ((nm * c1) / (jnp.sqrt(nv * c2) + ADAM_EPS) + ADAM_WD * w_ref[...])

    sh = jax.ShapeDtypeStruct((R, C), F32)
    return pl.pallas_call(body, name=name, grid=(R // tr,), in_specs=[blk] * 4, out_specs=[blk] * 3, out_shape=[sh] * 3,
                          compiler_params=_cp("parallel"))(w, g, m, v)


def _adamw_nd(w, g, m, v, *, name):
    shp = w.shape
    two = (-1, shp[-1]) if w.ndim > 1 else (1, -1)
    outs = _adamw(w.reshape(two), g.reshape(two), m.reshape(two), v.reshape(two), name=name)
    return [o.reshape(shp) for o in outs]


S5_TILE = 256
ATTN_BLOCK = 512
BIG = ("s5_w_in", "s5_w_glu", "s5_w_out", "fox_w_in", "fox_w_out", "ffn_w_up", "ffn_w_down")
WEIGHTS = ("norm_g", "ada_w", "ada_b", "s5_w_in", "s5_lam_re", "s5_lam_im", "s5_log_dt", "s5_b_re", "s5_b_im", "s5_c_re",
           "s5_c_im", "s5_d", "s5_w_glu", "s5_w_out", "fox_w_in", "fox_b_f", "fox_w_out", "ffn_w_up", "ffn_conv_w",
           "ffn_conv_b", "ffn_w_down", "final_g")
REPLICATED = ("s5_lam_re", "s5_lam_im", "s5_log_dt", "s5_b_re", "s5_b_im", "s5_c_re", "s5_c_im", "s5_d", "fox_b_f",
              "ffn_conv_b", "final_g")


S5_MATS = ("s5_w_in", "s5_w_glu", "s5_w_out")


def _row_group(arrs):
    return jnp.concatenate([a.reshape(-1, D_MODEL) for a in arrs], axis=0)


def _cols_join(g):
    return jnp.concatenate([g[j] for j in range(N_DEV)], axis=1)


def _cols_split(full):
    n = full.shape[1] // N_DEV
    return jnp.stack([full[:, j * n:(j + 1) * n] for j in range(N_DEV)])


def kernel(x, c, norm_g, ada_w, ada_b, s5_w_in, s5_lam_re, s5_lam_im, s5_log_dt, s5_b_re, s5_b_im, s5_c_re, s5_c_im, s5_d, s5_w_glu, s5_w_out, fox_w_in, fox_b_f, fox_w_out, ffn_w_up, ffn_conv_w, ffn_conv_b, ffn_w_down, final_g, loss_target, m_norm_g, m_ada_w, m_ada_b, m_s5_w_in, m_s5_lam_re, m_s5_lam_im, m_s5_log_dt, m_s5_b_re, m_s5_b_im, m_s5_c_re, m_s5_c_im, m_s5_d, m_s5_w_glu, m_s5_w_out, m_fox_w_in, m_fox_b_f, m_fox_w_out, m_ffn_w_up, m_ffn_conv_w, m_ffn_conv_b, m_ffn_w_down, m_final_g, v_norm_g, v_ada_w, v_ada_b, v_s5_w_in, v_s5_lam_re, v_s5_lam_im, v_s5_log_dt, v_s5_b_re, v_s5_b_im, v_s5_c_re, v_s5_c_im, v_s5_d, v_s5_w_glu, v_s5_w_out, v_fox_w_in, v_fox_b_f, v_fox_w_out, v_ffn_w_up, v_ffn_conv_w, v_ffn_conv_b, v_ffn_w_down, v_final_g):
    args = dict(locals())
    W = {n: args[n] for n in WEIGHTS}
    M = {n: args["m_" + n] for n in WEIGHTS}
    V = {n: args["v_" + n] for n in WEIGHTS}
    D, F = D_MODEL, D_FF
    me = _my_index()
    h0 = x[0]
    S = h0.shape[0]
    lt = min(S5_TILE, S)
    tb = min(ATTN_BLOCK, S)
    n_g, n_cw = norm_g.size, ffn_conv_w.size

    g0 = _exchange(_pack([c, norm_g, ffn_conv_w], F32, 8), gather=True, name="gather_small_in")
    c_all, ng_all, cw_all = _unpack(g0, [(D,), norm_g.shape, ffn_conv_w.shape])
    ng_full = ng_all.transpose(1, 2, 0, 3).reshape(2, 2, D)
    cw_full = cw_all.transpose(1, 2, 0, 3).reshape(2, 3, F)

    ncol = ada_w.shape[-1]
    modp = _adaln_fwd(c_all, ada_w.reshape(4, D, ncol), ada_b.reshape(4, 1, ncol), name="adaln_fwd")
    g1 = _exchange(_pack([modp], F32, 8), gather=True, name="gather_adaln")
    (mod_all,) = _unpack(g1, [modp.shape])
    mod = lax.dynamic_index_in_dim(mod_all, me, axis=2, keepdims=False).transpose(1, 0, 2).reshape(4, 3 * D)
    shift = [mod[k:k + 1, :D] for k in range(4)]
    scale = [mod[k:k + 1, D:2 * D] for k in range(4)]
    gate = [mod[k:k + 1, 2 * D:] for k in range(4)]
    gain = [_row(ng_full[k // 2, k % 2]) for k in range(4)]

    b16 = lambda a: a.astype(BF16)
    rows = D // N_DEV
    g_s5 = _exchange(_row_group([b16(W[n]) for n in S5_MATS]), gather=True, name="gather_s5_weights")
    full = {n: g_s5[:, i * rows:(i + 1) * rows].reshape(D, D) for i, n in enumerate(S5_MATS)}
    bf_pad = jnp.pad(fox_b_f, ((0, 0), (0, 128 - HEADS)))

    def ffn_fwd(h, k, layer):
        hn = _modulate(h, gain[k], shift[k], scale[k], name=f"modulate{k}")
        up = _mm(hn, full[f"ffn_w_up{layer}"], name=f"ffn_up{layer}", o_h=True, out_dtype=BF16, tn=1408)
        z = _conv_gate_fwd(up, cw_full[layer], ffn_conv_b[layer:layer + 1], name=f"conv_gate{layer}")
        m, h_out = _mm(z, full[f"ffn_w_down{layer}"], name=f"ffn_down{layer}", res=h, gate=gate[k])
        return h_out, (hn, up, z, m)

    lb_re, lb_im, bb_re, bb_im = _s5_disc(s5_lam_re[0], s5_lam_im[0], s5_log_dt[0], s5_b_re[0], s5_b_im[0])
    wb, wc, lam = _s5_pack(lb_re, lb_im, bb_re, bb_im, s5_c_re[0], s5_c_im[0])
    hn0 = _modulate(h0, gain[0], shift[0], scale[0], name="modulate0")
    u = _mm(hn0, full["s5_w_in"], name="s5_in", out_dtype=BF16)
    (ypre, yg, st), (g_up0, g_rows0, g_fox) = _s5_fwd(
        u, wb, wc, lam, s5_d, lt=lt, name="s5_scan",
        comm=[(b16(ffn_w_up[0]), True), (_row_group([b16(fox_w_out), b16(ffn_w_down[0])]), True), (b16(fox_w_in[0]), True)])
    full["ffn_w_up0"] = _cols_join(g_up0)
    full["fox_w_out"] = g_rows0[:, :rows].reshape(D, D)
    full["ffn_w_down0"] = g_rows0[:, rows:].reshape(F, D)
    full["fox_w_in"] = _cols_join(g_fox)
    w_proj = jnp.pad(full["fox_w_in"], ((0, 0), (0, 3 * D + 128 - full["fox_w_in"].shape[1])))
    w_qkv, w_f = w_proj[:, :3 * D], w_proj[:, 3 * D:]
    gl = _mm(yg, full["s5_w_glu"], name="s5_glu", out_dtype=BF16)
    y2 = _glu_fwd(yg, gl, name="s5_glu_gate")
    m0, h1 = _mm(y2, full["s5_w_out"], name="s5_out", res=h0, gate=gate[0])
    h2, (hn1, up0, z0, m1) = ffn_fwd(h1, 1, 0)

    hn2 = _modulate(h2, gain[2], shift[2], scale[2], name="modulate2")
    nb = S // tb
    w_qkv_s = jnp.concatenate([w_qkv[:, :D] * Q_SCALE, w_qkv[:, D:]], axis=1)
    qkv = _mm(hn2, w_qkv_s, name="fox_qkv", out_dtype=BF16)
    qkvT = _mm(w_qkv_s, hn2, name="fox_qkv_t", ta=True, tb=True, out_dtype=BF16)
    fl = _mm(hn2, w_f, name="fox_f", out_dtype=F32)
    fcol = _fgate_fwd(fl, bf_pad, name="fox_fgate")
    f_heads = fcol[:, :HEADS].T.reshape(HP, 2, S)
    ka = _fox_keys(qkv, fcol, name="fox_keys")
    vtb = _blocked(qkvT[2 * D:].reshape(HP, 128, S), nb, tb)
    aug = _aug_rows(f_heads)
    (oT, lse), (g_up1, g_down1) = _foxt_fwd(qkvT, ka, vtb, aug, tb=tb, name="fox_attn",
                                            comm=[(b16(ffn_w_up[1]), True), (b16(ffn_w_down[1]), True)])
    full["ffn_w_up1"] = _cols_join(g_up1)
    full["ffn_w_down1"] = g_down1.reshape(F, D)
    m2, h3 = _mm(oT, full["fox_w_out"], name="fox_out", ta=True, res=h2, gate=gate[2])
    h4, (hn3, up1, z1, m3) = ffn_fwd(h3, 3, 1)

    lblk, dh, d_final_g = _loss_head(h4, _row(final_g), loss_target[0], name="loss_head")
    loss = lax.psum(lblk[0, 0], ("x", "y", "c"))

    G = {}
    dmod = [None] * 4

    def norm_bwd(h, dhn, dh_in, k):
        dh_out, sums = _modulate_bwd(h, dhn, dh_in, gain[k], scale[k], name=f"modulate_bwd{k}")
        return dh_out, sums[0:1], sums[1:2], sums[2]

    def ffn_bwd(dh_in, h, k, layer, saved):
        hn, up, z, m = saved
        dm, dgate = _gate_bwd(dh_in, m, gate[k], name=f"gate_bwd{k}")
        dz = _mm(dm, full[f"ffn_w_down{layer}"], name=f"ffn_down_dx{layer}", tb=True, out_dtype=BF16, tn=1408)
        dw_down = _mm(z, dm, name=f"ffn_down_dw{layer}", ta=True, tm=1408, tn=1024, tk=2048)
        d_up, cs = _conv_gate_bwd(up, dz, cw_full[layer], ffn_conv_b[layer:layer + 1], name=f"conv_gate_bwd{layer}")
        dhn = _mm(d_up, full[f"ffn_w_up{layer}"], name=f"ffn_up_dx{layer}", tb=True, a_h=True, tk=2816)
        dw_up = _mm(hn, d_up, name=f"ffn_up_dw{layer}", ta=True, b_h=True, tm=1024, tn=1408, tk=2048)
        dh_out, dshift, dscale, dg = norm_bwd(h, dhn, dh_in, k)
        dmod[k] = jnp.concatenate([dshift, dscale, dgate], axis=1)
        return dh_out, dg, dw_up, dw_down, cs[0:3], cs[3]

    dh, dg3, dw_up1, dw_down1, dcw1, dcb1 = ffn_bwd(dh, h3, 3, 1, (hn3, up1, z1, m3))

    dm, dgate2 = _gate_bwd(dh, m2, gate[2], name="gate_bwd2")
    do = _mm(dm, full["fox_w_out"], name="fox_out_dx", tb=True, out_dtype=BF16)
    doT = _mm(full["fox_w_out"], dm, name="fox_out_dx_t", tb=True, out_dtype=BF16)
    dw_fox_out = _mm(oT, dm, name="fox_out_dw", tn=1024, tk=2048)
    to_rows = lambda g: b16(g).reshape(N_DEV, -1, D)
    delta = _fox_delta(oT, doT, tb=tb, name="fox_attn_delta")
    (dqtb, dk, dv, dfk, dfq), (r_up1, r_down1) = _foxt_bwd(
        qkvT, qkv, ka, _blocked(qkvT[:D].reshape(HP, 128, S), nb, tb), _blocked(doT.reshape(HP, 128, S), nb, tb), do,
        _blocked(aug, nb, tb), _blocked(lse, nb, tb), _blocked(delta, nb, tb), tb=tb, name="fox_attn_bwd",
        comm=[(_cols_split(b16(dw_up1)), False), (to_rows(dw_down1), False)])
    dq = dqtb.transpose(1, 3, 0, 2).reshape(S, D)
    dF = dfk[:, :, :2].transpose(1, 0, 2).reshape(S, HEADS) + dfq[:, :, :2, :].transpose(1, 3, 0, 2).reshape(S, HEADS)
    dF = jnp.pad(dF, ((0, 0), (0, 128 - HEADS)))
    dfl, dbf = _fgate_bwd(dF, fl, bf_pad, name="fox_fgate_bwd")
    dproj = jnp.concatenate([dq, dk, dv, dfl.astype(BF16)], axis=1)
    dhn = _mm(dproj, w_proj, name="fox_in_dx", tb=True)
    dw_proj = _mm(hn2, dproj, name="fox_in_dw", ta=True, tn=640, tk=2048)
    dw_fox_in = dw_proj[:, :full["fox_w_in"].shape[1]]
    dh, dshift, dscale, dg2 = norm_bwd(h2, dhn, dh, 2)
    dmod[2] = jnp.concatenate([dshift, dscale, dgate2], axis=1)

    dh, dg1, dw_up0, dw_down0, dcw0, dcb0 = ffn_bwd(dh, h1, 1, 0, (hn1, up0, z0, m1))

    dm, dgate0 = _gate_bwd(dh, m0, gate[0], name="gate_bwd0")
    dy2 = _mm(dm, full["s5_w_out"], name="s5_out_dx", tb=True, out_dtype=BF16)
    G["s5_w_out"] = _mm(y2, dm, name="s5_out_dw", ta=True, tn=1024, tk=2048)
    da, dgl = _glu_bwd(dy2, yg, gl, name="s5_glu_bwd")
    dyg_b = _mm(dgl, full["s5_w_glu"], name="s5_glu_dx", tb=True)
    G["s5_w_glu"] = _mm(yg, dgl, name="s5_glu_dw", ta=True, tn=1024, tk=2048)
    dyp, dd = _gelu_bwd(da, dyg_b, ypre, u, name="s5_gelu_bwd")
    (du, dwb, dwc, dlam), (r_up0, r_rows0, r_fox) = _s5_bwd(
        u, dyp, st, wb, wc, lam, s5_d, lt=lt, name="s5_scan_bwd",
        comm=[(_cols_split(b16(dw_up0)), False), (jnp.concatenate([to_rows(dw_fox_out), to_rows(dw_down0)], axis=1), False),
              (_cols_split(b16(dw_fox_in)), False)])
    dhn = _mm(du, full["s5_w_in"], name="s5_in_dx", tb=True)
    G["s5_w_in"] = _mm(hn0, du, name="s5_in_dw", ta=True, tn=1024, tk=2048)
    dh, dshift, dscale, dg0 = norm_bwd(h0, dhn, dh, 0)
    dmod[0] = jnp.concatenate([dshift, dscale, dgate0], axis=1)
    grad_x = dh[None]

    dlb_re, dlb_im, dbb_re, dbb_im, dc_re, dc_im = _s5_unpack(dwb, dwc, dlam)
    _, disc_vjp = jax.vjp(_s5_disc, s5_lam_re[0], s5_lam_im[0], s5_log_dt[0], s5_b_re[0], s5_b_im[0])
    dlam_re, dlam_im, dlog_dt, db_re, db_im = disc_vjp((dlb_re, dlb_im, dbb_re, dbb_im))

    g2 = _exchange(_pack([jnp.concatenate(dmod, axis=0)], F32, 8), gather=True, name="gather_dmod")
    (dmod_all,) = _unpack(g2, [(4, 3 * D)])
    dmod_mine = _columns_of_mod(dmod_all, me, ncol)
    d_ada_w, d_ada_b = _adaln_bwd(c_all, dmod_mine, name="adaln_bwd")
    G["ada_w"] = d_ada_w.reshape(ada_w.shape)
    G["ada_b"] = d_ada_b.reshape(ada_b.shape)

    r_s5 = _exchange(jnp.concatenate([to_rows(G[n]) for n in S5_MATS], axis=1), gather=False, name="scatter_s5_grads")
    s5_sum = _sum8(r_s5, name="sum_s5_grads")
    for i, n in enumerate(S5_MATS):
        G[n] = s5_sum[i * rows:(i + 1) * rows][None]
    rows0_sum = _sum8(r_rows0, name="sum_rows0_grads")
    G["fox_w_out"] = rows0_sum[:rows][None]
    G["ffn_w_down"] = jnp.stack([rows0_sum[rows:], _sum8(r_down1, name="sum_down1_grads")])
    G["ffn_w_up"] = jnp.stack([_sum8(r_up0, name="sum_up0_grads"), _sum8(r_up1, name="sum_up1_grads")])
    G["fox_w_in"] = _sum8(r_fox, name="sum_fox_w_in")[None]

    small_full = {
        "norm_g": jnp.stack([dg0, dg1, dg2, dg3]).reshape(2, 2, D),
        "ffn_conv_w": jnp.stack([dcw0, dcw1]),
        "s5_lam_re": dlam_re[None], "s5_lam_im": dlam_im[None], "s5_log_dt": dlog_dt[None],
        "s5_b_re": db_re[None], "s5_b_im": db_im[None], "s5_c_re": dc_re[None], "s5_c_im": dc_im[None],
        "s5_d": dd, "fox_b_f": dbf[:, :HEADS], "ffn_conv_b": jnp.stack([dcb0, dcb1]), "final_g": d_final_g[0],
    }
    names = tuple(small_full)
    g3 = _exchange(_pack([small_full[n] for n in names], F32, 8), gather=True, name="gather_small_grads")
    summed = dict(zip(names, _unpack(_sum8(g3, name="sum_small_grads"), [small_full[n].shape for n in names])))
    G["norm_g"] = lax.dynamic_slice_in_dim(summed["norm_g"], me * norm_g.shape[-1], norm_g.shape[-1], axis=2)
    G["ffn_conv_w"] = lax.dynamic_slice_in_dim(summed["ffn_conv_w"], me * ffn_conv_w.shape[-1], ffn_conv_w.shape[-1], axis=2)
    for n in REPLICATED:
        G[n] = summed[n]

    delta, new_m, new_v = {}, {}, {}
    small = tuple(n for n in WEIGHTS if n not in BIG and n != "ada_w")
    for n in WEIGHTS:
        if n not in small:
            delta[n], new_m[n], new_v[n] = _adamw_nd(W[n], G[n], M[n], V[n], name=f"adamw_{n}")
    packed = [_pack([src[n] for n in small], F32, 8) for src in (W, G, M, V)]
    for dst, buf in zip((delta, new_m, new_v), _adamw(*packed, name="adamw_small")):
        dst.update(zip(small, _unpack(buf, [W[n].shape for n in small])))

    return (loss, grad_x, *[G[n] for n in WEIGHTS], *[delta[n] for n in WEIGHTS], *[new_m[n] for n in WEIGHTS],
            *[new_v[n] for n in WEIGHTS])


def _columns_of_mod(dmod_all, me, ncol):
    flat = lax.dynamic_slice_in_dim(dmod_all, me * ncol, ncol, axis=2)
    return flat.transpose(1, 0, 2)
```

```python
import functools
import math

import jax
import jax.numpy as jnp
from jax import lax
from jax.experimental import pallas as pl
from jax.experimental.pallas import tpu as pltpu

F32, BF16 = jnp.float32, jnp.bfloat16
EPS = 1e-6
N_DEV = 8
D_MODEL = 1024
D_FF = 2816
HEADS = 16
HEAD_DIM = 64
S5_G, S5_P, S5_C = 64, 64, 16
S5_SUB = 8
V7X_VMEM_LIMIT = 56 * 1024 * 1024
NEG = -1e30
ADAM_LR, ADAM_B1, ADAM_B2, ADAM_EPS, ADAM_WD, ADAM_STEP = 1e-3, 0.9, 0.999, 1e-8, 0.01, 10
GELU_K = math.sqrt(2.0 / math.pi)
MESH_ID = pl.DeviceIdType.MESH


def _cp(*sem):
    return pltpu.CompilerParams(dimension_semantics=sem, vmem_limit_bytes=V7X_VMEM_LIMIT)


def _tile(n, target, mult=128):
    if n <= target:
        return n
    t = (target // mult) * mult
    while t >= mult:
        if n % t == 0:
            return t
        t -= mult
    return n


def _row(v):
    return v.reshape(1, -1).astype(F32)


def _mm(a, b, *, name, ta=False, tb=False, out_dtype=F32, tm=1024, tn=512, tk=None, res=None, gate=None,
        a_h=False, b_h=False, o_h=False):
    if a_h:
        M, K = a.shape[1], 2 * a.shape[2]
    elif ta:
        K, M = a.shape
    else:
        M, K = a.shape
    if b_h:
        N = 2 * b.shape[2]
    else:
        N = b.shape[0] if tb else b.shape[1]
    half_n = N // 2
    tm = _tile(M, tm, 128 if ta else 8)
    tn = _tile(half_n if (b_h or o_h) else N, tn)
    tk = K if tk is None else _tile(K // 2 if a_h else K, tk)
    nk = K // tk
    nkh, nnh = (K // 2) // tk if a_h else 1, half_n // tn
    if a_h:
        a_spec = pl.BlockSpec((None, tm, tk), lambda i, j, k: (k // nkh, i, k % nkh))
    elif ta:
        a_spec = pl.BlockSpec((tk, tm), lambda i, j, k: (k, i))
    else:
        a_spec = pl.BlockSpec((tm, tk), lambda i, j, k: (i, k))
    if b_h:
        b_spec = pl.BlockSpec((None, tk, tn), lambda i, j, k: (j // nnh, k, j % nnh))
    elif tb:
        b_spec = pl.BlockSpec((tn, tk), lambda i, j, k: (j, k))
    else:
        b_spec = pl.BlockSpec((tk, tn), lambda i, j, k: (k, j))
    if o_h:
        o_spec = pl.BlockSpec((None, tm, tn), lambda i, j, k: (j // nnh, i, j % nnh))
    else:
        o_spec = pl.BlockSpec((tm, tn), lambda i, j, k: (i, j))
    dn = (((0 if ta else 1,), (1 if tb else 0,)), ((), ()))
    fused = res is not None

    def body(*refs):
        if fused:
            a_ref, b_ref, r_ref, g_ref, m_ref, o_ref, acc_ref = refs
        else:
            a_ref, b_ref, o_ref, acc_ref = refs
        p = lax.dot_general(a_ref[...].astype(BF16), b_ref[...].astype(BF16), dn, preferred_element_type=F32)

        def finish(acc):
            if fused:
                m_ref[...] = acc.astype(m_ref.dtype)
                o_ref[...] = r_ref[...] + g_ref[...] * acc
            else:
                o_ref[...] = acc.astype(o_ref.dtype)

        if nk == 1:
            finish(p)
        else:
            k = pl.program_id(2)

            @pl.when(k == 0)
            def _():
                acc_ref[...] = p

            @pl.when(k > 0)
            def _():
                acc_ref[...] += p

            @pl.when(k == nk - 1)
            def _():
                finish(acc_ref[...])

    in_specs = [a_spec, b_spec]
    args = [a, b]
    if fused:
        in_specs += [o_spec, pl.BlockSpec((1, tn), lambda i, j, k: (0, j))]
        args += [res, gate]
        out_shape = [jax.ShapeDtypeStruct((M, N), BF16), jax.ShapeDtypeStruct((M, N), F32)]
        out_specs = [o_spec, o_spec]
    else:
        out_shape = jax.ShapeDtypeStruct((2, M, half_n) if o_h else (M, N), out_dtype)
        out_specs = o_spec
    return pl.pallas_call(
        body, name=name, grid=(M // tm, N // tn, nk), in_specs=in_specs, out_specs=out_specs, out_shape=out_shape,
        scratch_shapes=[pltpu.VMEM((tm, tn) if nk > 1 else (8, 128), F32)],
        compiler_params=_cp("parallel", "parallel", "arbitrary"),
    )(*args)


def _modulate(h, g, shift, scale, *, name):
    S, D = h.shape
    tm = _tile(S, 512, 8)
    vec = pl.BlockSpec((1, D), lambda i: (0, 0))
    blk = pl.BlockSpec((tm, D), lambda i: (i, 0))

    def body(h_ref, g_ref, sh_ref, sc_ref, o_ref):
        x = h_ref[...]
        r = lax.rsqrt(jnp.mean(x * x, axis=-1, keepdims=True) + EPS)
        o_ref[...] = ((x * r * g_ref[...]) * (1.0 + sc_ref[...]) + sh_ref[...]).astype(o_ref.dtype)

    return pl.pallas_call(body, name=name, grid=(S // tm,), in_specs=[blk, vec, vec, vec], out_specs=blk,
                          out_shape=jax.ShapeDtypeStruct((S, D), BF16), compiler_params=_cp("parallel"))(h, g, shift, scale)


def _modulate_bwd(h, dhn, dh_in, g, scale, m_prev=None, gate_prev=None, *, name):
    S, D = h.shape
    tm = _tile(S, 512, 8)
    vec = pl.BlockSpec((1, D), lambda i: (0, 0))
    blk = pl.BlockSpec((tm, D), lambda i: (i, 0))
    sums = pl.BlockSpec((8, D), lambda i: (0, 0))
    below = m_prev is not None

    def body(h_ref, dhn_ref, dhi_ref, g_ref, sc_ref, *rest):
        dh_ref, s_ref = rest[-3:-1] if below else rest[-2:]

        @pl.when(pl.program_id(0) == 0)
        def _():
            s_ref[...] = jnp.zeros_like(s_ref)

        x = h_ref[...]
        r = lax.rsqrt(jnp.mean(x * x, axis=-1, keepdims=True) + EPS)
        xhat = x * r
        gv = g_ref[...]
        dhn_v = dhn_ref[...].astype(F32)
        dn = dhn_v * (1.0 + sc_ref[...])
        s_ref[0:1, :] += jnp.sum(dhn_v, axis=0, keepdims=True)
        s_ref[1:2, :] += jnp.sum(dhn_v * (xhat * gv), axis=0, keepdims=True)
        s_ref[2:3, :] += jnp.sum(dn * xhat, axis=0, keepdims=True)
        dxh = dn * gv
        dh = dhi_ref[...] + r * (dxh - xhat * jnp.mean(dxh * xhat, axis=-1, keepdims=True))
        dh_ref[...] = dh
        if below:
            mp_ref, gp_ref, dm_ref = rest[0], rest[1], rest[-1]
            dm_ref[...] = (dh * gp_ref[...]).astype(dm_ref.dtype)
            s_ref[3:4, :] += jnp.sum(dh * mp_ref[...].astype(F32), axis=0, keepdims=True)

    extra_in, extra_args = ([blk, vec], [m_prev, gate_prev]) if below else ([], [])
    return pl.pallas_call(body, name=name, grid=(S // tm,), in_specs=[blk, blk, blk, vec, vec] + extra_in,
                          out_specs=[blk, sums] + ([blk] if below else []),
                          out_shape=[jax.ShapeDtypeStruct((S, D), F32), jax.ShapeDtypeStruct((8, D), F32)]
                          + ([jax.ShapeDtypeStruct((S, D), BF16)] if below else []),
                          compiler_params=_cp("arbitrary"))(h, dhn, dh_in, g, scale, *extra_args)


def _loss_head(h, g, tgt, m_prev, gate_prev, *, name):
    S, D = h.shape
    tm = _tile(S, 512, 8)
    vec = pl.BlockSpec((1, D), lambda i: (0, 0))
    blk = pl.BlockSpec((tm, D), lambda i: (i, 0))
    lss = pl.BlockSpec((8, 128), lambda i: (0, 0))
    sums = pl.BlockSpec((8, D), lambda i: (0, 0))

    def body(h_ref, g_ref, t_ref, mp_ref, gp_ref, l_ref, dh_ref, s_ref, dm_ref):
        @pl.when(pl.program_id(0) == 0)
        def _():
            l_ref[...] = jnp.zeros_like(l_ref)
            s_ref[...] = jnp.zeros_like(s_ref)

        x = h_ref[...]
        r = lax.rsqrt(jnp.mean(x * x, axis=-1, keepdims=True) + EPS)
        xhat = x * r
        gv = g_ref[...]
        e = xhat * gv - t_ref[...]
        l_ref[...] += 0.5 * jnp.sum(jnp.mean(e * e, axis=-1, keepdims=True))
        dy = e * (1.0 / D)
        s_ref[0:1, :] += jnp.sum(dy * xhat, axis=0, keepdims=True)
        dxh = dy * gv
        dh = r * (dxh - xhat * jnp.mean(dxh * xhat, axis=-1, keepdims=True))
        dh_ref[...] = dh
        dm_ref[...] = (dh * gp_ref[...]).astype(dm_ref.dtype)
        s_ref[1:2, :] += jnp.sum(dh * mp_ref[...].astype(F32), axis=0, keepdims=True)

    return pl.pallas_call(body, name=name, grid=(S // tm,), in_specs=[blk, vec, blk, blk, vec], out_specs=[lss, blk, sums, blk],
                          out_shape=[jax.ShapeDtypeStruct((8, 128), F32), jax.ShapeDtypeStruct((S, D), F32),
                                     jax.ShapeDtypeStruct((8, D), F32), jax.ShapeDtypeStruct((S, D), BF16)],
                          compiler_params=_cp("arbitrary"))(h, g, tgt, m_prev, gate_prev)


def _shift_down(x, k, edge):
    tm = x.shape[0]
    rows = lax.broadcasted_iota(jnp.int32, x.shape, 0)
    out = pltpu.roll(x, k, 0)
    for j in range(k):
        out = jnp.where(rows == j, edge[8 - k + j:8 - k + j + 1, :], out)
    return out


def _shift_up(x, k, edge):
    tm = x.shape[0]
    rows = lax.broadcasted_iota(jnp.int32, x.shape, 0)
    out = pltpu.roll(x, tm - k, 0)
    for j in range(k):
        out = jnp.where(rows == tm - k + j, edge[j:j + 1, :], out)
    return out


def _conv_gate_fwd(up, cw, cb, *, name):
    _, S, F = up.shape
    tf = _tile(F, 1408)
    nf = F // tf
    tm = _tile(S, 512, 8)

    def body(a_ref, b_ref, w_ref, cb_ref, z_ref, edge_ref):
        @pl.when(pl.program_id(1) == 0)
        def _():
            edge_ref[...] = jnp.zeros_like(edge_ref)

        a = a_ref[...].astype(F32)
        edge = edge_ref[...]
        w = w_ref[...]
        ac = cb_ref[...] + w[2:3, :] * a + w[1:2, :] * _shift_down(a, 1, edge) + w[0:1, :] * _shift_down(a, 2, edge)
        edge_ref[...] = a[tm - 8:tm, :]
        z_ref[...] = (ac * jax.nn.sigmoid(ac) * b_ref[...].astype(F32)).astype(z_ref.dtype)

    return pl.pallas_call(
        body, name=name, grid=(nf, S // tm),
        in_specs=[pl.BlockSpec((None, tm, tf), lambda j, i: (0, i, j)), pl.BlockSpec((None, tm, tf), lambda j, i: (1, i, j)),
                  pl.BlockSpec((3, tf), lambda j, i: (0, j)), pl.BlockSpec((1, tf), lambda j, i: (0, j))],
        out_specs=pl.BlockSpec((tm, tf), lambda j, i: (i, j)),
        out_shape=jax.ShapeDtypeStruct((S, F), BF16), scratch_shapes=[pltpu.VMEM((8, tf), F32)],
        compiler_params=_cp("parallel", "arbitrary"))(up, up, cw, cb)


def _conv_gate_bwd(up, dz, cw, cb, *, name):
    _, S, F = up.shape
    tf = _tile(F, 1408)
    nf = F // tf
    tm = _tile(S, 512, 8)
    nt = S // tm
    hb = tm // 8

    def body(a_ref, ah_ref, b_ref, dz_ref, w_ref, cb_ref, d_ref, s_ref, edge_ref):
        i = pl.program_id(1)

        @pl.when(i == 0)
        def _():
            edge_ref[...] = jnp.zeros_like(edge_ref)
            s_ref[...] = jnp.zeros_like(s_ref)

        a = a_ref[...].astype(F32)
        halo = jnp.where(i == nt - 1, 0.0, ah_ref[...].astype(F32))
        w = w_ref[...]
        a1 = _shift_down(a, 1, halo)
        a2 = _shift_down(a, 2, halo)
        ac = cb_ref[...] + w[2:3, :] * a + w[1:2, :] * a1 + w[0:1, :] * a2
        sg = jax.nn.sigmoid(ac)
        dzv = dz_ref[...].astype(F32)
        d_ref[1] = (dzv * ac * sg).astype(d_ref.dtype)
        dac = dzv * b_ref[...].astype(F32) * (sg * (1.0 + ac * (1.0 - sg)))
        s_ref[0:1, :] += jnp.sum(dac * a2, axis=0, keepdims=True)
        s_ref[1:2, :] += jnp.sum(dac * a1, axis=0, keepdims=True)
        s_ref[2:3, :] += jnp.sum(dac * a, axis=0, keepdims=True)
        s_ref[3:4, :] += jnp.sum(dac, axis=0, keepdims=True)
        edge = edge_ref[...]
        da = w[2:3, :] * dac + w[1:2, :] * _shift_up(dac, 1, edge) + w[0:1, :] * _shift_up(dac, 2, edge)
        edge_ref[...] = dac[0:8, :]
        d_ref[0] = da.astype(d_ref.dtype)

    tile = lambda hlf: pl.BlockSpec((None, tm, tf), lambda j, i: (hlf, nt - 1 - i, j))
    d_up, sums = pl.pallas_call(
        body, name=name, grid=(nf, nt),
        in_specs=[tile(0),
                  pl.BlockSpec((None, 8, tf), lambda j, i: (0, jnp.maximum((nt - 1 - i) * hb - 1, 0), j)),
                  tile(1), pl.BlockSpec((tm, tf), lambda j, i: (nt - 1 - i, j)),
                  pl.BlockSpec((3, tf), lambda j, i: (0, j)), pl.BlockSpec((1, tf), lambda j, i: (0, j))],
        out_specs=[pl.BlockSpec((2, tm, tf), lambda j, i: (0, nt - 1 - i, j)), pl.BlockSpec((8, tf), lambda j, i: (0, j))],
        out_shape=[jax.ShapeDtypeStruct((2, S, F), BF16), jax.ShapeDtypeStruct((8, F), F32)],
        scratch_shapes=[pltpu.VMEM((8, tf), F32)],
        compiler_params=_cp("parallel", "arbitrary"))(up, up, up, dz, cw, cb)
    return d_up, sums


def _log_sigmoid(x):
    return jnp.minimum(x, 0.0) - jnp.log(1.0 + jnp.exp(-jnp.abs(x)))


def _tri_ones(n, upper):
    r = lax.broadcasted_iota(jnp.int32, (n, n), 0)
    c = lax.broadcasted_iota(jnp.int32, (n, n), 1)
    return jnp.where((c >= r) if upper else (c <= r), 1.0, 0.0).astype(F32)


def _fgate_fwd(fl, bf, *, name):
    S, W = fl.shape
    tb = _tile(S, 256, 8)

    def body(fl_ref, b_ref, o_ref, carry_ref):
        @pl.when(pl.program_id(0) == 0)
        def _():
            carry_ref[...] = jnp.zeros_like(carry_ref)

        lf = _log_sigmoid(fl_ref[...] + b_ref[...])
        cs = jnp.dot(_tri_ones(tb, False), lf, precision=lax.Precision.HIGHEST, preferred_element_type=F32) + carry_ref[0:1, :]
        o_ref[...] = cs
        carry_ref[...] = jnp.broadcast_to(cs[tb - 1:tb, :], carry_ref.shape)

    blk = pl.BlockSpec((tb, W), lambda i: (i, 0))
    return pl.pallas_call(body, name=name, grid=(S // tb,), in_specs=[blk, pl.BlockSpec((1, W), lambda i: (0, 0))], out_specs=blk,
                          out_shape=jax.ShapeDtypeStruct((S, W), F32), scratch_shapes=[pltpu.VMEM((8, W), F32)],
                          compiler_params=_cp("arbitrary"))(fl, bf)


def _fgate_bwd(dF, fl, bf, *, name):
    S, W = fl.shape
    tb = _tile(S, 256, 8)
    nb = S // tb

    def body(d_ref, fl_ref, b_ref, o_ref, s_ref, carry_ref):
        @pl.when(pl.program_id(0) == 0)
        def _():
            carry_ref[...] = jnp.zeros_like(carry_ref)
            s_ref[...] = jnp.zeros_like(s_ref)

        rc = jnp.dot(_tri_ones(tb, True), d_ref[...], precision=lax.Precision.HIGHEST, preferred_element_type=F32) + carry_ref[0:1, :]
        carry_ref[...] = jnp.broadcast_to(rc[0:1, :], carry_ref.shape)
        dfl = rc * jax.nn.sigmoid(-(fl_ref[...] + b_ref[...]))
        o_ref[...] = dfl
        s_ref[...] += jnp.sum(dfl, axis=0, keepdims=True)

    blk = pl.BlockSpec((tb, W), lambda i: (nb - 1 - i, 0))
    vec = pl.BlockSpec((1, W), lambda i: (0, 0))
    return pl.pallas_call(body, name=name, grid=(nb,), in_specs=[blk, blk, vec], out_specs=[blk, vec],
                          out_shape=[jax.ShapeDtypeStruct((S, W), F32), jax.ShapeDtypeStruct((1, W), F32)],
                          scratch_shapes=[pltpu.VMEM((8, W), F32)], compiler_params=_cp("arbitrary"))(dF, fl, bf)


NT_DIMS = (((1,), (1,)), ((), ()))
TN_DIMS = (((0,), (0,)), ((), ()))
HP = HEADS // 2
Q_SCALE = HEAD_DIM ** -0.5


def _head_mask(x, hh):
    lanes = lax.broadcasted_iota(jnp.int32, x.shape, 1)
    return jnp.where((lanes >= hh * HEAD_DIM) & (lanes < (hh + 1) * HEAD_DIM), x, jnp.zeros_like(x))


def _lanes_from(cols, shape):
    lanes = lax.broadcasted_iota(jnp.int32, shape, 1)
    out = jnp.zeros(shape, F32)
    for i, cvec in enumerate(cols):
        out = jnp.where(lanes == i, cvec, out)
    return out


AUG_ROWS = 16


def _qa_pair(qt, aug):
    fill = jnp.zeros((HEAD_DIM - AUG_ROWS, qt.shape[1]), qt.dtype)
    return [jnp.concatenate([qt[0:HEAD_DIM], aug[0], fill], axis=0), jnp.concatenate([aug[1], fill, qt[HEAD_DIM:]], axis=0)]


def _rows_of_head(xt, hh):
    rows = lax.broadcasted_iota(jnp.int32, xt.shape, 0)
    return jnp.where((rows >= hh * HEAD_DIM) & (rows < (hh + 1) * HEAD_DIM), xt, jnp.zeros_like(xt))


def _put_rows(ref, rows):
    for i, r in enumerate(rows):
        ref[i:i + 1, :] = r
    ref[len(rows):, :] = jnp.zeros((ref.shape[0] - len(rows), ref.shape[1]), ref.dtype)


def _diag_mask_t(s):
    rows = lax.broadcasted_iota(jnp.int32, s.shape, 0)
    cols = lax.broadcasted_iota(jnp.int32, s.shape, 1)
    return jnp.where(rows <= cols, s, NEG)


def _foxt_fwd(qkvT, ka, vtb, augq, *, tb, name, comm=()):
    S = qkvT.shape[1]
    nb = S // tb

    def body(q_ref, ka_ref, v_ref, aq_ref, o_ref, lse_ref, m0, m1, l0, l1, acc0, acc1):
        qi = pl.program_id(1)
        qa = _qa_pair(q_ref[...], aq_ref[...])
        state = ((m0, l0, acc0), (m1, l1, acc1))
        for m_sc, l_sc, acc_sc in state:
            m_sc[...] = jnp.full_like(m_sc, NEG)
            l_sc[...] = jnp.zeros_like(l_sc)
            acc_sc[...] = jnp.zeros_like(acc_sc)

        def block(kj, masked):
            k0 = pl.multiple_of(kj * tb, tb)
            logits = [jnp.dot(ka_ref[hh, pl.ds(k0, tb), :], qa[hh], preferred_element_type=F32) for hh in range(2)]
            updates = []
            for hh, (m_sc, l_sc, acc_sc) in enumerate(state):
                s = _diag_mask_t(logits[hh]) if masked else logits[hh]
                m_old = m_sc[...]
                m_new = jnp.maximum(m_old, jnp.max(s, axis=0, keepdims=True))
                alpha = jnp.exp(m_old - m_new)
                p = jnp.exp(s - m_new)
                l_sc[...] = alpha * l_sc[...] + jnp.sum(p, axis=0, keepdims=True)
                m_sc[...] = m_new
                vt = v_ref[kj, HEAD_DIM * hh:HEAD_DIM * (hh + 1), :]
                updates.append((alpha, jnp.dot(vt, p.astype(BF16), preferred_element_type=F32)))
            for (alpha, pv), (_, _, acc_sc) in zip(updates, state):
                acc_sc[...] = alpha * acc_sc[...] + pv

        def off_diagonal(kj, c):
            block(kj, False)
            return c

        lax.fori_loop(0, qi, off_diagonal, 0)
        block(qi, True)
        for hh, (m_sc, l_sc, acc_sc) in enumerate(state):
            o_ref[HEAD_DIM * hh:HEAD_DIM * (hh + 1), :] = (acc_sc[...] / l_sc[...]).astype(o_ref.dtype)
        _put_rows(lse_ref, [m_sc[...] + jnp.log(l_sc[...]) for m_sc, l_sc, _ in state])

    return _call(
        body, name=name, grid=(HP, nb), comm=comm, sem=("parallel", "parallel"), args=(qkvT, ka, vtb, augq),
        in_specs=[pl.BlockSpec((128, tb), lambda hp, qi: (hp, qi)),
                  pl.BlockSpec((None, 2, S, 128), lambda hp, qi: (hp, 0, 0, 0)),
                  pl.BlockSpec((None, nb, 128, tb), lambda hp, qi: (hp, 0, 0, 0)),
                  pl.BlockSpec((None, 2, AUG_ROWS, tb), lambda hp, qi: (hp, 0, 0, qi))],
        out_specs=[pl.BlockSpec((128, tb), lambda hp, qi: (hp, qi)), pl.BlockSpec((None, 8, tb), lambda hp, qi: (hp, 0, qi))],
        out_shape=[jax.ShapeDtypeStruct((D_MODEL, S), BF16), jax.ShapeDtypeStruct((HP, 8, S), F32)],
        scratch_shapes=[pltpu.VMEM((1, tb), F32)] * 4 + [pltpu.VMEM((HEAD_DIM, tb), F32)] * 2)


def _foxt_bwd(qkvT, qkv, ka, qtb, dotb, do, augqb, lseb, otb, *, tb, name, comm=()):
    S = qkv.shape[0]
    nb = S // tb

    def body(ka_ref, v_ref, kt_ref, qt_ref, dot_ref, qn_ref, dn_ref, aq_ref, ls_ref, ot_ref,
             dq_ref, dk_ref, dv_ref, df_ref, dr_ref, dq_sc, dk_sc, dv_sc, ds_sc, dr_sc):
        kj = pl.program_id(1)

        @pl.when(kj == 0)
        def _():
            dq_sc[...] = jnp.zeros_like(dq_sc)
            dr_sc[...] = jnp.zeros_like(dr_sc)

        dk_sc[...] = jnp.zeros_like(dk_sc)
        dv_sc[...] = jnp.zeros_like(dv_sc)
        ds_sc[...] = jnp.zeros_like(ds_sc)
        v2 = v_ref[...]
        kth = [_rows_of_head(kt_ref[...], hh) for hh in range(2)]

        def block(qi, masked):
            q0 = pl.multiple_of(qi * tb, tb)
            qa = _qa_pair(qt_ref[qi], aq_ref[qi])
            dot, ls = dot_ref[qi], ls_ref[qi]
            prod = ot_ref[qi].astype(F32) * dot.astype(F32)
            delta = [jnp.sum(prod[HEAD_DIM * hh:HEAD_DIM * (hh + 1)], axis=0, keepdims=True) for hh in range(2)]
            qn, dn = qn_ref[pl.ds(q0, tb), :], dn_ref[pl.ds(q0, tb), :]
            dq_new = dq_sc[qi]
            for hh in range(2):
                s = jnp.dot(ka_ref[hh], qa[hh], preferred_element_type=F32)
                if masked:
                    s = _diag_mask_t(s)
                p = jnp.exp(s - ls[hh:hh + 1, :])
                dp = jnp.dot(v2, _rows_of_head(dot, hh), preferred_element_type=F32)
                ds = p * (dp - delta[hh])
                dsb = ds.astype(BF16)
                dv_sc[...] += jnp.dot(p.astype(BF16), _head_mask(dn, hh), preferred_element_type=F32)
                dk_sc[...] += jnp.dot(dsb, _head_mask(qn, hh), preferred_element_type=F32)
                dq_new = dq_new + jnp.dot(kth[hh], dsb, preferred_element_type=F32)
                part = ds[:, 0:128]
                for j in range(1, tb // 128):
                    part = part + ds[:, 128 * j:128 * (j + 1)]
                ds_sc[hh] += part
                dr_sc[qi, hh:hh + 1, :] += jnp.sum(ds, axis=0, keepdims=True)
            dq_sc[qi] = dq_new

        def off_diagonal(i, c):
            block(kj + 1 + i, False)
            return c

        block(kj, True)
        lax.fori_loop(0, nb - 1 - kj, off_diagonal, 0)
        dk_ref[...] = dk_sc[...].astype(dk_ref.dtype)
        dv_ref[...] = dv_sc[...].astype(dv_ref.dtype)
        df_ref[...] = _lanes_from([-jnp.sum(ds_sc[hh], axis=1, keepdims=True) for hh in range(2)], (tb, 128))

        @pl.when(kj == nb - 1)
        def _():
            dq_ref[...] = (dq_sc[...] * Q_SCALE).astype(dq_ref.dtype)
            dr_ref[...] = dr_sc[...]

    resident = lambda rows: pl.BlockSpec((None, nb) + rows, lambda hp, kj: (hp,) + (0,) * (len(rows) + 1))
    whole = pl.BlockSpec((S, 128), lambda hp, kj: (0, hp))
    kblk = lambda off: pl.BlockSpec((tb, 128), lambda hp, kj: (kj, off + hp))
    return _call(
        body, name=name, grid=(HP, nb), comm=comm, sem=("parallel", "arbitrary"),
        args=(ka, qkv, qkvT, qtb, dotb, qkv, do, augqb, lseb, otb),
        in_specs=[pl.BlockSpec((None, 2, tb, 128), lambda hp, kj: (hp, 0, kj, 0)), kblk(2 * HP),
                  pl.BlockSpec((128, tb), lambda hp, kj: (HP + hp, kj)),
                  resident((128, tb)), resident((128, tb)), whole, whole, resident((2, AUG_ROWS, tb)), resident((8, tb)),
                  resident((128, tb))],
        out_specs=[resident((128, tb)), kblk(0), kblk(0), pl.BlockSpec((None, tb, 128), lambda hp, kj: (hp, kj, 0)),
                   resident((8, tb))],
        out_shape=[jax.ShapeDtypeStruct((HP, nb, 128, tb), BF16), jax.ShapeDtypeStruct((S, D_MODEL), BF16),
                   jax.ShapeDtypeStruct((S, D_MODEL), BF16), jax.ShapeDtypeStruct((HP, S, 128), F32),
                   jax.ShapeDtypeStruct((HP, nb, 8, tb), F32)],
        scratch_shapes=[pltpu.VMEM((nb, 128, tb), F32), pltpu.VMEM((tb, 128), F32), pltpu.VMEM((tb, 128), F32),
                        pltpu.VMEM((2, tb, 128), F32), pltpu.VMEM((nb, 8, tb), F32)])


def _split3(x):
    rnd = lambda v: lax.reduce_precision(v, exponent_bits=8, mantissa_bits=7)
    hi = rnd(x)
    mid = rnd(x - hi)
    lo = rnd(x - hi - mid)
    return hi.astype(BF16), mid.astype(BF16), lo.astype(BF16)


def _blocked(xt, nb, tb):
    lead = xt.shape[:-1]
    x = xt.reshape(lead + (nb, tb))
    return jnp.moveaxis(x, -2, 1)


def _aug_rows(bias):
    ones = jnp.ones(bias.shape, BF16)
    zeros = jnp.zeros(bias.shape, BF16)
    return jnp.stack(list(_split3(bias)) + [ones] * 3 + [zeros] * (AUG_ROWS - 6), axis=2)


def _fox_keys(qkv, fcol, *, name):
    S = qkv.shape[0]
    tr = _tile(S, 2048, 16)

    def body(k_ref, f_ref, o_ref):
        hp = pl.program_id(1)
        k2 = k_ref[...]
        lanes = lax.broadcasted_iota(jnp.int32, (tr, 128), 1)
        for hh in range(2):
            nf = -jnp.sum(jnp.where(lanes == 2 * hp + hh, f_ref[...], 0.0), axis=1, keepdims=True)
            hi = nf.astype(BF16).astype(F32)
            mid = (nf - hi).astype(BF16).astype(F32)
            lo = nf - hi - mid
            base = HEAD_DIM * (1 - hh)
            aug = jnp.where((lanes >= base) & (lanes < base + 3), 1.0, 0.0)
            for j, piece in enumerate((hi, mid, lo)):
                aug = jnp.where(lanes == base + 3 + j, piece, aug)
            mine = (lanes >= HEAD_DIM * hh) & (lanes < HEAD_DIM * (hh + 1))
            o_ref[hh] = jnp.where(mine, k2, aug.astype(BF16))

    return pl.pallas_call(
        body, name=name, grid=(S // tr, HP),
        in_specs=[pl.BlockSpec((tr, 128), lambda i, hp: (i, HP + hp)), pl.BlockSpec((tr, 128), lambda i, hp: (i, 0))],
        out_specs=pl.BlockSpec((None, 2, tr, 128), lambda i, hp: (hp, 0, i, 0)),
        out_shape=jax.ShapeDtypeStruct((HP, 2, S, 128), BF16), compiler_params=_cp("parallel", "parallel"))(qkv, fcol)


HALF = S5_SUB * S5_P
NCB = 2 * HALF // 128
RE, IM = slice(0, NCB // 2), slice(NCB // 2, NCB)


def _gelu(x):
    return 0.5 * x * (1.0 + jnp.tanh(GELU_K * (x + 0.044715 * x * x * x)))


def _gelu_grad(x):
    t = jnp.tanh(GELU_K * (x + 0.044715 * x * x * x))
    return 0.5 * (1.0 + t) + 0.5 * x * (1.0 - t * t) * GELU_K * (1.0 + 3.0 * 0.044715 * x * x)


def _s5_put(buf, s, val, lt):
    for cb in range(NCB):
        buf[cb, pl.ds(s, lt, stride=S5_SUB), :] = val[:, 128 * cb:128 * (cb + 1)]


def _s5_get(buf, s, lt):
    return jnp.concatenate([buf[cb, pl.ds(s, lt, stride=S5_SUB), :] for cb in range(NCB)], axis=1)


def _s5_project_in(u_ref, wb_ref, buf, lt):
    for s in range(S5_SUB):
        _s5_put(buf, s, jnp.dot(u_ref[:, 128 * s:128 * (s + 1)], wb_ref[s], preferred_element_type=F32), lt)


def _s5_scan(buf, lam_ref, h0, lt):
    a_re, a_im = lam_ref[RE], lam_ref[IM]

    def step(t, carry):
        hr, hi = carry
        r0 = pl.multiple_of(t * S5_SUB, S5_SUB)
        nr = a_re * hr - a_im * hi + buf[RE, pl.ds(r0, S5_SUB), :]
        ni = a_re * hi + a_im * hr + buf[IM, pl.ds(r0, S5_SUB), :]
        buf[RE, pl.ds(r0, S5_SUB), :] = nr
        buf[IM, pl.ds(r0, S5_SUB), :] = ni
        return nr, ni

    return lax.fori_loop(0, lt, step, (h0[RE], h0[IM]), unroll=4)


def _s5_fwd(u, wb, wc, lam, dskip, *, lt, name, comm=()):
    S, D = u.shape
    nt = S // lt

    def body(u_ref, wb_ref, wc_ref, lam_ref, d_ref, yp_ref, yg_ref, st_ref, buf, h_sc):
        @pl.when(pl.program_id(0) == 0)
        def _():
            h_sc[...] = jnp.zeros_like(h_sc)

        st_ref[...] = h_sc[...]
        _s5_project_in(u_ref, wb_ref, buf, lt)
        hr, hi = _s5_scan(buf, lam_ref, h_sc[...], lt)
        h_sc[RE] = hr
        h_sc[IM] = hi
        for s in range(S5_SUB):
            cols = slice(128 * s, 128 * (s + 1))
            hs = _s5_get(buf, s, lt).astype(BF16)
            yp = jnp.dot(hs, wc_ref[s], preferred_element_type=F32) + d_ref[:, cols] * u_ref[:, cols].astype(F32)
            yp_ref[:, cols] = yp.astype(yp_ref.dtype)
            yg_ref[:, cols] = _gelu(yp).astype(yg_ref.dtype)

    blk = pl.BlockSpec((lt, D), lambda i: (i, 0))
    full = lambda shp: pl.BlockSpec(shp, lambda i: (0,) * len(shp))
    state = (NCB, S5_SUB, 128)
    return _call(
        body, name=name, grid=(nt,), comm=comm, sem=("arbitrary",), args=(u, wb, wc, lam, dskip),
        in_specs=[blk, full(wb.shape), full(wc.shape), full(lam.shape), full(dskip.shape)],
        out_specs=[blk, blk, pl.BlockSpec((None,) + state, lambda i: (i, 0, 0, 0))],
        out_shape=[jax.ShapeDtypeStruct((S, D), BF16), jax.ShapeDtypeStruct((S, D), BF16), jax.ShapeDtypeStruct((nt,) + state, F32)],
        scratch_shapes=[pltpu.VMEM((NCB, lt * S5_SUB, 128), F32), pltpu.VMEM(state, F32)])


def _s5_bwd(u, dyp, st, wb, wc, lam, dskip, *, lt, name, comm=()):
    S, D = u.shape
    nt = S // lt

    def body(u_ref, dy_ref, st_ref, wb_ref, wc_ref, lam_ref, d_ref, du_ref, dwb_ref, dwc_ref, dlam_ref, hbuf, gbuf, g_sc):
        @pl.when(pl.program_id(0) == 0)
        def _():
            g_sc[...] = jnp.zeros_like(g_sc)
            dwb_ref[...] = jnp.zeros_like(dwb_ref)
            dwc_ref[...] = jnp.zeros_like(dwc_ref)
            dlam_ref[...] = jnp.zeros_like(dlam_ref)

        _s5_project_in(u_ref, wb_ref, hbuf, lt)
        _s5_scan(hbuf, lam_ref, st_ref[...], lt)
        for s in range(S5_SUB):
            dys = dy_ref[:, 128 * s:128 * (s + 1)]
            _s5_put(gbuf, s, lax.dot_general(dys, wc_ref[s], NT_DIMS, preferred_element_type=F32), lt)
        a_re, a_im = lam_ref[RE], lam_ref[IM]

        def one(t, carry, hp_re, hp_im):
            gr, gi, dar, dai = carry
            r0 = pl.multiple_of(t * S5_SUB, S5_SUB)
            nr = gbuf[RE, pl.ds(r0, S5_SUB), :] + a_re * gr + a_im * gi
            ni = gbuf[IM, pl.ds(r0, S5_SUB), :] + a_re * gi - a_im * gr
            gbuf[RE, pl.ds(r0, S5_SUB), :] = nr
            gbuf[IM, pl.ds(r0, S5_SUB), :] = ni
            return nr, ni, dar + nr * hp_re + ni * hp_im, dai + ni * hp_re - nr * hp_im

        def step(k, carry):
            t = lt - 1 - k
            p0 = pl.multiple_of((t - 1) * S5_SUB, S5_SUB)
            return one(t, carry, hbuf[RE, pl.ds(p0, S5_SUB), :], hbuf[IM, pl.ds(p0, S5_SUB), :])

        init = (g_sc[RE], g_sc[IM], dlam_ref[RE], dlam_ref[IM])
        carry = lax.fori_loop(0, lt - 1, step, init, unroll=2)
        gr, gi, dar, dai = one(0, carry, st_ref[RE], st_ref[IM])
        g_sc[RE] = gr
        g_sc[IM] = gi
        dlam_ref[RE] = dar
        dlam_ref[IM] = dai
        for s in range(S5_SUB):
            cols = slice(128 * s, 128 * (s + 1))
            gs = _s5_get(gbuf, s, lt).astype(BF16)
            hs = _s5_get(hbuf, s, lt).astype(BF16)
            us, dys = u_ref[:, cols], dy_ref[:, cols]
            du = lax.dot_general(gs, wb_ref[s], NT_DIMS, preferred_element_type=F32) + d_ref[:, cols] * dys.astype(F32)
            du_ref[:, cols] = du.astype(du_ref.dtype)
            dwb_ref[s] += lax.dot_general(us, gs, TN_DIMS, preferred_element_type=F32)
            dwc_ref[s] += lax.dot_general(hs, dys, TN_DIMS, preferred_element_type=F32)

    blk = pl.BlockSpec((lt, D), lambda i: (nt - 1 - i, 0))
    full = lambda shp: pl.BlockSpec(shp, lambda i: (0,) * len(shp))
    state = (NCB, S5_SUB, 128)
    words = pltpu.VMEM((NCB, lt * S5_SUB, 128), F32)
    return _call(
        body, name=name, grid=(nt,), comm=comm, sem=("arbitrary",), args=(u, dyp, st, wb, wc, lam, dskip),
        in_specs=[blk, blk, pl.BlockSpec((None,) + state, lambda i: (nt - 1 - i, 0, 0, 0)),
                  full(wb.shape), full(wc.shape), full(lam.shape), full(dskip.shape)],
        out_specs=[blk, full(wb.shape), full(wc.shape), full(lam.shape)],
        out_shape=[jax.ShapeDtypeStruct((S, D), BF16), jax.ShapeDtypeStruct(wb.shape, F32), jax.ShapeDtypeStruct(wc.shape, F32),
                   jax.ShapeDtypeStruct(lam.shape, F32)],
        scratch_shapes=[words, words, pltpu.VMEM(state, F32)])


def _s5_disc(lam_re, lam_im, log_dt, b_re, b_im):
    dt = jnp.exp(log_dt)[:, None]
    mag = jnp.exp(lam_re * dt)
    lb_re, lb_im = mag * jnp.cos(lam_im * dt), mag * jnp.sin(lam_im * dt)
    num_re = lb_re - 1.0
    den = lam_re * lam_re + lam_im * lam_im
    k_re = (num_re * lam_re + lb_im * lam_im) / den
    k_im = (lb_im * lam_re - num_re * lam_im) / den
    bb_re = k_re[..., None] * b_re - k_im[..., None] * b_im
    bb_im = k_re[..., None] * b_im + k_im[..., None] * b_re
    return lb_re, lb_im, bb_re, bb_im


def _s5_pack(lb_re, lb_im, bb_re, bb_im, c_re, c_im):
    eye = jnp.eye(S5_SUB, dtype=F32)
    tb = lambda x: jnp.einsum("sgcp,gh->sgchp", x.reshape(S5_SUB, S5_SUB, S5_P, S5_C).transpose(0, 1, 3, 2), eye).reshape(S5_SUB, 128, HALF)
    tc = lambda x: jnp.einsum("sgpc,gh->sgphc", x.reshape(S5_SUB, S5_SUB, S5_C, S5_P).transpose(0, 1, 3, 2), eye).reshape(S5_SUB, HALF, 128)
    wb = jnp.concatenate([tb(bb_re), tb(bb_im)], axis=2).astype(BF16)
    wc = jnp.concatenate([tc(c_re), -tc(c_im)], axis=1).astype(BF16)
    lam = jnp.concatenate([lb_re.reshape(S5_SUB, HALF), lb_im.reshape(S5_SUB, HALF)], axis=1)
    return wb, wc, lam.reshape(S5_SUB, NCB, 128).transpose(1, 0, 2)


def _s5_unpack(dwb, dwc, dlam):
    db = jnp.einsum("sgcrgp->rsgpc", dwb.reshape(S5_SUB, S5_SUB, S5_C, 2, S5_SUB, S5_P)).reshape(2, S5_G, S5_P, S5_C)
    dc = jnp.einsum("srgpgc->rsgcp", dwc.reshape(S5_SUB, 2, S5_SUB, S5_P, S5_SUB, S5_C)).reshape(2, S5_G, S5_C, S5_P)
    dlam = dlam.transpose(1, 0, 2).reshape(S5_SUB, 2 * HALF)
    return (dlam[:, :HALF].reshape(S5_G, S5_P), dlam[:, HALF:].reshape(S5_G, S5_P), db[0], db[1], dc[0], -dc[1])


def _glu_fwd(yg, gl, *, name):
    S, D = yg.shape
    tm = _tile(S, 512, 8)
    blk = pl.BlockSpec((tm, D), lambda i: (i, 0))

    def body(y_ref, g_ref, o_ref):
        o_ref[...] = (y_ref[...].astype(F32) * jax.nn.sigmoid(g_ref[...].astype(F32))).astype(o_ref.dtype)

    return pl.pallas_call(body, name=name, grid=(S // tm,), in_specs=[blk, blk], out_specs=blk,
                          out_shape=jax.ShapeDtypeStruct((S, D), BF16), compiler_params=_cp("parallel"))(yg, gl)


def _glu_bwd(dy2, yg, gl, *, name):
    S, D = yg.shape
    tm = _tile(S, 512, 8)
    blk = pl.BlockSpec((tm, D), lambda i: (i, 0))

    def body(d_ref, y_ref, g_ref, da_ref, dg_ref):
        d = d_ref[...].astype(F32)
        sg = jax.nn.sigmoid(g_ref[...].astype(F32))
        da_ref[...] = d * sg
        dg_ref[...] = (d * y_ref[...].astype(F32) * sg * (1.0 - sg)).astype(dg_ref.dtype)

    return pl.pallas_call(body, name=name, grid=(S // tm,), in_specs=[blk, blk, blk], out_specs=[blk, blk],
                          out_shape=[jax.ShapeDtypeStruct((S, D), F32), jax.ShapeDtypeStruct((S, D), BF16)],
                          compiler_params=_cp("parallel"))(dy2, yg, gl)


def _gelu_bwd(da, db, ypre, u, *, name):
    S, D = ypre.shape
    tm = _tile(S, 512, 8)
    blk = pl.BlockSpec((tm, D), lambda i: (i, 0))
    vec = pl.BlockSpec((1, D), lambda i: (0, 0))

    def body(a_ref, b_ref, y_ref, u_ref, o_ref, s_ref):
        @pl.when(pl.program_id(0) == 0)
        def _():
            s_ref[...] = jnp.zeros_like(s_ref)

        dy = (a_ref[...] + b_ref[...]) * _gelu_grad(y_ref[...].astype(F32))
        o_ref[...] = dy.astype(o_ref.dtype)
        s_ref[...] += jnp.sum(dy * u_ref[...].astype(F32), axis=0, keepdims=True)

    return pl.pallas_call(body, name=name, grid=(S // tm,), in_specs=[blk, blk, blk, blk], out_specs=[blk, vec],
                          out_shape=[jax.ShapeDtypeStruct((S, D), BF16), jax.ShapeDtypeStruct((1, D), F32)],
                          compiler_params=_cp("arbitrary"))(da, db, ypre, u)


def _my_index():
    return 4 * lax.axis_index("x") + 2 * lax.axis_index("y") + lax.axis_index("c")


def _comm_plan(payloads):
    n = len(payloads)
    hbm = pl.BlockSpec(memory_space=pl.ANY)
    shapes = [jax.ShapeDtypeStruct((N_DEV,) + (x.shape if g else x.shape[1:]), x.dtype) for x, g in payloads]
    sems = [pltpu.SemaphoreType.DMA((n, N_DEV - 1)), pltpu.SemaphoreType.DMA((n, N_DEV - 1)), pltpu.SemaphoreType.DMA((n,))] if n else []
    return [hbm] * n, [hbm] * n, shapes, sems


def _comm_copies(x_refs, o_refs, gathers, send_sems, recv_sems, local_sems):
    mx, my, mc = lax.axis_index("x"), lax.axis_index("y"), lax.axis_index("c")
    me = 4 * mx + 2 * my + mc
    copies = []
    for i, (x_ref, o_ref, gather) in enumerate(zip(x_refs, o_refs, gathers)):
        src = (lambda j, r=x_ref: r) if gather else (lambda j, r=x_ref: r.at[j])
        copies.append(pltpu.make_async_copy(src(me), o_ref.at[me], local_sems.at[i]))
        for k in range(1, N_DEV):
            px, py, pc = mx ^ (k >> 2), my ^ ((k >> 1) & 1), mc ^ (k & 1)
            copies.append(pltpu.make_async_remote_copy(src(4 * px + 2 * py + pc), o_ref.at[me], send_sems.at[i, k - 1],
                                                       recv_sems.at[i, k - 1], device_id=(px, py, pc), device_id_type=MESH_ID))
    return copies


def _call(body, *, name, grid, in_specs, out_specs, out_shape, scratch_shapes, sem, args, comm=()):
    if not comm:
        outs = pl.pallas_call(body, name=name, grid=grid, in_specs=in_specs, out_specs=out_specs, out_shape=out_shape,
                              scratch_shapes=scratch_shapes, compiler_params=_cp(*sem))(*args)
        return outs, ()
    n, n_in, n_out = len(comm), len(in_specs), len(out_specs)
    c_in, c_out, c_shapes, c_sems = _comm_plan(comm)
    gathers = [g for _, g in comm]

    def wrapped(*refs):
        own_in, x_refs = refs[:n_in], refs[n_in:n_in + n]
        own_out, o_refs = refs[n_in + n:n_in + n + n_out], refs[n_in + n + n_out:n_in + 2 * n + n_out]
        own_scratch, sems = refs[n_in + 2 * n + n_out:-3], refs[-3:]
        ids = [pl.program_id(a) for a in range(len(grid))]
        first = functools.reduce(jnp.logical_and, [i == 0 for i in ids])
        last = functools.reduce(jnp.logical_and, [i == g - 1 for i, g in zip(ids, grid)])

        @pl.when(first)
        def _():
            for cp in _comm_copies(x_refs, o_refs, gathers, *sems):
                cp.start()

        body(*own_in, *own_out, *own_scratch)

        @pl.when(last)
        def _():
            for cp in _comm_copies(x_refs, o_refs, gathers, *sems):
                cp.wait()

    outs = pl.pallas_call(wrapped, name=name, grid=grid, in_specs=list(in_specs) + c_in, out_specs=list(out_specs) + c_out,
                          out_shape=list(out_shape) + c_shapes, scratch_shapes=list(scratch_shapes) + c_sems,
                          compiler_params=_cp(*["arbitrary"] * len(grid)))(*args, *[x for x, _ in comm])
    return outs[:n_out], outs[n_out:]


def _exchange_many(payloads, *, name):
    n = len(payloads)
    in_specs, out_specs, shapes, sems = _comm_plan(payloads)
    gathers = [g for _, g in payloads]

    def body(*refs):
        copies = _comm_copies(refs[:n], refs[n:2 * n], gathers, *refs[2 * n:])
        for cp in copies:
            cp.start()
        for cp in copies:
            cp.wait()

    return pl.pallas_call(body, name=name, in_specs=in_specs, out_specs=out_specs, out_shape=shapes, scratch_shapes=sems,
                          compiler_params=pltpu.CompilerParams(has_side_effects=True))(*[x for x, _ in payloads])


def _exchange(x, *, gather, name):
    return _exchange_many([(x, gather)], name=name)[0]


def _sum8(x8, *, name):
    _, R, C = x8.shape
    tr = _tile(R, 256, 16)

    def body(x_ref, o_ref):
        acc = x_ref[0].astype(F32)
        for j in range(1, N_DEV):
            acc = acc + x_ref[j].astype(F32)
        o_ref[...] = acc

    return pl.pallas_call(body, name=name, grid=(R // tr,), in_specs=[pl.BlockSpec((N_DEV, tr, C), lambda i: (0, i, 0))],
                          out_specs=pl.BlockSpec((tr, C), lambda i: (i, 0)), out_shape=jax.ShapeDtypeStruct((R, C), F32),
                          compiler_params=_cp("parallel"))(x8)


def _pack(arrs, dtype, row_mult):
    flat = jnp.concatenate([a.reshape(-1).astype(dtype) for a in arrs])
    n = flat.shape[0]
    rows = -(-n // 1024)
    rows = -(-rows // row_mult) * row_mult
    return jnp.pad(flat, (0, rows * 1024 - n)).reshape(rows, 1024)


def _unpack(buf, shapes):
    lead = buf.shape[:-2]
    flat = buf.reshape(lead + (-1,))
    out, off = [], 0
    for s in shapes:
        n = math.prod(s)
        out.append(flat[..., off:off + n].reshape(lead + tuple(s)))
        off += n
    return out


def _adaln_fwd(c_all, w, b, *, name):
    nk, D, n = w.shape

    def body(c_ref, w_ref, b_ref, o_ref):
        cv = c_ref[...]
        sc = (cv * jax.nn.sigmoid(cv)).astype(BF16)
        o_ref[...] = jnp.dot(sc, w_ref[...].astype(BF16), preferred_element_type=F32) + b_ref[...]

    return pl.pallas_call(body, name=name, grid=(nk,),
                          in_specs=[pl.BlockSpec((N_DEV, D), lambda k: (0, 0)), pl.BlockSpec((None, D, n), lambda k: (k, 0, 0)),
                                    pl.BlockSpec((None, 1, n), lambda k: (k, 0, 0))],
                          out_specs=pl.BlockSpec((None, N_DEV, n), lambda k: (k, 0, 0)),
                          out_shape=jax.ShapeDtypeStruct((nk, N_DEV, n), F32), compiler_params=_cp("parallel"))(c_all, w, b)


def _adaln_bwd(c_all, dmod, *, name):
    nk, _, n = dmod.shape
    D = c_all.shape[1]

    def body(c_ref, d_ref, dw_ref, db_ref):
        cv = c_ref[...]
        sc = (cv * jax.nn.sigmoid(cv)).astype(BF16).astype(F32)
        dm = d_ref[...]
        dw_ref[...] = lax.dot_general(sc, dm.astype(BF16).astype(F32), TN_DIMS, precision=lax.Precision.HIGHEST,
                                      preferred_element_type=F32)
        db_ref[...] = jnp.sum(dm, axis=0, keepdims=True)

    return pl.pallas_call(body, name=name, grid=(nk,),
                          in_specs=[pl.BlockSpec((N_DEV, D), lambda k: (0, 0)), pl.BlockSpec((None, N_DEV, n), lambda k: (k, 0, 0))],
                          out_specs=[pl.BlockSpec((None, D, n), lambda k: (k, 0, 0)), pl.BlockSpec((None, 1, n), lambda k: (k, 0, 0))],
                          out_shape=[jax.ShapeDtypeStruct((nk, D, n), F32), jax.ShapeDtypeStruct((nk, 1, n), F32)],
                          compiler_params=_cp("parallel"))(c_all, dmod)


def _adamw(w, g, m, v, *, name):
    R, C = w.shape
    tr = _tile(R, 256, 8)
    blk = pl.BlockSpec((tr, C), lambda i: (i, 0))
    c1 = 1.0 / (1.0 - ADAM_B1 ** ADAM_STEP)
    c2 = 1.0 / (1.0 - ADAM_B2 ** ADAM_STEP)

    def body(w_ref, g_ref, m_ref, v_ref, d_ref, nm_ref, nv_ref):
        gv = g_ref[...]
        nm = ADAM_B1 * m_ref[...] + (1.0 - ADAM_B1) * gv
        nv = ADAM_B2 * v_ref[...] + (1.0 - ADAM_B2) * (gv * gv)
        nm_ref[...] = nm
        nv_ref[...] = nv
        d_ref[...] = -ADAM_LR * ((nm * c1) / (jnp.sqrt(nv * c2) + ADAM_EPS) + ADAM_WD * w_ref[...])

    sh = jax.ShapeDtypeStruct((R, C), F32)
    return pl.pallas_call(body, name=name, grid=(R // tr,), in_specs=[blk] * 4, out_specs=[blk] * 3, out_shape=[sh] * 3,
                          compiler_params=_cp("parallel"))(w, g, m, v)


def _adamw_nd(w, g, m, v, *, name):
    shp = w.shape
    two = (-1, shp[-1]) if w.ndim > 1 else (1, -1)
    outs = _adamw(w.reshape(two), g.reshape(two), m.reshape(two), v.reshape(two), name=name)
    return [o.reshape(shp) for o in outs]


S5_TILE = 256
ATTN_BLOCK = 512
BIG = ("s5_w_in", "s5_w_glu", "s5_w_out", "fox_w_in", "fox_w_out", "ffn_w_up", "ffn_w_down")
WEIGHTS = ("norm_g", "ada_w", "ada_b", "s5_w_in", "s5_lam_re", "s5_lam_im", "s5_log_dt", "s5_b_re", "s5_b_im", "s5_c_re",
           "s5_c_im", "s5_d", "s5_w_glu", "s5_w_out", "fox_w_in", "fox_b_f", "fox_w_out", "ffn_w_up", "ffn_conv_w",
           "ffn_conv_b", "ffn_w_down", "final_g")
REPLICATED = ("s5_lam_re", "s5_lam_im", "s5_log_dt", "s5_b_re", "s5_b_im", "s5_c_re", "s5_c_im", "s5_d", "fox_b_f",
              "ffn_conv_b", "final_g")


S5_MATS = ("s5_w_in", "s5_w_glu", "s5_w_out")


def _row_group(arrs):
    return jnp.concatenate([a.reshape(-1, D_MODEL) for a in arrs], axis=0)


def _cols_join(g):
    return jnp.concatenate([g[j] for j in range(N_DEV)], axis=1)


def _cols_split(full):
    n = full.shape[1] // N_DEV
    return jnp.stack([full[:, j * n:(j + 1) * n] for j in range(N_DEV)])


def kernel(x, c, norm_g, ada_w, ada_b, s5_w_in, s5_lam_re, s5_lam_im, s5_log_dt, s5_b_re, s5_b_im, s5_c_re, s5_c_im, s5_d, s5_w_glu, s5_w_out, fox_w_in, fox_b_f, fox_w_out, ffn_w_up, ffn_conv_w, ffn_conv_b, ffn_w_down, final_g, loss_target, m_norm_g, m_ada_w, m_ada_b, m_s5_w_in, m_s5_lam_re, m_s5_lam_im, m_s5_log_dt, m_s5_b_re, m_s5_b_im, m_s5_c_re, m_s5_c_im, m_s5_d, m_s5_w_glu, m_s5_w_out, m_fox_w_in, m_fox_b_f, m_fox_w_out, m_ffn_w_up, m_ffn_conv_w, m_ffn_conv_b, m_ffn_w_down, m_final_g, v_norm_g, v_ada_w, v_ada_b, v_s5_w_in, v_s5_lam_re, v_s5_lam_im, v_s5_log_dt, v_s5_b_re, v_s5_b_im, v_s5_c_re, v_s5_c_im, v_s5_d, v_s5_w_glu, v_s5_w_out, v_fox_w_in, v_fox_b_f, v_fox_w_out, v_ffn_w_up, v_ffn_conv_w, v_ffn_conv_b, v_ffn_w_down, v_final_g):
    args = dict(locals())
    W = {n: args[n] for n in WEIGHTS}
    M = {n: args["m_" + n] for n in WEIGHTS}
    V = {n: args["v_" + n] for n in WEIGHTS}
    D, F = D_MODEL, D_FF
    me = _my_index()
    h0 = x[0]
    S = h0.shape[0]
    lt = min(S5_TILE, S)
    tb = min(ATTN_BLOCK, S)
    n_g, n_cw = norm_g.size, ffn_conv_w.size

    g0 = _exchange(_pack([c, norm_g, ffn_conv_w], F32, 8), gather=True, name="gather_small_in")
    c_all, ng_all, cw_all = _unpack(g0, [(D,), norm_g.shape, ffn_conv_w.shape])
    ng_full = ng_all.transpose(1, 2, 0, 3).reshape(2, 2, D)
    cw_full = cw_all.transpose(1, 2, 0, 3).reshape(2, 3, F)

    ncol = ada_w.shape[-1]
    modp = _adaln_fwd(c_all, ada_w.reshape(4, D, ncol), ada_b.reshape(4, 1, ncol), name="adaln_fwd")
    g1 = _exchange(_pack([modp], F32, 8), gather=True, name="gather_adaln")
    (mod_all,) = _unpack(g1, [modp.shape])
    mod = lax.dynamic_index_in_dim(mod_all, me, axis=2, keepdims=False).transpose(1, 0, 2).reshape(4, 3 * D)
    shift = [mod[k:k + 1, :D] for k in range(4)]
    scale = [mod[k:k + 1, D:2 * D] for k in range(4)]
    gate = [mod[k:k + 1, 2 * D:] for k in range(4)]
    gain = [_row(ng_full[k // 2, k % 2]) for k in range(4)]

    b16 = lambda a: a.astype(BF16)
    rows = D // N_DEV
    g_s5 = _exchange(_row_group([b16(W[n]) for n in S5_MATS]), gather=True, name="gather_s5_weights")
    full = {n: g_s5[:, i * rows:(i + 1) * rows].reshape(D, D) for i, n in enumerate(S5_MATS)}
    bf_pad = jnp.pad(fox_b_f, ((0, 0), (0, 128 - HEADS)))

    def ffn_fwd(h, k, layer):
        hn = _modulate(h, gain[k], shift[k], scale[k], name=f"modulate{k}")
        up = _mm(hn, full[f"ffn_w_up{layer}"], name=f"ffn_up{layer}", o_h=True, out_dtype=BF16, tn=1408)
        z = _conv_gate_fwd(up, cw_full[layer], ffn_conv_b[layer:layer + 1], name=f"conv_gate{layer}")
        m, h_out = _mm(z, full[f"ffn_w_down{layer}"], name=f"ffn_down{layer}", res=h, gate=gate[k])
        return h_out, (hn, up, z, m)

    lb_re, lb_im, bb_re, bb_im = _s5_disc(s5_lam_re[0], s5_lam_im[0], s5_log_dt[0], s5_b_re[0], s5_b_im[0])
    wb, wc, lam = _s5_pack(lb_re, lb_im, bb_re, bb_im, s5_c_re[0], s5_c_im[0])
    hn0 = _modulate(h0, gain[0], shift[0], scale[0], name="modulate0")
    u = _mm(hn0, full["s5_w_in"], name="s5_in", out_dtype=BF16)
    (ypre, yg, st), (g_up0, g_rows0, g_fox) = _s5_fwd(
        u, wb, wc, lam, s5_d, lt=lt, name="s5_scan",
        comm=[(b16(ffn_w_up[0]), True), (_row_group([b16(fox_w_out), b16(ffn_w_down[0])]), True), (b16(fox_w_in[0]), True)])
    full["ffn_w_up0"] = _cols_join(g_up0)
    full["fox_w_out"] = g_rows0[:, :rows].reshape(D, D)
    full["ffn_w_down0"] = g_rows0[:, rows:].reshape(F, D)
    full["fox_w_in"] = _cols_join(g_fox)
    w_proj = jnp.pad(full["fox_w_in"], ((0, 0), (0, 3 * D + 128 - full["fox_w_in"].shape[1])))
    w_qkv, w_f = w_proj[:, :3 * D], w_proj[:, 3 * D:]
    gl = _mm(yg, full["s5_w_glu"], name="s5_glu", out_dtype=BF16)
    y2 = _glu_fwd(yg, gl, name="s5_glu_gate")
    m0, h1 = _mm(y2, full["s5_w_out"], name="s5_out", res=h0, gate=gate[0])
    h2, (hn1, up0, z0, m1) = ffn_fwd(h1, 1, 0)

    hn2 = _modulate(h2, gain[2], shift[2], scale[2], name="modulate2")
    nb = S // tb
    w_qkv_s = jnp.concatenate([w_qkv[:, :D] * Q_SCALE, w_qkv[:, D:]], axis=1)
    qkv = _mm(hn2, w_qkv_s, name="fox_qkv", out_dtype=BF16)
    qkvT = _mm(w_qkv_s, hn2, name="fox_qkv_t", ta=True, tb=True, out_dtype=BF16)
    fl = _mm(hn2, w_f, name="fox_f", out_dtype=F32)
    fcol = _fgate_fwd(fl, bf_pad, name="fox_fgate")
    f_heads = fcol[:, :HEADS].T.reshape(HP, 2, S)
    ka = _fox_keys(qkv, fcol, name="fox_keys")
    vtb = _blocked(qkvT[2 * D:].reshape(HP, 128, S), nb, tb)
    aug = _aug_rows(f_heads)
    (oT, lse), (g_up1, g_down1) = _foxt_fwd(qkvT, ka, vtb, aug, tb=tb, name="fox_attn",
                                            comm=[(b16(ffn_w_up[1]), True), (b16(ffn_w_down[1]), True)])
    full["ffn_w_up1"] = _cols_join(g_up1)
    full["ffn_w_down1"] = g_down1.reshape(F, D)
    m2, h3 = _mm(oT, full["fox_w_out"], name="fox_out", ta=True, res=h2, gate=gate[2])
    h4, (hn3, up1, z1, m3) = ffn_fwd(h3, 3, 1)

    lblk, dh, head_sums, dm = _loss_head(h4, _row(final_g), loss_target[0], m3, gate[3], name="loss_head")
    d_final_g, dgate = head_sums[0:1], head_sums[1:2]
    loss = lax.psum(lblk[0, 0], ("x", "y", "c"))

    G = {}
    dmod = [None] * 4
    mixer_out = [m0, m1, m2, m3]

    def norm_bwd(h, dhn, dh_in, k, dgate_k):
        if k == 0:
            dh_out, sums = _modulate_bwd(h, dhn, dh_in, gain[k], scale[k], name=f"modulate_bwd{k}")
            dm_below = dgate_below = None
        else:
            dh_out, sums, dm_below = _modulate_bwd(h, dhn, dh_in, gain[k], scale[k], mixer_out[k - 1], gate[k - 1],
                                                   name=f"modulate_bwd{k}")
            dgate_below = sums[3:4]
        dmod[k] = jnp.concatenate([sums[0:1], sums[1:2], dgate_k], axis=1)
        return dh_out, sums[2], dm_below, dgate_below

    def ffn_bwd(dh_in, dm, dgate_k, h, k, layer, saved):
        hn, up, z, m = saved
        dz = _mm(dm, full[f"ffn_w_down{layer}"], name=f"ffn_down_dx{layer}", tb=True, out_dtype=BF16, tn=1408)
        dw_down = _mm(z, dm, name=f"ffn_down_dw{layer}", ta=True, tm=1408, tn=1024, tk=2048)
        d_up, cs = _conv_gate_bwd(up, dz, cw_full[layer], ffn_conv_b[layer:layer + 1], name=f"conv_gate_bwd{layer}")
        dhn = _mm(d_up, full[f"ffn_w_up{layer}"], name=f"ffn_up_dx{layer}", tb=True, a_h=True, tk=2816)
        dw_up = _mm(hn, d_up, name=f"ffn_up_dw{layer}", ta=True, b_h=True, tm=1024, tn=1408, tk=2048)
        dh_out, dg, dm_below, dgate_below = norm_bwd(h, dhn, dh_in, k, dgate_k)
        return dh_out, dm_below, dgate_below, dg, dw_up, dw_down, cs[0:3], cs[3]

    dh, dm, dgate, dg3, dw_up1, dw_down1, dcw1, dcb1 = ffn_bwd(dh, dm, dgate, h3, 3, 1, (hn3, up1, z1, m3))

    do = _mm(dm, full["fox_w_out"], name="fox_out_dx", tb=True, out_dtype=BF16)
    doT = _mm(full["fox_w_out"], dm, name="fox_out_dx_t", tb=True, out_dtype=BF16)
    dw_fox_out = _mm(oT, dm, name="fox_out_dw", tn=1024, tk=2048)
    to_rows = lambda g: b16(g).reshape(N_DEV, -1, D)
    (dqtb, dk, dv, dfk, dfq), (r_up1, r_down1) = _foxt_bwd(
        qkvT, qkv, ka, _blocked(qkvT[:D].reshape(HP, 128, S), nb, tb), _blocked(doT.reshape(HP, 128, S), nb, tb), do,
        _blocked(aug, nb, tb), _blocked(lse, nb, tb), _blocked(oT.reshape(HP, 128, S), nb, tb), tb=tb, name="fox_attn_bwd",
        comm=[(_cols_split(b16(dw_up1)), False), (to_rows(dw_down1), False)])
    dq = dqtb.transpose(1, 3, 0, 2).reshape(S, D)
    dF = dfk[:, :, :2].transpose(1, 0, 2).reshape(S, HEADS) + dfq[:, :, :2, :].transpose(1, 3, 0, 2).reshape(S, HEADS)
    dF = jnp.pad(dF, ((0, 0), (0, 128 - HEADS)))
    dfl, dbf = _fgate_bwd(dF, fl, bf_pad, name="fox_fgate_bwd")
    dproj = jnp.concatenate([dq, dk, dv, dfl.astype(BF16)], axis=1)
    dhn = _mm(dproj, w_proj, name="fox_in_dx", tb=True)
    dw_proj = _mm(hn2, dproj, name="fox_in_dw", ta=True, tn=640, tk=2048)
    dw_fox_in = dw_proj[:, :full["fox_w_in"].shape[1]]
    dh, dg2, dm, dgate = norm_bwd(h2, dhn, dh, 2, dgate)

    dh, dm, dgate, dg1, dw_up0, dw_down0, dcw0, dcb0 = ffn_bwd(dh, dm, dgate, h1, 1, 0, (hn1, up0, z0, m1))

    dy2 = _mm(dm, full["s5_w_out"], name="s5_out_dx", tb=True, out_dtype=BF16)
    G["s5_w_out"] = _mm(y2, dm, name="s5_out_dw", ta=True, tn=1024, tk=2048)
    da, dgl = _glu_bwd(dy2, yg, gl, name="s5_glu_bwd")
    dyg_b = _mm(dgl, full["s5_w_glu"], name="s5_glu_dx", tb=True)
    G["s5_w_glu"] = _mm(yg, dgl, name="s5_glu_dw", ta=True, tn=1024, tk=2048)
    dyp, dd = _gelu_bwd(da, dyg_b, ypre, u, name="s5_gelu_bwd")
    (du, dwb, dwc, dlam), (r_up0, r_rows0, r_fox) = _s5_bwd(
        u, dyp, st, wb, wc, lam, s5_d, lt=lt, name="s5_scan_bwd",
        comm=[(_cols_split(b16(dw_up0)), False), (jnp.concatenate([to_rows(dw_fox_out), to_rows(dw_down0)], axis=1), False),
              (_cols_split(b16(dw_fox_in)), False)])
    dhn = _mm(du, full["s5_w_in"], name="s5_in_dx", tb=True)
    G["s5_w_in"] = _mm(hn0, du, name="s5_in_dw", ta=True, tn=1024, tk=2048)
    dh, dg0, _, _ = norm_bwd(h0, dhn, dh, 0, dgate)
    grad_x = dh[None]

    dlb_re, dlb_im, dbb_re, dbb_im, dc_re, dc_im = _s5_unpack(dwb, dwc, dlam)
    _, disc_vjp = jax.vjp(_s5_disc, s5_lam_re[0], s5_lam_im[0], s5_log_dt[0], s5_b_re[0], s5_b_im[0])
    dlam_re, dlam_im, dlog_dt, db_re, db_im = disc_vjp((dlb_re, dlb_im, dbb_re, dbb_im))

    g2 = _exchange(_pack([jnp.concatenate(dmod, axis=0)], F32, 8), gather=True, name="gather_dmod")
    (dmod_all,) = _unpack(g2, [(4, 3 * D)])
    dmod_mine = _columns_of_mod(dmod_all, me, ncol)
    d_ada_w, d_ada_b = _adaln_bwd(c_all, dmod_mine, name="adaln_bwd")
    G["ada_w"] = d_ada_w.reshape(ada_w.shape)
    G["ada_b"] = d_ada_b.reshape(ada_b.shape)

    small_full = {
        "norm_g": jnp.stack([dg0, dg1, dg2, dg3]).reshape(2, 2, D),
        "ffn_conv_w": jnp.stack([dcw0, dcw1]),
        "s5_lam_re": dlam_re[None], "s5_lam_im": dlam_im[None], "s5_log_dt": dlog_dt[None],
        "s5_b_re": db_re[None], "s5_b_im": db_im[None], "s5_c_re": dc_re[None], "s5_c_im": dc_im[None],
        "s5_d": dd, "fox_b_f": dbf[:, :HEADS], "ffn_conv_b": jnp.stack([dcb0, dcb1]), "final_g": d_final_g[0],
    }
    names = tuple(small_full)
    small_pack = _pack([small_full[n] for n in names], F32, 8 * N_DEV)
    r_s5, r_small = _exchange_many([(jnp.concatenate([to_rows(G[n]) for n in S5_MATS], axis=1), False),
                                    (small_pack.reshape(N_DEV, -1, 1024), False)], name="scatter_s5_and_small_grads")
    s5_sum = _sum8(r_s5, name="sum_s5_grads")
    for i, n in enumerate(S5_MATS):
        G[n] = s5_sum[i * rows:(i + 1) * rows][None]
    rows0_sum = _sum8(r_rows0, name="sum_rows0_grads")
    G["fox_w_out"] = rows0_sum[:rows][None]
    G["ffn_w_down"] = jnp.stack([rows0_sum[rows:], _sum8(r_down1, name="sum_down1_grads")])
    G["ffn_w_up"] = jnp.stack([_sum8(r_up0, name="sum_up0_grads"), _sum8(r_up1, name="sum_up1_grads")])
    G["fox_w_in"] = _sum8(r_fox, name="sum_fox_w_in")[None]

    g3 = _exchange(_sum8(r_small, name="sum_small_grads"), gather=True, name="gather_small_sums")
    summed = dict(zip(names, _unpack(g3.reshape(small_pack.shape), [small_full[n].shape for n in names])))
    G["norm_g"] = lax.dynamic_slice_in_dim(summed["norm_g"], me * norm_g.shape[-1], norm_g.shape[-1], axis=2)
    G["ffn_conv_w"] = lax.dynamic_slice_in_dim(summed["ffn_conv_w"], me * ffn_conv_w.shape[-1], ffn_conv_w.shape[-1], axis=2)
    for n in REPLICATED:
        G[n] = summed[n]

    delta, new_m, new_v = {}, {}, {}
    small = tuple(n for n in WEIGHTS if n not in BIG and n != "ada_w")
    for n in WEIGHTS:
        if n not in small:
            delta[n], new_m[n], new_v[n] = _adamw_nd(W[n], G[n], M[n], V[n], name=f"adamw_{n}")
    packed = [_pack([src[n] for n in small], F32, 8) for src in (W, G, M, V)]
    for dst, buf in zip((delta, new_m, new_v), _adamw(*packed, name="adamw_small")):
        dst.update(zip(small, _unpack(buf, [W[n].shape for n in small])))

    return (loss, grad_x, *[G[n] for n in WEIGHTS], *[delta[n] for n in WEIGHTS], *[new_m[n] for n in WEIGHTS],
            *[new_v[n] for n in WEIGHTS])


def _columns_of_mod(dmod_all, me, ncol):
    flat = lax.dynamic_slice_in_dim(dmod_all, me * ncol, ncol, axis=2)
    return flat.transpose(1, 0, 2)
```

```python
import functools
import math

import jax
import jax.numpy as jnp
from jax import lax
from jax.experimental import pallas as pl
from jax.experimental.pallas import tpu as pltpu

F32, BF16 = jnp.float32, jnp.bfloat16
EPS = 1e-6
N_DEV = 8
D_MODEL = 1024
D_FF = 2816
HEADS = 16
HEAD_DIM = 64
S5_G, S5_P, S5_C = 64, 64, 16
S5_SUB = 8
V7X_VMEM_LIMIT = 56 * 1024 * 1024
NEG = -1e30
ADAM_LR, ADAM_B1, ADAM_B2, ADAM_EPS, ADAM_WD, ADAM_STEP = 1e-3, 0.9, 0.999, 1e-8, 0.01, 10
GELU_K = math.sqrt(2.0 / math.pi)
MESH_ID = pl.DeviceIdType.MESH


def _cp(*sem):
    return pltpu.CompilerParams(dimension_semantics=sem, vmem_limit_bytes=V7X_VMEM_LIMIT)


def _tile(n, target, mult=128):
    if n <= target:
        return n
    t = (target // mult) * mult
    while t >= mult:
        if n % t == 0:
            return t
        t -= mult
    return n


def _row(v):
    return v.reshape(1, -1).astype(F32)


def _mm(a, b, *, name, ta=False, tb=False, out_dtype=F32, tm=1024, tn=512, tk=None, res=None, gate=None,
        a_h=False, b_h=False, o_h=False, comm=()):
    if a_h:
        M, K = a.shape[1], 2 * a.shape[2]
    elif ta:
        K, M = a.shape
    else:
        M, K = a.shape
    if b_h:
        N = 2 * b.shape[2]
    else:
        N = b.shape[0] if tb else b.shape[1]
    half_n = N // 2
    tm = _tile(M, tm, 128 if ta else 8)
    tn = _tile(half_n if (b_h or o_h) else N, tn)
    tk = K if tk is None else _tile(K // 2 if a_h else K, tk)
    nk = K // tk
    nkh, nnh = (K // 2) // tk if a_h else 1, half_n // tn
    if a_h:
        a_spec = pl.BlockSpec((None, tm, tk), lambda i, j, k: (k // nkh, i, k % nkh))
    elif ta:
        a_spec = pl.BlockSpec((tk, tm), lambda i, j, k: (k, i))
    else:
        a_spec = pl.BlockSpec((tm, tk), lambda i, j, k: (i, k))
    if b_h:
        b_spec = pl.BlockSpec((None, tk, tn), lambda i, j, k: (j // nnh, k, j % nnh))
    elif tb:
        b_spec = pl.BlockSpec((tn, tk), lambda i, j, k: (j, k))
    else:
        b_spec = pl.BlockSpec((tk, tn), lambda i, j, k: (k, j))
    if o_h:
        o_spec = pl.BlockSpec((None, tm, tn), lambda i, j, k: (j // nnh, i, j % nnh))
    else:
        o_spec = pl.BlockSpec((tm, tn), lambda i, j, k: (i, j))
    dn = (((0 if ta else 1,), (1 if tb else 0,)), ((), ()))
    fused = res is not None

    def body(*refs):
        if fused:
            a_ref, b_ref, r_ref, g_ref, m_ref, o_ref, acc_ref = refs
        else:
            a_ref, b_ref, o_ref, acc_ref = refs
        p = lax.dot_general(a_ref[...].astype(BF16), b_ref[...].astype(BF16), dn, preferred_element_type=F32)

        def finish(acc):
            if fused:
                m_ref[...] = acc.astype(m_ref.dtype)
                o_ref[...] = r_ref[...] + g_ref[...] * acc
            else:
                o_ref[...] = acc.astype(o_ref.dtype)

        if nk == 1:
            finish(p)
        else:
            k = pl.program_id(2)

            @pl.when(k == 0)
            def _():
                acc_ref[...] = p

            @pl.when(k > 0)
            def _():
                acc_ref[...] += p

            @pl.when(k == nk - 1)
            def _():
                finish(acc_ref[...])

    in_specs = [a_spec, b_spec]
    args = [a, b]
    if fused:
        in_specs += [o_spec, pl.BlockSpec((1, tn), lambda i, j, k: (0, j))]
        args += [res, gate]
        out_shape = [jax.ShapeDtypeStruct((M, N), BF16), jax.ShapeDtypeStruct((M, N), F32)]
        out_specs = [o_spec, o_spec]
    else:
        out_shape = [jax.ShapeDtypeStruct((2, M, half_n) if o_h else (M, N), out_dtype)]
        out_specs = [o_spec]
    outs, exchanged = _call(
        body, name=name, grid=(M // tm, N // tn, nk), in_specs=in_specs, out_specs=out_specs, out_shape=out_shape,
        scratch_shapes=[pltpu.VMEM((tm, tn) if nk > 1 else (8, 128), F32)], sem=("parallel", "parallel", "arbitrary"),
        args=args, comm=comm)
    outs = tuple(outs) if fused else outs[0]
    return (outs, exchanged) if comm else outs


def _modulate(h, g, shift, scale, *, name):
    S, D = h.shape
    tm = _tile(S, 512, 8)
    vec = pl.BlockSpec((1, D), lambda i: (0, 0))
    blk = pl.BlockSpec((tm, D), lambda i: (i, 0))

    def body(h_ref, g_ref, sh_ref, sc_ref, o_ref):
        x = h_ref[...]
        r = lax.rsqrt(jnp.mean(x * x, axis=-1, keepdims=True) + EPS)
        o_ref[...] = ((x * r * g_ref[...]) * (1.0 + sc_ref[...]) + sh_ref[...]).astype(o_ref.dtype)

    return pl.pallas_call(body, name=name, grid=(S // tm,), in_specs=[blk, vec, vec, vec], out_specs=blk,
                          out_shape=jax.ShapeDtypeStruct((S, D), BF16), compiler_params=_cp("parallel"))(h, g, shift, scale)


def _modulate_bwd(h, dhn, dh_in, g, scale, m_prev=None, gate_prev=None, *, name):
    S, D = h.shape
    tm = _tile(S, 512, 8)
    vec = pl.BlockSpec((1, D), lambda i: (0, 0))
    blk = pl.BlockSpec((tm, D), lambda i: (i, 0))
    sums = pl.BlockSpec((8, D), lambda i: (0, 0))
    below = m_prev is not None

    def body(h_ref, dhn_ref, dhi_ref, g_ref, sc_ref, *rest):
        dh_ref, s_ref = rest[-3:-1] if below else rest[-2:]

        @pl.when(pl.program_id(0) == 0)
        def _():
            s_ref[...] = jnp.zeros_like(s_ref)

        x = h_ref[...]
        r = lax.rsqrt(jnp.mean(x * x, axis=-1, keepdims=True) + EPS)
        xhat = x * r
        gv = g_ref[...]
        dhn_v = dhn_ref[...].astype(F32)
        dn = dhn_v * (1.0 + sc_ref[...])
        s_ref[0:1, :] += jnp.sum(dhn_v, axis=0, keepdims=True)
        s_ref[1:2, :] += jnp.sum(dhn_v * (xhat * gv), axis=0, keepdims=True)
        s_ref[2:3, :] += jnp.sum(dn * xhat, axis=0, keepdims=True)
        dxh = dn * gv
        dh = dhi_ref[...] + r * (dxh - xhat * jnp.mean(dxh * xhat, axis=-1, keepdims=True))
        dh_ref[...] = dh
        if below:
            mp_ref, gp_ref, dm_ref = rest[0], rest[1], rest[-1]
            dm_ref[...] = (dh * gp_ref[...]).astype(dm_ref.dtype)
            s_ref[3:4, :] += jnp.sum(dh * mp_ref[...].astype(F32), axis=0, keepdims=True)

    extra_in, extra_args = ([blk, vec], [m_prev, gate_prev]) if below else ([], [])
    return pl.pallas_call(body, name=name, grid=(S // tm,), in_specs=[blk, blk, blk, vec, vec] + extra_in,
                          out_specs=[blk, sums] + ([blk] if below else []),
                          out_shape=[jax.ShapeDtypeStruct((S, D), F32), jax.ShapeDtypeStruct((8, D), F32)]
                          + ([jax.ShapeDtypeStruct((S, D), BF16)] if below else []),
                          compiler_params=_cp("arbitrary"))(h, dhn, dh_in, g, scale, *extra_args)


def _loss_head(h, g, tgt, m_prev, gate_prev, *, name):
    S, D = h.shape
    tm = _tile(S, 512, 8)
    vec = pl.BlockSpec((1, D), lambda i: (0, 0))
    blk = pl.BlockSpec((tm, D), lambda i: (i, 0))
    lss = pl.BlockSpec((8, 128), lambda i: (0, 0))
    sums = pl.BlockSpec((8, D), lambda i: (0, 0))

    def body(h_ref, g_ref, t_ref, mp_ref, gp_ref, l_ref, dh_ref, s_ref, dm_ref):
        @pl.when(pl.program_id(0) == 0)
        def _():
            l_ref[...] = jnp.zeros_like(l_ref)
            s_ref[...] = jnp.zeros_like(s_ref)

        x = h_ref[...]
        r = lax.rsqrt(jnp.mean(x * x, axis=-1, keepdims=True) + EPS)
        xhat = x * r
        gv = g_ref[...]
        e = xhat * gv - t_ref[...]
        l_ref[...] += 0.5 * jnp.sum(jnp.mean(e * e, axis=-1, keepdims=True))
        dy = e * (1.0 / D)
        s_ref[0:1, :] += jnp.sum(dy * xhat, axis=0, keepdims=True)
        dxh = dy * gv
        dh = r * (dxh - xhat * jnp.mean(dxh * xhat, axis=-1, keepdims=True))
        dh_ref[...] = dh
        dm_ref[...] = (dh * gp_ref[...]).astype(dm_ref.dtype)
        s_ref[1:2, :] += jnp.sum(dh * mp_ref[...].astype(F32), axis=0, keepdims=True)

    return pl.pallas_call(body, name=name, grid=(S // tm,), in_specs=[blk, vec, blk, blk, vec], out_specs=[lss, blk, sums, blk],
                          out_shape=[jax.ShapeDtypeStruct((8, 128), F32), jax.ShapeDtypeStruct((S, D), F32),
                                     jax.ShapeDtypeStruct((8, D), F32), jax.ShapeDtypeStruct((S, D), BF16)],
                          compiler_params=_cp("arbitrary"))(h, g, tgt, m_prev, gate_prev)


def _shift_down(x, k, edge):
    tm = x.shape[0]
    rows = lax.broadcasted_iota(jnp.int32, x.shape, 0)
    out = pltpu.roll(x, k, 0)
    for j in range(k):
        out = jnp.where(rows == j, edge[8 - k + j:8 - k + j + 1, :], out)
    return out


def _shift_up(x, k, edge):
    tm = x.shape[0]
    rows = lax.broadcasted_iota(jnp.int32, x.shape, 0)
    out = pltpu.roll(x, tm - k, 0)
    for j in range(k):
        out = jnp.where(rows == tm - k + j, edge[j:j + 1, :], out)
    return out


def _conv_gate_fwd(up, cw, cb, *, name):
    _, S, F = up.shape
    tf = _tile(F, 1408)
    nf = F // tf
    tm = _tile(S, 512, 8)

    def body(a_ref, b_ref, w_ref, cb_ref, z_ref, edge_ref):
        @pl.when(pl.program_id(1) == 0)
        def _():
            edge_ref[...] = jnp.zeros_like(edge_ref)

        a = a_ref[...].astype(F32)
        edge = edge_ref[...]
        w = w_ref[...]
        ac = cb_ref[...] + w[2:3, :] * a + w[1:2, :] * _shift_down(a, 1, edge) + w[0:1, :] * _shift_down(a, 2, edge)
        edge_ref[...] = a[tm - 8:tm, :]
        z_ref[...] = (ac * jax.nn.sigmoid(ac) * b_ref[...].astype(F32)).astype(z_ref.dtype)

    return pl.pallas_call(
        body, name=name, grid=(nf, S // tm),
        in_specs=[pl.BlockSpec((None, tm, tf), lambda j, i: (0, i, j)), pl.BlockSpec((None, tm, tf), lambda j, i: (1, i, j)),
                  pl.BlockSpec((3, tf), lambda j, i: (0, j)), pl.BlockSpec((1, tf), lambda j, i: (0, j))],
        out_specs=pl.BlockSpec((tm, tf), lambda j, i: (i, j)),
        out_shape=jax.ShapeDtypeStruct((S, F), BF16), scratch_shapes=[pltpu.VMEM((8, tf), F32)],
        compiler_params=_cp("parallel", "arbitrary"))(up, up, cw, cb)


def _conv_gate_bwd(up, dz, cw, cb, *, name):
    _, S, F = up.shape
    tf = _tile(F, 1408)
    nf = F // tf
    tm = _tile(S, 512, 8)
    nt = S // tm
    hb = tm // 8

    def body(a_ref, ah_ref, b_ref, dz_ref, w_ref, cb_ref, d_ref, s_ref, edge_ref):
        i = pl.program_id(1)

        @pl.when(i == 0)
        def _():
            edge_ref[...] = jnp.zeros_like(edge_ref)
            s_ref[...] = jnp.zeros_like(s_ref)

        a = a_ref[...].astype(F32)
        halo = jnp.where(i == nt - 1, 0.0, ah_ref[...].astype(F32))
        w = w_ref[...]
        a1 = _shift_down(a, 1, halo)
        a2 = _shift_down(a, 2, halo)
        ac = cb_ref[...] + w[2:3, :] * a + w[1:2, :] * a1 + w[0:1, :] * a2
        sg = jax.nn.sigmoid(ac)
        dzv = dz_ref[...].astype(F32)
        d_ref[1] = (dzv * ac * sg).astype(d_ref.dtype)
        dac = dzv * b_ref[...].astype(F32) * (sg * (1.0 + ac * (1.0 - sg)))
        s_ref[0:1, :] += jnp.sum(dac * a2, axis=0, keepdims=True)
        s_ref[1:2, :] += jnp.sum(dac * a1, axis=0, keepdims=True)
        s_ref[2:3, :] += jnp.sum(dac * a, axis=0, keepdims=True)
        s_ref[3:4, :] += jnp.sum(dac, axis=0, keepdims=True)
        edge = edge_ref[...]
        da = w[2:3, :] * dac + w[1:2, :] * _shift_up(dac, 1, edge) + w[0:1, :] * _shift_up(dac, 2, edge)
        edge_ref[...] = dac[0:8, :]
        d_ref[0] = da.astype(d_ref.dtype)

    tile = lambda hlf: pl.BlockSpec((None, tm, tf), lambda j, i: (hlf, nt - 1 - i, j))
    d_up, sums = pl.pallas_call(
        body, name=name, grid=(nf, nt),
        in_specs=[tile(0),
                  pl.BlockSpec((None, 8, tf), lambda j, i: (0, jnp.maximum((nt - 1 - i) * hb - 1, 0), j)),
                  tile(1), pl.BlockSpec((tm, tf), lambda j, i: (nt - 1 - i, j)),
                  pl.BlockSpec((3, tf), lambda j, i: (0, j)), pl.BlockSpec((1, tf), lambda j, i: (0, j))],
        out_specs=[pl.BlockSpec((2, tm, tf), lambda j, i: (0, nt - 1 - i, j)), pl.BlockSpec((8, tf), lambda j, i: (0, j))],
        out_shape=[jax.ShapeDtypeStruct((2, S, F), BF16), jax.ShapeDtypeStruct((8, F), F32)],
        scratch_shapes=[pltpu.VMEM((8, tf), F32)],
        compiler_params=_cp("parallel", "arbitrary"))(up, up, up, dz, cw, cb)
    return d_up, sums


def _log_sigmoid(x):
    return jnp.minimum(x, 0.0) - jnp.log(1.0 + jnp.exp(-jnp.abs(x)))


def _tri_ones(n, upper):
    r = lax.broadcasted_iota(jnp.int32, (n, n), 0)
    c = lax.broadcasted_iota(jnp.int32, (n, n), 1)
    return jnp.where((c >= r) if upper else (c <= r), 1.0, 0.0).astype(F32)


def _fgate_fwd(fl, bf, *, name):
    S, W = fl.shape
    tb = _tile(S, 256, 8)

    def body(fl_ref, b_ref, o_ref, carry_ref):
        @pl.when(pl.program_id(0) == 0)
        def _():
            carry_ref[...] = jnp.zeros_like(carry_ref)

        lf = _log_sigmoid(fl_ref[...] + b_ref[...])
        cs = jnp.dot(_tri_ones(tb, False), lf, precision=lax.Precision.HIGHEST, preferred_element_type=F32) + carry_ref[0:1, :]
        o_ref[...] = cs
        carry_ref[...] = jnp.broadcast_to(cs[tb - 1:tb, :], carry_ref.shape)

    blk = pl.BlockSpec((tb, W), lambda i: (i, 0))
    return pl.pallas_call(body, name=name, grid=(S // tb,), in_specs=[blk, pl.BlockSpec((1, W), lambda i: (0, 0))], out_specs=blk,
                          out_shape=jax.ShapeDtypeStruct((S, W), F32), scratch_shapes=[pltpu.VMEM((8, W), F32)],
                          compiler_params=_cp("arbitrary"))(fl, bf)


def _fgate_bwd(dF, fl, bf, *, name):
    S, W = fl.shape
    tb = _tile(S, 256, 8)
    nb = S // tb

    def body(d_ref, fl_ref, b_ref, o_ref, s_ref, carry_ref):
        @pl.when(pl.program_id(0) == 0)
        def _():
            carry_ref[...] = jnp.zeros_like(carry_ref)
            s_ref[...] = jnp.zeros_like(s_ref)

        rc = jnp.dot(_tri_ones(tb, True), d_ref[...], precision=lax.Precision.HIGHEST, preferred_element_type=F32) + carry_ref[0:1, :]
        carry_ref[...] = jnp.broadcast_to(rc[0:1, :], carry_ref.shape)
        dfl = rc * jax.nn.sigmoid(-(fl_ref[...] + b_ref[...]))
        o_ref[...] = dfl
        s_ref[...] += jnp.sum(dfl, axis=0, keepdims=True)

    blk = pl.BlockSpec((tb, W), lambda i: (nb - 1 - i, 0))
    vec = pl.BlockSpec((1, W), lambda i: (0, 0))
    return pl.pallas_call(body, name=name, grid=(nb,), in_specs=[blk, blk, vec], out_specs=[blk, vec],
                          out_shape=[jax.ShapeDtypeStruct((S, W), F32), jax.ShapeDtypeStruct((1, W), F32)],
                          scratch_shapes=[pltpu.VMEM((8, W), F32)], compiler_params=_cp("arbitrary"))(dF, fl, bf)


NT_DIMS = (((1,), (1,)), ((), ()))
TN_DIMS = (((0,), (0,)), ((), ()))
HP = HEADS // 2
Q_SCALE = HEAD_DIM ** -0.5


def _head_mask(x, hh):
    lanes = lax.broadcasted_iota(jnp.int32, x.shape, 1)
    return jnp.where((lanes >= hh * HEAD_DIM) & (lanes < (hh + 1) * HEAD_DIM), x, jnp.zeros_like(x))


def _lanes_from(cols, shape):
    lanes = lax.broadcasted_iota(jnp.int32, shape, 1)
    out = jnp.zeros(shape, F32)
    for i, cvec in enumerate(cols):
        out = jnp.where(lanes == i, cvec, out)
    return out


AUG_ROWS = 16


def _qa_pair(qt, aug):
    fill = jnp.zeros((HEAD_DIM - AUG_ROWS, qt.shape[1]), qt.dtype)
    return [jnp.concatenate([qt[0:HEAD_DIM], aug[0], fill], axis=0), jnp.concatenate([aug[1], fill, qt[HEAD_DIM:]], axis=0)]


def _rows_of_head(xt, hh):
    rows = lax.broadcasted_iota(jnp.int32, xt.shape, 0)
    return jnp.where((rows >= hh * HEAD_DIM) & (rows < (hh + 1) * HEAD_DIM), xt, jnp.zeros_like(xt))


def _put_rows(ref, rows):
    for i, r in enumerate(rows):
        ref[i:i + 1, :] = r
    ref[len(rows):, :] = jnp.zeros((ref.shape[0] - len(rows), ref.shape[1]), ref.dtype)


def _diag_mask_t(s):
    rows = lax.broadcasted_iota(jnp.int32, s.shape, 0)
    cols = lax.broadcasted_iota(jnp.int32, s.shape, 1)
    return jnp.where(rows <= cols, s, NEG)


def _foxt_fwd(qkvT, ka, vtb, augq, *, tb, name, comm=()):
    S = qkvT.shape[1]
    nb = S // tb

    def body(q_ref, ka_ref, v_ref, aq_ref, o_ref, lse_ref, m0, m1, l0, l1, acc0, acc1):
        qi = pl.program_id(1)
        qa = _qa_pair(q_ref[...], aq_ref[...])
        state = ((m0, l0, acc0), (m1, l1, acc1))
        for m_sc, l_sc, acc_sc in state:
            m_sc[...] = jnp.full_like(m_sc, NEG)
            l_sc[...] = jnp.zeros_like(l_sc)
            acc_sc[...] = jnp.zeros_like(acc_sc)

        def block(kj, masked):
            k0 = pl.multiple_of(kj * tb, tb)
            logits = [jnp.dot(ka_ref[hh, pl.ds(k0, tb), :], qa[hh], preferred_element_type=F32) for hh in range(2)]
            updates = []
            for hh, (m_sc, l_sc, acc_sc) in enumerate(state):
                s = _diag_mask_t(logits[hh]) if masked else logits[hh]
                m_old = m_sc[...]
                m_new = jnp.maximum(m_old, jnp.max(s, axis=0, keepdims=True))
                alpha = jnp.exp(m_old - m_new)
                p = jnp.exp(s - m_new)
                l_sc[...] = alpha * l_sc[...] + jnp.sum(p, axis=0, keepdims=True)
                m_sc[...] = m_new
                vt = v_ref[kj, HEAD_DIM * hh:HEAD_DIM * (hh + 1), :]
                updates.append((alpha, jnp.dot(vt, p.astype(BF16), preferred_element_type=F32)))
            for (alpha, pv), (_, _, acc_sc) in zip(updates, state):
                acc_sc[...] = alpha * acc_sc[...] + pv

        def off_diagonal(kj, c):
            block(kj, False)
            return c

        lax.fori_loop(0, qi, off_diagonal, 0)
        block(qi, True)
        for hh, (m_sc, l_sc, acc_sc) in enumerate(state):
            o_ref[HEAD_DIM * hh:HEAD_DIM * (hh + 1), :] = (acc_sc[...] / l_sc[...]).astype(o_ref.dtype)
        _put_rows(lse_ref, [m_sc[...] + jnp.log(l_sc[...]) for m_sc, l_sc, _ in state])

    return _call(
        body, name=name, grid=(HP, nb), comm=comm, sem=("parallel", "parallel"), args=(qkvT, ka, vtb, augq),
        in_specs=[pl.BlockSpec((128, tb), lambda hp, qi: (hp, qi)),
                  pl.BlockSpec((None, 2, S, 128), lambda hp, qi: (hp, 0, 0, 0)),
                  pl.BlockSpec((None, nb, 128, tb), lambda hp, qi: (hp, 0, 0, 0)),
                  pl.BlockSpec((None, 2, AUG_ROWS, tb), lambda hp, qi: (hp, 0, 0, qi))],
        out_specs=[pl.BlockSpec((128, tb), lambda hp, qi: (hp, qi)), pl.BlockSpec((None, 8, tb), lambda hp, qi: (hp, 0, qi))],
        out_shape=[jax.ShapeDtypeStruct((D_MODEL, S), BF16), jax.ShapeDtypeStruct((HP, 8, S), F32)],
        scratch_shapes=[pltpu.VMEM((1, tb), F32)] * 4 + [pltpu.VMEM((HEAD_DIM, tb), F32)] * 2)


def _foxt_bwd(qkvT, qkv, ka, qtb, dotb, do, augqb, lseb, otb, *, tb, name, comm=()):
    S = qkv.shape[0]
    nb = S // tb

    def body(ka_ref, v_ref, kt_ref, qt_ref, dot_ref, qn_ref, dn_ref, aq_ref, ls_ref, ot_ref,
             dq_ref, dk_ref, dv_ref, df_ref, dr_ref, dq_sc, dk_sc, dv_sc, ds_sc, dr_sc):
        kj = pl.program_id(1)

        @pl.when(kj == 0)
        def _():
            dq_sc[...] = jnp.zeros_like(dq_sc)
            dr_sc[...] = jnp.zeros_like(dr_sc)

        dk_sc[...] = jnp.zeros_like(dk_sc)
        dv_sc[...] = jnp.zeros_like(dv_sc)
        ds_sc[...] = jnp.zeros_like(ds_sc)
        v2 = v_ref[...]
        kth = [kt_ref[HEAD_DIM * hh:HEAD_DIM * (hh + 1), :] for hh in range(2)]

        def block(qi, masked):
            q0 = pl.multiple_of(qi * tb, tb)
            qa = _qa_pair(qt_ref[qi], aq_ref[qi])
            dot, ls = dot_ref[qi], ls_ref[qi]
            prod = ot_ref[qi].astype(F32) * dot.astype(F32)
            delta = [jnp.sum(prod[HEAD_DIM * hh:HEAD_DIM * (hh + 1)], axis=0, keepdims=True) for hh in range(2)]
            qn, dn = qn_ref[pl.ds(q0, tb), :], dn_ref[pl.ds(q0, tb), :]
            for hh in range(2):
                s = jnp.dot(ka_ref[hh], qa[hh], preferred_element_type=F32)
                if masked:
                    s = _diag_mask_t(s)
                p = jnp.exp(s - ls[hh:hh + 1, :])
                dp = jnp.dot(v2, _rows_of_head(dot, hh), preferred_element_type=F32)
                ds = p * (dp - delta[hh])
                dsb = ds.astype(BF16)
                dv_sc[...] += jnp.dot(p.astype(BF16), _head_mask(dn, hh), preferred_element_type=F32)
                dk_sc[...] += jnp.dot(dsb, _head_mask(qn, hh), preferred_element_type=F32)
                dq_sc[qi, HEAD_DIM * hh:HEAD_DIM * (hh + 1), :] += jnp.dot(kth[hh], dsb, preferred_element_type=F32)
                part = ds[:, 0:128]
                for j in range(1, tb // 128):
                    part = part + ds[:, 128 * j:128 * (j + 1)]
                ds_sc[hh] += part
                dr_sc[qi, hh:hh + 1, :] += jnp.sum(ds, axis=0, keepdims=True)

        def off_diagonal(i, c):
            block(kj + 1 + i, False)
            return c

        block(kj, True)
        lax.fori_loop(0, nb - 1 - kj, off_diagonal, 0)
        dk_ref[...] = dk_sc[...].astype(dk_ref.dtype)
        dv_ref[...] = dv_sc[...].astype(dv_ref.dtype)
        df_ref[...] = _lanes_from([-jnp.sum(ds_sc[hh], axis=1, keepdims=True) for hh in range(2)], (tb, 128))

        @pl.when(kj == nb - 1)
        def _():
            dq_ref[...] = (dq_sc[...] * Q_SCALE).astype(dq_ref.dtype)
            dr_ref[...] = dr_sc[...]

    resident = lambda rows: pl.BlockSpec((None, nb) + rows, lambda hp, kj: (hp,) + (0,) * (len(rows) + 1))
    whole = pl.BlockSpec((S, 128), lambda hp, kj: (0, hp))
    kblk = lambda off: pl.BlockSpec((tb, 128), lambda hp, kj: (kj, off + hp))
    return _call(
        body, name=name, grid=(HP, nb), comm=comm, sem=("parallel", "arbitrary"),
        args=(ka, qkv, qkvT, qtb, dotb, qkv, do, augqb, lseb, otb),
        in_specs=[pl.BlockSpec((None, 2, tb, 128), lambda hp, kj: (hp, 0, kj, 0)), kblk(2 * HP),
                  pl.BlockSpec((128, tb), lambda hp, kj: (HP + hp, kj)),
                  resident((128, tb)), resident((128, tb)), whole, whole, resident((2, AUG_ROWS, tb)), resident((8, tb)),
                  resident((128, tb))],
        out_specs=[resident((128, tb)), kblk(0), kblk(0), pl.BlockSpec((None, tb, 128), lambda hp, kj: (hp, kj, 0)),
                   resident((8, tb))],
        out_shape=[jax.ShapeDtypeStruct((HP, nb, 128, tb), BF16), jax.ShapeDtypeStruct((S, D_MODEL), BF16),
                   jax.ShapeDtypeStruct((S, D_MODEL), BF16), jax.ShapeDtypeStruct((HP, S, 128), F32),
                   jax.ShapeDtypeStruct((HP, nb, 8, tb), F32)],
        scratch_shapes=[pltpu.VMEM((nb, 128, tb), F32), pltpu.VMEM((tb, 128), F32), pltpu.VMEM((tb, 128), F32),
                        pltpu.VMEM((2, tb, 128), F32), pltpu.VMEM((nb, 8, tb), F32)])


def _split3(x):
    rnd = lambda v: lax.reduce_precision(v, exponent_bits=8, mantissa_bits=7)
    hi = rnd(x)
    mid = rnd(x - hi)
    lo = rnd(x - hi - mid)
    return hi.astype(BF16), mid.astype(BF16), lo.astype(BF16)


def _blocked(xt, nb, tb):
    lead = xt.shape[:-1]
    x = xt.reshape(lead + (nb, tb))
    return jnp.moveaxis(x, -2, 1)


def _aug_rows(bias):
    ones = jnp.ones(bias.shape, BF16)
    zeros = jnp.zeros(bias.shape, BF16)
    return jnp.stack(list(_split3(bias)) + [ones] * 3 + [zeros] * (AUG_ROWS - 6), axis=2)


def _fox_keys(qkv, fcol, *, name):
    S = qkv.shape[0]
    tr = _tile(S, 2048, 16)

    def body(k_ref, f_ref, o_ref):
        hp = pl.program_id(1)
        k2 = k_ref[...]
        lanes = lax.broadcasted_iota(jnp.int32, (tr, 128), 1)
        for hh in range(2):
            nf = -jnp.sum(jnp.where(lanes == 2 * hp + hh, f_ref[...], 0.0), axis=1, keepdims=True)
            hi = nf.astype(BF16).astype(F32)
            mid = (nf - hi).astype(BF16).astype(F32)
            lo = nf - hi - mid
            base = HEAD_DIM * (1 - hh)
            aug = jnp.where((lanes >= base) & (lanes < base + 3), 1.0, 0.0)
            for j, piece in enumerate((hi, mid, lo)):
                aug = jnp.where(lanes == base + 3 + j, piece, aug)
            mine = (lanes >= HEAD_DIM * hh) & (lanes < HEAD_DIM * (hh + 1))
            o_ref[hh] = jnp.where(mine, k2, aug.astype(BF16))

    return pl.pallas_call(
        body, name=name, grid=(S // tr, HP),
        in_specs=[pl.BlockSpec((tr, 128), lambda i, hp: (i, HP + hp)), pl.BlockSpec((tr, 128), lambda i, hp: (i, 0))],
        out_specs=pl.BlockSpec((None, 2, tr, 128), lambda i, hp: (hp, 0, i, 0)),
        out_shape=jax.ShapeDtypeStruct((HP, 2, S, 128), BF16), compiler_params=_cp("parallel", "parallel"))(qkv, fcol)


HALF = S5_SUB * S5_P
NCB = 2 * HALF // 128
RE, IM = slice(0, NCB // 2), slice(NCB // 2, NCB)


def _gelu(x):
    return 0.5 * x * (1.0 + jnp.tanh(GELU_K * (x + 0.044715 * x * x * x)))


def _gelu_grad(x):
    t = jnp.tanh(GELU_K * (x + 0.044715 * x * x * x))
    return 0.5 * (1.0 + t) + 0.5 * x * (1.0 - t * t) * GELU_K * (1.0 + 3.0 * 0.044715 * x * x)


def _s5_put(buf, s, val, lt):
    for cb in range(NCB):
        buf[cb, pl.ds(s, lt, stride=S5_SUB), :] = val[:, 128 * cb:128 * (cb + 1)]


def _s5_get(buf, s, lt):
    return jnp.concatenate([buf[cb, pl.ds(s, lt, stride=S5_SUB), :] for cb in range(NCB)], axis=1)


def _s5_project_in(u_ref, wb_ref, buf, lt):
    for s in range(S5_SUB):
        _s5_put(buf, s, jnp.dot(u_ref[:, 128 * s:128 * (s + 1)], wb_ref[s], preferred_element_type=F32), lt)


def _s5_scan(buf, lam_ref, h0, lt):
    a_re, a_im = lam_ref[RE], lam_ref[IM]

    def step(t, carry):
        hr, hi = carry
        r0 = pl.multiple_of(t * S5_SUB, S5_SUB)
        nr = a_re * hr - a_im * hi + buf[RE, pl.ds(r0, S5_SUB), :]
        ni = a_re * hi + a_im * hr + buf[IM, pl.ds(r0, S5_SUB), :]
        buf[RE, pl.ds(r0, S5_SUB), :] = nr
        buf[IM, pl.ds(r0, S5_SUB), :] = ni
        return nr, ni

    return lax.fori_loop(0, lt, step, (h0[RE], h0[IM]), unroll=4)


def _s5_fwd(u, wb, wc, lam, dskip, *, lt, name, comm=()):
    S, D = u.shape
    nt = S // lt

    def body(u_ref, wb_ref, wc_ref, lam_ref, d_ref, yp_ref, yg_ref, st_ref, buf, h_sc):
        @pl.when(pl.program_id(0) == 0)
        def _():
            h_sc[...] = jnp.zeros_like(h_sc)

        st_ref[...] = h_sc[...]
        _s5_project_in(u_ref, wb_ref, buf, lt)
        hr, hi = _s5_scan(buf, lam_ref, h_sc[...], lt)
        h_sc[RE] = hr
        h_sc[IM] = hi
        for s in range(S5_SUB):
            cols = slice(128 * s, 128 * (s + 1))
            hs = _s5_get(buf, s, lt).astype(BF16)
            yp = jnp.dot(hs, wc_ref[s], preferred_element_type=F32) + d_ref[:, cols] * u_ref[:, cols].astype(F32)
            yp_ref[:, cols] = yp.astype(yp_ref.dtype)
            yg_ref[:, cols] = _gelu(yp).astype(yg_ref.dtype)

    blk = pl.BlockSpec((lt, D), lambda i: (i, 0))
    full = lambda shp: pl.BlockSpec(shp, lambda i: (0,) * len(shp))
    state = (NCB, S5_SUB, 128)
    return _call(
        body, name=name, grid=(nt,), comm=comm, sem=("arbitrary",), args=(u, wb, wc, lam, dskip),
        in_specs=[blk, full(wb.shape), full(wc.shape), full(lam.shape), full(dskip.shape)],
        out_specs=[blk, blk, pl.BlockSpec((None,) + state, lambda i: (i, 0, 0, 0))],
        out_shape=[jax.ShapeDtypeStruct((S, D), BF16), jax.ShapeDtypeStruct((S, D), BF16), jax.ShapeDtypeStruct((nt,) + state, F32)],
        scratch_shapes=[pltpu.VMEM((NCB, lt * S5_SUB, 128), F32), pltpu.VMEM(state, F32)])


def _s5_bwd(u, dyp, st, wb, wc, lam, dskip, *, lt, name, comm=()):
    S, D = u.shape
    nt = S // lt

    def body(u_ref, dy_ref, st_ref, wb_ref, wc_ref, lam_ref, d_ref, du_ref, dwb_ref, dwc_ref, dlam_ref, hbuf, gbuf, g_sc):
        @pl.when(pl.program_id(0) == 0)
        def _():
            g_sc[...] = jnp.zeros_like(g_sc)
            dwb_ref[...] = jnp.zeros_like(dwb_ref)
            dwc_ref[...] = jnp.zeros_like(dwc_ref)
            dlam_ref[...] = jnp.zeros_like(dlam_ref)

        _s5_project_in(u_ref, wb_ref, hbuf, lt)
        _s5_scan(hbuf, lam_ref, st_ref[...], lt)
        for s in range(S5_SUB):
            dys = dy_ref[:, 128 * s:128 * (s + 1)]
            _s5_put(gbuf, s, lax.dot_general(dys, wc_ref[s], NT_DIMS, preferred_element_type=F32), lt)
        a_re, a_im = lam_ref[RE], lam_ref[IM]

        def one(t, carry, hp_re, hp_im):
            gr, gi, dar, dai = carry
            r0 = pl.multiple_of(t * S5_SUB, S5_SUB)
            nr = gbuf[RE, pl.ds(r0, S5_SUB), :] + a_re * gr + a_im * gi
            ni = gbuf[IM, pl.ds(r0, S5_SUB), :] + a_re * gi - a_im * gr
            gbuf[RE, pl.ds(r0, S5_SUB), :] = nr
            gbuf[IM, pl.ds(r0, S5_SUB), :] = ni
            return nr, ni, dar + nr * hp_re + ni * hp_im, dai + ni * hp_re - nr * hp_im

        def step(k, carry):
            t = lt - 1 - k
            p0 = pl.multiple_of((t - 1) * S5_SUB, S5_SUB)
            return one(t, carry, hbuf[RE, pl.ds(p0, S5_SUB), :], hbuf[IM, pl.ds(p0, S5_SUB), :])

        init = (g_sc[RE], g_sc[IM], dlam_ref[RE], dlam_ref[IM])
        carry = lax.fori_loop(0, lt - 1, step, init, unroll=2)
        gr, gi, dar, dai = one(0, carry, st_ref[RE], st_ref[IM])
        g_sc[RE] = gr
        g_sc[IM] = gi
        dlam_ref[RE] = dar
        dlam_ref[IM] = dai
        for s in range(S5_SUB):
            cols = slice(128 * s, 128 * (s + 1))
            gs = _s5_get(gbuf, s, lt).astype(BF16)
            hs = _s5_get(hbuf, s, lt).astype(BF16)
            us, dys = u_ref[:, cols], dy_ref[:, cols]
            du = lax.dot_general(gs, wb_ref[s], NT_DIMS, preferred_element_type=F32) + d_ref[:, cols] * dys.astype(F32)
            du_ref[:, cols] = du.astype(du_ref.dtype)
            dwb_ref[s] += lax.dot_general(us, gs, TN_DIMS, preferred_element_type=F32)
            dwc_ref[s] += lax.dot_general(hs, dys, TN_DIMS, preferred_element_type=F32)

    blk = pl.BlockSpec((lt, D), lambda i: (nt - 1 - i, 0))
    full = lambda shp: pl.BlockSpec(shp, lambda i: (0,) * len(shp))
    state = (NCB, S5_SUB, 128)
    words = pltpu.VMEM((NCB, lt * S5_SUB, 128), F32)
    return _call(
        body, name=name, grid=(nt,), comm=comm, sem=("arbitrary",), args=(u, dyp, st, wb, wc, lam, dskip),
        in_specs=[blk, blk, pl.BlockSpec((None,) + state, lambda i: (nt - 1 - i, 0, 0, 0)),
                  full(wb.shape), full(wc.shape), full(lam.shape), full(dskip.shape)],
        out_specs=[blk, full(wb.shape), full(wc.shape), full(lam.shape)],
        out_shape=[jax.ShapeDtypeStruct((S, D), BF16), jax.ShapeDtypeStruct(wb.shape, F32), jax.ShapeDtypeStruct(wc.shape, F32),
                   jax.ShapeDtypeStruct(lam.shape, F32)],
        scratch_shapes=[words, words, pltpu.VMEM(state, F32)])


def _s5_disc(lam_re, lam_im, log_dt, b_re, b_im):
    dt = jnp.exp(log_dt)[:, None]
    mag = jnp.exp(lam_re * dt)
    lb_re, lb_im = mag * jnp.cos(lam_im * dt), mag * jnp.sin(lam_im * dt)
    num_re = lb_re - 1.0
    den = lam_re * lam_re + lam_im * lam_im
    k_re = (num_re * lam_re + lb_im * lam_im) / den
    k_im = (lb_im * lam_re - num_re * lam_im) / den
    bb_re = k_re[..., None] * b_re - k_im[..., None] * b_im
    bb_im = k_re[..., None] * b_im + k_im[..., None] * b_re
    return lb_re, lb_im, bb_re, bb_im


def _s5_pack(lb_re, lb_im, bb_re, bb_im, c_re, c_im):
    eye = jnp.eye(S5_SUB, dtype=F32)
    tb = lambda x: jnp.einsum("sgcp,gh->sgchp", x.reshape(S5_SUB, S5_SUB, S5_P, S5_C).transpose(0, 1, 3, 2), eye).reshape(S5_SUB, 128, HALF)
    tc = lambda x: jnp.einsum("sgpc,gh->sgphc", x.reshape(S5_SUB, S5_SUB, S5_C, S5_P).transpose(0, 1, 3, 2), eye).reshape(S5_SUB, HALF, 128)
    wb = jnp.concatenate([tb(bb_re), tb(bb_im)], axis=2).astype(BF16)
    wc = jnp.concatenate([tc(c_re), -tc(c_im)], axis=1).astype(BF16)
    lam = jnp.concatenate([lb_re.reshape(S5_SUB, HALF), lb_im.reshape(S5_SUB, HALF)], axis=1)
    return wb, wc, lam.reshape(S5_SUB, NCB, 128).transpose(1, 0, 2)


def _s5_unpack(dwb, dwc, dlam):
    db = jnp.einsum("sgcrgp->rsgpc", dwb.reshape(S5_SUB, S5_SUB, S5_C, 2, S5_SUB, S5_P)).reshape(2, S5_G, S5_P, S5_C)
    dc = jnp.einsum("srgpgc->rsgcp", dwc.reshape(S5_SUB, 2, S5_SUB, S5_P, S5_SUB, S5_C)).reshape(2, S5_G, S5_C, S5_P)
    dlam = dlam.transpose(1, 0, 2).reshape(S5_SUB, 2 * HALF)
    return (dlam[:, :HALF].reshape(S5_G, S5_P), dlam[:, HALF:].reshape(S5_G, S5_P), db[0], db[1], dc[0], -dc[1])


def _glu_fwd(yg, gl, *, name):
    S, D = yg.shape
    tm = _tile(S, 512, 8)
    blk = pl.BlockSpec((tm, D), lambda i: (i, 0))

    def body(y_ref, g_ref, o_ref):
        o_ref[...] = (y_ref[...].astype(F32) * jax.nn.sigmoid(g_ref[...].astype(F32))).astype(o_ref.dtype)

    return pl.pallas_call(body, name=name, grid=(S // tm,), in_specs=[blk, blk], out_specs=blk,
                          out_shape=jax.ShapeDtypeStruct((S, D), BF16), compiler_params=_cp("parallel"))(yg, gl)


def _glu_bwd(dy2, yg, gl, *, name):
    S, D = yg.shape
    tm = _tile(S, 512, 8)
    blk = pl.BlockSpec((tm, D), lambda i: (i, 0))

    def body(d_ref, y_ref, g_ref, da_ref, dg_ref):
        d = d_ref[...].astype(F32)
        sg = jax.nn.sigmoid(g_ref[...].astype(F32))
        da_ref[...] = d * sg
        dg_ref[...] = (d * y_ref[...].astype(F32) * sg * (1.0 - sg)).astype(dg_ref.dtype)

    return pl.pallas_call(body, name=name, grid=(S // tm,), in_specs=[blk, blk, blk], out_specs=[blk, blk],
                          out_shape=[jax.ShapeDtypeStruct((S, D), F32), jax.ShapeDtypeStruct((S, D), BF16)],
                          compiler_params=_cp("parallel"))(dy2, yg, gl)


def _gelu_bwd(da, db, ypre, u, *, name):
    S, D = ypre.shape
    tm = _tile(S, 512, 8)
    blk = pl.BlockSpec((tm, D), lambda i: (i, 0))
    vec = pl.BlockSpec((1, D), lambda i: (0, 0))

    def body(a_ref, b_ref, y_ref, u_ref, o_ref, s_ref):
        @pl.when(pl.program_id(0) == 0)
        def _():
            s_ref[...] = jnp.zeros_like(s_ref)

        dy = (a_ref[...] + b_ref[...]) * _gelu_grad(y_ref[...].astype(F32))
        o_ref[...] = dy.astype(o_ref.dtype)
        s_ref[...] += jnp.sum(dy * u_ref[...].astype(F32), axis=0, keepdims=True)

    return pl.pallas_call(body, name=name, grid=(S // tm,), in_specs=[blk, blk, blk, blk], out_specs=[blk, vec],
                          out_shape=[jax.ShapeDtypeStruct((S, D), BF16), jax.ShapeDtypeStruct((1, D), F32)],
                          compiler_params=_cp("arbitrary"))(da, db, ypre, u)


def _my_index():
    return 4 * lax.axis_index("x") + 2 * lax.axis_index("y") + lax.axis_index("c")


def _comm_plan(payloads):
    n = len(payloads)
    hbm = pl.BlockSpec(memory_space=pl.ANY)
    shapes = [jax.ShapeDtypeStruct((N_DEV,) + (x.shape if g else x.shape[1:]), x.dtype) for x, g in payloads]
    sems = [pltpu.SemaphoreType.DMA((n, N_DEV - 1)), pltpu.SemaphoreType.DMA((n, N_DEV - 1)), pltpu.SemaphoreType.DMA((n,))] if n else []
    return [hbm] * n, [hbm] * n, shapes, sems


def _comm_copies(x_refs, o_refs, gathers, send_sems, recv_sems, local_sems):
    mx, my, mc = lax.axis_index("x"), lax.axis_index("y"), lax.axis_index("c")
    me = 4 * mx + 2 * my + mc
    copies = []
    for i, (x_ref, o_ref, gather) in enumerate(zip(x_refs, o_refs, gathers)):
        src = (lambda j, r=x_ref: r) if gather else (lambda j, r=x_ref: r.at[j])
        copies.append(pltpu.make_async_copy(src(me), o_ref.at[me], local_sems.at[i]))
        for k in range(1, N_DEV):
            px, py, pc = mx ^ (k >> 2), my ^ ((k >> 1) & 1), mc ^ (k & 1)
            copies.append(pltpu.make_async_remote_copy(src(4 * px + 2 * py + pc), o_ref.at[me], send_sems.at[i, k - 1],
                                                       recv_sems.at[i, k - 1], device_id=(px, py, pc), device_id_type=MESH_ID))
    return copies


def _call(body, *, name, grid, in_specs, out_specs, out_shape, scratch_shapes, sem, args, comm=()):
    if not comm:
        outs = pl.pallas_call(body, name=name, grid=grid, in_specs=in_specs, out_specs=out_specs, out_shape=out_shape,
                              scratch_shapes=scratch_shapes, compiler_params=_cp(*sem))(*args)
        return outs, ()
    n, n_in, n_out = len(comm), len(in_specs), len(out_specs)
    c_in, c_out, c_shapes, c_sems = _comm_plan(comm)
    gathers = [g for _, g in comm]

    def wrapped(*refs):
        own_in, x_refs = refs[:n_in], refs[n_in:n_in + n]
        own_out, o_refs = refs[n_in + n:n_in + n + n_out], refs[n_in + n + n_out:n_in + 2 * n + n_out]
        own_scratch, sems = refs[n_in + 2 * n + n_out:-3], refs[-3:]
        ids = [pl.program_id(a) for a in range(len(grid))]
        first = functools.reduce(jnp.logical_and, [i == 0 for i in ids])
        last = functools.reduce(jnp.logical_and, [i == g - 1 for i, g in zip(ids, grid)])

        @pl.when(first)
        def _():
            for cp in _comm_copies(x_refs, o_refs, gathers, *sems):
                cp.start()

        body(*own_in, *own_out, *own_scratch)

        @pl.when(last)
        def _():
            for cp in _comm_copies(x_refs, o_refs, gathers, *sems):
                cp.wait()

    outs = pl.pallas_call(wrapped, name=name, grid=grid, in_specs=list(in_specs) + c_in, out_specs=list(out_specs) + c_out,
                          out_shape=list(out_shape) + c_shapes, scratch_shapes=list(scratch_shapes) + c_sems,
                          compiler_params=_cp(*["arbitrary"] * len(grid)))(*args, *[x for x, _ in comm])
    return outs[:n_out], outs[n_out:]


def _exchange_many(payloads, *, name):
    n = len(payloads)
    in_specs, out_specs, shapes, sems = _comm_plan(payloads)
    gathers = [g for _, g in payloads]

    def body(*refs):
        copies = _comm_copies(refs[:n], refs[n:2 * n], gathers, *refs[2 * n:])
        for cp in copies:
            cp.start()
        for cp in copies:
            cp.wait()

    return pl.pallas_call(body, name=name, in_specs=in_specs, out_specs=out_specs, out_shape=shapes, scratch_shapes=sems,
                          compiler_params=pltpu.CompilerParams(has_side_effects=True))(*[x for x, _ in payloads])


def _exchange(x, *, gather, name):
    return _exchange_many([(x, gather)], name=name)[0]


def _sum8(x8, *, name):
    _, R, C = x8.shape
    tr = _tile(R, 256, 16)

    def body(x_ref, o_ref):
        acc = x_ref[0].astype(F32)
        for j in range(1, N_DEV):
            acc = acc + x_ref[j].astype(F32)
        o_ref[...] = acc

    return pl.pallas_call(body, name=name, grid=(R // tr,), in_specs=[pl.BlockSpec((N_DEV, tr, C), lambda i: (0, i, 0))],
                          out_specs=pl.BlockSpec((tr, C), lambda i: (i, 0)), out_shape=jax.ShapeDtypeStruct((R, C), F32),
                          compiler_params=_cp("parallel"))(x8)


def _pack(arrs, dtype, row_mult):
    flat = jnp.concatenate([a.reshape(-1).astype(dtype) for a in arrs])
    n = flat.shape[0]
    rows = -(-n // 1024)
    rows = -(-rows // row_mult) * row_mult
    return jnp.pad(flat, (0, rows * 1024 - n)).reshape(rows, 1024)


def _unpack(buf, shapes):
    lead = buf.shape[:-2]
    flat = buf.reshape(lead + (-1,))
    out, off = [], 0
    for s in shapes:
        n = math.prod(s)
        out.append(flat[..., off:off + n].reshape(lead + tuple(s)))
        off += n
    return out


def _adaln_fwd(c_all, w, b, *, name):
    nk, D, n = w.shape

    def body(c_ref, w_ref, b_ref, o_ref):
        cv = c_ref[...]
        sc = (cv * jax.nn.sigmoid(cv)).astype(BF16)
        o_ref[...] = jnp.dot(sc, w_ref[...].astype(BF16), preferred_element_type=F32) + b_ref[...]

    return pl.pallas_call(body, name=name, grid=(nk,),
                          in_specs=[pl.BlockSpec((N_DEV, D), lambda k: (0, 0)), pl.BlockSpec((None, D, n), lambda k: (k, 0, 0)),
                                    pl.BlockSpec((None, 1, n), lambda k: (k, 0, 0))],
                          out_specs=pl.BlockSpec((None, N_DEV, n), lambda k: (k, 0, 0)),
                          out_shape=jax.ShapeDtypeStruct((nk, N_DEV, n), F32), compiler_params=_cp("parallel"))(c_all, w, b)


def _adaln_bwd(c_all, dmod, *, name):
    nk, _, n = dmod.shape
    D = c_all.shape[1]

    def body(c_ref, d_ref, dw_ref, db_ref):
        cv = c_ref[...]
        sc = (cv * jax.nn.sigmoid(cv)).astype(BF16).astype(F32)
        dm = d_ref[...]
        dw_ref[...] = lax.dot_general(sc, dm.astype(BF16).astype(F32), TN_DIMS, precision=lax.Precision.HIGHEST,
                                      preferred_element_type=F32)
        db_ref[...] = jnp.sum(dm, axis=0, keepdims=True)

    return pl.pallas_call(body, name=name, grid=(nk,),
                          in_specs=[pl.BlockSpec((N_DEV, D), lambda k: (0, 0)), pl.BlockSpec((None, N_DEV, n), lambda k: (k, 0, 0))],
                          out_specs=[pl.BlockSpec((None, D, n), lambda k: (k, 0, 0)), pl.BlockSpec((None, 1, n), lambda k: (k, 0, 0))],
                          out_shape=[jax.ShapeDtypeStruct((nk, D, n), F32), jax.ShapeDtypeStruct((nk, 1, n), F32)],
                          compiler_params=_cp("parallel"))(c_all, dmod)


def _adamw(w, g, m, v, *, name):
    R, C = w.shape
    tr = _tile(R, 256, 8)
    blk = pl.BlockSpec((tr, C), lambda i: (i, 0))
    c1 = 1.0 / (1.0 - ADAM_B1 ** ADAM_STEP)
    c2 = 1.0 / (1.0 - ADAM_B2 ** ADAM_STEP)

    def body(w_ref, g_ref, m_ref, v_ref, d_ref, nm_ref, nv_ref):
        gv = g_ref[...]
        nm = ADAM_B1 * m_ref[...] + (1.0 - ADAM_B1) * gv
        nv = ADAM_B2 * v_ref[...] + (1.0 - ADAM_B2) * (gv * gv)
        nm_ref[...] = nm
        nv_ref[...] = nv
        d_ref[...] = -ADAM_LR * ((nm * c1) / (jnp.sqrt(nv * c2) + ADAM_EPS) + ADAM_WD * w_ref[...])

    sh = jax.ShapeDtypeStruct((R, C), F32)
    return pl.pallas_call(body, name=name, grid=(R // tr,), in_specs=[blk] * 4, out_specs=[blk] * 3, out_shape=[sh] * 3,
                          compiler_params=_cp("parallel"))(w, g, m, v)


def _adamw_nd(w, g, m, v, *, name):
    shp = w.shape
    two = (-1, shp[-1]) if w.ndim > 1 else (1, -1)
    outs = _adamw(w.reshape(two), g.reshape(two), m.reshape(two), v.reshape(two), name=name)
    return [o.reshape(shp) for o in outs]


S5_TILE = 256
ATTN_BLOCK = 512
BIG = ("s5_w_in", "s5_w_glu", "s5_w_out", "fox_w_in", "fox_w_out", "ffn_w_up", "ffn_w_down")
WEIGHTS = ("norm_g", "ada_w", "ada_b", "s5_w_in", "s5_lam_re", "s5_lam_im", "s5_log_dt", "s5_b_re", "s5_b_im", "s5_c_re",
           "s5_c_im", "s5_d", "s5_w_glu", "s5_w_out", "fox_w_in", "fox_b_f", "fox_w_out", "ffn_w_up", "ffn_conv_w",
           "ffn_conv_b", "ffn_w_down", "final_g")
REPLICATED = ("s5_lam_re", "s5_lam_im", "s5_log_dt", "s5_b_re", "s5_b_im", "s5_c_re", "s5_c_im", "s5_d", "fox_b_f",
              "ffn_conv_b", "final_g")


S5_MATS = ("s5_w_in", "s5_w_glu", "s5_w_out")


def _row_group(arrs):
    return jnp.concatenate([a.reshape(-1, D_MODEL) for a in arrs], axis=0)


def _cols_join(g):
    return jnp.concatenate([g[j] for j in range(N_DEV)], axis=1)


def _cols_split(full):
    n = full.shape[1] // N_DEV
    return jnp.stack([full[:, j * n:(j + 1) * n] for j in range(N_DEV)])


def kernel(x, c, norm_g, ada_w, ada_b, s5_w_in, s5_lam_re, s5_lam_im, s5_log_dt, s5_b_re, s5_b_im, s5_c_re, s5_c_im, s5_d, s5_w_glu, s5_w_out, fox_w_in, fox_b_f, fox_w_out, ffn_w_up, ffn_conv_w, ffn_conv_b, ffn_w_down, final_g, loss_target, m_norm_g, m_ada_w, m_ada_b, m_s5_w_in, m_s5_lam_re, m_s5_lam_im, m_s5_log_dt, m_s5_b_re, m_s5_b_im, m_s5_c_re, m_s5_c_im, m_s5_d, m_s5_w_glu, m_s5_w_out, m_fox_w_in, m_fox_b_f, m_fox_w_out, m_ffn_w_up, m_ffn_conv_w, m_ffn_conv_b, m_ffn_w_down, m_final_g, v_norm_g, v_ada_w, v_ada_b, v_s5_w_in, v_s5_lam_re, v_s5_lam_im, v_s5_log_dt, v_s5_b_re, v_s5_b_im, v_s5_c_re, v_s5_c_im, v_s5_d, v_s5_w_glu, v_s5_w_out, v_fox_w_in, v_fox_b_f, v_fox_w_out, v_ffn_w_up, v_ffn_conv_w, v_ffn_conv_b, v_ffn_w_down, v_final_g):
    args = dict(locals())
    W = {n: args[n] for n in WEIGHTS}
    M = {n: args["m_" + n] for n in WEIGHTS}
    V = {n: args["v_" + n] for n in WEIGHTS}
    D, F = D_MODEL, D_FF
    me = _my_index()
    h0 = x[0]
    S = h0.shape[0]
    lt = min(S5_TILE, S)
    tb = min(ATTN_BLOCK, S)
    n_g, n_cw = norm_g.size, ffn_conv_w.size

    g0 = _exchange(_pack([c, norm_g, ffn_conv_w], F32, 8), gather=True, name="gather_small_in")
    c_all, ng_all, cw_all = _unpack(g0, [(D,), norm_g.shape, ffn_conv_w.shape])
    ng_full = ng_all.transpose(1, 2, 0, 3).reshape(2, 2, D)
    cw_full = cw_all.transpose(1, 2, 0, 3).reshape(2, 3, F)

    ncol = ada_w.shape[-1]
    modp = _adaln_fwd(c_all, ada_w.reshape(4, D, ncol), ada_b.reshape(4, 1, ncol), name="adaln_fwd")
    g1 = _exchange(_pack([modp], F32, 8), gather=True, name="gather_adaln")
    (mod_all,) = _unpack(g1, [modp.shape])
    mod = lax.dynamic_index_in_dim(mod_all, me, axis=2, keepdims=False).transpose(1, 0, 2).reshape(4, 3 * D)
    shift = [mod[k:k + 1, :D] for k in range(4)]
    scale = [mod[k:k + 1, D:2 * D] for k in range(4)]
    gate = [mod[k:k + 1, 2 * D:] for k in range(4)]
    gain = [_row(ng_full[k // 2, k % 2]) for k in range(4)]

    b16 = lambda a: a.astype(BF16)
    rows = D // N_DEV
    full = {"s5_w_in": _exchange(b16(s5_w_in[0]), gather=True, name="gather_s5_w_in").reshape(D, D)}
    bf_pad = jnp.pad(fox_b_f, ((0, 0), (0, 128 - HEADS)))

    def ffn_fwd(h, k, layer, comm=()):
        hn = _modulate(h, gain[k], shift[k], scale[k], name=f"modulate{k}")
        up = _mm(hn, full[f"ffn_w_up{layer}"], name=f"ffn_up{layer}", o_h=True, out_dtype=BF16, tn=1408, comm=comm)
        up, exchanged = up if comm else (up, ())
        z = _conv_gate_fwd(up, cw_full[layer], ffn_conv_b[layer:layer + 1], name=f"conv_gate{layer}")
        m, h_out = _mm(z, full[f"ffn_w_down{layer}"], name=f"ffn_down{layer}", res=h, gate=gate[k])
        return h_out, (hn, up, z, m), exchanged

    lb_re, lb_im, bb_re, bb_im = _s5_disc(s5_lam_re[0], s5_lam_im[0], s5_log_dt[0], s5_b_re[0], s5_b_im[0])
    wb, wc, lam = _s5_pack(lb_re, lb_im, bb_re, bb_im, s5_c_re[0], s5_c_im[0])
    hn0 = _modulate(h0, gain[0], shift[0], scale[0], name="modulate0")
    u = _mm(hn0, full["s5_w_in"], name="s5_in", out_dtype=BF16)
    (ypre, yg, st), (g_up0, g_rows0) = _s5_fwd(
        u, wb, wc, lam, s5_d, lt=lt, name="s5_scan",
        comm=[(b16(ffn_w_up[0]), True), (_row_group([b16(s5_w_glu), b16(s5_w_out), b16(ffn_w_down[0])]), True)])
    full["ffn_w_up0"] = _cols_join(g_up0)
    full["s5_w_glu"] = g_rows0[:, :rows].reshape(D, D)
    full["s5_w_out"] = g_rows0[:, rows:2 * rows].reshape(D, D)
    full["ffn_w_down0"] = g_rows0[:, 2 * rows:].reshape(F, D)
    gl = _mm(yg, full["s5_w_glu"], name="s5_glu", out_dtype=BF16)
    y2 = _glu_fwd(yg, gl, name="s5_glu_gate")
    m0, h1 = _mm(y2, full["s5_w_out"], name="s5_out", res=h0, gate=gate[0])
    h2, (hn1, up0, z0, m1), (g_fox, g_fox_out) = ffn_fwd(h1, 1, 0, comm=[(b16(fox_w_in[0]), True), (b16(fox_w_out[0]), True)])
    full["fox_w_in"] = _cols_join(g_fox)
    full["fox_w_out"] = g_fox_out.reshape(D, D)
    w_proj = jnp.pad(full["fox_w_in"], ((0, 0), (0, 3 * D + 128 - full["fox_w_in"].shape[1])))
    w_qkv, w_f = w_proj[:, :3 * D], w_proj[:, 3 * D:]

    hn2 = _modulate(h2, gain[2], shift[2], scale[2], name="modulate2")
    nb = S // tb
    w_qkv_s = jnp.concatenate([w_qkv[:, :D] * Q_SCALE, w_qkv[:, D:]], axis=1)
    qkv = _mm(hn2, w_qkv_s, name="fox_qkv", out_dtype=BF16)
    qkvT = _mm(w_qkv_s, hn2, name="fox_qkv_t", ta=True, tb=True, out_dtype=BF16)
    fl = _mm(hn2, w_f, name="fox_f", out_dtype=F32)
    fcol = _fgate_fwd(fl, bf_pad, name="fox_fgate")
    f_heads = fcol[:, :HEADS].T.reshape(HP, 2, S)
    ka = _fox_keys(qkv, fcol, name="fox_keys")
    vtb = _blocked(qkvT[2 * D:].reshape(HP, 128, S), nb, tb)
    aug = _aug_rows(f_heads)
    (oT, lse), (g_up1, g_down1) = _foxt_fwd(qkvT, ka, vtb, aug, tb=tb, name="fox_attn",
                                            comm=[(b16(ffn_w_up[1]), True), (b16(ffn_w_down[1]), True)])
    full["ffn_w_up1"] = _cols_join(g_up1)
    full["ffn_w_down1"] = g_down1.reshape(F, D)
    m2, h3 = _mm(oT, full["fox_w_out"], name="fox_out", ta=True, res=h2, gate=gate[2])
    h4, (hn3, up1, z1, m3), _ = ffn_fwd(h3, 3, 1)

    lblk, dh, head_sums, dm = _loss_head(h4, _row(final_g), loss_target[0], m3, gate[3], name="loss_head")
    d_final_g, dgate = head_sums[0:1], head_sums[1:2]
    loss = lax.psum(lblk[0, 0], ("x", "y", "c"))

    G = {}
    dmod = [None] * 4
    mixer_out = [m0, m1, m2, m3]

    def norm_bwd(h, dhn, dh_in, k, dgate_k):
        if k == 0:
            dh_out, sums = _modulate_bwd(h, dhn, dh_in, gain[k], scale[k], name=f"modulate_bwd{k}")
            dm_below = dgate_below = None
        else:
            dh_out, sums, dm_below = _modulate_bwd(h, dhn, dh_in, gain[k], scale[k], mixer_out[k - 1], gate[k - 1],
                                                   name=f"modulate_bwd{k}")
            dgate_below = sums[3:4]
        dmod[k] = jnp.concatenate([sums[0:1], sums[1:2], dgate_k], axis=1)
        return dh_out, sums[2], dm_below, dgate_below

    def ffn_bwd(dh_in, dm, dgate_k, h, k, layer, saved):
        hn, up, z, m = saved
        dz = _mm(dm, full[f"ffn_w_down{layer}"], name=f"ffn_down_dx{layer}", tb=True, out_dtype=BF16, tn=1408)
        dw_down = _mm(z, dm, name=f"ffn_down_dw{layer}", ta=True, tm=1408, tn=1024, tk=2048)
        d_up, cs = _conv_gate_bwd(up, dz, cw_full[layer], ffn_conv_b[layer:layer + 1], name=f"conv_gate_bwd{layer}")
        dhn = _mm(d_up, full[f"ffn_w_up{layer}"], name=f"ffn_up_dx{layer}", tb=True, a_h=True, tk=2816)
        dw_up = _mm(hn, d_up, name=f"ffn_up_dw{layer}", ta=True, b_h=True, tm=1024, tn=1408, tk=2048)
        dh_out, dg, dm_below, dgate_below = norm_bwd(h, dhn, dh_in, k, dgate_k)
        return dh_out, dm_below, dgate_below, dg, dw_up, dw_down, cs[0:3], cs[3]

    dh, dm, dgate, dg3, dw_up1, dw_down1, dcw1, dcb1 = ffn_bwd(dh, dm, dgate, h3, 3, 1, (hn3, up1, z1, m3))

    do = _mm(dm, full["fox_w_out"], name="fox_out_dx", tb=True, out_dtype=BF16)
    doT = _mm(full["fox_w_out"], dm, name="fox_out_dx_t", tb=True, out_dtype=BF16)
    dw_fox_out = _mm(oT, dm, name="fox_out_dw", tn=1024, tk=2048)
    to_rows = lambda g: b16(g).reshape(N_DEV, -1, D)
    (dqtb, dk, dv, dfk, dfq), (r_up1, r_down1) = _foxt_bwd(
        qkvT, qkv, ka, _blocked(qkvT[:D].reshape(HP, 128, S), nb, tb), _blocked(doT.reshape(HP, 128, S), nb, tb), do,
        _blocked(aug, nb, tb), _blocked(lse, nb, tb), _blocked(oT.reshape(HP, 128, S), nb, tb), tb=tb, name="fox_attn_bwd",
        comm=[(_cols_split(b16(dw_up1)), False), (to_rows(dw_down1), False)])
    dq = dqtb.transpose(1, 3, 0, 2).reshape(S, D)
    dF = dfk[:, :, :2].transpose(1, 0, 2).reshape(S, HEADS) + dfq[:, :, :2, :].transpose(1, 3, 0, 2).reshape(S, HEADS)
    dF = jnp.pad(dF, ((0, 0), (0, 128 - HEADS)))
    dfl, dbf = _fgate_bwd(dF, fl, bf_pad, name="fox_fgate_bwd")
    dproj = jnp.concatenate([dq, dk, dv, dfl.astype(BF16)], axis=1)
    dhn = _mm(dproj, w_proj, name="fox_in_dx", tb=True)
    dw_proj = _mm(hn2, dproj, name="fox_in_dw", ta=True, tn=640, tk=2048)
    dw_fox_in = dw_proj[:, :full["fox_w_in"].shape[1]]
    dh, dg2, dm, dgate = norm_bwd(h2, dhn, dh, 2, dgate)

    dh, dm, dgate, dg1, dw_up0, dw_down0, dcw0, dcb0 = ffn_bwd(dh, dm, dgate, h1, 1, 0, (hn1, up0, z0, m1))

    dy2 = _mm(dm, full["s5_w_out"], name="s5_out_dx", tb=True, out_dtype=BF16)
    G["s5_w_out"] = _mm(y2, dm, name="s5_out_dw", ta=True, tn=1024, tk=2048)
    da, dgl = _glu_bwd(dy2, yg, gl, name="s5_glu_bwd")
    dyg_b = _mm(dgl, full["s5_w_glu"], name="s5_glu_dx", tb=True)
    G["s5_w_glu"] = _mm(yg, dgl, name="s5_glu_dw", ta=True, tn=1024, tk=2048)
    dyp, dd = _gelu_bwd(da, dyg_b, ypre, u, name="s5_gelu_bwd")
    (du, dwb, dwc, dlam), (r_up0, r_rows0, r_fox) = _s5_bwd(
        u, dyp, st, wb, wc, lam, s5_d, lt=lt, name="s5_scan_bwd",
        comm=[(_cols_split(b16(dw_up0)), False), (jnp.concatenate([to_rows(dw_fox_out), to_rows(dw_down0)], axis=1), False),
              (_cols_split(b16(dw_fox_in)), False)])
    dhn = _mm(du, full["s5_w_in"], name="s5_in_dx", tb=True)
    G["s5_w_in"] = _mm(hn0, du, name="s5_in_dw", ta=True, tn=1024, tk=2048)
    dh, dg0, _, _ = norm_bwd(h0, dhn, dh, 0, dgate)
    grad_x = dh[None]

    dlb_re, dlb_im, dbb_re, dbb_im, dc_re, dc_im = _s5_unpack(dwb, dwc, dlam)
    _, disc_vjp = jax.vjp(_s5_disc, s5_lam_re[0], s5_lam_im[0], s5_log_dt[0], s5_b_re[0], s5_b_im[0])
    dlam_re, dlam_im, dlog_dt, db_re, db_im = disc_vjp((dlb_re, dlb_im, dbb_re, dbb_im))

    g2 = _exchange(_pack([jnp.concatenate(dmod, axis=0)], F32, 8), gather=True, name="gather_dmod")
    (dmod_all,) = _unpack(g2, [(4, 3 * D)])
    dmod_mine = _columns_of_mod(dmod_all, me, ncol)
    d_ada_w, d_ada_b = _adaln_bwd(c_all, dmod_mine, name="adaln_bwd")
    G["ada_w"] = d_ada_w.reshape(ada_w.shape)
    G["ada_b"] = d_ada_b.reshape(ada_b.shape)

    small_full = {
        "norm_g": jnp.stack([dg0, dg1, dg2, dg3]).reshape(2, 2, D),
        "ffn_conv_w": jnp.stack([dcw0, dcw1]),
        "s5_lam_re": dlam_re[None], "s5_lam_im": dlam_im[None], "s5_log_dt": dlog_dt[None],
        "s5_b_re": db_re[None], "s5_b_im": db_im[None], "s5_c_re": dc_re[None], "s5_c_im": dc_im[None],
        "s5_d": dd, "fox_b_f": dbf[:, :HEADS], "ffn_conv_b": jnp.stack([dcb0, dcb1]), "final_g": d_final_g[0],
    }
    names = tuple(small_full)
    small_pack = _pack([small_full[n] for n in names], F32, 8 * N_DEV)
    r_s5, r_small = _exchange_many([(jnp.concatenate([to_rows(G[n]) for n in S5_MATS], axis=1), False),
                                    (small_pack.reshape(N_DEV, -1, 1024), False)], name="scatter_s5_and_small_grads")
    s5_sum = _sum8(r_s5, name="sum_s5_grads")
    for i, n in enumerate(S5_MATS):
        G[n] = s5_sum[i * rows:(i + 1) * rows][None]
    rows0_sum = _sum8(r_rows0, name="sum_rows0_grads")
    G["fox_w_out"] = rows0_sum[:rows][None]
    G["ffn_w_down"] = jnp.stack([rows0_sum[rows:], _sum8(r_down1, name="sum_down1_grads")])
    G["ffn_w_up"] = jnp.stack([_sum8(r_up0, name="sum_up0_grads"), _sum8(r_up1, name="sum_up1_grads")])
    G["fox_w_in"] = _sum8(r_fox, name="sum_fox_w_in")[None]

    g3 = _exchange(_sum8(r_small, name="sum_small_grads"), gather=True, name="gather_small_sums")
    summed = dict(zip(names, _unpack(g3.reshape(small_pack.shape), [small_full[n].shape for n in names])))
    G["norm_g"] = lax.dynamic_slice_in_dim(summed["norm_g"], me * norm_g.shape[-1], norm_g.shape[-1], axis=2)
    G["ffn_conv_w"] = lax.dynamic_slice_in_dim(summed["ffn_conv_w"], me * ffn_conv_w.shape[-1], ffn_conv_w.shape[-1], axis=2)
    for n in REPLICATED:
        G[n] = summed[n]

    delta, new_m, new_v = {}, {}, {}
    small = tuple(n for n in WEIGHTS if n not in BIG and n != "ada_w")
    for n in WEIGHTS:
        if n not in small:
            delta[n], new_m[n], new_v[n] = _adamw_nd(W[n], G[n], M[n], V[n], name=f"adamw_{n}")
    packed = [_pack([src[n] for n in small], F32, 8) for src in (W, G, M, V)]
    for dst, buf in zip((delta, new_m, new_v), _adamw(*packed, name="adamw_small")):
        dst.update(zip(small, _unpack(buf, [W[n].shape for n in small])))

    return (loss, grad_x, *[G[n] for n in WEIGHTS], *[delta[n] for n in WEIGHTS], *[new_m[n] for n in WEIGHTS],
            *[new_v[n] for n in WEIGHTS])


def _columns_of_mod(dmod_all, me, ncol):
    flat = lax.dynamic_slice_in_dim(dmod_all, me * ncol, ncol, axis=2)
    return flat.transpose(1, 0, 2)
```

```python
import functools
import math

import jax
import jax.numpy as jnp
from jax import lax
from jax.experimental import pallas as pl
from jax.experimental.pallas import tpu as pltpu

F32, BF16 = jnp.float32, jnp.bfloat16
EPS = 1e-6
N_DEV = 8
D_MODEL = 1024
D_FF = 2816
HEADS = 16
HEAD_DIM = 64
S5_G, S5_P, S5_C = 64, 64, 16
S5_SUB = 8
V7X_VMEM_LIMIT = 56 * 1024 * 1024
NEG = -1e30
ADAM_LR, ADAM_B1, ADAM_B2, ADAM_EPS, ADAM_WD, ADAM_STEP = 1e-3, 0.9, 0.999, 1e-8, 0.01, 10
GELU_K = math.sqrt(2.0 / math.pi)
MESH_ID = pl.DeviceIdType.MESH


def _cp(*sem):
    return pltpu.CompilerParams(dimension_semantics=sem, vmem_limit_bytes=V7X_VMEM_LIMIT)


def _tile(n, target, mult=128):
    if n <= target:
        return n
    t = (target // mult) * mult
    while t >= mult:
        if n % t == 0:
            return t
        t -= mult
    return n


def _row(v):
    return v.reshape(1, -1).astype(F32)


def _mm(a, b, *, name, ta=False, tb=False, out_dtype=F32, tm=1024, tn=512, tk=None, res=None, gate=None,
        a_h=False, b_h=False, o_h=False, with_t=False, comm=()):
    if a_h:
        M, K = a.shape[1], 2 * a.shape[2]
    elif ta:
        K, M = a.shape
    else:
        M, K = a.shape
    if b_h:
        N = 2 * b.shape[2]
    else:
        N = b.shape[0] if tb else b.shape[1]
    half_n = N // 2
    tm = _tile(M, tm, 128 if ta else 8)
    tn = _tile(half_n if (b_h or o_h) else N, tn)
    tk = K if tk is None else _tile(K // 2 if a_h else K, tk)
    nk = K // tk
    nkh, nnh = (K // 2) // tk if a_h else 1, half_n // tn
    if a_h:
        a_spec = pl.BlockSpec((None, tm, tk), lambda i, j, k: (k // nkh, i, k % nkh))
    elif ta:
        a_spec = pl.BlockSpec((tk, tm), lambda i, j, k: (k, i))
    else:
        a_spec = pl.BlockSpec((tm, tk), lambda i, j, k: (i, k))
    if b_h:
        b_spec = pl.BlockSpec((None, tk, tn), lambda i, j, k: (j // nnh, k, j % nnh))
    elif tb:
        b_spec = pl.BlockSpec((tn, tk), lambda i, j, k: (j, k))
    else:
        b_spec = pl.BlockSpec((tk, tn), lambda i, j, k: (k, j))
    if o_h:
        o_spec = pl.BlockSpec((None, tm, tn), lambda i, j, k: (j // nnh, i, j % nnh))
    else:
        o_spec = pl.BlockSpec((tm, tn), lambda i, j, k: (i, j))
    dn = (((0 if ta else 1,), (1 if tb else 0,)), ((), ()))
    fused = res is not None

    def body(*refs):
        if fused:
            a_ref, b_ref, r_ref, g_ref, m_ref, o_ref, acc_ref = refs
        elif with_t:
            a_ref, b_ref, o_ref, t_ref, acc_ref = refs
        else:
            a_ref, b_ref, o_ref, acc_ref = refs
        p = lax.dot_general(a_ref[...].astype(BF16), b_ref[...].astype(BF16), dn, preferred_element_type=F32)

        def finish(acc):
            if fused:
                m_ref[...] = acc.astype(m_ref.dtype)
                o_ref[...] = r_ref[...] + g_ref[...] * acc
            else:
                o_ref[...] = acc.astype(o_ref.dtype)
                if with_t:
                    t_ref[...] = acc.T.astype(t_ref.dtype)

        if nk == 1:
            finish(p)
        else:
            k = pl.program_id(2)

            @pl.when(k == 0)
            def _():
                acc_ref[...] = p

            @pl.when(k > 0)
            def _():
                acc_ref[...] += p

            @pl.when(k == nk - 1)
            def _():
                finish(acc_ref[...])

    in_specs = [a_spec, b_spec]
    args = [a, b]
    if fused:
        in_specs += [o_spec, pl.BlockSpec((1, tn), lambda i, j, k: (0, j))]
        args += [res, gate]
        out_shape = [jax.ShapeDtypeStruct((M, N), BF16), jax.ShapeDtypeStruct((M, N), F32)]
        out_specs = [o_spec, o_spec]
    else:
        out_shape = [jax.ShapeDtypeStruct((2, M, half_n) if o_h else (M, N), out_dtype)]
        out_specs = [o_spec]
        if with_t:
            out_shape.append(jax.ShapeDtypeStruct((N, M), out_dtype))
            out_specs.append(pl.BlockSpec((tn, tm), lambda i, j, k: (j, i)))
    outs, exchanged = _call(
        body, name=name, grid=(M // tm, N // tn, nk), in_specs=in_specs, out_specs=out_specs, out_shape=out_shape,
        scratch_shapes=[pltpu.VMEM((tm, tn) if nk > 1 else (8, 128), F32)], sem=("parallel", "parallel", "arbitrary"),
        args=args, comm=comm)
    outs = tuple(outs) if (fused or with_t) else outs[0]
    return (outs, exchanged) if comm else outs


def _modulate(h, g, shift, scale, *, name):
    S, D = h.shape
    tm = _tile(S, 512, 8)
    vec = pl.BlockSpec((1, D), lambda i: (0, 0))
    blk = pl.BlockSpec((tm, D), lambda i: (i, 0))

    def body(h_ref, g_ref, sh_ref, sc_ref, o_ref):
        x = h_ref[...]
        r = lax.rsqrt(jnp.mean(x * x, axis=-1, keepdims=True) + EPS)
        o_ref[...] = ((x * r * g_ref[...]) * (1.0 + sc_ref[...]) + sh_ref[...]).astype(o_ref.dtype)

    return pl.pallas_call(body, name=name, grid=(S // tm,), in_specs=[blk, vec, vec, vec], out_specs=blk,
                          out_shape=jax.ShapeDtypeStruct((S, D), BF16), compiler_params=_cp("parallel"))(h, g, shift, scale)


def _modulate_bwd(h, dhn, dh_in, g, scale, m_prev=None, gate_prev=None, *, name):
    S, D = h.shape
    tm = _tile(S, 512, 8)
    vec = pl.BlockSpec((1, D), lambda i: (0, 0))
    blk = pl.BlockSpec((tm, D), lambda i: (i, 0))
    sums = pl.BlockSpec((8, D), lambda i: (0, 0))
    below = m_prev is not None

    def body(h_ref, dhn_ref, dhi_ref, g_ref, sc_ref, *rest):
        dh_ref, s_ref = rest[-3:-1] if below else rest[-2:]

        @pl.when(pl.program_id(0) == 0)
        def _():
            s_ref[...] = jnp.zeros_like(s_ref)

        x = h_ref[...]
        r = lax.rsqrt(jnp.mean(x * x, axis=-1, keepdims=True) + EPS)
        xhat = x * r
        gv = g_ref[...]
        dhn_v = dhn_ref[...].astype(F32)
        dn = dhn_v * (1.0 + sc_ref[...])
        s_ref[0:1, :] += jnp.sum(dhn_v, axis=0, keepdims=True)
        s_ref[1:2, :] += jnp.sum(dhn_v * (xhat * gv), axis=0, keepdims=True)
        s_ref[2:3, :] += jnp.sum(dn * xhat, axis=0, keepdims=True)
        dxh = dn * gv
        dh = dhi_ref[...] + r * (dxh - xhat * jnp.mean(dxh * xhat, axis=-1, keepdims=True))
        dh_ref[...] = dh
        if below:
            mp_ref, gp_ref, dm_ref = rest[0], rest[1], rest[-1]
            dm_ref[...] = (dh * gp_ref[...]).astype(dm_ref.dtype)
            s_ref[3:4, :] += jnp.sum(dh * mp_ref[...].astype(F32), axis=0, keepdims=True)

    extra_in, extra_args = ([blk, vec], [m_prev, gate_prev]) if below else ([], [])
    return pl.pallas_call(body, name=name, grid=(S // tm,), in_specs=[blk, blk, blk, vec, vec] + extra_in,
                          out_specs=[blk, sums] + ([blk] if below else []),
                          out_shape=[jax.ShapeDtypeStruct((S, D), F32), jax.ShapeDtypeStruct((8, D), F32)]
                          + ([jax.ShapeDtypeStruct((S, D), BF16)] if below else []),
                          compiler_params=_cp("arbitrary"))(h, dhn, dh_in, g, scale, *extra_args)


def _loss_head(h, g, tgt, m_prev, gate_prev, *, name):
    S, D = h.shape
    tm = _tile(S, 512, 8)
    vec = pl.BlockSpec((1, D), lambda i: (0, 0))
    blk = pl.BlockSpec((tm, D), lambda i: (i, 0))
    lss = pl.BlockSpec((8, 128), lambda i: (0, 0))
    sums = pl.BlockSpec((8, D), lambda i: (0, 0))

    def body(h_ref, g_ref, t_ref, mp_ref, gp_ref, l_ref, dh_ref, s_ref, dm_ref):
        @pl.when(pl.program_id(0) == 0)
        def _():
            l_ref[...] = jnp.zeros_like(l_ref)
            s_ref[...] = jnp.zeros_like(s_ref)

        x = h_ref[...]
        r = lax.rsqrt(jnp.mean(x * x, axis=-1, keepdims=True) + EPS)
        xhat = x * r
        gv = g_ref[...]
        e = xhat * gv - t_ref[...]
        l_ref[...] += 0.5 * jnp.sum(jnp.mean(e * e, axis=-1, keepdims=True))
        dy = e * (1.0 / D)
        s_ref[0:1, :] += jnp.sum(dy * xhat, axis=0, keepdims=True)
        dxh = dy * gv
        dh = r * (dxh - xhat * jnp.mean(dxh * xhat, axis=-1, keepdims=True))
        dh_ref[...] = dh
        dm_ref[...] = (dh * gp_ref[...]).astype(dm_ref.dtype)
        s_ref[1:2, :] += jnp.sum(dh * mp_ref[...].astype(F32), axis=0, keepdims=True)

    return pl.pallas_call(body, name=name, grid=(S // tm,), in_specs=[blk, vec, blk, blk, vec], out_specs=[lss, blk, sums, blk],
                          out_shape=[jax.ShapeDtypeStruct((8, 128), F32), jax.ShapeDtypeStruct((S, D), F32),
                                     jax.ShapeDtypeStruct((8, D), F32), jax.ShapeDtypeStruct((S, D), BF16)],
                          compiler_params=_cp("arbitrary"))(h, g, tgt, m_prev, gate_prev)


def _shift_down(x, k, edge):
    tm = x.shape[0]
    rows = lax.broadcasted_iota(jnp.int32, x.shape, 0)
    out = pltpu.roll(x, k, 0)
    for j in range(k):
        out = jnp.where(rows == j, edge[8 - k + j:8 - k + j + 1, :], out)
    return out


def _shift_up(x, k, edge):
    tm = x.shape[0]
    rows = lax.broadcasted_iota(jnp.int32, x.shape, 0)
    out = pltpu.roll(x, tm - k, 0)
    for j in range(k):
        out = jnp.where(rows == tm - k + j, edge[j:j + 1, :], out)
    return out


def _conv_gate_fwd(up, cw, cb, *, name):
    _, S, F = up.shape
    tf = _tile(F, 1408)
    nf = F // tf
    tm = _tile(S, 512, 8)

    def body(a_ref, b_ref, w_ref, cb_ref, z_ref, edge_ref):
        @pl.when(pl.program_id(1) == 0)
        def _():
            edge_ref[...] = jnp.zeros_like(edge_ref)

        a = a_ref[...].astype(F32)
        edge = edge_ref[...]
        w = w_ref[...]
        ac = cb_ref[...] + w[2:3, :] * a + w[1:2, :] * _shift_down(a, 1, edge) + w[0:1, :] * _shift_down(a, 2, edge)
        edge_ref[...] = a[tm - 8:tm, :]
        z_ref[...] = (ac * jax.nn.sigmoid(ac) * b_ref[...].astype(F32)).astype(z_ref.dtype)

    return pl.pallas_call(
        body, name=name, grid=(nf, S // tm),
        in_specs=[pl.BlockSpec((None, tm, tf), lambda j, i: (0, i, j)), pl.BlockSpec((None, tm, tf), lambda j, i: (1, i, j)),
                  pl.BlockSpec((3, tf), lambda j, i: (0, j)), pl.BlockSpec((1, tf), lambda j, i: (0, j))],
        out_specs=pl.BlockSpec((tm, tf), lambda j, i: (i, j)),
        out_shape=jax.ShapeDtypeStruct((S, F), BF16), scratch_shapes=[pltpu.VMEM((8, tf), F32)],
        compiler_params=_cp("parallel", "arbitrary"))(up, up, cw, cb)


def _conv_gate_bwd(up, dz, cw, cb, *, name):
    _, S, F = up.shape
    tf = _tile(F, 1408)
    nf = F // tf
    tm = _tile(S, 512, 8)
    nt = S // tm
    hb = tm // 8

    def body(a_ref, ah_ref, b_ref, dz_ref, w_ref, cb_ref, d_ref, s_ref, edge_ref):
        i = pl.program_id(1)

        @pl.when(i == 0)
        def _():
            edge_ref[...] = jnp.zeros_like(edge_ref)
            s_ref[...] = jnp.zeros_like(s_ref)

        a = a_ref[...].astype(F32)
        halo = jnp.where(i == nt - 1, 0.0, ah_ref[...].astype(F32))
        w = w_ref[...]
        a1 = _shift_down(a, 1, halo)
        a2 = _shift_down(a, 2, halo)
        ac = cb_ref[...] + w[2:3, :] * a + w[1:2, :] * a1 + w[0:1, :] * a2
        sg = jax.nn.sigmoid(ac)
        dzv = dz_ref[...].astype(F32)
        d_ref[1] = (dzv * ac * sg).astype(d_ref.dtype)
        dac = dzv * b_ref[...].astype(F32) * (sg * (1.0 + ac * (1.0 - sg)))
        s_ref[0:1, :] += jnp.sum(dac * a2, axis=0, keepdims=True)
        s_ref[1:2, :] += jnp.sum(dac * a1, axis=0, keepdims=True)
        s_ref[2:3, :] += jnp.sum(dac * a, axis=0, keepdims=True)
        s_ref[3:4, :] += jnp.sum(dac, axis=0, keepdims=True)
        edge = edge_ref[...]
        da = w[2:3, :] * dac + w[1:2, :] * _shift_up(dac, 1, edge) + w[0:1, :] * _shift_up(dac, 2, edge)
        edge_ref[...] = dac[0:8, :]
        d_ref[0] = da.astype(d_ref.dtype)

    tile = lambda hlf: pl.BlockSpec((None, tm, tf), lambda j, i: (hlf, nt - 1 - i, j))
    d_up, sums = pl.pallas_call(
        body, name=name, grid=(nf, nt),
        in_specs=[tile(0),
                  pl.BlockSpec((None, 8, tf), lambda j, i: (0, jnp.maximum((nt - 1 - i) * hb - 1, 0), j)),
                  tile(1), pl.BlockSpec((tm, tf), lambda j, i: (nt - 1 - i, j)),
                  pl.BlockSpec((3, tf), lambda j, i: (0, j)), pl.BlockSpec((1, tf), lambda j, i: (0, j))],
        out_specs=[pl.BlockSpec((2, tm, tf), lambda j, i: (0, nt - 1 - i, j)), pl.BlockSpec((8, tf), lambda j, i: (0, j))],
        out_shape=[jax.ShapeDtypeStruct((2, S, F), BF16), jax.ShapeDtypeStruct((8, F), F32)],
        scratch_shapes=[pltpu.VMEM((8, tf), F32)],
        compiler_params=_cp("parallel", "arbitrary"))(up, up, up, dz, cw, cb)
    return d_up, sums


def _log_sigmoid(x):
    return jnp.minimum(x, 0.0) - jnp.log(1.0 + jnp.exp(-jnp.abs(x)))


def _tri_ones(n, upper):
    r = lax.broadcasted_iota(jnp.int32, (n, n), 0)
    c = lax.broadcasted_iota(jnp.int32, (n, n), 1)
    return jnp.where((c >= r) if upper else (c <= r), 1.0, 0.0).astype(F32)


def _fgate_fwd(fl, bf, *, name):
    S, W = fl.shape
    tb = _tile(S, 256, 8)

    def body(fl_ref, b_ref, o_ref, carry_ref):
        @pl.when(pl.program_id(0) == 0)
        def _():
            carry_ref[...] = jnp.zeros_like(carry_ref)

        lf = _log_sigmoid(fl_ref[...] + b_ref[...])
        cs = jnp.dot(_tri_ones(tb, False), lf, precision=lax.Precision.HIGHEST, preferred_element_type=F32) + carry_ref[0:1, :]
        o_ref[...] = cs
        carry_ref[...] = jnp.broadcast_to(cs[tb - 1:tb, :], carry_ref.shape)

    blk = pl.BlockSpec((tb, W), lambda i: (i, 0))
    return pl.pallas_call(body, name=name, grid=(S // tb,), in_specs=[blk, pl.BlockSpec((1, W), lambda i: (0, 0))], out_specs=blk,
                          out_shape=jax.ShapeDtypeStruct((S, W), F32), scratch_shapes=[pltpu.VMEM((8, W), F32)],
                          compiler_params=_cp("arbitrary"))(fl, bf)


def _fgate_bwd(dF, fl, bf, *, name):
    S, W = fl.shape
    tb = _tile(S, 256, 8)
    nb = S // tb

    def body(d_ref, fl_ref, b_ref, o_ref, s_ref, carry_ref):
        @pl.when(pl.program_id(0) == 0)
        def _():
            carry_ref[...] = jnp.zeros_like(carry_ref)
            s_ref[...] = jnp.zeros_like(s_ref)

        rc = jnp.dot(_tri_ones(tb, True), d_ref[...], precision=lax.Precision.HIGHEST, preferred_element_type=F32) + carry_ref[0:1, :]
        carry_ref[...] = jnp.broadcast_to(rc[0:1, :], carry_ref.shape)
        dfl = rc * jax.nn.sigmoid(-(fl_ref[...] + b_ref[...]))
        o_ref[...] = dfl
        s_ref[...] += jnp.sum(dfl, axis=0, keepdims=True)

    blk = pl.BlockSpec((tb, W), lambda i: (nb - 1 - i, 0))
    vec = pl.BlockSpec((1, W), lambda i: (0, 0))
    return pl.pallas_call(body, name=name, grid=(nb,), in_specs=[blk, blk, vec], out_specs=[blk, vec],
                          out_shape=[jax.ShapeDtypeStruct((S, W), F32), jax.ShapeDtypeStruct((1, W), F32)],
                          scratch_shapes=[pltpu.VMEM((8, W), F32)], compiler_params=_cp("arbitrary"))(dF, fl, bf)


NT_DIMS = (((1,), (1,)), ((), ()))
TN_DIMS = (((0,), (0,)), ((), ()))
HP = HEADS // 2
Q_SCALE = HEAD_DIM ** -0.5


def _head_mask(x, hh):
    lanes = lax.broadcasted_iota(jnp.int32, x.shape, 1)
    return jnp.where((lanes >= hh * HEAD_DIM) & (lanes < (hh + 1) * HEAD_DIM), x, jnp.zeros_like(x))


def _lanes_from(cols, shape):
    lanes = lax.broadcasted_iota(jnp.int32, shape, 1)
    out = jnp.zeros(shape, F32)
    for i, cvec in enumerate(cols):
        out = jnp.where(lanes == i, cvec, out)
    return out


AUG_ROWS = 16


def _qa_pair(qt, aug):
    fill = jnp.zeros((HEAD_DIM - AUG_ROWS, qt.shape[1]), qt.dtype)
    return [jnp.concatenate([qt[0:HEAD_DIM], aug[0], fill], axis=0), jnp.concatenate([aug[1], fill, qt[HEAD_DIM:]], axis=0)]


def _rows_of_head(xt, hh):
    rows = lax.broadcasted_iota(jnp.int32, xt.shape, 0)
    return jnp.where((rows >= hh * HEAD_DIM) & (rows < (hh + 1) * HEAD_DIM), xt, jnp.zeros_like(xt))


def _put_rows(ref, rows):
    for i, r in enumerate(rows):
        ref[i:i + 1, :] = r
    ref[len(rows):, :] = jnp.zeros((ref.shape[0] - len(rows), ref.shape[1]), ref.dtype)


def _diag_mask_t(s):
    rows = lax.broadcasted_iota(jnp.int32, s.shape, 0)
    cols = lax.broadcasted_iota(jnp.int32, s.shape, 1)
    return jnp.where(rows <= cols, s, NEG)


def _foxt_fwd(qkvT, ka, vtb, augq, *, tb, name, comm=()):
    S = qkvT.shape[1]
    nb = S // tb

    def body(q_ref, ka_ref, v_ref, aq_ref, o_ref, lse_ref, m0, m1, l0, l1, acc0, acc1):
        qi = pl.program_id(1)
        qa = _qa_pair(q_ref[...], aq_ref[...])
        state = ((m0, l0, acc0), (m1, l1, acc1))
        for m_sc, l_sc, acc_sc in state:
            m_sc[...] = jnp.full_like(m_sc, NEG)
            l_sc[...] = jnp.zeros_like(l_sc)
            acc_sc[...] = jnp.zeros_like(acc_sc)

        def block(kj, masked):
            k0 = pl.multiple_of(kj * tb, tb)
            logits = [jnp.dot(ka_ref[hh, pl.ds(k0, tb), :], qa[hh], preferred_element_type=F32) for hh in range(2)]
            updates = []
            for hh, (m_sc, l_sc, acc_sc) in enumerate(state):
                s = _diag_mask_t(logits[hh]) if masked else logits[hh]
                m_old = m_sc[...]
                m_new = jnp.maximum(m_old, jnp.max(s, axis=0, keepdims=True))
                alpha = jnp.exp(m_old - m_new)
                p = jnp.exp(s - m_new)
                l_sc[...] = alpha * l_sc[...] + jnp.sum(p, axis=0, keepdims=True)
                m_sc[...] = m_new
                vt = v_ref[kj, HEAD_DIM * hh:HEAD_DIM * (hh + 1), :]
                updates.append((alpha, jnp.dot(vt, p.astype(BF16), preferred_element_type=F32)))
            for (alpha, pv), (_, _, acc_sc) in zip(updates, state):
                acc_sc[...] = alpha * acc_sc[...] + pv

        def off_diagonal(kj, c):
            block(kj, False)
            return c

        lax.fori_loop(0, qi, off_diagonal, 0)
        block(qi, True)
        for hh, (m_sc, l_sc, acc_sc) in enumerate(state):
            o_ref[HEAD_DIM * hh:HEAD_DIM * (hh + 1), :] = (acc_sc[...] / l_sc[...]).astype(o_ref.dtype)
        _put_rows(lse_ref, [m_sc[...] + jnp.log(l_sc[...]) for m_sc, l_sc, _ in state])

    return _call(
        body, name=name, grid=(HP, nb), comm=comm, sem=("parallel", "parallel"), args=(qkvT, ka, vtb, augq),
        in_specs=[pl.BlockSpec((128, tb), lambda hp, qi: (hp, qi)),
                  pl.BlockSpec((None, 2, S, 128), lambda hp, qi: (hp, 0, 0, 0)),
                  pl.BlockSpec((None, nb, 128, tb), lambda hp, qi: (hp, 0, 0, 0)),
                  pl.BlockSpec((None, 2, AUG_ROWS, tb), lambda hp, qi: (hp, 0, 0, qi))],
        out_specs=[pl.BlockSpec((128, tb), lambda hp, qi: (hp, qi)), pl.BlockSpec((None, 8, tb), lambda hp, qi: (hp, 0, qi))],
        out_shape=[jax.ShapeDtypeStruct((D_MODEL, S), BF16), jax.ShapeDtypeStruct((HP, 8, S), F32)],
        scratch_shapes=[pltpu.VMEM((1, tb), F32)] * 4 + [pltpu.VMEM((HEAD_DIM, tb), F32)] * 2)


def _foxt_bwd(qkvT, qkv, ka, qtb, dotb, do, augqb, lseb, otb, *, tb, name, comm=()):
    S = qkv.shape[0]
    nb = S // tb

    def body(ka_ref, v_ref, kt_ref, qt_ref, dot_ref, qn_ref, dn_ref, aq_ref, ls_ref, ot_ref,
             dq_ref, dk_ref, dv_ref, df_ref, dr_ref, dq_sc, dk_sc, dv_sc, ds_sc, dr_sc):
        kj = pl.program_id(1)

        @pl.when(kj == 0)
        def _():
            dq_sc[...] = jnp.zeros_like(dq_sc)
            dr_sc[...] = jnp.zeros_like(dr_sc)

        dk_sc[...] = jnp.zeros_like(dk_sc)
        dv_sc[...] = jnp.zeros_like(dv_sc)
        ds_sc[...] = jnp.zeros_like(ds_sc)
        v2 = v_ref[...]
        kth = [kt_ref[HEAD_DIM * hh:HEAD_DIM * (hh + 1), :] for hh in range(2)]

        def block(qi, masked):
            q0 = pl.multiple_of(qi * tb, tb)
            qa = _qa_pair(qt_ref[qi], aq_ref[qi])
            dot, ls = dot_ref[qi], ls_ref[qi]
            prod = ot_ref[qi].astype(F32) * dot.astype(F32)
            delta = [jnp.sum(prod[HEAD_DIM * hh:HEAD_DIM * (hh + 1)], axis=0, keepdims=True) for hh in range(2)]
            qn, dn = qn_ref[pl.ds(q0, tb), :], dn_ref[pl.ds(q0, tb), :]
            for hh in range(2):
                s = jnp.dot(ka_ref[hh], qa[hh], preferred_element_type=F32)
                if masked:
                    s = _diag_mask_t(s)
                p = jnp.exp(s - ls[hh:hh + 1, :])
                dp = jnp.dot(v2, _rows_of_head(dot, hh), preferred_element_type=F32)
                ds = p * (dp - delta[hh])
                dsb = ds.astype(BF16)
                dv_sc[...] += jnp.dot(p.astype(BF16), _head_mask(dn, hh), preferred_element_type=F32)
                dk_sc[...] += jnp.dot(dsb, _head_mask(qn, hh), preferred_element_type=F32)
                dq_sc[qi, HEAD_DIM * hh:HEAD_DIM * (hh + 1), :] += jnp.dot(kth[hh], dsb, preferred_element_type=F32)
                part = ds[:, 0:128]
                for j in range(1, tb // 128):
                    part = part + ds[:, 128 * j:128 * (j + 1)]
                ds_sc[hh] += part
                dr_sc[qi, hh:hh + 1, :] += jnp.sum(ds, axis=0, keepdims=True)

        def off_diagonal(i, c):
            block(kj + 1 + i, False)
            return c

        block(kj, True)
        lax.fori_loop(0, nb - 1 - kj, off_diagonal, 0)
        dk_ref[...] = dk_sc[...].astype(dk_ref.dtype)
        dv_ref[...] = dv_sc[...].astype(dv_ref.dtype)
        df_ref[...] = _lanes_from([-jnp.sum(ds_sc[hh], axis=1, keepdims=True) for hh in range(2)], (tb, 128))

        @pl.when(kj == nb - 1)
        def _():
            dq_ref[...] = (dq_sc[...] * Q_SCALE).astype(dq_ref.dtype)
            dr_ref[...] = dr_sc[...]

    resident = lambda rows: pl.BlockSpec((None, nb) + rows, lambda hp, kj: (hp,) + (0,) * (len(rows) + 1))
    whole = pl.BlockSpec((S, 128), lambda hp, kj: (0, hp))
    kblk = lambda off: pl.BlockSpec((tb, 128), lambda hp, kj: (kj, off + hp))
    return _call(
        body, name=name, grid=(HP, nb), comm=comm, sem=("parallel", "arbitrary"),
        args=(ka, qkv, qkvT, qtb, dotb, qkv, do, augqb, lseb, otb),
        in_specs=[pl.BlockSpec((None, 2, tb, 128), lambda hp, kj: (hp, 0, kj, 0)), kblk(2 * HP),
                  pl.BlockSpec((128, tb), lambda hp, kj: (HP + hp, kj)),
                  resident((128, tb)), resident((128, tb)), whole, whole, resident((2, AUG_ROWS, tb)), resident((8, tb)),
                  resident((128, tb))],
        out_specs=[resident((128, tb)), kblk(0), kblk(0), pl.BlockSpec((None, tb, 128), lambda hp, kj: (hp, kj, 0)),
                   resident((8, tb))],
        out_shape=[jax.ShapeDtypeStruct((HP, nb, 128, tb), BF16), jax.ShapeDtypeStruct((S, D_MODEL), BF16),
                   jax.ShapeDtypeStruct((S, D_MODEL), BF16), jax.ShapeDtypeStruct((HP, S, 128), F32),
                   jax.ShapeDtypeStruct((HP, nb, 8, tb), F32)],
        scratch_shapes=[pltpu.VMEM((nb, 128, tb), F32), pltpu.VMEM((tb, 128), F32), pltpu.VMEM((tb, 128), F32),
                        pltpu.VMEM((2, tb, 128), F32), pltpu.VMEM((nb, 8, tb), F32)])


def _split3(x):
    rnd = lambda v: lax.reduce_precision(v, exponent_bits=8, mantissa_bits=7)
    hi = rnd(x)
    mid = rnd(x - hi)
    lo = rnd(x - hi - mid)
    return hi.astype(BF16), mid.astype(BF16), lo.astype(BF16)


def _blocked(xt, nb, tb):
    lead = xt.shape[:-1]
    x = xt.reshape(lead + (nb, tb))
    return jnp.moveaxis(x, -2, 1)


def _aug_rows(bias):
    ones = jnp.ones(bias.shape, BF16)
    zeros = jnp.zeros(bias.shape, BF16)
    return jnp.stack(list(_split3(bias)) + [ones] * 3 + [zeros] * (AUG_ROWS - 6), axis=2)


def _fox_keys(qkv, fcol, *, name):
    S = qkv.shape[0]
    tr = _tile(S, 2048, 16)

    def body(k_ref, f_ref, o_ref):
        hp = pl.program_id(1)
        k2 = k_ref[...]
        lanes = lax.broadcasted_iota(jnp.int32, (tr, 128), 1)
        for hh in range(2):
            nf = -jnp.sum(jnp.where(lanes == 2 * hp + hh, f_ref[...], 0.0), axis=1, keepdims=True)
            hi = nf.astype(BF16).astype(F32)
            mid = (nf - hi).astype(BF16).astype(F32)
            lo = nf - hi - mid
            base = HEAD_DIM * (1 - hh)
            aug = jnp.where((lanes >= base) & (lanes < base + 3), 1.0, 0.0)
            for j, piece in enumerate((hi, mid, lo)):
                aug = jnp.where(lanes == base + 3 + j, piece, aug)
            mine = (lanes >= HEAD_DIM * hh) & (lanes < HEAD_DIM * (hh + 1))
            o_ref[hh] = jnp.where(mine, k2, aug.astype(BF16))

    return pl.pallas_call(
        body, name=name, grid=(S // tr, HP),
        in_specs=[pl.BlockSpec((tr, 128), lambda i, hp: (i, HP + hp)), pl.BlockSpec((tr, 128), lambda i, hp: (i, 0))],
        out_specs=pl.BlockSpec((None, 2, tr, 128), lambda i, hp: (hp, 0, i, 0)),
        out_shape=jax.ShapeDtypeStruct((HP, 2, S, 128), BF16), compiler_params=_cp("parallel", "parallel"))(qkv, fcol)


HALF = S5_SUB * S5_P
NCB = 2 * HALF // 128
RE, IM = slice(0, NCB // 2), slice(NCB // 2, NCB)


def _gelu(x):
    return 0.5 * x * (1.0 + jnp.tanh(GELU_K * (x + 0.044715 * x * x * x)))


def _gelu_grad(x):
    t = jnp.tanh(GELU_K * (x + 0.044715 * x * x * x))
    return 0.5 * (1.0 + t) + 0.5 * x * (1.0 - t * t) * GELU_K * (1.0 + 3.0 * 0.044715 * x * x)


def _s5_put(buf, s, val, lt):
    for cb in range(NCB):
        buf[cb, pl.ds(s, lt, stride=S5_SUB), :] = val[:, 128 * cb:128 * (cb + 1)]


def _s5_get(buf, s, lt):
    return jnp.concatenate([buf[cb, pl.ds(s, lt, stride=S5_SUB), :] for cb in range(NCB)], axis=1)


def _s5_project_in(u_ref, wb_ref, buf, lt):
    for s in range(S5_SUB):
        _s5_put(buf, s, jnp.dot(u_ref[:, 128 * s:128 * (s + 1)], wb_ref[s], preferred_element_type=F32), lt)


def _s5_scan(buf, lam_ref, h0, lt):
    a_re, a_im = lam_ref[RE], lam_ref[IM]

    def step(t, carry):
        hr, hi = carry
        r0 = pl.multiple_of(t * S5_SUB, S5_SUB)
        nr = a_re * hr - a_im * hi + buf[RE, pl.ds(r0, S5_SUB), :]
        ni = a_re * hi + a_im * hr + buf[IM, pl.ds(r0, S5_SUB), :]
        buf[RE, pl.ds(r0, S5_SUB), :] = nr
        buf[IM, pl.ds(r0, S5_SUB), :] = ni
        return nr, ni

    return lax.fori_loop(0, lt, step, (h0[RE], h0[IM]), unroll=4)


def _s5_fwd(u, wb, wc, lam, dskip, *, lt, name, comm=()):
    S, D = u.shape
    nt = S // lt

    def body(u_ref, wb_ref, wc_ref, lam_ref, d_ref, yp_ref, yg_ref, st_ref, buf, h_sc):
        @pl.when(pl.program_id(0) == 0)
        def _():
            h_sc[...] = jnp.zeros_like(h_sc)

        st_ref[...] = h_sc[...]
        _s5_project_in(u_ref, wb_ref, buf, lt)
        hr, hi = _s5_scan(buf, lam_ref, h_sc[...], lt)
        h_sc[RE] = hr
        h_sc[IM] = hi
        for s in range(S5_SUB):
            cols = slice(128 * s, 128 * (s + 1))
            hs = _s5_get(buf, s, lt).astype(BF16)
            yp = jnp.dot(hs, wc_ref[s], preferred_element_type=F32) + d_ref[:, cols] * u_ref[:, cols].astype(F32)
            yp_ref[:, cols] = yp.astype(yp_ref.dtype)
            yg_ref[:, cols] = _gelu(yp).astype(yg_ref.dtype)

    blk = pl.BlockSpec((lt, D), lambda i: (i, 0))
    full = lambda shp: pl.BlockSpec(shp, lambda i: (0,) * len(shp))
    state = (NCB, S5_SUB, 128)
    return _call(
        body, name=name, grid=(nt,), comm=comm, sem=("arbitrary",), args=(u, wb, wc, lam, dskip),
        in_specs=[blk, full(wb.shape), full(wc.shape), full(lam.shape), full(dskip.shape)],
        out_specs=[blk, blk, pl.BlockSpec((None,) + state, lambda i: (i, 0, 0, 0))],
        out_shape=[jax.ShapeDtypeStruct((S, D), BF16), jax.ShapeDtypeStruct((S, D), BF16), jax.ShapeDtypeStruct((nt,) + state, F32)],
        scratch_shapes=[pltpu.VMEM((NCB, lt * S5_SUB, 128), F32), pltpu.VMEM(state, F32)])


def _s5_bwd(u, dyp, st, wb, wc, lam, dskip, *, lt, name, comm=()):
    S, D = u.shape
    nt = S // lt

    def body(u_ref, dy_ref, st_ref, wb_ref, wc_ref, lam_ref, d_ref, du_ref, dwb_ref, dwc_ref, dlam_ref, hbuf, gbuf, g_sc):
        @pl.when(pl.program_id(0) == 0)
        def _():
            g_sc[...] = jnp.zeros_like(g_sc)
            dwb_ref[...] = jnp.zeros_like(dwb_ref)
            dwc_ref[...] = jnp.zeros_like(dwc_ref)
            dlam_ref[...] = jnp.zeros_like(dlam_ref)

        _s5_project_in(u_ref, wb_ref, hbuf, lt)
        _s5_scan(hbuf, lam_ref, st_ref[...], lt)
        for s in range(S5_SUB):
            dys = dy_ref[:, 128 * s:128 * (s + 1)]
            _s5_put(gbuf, s, lax.dot_general(dys, wc_ref[s], NT_DIMS, preferred_element_type=F32), lt)
        a_re, a_im = lam_ref[RE], lam_ref[IM]

        def one(t, carry, hp_re, hp_im):
            gr, gi, dar, dai = carry
            r0 = pl.multiple_of(t * S5_SUB, S5_SUB)
            nr = gbuf[RE, pl.ds(r0, S5_SUB), :] + a_re * gr + a_im * gi
            ni = gbuf[IM, pl.ds(r0, S5_SUB), :] + a_re * gi - a_im * gr
            gbuf[RE, pl.ds(r0, S5_SUB), :] = nr
            gbuf[IM, pl.ds(r0, S5_SUB), :] = ni
            return nr, ni, dar + nr * hp_re + ni * hp_im, dai + ni * hp_re - nr * hp_im

        def step(k, carry):
            t = lt - 1 - k
            p0 = pl.multiple_of((t - 1) * S5_SUB, S5_SUB)
            return one(t, carry, hbuf[RE, pl.ds(p0, S5_SUB), :], hbuf[IM, pl.ds(p0, S5_SUB), :])

        init = (g_sc[RE], g_sc[IM], dlam_ref[RE], dlam_ref[IM])
        carry = lax.fori_loop(0, lt - 1, step, init, unroll=2)
        gr, gi, dar, dai = one(0, carry, st_ref[RE], st_ref[IM])
        g_sc[RE] = gr
        g_sc[IM] = gi
        dlam_ref[RE] = dar
        dlam_ref[IM] = dai
        for s in range(S5_SUB):
            cols = slice(128 * s, 128 * (s + 1))
            gs = _s5_get(gbuf, s, lt).astype(BF16)
            hs = _s5_get(hbuf, s, lt).astype(BF16)
            us, dys = u_ref[:, cols], dy_ref[:, cols]
            du = lax.dot_general(gs, wb_ref[s], NT_DIMS, preferred_element_type=F32) + d_ref[:, cols] * dys.astype(F32)
            du_ref[:, cols] = du.astype(du_ref.dtype)
            dwb_ref[s] += lax.dot_general(us, gs, TN_DIMS, preferred_element_type=F32)
            dwc_ref[s] += lax.dot_general(hs, dys, TN_DIMS, preferred_element_type=F32)

    blk = pl.BlockSpec((lt, D), lambda i: (nt - 1 - i, 0))
    full = lambda shp: pl.BlockSpec(shp, lambda i: (0,) * len(shp))
    state = (NCB, S5_SUB, 128)
    words = pltpu.VMEM((NCB, lt * S5_SUB, 128), F32)
    return _call(
        body, name=name, grid=(nt,), comm=comm, sem=("arbitrary",), args=(u, dyp, st, wb, wc, lam, dskip),
        in_specs=[blk, blk, pl.BlockSpec((None,) + state, lambda i: (nt - 1 - i, 0, 0, 0)),
                  full(wb.shape), full(wc.shape), full(lam.shape), full(dskip.shape)],
        out_specs=[blk, full(wb.shape), full(wc.shape), full(lam.shape)],
        out_shape=[jax.ShapeDtypeStruct((S, D), BF16), jax.ShapeDtypeStruct(wb.shape, F32), jax.ShapeDtypeStruct(wc.shape, F32),
                   jax.ShapeDtypeStruct(lam.shape, F32)],
        scratch_shapes=[words, words, pltpu.VMEM(state, F32)])


def _s5_disc(lam_re, lam_im, log_dt, b_re, b_im):
    dt = jnp.exp(log_dt)[:, None]
    mag = jnp.exp(lam_re * dt)
    lb_re, lb_im = mag * jnp.cos(lam_im * dt), mag * jnp.sin(lam_im * dt)
    num_re = lb_re - 1.0
    den = lam_re * lam_re + lam_im * lam_im
    k_re = (num_re * lam_re + lb_im * lam_im) / den
    k_im = (lb_im * lam_re - num_re * lam_im) / den
    bb_re = k_re[..., None] * b_re - k_im[..., None] * b_im
    bb_im = k_re[..., None] * b_im + k_im[..., None] * b_re
    return lb_re, lb_im, bb_re, bb_im


def _s5_pack(lb_re, lb_im, bb_re, bb_im, c_re, c_im):
    eye = jnp.eye(S5_SUB, dtype=F32)
    tb = lambda x: jnp.einsum("sgcp,gh->sgchp", x.reshape(S5_SUB, S5_SUB, S5_P, S5_C).transpose(0, 1, 3, 2), eye).reshape(S5_SUB, 128, HALF)
    tc = lambda x: jnp.einsum("sgpc,gh->sgphc", x.reshape(S5_SUB, S5_SUB, S5_C, S5_P).transpose(0, 1, 3, 2), eye).reshape(S5_SUB, HALF, 128)
    wb = jnp.concatenate([tb(bb_re), tb(bb_im)], axis=2).astype(BF16)
    wc = jnp.concatenate([tc(c_re), -tc(c_im)], axis=1).astype(BF16)
    lam = jnp.concatenate([lb_re.reshape(S5_SUB, HALF), lb_im.reshape(S5_SUB, HALF)], axis=1)
    return wb, wc, lam.reshape(S5_SUB, NCB, 128).transpose(1, 0, 2)


def _s5_unpack(dwb, dwc, dlam):
    db = jnp.einsum("sgcrgp->rsgpc", dwb.reshape(S5_SUB, S5_SUB, S5_C, 2, S5_SUB, S5_P)).reshape(2, S5_G, S5_P, S5_C)
    dc = jnp.einsum("srgpgc->rsgcp", dwc.reshape(S5_SUB, 2, S5_SUB, S5_P, S5_SUB, S5_C)).reshape(2, S5_G, S5_C, S5_P)
    dlam = dlam.transpose(1, 0, 2).reshape(S5_SUB, 2 * HALF)
    return (dlam[:, :HALF].reshape(S5_G, S5_P), dlam[:, HALF:].reshape(S5_G, S5_P), db[0], db[1], dc[0], -dc[1])


def _glu_fwd(yg, gl, *, name):
    S, D = yg.shape
    tm = _tile(S, 512, 8)
    blk = pl.BlockSpec((tm, D), lambda i: (i, 0))

    def body(y_ref, g_ref, o_ref):
        o_ref[...] = (y_ref[...].astype(F32) * jax.nn.sigmoid(g_ref[...].astype(F32))).astype(o_ref.dtype)

    return pl.pallas_call(body, name=name, grid=(S // tm,), in_specs=[blk, blk], out_specs=blk,
                          out_shape=jax.ShapeDtypeStruct((S, D), BF16), compiler_params=_cp("parallel"))(yg, gl)


def _glu_bwd(dy2, yg, gl, *, name):
    S, D = yg.shape
    tm = _tile(S, 512, 8)
    blk = pl.BlockSpec((tm, D), lambda i: (i, 0))

    def body(d_ref, y_ref, g_ref, da_ref, dg_ref):
        d = d_ref[...].astype(F32)
        sg = jax.nn.sigmoid(g_ref[...].astype(F32))
        da_ref[...] = d * sg
        dg_ref[...] = (d * y_ref[...].astype(F32) * sg * (1.0 - sg)).astype(dg_ref.dtype)

    return pl.pallas_call(body, name=name, grid=(S // tm,), in_specs=[blk, blk, blk], out_specs=[blk, blk],
                          out_shape=[jax.ShapeDtypeStruct((S, D), F32), jax.ShapeDtypeStruct((S, D), BF16)],
                          compiler_params=_cp("parallel"))(dy2, yg, gl)


def _gelu_bwd(da, db, ypre, u, *, name):
    S, D = ypre.shape
    tm = _tile(S, 512, 8)
    blk = pl.BlockSpec((tm, D), lambda i: (i, 0))
    vec = pl.BlockSpec((1, D), lambda i: (0, 0))

    def body(a_ref, b_ref, y_ref, u_ref, o_ref, s_ref):
        @pl.when(pl.program_id(0) == 0)
        def _():
            s_ref[...] = jnp.zeros_like(s_ref)

        dy = (a_ref[...] + b_ref[...]) * _gelu_grad(y_ref[...].astype(F32))
        o_ref[...] = dy.astype(o_ref.dtype)
        s_ref[...] += jnp.sum(dy * u_ref[...].astype(F32), axis=0, keepdims=True)

    return pl.pallas_call(body, name=name, grid=(S // tm,), in_specs=[blk, blk, blk, blk], out_specs=[blk, vec],
                          out_shape=[jax.ShapeDtypeStruct((S, D), BF16), jax.ShapeDtypeStruct((1, D), F32)],
                          compiler_params=_cp("arbitrary"))(da, db, ypre, u)


def _my_index():
    return 4 * lax.axis_index("x") + 2 * lax.axis_index("y") + lax.axis_index("c")


def _comm_plan(payloads):
    n = len(payloads)
    hbm = pl.BlockSpec(memory_space=pl.ANY)
    shapes = [jax.ShapeDtypeStruct((N_DEV,) + (x.shape if g else x.shape[1:]), x.dtype) for x, g in payloads]
    sems = [pltpu.SemaphoreType.DMA((n, N_DEV - 1)), pltpu.SemaphoreType.DMA((n, N_DEV - 1)), pltpu.SemaphoreType.DMA((n,))] if n else []
    return [hbm] * n, [hbm] * n, shapes, sems


def _comm_copies(x_refs, o_refs, gathers, send_sems, recv_sems, local_sems):
    mx, my, mc = lax.axis_index("x"), lax.axis_index("y"), lax.axis_index("c")
    me = 4 * mx + 2 * my + mc
    copies = []
    for i, (x_ref, o_ref, gather) in enumerate(zip(x_refs, o_refs, gathers)):
        src = (lambda j, r=x_ref: r) if gather else (lambda j, r=x_ref: r.at[j])
        copies.append(pltpu.make_async_copy(src(me), o_ref.at[me], local_sems.at[i]))
        for k in range(1, N_DEV):
            px, py, pc = mx ^ (k >> 2), my ^ ((k >> 1) & 1), mc ^ (k & 1)
            copies.append(pltpu.make_async_remote_copy(src(4 * px + 2 * py + pc), o_ref.at[me], send_sems.at[i, k - 1],
                                                       recv_sems.at[i, k - 1], device_id=(px, py, pc), device_id_type=MESH_ID))
    return copies


def _call(body, *, name, grid, in_specs, out_specs, out_shape, scratch_shapes, sem, args, comm=()):
    if not comm:
        outs = pl.pallas_call(body, name=name, grid=grid, in_specs=in_specs, out_specs=out_specs, out_shape=out_shape,
                              scratch_shapes=scratch_shapes, compiler_params=_cp(*sem))(*args)
        return outs, ()
    n, n_in, n_out = len(comm), len(in_specs), len(out_specs)
    c_in, c_out, c_shapes, c_sems = _comm_plan(comm)
    gathers = [g for _, g in comm]

    def wrapped(*refs):
        own_in, x_refs = refs[:n_in], refs[n_in:n_in + n]
        own_out, o_refs = refs[n_in + n:n_in + n + n_out], refs[n_in + n + n_out:n_in + 2 * n + n_out]
        own_scratch, sems = refs[n_in + 2 * n + n_out:-3], refs[-3:]
        ids = [pl.program_id(a) for a in range(len(grid))]
        first = functools.reduce(jnp.logical_and, [i == 0 for i in ids])
        last = functools.reduce(jnp.logical_and, [i == g - 1 for i, g in zip(ids, grid)])

        @pl.when(first)
        def _():
            for cp in _comm_copies(x_refs, o_refs, gathers, *sems):
                cp.start()

        body(*own_in, *own_out, *own_scratch)

        @pl.when(last)
        def _():
            for cp in _comm_copies(x_refs, o_refs, gathers, *sems):
                cp.wait()

    outs = pl.pallas_call(wrapped, name=name, grid=grid, in_specs=list(in_specs) + c_in, out_specs=list(out_specs) + c_out,
                          out_shape=list(out_shape) + c_shapes, scratch_shapes=list(scratch_shapes) + c_sems,
                          compiler_params=_cp(*["arbitrary"] * len(grid)))(*args, *[x for x, _ in comm])
    return outs[:n_out], outs[n_out:]


def _exchange_many(payloads, *, name):
    n = len(payloads)
    in_specs, out_specs, shapes, sems = _comm_plan(payloads)
    gathers = [g for _, g in payloads]

    def body(*refs):
        copies = _comm_copies(refs[:n], refs[n:2 * n], gathers, *refs[2 * n:])
        for cp in copies:
            cp.start()
        for cp in copies:
            cp.wait()

    return pl.pallas_call(body, name=name, in_specs=in_specs, out_specs=out_specs, out_shape=shapes, scratch_shapes=sems,
                          compiler_params=pltpu.CompilerParams(has_side_effects=True))(*[x for x, _ in payloads])


def _exchange(x, *, gather, name):
    return _exchange_many([(x, gather)], name=name)[0]


def _sum8(x8, *, name):
    _, R, C = x8.shape
    tr = _tile(R, 256, 16)

    def body(x_ref, o_ref):
        acc = x_ref[0].astype(F32)
        for j in range(1, N_DEV):
            acc = acc + x_ref[j].astype(F32)
        o_ref[...] = acc

    return pl.pallas_call(body, name=name, grid=(R // tr,), in_specs=[pl.BlockSpec((N_DEV, tr, C), lambda i: (0, i, 0))],
                          out_specs=pl.BlockSpec((tr, C), lambda i: (i, 0)), out_shape=jax.ShapeDtypeStruct((R, C), F32),
                          compiler_params=_cp("parallel"))(x8)


def _pack(arrs, dtype, row_mult):
    flat = jnp.concatenate([a.reshape(-1).astype(dtype) for a in arrs])
    n = flat.shape[0]
    rows = -(-n // 1024)
    rows = -(-rows // row_mult) * row_mult
    return jnp.pad(flat, (0, rows * 1024 - n)).reshape(rows, 1024)


def _unpack(buf, shapes):
    lead = buf.shape[:-2]
    flat = buf.reshape(lead + (-1,))
    out, off = [], 0
    for s in shapes:
        n = math.prod(s)
        out.append(flat[..., off:off + n].reshape(lead + tuple(s)))
        off += n
    return out


def _adaln_fwd(c_all, w, b, *, name):
    nk, D, n = w.shape

    def body(c_ref, w_ref, b_ref, o_ref):
        cv = c_ref[...]
        sc = (cv * jax.nn.sigmoid(cv)).astype(BF16)
        o_ref[...] = jnp.dot(sc, w_ref[...].astype(BF16), preferred_element_type=F32) + b_ref[...]

    return pl.pallas_call(body, name=name, grid=(nk,),
                          in_specs=[pl.BlockSpec((N_DEV, D), lambda k: (0, 0)), pl.BlockSpec((None, D, n), lambda k: (k, 0, 0)),
                                    pl.BlockSpec((None, 1, n), lambda k: (k, 0, 0))],
                          out_specs=pl.BlockSpec((None, N_DEV, n), lambda k: (k, 0, 0)),
                          out_shape=jax.ShapeDtypeStruct((nk, N_DEV, n), F32), compiler_params=_cp("parallel"))(c_all, w, b)


def _adaln_bwd(c_all, dmod, *, name):
    nk, _, n = dmod.shape
    D = c_all.shape[1]

    def body(c_ref, d_ref, dw_ref, db_ref):
        cv = c_ref[...]
        sc = (cv * jax.nn.sigmoid(cv)).astype(BF16).astype(F32)
        dm = d_ref[...]
        dw_ref[...] = lax.dot_general(sc, dm.astype(BF16).astype(F32), TN_DIMS, precision=lax.Precision.HIGHEST,
                                      preferred_element_type=F32)
        db_ref[...] = jnp.sum(dm, axis=0, keepdims=True)

    return pl.pallas_call(body, name=name, grid=(nk,),
                          in_specs=[pl.BlockSpec((N_DEV, D), lambda k: (0, 0)), pl.BlockSpec((None, N_DEV, n), lambda k: (k, 0, 0))],
                          out_specs=[pl.BlockSpec((None, D, n), lambda k: (k, 0, 0)), pl.BlockSpec((None, 1, n), lambda k: (k, 0, 0))],
                          out_shape=[jax.ShapeDtypeStruct((nk, D, n), F32), jax.ShapeDtypeStruct((nk, 1, n), F32)],
                          compiler_params=_cp("parallel"))(c_all, dmod)


def _adamw(w, g, m, v, *, name):
    R, C = w.shape
    tr = _tile(R, 256, 8)
    blk = pl.BlockSpec((tr, C), lambda i: (i, 0))
    c1 = 1.0 / (1.0 - ADAM_B1 ** ADAM_STEP)
    c2 = 1.0 / (1.0 - ADAM_B2 ** ADAM_STEP)

    def body(w_ref, g_ref, m_ref, v_ref, d_ref, nm_ref, nv_ref):
        gv = g_ref[...]
        nm = ADAM_B1 * m_ref[...] + (1.0 - ADAM_B1) * gv
        nv = ADAM_B2 * v_ref[...] + (1.0 - ADAM_B2) * (gv * gv)
        nm_ref[...] = nm
        nv_ref[...] = nv
        d_ref[...] = -ADAM_LR * ((nm * c1) / (jnp.sqrt(nv * c2) + ADAM_EPS) + ADAM_WD * w_ref[...])

    sh = jax.ShapeDtypeStruct((R, C), F32)
    return pl.pallas_call(body, name=name, grid=(R // tr,), in_specs=[blk] * 4, out_specs=[blk] * 3, out_shape=[sh] * 3,
                          compiler_params=_cp("parallel"))(w, g, m, v)


def _adamw_nd(w, g, m, v, *, name):
    shp = w.shape
    two = (-1, shp[-1]) if w.ndim > 1 else (1, -1)
    outs = _adamw(w.reshape(two), g.reshape(two), m.reshape(two), v.reshape(two), name=name)
    return [o.reshape(shp) for o in outs]


S5_TILE = 256
ATTN_BLOCK = 512
BIG = ("s5_w_in", "s5_w_glu", "s5_w_out", "fox_w_in", "fox_w_out", "ffn_w_up", "ffn_w_down")
WEIGHTS = ("norm_g", "ada_w", "ada_b", "s5_w_in", "s5_lam_re", "s5_lam_im", "s5_log_dt", "s5_b_re", "s5_b_im", "s5_c_re",
           "s5_c_im", "s5_d", "s5_w_glu", "s5_w_out", "fox_w_in", "fox_b_f", "fox_w_out", "ffn_w_up", "ffn_conv_w",
           "ffn_conv_b", "ffn_w_down", "final_g")
REPLICATED = ("s5_lam_re", "s5_lam_im", "s5_log_dt", "s5_b_re", "s5_b_im", "s5_c_re", "s5_c_im", "s5_d", "fox_b_f",
              "ffn_conv_b", "final_g")


S5_MATS = ("s5_w_in", "s5_w_glu", "s5_w_out")


def _row_group(arrs):
    return jnp.concatenate([a.reshape(-1, D_MODEL) for a in arrs], axis=0)


def _cols_join(g):
    return jnp.concatenate([g[j] for j in range(N_DEV)], axis=1)


def _cols_split(full):
    n = full.shape[1] // N_DEV
    return jnp.stack([full[:, j * n:(j + 1) * n] for j in range(N_DEV)])


def kernel(x, c, norm_g, ada_w, ada_b, s5_w_in, s5_lam_re, s5_lam_im, s5_log_dt, s5_b_re, s5_b_im, s5_c_re, s5_c_im, s5_d, s5_w_glu, s5_w_out, fox_w_in, fox_b_f, fox_w_out, ffn_w_up, ffn_conv_w, ffn_conv_b, ffn_w_down, final_g, loss_target, m_norm_g, m_ada_w, m_ada_b, m_s5_w_in, m_s5_lam_re, m_s5_lam_im, m_s5_log_dt, m_s5_b_re, m_s5_b_im, m_s5_c_re, m_s5_c_im, m_s5_d, m_s5_w_glu, m_s5_w_out, m_fox_w_in, m_fox_b_f, m_fox_w_out, m_ffn_w_up, m_ffn_conv_w, m_ffn_conv_b, m_ffn_w_down, m_final_g, v_norm_g, v_ada_w, v_ada_b, v_s5_w_in, v_s5_lam_re, v_s5_lam_im, v_s5_log_dt, v_s5_b_re, v_s5_b_im, v_s5_c_re, v_s5_c_im, v_s5_d, v_s5_w_glu, v_s5_w_out, v_fox_w_in, v_fox_b_f, v_fox_w_out, v_ffn_w_up, v_ffn_conv_w, v_ffn_conv_b, v_ffn_w_down, v_final_g):
    args = dict(locals())
    W = {n: args[n] for n in WEIGHTS}
    M = {n: args["m_" + n] for n in WEIGHTS}
    V = {n: args["v_" + n] for n in WEIGHTS}
    D, F = D_MODEL, D_FF
    me = _my_index()
    h0 = x[0]
    S = h0.shape[0]
    lt = min(S5_TILE, S)
    tb = min(ATTN_BLOCK, S)
    n_g, n_cw = norm_g.size, ffn_conv_w.size

    g0 = _exchange(_pack([c, norm_g, ffn_conv_w], F32, 8), gather=True, name="gather_small_in")
    c_all, ng_all, cw_all = _unpack(g0, [(D,), norm_g.shape, ffn_conv_w.shape])
    ng_full = ng_all.transpose(1, 2, 0, 3).reshape(2, 2, D)
    cw_full = cw_all.transpose(1, 2, 0, 3).reshape(2, 3, F)

    ncol = ada_w.shape[-1]
    modp = _adaln_fwd(c_all, ada_w.reshape(4, D, ncol), ada_b.reshape(4, 1, ncol), name="adaln_fwd")
    g1 = _exchange(_pack([modp], F32, 8), gather=True, name="gather_adaln")
    (mod_all,) = _unpack(g1, [modp.shape])
    mod = lax.dynamic_index_in_dim(mod_all, me, axis=2, keepdims=False).transpose(1, 0, 2).reshape(4, 3 * D)
    shift = [mod[k:k + 1, :D] for k in range(4)]
    scale = [mod[k:k + 1, D:2 * D] for k in range(4)]
    gate = [mod[k:k + 1, 2 * D:] for k in range(4)]
    gain = [_row(ng_full[k // 2, k % 2]) for k in range(4)]

    b16 = lambda a: a.astype(BF16)
    rows = D // N_DEV
    full = {"s5_w_in": _exchange(b16(s5_w_in[0]), gather=True, name="gather_s5_w_in").reshape(D, D)}
    bf_pad = jnp.pad(fox_b_f, ((0, 0), (0, 128 - HEADS)))

    def ffn_fwd(h, k, layer, comm=()):
        hn = _modulate(h, gain[k], shift[k], scale[k], name=f"modulate{k}")
        up = _mm(hn, full[f"ffn_w_up{layer}"], name=f"ffn_up{layer}", o_h=True, out_dtype=BF16, tn=1408, comm=comm)
        up, exchanged = up if comm else (up, ())
        z = _conv_gate_fwd(up, cw_full[layer], ffn_conv_b[layer:layer + 1], name=f"conv_gate{layer}")
        m, h_out = _mm(z, full[f"ffn_w_down{layer}"], name=f"ffn_down{layer}", res=h, gate=gate[k])
        return h_out, (hn, up, z, m), exchanged

    lb_re, lb_im, bb_re, bb_im = _s5_disc(s5_lam_re[0], s5_lam_im[0], s5_log_dt[0], s5_b_re[0], s5_b_im[0])
    wb, wc, lam = _s5_pack(lb_re, lb_im, bb_re, bb_im, s5_c_re[0], s5_c_im[0])
    hn0 = _modulate(h0, gain[0], shift[0], scale[0], name="modulate0")
    u = _mm(hn0, full["s5_w_in"], name="s5_in", out_dtype=BF16)
    (ypre, yg, st), (g_up0, g_rows0) = _s5_fwd(
        u, wb, wc, lam, s5_d, lt=lt, name="s5_scan",
        comm=[(b16(ffn_w_up[0]), True), (_row_group([b16(s5_w_glu), b16(s5_w_out), b16(ffn_w_down[0])]), True)])
    full["ffn_w_up0"] = _cols_join(g_up0)
    full["s5_w_glu"] = g_rows0[:, :rows].reshape(D, D)
    full["s5_w_out"] = g_rows0[:, rows:2 * rows].reshape(D, D)
    full["ffn_w_down0"] = g_rows0[:, 2 * rows:].reshape(F, D)
    gl = _mm(yg, full["s5_w_glu"], name="s5_glu", out_dtype=BF16)
    y2 = _glu_fwd(yg, gl, name="s5_glu_gate")
    m0, h1 = _mm(y2, full["s5_w_out"], name="s5_out", res=h0, gate=gate[0])
    h2, (hn1, up0, z0, m1), (g_fox, g_fox_out) = ffn_fwd(h1, 1, 0, comm=[(b16(fox_w_in[0]), True), (b16(fox_w_out[0]), True)])
    full["fox_w_in"] = _cols_join(g_fox)
    full["fox_w_out"] = g_fox_out.reshape(D, D)
    w_proj = jnp.pad(full["fox_w_in"], ((0, 0), (0, 3 * D + 128 - full["fox_w_in"].shape[1])))
    w_qkv, w_f = w_proj[:, :3 * D], w_proj[:, 3 * D:]

    hn2 = _modulate(h2, gain[2], shift[2], scale[2], name="modulate2")
    nb = S // tb
    w_qkv_s = jnp.concatenate([w_qkv[:, :D] * Q_SCALE, w_qkv[:, D:]], axis=1)
    qkv, qkvT = _mm(hn2, w_qkv_s, name="fox_qkv", out_dtype=BF16, with_t=True)
    fl = _mm(hn2, w_f, name="fox_f", out_dtype=F32)
    fcol = _fgate_fwd(fl, bf_pad, name="fox_fgate")
    f_heads = fcol[:, :HEADS].T.reshape(HP, 2, S)
    ka = _fox_keys(qkv, fcol, name="fox_keys")
    vtb = _blocked(qkvT[2 * D:].reshape(HP, 128, S), nb, tb)
    aug = _aug_rows(f_heads)
    (oT, lse), (g_up1, g_down1) = _foxt_fwd(qkvT, ka, vtb, aug, tb=tb, name="fox_attn",
                                            comm=[(b16(ffn_w_up[1]), True), (b16(ffn_w_down[1]), True)])
    full["ffn_w_up1"] = _cols_join(g_up1)
    full["ffn_w_down1"] = g_down1.reshape(F, D)
    m2, h3 = _mm(oT, full["fox_w_out"], name="fox_out", ta=True, res=h2, gate=gate[2])
    h4, (hn3, up1, z1, m3), _ = ffn_fwd(h3, 3, 1)

    lblk, dh, head_sums, dm = _loss_head(h4, _row(final_g), loss_target[0], m3, gate[3], name="loss_head")
    d_final_g, dgate = head_sums[0:1], head_sums[1:2]
    loss = lax.psum(lblk[0, 0], ("x", "y", "c"))

    G = {}
    dmod = [None] * 4
    mixer_out = [m0, m1, m2, m3]

    def norm_bwd(h, dhn, dh_in, k, dgate_k):
        if k == 0:
            dh_out, sums = _modulate_bwd(h, dhn, dh_in, gain[k], scale[k], name=f"modulate_bwd{k}")
            dm_below = dgate_below = None
        else:
            dh_out, sums, dm_below = _modulate_bwd(h, dhn, dh_in, gain[k], scale[k], mixer_out[k - 1], gate[k - 1],
                                                   name=f"modulate_bwd{k}")
            dgate_below = sums[3:4]
        dmod[k] = jnp.concatenate([sums[0:1], sums[1:2], dgate_k], axis=1)
        return dh_out, sums[2], dm_below, dgate_below

    def ffn_bwd(dh_in, dm, dgate_k, h, k, layer, saved):
        hn, up, z, m = saved
        dz = _mm(dm, full[f"ffn_w_down{layer}"], name=f"ffn_down_dx{layer}", tb=True, out_dtype=BF16, tn=1408)
        dw_down = _mm(z, dm, name=f"ffn_down_dw{layer}", ta=True, tm=1408, tn=1024, tk=2048)
        d_up, cs = _conv_gate_bwd(up, dz, cw_full[layer], ffn_conv_b[layer:layer + 1], name=f"conv_gate_bwd{layer}")
        dhn = _mm(d_up, full[f"ffn_w_up{layer}"], name=f"ffn_up_dx{layer}", tb=True, a_h=True, tk=2816)
        dw_up = _mm(hn, d_up, name=f"ffn_up_dw{layer}", ta=True, b_h=True, tm=1024, tn=1408, tk=2048)
        dh_out, dg, dm_below, dgate_below = norm_bwd(h, dhn, dh_in, k, dgate_k)
        return dh_out, dm_below, dgate_below, dg, dw_up, dw_down, cs[0:3], cs[3]

    dh, dm, dgate, dg3, dw_up1, dw_down1, dcw1, dcb1 = ffn_bwd(dh, dm, dgate, h3, 3, 1, (hn3, up1, z1, m3))

    do, doT = _mm(dm, full["fox_w_out"], name="fox_out_dx", tb=True, out_dtype=BF16, with_t=True)
    dw_fox_out = _mm(oT, dm, name="fox_out_dw", tn=1024, tk=2048)
    to_rows = lambda g: b16(g).reshape(N_DEV, -1, D)
    (dqtb, dk, dv, dfk, dfq), (r_up1, r_down1) = _foxt_bwd(
        qkvT, qkv, ka, _blocked(qkvT[:D].reshape(HP, 128, S), nb, tb), _blocked(doT.reshape(HP, 128, S), nb, tb), do,
        _blocked(aug, nb, tb), _blocked(lse, nb, tb), _blocked(oT.reshape(HP, 128, S), nb, tb), tb=tb, name="fox_attn_bwd",
        comm=[(_cols_split(b16(dw_up1)), False), (to_rows(dw_down1), False)])
    dq = dqtb.transpose(1, 3, 0, 2).reshape(S, D)
    dF = dfk[:, :, :2].transpose(1, 0, 2).reshape(S, HEADS) + dfq[:, :, :2, :].transpose(1, 3, 0, 2).reshape(S, HEADS)
    dF = jnp.pad(dF, ((0, 0), (0, 128 - HEADS)))
    dfl, dbf = _fgate_bwd(dF, fl, bf_pad, name="fox_fgate_bwd")
    dproj = jnp.concatenate([dq, dk, dv, dfl.astype(BF16)], axis=1)
    dhn = _mm(dproj, w_proj, name="fox_in_dx", tb=True)
    dw_proj = _mm(hn2, dproj, name="fox_in_dw", ta=True, tn=640, tk=2048)
    dw_fox_in = dw_proj[:, :full["fox_w_in"].shape[1]]
    dh, dg2, dm, dgate = norm_bwd(h2, dhn, dh, 2, dgate)

    dh, dm, dgate, dg1, dw_up0, dw_down0, dcw0, dcb0 = ffn_bwd(dh, dm, dgate, h1, 1, 0, (hn1, up0, z0, m1))

    dy2 = _mm(dm, full["s5_w_out"], name="s5_out_dx", tb=True, out_dtype=BF16)
    G["s5_w_out"] = _mm(y2, dm, name="s5_out_dw", ta=True, tn=1024, tk=2048)
    da, dgl = _glu_bwd(dy2, yg, gl, name="s5_glu_bwd")
    dyg_b = _mm(dgl, full["s5_w_glu"], name="s5_glu_dx", tb=True)
    G["s5_w_glu"] = _mm(yg, dgl, name="s5_glu_dw", ta=True, tn=1024, tk=2048)
    dyp, dd = _gelu_bwd(da, dyg_b, ypre, u, name="s5_gelu_bwd")
    (du, dwb, dwc, dlam), (r_up0, r_rows0, r_fox) = _s5_bwd(
        u, dyp, st, wb, wc, lam, s5_d, lt=lt, name="s5_scan_bwd",
        comm=[(_cols_split(b16(dw_up0)), False), (jnp.concatenate([to_rows(dw_fox_out), to_rows(dw_down0)], axis=1), False),
              (_cols_split(b16(dw_fox_in)), False)])
    dhn = _mm(du, full["s5_w_in"], name="s5_in_dx", tb=True)
    G["s5_w_in"] = _mm(hn0, du, name="s5_in_dw", ta=True, tn=1024, tk=2048)
    dh, dg0, _, _ = norm_bwd(h0, dhn, dh, 0, dgate)
    grad_x = dh[None]

    dlb_re, dlb_im, dbb_re, dbb_im, dc_re, dc_im = _s5_unpack(dwb, dwc, dlam)
    _, disc_vjp = jax.vjp(_s5_disc, s5_lam_re[0], s5_lam_im[0], s5_log_dt[0], s5_b_re[0], s5_b_im[0])
    dlam_re, dlam_im, dlog_dt, db_re, db_im = disc_vjp((dlb_re, dlb_im, dbb_re, dbb_im))

    g2 = _exchange(_pack([jnp.concatenate(dmod, axis=0)], F32, 8), gather=True, name="gather_dmod")
    (dmod_all,) = _unpack(g2, [(4, 3 * D)])
    dmod_mine = _columns_of_mod(dmod_all, me, ncol)
    d_ada_w, d_ada_b = _adaln_bwd(c_all, dmod_mine, name="adaln_bwd")
    G["ada_w"] = d_ada_w.reshape(ada_w.shape)
    G["ada_b"] = d_ada_b.reshape(ada_b.shape)

    small_full = {
        "norm_g": jnp.stack([dg0, dg1, dg2, dg3]).reshape(2, 2, D),
        "ffn_conv_w": jnp.stack([dcw0, dcw1]),
        "s5_lam_re": dlam_re[None], "s5_lam_im": dlam_im[None], "s5_log_dt": dlog_dt[None],
        "s5_b_re": db_re[None], "s5_b_im": db_im[None], "s5_c_re": dc_re[None], "s5_c_im": dc_im[None],
        "s5_d": dd, "fox_b_f": dbf[:, :HEADS], "ffn_conv_b": jnp.stack([dcb0, dcb1]), "final_g": d_final_g[0],
    }
    names = tuple(small_full)
    small_pack = _pack([small_full[n] for n in names], F32, 8 * N_DEV)
    r_s5, r_small = _exchange_many([(jnp.concatenate([to_rows(G[n]) for n in S5_MATS], axis=1), False),
                                    (small_pack.reshape(N_DEV, -1, 1024), False)], name="scatter_s5_and_small_grads")
    s5_sum = _sum8(r_s5, name="sum_s5_grads")
    for i, n in enumerate(S5_MATS):
        G[n] = s5_sum[i * rows:(i + 1) * rows][None]
    rows0_sum = _sum8(r_rows0, name="sum_rows0_grads")
    G["fox_w_out"] = rows0_sum[:rows][None]
    G["ffn_w_down"] = jnp.stack([rows0_sum[rows:], _sum8(r_down1, name="sum_down1_grads")])
    G["ffn_w_up"] = jnp.stack([_sum8(r_up0, name="sum_up0_grads"), _sum8(r_up1, name="sum_up1_grads")])
    G["fox_w_in"] = _sum8(r_fox, name="sum_fox_w_in")[None]

    g3 = _exchange(_sum8(r_small, name="sum_small_grads"), gather=True, name="gather_small_sums")
    summed = dict(zip(names, _unpack(g3.reshape(small_pack.shape), [small_full[n].shape for n in names])))
    G["norm_g"] = lax.dynamic_slice_in_dim(summed["norm_g"], me * norm_g.shape[-1], norm_g.shape[-1], axis=2)
    G["ffn_conv_w"] = lax.dynamic_slice_in_dim(summed["ffn_conv_w"], me * ffn_conv_w.shape[-1], ffn_conv_w.shape[-1], axis=2)
    for n in REPLICATED:
        G[n] = summed[n]

    delta, new_m, new_v = {}, {}, {}
    small = tuple(n for n in WEIGHTS if n not in BIG and n != "ada_w")
    for n in WEIGHTS:
        if n not in small:
            delta[n], new_m[n], new_v[n] = _adamw_nd(W[n], G[n], M[n], V[n], name=f"adamw_{n}")
    packed = [_pack([src[n] for n in small], F32, 8) for src in (W, G, M, V)]
    for dst, buf in zip((delta, new_m, new_v), _adamw(*packed, name="adamw_small")):
        dst.update(zip(small, _unpack(buf, [W[n].shape for n in small])))

    return (loss, grad_x, *[G[n] for n in WEIGHTS], *[delta[n] for n in WEIGHTS], *[new_m[n] for n in WEIGHTS],
            *[new_v[n] for n in WEIGHTS])


def _columns_of_mod(dmod_all, me, ncol):
    flat = lax.dynamic_slice_in_dim(dmod_all, me * ncol, ncol, axis=2)
    return flat.transpose(1, 0, 2)
```

```python
import functools
import math

import jax
import jax.numpy as jnp
from jax import lax
from jax.experimental import pallas as pl
from jax.experimental.pallas import tpu as pltpu

F32, BF16 = jnp.float32, jnp.bfloat16
EPS = 1e-6
N_DEV = 8
D_MODEL = 1024
D_FF = 2816
HEADS = 16
HEAD_DIM = 64
S5_G, S5_P, S5_C = 64, 64, 16
S5_SUB = 8
V7X_VMEM_LIMIT = 56 * 1024 * 1024
NEG = -1e30
ADAM_LR, ADAM_B1, ADAM_B2, ADAM_EPS, ADAM_WD, ADAM_STEP = 1e-3, 0.9, 0.999, 1e-8, 0.01, 10
GELU_K = math.sqrt(2.0 / math.pi)
MESH_ID = pl.DeviceIdType.MESH


def _cp(*sem):
    return pltpu.CompilerParams(dimension_semantics=sem, vmem_limit_bytes=V7X_VMEM_LIMIT)


def _tile(n, target, mult=128):
    if n <= target:
        return n
    t = (target // mult) * mult
    while t >= mult:
        if n % t == 0:
            return t
        t -= mult
    return n


def _row(v):
    return v.reshape(1, -1).astype(F32)


def _mm(a, b, *, name, ta=False, tb=False, out_dtype=F32, tm=1024, tn=512, tk=None, res=None, gate=None,
        a_h=False, b_h=False, o_h=False, with_t=False, comm=()):
    if a_h:
        M, K = a.shape[1], 2 * a.shape[2]
    elif ta:
        K, M = a.shape
    else:
        M, K = a.shape
    if b_h:
        N = 2 * b.shape[2]
    else:
        N = b.shape[0] if tb else b.shape[1]
    half_n = N // 2
    tm = _tile(M, tm, 128 if ta else 8)
    tn = _tile(half_n if (b_h or o_h) else N, tn)
    tk = K if tk is None else _tile(K // 2 if a_h else K, tk)
    nk = K // tk
    nkh, nnh = (K // 2) // tk if a_h else 1, half_n // tn
    if a_h:
        a_spec = pl.BlockSpec((None, tm, tk), lambda i, j, k: (k // nkh, i, k % nkh))
    elif ta:
        a_spec = pl.BlockSpec((tk, tm), lambda i, j, k: (k, i))
    else:
        a_spec = pl.BlockSpec((tm, tk), lambda i, j, k: (i, k))
    if b_h:
        b_spec = pl.BlockSpec((None, tk, tn), lambda i, j, k: (j // nnh, k, j % nnh))
    elif tb:
        b_spec = pl.BlockSpec((tn, tk), lambda i, j, k: (j, k))
    else:
        b_spec = pl.BlockSpec((tk, tn), lambda i, j, k: (k, j))
    if o_h:
        o_spec = pl.BlockSpec((None, tm, tn), lambda i, j, k: (j // nnh, i, j % nnh))
    else:
        o_spec = pl.BlockSpec((tm, tn), lambda i, j, k: (i, j))
    dn = (((0 if ta else 1,), (1 if tb else 0,)), ((), ()))
    fused = res is not None

    def body(*refs):
        if fused:
            a_ref, b_ref, r_ref, g_ref, m_ref, o_ref, acc_ref = refs
        elif with_t:
            a_ref, b_ref, o_ref, t_ref, acc_ref = refs
        else:
            a_ref, b_ref, o_ref, acc_ref = refs
        p = lax.dot_general(a_ref[...].astype(BF16), b_ref[...].astype(BF16), dn, preferred_element_type=F32)

        def finish(acc):
            if fused:
                m_ref[...] = acc.astype(m_ref.dtype)
                o_ref[...] = r_ref[...] + g_ref[...] * acc
            else:
                o_ref[...] = acc.astype(o_ref.dtype)
                if with_t:
                    t_ref[...] = acc.T.astype(t_ref.dtype)

        if nk == 1:
            finish(p)
        else:
            k = pl.program_id(2)

            @pl.when(k == 0)
            def _():
                acc_ref[...] = p

            @pl.when(k > 0)
            def _():
                acc_ref[...] += p

            @pl.when(k == nk - 1)
            def _():
                finish(acc_ref[...])

    in_specs = [a_spec, b_spec]
    args = [a, b]
    if fused:
        in_specs += [o_spec, pl.BlockSpec((1, tn), lambda i, j, k: (0, j))]
        args += [res, gate]
        out_shape = [jax.ShapeDtypeStruct((M, N), BF16), jax.ShapeDtypeStruct((M, N), F32)]
        out_specs = [o_spec, o_spec]
    else:
        out_shape = [jax.ShapeDtypeStruct((2, M, half_n) if o_h else (M, N), out_dtype)]
        out_specs = [o_spec]
        if with_t:
            out_shape.append(jax.ShapeDtypeStruct((N, M), out_dtype))
            out_specs.append(pl.BlockSpec((tn, tm), lambda i, j, k: (j, i)))
    outs, exchanged = _call(
        body, name=name, grid=(M // tm, N // tn, nk), in_specs=in_specs, out_specs=out_specs, out_shape=out_shape,
        scratch_shapes=[pltpu.VMEM((tm, tn) if nk > 1 else (8, 128), F32)], sem=("parallel", "parallel", "arbitrary"),
        args=args, comm=comm)
    outs = tuple(outs) if (fused or with_t) else outs[0]
    return (outs, exchanged) if comm else outs


def _modulate(h, g, shift, scale, *, name):
    S, D = h.shape
    tm = _tile(S, 512, 8)
    vec = pl.BlockSpec((1, D), lambda i: (0, 0))
    blk = pl.BlockSpec((tm, D), lambda i: (i, 0))

    def body(h_ref, g_ref, sh_ref, sc_ref, o_ref):
        x = h_ref[...]
        r = lax.rsqrt(jnp.mean(x * x, axis=-1, keepdims=True) + EPS)
        o_ref[...] = ((x * r * g_ref[...]) * (1.0 + sc_ref[...]) + sh_ref[...]).astype(o_ref.dtype)

    return pl.pallas_call(body, name=name, grid=(S // tm,), in_specs=[blk, vec, vec, vec], out_specs=blk,
                          out_shape=jax.ShapeDtypeStruct((S, D), BF16), compiler_params=_cp("parallel"))(h, g, shift, scale)


def _modulate_bwd(h, dhn, dh_in, g, scale, m_prev=None, gate_prev=None, *, name):
    S, D = h.shape
    tm = _tile(S, 512, 8)
    vec = pl.BlockSpec((1, D), lambda i: (0, 0))
    blk = pl.BlockSpec((tm, D), lambda i: (i, 0))
    sums = pl.BlockSpec((8, D), lambda i: (0, 0))
    below = m_prev is not None

    def body(h_ref, dhn_ref, dhi_ref, g_ref, sc_ref, *rest):
        dh_ref, s_ref = rest[-3:-1] if below else rest[-2:]

        @pl.when(pl.program_id(0) == 0)
        def _():
            s_ref[...] = jnp.zeros_like(s_ref)

        x = h_ref[...]
        r = lax.rsqrt(jnp.mean(x * x, axis=-1, keepdims=True) + EPS)
        xhat = x * r
        gv = g_ref[...]
        dhn_v = dhn_ref[...].astype(F32)
        dn = dhn_v * (1.0 + sc_ref[...])
        s_ref[0:1, :] += jnp.sum(dhn_v, axis=0, keepdims=True)
        s_ref[1:2, :] += jnp.sum(dhn_v * (xhat * gv), axis=0, keepdims=True)
        s_ref[2:3, :] += jnp.sum(dn * xhat, axis=0, keepdims=True)
        dxh = dn * gv
        dh = dhi_ref[...] + r * (dxh - xhat * jnp.mean(dxh * xhat, axis=-1, keepdims=True))
        dh_ref[...] = dh
        if below:
            mp_ref, gp_ref, dm_ref = rest[0], rest[1], rest[-1]
            dm_ref[...] = (dh * gp_ref[...]).astype(dm_ref.dtype)
            s_ref[3:4, :] += jnp.sum(dh * mp_ref[...].astype(F32), axis=0, keepdims=True)

    extra_in, extra_args = ([blk, vec], [m_prev, gate_prev]) if below else ([], [])
    return pl.pallas_call(body, name=name, grid=(S // tm,), in_specs=[blk, blk, blk, vec, vec] + extra_in,
                          out_specs=[blk, sums] + ([blk] if below else []),
                          out_shape=[jax.ShapeDtypeStruct((S, D), F32), jax.ShapeDtypeStruct((8, D), F32)]
                          + ([jax.ShapeDtypeStruct((S, D), BF16)] if below else []),
                          compiler_params=_cp("arbitrary"))(h, dhn, dh_in, g, scale, *extra_args)


def _loss_head(h, g, tgt, m_prev, gate_prev, *, name):
    S, D = h.shape
    tm = _tile(S, 512, 8)
    vec = pl.BlockSpec((1, D), lambda i: (0, 0))
    blk = pl.BlockSpec((tm, D), lambda i: (i, 0))
    lss = pl.BlockSpec((8, 128), lambda i: (0, 0))
    sums = pl.BlockSpec((8, D), lambda i: (0, 0))

    def body(h_ref, g_ref, t_ref, mp_ref, gp_ref, l_ref, dh_ref, s_ref, dm_ref):
        @pl.when(pl.program_id(0) == 0)
        def _():
            l_ref[...] = jnp.zeros_like(l_ref)
            s_ref[...] = jnp.zeros_like(s_ref)

        x = h_ref[...]
        r = lax.rsqrt(jnp.mean(x * x, axis=-1, keepdims=True) + EPS)
        xhat = x * r
        gv = g_ref[...]
        e = xhat * gv - t_ref[...]
        l_ref[...] += 0.5 * jnp.sum(jnp.mean(e * e, axis=-1, keepdims=True))
        dy = e * (1.0 / D)
        s_ref[0:1, :] += jnp.sum(dy * xhat, axis=0, keepdims=True)
        dxh = dy * gv
        dh = r * (dxh - xhat * jnp.mean(dxh * xhat, axis=-1, keepdims=True))
        dh_ref[...] = dh
        dm_ref[...] = (dh * gp_ref[...]).astype(dm_ref.dtype)
        s_ref[1:2, :] += jnp.sum(dh * mp_ref[...].astype(F32), axis=0, keepdims=True)

    return pl.pallas_call(body, name=name, grid=(S // tm,), in_specs=[blk, vec, blk, blk, vec], out_specs=[lss, blk, sums, blk],
                          out_shape=[jax.ShapeDtypeStruct((8, 128), F32), jax.ShapeDtypeStruct((S, D), F32),
                                     jax.ShapeDtypeStruct((8, D), F32), jax.ShapeDtypeStruct((S, D), BF16)],
                          compiler_params=_cp("arbitrary"))(h, g, tgt, m_prev, gate_prev)


def _shift_down(x, k, edge):
    tm = x.shape[0]
    rows = lax.broadcasted_iota(jnp.int32, x.shape, 0)
    out = pltpu.roll(x, k, 0)
    for j in range(k):
        out = jnp.where(rows == j, edge[8 - k + j:8 - k + j + 1, :], out)
    return out


def _shift_up(x, k, edge):
    tm = x.shape[0]
    rows = lax.broadcasted_iota(jnp.int32, x.shape, 0)
    out = pltpu.roll(x, tm - k, 0)
    for j in range(k):
        out = jnp.where(rows == tm - k + j, edge[j:j + 1, :], out)
    return out


def _conv_gate_fwd(up, cw, cb, *, name):
    _, S, F = up.shape
    tf = _tile(F, 1408)
    nf = F // tf
    tm = _tile(S, 512, 8)

    def body(a_ref, b_ref, w_ref, cb_ref, z_ref, edge_ref):
        @pl.when(pl.program_id(1) == 0)
        def _():
            edge_ref[...] = jnp.zeros_like(edge_ref)

        a = a_ref[...].astype(F32)
        edge = edge_ref[...]
        w = w_ref[...]
        ac = cb_ref[...] + w[2:3, :] * a + w[1:2, :] * _shift_down(a, 1, edge) + w[0:1, :] * _shift_down(a, 2, edge)
        edge_ref[...] = a[tm - 8:tm, :]
        z_ref[...] = (ac * jax.nn.sigmoid(ac) * b_ref[...].astype(F32)).astype(z_ref.dtype)

    return pl.pallas_call(
        body, name=name, grid=(nf, S // tm),
        in_specs=[pl.BlockSpec((None, tm, tf), lambda j, i: (0, i, j)), pl.BlockSpec((None, tm, tf), lambda j, i: (1, i, j)),
                  pl.BlockSpec((3, tf), lambda j, i: (0, j)), pl.BlockSpec((1, tf), lambda j, i: (0, j))],
        out_specs=pl.BlockSpec((tm, tf), lambda j, i: (i, j)),
        out_shape=jax.ShapeDtypeStruct((S, F), BF16), scratch_shapes=[pltpu.VMEM((8, tf), F32)],
        compiler_params=_cp("parallel", "arbitrary"))(up, up, cw, cb)


def _conv_gate_bwd(up, dz, cw, cb, *, name):
    _, S, F = up.shape
    tf = _tile(F, 1408)
    nf = F // tf
    tm = _tile(S, 512, 8)
    nt = S // tm
    hb = tm // 8

    def body(a_ref, ah_ref, b_ref, dz_ref, w_ref, cb_ref, d_ref, s_ref, edge_ref):
        i = pl.program_id(1)

        @pl.when(i == 0)
        def _():
            edge_ref[...] = jnp.zeros_like(edge_ref)
            s_ref[...] = jnp.zeros_like(s_ref)

        a = a_ref[...].astype(F32)
        halo = jnp.where(i == nt - 1, 0.0, ah_ref[...].astype(F32))
        w = w_ref[...]
        a1 = _shift_down(a, 1, halo)
        a2 = _shift_down(a, 2, halo)
        ac = cb_ref[...] + w[2:3, :] * a + w[1:2, :] * a1 + w[0:1, :] * a2
        sg = jax.nn.sigmoid(ac)
        dzv = dz_ref[...].astype(F32)
        si = ac * sg
        d_ref[1] = (dzv * si).astype(d_ref.dtype)
        dac = (dzv * b_ref[...].astype(F32)) * (sg + si * (1.0 - sg))
        s_ref[0:1, :] += jnp.sum(dac * a2, axis=0, keepdims=True)
        s_ref[1:2, :] += jnp.sum(dac * a1, axis=0, keepdims=True)
        s_ref[2:3, :] += jnp.sum(dac * a, axis=0, keepdims=True)
        s_ref[3:4, :] += jnp.sum(dac, axis=0, keepdims=True)
        edge = edge_ref[...]
        da = w[2:3, :] * dac + w[1:2, :] * _shift_up(dac, 1, edge) + w[0:1, :] * _shift_up(dac, 2, edge)
        edge_ref[...] = dac[0:8, :]
        d_ref[0] = da.astype(d_ref.dtype)

    tile = lambda hlf: pl.BlockSpec((None, tm, tf), lambda j, i: (hlf, nt - 1 - i, j))
    d_up, sums = pl.pallas_call(
        body, name=name, grid=(nf, nt),
        in_specs=[tile(0),
                  pl.BlockSpec((None, 8, tf), lambda j, i: (0, jnp.maximum((nt - 1 - i) * hb - 1, 0), j)),
                  tile(1), pl.BlockSpec((tm, tf), lambda j, i: (nt - 1 - i, j)),
                  pl.BlockSpec((3, tf), lambda j, i: (0, j)), pl.BlockSpec((1, tf), lambda j, i: (0, j))],
        out_specs=[pl.BlockSpec((2, tm, tf), lambda j, i: (0, nt - 1 - i, j)), pl.BlockSpec((8, tf), lambda j, i: (0, j))],
        out_shape=[jax.ShapeDtypeStruct((2, S, F), BF16), jax.ShapeDtypeStruct((8, F), F32)],
        scratch_shapes=[pltpu.VMEM((8, tf), F32)],
        compiler_params=_cp("parallel", "arbitrary"))(up, up, up, dz, cw, cb)
    return d_up, sums


def _log_sigmoid(x):
    return jnp.minimum(x, 0.0) - jnp.log(1.0 + jnp.exp(-jnp.abs(x)))


def _tri_ones(n, upper):
    r = lax.broadcasted_iota(jnp.int32, (n, n), 0)
    c = lax.broadcasted_iota(jnp.int32, (n, n), 1)
    return jnp.where((c >= r) if upper else (c <= r), 1.0, 0.0).astype(F32)


def _fgate_fwd(fl, bf, *, name):
    S, W = fl.shape
    tb = _tile(S, 256, 8)

    def body(fl_ref, b_ref, o_ref, carry_ref):
        @pl.when(pl.program_id(0) == 0)
        def _():
            carry_ref[...] = jnp.zeros_like(carry_ref)

        lf = _log_sigmoid(fl_ref[...] + b_ref[...])
        cs = jnp.dot(_tri_ones(tb, False), lf, precision=lax.Precision.HIGHEST, preferred_element_type=F32) + carry_ref[0:1, :]
        o_ref[...] = cs
        carry_ref[...] = jnp.broadcast_to(cs[tb - 1:tb, :], carry_ref.shape)

    blk = pl.BlockSpec((tb, W), lambda i: (i, 0))
    return pl.pallas_call(body, name=name, grid=(S // tb,), in_specs=[blk, pl.BlockSpec((1, W), lambda i: (0, 0))], out_specs=blk,
                          out_shape=jax.ShapeDtypeStruct((S, W), F32), scratch_shapes=[pltpu.VMEM((8, W), F32)],
                          compiler_params=_cp("arbitrary"))(fl, bf)


def _fgate_bwd(dF, fl, bf, *, name):
    S, W = fl.shape
    tb = _tile(S, 256, 8)
    nb = S // tb

    def body(d_ref, fl_ref, b_ref, o_ref, s_ref, carry_ref):
        @pl.when(pl.program_id(0) == 0)
        def _():
            carry_ref[...] = jnp.zeros_like(carry_ref)
            s_ref[...] = jnp.zeros_like(s_ref)

        rc = jnp.dot(_tri_ones(tb, True), d_ref[...], precision=lax.Precision.HIGHEST, preferred_element_type=F32) + carry_ref[0:1, :]
        carry_ref[...] = jnp.broadcast_to(rc[0:1, :], carry_ref.shape)
        dfl = rc * jax.nn.sigmoid(-(fl_ref[...] + b_ref[...]))
        o_ref[...] = dfl
        s_ref[...] += jnp.sum(dfl, axis=0, keepdims=True)

    blk = pl.BlockSpec((tb, W), lambda i: (nb - 1 - i, 0))
    vec = pl.BlockSpec((1, W), lambda i: (0, 0))
    return pl.pallas_call(body, name=name, grid=(nb,), in_specs=[blk, blk, vec], out_specs=[blk, vec],
                          out_shape=[jax.ShapeDtypeStruct((S, W), F32), jax.ShapeDtypeStruct((1, W), F32)],
                          scratch_shapes=[pltpu.VMEM((8, W), F32)], compiler_params=_cp("arbitrary"))(dF, fl, bf)


NT_DIMS = (((1,), (1,)), ((), ()))
TN_DIMS = (((0,), (0,)), ((), ()))
HP = HEADS // 2
Q_SCALE = HEAD_DIM ** -0.5


def _head_mask(x, hh):
    lanes = lax.broadcasted_iota(jnp.int32, x.shape, 1)
    return jnp.where((lanes >= hh * HEAD_DIM) & (lanes < (hh + 1) * HEAD_DIM), x, jnp.zeros_like(x))


def _lanes_from(cols, shape):
    lanes = lax.broadcasted_iota(jnp.int32, shape, 1)
    out = jnp.zeros(shape, F32)
    for i, cvec in enumerate(cols):
        out = jnp.where(lanes == i, cvec, out)
    return out


AUG_ROWS = 16


def _qa_pair(qt, aug):
    fill = jnp.zeros((HEAD_DIM - AUG_ROWS, qt.shape[1]), qt.dtype)
    return [jnp.concatenate([qt[0:HEAD_DIM], aug[0], fill], axis=0), jnp.concatenate([aug[1], fill, qt[HEAD_DIM:]], axis=0)]


def _rows_of_head(xt, hh):
    rows = lax.broadcasted_iota(jnp.int32, xt.shape, 0)
    return jnp.where((rows >= hh * HEAD_DIM) & (rows < (hh + 1) * HEAD_DIM), xt, jnp.zeros_like(xt))


def _put_rows(ref, rows):
    for i, r in enumerate(rows):
        ref[i:i + 1, :] = r
    ref[len(rows):, :] = jnp.zeros((ref.shape[0] - len(rows), ref.shape[1]), ref.dtype)


def _diag_mask_t(s):
    rows = lax.broadcasted_iota(jnp.int32, s.shape, 0)
    cols = lax.broadcasted_iota(jnp.int32, s.shape, 1)
    return jnp.where(rows <= cols, s, NEG)


def _foxt_fwd(qkvT, ka, vtb, augq, *, tb, name, comm=()):
    S = qkvT.shape[1]
    nb = S // tb

    def body(q_ref, ka_ref, v_ref, aq_ref, o_ref, lse_ref, m0, m1, l0, l1, acc0, acc1):
        qi = pl.program_id(1)
        qa = _qa_pair(q_ref[...], aq_ref[...])
        state = ((m0, l0, acc0), (m1, l1, acc1))
        for m_sc, l_sc, acc_sc in state:
            m_sc[...] = jnp.full_like(m_sc, NEG)
            l_sc[...] = jnp.zeros_like(l_sc)
            acc_sc[...] = jnp.zeros_like(acc_sc)

        def block(kj, masked):
            k0 = pl.multiple_of(kj * tb, tb)
            logits = [jnp.dot(ka_ref[hh, pl.ds(k0, tb), :], qa[hh], preferred_element_type=F32) for hh in range(2)]
            updates = []
            for hh, (m_sc, l_sc, acc_sc) in enumerate(state):
                s = _diag_mask_t(logits[hh]) if masked else logits[hh]
                m_old = m_sc[...]
                m_new = jnp.maximum(m_old, jnp.max(s, axis=0, keepdims=True))
                alpha = jnp.exp(m_old - m_new)
                p = jnp.exp(s - m_new)
                l_sc[...] = alpha * l_sc[...] + jnp.sum(p, axis=0, keepdims=True)
                m_sc[...] = m_new
                vt = v_ref[kj, HEAD_DIM * hh:HEAD_DIM * (hh + 1), :]
                updates.append((alpha, jnp.dot(vt, p.astype(BF16), preferred_element_type=F32)))
            for (alpha, pv), (_, _, acc_sc) in zip(updates, state):
                acc_sc[...] = alpha * acc_sc[...] + pv

        def off_diagonal(kj, c):
            block(kj, False)
            return c

        lax.fori_loop(0, qi, off_diagonal, 0)
        block(qi, True)
        for hh, (m_sc, l_sc, acc_sc) in enumerate(state):
            o_ref[HEAD_DIM * hh:HEAD_DIM * (hh + 1), :] = (acc_sc[...] / l_sc[...]).astype(o_ref.dtype)
        _put_rows(lse_ref, [m_sc[...] + jnp.log(l_sc[...]) for m_sc, l_sc, _ in state])

    return _call(
        body, name=name, grid=(HP, nb), comm=comm, sem=("parallel", "parallel"), args=(qkvT, ka, vtb, augq),
        in_specs=[pl.BlockSpec((128, tb), lambda hp, qi: (hp, qi)),
                  pl.BlockSpec((None, 2, S, 128), lambda hp, qi: (hp, 0, 0, 0)),
                  pl.BlockSpec((None, nb, 128, tb), lambda hp, qi: (hp, 0, 0, 0)),
                  pl.BlockSpec((None, 2, AUG_ROWS, tb), lambda hp, qi: (hp, 0, 0, qi))],
        out_specs=[pl.BlockSpec((128, tb), lambda hp, qi: (hp, qi)), pl.BlockSpec((None, 8, tb), lambda hp, qi: (hp, 0, qi))],
        out_shape=[jax.ShapeDtypeStruct((D_MODEL, S), BF16), jax.ShapeDtypeStruct((HP, 8, S), F32)],
        scratch_shapes=[pltpu.VMEM((1, tb), F32)] * 4 + [pltpu.VMEM((HEAD_DIM, tb), F32)] * 2)


def _foxt_bwd(qkvT, qkv, ka, qtb, dotb, do, augqb, lseb, otb, *, tb, name, comm=()):
    S = qkv.shape[0]
    nb = S // tb

    def body(ka_ref, v_ref, kt_ref, qt_ref, dot_ref, qn_ref, dn_ref, aq_ref, ls_ref, ot_ref,
             dq_ref, dk_ref, dv_ref, df_ref, dr_ref, dq_sc, dk_sc, dv_sc, ds_sc, dr_sc):
        kj = pl.program_id(1)

        @pl.when(kj == 0)
        def _():
            dq_sc[...] = jnp.zeros_like(dq_sc)
            dr_sc[...] = jnp.zeros_like(dr_sc)

        dk_sc[...] = jnp.zeros_like(dk_sc)
        dv_sc[...] = jnp.zeros_like(dv_sc)
        ds_sc[...] = jnp.zeros_like(ds_sc)
        v2 = v_ref[...]
        kth = [kt_ref[HEAD_DIM * hh:HEAD_DIM * (hh + 1), :] for hh in range(2)]

        def block(qi, masked):
            q0 = pl.multiple_of(qi * tb, tb)
            qa = _qa_pair(qt_ref[qi], aq_ref[qi])
            dot, ls = dot_ref[qi], ls_ref[qi]
            prod = ot_ref[qi].astype(F32) * dot.astype(F32)
            delta = [jnp.sum(prod[HEAD_DIM * hh:HEAD_DIM * (hh + 1)], axis=0, keepdims=True) for hh in range(2)]
            qn, dn = qn_ref[pl.ds(q0, tb), :], dn_ref[pl.ds(q0, tb), :]
            for hh in range(2):
                s = jnp.dot(ka_ref[hh], qa[hh], preferred_element_type=F32)
                if masked:
                    s = _diag_mask_t(s)
                p = jnp.exp(s - ls[hh:hh + 1, :])
                dp = jnp.dot(v2, _rows_of_head(dot, hh), preferred_element_type=F32)
                ds = p * (dp - delta[hh])
                dsb = ds.astype(BF16)
                dv_sc[...] += jnp.dot(p.astype(BF16), _head_mask(dn, hh), preferred_element_type=F32)
                dk_sc[...] += jnp.dot(dsb, _head_mask(qn, hh), preferred_element_type=F32)
                dq_sc[qi, HEAD_DIM * hh:HEAD_DIM * (hh + 1), :] += jnp.dot(kth[hh], dsb, preferred_element_type=F32)
                part = ds[:, 0:128]
                for j in range(1, tb // 128):
                    part = part + ds[:, 128 * j:128 * (j + 1)]
                ds_sc[hh] += part
                dr_sc[qi, hh:hh + 1, :] += jnp.sum(ds, axis=0, keepdims=True)

        def off_diagonal(i, c):
            block(kj + 1 + i, False)
            return c

        block(kj, True)
        lax.fori_loop(0, nb - 1 - kj, off_diagonal, 0)
        dk_ref[...] = dk_sc[...].astype(dk_ref.dtype)
        dv_ref[...] = dv_sc[...].astype(dv_ref.dtype)
        df_ref[...] = _lanes_from([-jnp.sum(ds_sc[hh], axis=1, keepdims=True) for hh in range(2)], (tb, 128))

        @pl.when(kj == nb - 1)
        def _():
            dq_ref[...] = (dq_sc[...] * Q_SCALE).astype(dq_ref.dtype)
            dr_ref[...] = dr_sc[...]

    resident = lambda rows: pl.BlockSpec((None, nb) + rows, lambda hp, kj: (hp,) + (0,) * (len(rows) + 1))
    whole = pl.BlockSpec((S, 128), lambda hp, kj: (0, hp))
    kblk = lambda off: pl.BlockSpec((tb, 128), lambda hp, kj: (kj, off + hp))
    return _call(
        body, name=name, grid=(HP, nb), comm=comm, sem=("parallel", "arbitrary"),
        args=(ka, qkv, qkvT, qtb, dotb, qkv, do, augqb, lseb, otb),
        in_specs=[pl.BlockSpec((None, 2, tb, 128), lambda hp, kj: (hp, 0, kj, 0)), kblk(2 * HP),
                  pl.BlockSpec((128, tb), lambda hp, kj: (HP + hp, kj)),
                  resident((128, tb)), resident((128, tb)), whole, whole, resident((2, AUG_ROWS, tb)), resident((8, tb)),
                  resident((128, tb))],
        out_specs=[resident((128, tb)), kblk(0), kblk(0), pl.BlockSpec((None, tb, 128), lambda hp, kj: (hp, kj, 0)),
                   resident((8, tb))],
        out_shape=[jax.ShapeDtypeStruct((HP, nb, 128, tb), BF16), jax.ShapeDtypeStruct((S, D_MODEL), BF16),
                   jax.ShapeDtypeStruct((S, D_MODEL), BF16), jax.ShapeDtypeStruct((HP, S, 128), F32),
                   jax.ShapeDtypeStruct((HP, nb, 8, tb), F32)],
        scratch_shapes=[pltpu.VMEM((nb, 128, tb), F32), pltpu.VMEM((tb, 128), F32), pltpu.VMEM((tb, 128), F32),
                        pltpu.VMEM((2, tb, 128), F32), pltpu.VMEM((nb, 8, tb), F32)])


def _split3(x):
    rnd = lambda v: lax.reduce_precision(v, exponent_bits=8, mantissa_bits=7)
    hi = rnd(x)
    mid = rnd(x - hi)
    lo = rnd(x - hi - mid)
    return hi.astype(BF16), mid.astype(BF16), lo.astype(BF16)


def _blocked(xt, nb, tb):
    lead = xt.shape[:-1]
    x = xt.reshape(lead + (nb, tb))
    return jnp.moveaxis(x, -2, 1)


def _aug_rows(bias):
    ones = jnp.ones(bias.shape, BF16)
    zeros = jnp.zeros(bias.shape, BF16)
    return jnp.stack(list(_split3(bias)) + [ones] * 3 + [zeros] * (AUG_ROWS - 6), axis=2)


def _fox_keys(qkv, fcol, *, name):
    S = qkv.shape[0]
    tr = _tile(S, 2048, 16)

    def body(k_ref, f_ref, o_ref):
        hp = pl.program_id(1)
        k2 = k_ref[...]
        lanes = lax.broadcasted_iota(jnp.int32, (tr, 128), 1)
        for hh in range(2):
            nf = -jnp.sum(jnp.where(lanes == 2 * hp + hh, f_ref[...], 0.0), axis=1, keepdims=True)
            hi = nf.astype(BF16).astype(F32)
            mid = (nf - hi).astype(BF16).astype(F32)
            lo = nf - hi - mid
            base = HEAD_DIM * (1 - hh)
            aug = jnp.where((lanes >= base) & (lanes < base + 3), 1.0, 0.0)
            for j, piece in enumerate((hi, mid, lo)):
                aug = jnp.where(lanes == base + 3 + j, piece, aug)
            mine = (lanes >= HEAD_DIM * hh) & (lanes < HEAD_DIM * (hh + 1))
            o_ref[hh] = jnp.where(mine, k2, aug.astype(BF16))

    return pl.pallas_call(
        body, name=name, grid=(S // tr, HP),
        in_specs=[pl.BlockSpec((tr, 128), lambda i, hp: (i, HP + hp)), pl.BlockSpec((tr, 128), lambda i, hp: (i, 0))],
        out_specs=pl.BlockSpec((None, 2, tr, 128), lambda i, hp: (hp, 0, i, 0)),
        out_shape=jax.ShapeDtypeStruct((HP, 2, S, 128), BF16), compiler_params=_cp("parallel", "parallel"))(qkv, fcol)


HALF = S5_SUB * S5_P
NCB = 2 * HALF // 128
RE, IM = slice(0, NCB // 2), slice(NCB // 2, NCB)


def _gelu(x):
    return 0.5 * x * (1.0 + jnp.tanh(GELU_K * (x + 0.044715 * x * x * x)))


def _gelu_grad(x):
    t = jnp.tanh(GELU_K * (x + 0.044715 * x * x * x))
    return 0.5 * (1.0 + t) + 0.5 * x * (1.0 - t * t) * GELU_K * (1.0 + 3.0 * 0.044715 * x * x)


def _s5_put(buf, s, val, lt):
    for cb in range(NCB):
        buf[cb, pl.ds(s, lt, stride=S5_SUB), :] = val[:, 128 * cb:128 * (cb + 1)]


def _s5_get(buf, s, lt):
    return jnp.concatenate([buf[cb, pl.ds(s, lt, stride=S5_SUB), :] for cb in range(NCB)], axis=1)


def _s5_project_in(u_ref, wb_ref, buf, lt):
    for s in range(S5_SUB):
        _s5_put(buf, s, jnp.dot(u_ref[:, 128 * s:128 * (s + 1)], wb_ref[s], preferred_element_type=F32), lt)


def _s5_scan(buf, lam_ref, h0, lt):
    a_re, a_im = lam_ref[RE], lam_ref[IM]

    def step(t, carry):
        hr, hi = carry
        r0 = pl.multiple_of(t * S5_SUB, S5_SUB)
        nr = a_re * hr - a_im * hi + buf[RE, pl.ds(r0, S5_SUB), :]
        ni = a_re * hi + a_im * hr + buf[IM, pl.ds(r0, S5_SUB), :]
        buf[RE, pl.ds(r0, S5_SUB), :] = nr
        buf[IM, pl.ds(r0, S5_SUB), :] = ni
        return nr, ni

    return lax.fori_loop(0, lt, step, (h0[RE], h0[IM]), unroll=4)


def _s5_fwd(u, wb, wc, lam, dskip, *, lt, name, comm=()):
    S, D = u.shape
    nt = S // lt

    def body(u_ref, wb_ref, wc_ref, lam_ref, d_ref, yp_ref, yg_ref, st_ref, buf, h_sc):
        @pl.when(pl.program_id(0) == 0)
        def _():
            h_sc[...] = jnp.zeros_like(h_sc)

        st_ref[...] = h_sc[...]
        _s5_project_in(u_ref, wb_ref, buf, lt)
        hr, hi = _s5_scan(buf, lam_ref, h_sc[...], lt)
        h_sc[RE] = hr
        h_sc[IM] = hi
        for s in range(S5_SUB):
            cols = slice(128 * s, 128 * (s + 1))
            hs = _s5_get(buf, s, lt).astype(BF16)
            yp = jnp.dot(hs, wc_ref[s], preferred_element_type=F32) + d_ref[:, cols] * u_ref[:, cols].astype(F32)
            yp_ref[:, cols] = yp.astype(yp_ref.dtype)
            yg_ref[:, cols] = _gelu(yp).astype(yg_ref.dtype)

    blk = pl.BlockSpec((lt, D), lambda i: (i, 0))
    full = lambda shp: pl.BlockSpec(shp, lambda i: (0,) * len(shp))
    state = (NCB, S5_SUB, 128)
    return _call(
        body, name=name, grid=(nt,), comm=comm, sem=("arbitrary",), args=(u, wb, wc, lam, dskip),
        in_specs=[blk, full(wb.shape), full(wc.shape), full(lam.shape), full(dskip.shape)],
        out_specs=[blk, blk, pl.BlockSpec((None,) + state, lambda i: (i, 0, 0, 0))],
        out_shape=[jax.ShapeDtypeStruct((S, D), BF16), jax.ShapeDtypeStruct((S, D), BF16), jax.ShapeDtypeStruct((nt,) + state, F32)],
        scratch_shapes=[pltpu.VMEM((NCB, lt * S5_SUB, 128), F32), pltpu.VMEM(state, F32)])


def _s5_bwd(u, dyp, st, wb, wc, lam, dskip, *, lt, name, comm=()):
    S, D = u.shape
    nt = S // lt

    def body(u_ref, dy_ref, st_ref, wb_ref, wc_ref, lam_ref, d_ref, du_ref, dwb_ref, dwc_ref, dlam_ref, hbuf, gbuf, g_sc):
        @pl.when(pl.program_id(0) == 0)
        def _():
            g_sc[...] = jnp.zeros_like(g_sc)
            dwb_ref[...] = jnp.zeros_like(dwb_ref)
            dwc_ref[...] = jnp.zeros_like(dwc_ref)
            dlam_ref[...] = jnp.zeros_like(dlam_ref)

        _s5_project_in(u_ref, wb_ref, hbuf, lt)
        _s5_scan(hbuf, lam_ref, st_ref[...], lt)
        for s in range(S5_SUB):
            dys = dy_ref[:, 128 * s:128 * (s + 1)]
            _s5_put(gbuf, s, lax.dot_general(dys, wc_ref[s], NT_DIMS, preferred_element_type=F32), lt)
        a_re, a_im = lam_ref[RE], lam_ref[IM]

        def one(t, carry, hp_re, hp_im):
            gr, gi, dar, dai = carry
            r0 = pl.multiple_of(t * S5_SUB, S5_SUB)
            nr = gbuf[RE, pl.ds(r0, S5_SUB), :] + a_re * gr + a_im * gi
            ni = gbuf[IM, pl.ds(r0, S5_SUB), :] + a_re * gi - a_im * gr
            gbuf[RE, pl.ds(r0, S5_SUB), :] = nr
            gbuf[IM, pl.ds(r0, S5_SUB), :] = ni
            return nr, ni, dar + nr * hp_re + ni * hp_im, dai + ni * hp_re - nr * hp_im

        def step(k, carry):
            t = lt - 1 - k
            p0 = pl.multiple_of((t - 1) * S5_SUB, S5_SUB)
            return one(t, carry, hbuf[RE, pl.ds(p0, S5_SUB), :], hbuf[IM, pl.ds(p0, S5_SUB), :])

        init = (g_sc[RE], g_sc[IM], dlam_ref[RE], dlam_ref[IM])
        carry = lax.fori_loop(0, lt - 1, step, init, unroll=2)
        gr, gi, dar, dai = one(0, carry, st_ref[RE], st_ref[IM])
        g_sc[RE] = gr
        g_sc[IM] = gi
        dlam_ref[RE] = dar
        dlam_ref[IM] = dai
        for s in range(S5_SUB):
            cols = slice(128 * s, 128 * (s + 1))
            gs = _s5_get(gbuf, s, lt).astype(BF16)
            hs = _s5_get(hbuf, s, lt).astype(BF16)
            us, dys = u_ref[:, cols], dy_ref[:, cols]
            du = lax.dot_general(gs, wb_ref[s], NT_DIMS, preferred_element_type=F32) + d_ref[:, cols] * dys.astype(F32)
            du_ref[:, cols] = du.astype(du_ref.dtype)
            dwb_ref[s] += lax.dot_general(us, gs, TN_DIMS, preferred_element_type=F32)
            dwc_ref[s] += lax.dot_general(dys, hs, TN_DIMS, preferred_element_type=F32)

    blk = pl.BlockSpec((lt, D), lambda i: (nt - 1 - i, 0))
    full = lambda shp: pl.BlockSpec(shp, lambda i: (0,) * len(shp))
    state = (NCB, S5_SUB, 128)
    words = pltpu.VMEM((NCB, lt * S5_SUB, 128), F32)
    return _call(
        body, name=name, grid=(nt,), comm=comm, sem=("arbitrary",), args=(u, dyp, st, wb, wc, lam, dskip),
        in_specs=[blk, blk, pl.BlockSpec((None,) + state, lambda i: (nt - 1 - i, 0, 0, 0)),
                  full(wb.shape), full(wc.shape), full(lam.shape), full(dskip.shape)],
        out_specs=[blk, full(wb.shape), full(wb.shape), full(lam.shape)],
        out_shape=[jax.ShapeDtypeStruct((S, D), BF16), jax.ShapeDtypeStruct(wb.shape, F32), jax.ShapeDtypeStruct(wb.shape, F32),
                   jax.ShapeDtypeStruct(lam.shape, F32)],
        scratch_shapes=[words, words, pltpu.VMEM(state, F32)])


def _s5_disc(lam_re, lam_im, log_dt, b_re, b_im):
    dt = jnp.exp(log_dt)[:, None]
    mag = jnp.exp(lam_re * dt)
    lb_re, lb_im = mag * jnp.cos(lam_im * dt), mag * jnp.sin(lam_im * dt)
    num_re = lb_re - 1.0
    den = lam_re * lam_re + lam_im * lam_im
    k_re = (num_re * lam_re + lb_im * lam_im) / den
    k_im = (lb_im * lam_re - num_re * lam_im) / den
    bb_re = k_re[..., None] * b_re - k_im[..., None] * b_im
    bb_im = k_re[..., None] * b_im + k_im[..., None] * b_re
    return lb_re, lb_im, bb_re, bb_im


def _s5_pack(lb_re, lb_im, bb_re, bb_im, c_re, c_im):
    eye = jnp.eye(S5_SUB, dtype=F32)
    tb = lambda x: jnp.einsum("sgcp,gh->sgchp", x.reshape(S5_SUB, S5_SUB, S5_P, S5_C).transpose(0, 1, 3, 2), eye).reshape(S5_SUB, 128, HALF)
    tc = lambda x: jnp.einsum("sgpc,gh->sgphc", x.reshape(S5_SUB, S5_SUB, S5_C, S5_P).transpose(0, 1, 3, 2), eye).reshape(S5_SUB, HALF, 128)
    wb = jnp.concatenate([tb(bb_re), tb(bb_im)], axis=2).astype(BF16)
    wc = jnp.concatenate([tc(c_re), -tc(c_im)], axis=1).astype(BF16)
    lam = jnp.concatenate([lb_re.reshape(S5_SUB, HALF), lb_im.reshape(S5_SUB, HALF)], axis=1)
    return wb, wc, lam.reshape(S5_SUB, NCB, 128).transpose(1, 0, 2)


def _s5_unpack(dwb, dwc, dlam):
    db = jnp.einsum("sgcrgp->rsgpc", dwb.reshape(S5_SUB, S5_SUB, S5_C, 2, S5_SUB, S5_P)).reshape(2, S5_G, S5_P, S5_C)
    dc = jnp.einsum("sgcrgp->rsgcp", dwc.reshape(S5_SUB, S5_SUB, S5_C, 2, S5_SUB, S5_P)).reshape(2, S5_G, S5_C, S5_P)
    dlam = dlam.transpose(1, 0, 2).reshape(S5_SUB, 2 * HALF)
    return (dlam[:, :HALF].reshape(S5_G, S5_P), dlam[:, HALF:].reshape(S5_G, S5_P), db[0], db[1], dc[0], -dc[1])


def _glu_fwd(yg, gl, *, name):
    S, D = yg.shape
    tm = _tile(S, 512, 8)
    blk = pl.BlockSpec((tm, D), lambda i: (i, 0))

    def body(y_ref, g_ref, o_ref):
        o_ref[...] = (y_ref[...].astype(F32) * jax.nn.sigmoid(g_ref[...].astype(F32))).astype(o_ref.dtype)

    return pl.pallas_call(body, name=name, grid=(S // tm,), in_specs=[blk, blk], out_specs=blk,
                          out_shape=jax.ShapeDtypeStruct((S, D), BF16), compiler_params=_cp("parallel"))(yg, gl)


def _glu_bwd(dy2, yg, gl, *, name):
    S, D = yg.shape
    tm = _tile(S, 512, 8)
    blk = pl.BlockSpec((tm, D), lambda i: (i, 0))

    def body(d_ref, y_ref, g_ref, da_ref, dg_ref):
        d = d_ref[...].astype(F32)
        sg = jax.nn.sigmoid(g_ref[...].astype(F32))
        da_ref[...] = d * sg
        dg_ref[...] = (d * y_ref[...].astype(F32) * sg * (1.0 - sg)).astype(dg_ref.dtype)

    return pl.pallas_call(body, name=name, grid=(S // tm,), in_specs=[blk, blk, blk], out_specs=[blk, blk],
                          out_shape=[jax.ShapeDtypeStruct((S, D), F32), jax.ShapeDtypeStruct((S, D), BF16)],
                          compiler_params=_cp("parallel"))(dy2, yg, gl)


def _gelu_bwd(da, db, ypre, u, *, name):
    S, D = ypre.shape
    tm = _tile(S, 512, 8)
    blk = pl.BlockSpec((tm, D), lambda i: (i, 0))
    vec = pl.BlockSpec((1, D), lambda i: (0, 0))

    def body(a_ref, b_ref, y_ref, u_ref, o_ref, s_ref):
        @pl.when(pl.program_id(0) == 0)
        def _():
            s_ref[...] = jnp.zeros_like(s_ref)

        dy = (a_ref[...] + b_ref[...]) * _gelu_grad(y_ref[...].astype(F32))
        o_ref[...] = dy.astype(o_ref.dtype)
        s_ref[...] += jnp.sum(dy * u_ref[...].astype(F32), axis=0, keepdims=True)

    return pl.pallas_call(body, name=name, grid=(S // tm,), in_specs=[blk, blk, blk, blk], out_specs=[blk, vec],
                          out_shape=[jax.ShapeDtypeStruct((S, D), BF16), jax.ShapeDtypeStruct((1, D), F32)],
                          compiler_params=_cp("arbitrary"))(da, db, ypre, u)


def _my_index():
    return 4 * lax.axis_index("x") + 2 * lax.axis_index("y") + lax.axis_index("c")


def _comm_plan(payloads):
    n = len(payloads)
    hbm = pl.BlockSpec(memory_space=pl.ANY)
    shapes = [jax.ShapeDtypeStruct((N_DEV,) + (x.shape if g else x.shape[1:]), x.dtype) for x, g in payloads]
    sems = [pltpu.SemaphoreType.DMA((n, N_DEV - 1)), pltpu.SemaphoreType.DMA((n, N_DEV - 1)), pltpu.SemaphoreType.DMA((n,))] if n else []
    return [hbm] * n, [hbm] * n, shapes, sems


def _comm_copies(x_refs, o_refs, gathers, send_sems, recv_sems, local_sems):
    mx, my, mc = lax.axis_index("x"), lax.axis_index("y"), lax.axis_index("c")
    me = 4 * mx + 2 * my + mc
    copies = []
    for i, (x_ref, o_ref, gather) in enumerate(zip(x_refs, o_refs, gathers)):
        src = (lambda j, r=x_ref: r) if gather else (lambda j, r=x_ref: r.at[j])
        copies.append(pltpu.make_async_copy(src(me), o_ref.at[me], local_sems.at[i]))
        for k in range(1, N_DEV):
            px, py, pc = mx ^ (k >> 2), my ^ ((k >> 1) & 1), mc ^ (k & 1)
            copies.append(pltpu.make_async_remote_copy(src(4 * px + 2 * py + pc), o_ref.at[me], send_sems.at[i, k - 1],
                                                       recv_sems.at[i, k - 1], device_id=(px, py, pc), device_id_type=MESH_ID))
    return copies


def _call(body, *, name, grid, in_specs, out_specs, out_shape, scratch_shapes, sem, args, comm=()):
    if not comm:
        outs = pl.pallas_call(body, name=name, grid=grid, in_specs=in_specs, out_specs=out_specs, out_shape=out_shape,
                              scratch_shapes=scratch_shapes, compiler_params=_cp(*sem))(*args)
        return outs, ()
    n, n_in, n_out = len(comm), len(in_specs), len(out_specs)
    c_in, c_out, c_shapes, c_sems = _comm_plan(comm)
    gathers = [g for _, g in comm]

    def wrapped(*refs):
        own_in, x_refs = refs[:n_in], refs[n_in:n_in + n]
        own_out, o_refs = refs[n_in + n:n_in + n + n_out], refs[n_in + n + n_out:n_in + 2 * n + n_out]
        own_scratch, sems = refs[n_in + 2 * n + n_out:-3], refs[-3:]
        ids = [pl.program_id(a) for a in range(len(grid))]
        first = functools.reduce(jnp.logical_and, [i == 0 for i in ids])
        last = functools.reduce(jnp.logical_and, [i == g - 1 for i, g in zip(ids, grid)])

        @pl.when(first)
        def _():
            for cp in _comm_copies(x_refs, o_refs, gathers, *sems):
                cp.start()

        body(*own_in, *own_out, *own_scratch)

        @pl.when(last)
        def _():
            for cp in _comm_copies(x_refs, o_refs, gathers, *sems):
                cp.wait()

    outs = pl.pallas_call(wrapped, name=name, grid=grid, in_specs=list(in_specs) + c_in, out_specs=list(out_specs) + c_out,
                          out_shape=list(out_shape) + c_shapes, scratch_shapes=list(scratch_shapes) + c_sems,
                          compiler_params=_cp(*["arbitrary"] * len(grid)))(*args, *[x for x, _ in comm])
    return outs[:n_out], outs[n_out:]


def _exchange_many(payloads, *, name):
    n = len(payloads)
    in_specs, out_specs, shapes, sems = _comm_plan(payloads)
    gathers = [g for _, g in payloads]

    def body(*refs):
        copies = _comm_copies(refs[:n], refs[n:2 * n], gathers, *refs[2 * n:])
        for cp in copies:
            cp.start()
        for cp in copies:
            cp.wait()

    return pl.pallas_call(body, name=name, in_specs=in_specs, out_specs=out_specs, out_shape=shapes, scratch_shapes=sems,
                          compiler_params=pltpu.CompilerParams(has_side_effects=True))(*[x for x, _ in payloads])


def _exchange(x, *, gather, name):
    return _exchange_many([(x, gather)], name=name)[0]


def _sum8(x8, *, name):
    _, R, C = x8.shape
    tr = _tile(R, 256, 16)

    def body(x_ref, o_ref):
        acc = x_ref[0].astype(F32)
        for j in range(1, N_DEV):
            acc = acc + x_ref[j].astype(F32)
        o_ref[...] = acc

    return pl.pallas_call(body, name=name, grid=(R // tr,), in_specs=[pl.BlockSpec((N_DEV, tr, C), lambda i: (0, i, 0))],
                          out_specs=pl.BlockSpec((tr, C), lambda i: (i, 0)), out_shape=jax.ShapeDtypeStruct((R, C), F32),
                          compiler_params=_cp("parallel"))(x8)


def _pack(arrs, dtype, row_mult):
    flat = jnp.concatenate([a.reshape(-1).astype(dtype) for a in arrs])
    n = flat.shape[0]
    rows = -(-n // 1024)
    rows = -(-rows // row_mult) * row_mult
    return jnp.pad(flat, (0, rows * 1024 - n)).reshape(rows, 1024)


def _unpack(buf, shapes):
    lead = buf.shape[:-2]
    flat = buf.reshape(lead + (-1,))
    out, off = [], 0
    for s in shapes:
        n = math.prod(s)
        out.append(flat[..., off:off + n].reshape(lead + tuple(s)))
        off += n
    return out


def _adaln_fwd(c_all, w, b, *, name):
    nk, D, n = w.shape

    def body(c_ref, w_ref, b_ref, o_ref):
        cv = c_ref[...]
        sc = (cv * jax.nn.sigmoid(cv)).astype(BF16)
        o_ref[...] = jnp.dot(sc, w_ref[...].astype(BF16), preferred_element_type=F32) + b_ref[...]

    return pl.pallas_call(body, name=name, grid=(nk,),
                          in_specs=[pl.BlockSpec((N_DEV, D), lambda k: (0, 0)), pl.BlockSpec((None, D, n), lambda k: (k, 0, 0)),
                                    pl.BlockSpec((None, 1, n), lambda k: (k, 0, 0))],
                          out_specs=pl.BlockSpec((None, N_DEV, n), lambda k: (k, 0, 0)),
                          out_shape=jax.ShapeDtypeStruct((nk, N_DEV, n), F32), compiler_params=_cp("parallel"))(c_all, w, b)


def _adaln_bwd(c_all, dmod, *, name):
    nk, _, n = dmod.shape
    D = c_all.shape[1]

    def body(c_ref, d_ref, dw_ref, db_ref):
        cv = c_ref[...]
        sc = (cv * jax.nn.sigmoid(cv)).astype(BF16).astype(F32)
        dm = d_ref[...]
        dw_ref[...] = lax.dot_general(sc, dm.astype(BF16).astype(F32), TN_DIMS, precision=lax.Precision.HIGHEST,
                                      preferred_element_type=F32)
        db_ref[...] = jnp.sum(dm, axis=0, keepdims=True)

    return pl.pallas_call(body, name=name, grid=(nk,),
                          in_specs=[pl.BlockSpec((N_DEV, D), lambda k: (0, 0)), pl.BlockSpec((None, N_DEV, n), lambda k: (k, 0, 0))],
                          out_specs=[pl.BlockSpec((None, D, n), lambda k: (k, 0, 0)), pl.BlockSpec((None, 1, n), lambda k: (k, 0, 0))],
                          out_shape=[jax.ShapeDtypeStruct((nk, D, n), F32), jax.ShapeDtypeStruct((nk, 1, n), F32)],
                          compiler_params=_cp("parallel"))(c_all, dmod)


def _adamw(w, g, m, v, *, name):
    R, C = w.shape
    tr = _tile(R, 256, 8)
    blk = pl.BlockSpec((tr, C), lambda i: (i, 0))
    c1 = 1.0 / (1.0 - ADAM_B1 ** ADAM_STEP)
    c2 = 1.0 / (1.0 - ADAM_B2 ** ADAM_STEP)

    def body(w_ref, g_ref, m_ref, v_ref, d_ref, nm_ref, nv_ref):
        gv = g_ref[...]
        nm = ADAM_B1 * m_ref[...] + (1.0 - ADAM_B1) * gv
        nv = ADAM_B2 * v_ref[...] + (1.0 - ADAM_B2) * (gv * gv)
        nm_ref[...] = nm
        nv_ref[...] = nv
        d_ref[...] = -ADAM_LR * ((nm * c1) / (jnp.sqrt(nv * c2) + ADAM_EPS) + ADAM_WD * w_ref[...])

    sh = jax.ShapeDtypeStruct((R, C), F32)
    return pl.pallas_call(body, name=name, grid=(R // tr,), in_specs=[blk] * 4, out_specs=[blk] * 3, out_shape=[sh] * 3,
                          compiler_params=_cp("parallel"))(w, g, m, v)


def _adamw_nd(w, g, m, v, *, name):
    shp = w.shape
    two = (-1, shp[-1]) if w.ndim > 1 else (1, -1)
    outs = _adamw(w.reshape(two), g.reshape(two), m.reshape(two), v.reshape(two), name=name)
    return [o.reshape(shp) for o in outs]


S5_TILE = 256
ATTN_BLOCK = 512
BIG = ("s5_w_in", "s5_w_glu", "s5_w_out", "fox_w_in", "fox_w_out", "ffn_w_up", "ffn_w_down")
WEIGHTS = ("norm_g", "ada_w", "ada_b", "s5_w_in", "s5_lam_re", "s5_lam_im", "s5_log_dt", "s5_b_re", "s5_b_im", "s5_c_re",
           "s5_c_im", "s5_d", "s5_w_glu", "s5_w_out", "fox_w_in", "fox_b_f", "fox_w_out", "ffn_w_up", "ffn_conv_w",
           "ffn_conv_b", "ffn_w_down", "final_g")
REPLICATED = ("s5_lam_re", "s5_lam_im", "s5_log_dt", "s5_b_re", "s5_b_im", "s5_c_re", "s5_c_im", "s5_d", "fox_b_f",
              "ffn_conv_b", "final_g")


S5_MATS = ("s5_w_in", "s5_w_glu", "s5_w_out")


def _row_group(arrs):
    return jnp.concatenate([a.reshape(-1, D_MODEL) for a in arrs], axis=0)


def _cols_join(g):
    return jnp.concatenate([g[j] for j in range(N_DEV)], axis=1)


def _cols_split(full):
    n = full.shape[1] // N_DEV
    return jnp.stack([full[:, j * n:(j + 1) * n] for j in range(N_DEV)])


def kernel(x, c, norm_g, ada_w, ada_b, s5_w_in, s5_lam_re, s5_lam_im, s5_log_dt, s5_b_re, s5_b_im, s5_c_re, s5_c_im, s5_d, s5_w_glu, s5_w_out, fox_w_in, fox_b_f, fox_w_out, ffn_w_up, ffn_conv_w, ffn_conv_b, ffn_w_down, final_g, loss_target, m_norm_g, m_ada_w, m_ada_b, m_s5_w_in, m_s5_lam_re, m_s5_lam_im, m_s5_log_dt, m_s5_b_re, m_s5_b_im, m_s5_c_re, m_s5_c_im, m_s5_d, m_s5_w_glu, m_s5_w_out, m_fox_w_in, m_fox_b_f, m_fox_w_out, m_ffn_w_up, m_ffn_conv_w, m_ffn_conv_b, m_ffn_w_down, m_final_g, v_norm_g, v_ada_w, v_ada_b, v_s5_w_in, v_s5_lam_re, v_s5_lam_im, v_s5_log_dt, v_s5_b_re, v_s5_b_im, v_s5_c_re, v_s5_c_im, v_s5_d, v_s5_w_glu, v_s5_w_out, v_fox_w_in, v_fox_b_f, v_fox_w_out, v_ffn_w_up, v_ffn_conv_w, v_ffn_conv_b, v_ffn_w_down, v_final_g):
    args = dict(locals())
    W = {n: args[n] for n in WEIGHTS}
    M = {n: args["m_" + n] for n in WEIGHTS}
    V = {n: args["v_" + n] for n in WEIGHTS}
    D, F = D_MODEL, D_FF
    me = _my_index()
    h0 = x[0]
    S = h0.shape[0]
    lt = min(S5_TILE, S)
    tb = min(ATTN_BLOCK, S)
    n_g, n_cw = norm_g.size, ffn_conv_w.size

    g0 = _exchange(_pack([c, norm_g, ffn_conv_w], F32, 8), gather=True, name="gather_small_in")
    c_all, ng_all, cw_all = _unpack(g0, [(D,), norm_g.shape, ffn_conv_w.shape])
    ng_full = ng_all.transpose(1, 2, 0, 3).reshape(2, 2, D)
    cw_full = cw_all.transpose(1, 2, 0, 3).reshape(2, 3, F)

    ncol = ada_w.shape[-1]
    modp = _adaln_fwd(c_all, ada_w.reshape(4, D, ncol), ada_b.reshape(4, 1, ncol), name="adaln_fwd")
    g1 = _exchange(_pack([modp], F32, 8), gather=True, name="gather_adaln")
    (mod_all,) = _unpack(g1, [modp.shape])
    mod = lax.dynamic_index_in_dim(mod_all, me, axis=2, keepdims=False).transpose(1, 0, 2).reshape(4, 3 * D)
    shift = [mod[k:k + 1, :D] for k in range(4)]
    scale = [mod[k:k + 1, D:2 * D] for k in range(4)]
    gate = [mod[k:k + 1, 2 * D:] for k in range(4)]
    gain = [_row(ng_full[k // 2, k % 2]) for k in range(4)]

    b16 = lambda a: a.astype(BF16)
    rows = D // N_DEV
    full = {"s5_w_in": _exchange(b16(s5_w_in[0]), gather=True, name="gather_s5_w_in").reshape(D, D)}
    bf_pad = jnp.pad(fox_b_f, ((0, 0), (0, 128 - HEADS)))

    def ffn_fwd(h, k, layer, comm=()):
        hn = _modulate(h, gain[k], shift[k], scale[k], name=f"modulate{k}")
        up = _mm(hn, full[f"ffn_w_up{layer}"], name=f"ffn_up{layer}", o_h=True, out_dtype=BF16, tn=1408, comm=comm)
        up, exchanged = up if comm else (up, ())
        z = _conv_gate_fwd(up, cw_full[layer], ffn_conv_b[layer:layer + 1], name=f"conv_gate{layer}")
        m, h_out = _mm(z, full[f"ffn_w_down{layer}"], name=f"ffn_down{layer}", tn=1024, res=h, gate=gate[k])
        return h_out, (hn, up, z, m), exchanged

    lb_re, lb_im, bb_re, bb_im = _s5_disc(s5_lam_re[0], s5_lam_im[0], s5_log_dt[0], s5_b_re[0], s5_b_im[0])
    wb, wc, lam = _s5_pack(lb_re, lb_im, bb_re, bb_im, s5_c_re[0], s5_c_im[0])
    hn0 = _modulate(h0, gain[0], shift[0], scale[0], name="modulate0")
    u = _mm(hn0, full["s5_w_in"], name="s5_in", out_dtype=BF16)
    (ypre, yg, st), (g_up0, g_rows0) = _s5_fwd(
        u, wb, wc, lam, s5_d, lt=lt, name="s5_scan",
        comm=[(b16(ffn_w_up[0]), True), (_row_group([b16(s5_w_glu), b16(s5_w_out), b16(ffn_w_down[0])]), True)])
    full["ffn_w_up0"] = _cols_join(g_up0)
    full["s5_w_glu"] = g_rows0[:, :rows].reshape(D, D)
    full["s5_w_out"] = g_rows0[:, rows:2 * rows].reshape(D, D)
    full["ffn_w_down0"] = g_rows0[:, 2 * rows:].reshape(F, D)
    gl = _mm(yg, full["s5_w_glu"], name="s5_glu", out_dtype=BF16)
    y2 = _glu_fwd(yg, gl, name="s5_glu_gate")
    m0, h1 = _mm(y2, full["s5_w_out"], name="s5_out", tn=1024, res=h0, gate=gate[0])
    h2, (hn1, up0, z0, m1), (g_fox, g_fox_out) = ffn_fwd(h1, 1, 0, comm=[(b16(fox_w_in[0]), True), (b16(fox_w_out[0]), True)])
    full["fox_w_in"] = _cols_join(g_fox)
    full["fox_w_out"] = g_fox_out.reshape(D, D)
    w_proj = jnp.pad(full["fox_w_in"], ((0, 0), (0, 3 * D + 128 - full["fox_w_in"].shape[1])))
    w_qkv, w_f = w_proj[:, :3 * D], w_proj[:, 3 * D:]

    hn2 = _modulate(h2, gain[2], shift[2], scale[2], name="modulate2")
    nb = S // tb
    w_qkv_s = jnp.concatenate([w_qkv[:, :D] * Q_SCALE, w_qkv[:, D:]], axis=1)
    qkv, qkvT = _mm(hn2, w_qkv_s, name="fox_qkv", out_dtype=BF16, with_t=True)
    fl = _mm(hn2, w_f, name="fox_f", out_dtype=F32)
    fcol = _fgate_fwd(fl, bf_pad, name="fox_fgate")
    f_heads = fcol[:, :HEADS].T.reshape(HP, 2, S)
    ka = _fox_keys(qkv, fcol, name="fox_keys")
    vtb = _blocked(qkvT[2 * D:].reshape(HP, 128, S), nb, tb)
    aug = _aug_rows(f_heads)
    (oT, lse), (g_up1, g_down1) = _foxt_fwd(qkvT, ka, vtb, aug, tb=tb, name="fox_attn",
                                            comm=[(b16(ffn_w_up[1]), True), (b16(ffn_w_down[1]), True)])
    full["ffn_w_up1"] = _cols_join(g_up1)
    full["ffn_w_down1"] = g_down1.reshape(F, D)
    m2, h3 = _mm(oT, full["fox_w_out"], name="fox_out", ta=True, tn=1024, res=h2, gate=gate[2])
    h4, (hn3, up1, z1, m3), _ = ffn_fwd(h3, 3, 1)

    lblk, dh, head_sums, dm = _loss_head(h4, _row(final_g), loss_target[0], m3, gate[3], name="loss_head")
    d_final_g, dgate = head_sums[0:1], head_sums[1:2]
    loss = lax.psum(lblk[0, 0], ("x", "y", "c"))

    G = {}
    dmod = [None] * 4
    mixer_out = [m0, m1, m2, m3]

    def norm_bwd(h, dhn, dh_in, k, dgate_k):
        if k == 0:
            dh_out, sums = _modulate_bwd(h, dhn, dh_in, gain[k], scale[k], name=f"modulate_bwd{k}")
            dm_below = dgate_below = None
        else:
            dh_out, sums, dm_below = _modulate_bwd(h, dhn, dh_in, gain[k], scale[k], mixer_out[k - 1], gate[k - 1],
                                                   name=f"modulate_bwd{k}")
            dgate_below = sums[3:4]
        dmod[k] = jnp.concatenate([sums[0:1], sums[1:2], dgate_k], axis=1)
        return dh_out, sums[2], dm_below, dgate_below

    def ffn_bwd(dh_in, dm, dgate_k, h, k, layer, saved):
        hn, up, z, m = saved
        dz = _mm(dm, full[f"ffn_w_down{layer}"], name=f"ffn_down_dx{layer}", tb=True, out_dtype=BF16, tn=1408)
        dw_down = _mm(z, dm, name=f"ffn_down_dw{layer}", ta=True, tm=1408, tn=1024, tk=2048)
        d_up, cs = _conv_gate_bwd(up, dz, cw_full[layer], ffn_conv_b[layer:layer + 1], name=f"conv_gate_bwd{layer}")
        dhn = _mm(d_up, full[f"ffn_w_up{layer}"], name=f"ffn_up_dx{layer}", tb=True, a_h=True, tk=2816)
        dw_up = _mm(hn, d_up, name=f"ffn_up_dw{layer}", ta=True, b_h=True, tm=1024, tn=1408, tk=2048)
        dh_out, dg, dm_below, dgate_below = norm_bwd(h, dhn, dh_in, k, dgate_k)
        return dh_out, dm_below, dgate_below, dg, dw_up, dw_down, cs[0:3], cs[3]

    dh, dm, dgate, dg3, dw_up1, dw_down1, dcw1, dcb1 = ffn_bwd(dh, dm, dgate, h3, 3, 1, (hn3, up1, z1, m3))

    do, doT = _mm(dm, full["fox_w_out"], name="fox_out_dx", tb=True, out_dtype=BF16, with_t=True)
    dw_fox_out = _mm(oT, dm, name="fox_out_dw", tn=1024, tk=2048)
    to_rows = lambda g: b16(g).reshape(N_DEV, -1, D)
    (dqtb, dk, dv, dfk, dfq), (r_up1, r_down1) = _foxt_bwd(
        qkvT, qkv, ka, _blocked(qkvT[:D].reshape(HP, 128, S), nb, tb), _blocked(doT.reshape(HP, 128, S), nb, tb), do,
        _blocked(aug, nb, tb), _blocked(lse, nb, tb), _blocked(oT.reshape(HP, 128, S), nb, tb), tb=tb, name="fox_attn_bwd",
        comm=[(_cols_split(b16(dw_up1)), False), (to_rows(dw_down1), False)])
    dq = dqtb.transpose(1, 3, 0, 2).reshape(S, D)
    dF = dfk[:, :, :2].transpose(1, 0, 2).reshape(S, HEADS) + dfq[:, :, :2, :].transpose(1, 3, 0, 2).reshape(S, HEADS)
    dF = jnp.pad(dF, ((0, 0), (0, 128 - HEADS)))
    dfl, dbf = _fgate_bwd(dF, fl, bf_pad, name="fox_fgate_bwd")
    dproj = jnp.concatenate([dq, dk, dv, dfl.astype(BF16)], axis=1)
    dhn = _mm(dproj, w_proj, name="fox_in_dx", tb=True)
    dw_proj = _mm(hn2, dproj, name="fox_in_dw", ta=True, tn=640, tk=2048)
    dw_fox_in = dw_proj[:, :full["fox_w_in"].shape[1]]
    dh, dg2, dm, dgate = norm_bwd(h2, dhn, dh, 2, dgate)

    dh, dm, dgate, dg1, dw_up0, dw_down0, dcw0, dcb0 = ffn_bwd(dh, dm, dgate, h1, 1, 0, (hn1, up0, z0, m1))

    dy2 = _mm(dm, full["s5_w_out"], name="s5_out_dx", tb=True, out_dtype=BF16)
    G["s5_w_out"] = _mm(y2, dm, name="s5_out_dw", ta=True, tn=1024, tk=2048)
    da, dgl = _glu_bwd(dy2, yg, gl, name="s5_glu_bwd")
    dyg_b = _mm(dgl, full["s5_w_glu"], name="s5_glu_dx", tb=True)
    G["s5_w_glu"] = _mm(yg, dgl, name="s5_glu_dw", ta=True, tn=1024, tk=2048)
    dyp, dd = _gelu_bwd(da, dyg_b, ypre, u, name="s5_gelu_bwd")
    (du, dwb, dwc, dlam), (r_up0, r_rows0, r_fox) = _s5_bwd(
        u, dyp, st, wb, wc, lam, s5_d, lt=lt, name="s5_scan_bwd",
        comm=[(_cols_split(b16(dw_up0)), False), (jnp.concatenate([to_rows(dw_fox_out), to_rows(dw_down0)], axis=1), False),
              (_cols_split(b16(dw_fox_in)), False)])
    dhn = _mm(du, full["s5_w_in"], name="s5_in_dx", tb=True)
    G["s5_w_in"] = _mm(hn0, du, name="s5_in_dw", ta=True, tn=1024, tk=2048)
    dh, dg0, _, _ = norm_bwd(h0, dhn, dh, 0, dgate)
    grad_x = dh[None]

    dlb_re, dlb_im, dbb_re, dbb_im, dc_re, dc_im = _s5_unpack(dwb, dwc, dlam)
    _, disc_vjp = jax.vjp(_s5_disc, s5_lam_re[0], s5_lam_im[0], s5_log_dt[0], s5_b_re[0], s5_b_im[0])
    dlam_re, dlam_im, dlog_dt, db_re, db_im = disc_vjp((dlb_re, dlb_im, dbb_re, dbb_im))

    g2 = _exchange(_pack([jnp.concatenate(dmod, axis=0)], F32, 8), gather=True, name="gather_dmod")
    (dmod_all,) = _unpack(g2, [(4, 3 * D)])
    dmod_mine = _columns_of_mod(dmod_all, me, ncol)
    d_ada_w, d_ada_b = _adaln_bwd(c_all, dmod_mine, name="adaln_bwd")
    G["ada_w"] = d_ada_w.reshape(ada_w.shape)
    G["ada_b"] = d_ada_b.reshape(ada_b.shape)

    small_full = {
        "norm_g": jnp.stack([dg0, dg1, dg2, dg3]).reshape(2, 2, D),
        "ffn_conv_w": jnp.stack([dcw0, dcw1]),
        "s5_lam_re": dlam_re[None], "s5_lam_im": dlam_im[None], "s5_log_dt": dlog_dt[None],
        "s5_b_re": db_re[None], "s5_b_im": db_im[None], "s5_c_re": dc_re[None], "s5_c_im": dc_im[None],
        "s5_d": dd, "fox_b_f": dbf[:, :HEADS], "ffn_conv_b": jnp.stack([dcb0, dcb1]), "final_g": d_final_g[0],
    }
    names = tuple(small_full)
    small_pack = _pack([small_full[n] for n in names], F32, 8 * N_DEV)
    r_s5, r_small = _exchange_many([(jnp.concatenate([to_rows(G[n]) for n in S5_MATS], axis=1), False),
                                    (small_pack.reshape(N_DEV, -1, 1024), False)], name="scatter_s5_and_small_grads")
    s5_sum = _sum8(r_s5, name="sum_s5_grads")
    for i, n in enumerate(S5_MATS):
        G[n] = s5_sum[i * rows:(i + 1) * rows][None]
    rows0_sum = _sum8(r_rows0, name="sum_rows0_grads")
    G["fox_w_out"] = rows0_sum[:rows][None]
    G["ffn_w_down"] = jnp.stack([rows0_sum[rows:], _sum8(r_down1, name="sum_down1_grads")])
    G["ffn_w_up"] = jnp.stack([_sum8(r_up0, name="sum_up0_grads"), _sum8(r_up1, name="sum_up1_grads")])
    G["fox_w_in"] = _sum8(r_fox, name="sum_fox_w_in")[None]

    g3 = _exchange(_sum8(r_small, name="sum_small_grads"), gather=True, name="gather_small_sums")
    summed = dict(zip(names, _unpack(g3.reshape(small_pack.shape), [small_full[n].shape for n in names])))
    G["norm_g"] = lax.dynamic_slice_in_dim(summed["norm_g"], me * norm_g.shape[-1], norm_g.shape[-1], axis=2)
    G["ffn_conv_w"] = lax.dynamic_slice_in_dim(summed["ffn_conv_w"], me * ffn_conv_w.shape[-1], ffn_conv_w.shape[-1], axis=2)
    for n in REPLICATED:
        G[n] = summed[n]

    delta, new_m, new_v = {}, {}, {}
    small = tuple(n for n in WEIGHTS if n not in BIG and n != "ada_w")
    for n in WEIGHTS:
        if n not in small:
            delta[n], new_m[n], new_v[n] = _adamw_nd(W[n], G[n], M[n], V[n], name=f"adamw_{n}")
    packed = [_pack([src[n] for n in small], F32, 8) for src in (W, G, M, V)]
    for dst, buf in zip((delta, new_m, new_v), _adamw(*packed, name="adamw_small")):
        dst.update(zip(small, _unpack(buf, [W[n].shape for n in small])))

    return (loss, grad_x, *[G[n] for n in WEIGHTS], *[delta[n] for n in WEIGHTS], *[new_m[n] for n in WEIGHTS],
            *[new_v[n] for n in WEIGHTS])


def _columns_of_mod(dmod_all, me, ncol):
    flat = lax.dynamic_slice_in_dim(dmod_all, me * ncol, ncol, axis=2)
    return flat.transpose(1, 0, 2)
```

```python
import functools
import math

import jax
import jax.numpy as jnp
from jax import lax
from jax.experimental import pallas as pl
from jax.experimental.pallas import tpu as pltpu

F32, BF16 = jnp.float32, jnp.bfloat16
EPS = 1e-6
N_DEV = 8
D_MODEL = 1024
D_FF = 2816
HEADS = 16
HEAD_DIM = 64
S5_G, S5_P, S5_C = 64, 64, 16
S5_SUB = 8
V7X_VMEM_LIMIT = 56 * 1024 * 1024
NEG = -1e30
ADAM_LR, ADAM_B1, ADAM_B2, ADAM_EPS, ADAM_WD, ADAM_STEP = 1e-3, 0.9, 0.999, 1e-8, 0.01, 10
GELU_K = math.sqrt(2.0 / math.pi)
MESH_ID = pl.DeviceIdType.MESH


def _cp(*sem):
    return pltpu.CompilerParams(dimension_semantics=sem, vmem_limit_bytes=V7X_VMEM_LIMIT)


def _tile(n, target, mult=128):
    if n <= target:
        return n
    t = (target // mult) * mult
    while t >= mult:
        if n % t == 0:
            return t
        t -= mult
    return n


def _row(v):
    return v.reshape(1, -1).astype(F32)


def _mm(a, b, *, name, ta=False, tb=False, out_dtype=F32, tm=1024, tn=1024, tk=None, res=None, gate=None,
        a_h=False, b_h=False, o_h=False, with_t=False, comm=()):
    if a_h:
        M, K = a.shape[1], 2 * a.shape[2]
    elif ta:
        K, M = a.shape
    else:
        M, K = a.shape
    if b_h:
        N = 2 * b.shape[2]
    else:
        N = b.shape[0] if tb else b.shape[1]
    half_n = N // 2
    tm = _tile(M, tm, 128 if ta else 8)
    tn = _tile(half_n if (b_h or o_h) else N, tn)
    tk = K if tk is None else _tile(K // 2 if a_h else K, tk)
    nk = K // tk
    nkh, nnh = (K // 2) // tk if a_h else 1, half_n // tn
    if a_h:
        a_spec = pl.BlockSpec((None, tm, tk), lambda i, j, k: (k // nkh, i, k % nkh))
    elif ta:
        a_spec = pl.BlockSpec((tk, tm), lambda i, j, k: (k, i))
    else:
        a_spec = pl.BlockSpec((tm, tk), lambda i, j, k: (i, k))
    if b_h:
        b_spec = pl.BlockSpec((None, tk, tn), lambda i, j, k: (j // nnh, k, j % nnh))
    elif tb:
        b_spec = pl.BlockSpec((tn, tk), lambda i, j, k: (j, k))
    else:
        b_spec = pl.BlockSpec((tk, tn), lambda i, j, k: (k, j))
    if o_h:
        o_spec = pl.BlockSpec((None, tm, tn), lambda i, j, k: (j // nnh, i, j % nnh))
    else:
        o_spec = pl.BlockSpec((tm, tn), lambda i, j, k: (i, j))
    dn = (((0 if ta else 1,), (1 if tb else 0,)), ((), ()))
    fused = res is not None

    def body(*refs):
        if fused:
            a_ref, b_ref, r_ref, g_ref, m_ref, o_ref, acc_ref = refs
        elif with_t:
            a_ref, b_ref, o_ref, t_ref, acc_ref = refs
        else:
            a_ref, b_ref, o_ref, acc_ref = refs
        p = lax.dot_general(a_ref[...].astype(BF16), b_ref[...].astype(BF16), dn, preferred_element_type=F32)

        def finish(acc):
            if fused:
                m_ref[...] = acc.astype(m_ref.dtype)
                o_ref[...] = r_ref[...] + g_ref[...] * acc
            else:
                o_ref[...] = acc.astype(o_ref.dtype)
                if with_t:
                    t_ref[...] = acc.T.astype(t_ref.dtype)

        if nk == 1:
            finish(p)
        else:
            k = pl.program_id(2)

            @pl.when(k == 0)
            def _():
                acc_ref[...] = p

            @pl.when(k > 0)
            def _():
                acc_ref[...] += p

            @pl.when(k == nk - 1)
            def _():
                finish(acc_ref[...])

    in_specs = [a_spec, b_spec]
    args = [a, b]
    if fused:
        in_specs += [o_spec, pl.BlockSpec((1, tn), lambda i, j, k: (0, j))]
        args += [res, gate]
        out_shape = [jax.ShapeDtypeStruct((M, N), BF16), jax.ShapeDtypeStruct((M, N), F32)]
        out_specs = [o_spec, o_spec]
    else:
        out_shape = [jax.ShapeDtypeStruct((2, M, half_n) if o_h else (M, N), out_dtype)]
        out_specs = [o_spec]
        if with_t:
            out_shape.append(jax.ShapeDtypeStruct((N, M), out_dtype))
            out_specs.append(pl.BlockSpec((tn, tm), lambda i, j, k: (j, i)))
    outs, exchanged = _call(
        body, name=name, grid=(M // tm, N // tn, nk), in_specs=in_specs, out_specs=out_specs, out_shape=out_shape,
        scratch_shapes=[pltpu.VMEM((tm, tn) if nk > 1 else (8, 128), F32)], sem=("parallel", "parallel", "arbitrary"),
        args=args, comm=comm)
    outs = tuple(outs) if (fused or with_t) else outs[0]
    return (outs, exchanged) if comm else outs


def _modulate(h, g, shift, scale, *, name):
    S, D = h.shape
    tm = _tile(S, 512, 8)
    vec = pl.BlockSpec((1, D), lambda i: (0, 0))
    blk = pl.BlockSpec((tm, D), lambda i: (i, 0))

    def body(h_ref, g_ref, sh_ref, sc_ref, o_ref):
        x = h_ref[...]
        r = lax.rsqrt(jnp.mean(x * x, axis=-1, keepdims=True) + EPS)
        o_ref[...] = ((x * r * g_ref[...]) * (1.0 + sc_ref[...]) + sh_ref[...]).astype(o_ref.dtype)

    return pl.pallas_call(body, name=name, grid=(S // tm,), in_specs=[blk, vec, vec, vec], out_specs=blk,
                          out_shape=jax.ShapeDtypeStruct((S, D), BF16), compiler_params=_cp("parallel"))(h, g, shift, scale)


def _modulate_bwd(h, dhn, dh_in, g, scale, m_prev=None, gate_prev=None, *, name):
    S, D = h.shape
    tm = _tile(S, 512, 8)
    vec = pl.BlockSpec((1, D), lambda i: (0, 0))
    blk = pl.BlockSpec((tm, D), lambda i: (i, 0))
    sums = pl.BlockSpec((8, D), lambda i: (0, 0))
    below = m_prev is not None

    def body(h_ref, dhn_ref, dhi_ref, g_ref, sc_ref, *rest):
        dh_ref, s_ref = rest[-3:-1] if below else rest[-2:]

        @pl.when(pl.program_id(0) == 0)
        def _():
            s_ref[...] = jnp.zeros_like(s_ref)

        x = h_ref[...]
        r = lax.rsqrt(jnp.mean(x * x, axis=-1, keepdims=True) + EPS)
        xhat = x * r
        gv = g_ref[...]
        dhn_v = dhn_ref[...].astype(F32)
        dn = dhn_v * (1.0 + sc_ref[...])
        s_ref[0:1, :] += jnp.sum(dhn_v, axis=0, keepdims=True)
        s_ref[1:2, :] += jnp.sum(dhn_v * (xhat * gv), axis=0, keepdims=True)
        s_ref[2:3, :] += jnp.sum(dn * xhat, axis=0, keepdims=True)
        dxh = dn * gv
        dh = dhi_ref[...] + r * (dxh - xhat * jnp.mean(dxh * xhat, axis=-1, keepdims=True))
        dh_ref[...] = dh
        if below:
            mp_ref, gp_ref, dm_ref = rest[0], rest[1], rest[-1]
            dm_ref[...] = (dh * gp_ref[...]).astype(dm_ref.dtype)
            s_ref[3:4, :] += jnp.sum(dh * mp_ref[...].astype(F32), axis=0, keepdims=True)

    extra_in, extra_args = ([blk, vec], [m_prev, gate_prev]) if below else ([], [])
    return pl.pallas_call(body, name=name, grid=(S // tm,), in_specs=[blk, blk, blk, vec, vec] + extra_in,
                          out_specs=[blk, sums] + ([blk] if below else []),
                          out_shape=[jax.ShapeDtypeStruct((S, D), F32), jax.ShapeDtypeStruct((8, D), F32)]
                          + ([jax.ShapeDtypeStruct((S, D), BF16)] if below else []),
                          compiler_params=_cp("arbitrary"))(h, dhn, dh_in, g, scale, *extra_args)


def _loss_head(h, g, tgt, m_prev, gate_prev, *, name):
    S, D = h.shape
    tm = _tile(S, 512, 8)
    vec = pl.BlockSpec((1, D), lambda i: (0, 0))
    blk = pl.BlockSpec((tm, D), lambda i: (i, 0))
    lss = pl.BlockSpec((8, 128), lambda i: (0, 0))
    sums = pl.BlockSpec((8, D), lambda i: (0, 0))

    def body(h_ref, g_ref, t_ref, mp_ref, gp_ref, l_ref, dh_ref, s_ref, dm_ref):
        @pl.when(pl.program_id(0) == 0)
        def _():
            l_ref[...] = jnp.zeros_like(l_ref)
            s_ref[...] = jnp.zeros_like(s_ref)

        x = h_ref[...]
        r = lax.rsqrt(jnp.mean(x * x, axis=-1, keepdims=True) + EPS)
        xhat = x * r
        gv = g_ref[...]
        e = xhat * gv - t_ref[...]
        l_ref[...] += 0.5 * jnp.sum(jnp.mean(e * e, axis=-1, keepdims=True))
        dy = e * (1.0 / D)
        s_ref[0:1, :] += jnp.sum(dy * xhat, axis=0, keepdims=True)
        dxh = dy * gv
        dh = r * (dxh - xhat * jnp.mean(dxh * xhat, axis=-1, keepdims=True))
        dh_ref[...] = dh
        dm_ref[...] = (dh * gp_ref[...]).astype(dm_ref.dtype)
        s_ref[1:2, :] += jnp.sum(dh * mp_ref[...].astype(F32), axis=0, keepdims=True)

    return pl.pallas_call(body, name=name, grid=(S // tm,), in_specs=[blk, vec, blk, blk, vec], out_specs=[lss, blk, sums, blk],
                          out_shape=[jax.ShapeDtypeStruct((8, 128), F32), jax.ShapeDtypeStruct((S, D), F32),
                                     jax.ShapeDtypeStruct((8, D), F32), jax.ShapeDtypeStruct((S, D), BF16)],
                          compiler_params=_cp("arbitrary"))(h, g, tgt, m_prev, gate_prev)


def _shift_down(x, k, edge):
    tm = x.shape[0]
    rows = lax.broadcasted_iota(jnp.int32, x.shape, 0)
    out = pltpu.roll(x, k, 0)
    for j in range(k):
        out = jnp.where(rows == j, edge[8 - k + j:8 - k + j + 1, :], out)
    return out


def _shift_up(x, k, edge):
    tm = x.shape[0]
    rows = lax.broadcasted_iota(jnp.int32, x.shape, 0)
    out = pltpu.roll(x, tm - k, 0)
    for j in range(k):
        out = jnp.where(rows == tm - k + j, edge[j:j + 1, :], out)
    return out


def _conv_gate_fwd(up, cw, cb, *, name):
    _, S, F = up.shape
    tf = _tile(F, 1408)
    nf = F // tf
    tm = _tile(S, 512, 8)

    def body(a_ref, b_ref, w_ref, cb_ref, z_ref, edge_ref):
        @pl.when(pl.program_id(1) == 0)
        def _():
            edge_ref[...] = jnp.zeros_like(edge_ref)

        a = a_ref[...].astype(F32)
        edge = edge_ref[...]
        w = w_ref[...]
        ac = cb_ref[...] + w[2:3, :] * a + w[1:2, :] * _shift_down(a, 1, edge) + w[0:1, :] * _shift_down(a, 2, edge)
        edge_ref[...] = a[tm - 8:tm, :]
        z_ref[...] = (ac * jax.nn.sigmoid(ac) * b_ref[...].astype(F32)).astype(z_ref.dtype)

    return pl.pallas_call(
        body, name=name, grid=(nf, S // tm),
        in_specs=[pl.BlockSpec((None, tm, tf), lambda j, i: (0, i, j)), pl.BlockSpec((None, tm, tf), lambda j, i: (1, i, j)),
                  pl.BlockSpec((3, tf), lambda j, i: (0, j)), pl.BlockSpec((1, tf), lambda j, i: (0, j))],
        out_specs=pl.BlockSpec((tm, tf), lambda j, i: (i, j)),
        out_shape=jax.ShapeDtypeStruct((S, F), BF16), scratch_shapes=[pltpu.VMEM((8, tf), F32)],
        compiler_params=_cp("parallel", "arbitrary"))(up, up, cw, cb)


def _conv_gate_bwd(up, dz, cw, cb, *, name):
    _, S, F = up.shape
    tf = _tile(F, 1408)
    nf = F // tf
    tm = _tile(S, 512, 8)
    nt = S // tm
    hb = tm // 8

    def body(a_ref, ah_ref, b_ref, dz_ref, w_ref, cb_ref, d_ref, s_ref, edge_ref):
        i = pl.program_id(1)

        @pl.when(i == 0)
        def _():
            edge_ref[...] = jnp.zeros_like(edge_ref)
            s_ref[...] = jnp.zeros_like(s_ref)

        a = a_ref[...].astype(F32)
        halo = jnp.where(i == nt - 1, 0.0, ah_ref[...].astype(F32))
        w = w_ref[...]
        a1 = _shift_down(a, 1, halo)
        a2 = _shift_down(a, 2, halo)
        ac = cb_ref[...] + w[2:3, :] * a + w[1:2, :] * a1 + w[0:1, :] * a2
        sg = jax.nn.sigmoid(ac)
        dzv = dz_ref[...].astype(F32)
        si = ac * sg
        d_ref[1] = (dzv * si).astype(d_ref.dtype)
        dac = (dzv * b_ref[...].astype(F32)) * (sg + si * (1.0 - sg))
        s_ref[0:1, :] += jnp.sum(dac * a2, axis=0, keepdims=True)
        s_ref[1:2, :] += jnp.sum(dac * a1, axis=0, keepdims=True)
        s_ref[2:3, :] += jnp.sum(dac * a, axis=0, keepdims=True)
        s_ref[3:4, :] += jnp.sum(dac, axis=0, keepdims=True)
        edge = edge_ref[...]
        da = w[2:3, :] * dac + w[1:2, :] * _shift_up(dac, 1, edge) + w[0:1, :] * _shift_up(dac, 2, edge)
        edge_ref[...] = dac[0:8, :]
        d_ref[0] = da.astype(d_ref.dtype)

    tile = lambda hlf: pl.BlockSpec((None, tm, tf), lambda j, i: (hlf, nt - 1 - i, j))
    d_up, sums = pl.pallas_call(
        body, name=name, grid=(nf, nt),
        in_specs=[tile(0),
                  pl.BlockSpec((None, 8, tf), lambda j, i: (0, jnp.maximum((nt - 1 - i) * hb - 1, 0), j)),
                  tile(1), pl.BlockSpec((tm, tf), lambda j, i: (nt - 1 - i, j)),
                  pl.BlockSpec((3, tf), lambda j, i: (0, j)), pl.BlockSpec((1, tf), lambda j, i: (0, j))],
        out_specs=[pl.BlockSpec((2, tm, tf), lambda j, i: (0, nt - 1 - i, j)), pl.BlockSpec((8, tf), lambda j, i: (0, j))],
        out_shape=[jax.ShapeDtypeStruct((2, S, F), BF16), jax.ShapeDtypeStruct((8, F), F32)],
        scratch_shapes=[pltpu.VMEM((8, tf), F32)],
        compiler_params=_cp("parallel", "arbitrary"))(up, up, up, dz, cw, cb)
    return d_up, sums


def _log_sigmoid(x):
    return jnp.minimum(x, 0.0) - jnp.log(1.0 + jnp.exp(-jnp.abs(x)))


def _tri_ones(n, upper):
    r = lax.broadcasted_iota(jnp.int32, (n, n), 0)
    c = lax.broadcasted_iota(jnp.int32, (n, n), 1)
    return jnp.where((c >= r) if upper else (c <= r), 1.0, 0.0).astype(F32)


def _fgate_fwd(fl, bf, *, name):
    S, W = fl.shape
    tb = _tile(S, 256, 8)

    def body(fl_ref, b_ref, o_ref, carry_ref):
        @pl.when(pl.program_id(0) == 0)
        def _():
            carry_ref[...] = jnp.zeros_like(carry_ref)

        lf = _log_sigmoid(fl_ref[...] + b_ref[...])
        cs = jnp.dot(_tri_ones(tb, False), lf, precision=lax.Precision.HIGHEST, preferred_element_type=F32) + carry_ref[0:1, :]
        o_ref[...] = cs
        carry_ref[...] = jnp.broadcast_to(cs[tb - 1:tb, :], carry_ref.shape)

    blk = pl.BlockSpec((tb, W), lambda i: (i, 0))
    return pl.pallas_call(body, name=name, grid=(S // tb,), in_specs=[blk, pl.BlockSpec((1, W), lambda i: (0, 0))], out_specs=blk,
                          out_shape=jax.ShapeDtypeStruct((S, W), F32), scratch_shapes=[pltpu.VMEM((8, W), F32)],
                          compiler_params=_cp("arbitrary"))(fl, bf)


def _fgate_bwd(dF, fl, bf, *, name):
    S, W = fl.shape
    tb = _tile(S, 256, 8)
    nb = S // tb

    def body(d_ref, fl_ref, b_ref, o_ref, s_ref, carry_ref):
        @pl.when(pl.program_id(0) == 0)
        def _():
            carry_ref[...] = jnp.zeros_like(carry_ref)
            s_ref[...] = jnp.zeros_like(s_ref)

        rc = jnp.dot(_tri_ones(tb, True), d_ref[...], precision=lax.Precision.HIGHEST, preferred_element_type=F32) + carry_ref[0:1, :]
        carry_ref[...] = jnp.broadcast_to(rc[0:1, :], carry_ref.shape)
        dfl = rc * jax.nn.sigmoid(-(fl_ref[...] + b_ref[...]))
        o_ref[...] = dfl
        s_ref[...] += jnp.sum(dfl, axis=0, keepdims=True)

    blk = pl.BlockSpec((tb, W), lambda i: (nb - 1 - i, 0))
    vec = pl.BlockSpec((1, W), lambda i: (0, 0))
    return pl.pallas_call(body, name=name, grid=(nb,), in_specs=[blk, blk, vec], out_specs=[blk, vec],
                          out_shape=[jax.ShapeDtypeStruct((S, W), F32), jax.ShapeDtypeStruct((1, W), F32)],
                          scratch_shapes=[pltpu.VMEM((8, W), F32)], compiler_params=_cp("arbitrary"))(dF, fl, bf)


NT_DIMS = (((1,), (1,)), ((), ()))
TN_DIMS = (((0,), (0,)), ((), ()))
HP = HEADS // 2
Q_SCALE = HEAD_DIM ** -0.5


def _head_mask(x, hh):
    lanes = lax.broadcasted_iota(jnp.int32, x.shape, 1)
    return jnp.where((lanes >= hh * HEAD_DIM) & (lanes < (hh + 1) * HEAD_DIM), x, jnp.zeros_like(x))


def _lanes_from(cols, shape):
    lanes = lax.broadcasted_iota(jnp.int32, shape, 1)
    out = jnp.zeros(shape, F32)
    for i, cvec in enumerate(cols):
        out = jnp.where(lanes == i, cvec, out)
    return out


AUG_ROWS = 16


def _qa_pair(qt, aug):
    fill = jnp.zeros((HEAD_DIM - AUG_ROWS, qt.shape[1]), qt.dtype)
    return [jnp.concatenate([qt[0:HEAD_DIM], aug[0], fill], axis=0), jnp.concatenate([aug[1], fill, qt[HEAD_DIM:]], axis=0)]


def _rows_of_head(xt, hh):
    rows = lax.broadcasted_iota(jnp.int32, xt.shape, 0)
    return jnp.where((rows >= hh * HEAD_DIM) & (rows < (hh + 1) * HEAD_DIM), xt, jnp.zeros_like(xt))


def _put_rows(ref, rows):
    for i, r in enumerate(rows):
        ref[i:i + 1, :] = r
    ref[len(rows):, :] = jnp.zeros((ref.shape[0] - len(rows), ref.shape[1]), ref.dtype)


def _diag_mask_t(s):
    rows = lax.broadcasted_iota(jnp.int32, s.shape, 0)
    cols = lax.broadcasted_iota(jnp.int32, s.shape, 1)
    return jnp.where(rows <= cols, s, NEG)


def _foxt_fwd(qkvT, ka, vtb, augq, *, tb, name, comm=()):
    S = qkvT.shape[1]
    nb = S // tb

    def body(q_ref, ka_ref, v_ref, aq_ref, o_ref, lse_ref, m0, m1, l0, l1, acc0, acc1):
        qi = pl.program_id(1)
        qa = _qa_pair(q_ref[...], aq_ref[...])
        state = ((m0, l0, acc0), (m1, l1, acc1))
        for m_sc, l_sc, acc_sc in state:
            m_sc[...] = jnp.full_like(m_sc, NEG)
            l_sc[...] = jnp.zeros_like(l_sc)
            acc_sc[...] = jnp.zeros_like(acc_sc)

        def block(kj, masked):
            k0 = pl.multiple_of(kj * tb, tb)
            logits = [jnp.dot(ka_ref[hh, pl.ds(k0, tb), :], qa[hh], preferred_element_type=F32) for hh in range(2)]
            updates = []
            for hh, (m_sc, l_sc, acc_sc) in enumerate(state):
                s = _diag_mask_t(logits[hh]) if masked else logits[hh]
                m_old = m_sc[...]
                m_new = jnp.maximum(m_old, jnp.max(s, axis=0, keepdims=True))
                alpha = jnp.exp(m_old - m_new)
                p = jnp.exp(s - m_new)
                l_sc[...] = alpha * l_sc[...] + jnp.sum(p, axis=0, keepdims=True)
                m_sc[...] = m_new
                vt = v_ref[kj, HEAD_DIM * hh:HEAD_DIM * (hh + 1), :]
                updates.append((alpha, jnp.dot(vt, p.astype(BF16), preferred_element_type=F32)))
            for (alpha, pv), (_, _, acc_sc) in zip(updates, state):
                acc_sc[...] = alpha * acc_sc[...] + pv

        def off_diagonal(kj, c):
            block(kj, False)
            return c

        lax.fori_loop(0, qi, off_diagonal, 0)
        block(qi, True)
        for hh, (m_sc, l_sc, acc_sc) in enumerate(state):
            o_ref[HEAD_DIM * hh:HEAD_DIM * (hh + 1), :] = (acc_sc[...] / l_sc[...]).astype(o_ref.dtype)
        _put_rows(lse_ref, [m_sc[...] + jnp.log(l_sc[...]) for m_sc, l_sc, _ in state])

    return _call(
        body, name=name, grid=(HP, nb), comm=comm, sem=("parallel", "parallel"), args=(qkvT, ka, vtb, augq),
        in_specs=[pl.BlockSpec((128, tb), lambda hp, qi: (hp, qi)),
                  pl.BlockSpec((None, 2, S, 128), lambda hp, qi: (hp, 0, 0, 0)),
                  pl.BlockSpec((None, nb, 128, tb), lambda hp, qi: (hp, 0, 0, 0)),
                  pl.BlockSpec((None, 2, AUG_ROWS, tb), lambda hp, qi: (hp, 0, 0, qi))],
        out_specs=[pl.BlockSpec((128, tb), lambda hp, qi: (hp, qi)), pl.BlockSpec((None, 8, tb), lambda hp, qi: (hp, 0, qi))],
        out_shape=[jax.ShapeDtypeStruct((D_MODEL, S), BF16), jax.ShapeDtypeStruct((HP, 8, S), F32)],
        scratch_shapes=[pltpu.VMEM((1, tb), F32)] * 4 + [pltpu.VMEM((HEAD_DIM, tb), F32)] * 2)


def _foxt_bwd(qkvT, qkv, ka, qtb, dotb, do, augqb, lseb, otb, *, tb, name, comm=()):
    S = qkv.shape[0]
    nb = S // tb

    def body(ka_ref, v_ref, kt_ref, qt_ref, dot_ref, qn_ref, dn_ref, aq_ref, ls_ref, ot_ref,
             dq_ref, dk_ref, dv_ref, df_ref, dr_ref, dq_sc, dk_sc, dv_sc, ds_sc, dr_sc):
        kj = pl.program_id(1)

        @pl.when(kj == 0)
        def _():
            dq_sc[...] = jnp.zeros_like(dq_sc)
            dr_sc[...] = jnp.zeros_like(dr_sc)

        dk_sc[...] = jnp.zeros_like(dk_sc)
        dv_sc[...] = jnp.zeros_like(dv_sc)
        ds_sc[...] = jnp.zeros_like(ds_sc)
        v2 = v_ref[...]
        kth = [kt_ref[HEAD_DIM * hh:HEAD_DIM * (hh + 1), :] for hh in range(2)]

        def block(qi, masked):
            q0 = pl.multiple_of(qi * tb, tb)
            qa = _qa_pair(qt_ref[qi], aq_ref[qi])
            dot, ls = dot_ref[qi], ls_ref[qi]
            prod = ot_ref[qi].astype(F32) * dot.astype(F32)
            delta = [jnp.sum(prod[HEAD_DIM * hh:HEAD_DIM * (hh + 1)], axis=0, keepdims=True) for hh in range(2)]
            qn, dn = qn_ref[pl.ds(q0, tb), :], dn_ref[pl.ds(q0, tb), :]
            for hh in range(2):
                s = jnp.dot(ka_ref[hh], qa[hh], preferred_element_type=F32)
                if masked:
                    s = _diag_mask_t(s)
                p = jnp.exp(s - ls[hh:hh + 1, :])
                dp = jnp.dot(v2, _rows_of_head(dot, hh), preferred_element_type=F32)
                ds = p * (dp - delta[hh])
                dsb = ds.astype(BF16)
                dv_sc[...] += jnp.dot(p.astype(BF16), _head_mask(dn, hh), preferred_element_type=F32)
                dk_sc[...] += jnp.dot(dsb, _head_mask(qn, hh), preferred_element_type=F32)
                dq_sc[qi, HEAD_DIM * hh:HEAD_DIM * (hh + 1), :] += jnp.dot(kth[hh], dsb, preferred_element_type=F32)
                part = ds[:, 0:128]
                for j in range(1, tb // 128):
                    part = part + ds[:, 128 * j:128 * (j + 1)]
                ds_sc[hh] += part
                dr_sc[qi, hh:hh + 1, :] += jnp.sum(ds, axis=0, keepdims=True)

        def off_diagonal(i, c):
            block(kj + 1 + i, False)
            return c

        block(kj, True)
        lax.fori_loop(0, nb - 1 - kj, off_diagonal, 0)
        dk_ref[...] = dk_sc[...].astype(dk_ref.dtype)
        dv_ref[...] = dv_sc[...].astype(dv_ref.dtype)
        df_ref[...] = _lanes_from([-jnp.sum(ds_sc[hh], axis=1, keepdims=True) for hh in range(2)], (tb, 128))

        @pl.when(kj == nb - 1)
        def _():
            dq_ref[...] = (dq_sc[...] * Q_SCALE).astype(dq_ref.dtype)
            dr_ref[...] = dr_sc[...]

    resident = lambda rows: pl.BlockSpec((None, nb) + rows, lambda hp, kj: (hp,) + (0,) * (len(rows) + 1))
    whole = pl.BlockSpec((S, 128), lambda hp, kj: (0, hp))
    kblk = lambda off: pl.BlockSpec((tb, 128), lambda hp, kj: (kj, off + hp))
    return _call(
        body, name=name, grid=(HP, nb), comm=comm, sem=("parallel", "arbitrary"),
        args=(ka, qkv, qkvT, qtb, dotb, qkv, do, augqb, lseb, otb),
        in_specs=[pl.BlockSpec((None, 2, tb, 128), lambda hp, kj: (hp, 0, kj, 0)), kblk(2 * HP),
                  pl.BlockSpec((128, tb), lambda hp, kj: (HP + hp, kj)),
                  resident((128, tb)), resident((128, tb)), whole, whole, resident((2, AUG_ROWS, tb)), resident((8, tb)),
                  resident((128, tb))],
        out_specs=[resident((128, tb)), kblk(0), kblk(0), pl.BlockSpec((None, tb, 128), lambda hp, kj: (hp, kj, 0)),
                   resident((8, tb))],
        out_shape=[jax.ShapeDtypeStruct((HP, nb, 128, tb), BF16), jax.ShapeDtypeStruct((S, D_MODEL), BF16),
                   jax.ShapeDtypeStruct((S, D_MODEL), BF16), jax.ShapeDtypeStruct((HP, S, 128), F32),
                   jax.ShapeDtypeStruct((HP, nb, 8, tb), F32)],
        scratch_shapes=[pltpu.VMEM((nb, 128, tb), F32), pltpu.VMEM((tb, 128), F32), pltpu.VMEM((tb, 128), F32),
                        pltpu.VMEM((2, tb, 128), F32), pltpu.VMEM((nb, 8, tb), F32)])


def _split3(x):
    rnd = lambda v: lax.reduce_precision(v, exponent_bits=8, mantissa_bits=7)
    hi = rnd(x)
    mid = rnd(x - hi)
    lo = rnd(x - hi - mid)
    return hi.astype(BF16), mid.astype(BF16), lo.astype(BF16)


def _blocked(xt, nb, tb):
    lead = xt.shape[:-1]
    x = xt.reshape(lead + (nb, tb))
    return jnp.moveaxis(x, -2, 1)


def _aug_rows(bias):
    ones = jnp.ones(bias.shape, BF16)
    zeros = jnp.zeros(bias.shape, BF16)
    return jnp.stack(list(_split3(bias)) + [ones] * 3 + [zeros] * (AUG_ROWS - 6), axis=2)


def _fox_keys(qkv, fcol, *, name):
    S = qkv.shape[0]
    tr = _tile(S, 2048, 16)

    def body(k_ref, f_ref, o_ref):
        hp = pl.program_id(1)
        k2 = k_ref[...]
        lanes = lax.broadcasted_iota(jnp.int32, (tr, 128), 1)
        for hh in range(2):
            nf = -jnp.sum(jnp.where(lanes == 2 * hp + hh, f_ref[...], 0.0), axis=1, keepdims=True)
            hi = nf.astype(BF16).astype(F32)
            mid = (nf - hi).astype(BF16).astype(F32)
            lo = nf - hi - mid
            base = HEAD_DIM * (1 - hh)
            aug = jnp.where((lanes >= base) & (lanes < base + 3), 1.0, 0.0)
            for j, piece in enumerate((hi, mid, lo)):
                aug = jnp.where(lanes == base + 3 + j, piece, aug)
            mine = (lanes >= HEAD_DIM * hh) & (lanes < HEAD_DIM * (hh + 1))
            o_ref[hh] = jnp.where(mine, k2, aug.astype(BF16))

    return pl.pallas_call(
        body, name=name, grid=(S // tr, HP),
        in_specs=[pl.BlockSpec((tr, 128), lambda i, hp: (i, HP + hp)), pl.BlockSpec((tr, 128), lambda i, hp: (i, 0))],
        out_specs=pl.BlockSpec((None, 2, tr, 128), lambda i, hp: (hp, 0, i, 0)),
        out_shape=jax.ShapeDtypeStruct((HP, 2, S, 128), BF16), compiler_params=_cp("parallel", "parallel"))(qkv, fcol)


HALF = S5_SUB * S5_P
NCB = 2 * HALF // 128
RE, IM = slice(0, NCB // 2), slice(NCB // 2, NCB)


def _gelu(x):
    return 0.5 * x * (1.0 + jnp.tanh(GELU_K * (x + 0.044715 * x * x * x)))


def _gelu_grad(x):
    t = jnp.tanh(GELU_K * (x + 0.044715 * x * x * x))
    return 0.5 * (1.0 + t) + 0.5 * x * (1.0 - t * t) * GELU_K * (1.0 + 3.0 * 0.044715 * x * x)


def _s5_put(buf, s, val, lt):
    for cb in range(NCB):
        buf[cb, pl.ds(s, lt, stride=S5_SUB), :] = val[:, 128 * cb:128 * (cb + 1)]


def _s5_get(buf, s, lt):
    return jnp.concatenate([buf[cb, pl.ds(s, lt, stride=S5_SUB), :] for cb in range(NCB)], axis=1)


def _s5_project_in(u_ref, wb_ref, buf, lt):
    for s in range(S5_SUB):
        _s5_put(buf, s, jnp.dot(u_ref[:, 128 * s:128 * (s + 1)], wb_ref[s], preferred_element_type=F32), lt)


def _s5_scan(buf, lam_ref, h0, lt):
    a_re, a_im = lam_ref[RE], lam_ref[IM]

    def step(t, carry):
        hr, hi = carry
        r0 = pl.multiple_of(t * S5_SUB, S5_SUB)
        nr = a_re * hr - a_im * hi + buf[RE, pl.ds(r0, S5_SUB), :]
        ni = a_re * hi + a_im * hr + buf[IM, pl.ds(r0, S5_SUB), :]
        buf[RE, pl.ds(r0, S5_SUB), :] = nr
        buf[IM, pl.ds(r0, S5_SUB), :] = ni
        return nr, ni

    return lax.fori_loop(0, lt, step, (h0[RE], h0[IM]), unroll=4)


def _s5_fwd(u, wb, wc, lam, dskip, *, lt, name, comm=()):
    S, D = u.shape
    nt = S // lt

    def body(u_ref, wb_ref, wc_ref, lam_ref, d_ref, yp_ref, yg_ref, st_ref, buf, h_sc):
        @pl.when(pl.program_id(0) == 0)
        def _():
            h_sc[...] = jnp.zeros_like(h_sc)

        st_ref[...] = h_sc[...]
        _s5_project_in(u_ref, wb_ref, buf, lt)
        hr, hi = _s5_scan(buf, lam_ref, h_sc[...], lt)
        h_sc[RE] = hr
        h_sc[IM] = hi
        for s in range(S5_SUB):
            cols = slice(128 * s, 128 * (s + 1))
            hs = _s5_get(buf, s, lt).astype(BF16)
            yp = jnp.dot(hs, wc_ref[s], preferred_element_type=F32) + d_ref[:, cols] * u_ref[:, cols].astype(F32)
            yp_ref[:, cols] = yp.astype(yp_ref.dtype)
            yg_ref[:, cols] = _gelu(yp).astype(yg_ref.dtype)

    blk = pl.BlockSpec((lt, D), lambda i: (i, 0))
    full = lambda shp: pl.BlockSpec(shp, lambda i: (0,) * len(shp))
    state = (NCB, S5_SUB, 128)
    return _call(
        body, name=name, grid=(nt,), comm=comm, sem=("arbitrary",), args=(u, wb, wc, lam, dskip),
        in_specs=[blk, full(wb.shape), full(wc.shape), full(lam.shape), full(dskip.shape)],
        out_specs=[blk, blk, pl.BlockSpec((None,) + state, lambda i: (i, 0, 0, 0))],
        out_shape=[jax.ShapeDtypeStruct((S, D), BF16), jax.ShapeDtypeStruct((S, D), BF16), jax.ShapeDtypeStruct((nt,) + state, F32)],
        scratch_shapes=[pltpu.VMEM((NCB, lt * S5_SUB, 128), F32), pltpu.VMEM(state, F32)])


def _s5_bwd(u, dyp, st, wb, wc, lam, dskip, *, lt, name, comm=()):
    S, D = u.shape
    nt = S // lt

    def body(u_ref, dy_ref, st_ref, wb_ref, wc_ref, lam_ref, d_ref, du_ref, dwb_ref, dwc_ref, dlam_ref, hbuf, gbuf, g_sc):
        @pl.when(pl.program_id(0) == 0)
        def _():
            g_sc[...] = jnp.zeros_like(g_sc)
            dwb_ref[...] = jnp.zeros_like(dwb_ref)
            dwc_ref[...] = jnp.zeros_like(dwc_ref)
            dlam_ref[...] = jnp.zeros_like(dlam_ref)

        _s5_project_in(u_ref, wb_ref, hbuf, lt)
        _s5_scan(hbuf, lam_ref, st_ref[...], lt)
        for s in range(S5_SUB):
            dys = dy_ref[:, 128 * s:128 * (s + 1)]
            _s5_put(gbuf, s, lax.dot_general(dys, wc_ref[s], NT_DIMS, preferred_element_type=F32), lt)
        a_re, a_im = lam_ref[RE], lam_ref[IM]

        def one(t, carry, hp_re, hp_im):
            gr, gi, dar, dai = carry
            r0 = pl.multiple_of(t * S5_SUB, S5_SUB)
            nr = gbuf[RE, pl.ds(r0, S5_SUB), :] + a_re * gr + a_im * gi
            ni = gbuf[IM, pl.ds(r0, S5_SUB), :] + a_re * gi - a_im * gr
            gbuf[RE, pl.ds(r0, S5_SUB), :] = nr
            gbuf[IM, pl.ds(r0, S5_SUB), :] = ni
            return nr, ni, dar + nr * hp_re + ni * hp_im, dai + ni * hp_re - nr * hp_im

        def step(k, carry):
            t = lt - 1 - k
            p0 = pl.multiple_of((t - 1) * S5_SUB, S5_SUB)
            return one(t, carry, hbuf[RE, pl.ds(p0, S5_SUB), :], hbuf[IM, pl.ds(p0, S5_SUB), :])

        init = (g_sc[RE], g_sc[IM], dlam_ref[RE], dlam_ref[IM])
        carry = lax.fori_loop(0, lt - 1, step, init, unroll=2)
        gr, gi, dar, dai = one(0, carry, st_ref[RE], st_ref[IM])
        g_sc[RE] = gr
        g_sc[IM] = gi
        dlam_ref[RE] = dar
        dlam_ref[IM] = dai
        for s in range(S5_SUB):
            cols = slice(128 * s, 128 * (s + 1))
            gs = _s5_get(gbuf, s, lt).astype(BF16)
            hs = _s5_get(hbuf, s, lt).astype(BF16)
            us, dys = u_ref[:, cols], dy_ref[:, cols]
            du = lax.dot_general(gs, wb_ref[s], NT_DIMS, preferred_element_type=F32) + d_ref[:, cols] * dys.astype(F32)
            du_ref[:, cols] = du.astype(du_ref.dtype)
            dwb_ref[s] += lax.dot_general(us, gs, TN_DIMS, preferred_element_type=F32)
            dwc_ref[s] += lax.dot_general(hs, dys, TN_DIMS, preferred_element_type=F32)

    blk = pl.BlockSpec((lt, D), lambda i: (nt - 1 - i, 0))
    full = lambda shp: pl.BlockSpec(shp, lambda i: (0,) * len(shp))
    state = (NCB, S5_SUB, 128)
    words = pltpu.VMEM((NCB, lt * S5_SUB, 128), F32)
    return _call(
        body, name=name, grid=(nt,), comm=comm, sem=("arbitrary",), args=(u, dyp, st, wb, wc, lam, dskip),
        in_specs=[blk, blk, pl.BlockSpec((None,) + state, lambda i: (nt - 1 - i, 0, 0, 0)),
                  full(wb.shape), full(wc.shape), full(lam.shape), full(dskip.shape)],
        out_specs=[blk, full(wb.shape), full(wc.shape), full(lam.shape)],
        out_shape=[jax.ShapeDtypeStruct((S, D), BF16), jax.ShapeDtypeStruct(wb.shape, F32), jax.ShapeDtypeStruct(wc.shape, F32),
                   jax.ShapeDtypeStruct(lam.shape, F32)],
        scratch_shapes=[words, words, pltpu.VMEM(state, F32)])


def _s5_disc(lam_re, lam_im, log_dt, b_re, b_im):
    dt = jnp.exp(log_dt)[:, None]
    mag = jnp.exp(lam_re * dt)
    lb_re, lb_im = mag * jnp.cos(lam_im * dt), mag * jnp.sin(lam_im * dt)
    num_re = lb_re - 1.0
    den = lam_re * lam_re + lam_im * lam_im
    k_re = (num_re * lam_re + lb_im * lam_im) / den
    k_im = (lb_im * lam_re - num_re * lam_im) / den
    bb_re = k_re[..., None] * b_re - k_im[..., None] * b_im
    bb_im = k_re[..., None] * b_im + k_im[..., None] * b_re
    return lb_re, lb_im, bb_re, bb_im


def _s5_pack(lb_re, lb_im, bb_re, bb_im, c_re, c_im):
    eye = jnp.eye(S5_SUB, dtype=F32)
    tb = lambda x: jnp.einsum("sgcp,gh->sgchp", x.reshape(S5_SUB, S5_SUB, S5_P, S5_C).transpose(0, 1, 3, 2), eye).reshape(S5_SUB, 128, HALF)
    tc = lambda x: jnp.einsum("sgpc,gh->sgphc", x.reshape(S5_SUB, S5_SUB, S5_C, S5_P).transpose(0, 1, 3, 2), eye).reshape(S5_SUB, HALF, 128)
    wb = jnp.concatenate([tb(bb_re), tb(bb_im)], axis=2).astype(BF16)
    wc = jnp.concatenate([tc(c_re), -tc(c_im)], axis=1).astype(BF16)
    lam = jnp.concatenate([lb_re.reshape(S5_SUB, HALF), lb_im.reshape(S5_SUB, HALF)], axis=1)
    return wb, wc, lam.reshape(S5_SUB, NCB, 128).transpose(1, 0, 2)


def _s5_unpack(dwb, dwc, dlam):
    db = jnp.einsum("sgcrgp->rsgpc", dwb.reshape(S5_SUB, S5_SUB, S5_C, 2, S5_SUB, S5_P)).reshape(2, S5_G, S5_P, S5_C)
    dc = jnp.einsum("srgpgc->rsgcp", dwc.reshape(S5_SUB, 2, S5_SUB, S5_P, S5_SUB, S5_C)).reshape(2, S5_G, S5_C, S5_P)
    dlam = dlam.transpose(1, 0, 2).reshape(S5_SUB, 2 * HALF)
    return (dlam[:, :HALF].reshape(S5_G, S5_P), dlam[:, HALF:].reshape(S5_G, S5_P), db[0], db[1], dc[0], -dc[1])


def _glu_fwd(yg, gl, *, name):
    S, D = yg.shape
    tm = _tile(S, 512, 8)
    blk = pl.BlockSpec((tm, D), lambda i: (i, 0))

    def body(y_ref, g_ref, o_ref):
        o_ref[...] = (y_ref[...].astype(F32) * jax.nn.sigmoid(g_ref[...].astype(F32))).astype(o_ref.dtype)

    return pl.pallas_call(body, name=name, grid=(S // tm,), in_specs=[blk, blk], out_specs=blk,
                          out_shape=jax.ShapeDtypeStruct((S, D), BF16), compiler_params=_cp("parallel"))(yg, gl)


def _glu_bwd(dy2, yg, gl, *, name):
    S, D = yg.shape
    tm = _tile(S, 512, 8)
    blk = pl.BlockSpec((tm, D), lambda i: (i, 0))

    def body(d_ref, y_ref, g_ref, da_ref, dg_ref):
        d = d_ref[...].astype(F32)
        sg = jax.nn.sigmoid(g_ref[...].astype(F32))
        da_ref[...] = d * sg
        dg_ref[...] = (d * y_ref[...].astype(F32) * sg * (1.0 - sg)).astype(dg_ref.dtype)

    return pl.pallas_call(body, name=name, grid=(S // tm,), in_specs=[blk, blk, blk], out_specs=[blk, blk],
                          out_shape=[jax.ShapeDtypeStruct((S, D), F32), jax.ShapeDtypeStruct((S, D), BF16)],
                          compiler_params=_cp("parallel"))(dy2, yg, gl)


def _gelu_bwd(da, db, ypre, u, *, name):
    S, D = ypre.shape
    tm = _tile(S, 512, 8)
    blk = pl.BlockSpec((tm, D), lambda i: (i, 0))
    vec = pl.BlockSpec((1, D), lambda i: (0, 0))

    def body(a_ref, b_ref, y_ref, u_ref, o_ref, s_ref):
        @pl.when(pl.program_id(0) == 0)
        def _():
            s_ref[...] = jnp.zeros_like(s_ref)

        dy = (a_ref[...] + b_ref[...]) * _gelu_grad(y_ref[...].astype(F32))
        o_ref[...] = dy.astype(o_ref.dtype)
        s_ref[...] += jnp.sum(dy * u_ref[...].astype(F32), axis=0, keepdims=True)

    return pl.pallas_call(body, name=name, grid=(S // tm,), in_specs=[blk, blk, blk, blk], out_specs=[blk, vec],
                          out_shape=[jax.ShapeDtypeStruct((S, D), BF16), jax.ShapeDtypeStruct((1, D), F32)],
                          compiler_params=_cp("arbitrary"))(da, db, ypre, u)


def _my_index():
    return 4 * lax.axis_index("x") + 2 * lax.axis_index("y") + lax.axis_index("c")


def _comm_plan(payloads):
    n = len(payloads)
    hbm = pl.BlockSpec(memory_space=pl.ANY)
    shapes = [jax.ShapeDtypeStruct((N_DEV,) + (x.shape if g else x.shape[1:]), x.dtype) for x, g in payloads]
    sems = [pltpu.SemaphoreType.DMA((n, N_DEV - 1)), pltpu.SemaphoreType.DMA((n, N_DEV - 1)), pltpu.SemaphoreType.DMA((n,))] if n else []
    return [hbm] * n, [hbm] * n, shapes, sems


def _comm_copies(x_refs, o_refs, gathers, send_sems, recv_sems, local_sems):
    mx, my, mc = lax.axis_index("x"), lax.axis_index("y"), lax.axis_index("c")
    me = 4 * mx + 2 * my + mc
    copies = []
    for i, (x_ref, o_ref, gather) in enumerate(zip(x_refs, o_refs, gathers)):
        src = (lambda j, r=x_ref: r) if gather else (lambda j, r=x_ref: r.at[j])
        copies.append(pltpu.make_async_copy(src(me), o_ref.at[me], local_sems.at[i]))
        for k in range(1, N_DEV):
            px, py, pc = mx ^ (k >> 2), my ^ ((k >> 1) & 1), mc ^ (k & 1)
            copies.append(pltpu.make_async_remote_copy(src(4 * px + 2 * py + pc), o_ref.at[me], send_sems.at[i, k - 1],
                                                       recv_sems.at[i, k - 1], device_id=(px, py, pc), device_id_type=MESH_ID))
    return copies


def _call(body, *, name, grid, in_specs, out_specs, out_shape, scratch_shapes, sem, args, comm=()):
    if not comm:
        outs = pl.pallas_call(body, name=name, grid=grid, in_specs=in_specs, out_specs=out_specs, out_shape=out_shape,
                              scratch_shapes=scratch_shapes, compiler_params=_cp(*sem))(*args)
        return outs, ()
    n, n_in, n_out = len(comm), len(in_specs), len(out_specs)
    c_in, c_out, c_shapes, c_sems = _comm_plan(comm)
    gathers = [g for _, g in comm]

    def wrapped(*refs):
        own_in, x_refs = refs[:n_in], refs[n_in:n_in + n]
        own_out, o_refs = refs[n_in + n:n_in + n + n_out], refs[n_in + n + n_out:n_in + 2 * n + n_out]
        own_scratch, sems = refs[n_in + 2 * n + n_out:-3], refs[-3:]
        ids = [pl.program_id(a) for a in range(len(grid))]
        first = functools.reduce(jnp.logical_and, [i == 0 for i in ids])
        last = functools.reduce(jnp.logical_and, [i == g - 1 for i, g in zip(ids, grid)])

        @pl.when(first)
        def _():
            for cp in _comm_copies(x_refs, o_refs, gathers, *sems):
                cp.start()

        body(*own_in, *own_out, *own_scratch)

        @pl.when(last)
        def _():
            for cp in _comm_copies(x_refs, o_refs, gathers, *sems):
                cp.wait()

    outs = pl.pallas_call(wrapped, name=name, grid=grid, in_specs=list(in_specs) + c_in, out_specs=list(out_specs) + c_out,
                          out_shape=list(out_shape) + c_shapes, scratch_shapes=list(scratch_shapes) + c_sems,
                          compiler_params=_cp(*["arbitrary"] * len(grid)))(*args, *[x for x, _ in comm])
    return outs[:n_out], outs[n_out:]


def _exchange_many(payloads, *, name):
    n = len(payloads)
    in_specs, out_specs, shapes, sems = _comm_plan(payloads)
    gathers = [g for _, g in payloads]

    def body(*refs):
        copies = _comm_copies(refs[:n], refs[n:2 * n], gathers, *refs[2 * n:])
        for cp in copies:
            cp.start()
        for cp in copies:
            cp.wait()

    return pl.pallas_call(body, name=name, in_specs=in_specs, out_specs=out_specs, out_shape=shapes, scratch_shapes=sems,
                          compiler_params=pltpu.CompilerParams(has_side_effects=True))(*[x for x, _ in payloads])


def _exchange(x, *, gather, name):
    return _exchange_many([(x, gather)], name=name)[0]


def _sum8(x8, *, name):
    _, R, C = x8.shape
    tr = _tile(R, 256, 16)

    def body(x_ref, o_ref):
        acc = x_ref[0].astype(F32)
        for j in range(1, N_DEV):
            acc = acc + x_ref[j].astype(F32)
        o_ref[...] = acc

    return pl.pallas_call(body, name=name, grid=(R // tr,), in_specs=[pl.BlockSpec((N_DEV, tr, C), lambda i: (0, i, 0))],
                          out_specs=pl.BlockSpec((tr, C), lambda i: (i, 0)), out_shape=jax.ShapeDtypeStruct((R, C), F32),
                          compiler_params=_cp("parallel"))(x8)


def _pack(arrs, dtype, row_mult):
    flat = jnp.concatenate([a.reshape(-1).astype(dtype) for a in arrs])
    n = flat.shape[0]
    rows = -(-n // 1024)
    rows = -(-rows // row_mult) * row_mult
    return jnp.pad(flat, (0, rows * 1024 - n)).reshape(rows, 1024)


def _unpack(buf, shapes):
    lead = buf.shape[:-2]
    flat = buf.reshape(lead + (-1,))
    out, off = [], 0
    for s in shapes:
        n = math.prod(s)
        out.append(flat[..., off:off + n].reshape(lead + tuple(s)))
        off += n
    return out


def _adaln_fwd(c_all, w, b, *, name):
    nk, D, n = w.shape

    def body(c_ref, w_ref, b_ref, o_ref):
        cv = c_ref[...]
        sc = (cv * jax.nn.sigmoid(cv)).astype(BF16)
        o_ref[...] = jnp.dot(sc, w_ref[...].astype(BF16), preferred_element_type=F32) + b_ref[...]

    return pl.pallas_call(body, name=name, grid=(nk,),
                          in_specs=[pl.BlockSpec((N_DEV, D), lambda k: (0, 0)), pl.BlockSpec((None, D, n), lambda k: (k, 0, 0)),
                                    pl.BlockSpec((None, 1, n), lambda k: (k, 0, 0))],
                          out_specs=pl.BlockSpec((None, N_DEV, n), lambda k: (k, 0, 0)),
                          out_shape=jax.ShapeDtypeStruct((nk, N_DEV, n), F32), compiler_params=_cp("parallel"))(c_all, w, b)


def _adaln_bwd(c_all, dmod, *, name):
    nk, _, n = dmod.shape
    D = c_all.shape[1]

    def body(c_ref, d_ref, dw_ref, db_ref):
        cv = c_ref[...]
        sc = (cv * jax.nn.sigmoid(cv)).astype(BF16).astype(F32)
        dm = d_ref[...]
        dw_ref[...] = lax.dot_general(sc, dm.astype(BF16).astype(F32), TN_DIMS, precision=lax.Precision.HIGHEST,
                                      preferred_element_type=F32)
        db_ref[...] = jnp.sum(dm, axis=0, keepdims=True)

    return pl.pallas_call(body, name=name, grid=(nk,),
                          in_specs=[pl.BlockSpec((N_DEV, D), lambda k: (0, 0)), pl.BlockSpec((None, N_DEV, n), lambda k: (k, 0, 0))],
                          out_specs=[pl.BlockSpec((None, D, n), lambda k: (k, 0, 0)), pl.BlockSpec((None, 1, n), lambda k: (k, 0, 0))],
                          out_shape=[jax.ShapeDtypeStruct((nk, D, n), F32), jax.ShapeDtypeStruct((nk, 1, n), F32)],
                          compiler_params=_cp("parallel"))(c_all, dmod)


def _adamw(w, g, m, v, *, name):
    R, C = w.shape
    tr = _tile(R, 256, 8)
    blk = pl.BlockSpec((tr, C), lambda i: (i, 0))
    c1 = 1.0 / (1.0 - ADAM_B1 ** ADAM_STEP)
    c2 = 1.0 / (1.0 - ADAM_B2 ** ADAM_STEP)

    def body(w_ref, g_ref, m_ref, v_ref, d_ref, nm_ref, nv_ref):
        gv = g_ref[...]
        nm = ADAM_B1 * m_ref[...] + (1.0 - ADAM_B1) * gv
        nv = ADAM_B2 * v_ref[...] + (1.0 - ADAM_B2) * (gv * gv)
        nm_ref[...] = nm
        nv_ref[...] = nv
        d_ref[...] = -ADAM_LR * ((nm * c1) / (jnp.sqrt(nv * c2) + ADAM_EPS) + ADAM_WD * w_ref[...])

    sh = jax.ShapeDtypeStruct((R, C), F32)
    return pl.pallas_call(body, name=name, grid=(R // tr,), in_specs=[blk] * 4, out_specs=[blk] * 3, out_shape=[sh] * 3,
                          compiler_params=_cp("parallel"))(w, g, m, v)


def _adamw_nd(w, g, m, v, *, name):
    shp = w.shape
    two = (-1, shp[-1]) if w.ndim > 1 else (1, -1)
    outs = _adamw(w.reshape(two), g.reshape(two), m.reshape(two), v.reshape(two), name=name)
    return [o.reshape(shp) for o in outs]


S5_TILE = 256
ATTN_BLOCK = 512
BIG = ("s5_w_in", "s5_w_glu", "s5_w_out", "fox_w_in", "fox_w_out", "ffn_w_up", "ffn_w_down")
WEIGHTS = ("norm_g", "ada_w", "ada_b", "s5_w_in", "s5_lam_re", "s5_lam_im", "s5_log_dt", "s5_b_re", "s5_b_im", "s5_c_re",
           "s5_c_im", "s5_d", "s5_w_glu", "s5_w_out", "fox_w_in", "fox_b_f", "fox_w_out", "ffn_w_up", "ffn_conv_w",
           "ffn_conv_b", "ffn_w_down", "final_g")
REPLICATED = ("s5_lam_re", "s5_lam_im", "s5_log_dt", "s5_b_re", "s5_b_im", "s5_c_re", "s5_c_im", "s5_d", "fox_b_f",
              "ffn_conv_b", "final_g")


S5_MATS = ("s5_w_in", "s5_w_glu", "s5_w_out")


def _row_group(arrs):
    return jnp.concatenate([a.reshape(-1, D_MODEL) for a in arrs], axis=0)


def _cols_join(g):
    return jnp.concatenate([g[j] for j in range(N_DEV)], axis=1)


def _cols_split(full):
    n = full.shape[1] // N_DEV
    return jnp.stack([full[:, j * n:(j + 1) * n] for j in range(N_DEV)])


def kernel(x, c, norm_g, ada_w, ada_b, s5_w_in, s5_lam_re, s5_lam_im, s5_log_dt, s5_b_re, s5_b_im, s5_c_re, s5_c_im, s5_d, s5_w_glu, s5_w_out, fox_w_in, fox_b_f, fox_w_out, ffn_w_up, ffn_conv_w, ffn_conv_b, ffn_w_down, final_g, loss_target, m_norm_g, m_ada_w, m_ada_b, m_s5_w_in, m_s5_lam_re, m_s5_lam_im, m_s5_log_dt, m_s5_b_re, m_s5_b_im, m_s5_c_re, m_s5_c_im, m_s5_d, m_s5_w_glu, m_s5_w_out, m_fox_w_in, m_fox_b_f, m_fox_w_out, m_ffn_w_up, m_ffn_conv_w, m_ffn_conv_b, m_ffn_w_down, m_final_g, v_norm_g, v_ada_w, v_ada_b, v_s5_w_in, v_s5_lam_re, v_s5_lam_im, v_s5_log_dt, v_s5_b_re, v_s5_b_im, v_s5_c_re, v_s5_c_im, v_s5_d, v_s5_w_glu, v_s5_w_out, v_fox_w_in, v_fox_b_f, v_fox_w_out, v_ffn_w_up, v_ffn_conv_w, v_ffn_conv_b, v_ffn_w_down, v_final_g):
    args = dict(locals())
    W = {n: args[n] for n in WEIGHTS}
    M = {n: args["m_" + n] for n in WEIGHTS}
    V = {n: args["v_" + n] for n in WEIGHTS}
    D, F = D_MODEL, D_FF
    me = _my_index()
    h0 = x[0]
    S = h0.shape[0]
    lt = min(S5_TILE, S)
    tb = min(ATTN_BLOCK, S)
    n_g, n_cw = norm_g.size, ffn_conv_w.size

    g0 = _exchange(_pack([c, norm_g, ffn_conv_w], F32, 8), gather=True, name="gather_small_in")
    c_all, ng_all, cw_all = _unpack(g0, [(D,), norm_g.shape, ffn_conv_w.shape])
    ng_full = ng_all.transpose(1, 2, 0, 3).reshape(2, 2, D)
    cw_full = cw_all.transpose(1, 2, 0, 3).reshape(2, 3, F)

    ncol = ada_w.shape[-1]
    modp = _adaln_fwd(c_all, ada_w.reshape(4, D, ncol), ada_b.reshape(4, 1, ncol), name="adaln_fwd")
    g1 = _exchange(_pack([modp], F32, 8), gather=True, name="gather_adaln")
    (mod_all,) = _unpack(g1, [modp.shape])
    mod = lax.dynamic_index_in_dim(mod_all, me, axis=2, keepdims=False).transpose(1, 0, 2).reshape(4, 3 * D)
    shift = [mod[k:k + 1, :D] for k in range(4)]
    scale = [mod[k:k + 1, D:2 * D] for k in range(4)]
    gate = [mod[k:k + 1, 2 * D:] for k in range(4)]
    gain = [_row(ng_full[k // 2, k % 2]) for k in range(4)]

    b16 = lambda a: a.astype(BF16)
    rows = D // N_DEV
    full = {"s5_w_in": _exchange(b16(s5_w_in[0]), gather=True, name="gather_s5_w_in").reshape(D, D)}
    bf_pad = jnp.pad(fox_b_f, ((0, 0), (0, 128 - HEADS)))

    def ffn_fwd(h, k, layer, comm=()):
        hn = _modulate(h, gain[k], shift[k], scale[k], name=f"modulate{k}")
        up = _mm(hn, full[f"ffn_w_up{layer}"], name=f"ffn_up{layer}", o_h=True, out_dtype=BF16, tn=1408, comm=comm)
        up, exchanged = up if comm else (up, ())
        z = _conv_gate_fwd(up, cw_full[layer], ffn_conv_b[layer:layer + 1], name=f"conv_gate{layer}")
        m, h_out = _mm(z, full[f"ffn_w_down{layer}"], name=f"ffn_down{layer}", tn=1024, res=h, gate=gate[k])
        return h_out, (hn, up, z, m), exchanged

    lb_re, lb_im, bb_re, bb_im = _s5_disc(s5_lam_re[0], s5_lam_im[0], s5_log_dt[0], s5_b_re[0], s5_b_im[0])
    wb, wc, lam = _s5_pack(lb_re, lb_im, bb_re, bb_im, s5_c_re[0], s5_c_im[0])
    hn0 = _modulate(h0, gain[0], shift[0], scale[0], name="modulate0")
    u = _mm(hn0, full["s5_w_in"], name="s5_in", out_dtype=BF16)
    (ypre, yg, st), (g_up0, g_rows0) = _s5_fwd(
        u, wb, wc, lam, s5_d, lt=lt, name="s5_scan",
        comm=[(b16(ffn_w_up[0]), True), (_row_group([b16(s5_w_glu), b16(s5_w_out), b16(ffn_w_down[0])]), True)])
    full["ffn_w_up0"] = _cols_join(g_up0)
    full["s5_w_glu"] = g_rows0[:, :rows].reshape(D, D)
    full["s5_w_out"] = g_rows0[:, rows:2 * rows].reshape(D, D)
    full["ffn_w_down0"] = g_rows0[:, 2 * rows:].reshape(F, D)
    gl = _mm(yg, full["s5_w_glu"], name="s5_glu", out_dtype=BF16)
    y2 = _glu_fwd(yg, gl, name="s5_glu_gate")
    m0, h1 = _mm(y2, full["s5_w_out"], name="s5_out", tn=1024, res=h0, gate=gate[0])
    h2, (hn1, up0, z0, m1), (g_fox, g_fox_out) = ffn_fwd(h1, 1, 0, comm=[(b16(fox_w_in[0]), True), (b16(fox_w_out[0]), True)])
    full["fox_w_in"] = _cols_join(g_fox)
    full["fox_w_out"] = g_fox_out.reshape(D, D)
    w_proj = jnp.pad(full["fox_w_in"], ((0, 0), (0, 3 * D + 128 - full["fox_w_in"].shape[1])))
    w_qkv, w_f = w_proj[:, :3 * D], w_proj[:, 3 * D:]

    hn2 = _modulate(h2, gain[2], shift[2], scale[2], name="modulate2")
    nb = S // tb
    w_qkv_s = jnp.concatenate([w_qkv[:, :D] * Q_SCALE, w_qkv[:, D:]], axis=1)
    qkv, qkvT = _mm(hn2, w_qkv_s, name="fox_qkv", out_dtype=BF16, with_t=True)
    fl = _mm(hn2, w_f, name="fox_f", out_dtype=F32)
    fcol = _fgate_fwd(fl, bf_pad, name="fox_fgate")
    f_heads = fcol[:, :HEADS].T.reshape(HP, 2, S)
    ka = _fox_keys(qkv, fcol, name="fox_keys")
    vtb = _blocked(qkvT[2 * D:].reshape(HP, 128, S), nb, tb)
    aug = _aug_rows(f_heads)
    (oT, lse), (g_up1, g_down1) = _foxt_fwd(qkvT, ka, vtb, aug, tb=tb, name="fox_attn",
                                            comm=[(b16(ffn_w_up[1]), True), (b16(ffn_w_down[1]), True)])
    full["ffn_w_up1"] = _cols_join(g_up1)
    full["ffn_w_down1"] = g_down1.reshape(F, D)
    m2, h3 = _mm(oT, full["fox_w_out"], name="fox_out", ta=True, tn=1024, res=h2, gate=gate[2])
    h4, (hn3, up1, z1, m3), _ = ffn_fwd(h3, 3, 1)

    lblk, dh, head_sums, dm = _loss_head(h4, _row(final_g), loss_target[0], m3, gate[3], name="loss_head")
    d_final_g, dgate = head_sums[0:1], head_sums[1:2]
    loss = lax.psum(lblk[0, 0], ("x", "y", "c"))

    G = {}
    dmod = [None] * 4
    mixer_out = [m0, m1, m2, m3]

    def norm_bwd(h, dhn, dh_in, k, dgate_k):
        if k == 0:
            dh_out, sums = _modulate_bwd(h, dhn, dh_in, gain[k], scale[k], name=f"modulate_bwd{k}")
            dm_below = dgate_below = None
        else:
            dh_out, sums, dm_below = _modulate_bwd(h, dhn, dh_in, gain[k], scale[k], mixer_out[k - 1], gate[k - 1],
                                                   name=f"modulate_bwd{k}")
            dgate_below = sums[3:4]
        dmod[k] = jnp.concatenate([sums[0:1], sums[1:2], dgate_k], axis=1)
        return dh_out, sums[2], dm_below, dgate_below

    def ffn_bwd(dh_in, dm, dgate_k, h, k, layer, saved):
        hn, up, z, m = saved
        dz = _mm(dm, full[f"ffn_w_down{layer}"], name=f"ffn_down_dx{layer}", tb=True, out_dtype=BF16, tn=1408)
        dw_down = _mm(z, dm, name=f"ffn_down_dw{layer}", ta=True, tm=1408, tn=1024, tk=2048)
        d_up, cs = _conv_gate_bwd(up, dz, cw_full[layer], ffn_conv_b[layer:layer + 1], name=f"conv_gate_bwd{layer}")
        dhn = _mm(d_up, full[f"ffn_w_up{layer}"], name=f"ffn_up_dx{layer}", tb=True, a_h=True, tk=2816)
        dw_up = _mm(hn, d_up, name=f"ffn_up_dw{layer}", ta=True, b_h=True, tm=1024, tn=1408, tk=2048)
        dh_out, dg, dm_below, dgate_below = norm_bwd(h, dhn, dh_in, k, dgate_k)
        return dh_out, dm_below, dgate_below, dg, dw_up, dw_down, cs[0:3], cs[3]

    dh, dm, dgate, dg3, dw_up1, dw_down1, dcw1, dcb1 = ffn_bwd(dh, dm, dgate, h3, 3, 1, (hn3, up1, z1, m3))

    do, doT = _mm(dm, full["fox_w_out"], name="fox_out_dx", tb=True, out_dtype=BF16, with_t=True)
    dw_fox_out = _mm(oT, dm, name="fox_out_dw", tn=1024, tk=2048)
    to_rows = lambda g: b16(g).reshape(N_DEV, -1, D)
    (dqtb, dk, dv, dfk, dfq), (r_up1, r_down1) = _foxt_bwd(
        qkvT, qkv, ka, _blocked(qkvT[:D].reshape(HP, 128, S), nb, tb), _blocked(doT.reshape(HP, 128, S), nb, tb), do,
        _blocked(aug, nb, tb), _blocked(lse, nb, tb), _blocked(oT.reshape(HP, 128, S), nb, tb), tb=tb, name="fox_attn_bwd",
        comm=[(_cols_split(b16(dw_up1)), False), (to_rows(dw_down1), False)])
    dq = dqtb.transpose(1, 3, 0, 2).reshape(S, D)
    dF = dfk[:, :, :2].transpose(1, 0, 2).reshape(S, HEADS) + dfq[:, :, :2, :].transpose(1, 3, 0, 2).reshape(S, HEADS)
    dF = jnp.pad(dF, ((0, 0), (0, 128 - HEADS)))
    dfl, dbf = _fgate_bwd(dF, fl, bf_pad, name="fox_fgate_bwd")
    dproj = jnp.concatenate([dq, dk, dv, dfl.astype(BF16)], axis=1)
    dhn = _mm(dproj, w_proj, name="fox_in_dx", tb=True)
    dw_proj = _mm(hn2, dproj, name="fox_in_dw", ta=True, tn=640, tk=2048)
    dw_fox_in = dw_proj[:, :full["fox_w_in"].shape[1]]
    dh, dg2, dm, dgate = norm_bwd(h2, dhn, dh, 2, dgate)

    dh, dm, dgate, dg1, dw_up0, dw_down0, dcw0, dcb0 = ffn_bwd(dh, dm, dgate, h1, 1, 0, (hn1, up0, z0, m1))

    dy2 = _mm(dm, full["s5_w_out"], name="s5_out_dx", tb=True, out_dtype=BF16)
    G["s5_w_out"] = _mm(y2, dm, name="s5_out_dw", ta=True, tn=1024, tk=2048)
    da, dgl = _glu_bwd(dy2, yg, gl, name="s5_glu_bwd")
    dyg_b = _mm(dgl, full["s5_w_glu"], name="s5_glu_dx", tb=True)
    G["s5_w_glu"] = _mm(yg, dgl, name="s5_glu_dw", ta=True, tn=1024, tk=2048)
    dyp, dd = _gelu_bwd(da, dyg_b, ypre, u, name="s5_gelu_bwd")
    (du, dwb, dwc, dlam), (r_up0, r_rows0, r_fox) = _s5_bwd(
        u, dyp, st, wb, wc, lam, s5_d, lt=lt, name="s5_scan_bwd",
        comm=[(_cols_split(b16(dw_up0)), False), (jnp.concatenate([to_rows(dw_fox_out), to_rows(dw_down0)], axis=1), False),
              (_cols_split(b16(dw_fox_in)), False)])
    dhn = _mm(du, full["s5_w_in"], name="s5_in_dx", tb=True)
    G["s5_w_in"] = _mm(hn0, du, name="s5_in_dw", ta=True, tn=1024, tk=2048)
    dh, dg0, _, _ = norm_bwd(h0, dhn, dh, 0, dgate)
    grad_x = dh[None]

    dlb_re, dlb_im, dbb_re, dbb_im, dc_re, dc_im = _s5_unpack(dwb, dwc, dlam)
    _, disc_vjp = jax.vjp(_s5_disc, s5_lam_re[0], s5_lam_im[0], s5_log_dt[0], s5_b_re[0], s5_b_im[0])
    dlam_re, dlam_im, dlog_dt, db_re, db_im = disc_vjp((dlb_re, dlb_im, dbb_re, dbb_im))

    g2 = _exchange(_pack([jnp.concatenate(dmod, axis=0)], F32, 8), gather=True, name="gather_dmod")
    (dmod_all,) = _unpack(g2, [(4, 3 * D)])
    dmod_mine = _columns_of_mod(dmod_all, me, ncol)
    d_ada_w, d_ada_b = _adaln_bwd(c_all, dmod_mine, name="adaln_bwd")
    G["ada_w"] = d_ada_w.reshape(ada_w.shape)
    G["ada_b"] = d_ada_b.reshape(ada_b.shape)

    small_full = {
        "norm_g": jnp.stack([dg0, dg1, dg2, dg3]).reshape(2, 2, D),
        "ffn_conv_w": jnp.stack([dcw0, dcw1]),
        "s5_lam_re": dlam_re[None], "s5_lam_im": dlam_im[None], "s5_log_dt": dlog_dt[None],
        "s5_b_re": db_re[None], "s5_b_im": db_im[None], "s5_c_re": dc_re[None], "s5_c_im": dc_im[None],
        "s5_d": dd, "fox_b_f": dbf[:, :HEADS], "ffn_conv_b": jnp.stack([dcb0, dcb1]), "final_g": d_final_g[0],
    }
    names = tuple(small_full)
    small_pack = _pack([small_full[n] for n in names], F32, 8 * N_DEV)
    r_s5, r_small = _exchange_many([(jnp.concatenate([to_rows(G[n]) for n in S5_MATS], axis=1), False),
                                    (small_pack.reshape(N_DEV, -1, 1024), False)], name="scatter_s5_and_small_grads")
    s5_sum = _sum8(r_s5, name="sum_s5_grads")
    for i, n in enumerate(S5_MATS):
        G[n] = s5_sum[i * rows:(i + 1) * rows][None]
    rows0_sum = _sum8(r_rows0, name="sum_rows0_grads")
    G["fox_w_out"] = rows0_sum[:rows][None]
    G["ffn_w_down"] = jnp.stack([rows0_sum[rows:], _sum8(r_down1, name="sum_down1_grads")])
    G["ffn_w_up"] = jnp.stack([_sum8(r_up0, name="sum_up0_grads"), _sum8(r_up1, name="sum_up1_grads")])
    G["fox_w_in"] = _sum8(r_fox, name="sum_fox_w_in")[None]

    g3 = _exchange(_sum8(r_small, name="sum_small_grads"), gather=True, name="gather_small_sums")
    summed = dict(zip(names, _unpack(g3.reshape(small_pack.shape), [small_full[n].shape for n in names])))
    G["norm_g"] = lax.dynamic_slice_in_dim(summed["norm_g"], me * norm_g.shape[-1], norm_g.shape[-1], axis=2)
    G["ffn_conv_w"] = lax.dynamic_slice_in_dim(summed["ffn_conv_w"], me * ffn_conv_w.shape[-1], ffn_conv_w.shape[-1], axis=2)
    for n in REPLICATED:
        G[n] = summed[n]

    delta, new_m, new_v = {}, {}, {}
    small = tuple(n for n in WEIGHTS if n not in BIG and n != "ada_w")
    for n in WEIGHTS:
        if n not in small:
            delta[n], new_m[n], new_v[n] = _adamw_nd(W[n], G[n], M[n], V[n], name=f"adamw_{n}")
    packed = [_pack([src[n] for n in small], F32, 8) for src in (W, G, M, V)]
    for dst, buf in zip((delta, new_m, new_v), _adamw(*packed, name="adamw_small")):
        dst.update(zip(small, _unpack(buf, [W[n].shape for n in small])))

    return (loss, grad_x, *[G[n] for n in WEIGHTS], *[delta[n] for n in WEIGHTS], *[new_m[n] for n in WEIGHTS],
            *[new_v[n] for n in WEIGHTS])


def _columns_of_mod(dmod_all, me, ncol):
    flat = lax.dynamic_slice_in_dim(dmod_all, me * ncol, ncol, axis=2)
    return flat.transpose(1, 0, 2)
```

```python
import functools
import math

import jax
import jax.numpy as jnp
from jax import lax
from jax.experimental import pallas as pl
from jax.experimental.pallas import tpu as pltpu

F32, BF16 = jnp.float32, jnp.bfloat16
EPS = 1e-6
N_DEV = 8
D_MODEL = 1024
D_FF = 2816
HEADS = 16
HEAD_DIM = 64
S5_G, S5_P, S5_C = 64, 64, 16
S5_SUB = 8
V7X_VMEM_LIMIT = 56 * 1024 * 1024
NEG = -1e30
ADAM_LR, ADAM_B1, ADAM_B2, ADAM_EPS, ADAM_WD, ADAM_STEP = 1e-3, 0.9, 0.999, 1e-8, 0.01, 10
GELU_K = math.sqrt(2.0 / math.pi)
MESH_ID = pl.DeviceIdType.MESH


def _cp(*sem):
    return pltpu.CompilerParams(dimension_semantics=sem, vmem_limit_bytes=V7X_VMEM_LIMIT)


def _tile(n, target, mult=128):
    if n <= target:
        return n
    t = (target // mult) * mult
    while t >= mult:
        if n % t == 0:
            return t
        t -= mult
    return n


def _row(v):
    return v.reshape(1, -1).astype(F32)


def _mm(a, b, *, name, ta=False, tb=False, out_dtype=F32, tm=1024, tn=1024, tk=None, res=None, gate=None,
        a_h=False, b_h=False, o_h=False, with_t=False, comm=()):
    if a_h:
        M, K = a.shape[1], 2 * a.shape[2]
    elif ta:
        K, M = a.shape
    else:
        M, K = a.shape
    if b_h:
        N = 2 * b.shape[2]
    else:
        N = b.shape[0] if tb else b.shape[1]
    half_n = N // 2
    tm = _tile(M, tm, 128 if ta else 8)
    tn = _tile(half_n if (b_h or o_h) else N, tn)
    tk = K if tk is None else _tile(K // 2 if a_h else K, tk)
    nk = K // tk
    nkh, nnh = (K // 2) // tk if a_h else 1, half_n // tn
    if a_h:
        a_spec = pl.BlockSpec((None, tm, tk), lambda i, j, k: (k // nkh, i, k % nkh))
    elif ta:
        a_spec = pl.BlockSpec((tk, tm), lambda i, j, k: (k, i))
    else:
        a_spec = pl.BlockSpec((tm, tk), lambda i, j, k: (i, k))
    if b_h:
        b_spec = pl.BlockSpec((None, tk, tn), lambda i, j, k: (j // nnh, k, j % nnh))
    elif tb:
        b_spec = pl.BlockSpec((tn, tk), lambda i, j, k: (j, k))
    else:
        b_spec = pl.BlockSpec((tk, tn), lambda i, j, k: (k, j))
    if o_h:
        o_spec = pl.BlockSpec((None, tm, tn), lambda i, j, k: (j // nnh, i, j % nnh))
    else:
        o_spec = pl.BlockSpec((tm, tn), lambda i, j, k: (i, j))
    dn = (((0 if ta else 1,), (1 if tb else 0,)), ((), ()))
    fused = res is not None

    def body(*refs):
        if fused:
            a_ref, b_ref, r_ref, g_ref, m_ref, o_ref, acc_ref = refs
        elif with_t:
            a_ref, b_ref, o_ref, t_ref, acc_ref = refs
        else:
            a_ref, b_ref, o_ref, acc_ref = refs
        p = lax.dot_general(a_ref[...].astype(BF16), b_ref[...].astype(BF16), dn, preferred_element_type=F32)

        def finish(acc):
            if fused:
                m_ref[...] = acc.astype(m_ref.dtype)
                o_ref[...] = r_ref[...] + g_ref[...] * acc
            else:
                o_ref[...] = acc.astype(o_ref.dtype)
                if with_t:
                    t_ref[...] = acc.T.astype(t_ref.dtype)

        if nk == 1:
            finish(p)
        else:
            k = pl.program_id(2)

            @pl.when(k == 0)
            def _():
                acc_ref[...] = p

            @pl.when(k > 0)
            def _():
                acc_ref[...] += p

            @pl.when(k == nk - 1)
            def _():
                finish(acc_ref[...])

    in_specs = [a_spec, b_spec]
    args = [a, b]
    if fused:
        in_specs += [o_spec, pl.BlockSpec((1, tn), lambda i, j, k: (0, j))]
        args += [res, gate]
        out_shape = [jax.ShapeDtypeStruct((M, N), BF16), jax.ShapeDtypeStruct((M, N), F32)]
        out_specs = [o_spec, o_spec]
    else:
        out_shape = [jax.ShapeDtypeStruct((2, M, half_n) if o_h else (M, N), out_dtype)]
        out_specs = [o_spec]
        if with_t:
            out_shape.append(jax.ShapeDtypeStruct((N, M), out_dtype))
            out_specs.append(pl.BlockSpec((tn, tm), lambda i, j, k: (j, i)))
    outs, exchanged = _call(
        body, name=name, grid=(M // tm, N // tn, nk), in_specs=in_specs, out_specs=out_specs, out_shape=out_shape,
        scratch_shapes=[pltpu.VMEM((tm, tn) if nk > 1 else (8, 128), F32)], sem=("parallel", "parallel", "arbitrary"),
        args=args, comm=comm)
    outs = tuple(outs) if (fused or with_t) else outs[0]
    return (outs, exchanged) if comm else outs


def _modulate(h, g, shift, scale, *, name):
    S, D = h.shape
    tm = _tile(S, 512, 8)
    vec = pl.BlockSpec((1, D), lambda i: (0, 0))
    blk = pl.BlockSpec((tm, D), lambda i: (i, 0))

    def body(h_ref, g_ref, sh_ref, sc_ref, o_ref):
        x = h_ref[...]
        r = lax.rsqrt(jnp.mean(x * x, axis=-1, keepdims=True) + EPS)
        o_ref[...] = ((x * r * g_ref[...]) * (1.0 + sc_ref[...]) + sh_ref[...]).astype(o_ref.dtype)

    return pl.pallas_call(body, name=name, grid=(S // tm,), in_specs=[blk, vec, vec, vec], out_specs=blk,
                          out_shape=jax.ShapeDtypeStruct((S, D), BF16), compiler_params=_cp("parallel"))(h, g, shift, scale)


def _modulate_bwd(h, dhn, dh_in, g, scale, m_prev=None, gate_prev=None, *, name):
    S, D = h.shape
    tm = _tile(S, 512, 8)
    vec = pl.BlockSpec((1, D), lambda i: (0, 0))
    blk = pl.BlockSpec((tm, D), lambda i: (i, 0))
    sums = pl.BlockSpec((8, D), lambda i: (0, 0))
    below = m_prev is not None

    def body(h_ref, dhn_ref, dhi_ref, g_ref, sc_ref, *rest):
        dh_ref, s_ref = rest[-3:-1] if below else rest[-2:]

        @pl.when(pl.program_id(0) == 0)
        def _():
            s_ref[...] = jnp.zeros_like(s_ref)

        x = h_ref[...]
        r = lax.rsqrt(jnp.mean(x * x, axis=-1, keepdims=True) + EPS)
        xhat = x * r
        gv = g_ref[...]
        dhn_v = dhn_ref[...].astype(F32)
        dn = dhn_v * (1.0 + sc_ref[...])
        s_ref[0:1, :] += jnp.sum(dhn_v, axis=0, keepdims=True)
        s_ref[1:2, :] += jnp.sum(dhn_v * (xhat * gv), axis=0, keepdims=True)
        s_ref[2:3, :] += jnp.sum(dn * xhat, axis=0, keepdims=True)
        dxh = dn * gv
        dh = dhi_ref[...] + r * (dxh - xhat * jnp.mean(dxh * xhat, axis=-1, keepdims=True))
        dh_ref[...] = dh
        if below:
            mp_ref, gp_ref, dm_ref = rest[0], rest[1], rest[-1]
            dm_ref[...] = (dh * gp_ref[...]).astype(dm_ref.dtype)
            s_ref[3:4, :] += jnp.sum(dh * mp_ref[...].astype(F32), axis=0, keepdims=True)

    extra_in, extra_args = ([blk, vec], [m_prev, gate_prev]) if below else ([], [])
    return pl.pallas_call(body, name=name, grid=(S // tm,), in_specs=[blk, blk, blk, vec, vec] + extra_in,
                          out_specs=[blk, sums] + ([blk] if below else []),
                          out_shape=[jax.ShapeDtypeStruct((S, D), F32), jax.ShapeDtypeStruct((8, D), F32)]
                          + ([jax.ShapeDtypeStruct((S, D), BF16)] if below else []),
                          compiler_params=_cp("arbitrary"))(h, dhn, dh_in, g, scale, *extra_args)


def _loss_head(h, g, tgt, m_prev, gate_prev, *, name):
    S, D = h.shape
    tm = _tile(S, 512, 8)
    vec = pl.BlockSpec((1, D), lambda i: (0, 0))
    blk = pl.BlockSpec((tm, D), lambda i: (i, 0))
    lss = pl.BlockSpec((8, 128), lambda i: (0, 0))
    sums = pl.BlockSpec((8, D), lambda i: (0, 0))

    def body(h_ref, g_ref, t_ref, mp_ref, gp_ref, l_ref, dh_ref, s_ref, dm_ref):
        @pl.when(pl.program_id(0) == 0)
        def _():
            l_ref[...] = jnp.zeros_like(l_ref)
            s_ref[...] = jnp.zeros_like(s_ref)

        x = h_ref[...]
        r = lax.rsqrt(jnp.mean(x * x, axis=-1, keepdims=True) + EPS)
        xhat = x * r
        gv = g_ref[...]
        e = xhat * gv - t_ref[...]
        l_ref[...] += 0.5 * jnp.sum(jnp.mean(e * e, axis=-1, keepdims=True))
        dy = e * (1.0 / D)
        s_ref[0:1, :] += jnp.sum(dy * xhat, axis=0, keepdims=True)
        dxh = dy * gv
        dh = r * (dxh - xhat * jnp.mean(dxh * xhat, axis=-1, keepdims=True))
        dh_ref[...] = dh
        dm_ref[...] = (dh * gp_ref[...]).astype(dm_ref.dtype)
        s_ref[1:2, :] += jnp.sum(dh * mp_ref[...].astype(F32), axis=0, keepdims=True)

    return pl.pallas_call(body, name=name, grid=(S // tm,), in_specs=[blk, vec, blk, blk, vec], out_specs=[lss, blk, sums, blk],
                          out_shape=[jax.ShapeDtypeStruct((8, 128), F32), jax.ShapeDtypeStruct((S, D), F32),
                                     jax.ShapeDtypeStruct((8, D), F32), jax.ShapeDtypeStruct((S, D), BF16)],
                          compiler_params=_cp("arbitrary"))(h, g, tgt, m_prev, gate_prev)


def _shift_down(x, k, edge):
    tm = x.shape[0]
    rows = lax.broadcasted_iota(jnp.int32, x.shape, 0)
    out = pltpu.roll(x, k, 0)
    for j in range(k):
        out = jnp.where(rows == j, edge[8 - k + j:8 - k + j + 1, :], out)
    return out


def _shift_up(x, k, edge):
    tm = x.shape[0]
    rows = lax.broadcasted_iota(jnp.int32, x.shape, 0)
    out = pltpu.roll(x, tm - k, 0)
    for j in range(k):
        out = jnp.where(rows == tm - k + j, edge[j:j + 1, :], out)
    return out


def _conv_gate_fwd(up, cw, cb, *, name):
    _, S, F = up.shape
    tf = _tile(F, 1408)
    nf = F // tf
    tm = _tile(S, 512, 8)

    def body(a_ref, b_ref, w_ref, cb_ref, z_ref, edge_ref):
        @pl.when(pl.program_id(1) == 0)
        def _():
            edge_ref[...] = jnp.zeros_like(edge_ref)

        a = a_ref[...].astype(F32)
        edge = edge_ref[...]
        w = w_ref[...]
        ac = cb_ref[...] + w[2:3, :] * a + w[1:2, :] * _shift_down(a, 1, edge) + w[0:1, :] * _shift_down(a, 2, edge)
        edge_ref[...] = a[tm - 8:tm, :]
        z_ref[...] = (ac * jax.nn.sigmoid(ac) * b_ref[...].astype(F32)).astype(z_ref.dtype)

    return pl.pallas_call(
        body, name=name, grid=(nf, S // tm),
        in_specs=[pl.BlockSpec((None, tm, tf), lambda j, i: (0, i, j)), pl.BlockSpec((None, tm, tf), lambda j, i: (1, i, j)),
                  pl.BlockSpec((3, tf), lambda j, i: (0, j)), pl.BlockSpec((1, tf), lambda j, i: (0, j))],
        out_specs=pl.BlockSpec((tm, tf), lambda j, i: (i, j)),
        out_shape=jax.ShapeDtypeStruct((S, F), BF16), scratch_shapes=[pltpu.VMEM((8, tf), F32)],
        compiler_params=_cp("parallel", "arbitrary"))(up, up, cw, cb)


def _conv_gate_bwd(up, dz, cw, cb, *, name):
    _, S, F = up.shape
    tf = _tile(F, 1408)
    nf = F // tf
    tm = _tile(S, 512, 8)
    nt = S // tm
    hb = tm // 8

    def body(a_ref, ah_ref, b_ref, dz_ref, w_ref, cb_ref, d_ref, s_ref, edge_ref):
        i = pl.program_id(1)

        @pl.when(i == 0)
        def _():
            edge_ref[...] = jnp.zeros_like(edge_ref)
            s_ref[...] = jnp.zeros_like(s_ref)

        a = a_ref[...].astype(F32)
        halo = jnp.where(i == nt - 1, 0.0, ah_ref[...].astype(F32))
        w = w_ref[...]
        a1 = _shift_down(a, 1, halo)
        a2 = _shift_down(a, 2, halo)
        ac = cb_ref[...] + w[2:3, :] * a + w[1:2, :] * a1 + w[0:1, :] * a2
        sg = jax.nn.sigmoid(ac)
        dzv = dz_ref[...].astype(F32)
        si = ac * sg
        d_ref[1] = (dzv * si).astype(d_ref.dtype)
        dac = (dzv * b_ref[...].astype(F32)) * (sg + si * (1.0 - sg))
        s_ref[0:1, :] += jnp.sum(dac * a2, axis=0, keepdims=True)
        s_ref[1:2, :] += jnp.sum(dac * a1, axis=0, keepdims=True)
        s_ref[2:3, :] += jnp.sum(dac * a, axis=0, keepdims=True)
        s_ref[3:4, :] += jnp.sum(dac, axis=0, keepdims=True)
        edge = edge_ref[...]
        da = w[2:3, :] * dac + w[1:2, :] * _shift_up(dac, 1, edge) + w[0:1, :] * _shift_up(dac, 2, edge)
        edge_ref[...] = dac[0:8, :]
        d_ref[0] = da.astype(d_ref.dtype)

    tile = lambda hlf: pl.BlockSpec((None, tm, tf), lambda j, i: (hlf, nt - 1 - i, j))
    d_up, sums = pl.pallas_call(
        body, name=name, grid=(nf, nt),
        in_specs=[tile(0),
                  pl.BlockSpec((None, 8, tf), lambda j, i: (0, jnp.maximum((nt - 1 - i) * hb - 1, 0), j)),
                  tile(1), pl.BlockSpec((tm, tf), lambda j, i: (nt - 1 - i, j)),
                  pl.BlockSpec((3, tf), lambda j, i: (0, j)), pl.BlockSpec((1, tf), lambda j, i: (0, j))],
        out_specs=[pl.BlockSpec((2, tm, tf), lambda j, i: (0, nt - 1 - i, j)), pl.BlockSpec((8, tf), lambda j, i: (0, j))],
        out_shape=[jax.ShapeDtypeStruct((2, S, F), BF16), jax.ShapeDtypeStruct((8, F), F32)],
        scratch_shapes=[pltpu.VMEM((8, tf), F32)],
        compiler_params=_cp("parallel", "arbitrary"))(up, up, up, dz, cw, cb)
    return d_up, sums


def _log_sigmoid(x):
    return jnp.minimum(x, 0.0) - jnp.log(1.0 + jnp.exp(-jnp.abs(x)))


def _tri_ones(n, upper):
    r = lax.broadcasted_iota(jnp.int32, (n, n), 0)
    c = lax.broadcasted_iota(jnp.int32, (n, n), 1)
    return jnp.where((c >= r) if upper else (c <= r), 1.0, 0.0).astype(F32)


def _fgate_fwd(fl, bf, *, name):
    S, W = fl.shape
    tb = _tile(S, 256, 8)

    def body(fl_ref, b_ref, o_ref, carry_ref):
        @pl.when(pl.program_id(0) == 0)
        def _():
            carry_ref[...] = jnp.zeros_like(carry_ref)

        lf = _log_sigmoid(fl_ref[...] + b_ref[...])
        cs = jnp.dot(_tri_ones(tb, False), lf, precision=lax.Precision.HIGHEST, preferred_element_type=F32) + carry_ref[0:1, :]
        o_ref[...] = cs
        carry_ref[...] = jnp.broadcast_to(cs[tb - 1:tb, :], carry_ref.shape)

    blk = pl.BlockSpec((tb, W), lambda i: (i, 0))
    return pl.pallas_call(body, name=name, grid=(S // tb,), in_specs=[blk, pl.BlockSpec((1, W), lambda i: (0, 0))], out_specs=blk,
                          out_shape=jax.ShapeDtypeStruct((S, W), F32), scratch_shapes=[pltpu.VMEM((8, W), F32)],
                          compiler_params=_cp("arbitrary"))(fl, bf)


def _fgate_bwd(dF, fl, bf, *, name):
    S, W = fl.shape
    tb = _tile(S, 256, 8)
    nb = S // tb

    def body(d_ref, fl_ref, b_ref, o_ref, s_ref, carry_ref):
        @pl.when(pl.program_id(0) == 0)
        def _():
            carry_ref[...] = jnp.zeros_like(carry_ref)
            s_ref[...] = jnp.zeros_like(s_ref)

        rc = jnp.dot(_tri_ones(tb, True), d_ref[...], precision=lax.Precision.HIGHEST, preferred_element_type=F32) + carry_ref[0:1, :]
        carry_ref[...] = jnp.broadcast_to(rc[0:1, :], carry_ref.shape)
        dfl = rc * jax.nn.sigmoid(-(fl_ref[...] + b_ref[...]))
        o_ref[...] = dfl
        s_ref[...] += jnp.sum(dfl, axis=0, keepdims=True)

    blk = pl.BlockSpec((tb, W), lambda i: (nb - 1 - i, 0))
    vec = pl.BlockSpec((1, W), lambda i: (0, 0))
    return pl.pallas_call(body, name=name, grid=(nb,), in_specs=[blk, blk, vec], out_specs=[blk, vec],
                          out_shape=[jax.ShapeDtypeStruct((S, W), F32), jax.ShapeDtypeStruct((1, W), F32)],
                          scratch_shapes=[pltpu.VMEM((8, W), F32)], compiler_params=_cp("arbitrary"))(dF, fl, bf)


NT_DIMS = (((1,), (1,)), ((), ()))
TN_DIMS = (((0,), (0,)), ((), ()))
HP = HEADS // 2
Q_SCALE = HEAD_DIM ** -0.5


def _head_mask(x, hh):
    lanes = lax.broadcasted_iota(jnp.int32, x.shape, 1)
    return jnp.where((lanes >= hh * HEAD_DIM) & (lanes < (hh + 1) * HEAD_DIM), x, jnp.zeros_like(x))


def _lanes_from(cols, shape):
    lanes = lax.broadcasted_iota(jnp.int32, shape, 1)
    out = jnp.zeros(shape, F32)
    for i, cvec in enumerate(cols):
        out = jnp.where(lanes == i, cvec, out)
    return out


AUG_ROWS = 16


def _qa_pair(qt, aug):
    fill = jnp.zeros((HEAD_DIM - AUG_ROWS, qt.shape[1]), qt.dtype)
    return [jnp.concatenate([qt[0:HEAD_DIM], aug[0], fill], axis=0), jnp.concatenate([aug[1], fill, qt[HEAD_DIM:]], axis=0)]


def _rows_of_head(xt, hh):
    rows = lax.broadcasted_iota(jnp.int32, xt.shape, 0)
    return jnp.where((rows >= hh * HEAD_DIM) & (rows < (hh + 1) * HEAD_DIM), xt, jnp.zeros_like(xt))


def _put_rows(ref, rows):
    for i, r in enumerate(rows):
        ref[i:i + 1, :] = r
    ref[len(rows):, :] = jnp.zeros((ref.shape[0] - len(rows), ref.shape[1]), ref.dtype)


def _diag_mask_t(s):
    rows = lax.broadcasted_iota(jnp.int32, s.shape, 0)
    cols = lax.broadcasted_iota(jnp.int32, s.shape, 1)
    return jnp.where(rows <= cols, s, NEG)


def _foxt_fwd(qkvT, ka, vtb, augq, *, tb, name, comm=()):
    S = qkvT.shape[1]
    nb = S // tb

    def body(q_ref, ka_ref, v_ref, aq_ref, o_ref, lse_ref, m0, m1, l0, l1, acc0, acc1):
        qi = pl.program_id(1)
        qa = _qa_pair(q_ref[...], aq_ref[...])
        state = ((m0, l0, acc0), (m1, l1, acc1))
        for m_sc, l_sc, acc_sc in state:
            m_sc[...] = jnp.full_like(m_sc, NEG)
            l_sc[...] = jnp.zeros_like(l_sc)
            acc_sc[...] = jnp.zeros_like(acc_sc)

        def block(kj, masked):
            k0 = pl.multiple_of(kj * tb, tb)
            logits = [jnp.dot(ka_ref[hh, pl.ds(k0, tb), :], qa[hh], preferred_element_type=F32) for hh in range(2)]
            updates = []
            for hh, (m_sc, l_sc, acc_sc) in enumerate(state):
                s = _diag_mask_t(logits[hh]) if masked else logits[hh]
                m_old = m_sc[...]
                m_new = jnp.maximum(m_old, jnp.max(s, axis=0, keepdims=True))
                alpha = jnp.exp(m_old - m_new)
                p = jnp.exp(s - m_new)
                l_sc[...] = alpha * l_sc[...] + jnp.sum(p, axis=0, keepdims=True)
                m_sc[...] = m_new
                vt = v_ref[kj, HEAD_DIM * hh:HEAD_DIM * (hh + 1), :]
                updates.append((alpha, jnp.dot(vt, p.astype(BF16), preferred_element_type=F32)))
            for (alpha, pv), (_, _, acc_sc) in zip(updates, state):
                acc_sc[...] = alpha * acc_sc[...] + pv

        def off_diagonal(kj, c):
            block(kj, False)
            return c

        lax.fori_loop(0, qi, off_diagonal, 0)
        block(qi, True)
        for hh, (m_sc, l_sc, acc_sc) in enumerate(state):
            o_ref[HEAD_DIM * hh:HEAD_DIM * (hh + 1), :] = (acc_sc[...] / l_sc[...]).astype(o_ref.dtype)
        _put_rows(lse_ref, [m_sc[...] + jnp.log(l_sc[...]) for m_sc, l_sc, _ in state])

    return _call(
        body, name=name, grid=(HP, nb), comm=comm, sem=("parallel", "parallel"), args=(qkvT, ka, vtb, augq),
        in_specs=[pl.BlockSpec((128, tb), lambda hp, qi: (hp, qi)),
                  pl.BlockSpec((None, 2, S, 128), lambda hp, qi: (hp, 0, 0, 0)),
                  pl.BlockSpec((None, nb, 128, tb), lambda hp, qi: (hp, 0, 0, 0)),
                  pl.BlockSpec((None, 2, AUG_ROWS, tb), lambda hp, qi: (hp, 0, 0, qi))],
        out_specs=[pl.BlockSpec((128, tb), lambda hp, qi: (hp, qi)), pl.BlockSpec((None, 8, tb), lambda hp, qi: (hp, 0, qi))],
        out_shape=[jax.ShapeDtypeStruct((D_MODEL, S), BF16), jax.ShapeDtypeStruct((HP, 8, S), F32)],
        scratch_shapes=[pltpu.VMEM((1, tb), F32)] * 4 + [pltpu.VMEM((HEAD_DIM, tb), F32)] * 2)


def _foxt_bwd(qkvT, qkv, ka, qtb, dotb, do, augqb, lseb, otb, *, tb, name, comm=()):
    S = qkv.shape[0]
    nb = S // tb

    def body(ka_ref, v_ref, kt_ref, qt_ref, dot_ref, qn_ref, dn_ref, aq_ref, ls_ref, ot_ref,
             dq_ref, dk_ref, dv_ref, df_ref, dr_ref, dq_sc, dk_sc, dv_sc, ds_sc, dr_sc):
        kj = pl.program_id(1)

        @pl.when(kj == 0)
        def _():
            dq_sc[...] = jnp.zeros_like(dq_sc)
            dr_sc[...] = jnp.zeros_like(dr_sc)

        dk_sc[...] = jnp.zeros_like(dk_sc)
        dv_sc[...] = jnp.zeros_like(dv_sc)
        ds_sc[...] = jnp.zeros_like(ds_sc)
        v2 = v_ref[...]
        kth = [kt_ref[HEAD_DIM * hh:HEAD_DIM * (hh + 1), :] for hh in range(2)]

        def block(qi, masked):
            q0 = pl.multiple_of(qi * tb, tb)
            qa = _qa_pair(qt_ref[qi], aq_ref[qi])
            dot, ls = dot_ref[qi], ls_ref[qi]
            prod = ot_ref[qi].astype(F32) * dot.astype(F32)
            delta = [jnp.sum(prod[HEAD_DIM * hh:HEAD_DIM * (hh + 1)], axis=0, keepdims=True) for hh in range(2)]
            qn, dn = qn_ref[pl.ds(q0, tb), :], dn_ref[pl.ds(q0, tb), :]
            for hh in range(2):
                s = jnp.dot(ka_ref[hh], qa[hh], preferred_element_type=F32)
                if masked:
                    s = _diag_mask_t(s)
                p = jnp.exp(s - ls[hh:hh + 1, :])
                dp = jnp.dot(v2, _rows_of_head(dot, hh), preferred_element_type=F32)
                ds = p * (dp - delta[hh])
                dsb = ds.astype(BF16)
                dv_sc[...] += jnp.dot(p.astype(BF16), _head_mask(dn, hh), preferred_element_type=F32)
                dk_sc[...] += jnp.dot(dsb, _head_mask(qn, hh), preferred_element_type=F32)
                dq_sc[qi, HEAD_DIM * hh:HEAD_DIM * (hh + 1), :] += jnp.dot(kth[hh], dsb, preferred_element_type=F32)
                part = ds[:, 0:128]
                for j in range(1, tb // 128):
                    part = part + ds[:, 128 * j:128 * (j + 1)]
                ds_sc[hh] += part
                dr_sc[qi, hh:hh + 1, :] += jnp.sum(ds, axis=0, keepdims=True)

        def off_diagonal(i, c):
            block(kj + 1 + i, False)
            return c

        block(kj, True)
        lax.fori_loop(0, nb - 1 - kj, off_diagonal, 0)
        dk_ref[...] = dk_sc[...].astype(dk_ref.dtype)
        dv_ref[...] = dv_sc[...].astype(dv_ref.dtype)
        df_ref[...] = _lanes_from([-jnp.sum(ds_sc[hh], axis=1, keepdims=True) for hh in range(2)], (tb, 128))

        @pl.when(kj == nb - 1)
        def _():
            dq_ref[...] = (dq_sc[...] * Q_SCALE).astype(dq_ref.dtype)
            dr_ref[...] = dr_sc[...]

    resident = lambda rows: pl.BlockSpec((None, nb) + rows, lambda hp, kj: (hp,) + (0,) * (len(rows) + 1))
    whole = pl.BlockSpec((S, 128), lambda hp, kj: (0, hp))
    kblk = lambda off: pl.BlockSpec((tb, 128), lambda hp, kj: (kj, off + hp))
    return _call(
        body, name=name, grid=(HP, nb), comm=comm, sem=("parallel", "arbitrary"),
        args=(ka, qkv, qkvT, qtb, dotb, qkv, do, augqb, lseb, otb),
        in_specs=[pl.BlockSpec((None, 2, tb, 128), lambda hp, kj: (hp, 0, kj, 0)), kblk(2 * HP),
                  pl.BlockSpec((128, tb), lambda hp, kj: (HP + hp, kj)),
                  resident((128, tb)), resident((128, tb)), whole, whole, resident((2, AUG_ROWS, tb)), resident((8, tb)),
                  resident((128, tb))],
        out_specs=[resident((128, tb)), kblk(0), kblk(0), pl.BlockSpec((None, tb, 128), lambda hp, kj: (hp, kj, 0)),
                   resident((8, tb))],
        out_shape=[jax.ShapeDtypeStruct((HP, nb, 128, tb), BF16), jax.ShapeDtypeStruct((S, D_MODEL), BF16),
                   jax.ShapeDtypeStruct((S, D_MODEL), BF16), jax.ShapeDtypeStruct((HP, S, 128), F32),
                   jax.ShapeDtypeStruct((HP, nb, 8, tb), F32)],
        scratch_shapes=[pltpu.VMEM((nb, 128, tb), F32), pltpu.VMEM((tb, 128), F32), pltpu.VMEM((tb, 128), F32),
                        pltpu.VMEM((2, tb, 128), F32), pltpu.VMEM((nb, 8, tb), F32)])


def _split3(x):
    rnd = lambda v: lax.reduce_precision(v, exponent_bits=8, mantissa_bits=7)
    hi = rnd(x)
    mid = rnd(x - hi)
    lo = rnd(x - hi - mid)
    return hi.astype(BF16), mid.astype(BF16), lo.astype(BF16)


def _blocked(xt, nb, tb):
    lead = xt.shape[:-1]
    x = xt.reshape(lead + (nb, tb))
    return jnp.moveaxis(x, -2, 1)


def _aug_rows(bias):
    ones = jnp.ones(bias.shape, BF16)
    zeros = jnp.zeros(bias.shape, BF16)
    return jnp.stack(list(_split3(bias)) + [ones] * 3 + [zeros] * (AUG_ROWS - 6), axis=2)


def _fox_keys(qkv, fcol, *, name):
    S = qkv.shape[0]
    tr = _tile(S, 2048, 16)

    def body(k_ref, f_ref, o_ref):
        hp = pl.program_id(1)
        k2 = k_ref[...]
        lanes = lax.broadcasted_iota(jnp.int32, (tr, 128), 1)
        for hh in range(2):
            nf = -jnp.sum(jnp.where(lanes == 2 * hp + hh, f_ref[...], 0.0), axis=1, keepdims=True)
            hi = nf.astype(BF16).astype(F32)
            mid = (nf - hi).astype(BF16).astype(F32)
            lo = nf - hi - mid
            base = HEAD_DIM * (1 - hh)
            aug = jnp.where((lanes >= base) & (lanes < base + 3), 1.0, 0.0)
            for j, piece in enumerate((hi, mid, lo)):
                aug = jnp.where(lanes == base + 3 + j, piece, aug)
            mine = (lanes >= HEAD_DIM * hh) & (lanes < HEAD_DIM * (hh + 1))
            o_ref[hh] = jnp.where(mine, k2, aug.astype(BF16))

    return pl.pallas_call(
        body, name=name, grid=(S // tr, HP),
        in_specs=[pl.BlockSpec((tr, 128), lambda i, hp: (i, HP + hp)), pl.BlockSpec((tr, 128), lambda i, hp: (i, 0))],
        out_specs=pl.BlockSpec((None, 2, tr, 128), lambda i, hp: (hp, 0, i, 0)),
        out_shape=jax.ShapeDtypeStruct((HP, 2, S, 128), BF16), compiler_params=_cp("parallel", "parallel"))(qkv, fcol)


HALF = S5_SUB * S5_P
NCB = 2 * HALF // 128
RE, IM = slice(0, NCB // 2), slice(NCB // 2, NCB)


def _gelu(x):
    return 0.5 * x * (1.0 + jnp.tanh(GELU_K * (x + 0.044715 * x * x * x)))


def _gelu_grad(x):
    t = jnp.tanh(GELU_K * (x + 0.044715 * x * x * x))
    return 0.5 * (1.0 + t) + 0.5 * x * (1.0 - t * t) * GELU_K * (1.0 + 3.0 * 0.044715 * x * x)


def _s5_put(buf, s, val, lt):
    for cb in range(NCB):
        buf[cb, pl.ds(s, lt, stride=S5_SUB), :] = val[:, 128 * cb:128 * (cb + 1)]


def _s5_get(buf, s, lt):
    return jnp.concatenate([buf[cb, pl.ds(s, lt, stride=S5_SUB), :] for cb in range(NCB)], axis=1)


def _s5_project_in(u_ref, wb_ref, buf, lt):
    for s in range(S5_SUB):
        _s5_put(buf, s, jnp.dot(u_ref[:, 128 * s:128 * (s + 1)], wb_ref[s], preferred_element_type=F32), lt)


def _s5_scan(buf, lam_ref, h0, lt):
    a_re, a_im = lam_ref[RE], lam_ref[IM]

    def step(t, carry):
        hr, hi = carry
        r0 = pl.multiple_of(t * S5_SUB, S5_SUB)
        nr = a_re * hr - a_im * hi + buf[RE, pl.ds(r0, S5_SUB), :]
        ni = a_re * hi + a_im * hr + buf[IM, pl.ds(r0, S5_SUB), :]
        buf[RE, pl.ds(r0, S5_SUB), :] = nr
        buf[IM, pl.ds(r0, S5_SUB), :] = ni
        return nr, ni

    return lax.fori_loop(0, lt, step, (h0[RE], h0[IM]), unroll=4)


def _s5_fwd(u, wb, wc, lam, dskip, *, lt, name, comm=()):
    S, D = u.shape
    nt = S // lt

    def body(u_ref, wb_ref, wc_ref, lam_ref, d_ref, yp_ref, yg_ref, st_ref, buf, h_sc):
        @pl.when(pl.program_id(0) == 0)
        def _():
            h_sc[...] = jnp.zeros_like(h_sc)

        st_ref[...] = h_sc[...]
        _s5_project_in(u_ref, wb_ref, buf, lt)
        hr, hi = _s5_scan(buf, lam_ref, h_sc[...], lt)
        h_sc[RE] = hr
        h_sc[IM] = hi
        for s in range(S5_SUB):
            cols = slice(128 * s, 128 * (s + 1))
            hs = _s5_get(buf, s, lt).astype(BF16)
            yp = jnp.dot(hs, wc_ref[s], preferred_element_type=F32) + d_ref[:, cols] * u_ref[:, cols].astype(F32)
            yp_ref[:, cols] = yp.astype(yp_ref.dtype)
            yg_ref[:, cols] = _gelu(yp).astype(yg_ref.dtype)

    blk = pl.BlockSpec((lt, D), lambda i: (i, 0))
    full = lambda shp: pl.BlockSpec(shp, lambda i: (0,) * len(shp))
    state = (NCB, S5_SUB, 128)
    return _call(
        body, name=name, grid=(nt,), comm=comm, sem=("arbitrary",), args=(u, wb, wc, lam, dskip),
        in_specs=[blk, full(wb.shape), full(wc.shape), full(lam.shape), full(dskip.shape)],
        out_specs=[blk, blk, pl.BlockSpec((None,) + state, lambda i: (i, 0, 0, 0))],
        out_shape=[jax.ShapeDtypeStruct((S, D), BF16), jax.ShapeDtypeStruct((S, D), BF16), jax.ShapeDtypeStruct((nt,) + state, F32)],
        scratch_shapes=[pltpu.VMEM((NCB, lt * S5_SUB, 128), F32), pltpu.VMEM(state, F32)])


def _s5_bwd(u, dyp, st, wb, wc, lam, dskip, *, lt, name, comm=()):
    S, D = u.shape
    nt = S // lt

    def body(u_ref, dy_ref, st_ref, wb_ref, wc_ref, lam_ref, d_ref, du_ref, dwb_ref, dwc_ref, dlam_ref, hbuf, gbuf, g_sc):
        @pl.when(pl.program_id(0) == 0)
        def _():
            g_sc[...] = jnp.zeros_like(g_sc)
            dwb_ref[...] = jnp.zeros_like(dwb_ref)
            dwc_ref[...] = jnp.zeros_like(dwc_ref)
            dlam_ref[...] = jnp.zeros_like(dlam_ref)

        _s5_project_in(u_ref, wb_ref, hbuf, lt)
        _s5_scan(hbuf, lam_ref, st_ref[...], lt)
        for s in range(S5_SUB):
            dys = dy_ref[:, 128 * s:128 * (s + 1)]
            _s5_put(gbuf, s, lax.dot_general(dys, wc_ref[s], NT_DIMS, preferred_element_type=F32), lt)
        a_re, a_im = lam_ref[RE], lam_ref[IM]

        def one(t, carry, hp_re, hp_im):
            gr, gi, dar, dai = carry
            r0 = pl.multiple_of(t * S5_SUB, S5_SUB)
            nr = gbuf[RE, pl.ds(r0, S5_SUB), :] + a_re * gr + a_im * gi
            ni = gbuf[IM, pl.ds(r0, S5_SUB), :] + a_re * gi - a_im * gr
            gbuf[RE, pl.ds(r0, S5_SUB), :] = nr
            gbuf[IM, pl.ds(r0, S5_SUB), :] = ni
            return nr, ni, dar + nr * hp_re + ni * hp_im, dai + ni * hp_re - nr * hp_im

        def step(k, carry):
            t = lt - 1 - k
            p0 = pl.multiple_of((t - 1) * S5_SUB, S5_SUB)
            return one(t, carry, hbuf[RE, pl.ds(p0, S5_SUB), :], hbuf[IM, pl.ds(p0, S5_SUB), :])

        init = (g_sc[RE], g_sc[IM], dlam_ref[RE], dlam_ref[IM])
        carry = lax.fori_loop(0, lt - 1, step, init, unroll=2)
        gr, gi, dar, dai = one(0, carry, st_ref[RE], st_ref[IM])
        g_sc[RE] = gr
        g_sc[IM] = gi
        dlam_ref[RE] = dar
        dlam_ref[IM] = dai
        for s in range(S5_SUB):
            cols = slice(128 * s, 128 * (s + 1))
            gs = _s5_get(gbuf, s, lt).astype(BF16)
            hs = _s5_get(hbuf, s, lt).astype(BF16)
            us, dys = u_ref[:, cols], dy_ref[:, cols]
            du = lax.dot_general(gs, wb_ref[s], NT_DIMS, preferred_element_type=F32) + d_ref[:, cols] * dys.astype(F32)
            du_ref[:, cols] = du.astype(du_ref.dtype)
            dwb_ref[s] += lax.dot_general(us, gs, TN_DIMS, preferred_element_type=F32)
            dwc_ref[s] += lax.dot_general(hs, dys, TN_DIMS, preferred_element_type=F32)

    blk = pl.BlockSpec((lt, D), lambda i: (nt - 1 - i, 0))
    full = lambda shp: pl.BlockSpec(shp, lambda i: (0,) * len(shp))
    state = (NCB, S5_SUB, 128)
    words = pltpu.VMEM((NCB, lt * S5_SUB, 128), F32)
    return _call(
        body, name=name, grid=(nt,), comm=comm, sem=("arbitrary",), args=(u, dyp, st, wb, wc, lam, dskip),
        in_specs=[blk, blk, pl.BlockSpec((None,) + state, lambda i: (nt - 1 - i, 0, 0, 0)),
                  full(wb.shape), full(wc.shape), full(lam.shape), full(dskip.shape)],
        out_specs=[blk, full(wb.shape), full(wc.shape), full(lam.shape)],
        out_shape=[jax.ShapeDtypeStruct((S, D), BF16), jax.ShapeDtypeStruct(wb.shape, F32), jax.ShapeDtypeStruct(wc.shape, F32),
                   jax.ShapeDtypeStruct(lam.shape, F32)],
        scratch_shapes=[words, words, pltpu.VMEM(state, F32)])


def _s5_disc(lam_re, lam_im, log_dt, b_re, b_im):
    dt = jnp.exp(log_dt)[:, None]
    mag = jnp.exp(lam_re * dt)
    lb_re, lb_im = mag * jnp.cos(lam_im * dt), mag * jnp.sin(lam_im * dt)
    num_re = lb_re - 1.0
    den = lam_re * lam_re + lam_im * lam_im
    k_re = (num_re * lam_re + lb_im * lam_im) / den
    k_im = (lb_im * lam_re - num_re * lam_im) / den
    bb_re = k_re[..., None] * b_re - k_im[..., None] * b_im
    bb_im = k_re[..., None] * b_im + k_im[..., None] * b_re
    return lb_re, lb_im, bb_re, bb_im


def _s5_pack(lb_re, lb_im, bb_re, bb_im, c_re, c_im):
    eye = jnp.eye(S5_SUB, dtype=F32)
    tb = lambda x: jnp.einsum("sgcp,gh->sgchp", x.reshape(S5_SUB, S5_SUB, S5_P, S5_C).transpose(0, 1, 3, 2), eye).reshape(S5_SUB, 128, HALF)
    tc = lambda x: jnp.einsum("sgpc,gh->sgphc", x.reshape(S5_SUB, S5_SUB, S5_C, S5_P).transpose(0, 1, 3, 2), eye).reshape(S5_SUB, HALF, 128)
    wb = jnp.concatenate([tb(bb_re), tb(bb_im)], axis=2).astype(BF16)
    wc = jnp.concatenate([tc(c_re), -tc(c_im)], axis=1).astype(BF16)
    lam = jnp.concatenate([lb_re.reshape(S5_SUB, HALF), lb_im.reshape(S5_SUB, HALF)], axis=1)
    return wb, wc, lam.reshape(S5_SUB, NCB, 128).transpose(1, 0, 2)


def _s5_unpack(dwb, dwc, dlam):
    db = jnp.einsum("sgcrgp->rsgpc", dwb.reshape(S5_SUB, S5_SUB, S5_C, 2, S5_SUB, S5_P)).reshape(2, S5_G, S5_P, S5_C)
    dc = jnp.einsum("srgpgc->rsgcp", dwc.reshape(S5_SUB, 2, S5_SUB, S5_P, S5_SUB, S5_C)).reshape(2, S5_G, S5_C, S5_P)
    dlam = dlam.transpose(1, 0, 2).reshape(S5_SUB, 2 * HALF)
    return (dlam[:, :HALF].reshape(S5_G, S5_P), dlam[:, HALF:].reshape(S5_G, S5_P), db[0], db[1], dc[0], -dc[1])


def _glu_fwd(yg, gl, *, name):
    S, D = yg.shape
    tm = _tile(S, 512, 8)
    blk = pl.BlockSpec((tm, D), lambda i: (i, 0))

    def body(y_ref, g_ref, o_ref):
        o_ref[...] = (y_ref[...].astype(F32) * jax.nn.sigmoid(g_ref[...].astype(F32))).astype(o_ref.dtype)

    return pl.pallas_call(body, name=name, grid=(S // tm,), in_specs=[blk, blk], out_specs=blk,
                          out_shape=jax.ShapeDtypeStruct((S, D), BF16), compiler_params=_cp("parallel"))(yg, gl)


def _glu_bwd(dy2, yg, gl, *, name):
    S, D = yg.shape
    tm = _tile(S, 512, 8)
    blk = pl.BlockSpec((tm, D), lambda i: (i, 0))

    def body(d_ref, y_ref, g_ref, da_ref, dg_ref):
        d = d_ref[...].astype(F32)
        sg = jax.nn.sigmoid(g_ref[...].astype(F32))
        da_ref[...] = d * sg
        dg_ref[...] = (d * y_ref[...].astype(F32) * sg * (1.0 - sg)).astype(dg_ref.dtype)

    return pl.pallas_call(body, name=name, grid=(S // tm,), in_specs=[blk, blk, blk], out_specs=[blk, blk],
                          out_shape=[jax.ShapeDtypeStruct((S, D), F32), jax.ShapeDtypeStruct((S, D), BF16)],
                          compiler_params=_cp("parallel"))(dy2, yg, gl)


def _gelu_bwd(da, db, ypre, u, *, name):
    S, D = ypre.shape
    tm = _tile(S, 512, 8)
    blk = pl.BlockSpec((tm, D), lambda i: (i, 0))
    vec = pl.BlockSpec((1, D), lambda i: (0, 0))

    def body(a_ref, b_ref, y_ref, u_ref, o_ref, s_ref):
        @pl.when(pl.program_id(0) == 0)
        def _():
            s_ref[...] = jnp.zeros_like(s_ref)

        dy = (a_ref[...] + b_ref[...]) * _gelu_grad(y_ref[...].astype(F32))
        o_ref[...] = dy.astype(o_ref.dtype)
        s_ref[...] += jnp.sum(dy * u_ref[...].astype(F32), axis=0, keepdims=True)

    return pl.pallas_call(body, name=name, grid=(S // tm,), in_specs=[blk, blk, blk, blk], out_specs=[blk, vec],
                          out_shape=[jax.ShapeDtypeStruct((S, D), BF16), jax.ShapeDtypeStruct((1, D), F32)],
                          compiler_params=_cp("arbitrary"))(da, db, ypre, u)


def _my_index():
    return 4 * lax.axis_index("x") + 2 * lax.axis_index("y") + lax.axis_index("c")


def _comm_plan(payloads):
    n = len(payloads)
    hbm = pl.BlockSpec(memory_space=pl.ANY)
    shapes = [jax.ShapeDtypeStruct((N_DEV,) + (x.shape if g else x.shape[1:]), x.dtype) for x, g in payloads]
    sems = [pltpu.SemaphoreType.DMA((n, N_DEV - 1)), pltpu.SemaphoreType.DMA((n, N_DEV - 1)), pltpu.SemaphoreType.DMA((n,))] if n else []
    return [hbm] * n, [hbm] * n, shapes, sems


def _comm_copies(x_refs, o_refs, gathers, send_sems, recv_sems, local_sems):
    mx, my, mc = lax.axis_index("x"), lax.axis_index("y"), lax.axis_index("c")
    me = 4 * mx + 2 * my + mc
    copies = []
    for i, (x_ref, o_ref, gather) in enumerate(zip(x_refs, o_refs, gathers)):
        src = (lambda j, r=x_ref: r) if gather else (lambda j, r=x_ref: r.at[j])
        copies.append(pltpu.make_async_copy(src(me), o_ref.at[me], local_sems.at[i]))
        for k in range(1, N_DEV):
            px, py, pc = mx ^ (k >> 2), my ^ ((k >> 1) & 1), mc ^ (k & 1)
            copies.append(pltpu.make_async_remote_copy(src(4 * px + 2 * py + pc), o_ref.at[me], send_sems.at[i, k - 1],
                                                       recv_sems.at[i, k - 1], device_id=(px, py, pc), device_id_type=MESH_ID))
    return copies


def _call(body, *, name, grid, in_specs, out_specs, out_shape, scratch_shapes, sem, args, comm=()):
    if not comm:
        outs = pl.pallas_call(body, name=name, grid=grid, in_specs=in_specs, out_specs=out_specs, out_shape=out_shape,
                              scratch_shapes=scratch_shapes, compiler_params=_cp(*sem))(*args)
        return outs, ()
    n, n_in, n_out = len(comm), len(in_specs), len(out_specs)
    c_in, c_out, c_shapes, c_sems = _comm_plan(comm)
    gathers = [g for _, g in comm]

    def wrapped(*refs):
        own_in, x_refs = refs[:n_in], refs[n_in:n_in + n]
        own_out, o_refs = refs[n_in + n:n_in + n + n_out], refs[n_in + n + n_out:n_in + 2 * n + n_out]
        own_scratch, sems = refs[n_in + 2 * n + n_out:-3], refs[-3:]
        ids = [pl.program_id(a) for a in range(len(grid))]
        first = functools.reduce(jnp.logical_and, [i == 0 for i in ids])
        last = functools.reduce(jnp.logical_and, [i == g - 1 for i, g in zip(ids, grid)])

        @pl.when(first)
        def _():
            for cp in _comm_copies(x_refs, o_refs, gathers, *sems):
                cp.start()

        body(*own_in, *own_out, *own_scratch)

        @pl.when(last)
        def _():
            for cp in _comm_copies(x_refs, o_refs, gathers, *sems):
                cp.wait()

    outs = pl.pallas_call(wrapped, name=name, grid=grid, in_specs=list(in_specs) + c_in, out_specs=list(out_specs) + c_out,
                          out_shape=list(out_shape) + c_shapes, scratch_shapes=list(scratch_shapes) + c_sems,
                          compiler_params=_cp(*["arbitrary"] * len(grid)))(*args, *[x for x, _ in comm])
    return outs[:n_out], outs[n_out:]


def _exchange_many(payloads, *, name):
    n = len(payloads)
    in_specs, out_specs, shapes, sems = _comm_plan(payloads)
    gathers = [g for _, g in payloads]

    def body(*refs):
        copies = _comm_copies(refs[:n], refs[n:2 * n], gathers, *refs[2 * n:])
        for cp in copies:
            cp.start()
        for cp in copies:
            cp.wait()

    return pl.pallas_call(body, name=name, in_specs=in_specs, out_specs=out_specs, out_shape=shapes, scratch_shapes=sems,
                          compiler_params=pltpu.CompilerParams(has_side_effects=True))(*[x for x, _ in payloads])


def _exchange(x, *, gather, name):
    return _exchange_many([(x, gather)], name=name)[0]


def _sum8(x8, *, name):
    _, R, C = x8.shape
    tr = _tile(R, 256, 16)

    def body(x_ref, o_ref):
        acc = x_ref[0].astype(F32)
        for j in range(1, N_DEV):
            acc = acc + x_ref[j].astype(F32)
        o_ref[...] = acc

    return pl.pallas_call(body, name=name, grid=(R // tr,), in_specs=[pl.BlockSpec((N_DEV, tr, C), lambda i: (0, i, 0))],
                          out_specs=pl.BlockSpec((tr, C), lambda i: (i, 0)), out_shape=jax.ShapeDtypeStruct((R, C), F32),
                          compiler_params=_cp("parallel"))(x8)


def _pack(arrs, dtype, row_mult):
    flat = jnp.concatenate([a.reshape(-1).astype(dtype) for a in arrs])
    n = flat.shape[0]
    rows = -(-n // 1024)
    rows = -(-rows // row_mult) * row_mult
    return jnp.pad(flat, (0, rows * 1024 - n)).reshape(rows, 1024)


def _unpack(buf, shapes):
    lead = buf.shape[:-2]
    flat = buf.reshape(lead + (-1,))
    out, off = [], 0
    for s in shapes:
        n = math.prod(s)
        out.append(flat[..., off:off + n].reshape(lead + tuple(s)))
        off += n
    return out


def _adaln_fwd(c_all, w, b, *, name):
    nk, D, n = w.shape

    def body(c_ref, w_ref, b_ref, o_ref):
        cv = c_ref[...]
        sc = (cv * jax.nn.sigmoid(cv)).astype(BF16)
        o_ref[...] = jnp.dot(sc, w_ref[...].astype(BF16), preferred_element_type=F32) + b_ref[...]

    return pl.pallas_call(body, name=name, grid=(nk,),
                          in_specs=[pl.BlockSpec((N_DEV, D), lambda k: (0, 0)), pl.BlockSpec((None, D, n), lambda k: (k, 0, 0)),
                                    pl.BlockSpec((None, 1, n), lambda k: (k, 0, 0))],
                          out_specs=pl.BlockSpec((None, N_DEV, n), lambda k: (k, 0, 0)),
                          out_shape=jax.ShapeDtypeStruct((nk, N_DEV, n), F32), compiler_params=_cp("parallel"))(c_all, w, b)


def _adaln_bwd(c_all, dmod, *, name):
    nk, _, n = dmod.shape
    D = c_all.shape[1]

    def body(c_ref, d_ref, dw_ref, db_ref):
        cv = c_ref[...]
        sc = (cv * jax.nn.sigmoid(cv)).astype(BF16).astype(F32)
        dm = d_ref[...]
        dw_ref[...] = lax.dot_general(sc, dm.astype(BF16).astype(F32), TN_DIMS, precision=lax.Precision.HIGHEST,
                                      preferred_element_type=F32)
        db_ref[...] = jnp.sum(dm, axis=0, keepdims=True)

    return pl.pallas_call(body, name=name, grid=(nk,),
                          in_specs=[pl.BlockSpec((N_DEV, D), lambda k: (0, 0)), pl.BlockSpec((None, N_DEV, n), lambda k: (k, 0, 0))],
                          out_specs=[pl.BlockSpec((None, D, n), lambda k: (k, 0, 0)), pl.BlockSpec((None, 1, n), lambda k: (k, 0, 0))],
                          out_shape=[jax.ShapeDtypeStruct((nk, D, n), F32), jax.ShapeDtypeStruct((nk, 1, n), F32)],
                          compiler_params=_cp("parallel"))(c_all, dmod)


def _adamw(w, g, m, v, *, name):
    R, C = w.shape
    tr = _tile(R, 256, 8)
    blk = pl.BlockSpec((tr, C), lambda i: (i, 0))
    c1 = 1.0 / (1.0 - ADAM_B1 ** ADAM_STEP)
    c2 = 1.0 / (1.0 - ADAM_B2 ** ADAM_STEP)

    def body(w_ref, g_ref, m_ref, v_ref, d_ref, nm_ref, nv_ref):
        gv = g_ref[...]
        nm = ADAM_B1 * m_ref[...] + (1.0 - ADAM_B1) * gv
        nv = ADAM_B2 * v_ref[...] + (1.0 - ADAM_B2) * (gv * gv)
        nm_ref[...] = nm
        nv_ref[...] = nv
        d_ref[...] = -ADAM_LR * ((nm * c1) / (jnp.sqrt(nv * c2) + ADAM_EPS) + ADAM_WD * w_ref[...])

    sh = jax.ShapeDtypeStruct((R, C), F32)
    return pl.pallas_call(body, name=name, grid=(R // tr,), in_specs=[blk] * 4, out_specs=[blk] * 3, out_shape=[sh] * 3,
                          compiler_params=_cp("parallel"))(w, g, m, v)


def _adamw_nd(w, g, m, v, *, name):
    shp = w.shape
    two = (-1, shp[-1]) if w.ndim > 1 else (1, -1)
    outs = _adamw(w.reshape(two), g.reshape(two), m.reshape(two), v.reshape(two), name=name)
    return [o.reshape(shp) for o in outs]


S5_TILE = 256
ATTN_BLOCK = 512
BIG = ("s5_w_in", "s5_w_glu", "s5_w_out", "fox_w_in", "fox_w_out", "ffn_w_up", "ffn_w_down")
WEIGHTS = ("norm_g", "ada_w", "ada_b", "s5_w_in", "s5_lam_re", "s5_lam_im", "s5_log_dt", "s5_b_re", "s5_b_im", "s5_c_re",
           "s5_c_im", "s5_d", "s5_w_glu", "s5_w_out", "fox_w_in", "fox_b_f", "fox_w_out", "ffn_w_up", "ffn_conv_w",
           "ffn_conv_b", "ffn_w_down", "final_g")
REPLICATED = ("s5_lam_re", "s5_lam_im", "s5_log_dt", "s5_b_re", "s5_b_im", "s5_c_re", "s5_c_im", "s5_d", "fox_b_f",
              "ffn_conv_b", "final_g")


def _row_group(arrs):
    return jnp.concatenate([a.reshape(-1, D_MODEL) for a in arrs], axis=0)


def _cols_join(g):
    return jnp.concatenate([g[j] for j in range(N_DEV)], axis=1)


def _cols_split(full):
    n = full.shape[1] // N_DEV
    return jnp.stack([full[:, j * n:(j + 1) * n] for j in range(N_DEV)])


def kernel(x, c, norm_g, ada_w, ada_b, s5_w_in, s5_lam_re, s5_lam_im, s5_log_dt, s5_b_re, s5_b_im, s5_c_re, s5_c_im, s5_d, s5_w_glu, s5_w_out, fox_w_in, fox_b_f, fox_w_out, ffn_w_up, ffn_conv_w, ffn_conv_b, ffn_w_down, final_g, loss_target, m_norm_g, m_ada_w, m_ada_b, m_s5_w_in, m_s5_lam_re, m_s5_lam_im, m_s5_log_dt, m_s5_b_re, m_s5_b_im, m_s5_c_re, m_s5_c_im, m_s5_d, m_s5_w_glu, m_s5_w_out, m_fox_w_in, m_fox_b_f, m_fox_w_out, m_ffn_w_up, m_ffn_conv_w, m_ffn_conv_b, m_ffn_w_down, m_final_g, v_norm_g, v_ada_w, v_ada_b, v_s5_w_in, v_s5_lam_re, v_s5_lam_im, v_s5_log_dt, v_s5_b_re, v_s5_b_im, v_s5_c_re, v_s5_c_im, v_s5_d, v_s5_w_glu, v_s5_w_out, v_fox_w_in, v_fox_b_f, v_fox_w_out, v_ffn_w_up, v_ffn_conv_w, v_ffn_conv_b, v_ffn_w_down, v_final_g):
    args = dict(locals())
    W = {n: args[n] for n in WEIGHTS}
    M = {n: args["m_" + n] for n in WEIGHTS}
    V = {n: args["v_" + n] for n in WEIGHTS}
    D, F = D_MODEL, D_FF
    me = _my_index()
    h0 = x[0]
    S = h0.shape[0]
    lt = min(S5_TILE, S)
    tb = min(ATTN_BLOCK, S)

    g0 = _exchange(_pack([c, norm_g, ffn_conv_w], F32, 8), gather=True, name="gather_small_in")
    c_all, ng_all, cw_all = _unpack(g0, [(D,), norm_g.shape, ffn_conv_w.shape])
    ng_full = ng_all.transpose(1, 2, 0, 3).reshape(2, 2, D)
    cw_full = cw_all.transpose(1, 2, 0, 3).reshape(2, 3, F)

    ncol = ada_w.shape[-1]
    modp = _adaln_fwd(c_all, ada_w.reshape(4, D, ncol), ada_b.reshape(4, 1, ncol), name="adaln_fwd")
    g1 = _exchange(_pack([modp], F32, 8), gather=True, name="gather_adaln")
    (mod_all,) = _unpack(g1, [modp.shape])
    mod = lax.dynamic_index_in_dim(mod_all, me, axis=2, keepdims=False).transpose(1, 0, 2).reshape(4, 3 * D)
    shift = [mod[k:k + 1, :D] for k in range(4)]
    scale = [mod[k:k + 1, D:2 * D] for k in range(4)]
    gate = [mod[k:k + 1, 2 * D:] for k in range(4)]
    gain = [_row(ng_full[k // 2, k % 2]) for k in range(4)]

    b16 = lambda a: a.astype(BF16)
    rows = D // N_DEV
    full = {"s5_w_in": _exchange(b16(s5_w_in[0]), gather=True, name="gather_s5_w_in").reshape(D, D)}
    bf_pad = jnp.pad(fox_b_f, ((0, 0), (0, 128 - HEADS)))

    def ffn_fwd(h, k, layer, comm=()):
        hn = _modulate(h, gain[k], shift[k], scale[k], name=f"modulate{k}")
        up = _mm(hn, full[f"ffn_w_up{layer}"], name=f"ffn_up{layer}", o_h=True, out_dtype=BF16, tn=1408, comm=comm)
        up, exchanged = up if comm else (up, ())
        z = _conv_gate_fwd(up, cw_full[layer], ffn_conv_b[layer:layer + 1], name=f"conv_gate{layer}")
        m, h_out = _mm(z, full[f"ffn_w_down{layer}"], name=f"ffn_down{layer}", res=h, gate=gate[k])
        return h_out, (hn, up, z, m), exchanged

    lb_re, lb_im, bb_re, bb_im = _s5_disc(s5_lam_re[0], s5_lam_im[0], s5_log_dt[0], s5_b_re[0], s5_b_im[0])
    wb, wc, lam = _s5_pack(lb_re, lb_im, bb_re, bb_im, s5_c_re[0], s5_c_im[0])
    hn0 = _modulate(h0, gain[0], shift[0], scale[0], name="modulate0")
    u = _mm(hn0, full["s5_w_in"], name="s5_in", out_dtype=BF16)
    (ypre, yg, st), (g_up0, g_rows0) = _s5_fwd(
        u, wb, wc, lam, s5_d, lt=lt, name="s5_scan",
        comm=[(b16(ffn_w_up[0]), True), (_row_group([b16(s5_w_glu), b16(s5_w_out), b16(ffn_w_down[0])]), True)])
    full["ffn_w_up0"] = _cols_join(g_up0)
    full["s5_w_glu"] = g_rows0[:, :rows].reshape(D, D)
    full["s5_w_out"] = g_rows0[:, rows:2 * rows].reshape(D, D)
    full["ffn_w_down0"] = g_rows0[:, 2 * rows:].reshape(F, D)
    gl = _mm(yg, full["s5_w_glu"], name="s5_glu", out_dtype=BF16)
    y2 = _glu_fwd(yg, gl, name="s5_glu_gate")
    m0, h1 = _mm(y2, full["s5_w_out"], name="s5_out", res=h0, gate=gate[0])
    h2, (hn1, up0, z0, m1), (g_fox, g_fox_out) = ffn_fwd(h1, 1, 0, comm=[(b16(fox_w_in[0]), True), (b16(fox_w_out[0]), True)])
    full["fox_w_in"] = _cols_join(g_fox)
    full["fox_w_out"] = g_fox_out.reshape(D, D)
    w_proj = jnp.pad(full["fox_w_in"], ((0, 0), (0, 3 * D + 128 - full["fox_w_in"].shape[1])))
    w_qkv, w_f = w_proj[:, :3 * D], w_proj[:, 3 * D:]

    hn2 = _modulate(h2, gain[2], shift[2], scale[2], name="modulate2")
    nb = S // tb
    w_qkv_s = jnp.concatenate([w_qkv[:, :D] * Q_SCALE, w_qkv[:, D:]], axis=1)
    qkv, qkvT = _mm(hn2, w_qkv_s, name="fox_qkv", out_dtype=BF16, with_t=True)
    fl = _mm(hn2, w_f, name="fox_f", out_dtype=F32)
    fcol = _fgate_fwd(fl, bf_pad, name="fox_fgate")
    f_heads = fcol[:, :HEADS].T.reshape(HP, 2, S)
    ka = _fox_keys(qkv, fcol, name="fox_keys")
    vtb = _blocked(qkvT[2 * D:].reshape(HP, 128, S), nb, tb)
    aug = _aug_rows(f_heads)
    (oT, lse), (g_up1, g_down1) = _foxt_fwd(qkvT, ka, vtb, aug, tb=tb, name="fox_attn",
                                            comm=[(b16(ffn_w_up[1]), True), (b16(ffn_w_down[1]), True)])
    full["ffn_w_up1"] = _cols_join(g_up1)
    full["ffn_w_down1"] = g_down1.reshape(F, D)
    m2, h3 = _mm(oT, full["fox_w_out"], name="fox_out", ta=True, res=h2, gate=gate[2])
    h4, (hn3, up1, z1, m3), _ = ffn_fwd(h3, 3, 1)

    lblk, dh, head_sums, dm = _loss_head(h4, _row(final_g), loss_target[0], m3, gate[3], name="loss_head")
    d_final_g, dgate = head_sums[0:1], head_sums[1:2]
    loss = lax.psum(lblk[0, 0], ("x", "y", "c"))

    G = {}
    dmod = [None] * 4
    mixer_out = [m0, m1, m2, m3]

    def norm_bwd(h, dhn, dh_in, k, dgate_k):
        if k == 0:
            dh_out, sums = _modulate_bwd(h, dhn, dh_in, gain[k], scale[k], name=f"modulate_bwd{k}")
            dm_below = dgate_below = None
        else:
            dh_out, sums, dm_below = _modulate_bwd(h, dhn, dh_in, gain[k], scale[k], mixer_out[k - 1], gate[k - 1],
                                                   name=f"modulate_bwd{k}")
            dgate_below = sums[3:4]
        dmod[k] = jnp.concatenate([sums[0:1], sums[1:2], dgate_k], axis=1)
        return dh_out, sums[2], dm_below, dgate_below

    def ffn_bwd(dh_in, dm, dgate_k, h, k, layer, saved):
        hn, up, z, m = saved
        dz = _mm(dm, full[f"ffn_w_down{layer}"], name=f"ffn_down_dx{layer}", tb=True, out_dtype=BF16, tn=1408)
        dw_down = _mm(z, dm, name=f"ffn_down_dw{layer}", ta=True, tm=1408, tk=2048)
        d_up, cs = _conv_gate_bwd(up, dz, cw_full[layer], ffn_conv_b[layer:layer + 1], name=f"conv_gate_bwd{layer}")
        dhn = _mm(d_up, full[f"ffn_w_up{layer}"], name=f"ffn_up_dx{layer}", tb=True, a_h=True, tk=2816)
        dw_up = _mm(hn, d_up, name=f"ffn_up_dw{layer}", ta=True, b_h=True, tm=1024, tn=1408, tk=2048)
        dh_out, dg, dm_below, dgate_below = norm_bwd(h, dhn, dh_in, k, dgate_k)
        return dh_out, dm_below, dgate_below, dg, dw_up, dw_down, cs[0:3], cs[3]

    dh, dm, dgate, dg3, dw_up1, dw_down1, dcw1, dcb1 = ffn_bwd(dh, dm, dgate, h3, 3, 1, (hn3, up1, z1, m3))

    do, doT = _mm(dm, full["fox_w_out"], name="fox_out_dx", tb=True, out_dtype=BF16, with_t=True)
    dw_fox_out = _mm(oT, dm, name="fox_out_dw", tk=2048)
    to_rows = lambda g: b16(g).reshape(N_DEV, -1, D)
    (dqtb, dk, dv, dfk, dfq), (r_up1, r_down1) = _foxt_bwd(
        qkvT, qkv, ka, _blocked(qkvT[:D].reshape(HP, 128, S), nb, tb), _blocked(doT.reshape(HP, 128, S), nb, tb), do,
        _blocked(aug, nb, tb), _blocked(lse, nb, tb), _blocked(oT.reshape(HP, 128, S), nb, tb), tb=tb, name="fox_attn_bwd",
        comm=[(_cols_split(b16(dw_up1)), False), (to_rows(dw_down1), False)])
    dq = dqtb.transpose(1, 3, 0, 2).reshape(S, D)
    dF = dfk[:, :, :2].transpose(1, 0, 2).reshape(S, HEADS) + dfq[:, :, :2, :].transpose(1, 3, 0, 2).reshape(S, HEADS)
    dF = jnp.pad(dF, ((0, 0), (0, 128 - HEADS)))
    dfl, dbf = _fgate_bwd(dF, fl, bf_pad, name="fox_fgate_bwd")
    dproj = jnp.concatenate([dq, dk, dv, dfl.astype(BF16)], axis=1)
    dhn = _mm(dproj, w_proj, name="fox_in_dx", tb=True)
    dw_proj = _mm(hn2, dproj, name="fox_in_dw", ta=True, tn=640, tk=2048)
    dw_fox_in = dw_proj[:, :full["fox_w_in"].shape[1]]
    dh, dg2, dm, dgate = norm_bwd(h2, dhn, dh, 2, dgate)

    dh, dm, dgate, dg1, dw_up0, dw_down0, dcw0, dcb0 = ffn_bwd(dh, dm, dgate, h1, 1, 0, (hn1, up0, z0, m1))

    dy2 = _mm(dm, full["s5_w_out"], name="s5_out_dx", tb=True, out_dtype=BF16)
    G["s5_w_out"] = _mm(y2, dm, name="s5_out_dw", ta=True, tk=2048)
    da, dgl = _glu_bwd(dy2, yg, gl, name="s5_glu_bwd")
    dyg_b = _mm(dgl, full["s5_w_glu"], name="s5_glu_dx", tb=True)
    G["s5_w_glu"] = _mm(yg, dgl, name="s5_glu_dw", ta=True, tk=2048)
    dyp, dd = _gelu_bwd(da, dyg_b, ypre, u, name="s5_gelu_bwd")
    row_grads = [dw_fox_out, dw_down0, G["s5_w_out"], G["s5_w_glu"]]
    (du, dwb, dwc, dlam), (r_up0, r_rows0, r_fox) = _s5_bwd(
        u, dyp, st, wb, wc, lam, s5_d, lt=lt, name="s5_scan_bwd",
        comm=[(_cols_split(b16(dw_up0)), False), (jnp.concatenate([to_rows(g) for g in row_grads], axis=1), False),
              (_cols_split(b16(dw_fox_in)), False)])
    dhn = _mm(du, full["s5_w_in"], name="s5_in_dx", tb=True)
    G["s5_w_in"] = _mm(hn0, du, name="s5_in_dw", ta=True, tk=2048)
    dh, dg0, _, _ = norm_bwd(h0, dhn, dh, 0, dgate)
    grad_x = dh[None]

    dlb_re, dlb_im, dbb_re, dbb_im, dc_re, dc_im = _s5_unpack(dwb, dwc, dlam)
    _, disc_vjp = jax.vjp(_s5_disc, s5_lam_re[0], s5_lam_im[0], s5_log_dt[0], s5_b_re[0], s5_b_im[0])
    dlam_re, dlam_im, dlog_dt, db_re, db_im = disc_vjp((dlb_re, dlb_im, dbb_re, dbb_im))

    g2 = _exchange(_pack([jnp.concatenate(dmod, axis=0)], F32, 8), gather=True, name="gather_dmod")
    (dmod_all,) = _unpack(g2, [(4, 3 * D)])
    dmod_mine = _columns_of_mod(dmod_all, me, ncol)
    d_ada_w, d_ada_b = _adaln_bwd(c_all, dmod_mine, name="adaln_bwd")
    G["ada_w"] = d_ada_w.reshape(ada_w.shape)
    G["ada_b"] = d_ada_b.reshape(ada_b.shape)

    small_full = {
        "norm_g": jnp.stack([dg0, dg1, dg2, dg3]).reshape(2, 2, D),
        "ffn_conv_w": jnp.stack([dcw0, dcw1]),
        "s5_lam_re": dlam_re[None], "s5_lam_im": dlam_im[None], "s5_log_dt": dlog_dt[None],
        "s5_b_re": db_re[None], "s5_b_im": db_im[None], "s5_c_re": dc_re[None], "s5_c_im": dc_im[None],
        "s5_d": dd, "fox_b_f": dbf[:, :HEADS], "ffn_conv_b": jnp.stack([dcb0, dcb1]), "final_g": d_final_g[0],
    }
    names = tuple(small_full)
    small_pack = _pack([small_full[n] for n in names], F32, 8 * N_DEV)
    r_in, r_small = _exchange_many([(to_rows(G["s5_w_in"]), False), (small_pack.reshape(N_DEV, -1, 1024), False)],
                                   name="scatter_s5_in_and_small_grads")
    G["s5_w_in"] = _sum8(r_in, name="sum_s5_in_grads")[None]
    rows0_sum = _sum8(r_rows0, name="sum_rows0_grads")
    down_rows = F // N_DEV
    G["fox_w_out"] = rows0_sum[:rows][None]
    G["s5_w_out"] = rows0_sum[rows + down_rows:2 * rows + down_rows][None]
    G["s5_w_glu"] = rows0_sum[2 * rows + down_rows:][None]
    G["ffn_w_down"] = jnp.stack([rows0_sum[rows:rows + down_rows], _sum8(r_down1, name="sum_down1_grads")])
    G["ffn_w_up"] = jnp.stack([_sum8(r_up0, name="sum_up0_grads"), _sum8(r_up1, name="sum_up1_grads")])
    G["fox_w_in"] = _sum8(r_fox, name="sum_fox_w_in")[None]

    g3 = _exchange(_sum8(r_small, name="sum_small_grads"), gather=True, name="gather_small_sums")
    summed = dict(zip(names, _unpack(g3.reshape(small_pack.shape), [small_full[n].shape for n in names])))
    G["norm_g"] = lax.dynamic_slice_in_dim(summed["norm_g"], me * norm_g.shape[-1], norm_g.shape[-1], axis=2)
    G["ffn_conv_w"] = lax.dynamic_slice_in_dim(summed["ffn_conv_w"], me * ffn_conv_w.shape[-1], ffn_conv_w.shape[-1], axis=2)
    for n in REPLICATED:
        G[n] = summed[n]

    delta, new_m, new_v = {}, {}, {}
    small = tuple(n for n in WEIGHTS if n not in BIG and n != "ada_w")
    for n in WEIGHTS:
        if n not in small:
            delta[n], new_m[n], new_v[n] = _adamw_nd(W[n], G[n], M[n], V[n], name=f"adamw_{n}")
    packed = [_pack([src[n] for n in small], F32, 8) for src in (W, G, M, V)]
    for dst, buf in zip((delta, new_m, new_v), _adamw(*packed, name="adamw_small")):
        dst.update(zip(small, _unpack(buf, [W[n].shape for n in small])))

    return (loss, grad_x, *[G[n] for n in WEIGHTS], *[delta[n] for n in WEIGHTS], *[new_m[n] for n in WEIGHTS],
            *[new_v[n] for n in WEIGHTS])


def _columns_of_mod(dmod_all, me, ncol):
    flat = lax.dynamic_slice_in_dim(dmod_all, me * ncol, ncol, axis=2)
    return flat.transpose(1, 0, 2)
```

```python
import functools
import math

import jax
import jax.numpy as jnp
from jax import lax
from jax.experimental import pallas as pl
from jax.experimental.pallas import tpu as pltpu

F32, BF16 = jnp.float32, jnp.bfloat16
EPS = 1e-6
N_DEV = 8
D_MODEL = 1024
D_FF = 2816
HEADS = 16
HEAD_DIM = 64
S5_G, S5_P, S5_C = 64, 64, 16
S5_SUB = 8
V7X_VMEM_LIMIT = 56 * 1024 * 1024
NEG = -1e30
ADAM_LR, ADAM_B1, ADAM_B2, ADAM_EPS, ADAM_WD, ADAM_STEP = 1e-3, 0.9, 0.999, 1e-8, 0.01, 10
GELU_K = math.sqrt(2.0 / math.pi)
MESH_ID = pl.DeviceIdType.MESH


def _cp(*sem):
    return pltpu.CompilerParams(dimension_semantics=sem, vmem_limit_bytes=V7X_VMEM_LIMIT)


def _tile(n, target, mult=128):
    if n <= target:
        return n
    t = (target // mult) * mult
    while t >= mult:
        if n % t == 0:
            return t
        t -= mult
    return n


def _row(v):
    return v.reshape(1, -1).astype(F32)


def _mm(a, b, *, name, ta=False, tb=False, out_dtype=F32, tm=1024, tn=1024, tk=None, res=None, gate=None,
        a_h=False, b_h=False, o_h=False, with_t=False, comm=()):
    if a_h:
        M, K = a.shape[1], 2 * a.shape[2]
    elif ta:
        K, M = a.shape
    else:
        M, K = a.shape
    if b_h:
        N = 2 * b.shape[2]
    else:
        N = b.shape[0] if tb else b.shape[1]
    half_n = N // 2
    tm = _tile(M, tm, 128 if ta else 8)
    tn = _tile(half_n if (b_h or o_h) else N, tn)
    tk = K if tk is None else _tile(K // 2 if a_h else K, tk)
    nk = K // tk
    nkh, nnh = (K // 2) // tk if a_h else 1, half_n // tn
    if a_h:
        a_spec = pl.BlockSpec((None, tm, tk), lambda i, j, k: (k // nkh, i, k % nkh))
    elif ta:
        a_spec = pl.BlockSpec((tk, tm), lambda i, j, k: (k, i))
    else:
        a_spec = pl.BlockSpec((tm, tk), lambda i, j, k: (i, k))
    if b_h:
        b_spec = pl.BlockSpec((None, tk, tn), lambda i, j, k: (j // nnh, k, j % nnh))
    elif tb:
        b_spec = pl.BlockSpec((tn, tk), lambda i, j, k: (j, k))
    else:
        b_spec = pl.BlockSpec((tk, tn), lambda i, j, k: (k, j))
    if o_h:
        o_spec = pl.BlockSpec((None, tm, tn), lambda i, j, k: (j // nnh, i, j % nnh))
    else:
        o_spec = pl.BlockSpec((tm, tn), lambda i, j, k: (i, j))
    dn = (((0 if ta else 1,), (1 if tb else 0,)), ((), ()))
    fused = res is not None

    def body(*refs):
        if fused:
            a_ref, b_ref, r_ref, g_ref, m_ref, o_ref, acc_ref = refs
        elif with_t:
            a_ref, b_ref, o_ref, t_ref, acc_ref = refs
        else:
            a_ref, b_ref, o_ref, acc_ref = refs
        p = lax.dot_general(a_ref[...].astype(BF16), b_ref[...].astype(BF16), dn, preferred_element_type=F32)

        def finish(acc):
            if fused:
                m_ref[...] = acc.astype(m_ref.dtype)
                o_ref[...] = r_ref[...] + g_ref[...] * acc
            else:
                o_ref[...] = acc.astype(o_ref.dtype)
                if with_t:
                    t_ref[...] = acc.T.astype(t_ref.dtype)

        if nk == 1:
            finish(p)
        else:
            k = pl.program_id(2)

            @pl.when(k == 0)
            def _():
                acc_ref[...] = p

            @pl.when(k > 0)
            def _():
                acc_ref[...] += p

            @pl.when(k == nk - 1)
            def _():
                finish(acc_ref[...])

    in_specs = [a_spec, b_spec]
    args = [a, b]
    if fused:
        in_specs += [o_spec, pl.BlockSpec((1, tn), lambda i, j, k: (0, j))]
        args += [res, gate]
        out_shape = [jax.ShapeDtypeStruct((M, N), BF16), jax.ShapeDtypeStruct((M, N), F32)]
        out_specs = [o_spec, o_spec]
    else:
        out_shape = [jax.ShapeDtypeStruct((2, M, half_n) if o_h else (M, N), out_dtype)]
        out_specs = [o_spec]
        if with_t:
            out_shape.append(jax.ShapeDtypeStruct((N, M), out_dtype))
            out_specs.append(pl.BlockSpec((tn, tm), lambda i, j, k: (j, i)))
    outs, exchanged = _call(
        body, name=name, grid=(M // tm, N // tn, nk), in_specs=in_specs, out_specs=out_specs, out_shape=out_shape,
        scratch_shapes=[pltpu.VMEM((tm, tn) if nk > 1 else (8, 128), F32)], sem=("parallel", "parallel", "arbitrary"),
        args=args, comm=comm)
    outs = tuple(outs) if (fused or with_t) else outs[0]
    return (outs, exchanged) if comm else outs


def _modulate(h, g, shift, scale, *, name):
    S, D = h.shape
    tm = _tile(S, 512, 8)
    vec = pl.BlockSpec((1, D), lambda i: (0, 0))
    blk = pl.BlockSpec((tm, D), lambda i: (i, 0))

    def body(h_ref, g_ref, sh_ref, sc_ref, o_ref):
        x = h_ref[...]
        r = lax.rsqrt(jnp.mean(x * x, axis=-1, keepdims=True) + EPS)
        o_ref[...] = ((x * r * g_ref[...]) * (1.0 + sc_ref[...]) + sh_ref[...]).astype(o_ref.dtype)

    return pl.pallas_call(body, name=name, grid=(S // tm,), in_specs=[blk, vec, vec, vec], out_specs=blk,
                          out_shape=jax.ShapeDtypeStruct((S, D), BF16), compiler_params=_cp("parallel"))(h, g, shift, scale)


def _modulate_bwd(h, dhn, dh_in, g, scale, m_prev=None, gate_prev=None, *, name):
    S, D = h.shape
    tm = _tile(S, 512, 8)
    vec = pl.BlockSpec((1, D), lambda i: (0, 0))
    blk = pl.BlockSpec((tm, D), lambda i: (i, 0))
    sums = pl.BlockSpec((8, D), lambda i: (0, 0))
    below = m_prev is not None

    def body(h_ref, dhn_ref, dhi_ref, g_ref, sc_ref, *rest):
        dh_ref, s_ref = rest[-3:-1] if below else rest[-2:]

        @pl.when(pl.program_id(0) == 0)
        def _():
            s_ref[...] = jnp.zeros_like(s_ref)

        x = h_ref[...]
        r = lax.rsqrt(jnp.mean(x * x, axis=-1, keepdims=True) + EPS)
        xhat = x * r
        gv = g_ref[...]
        dhn_v = dhn_ref[...].astype(F32)
        dn = dhn_v * (1.0 + sc_ref[...])
        s_ref[0:1, :] += jnp.sum(dhn_v, axis=0, keepdims=True)
        s_ref[1:2, :] += jnp.sum(dhn_v * (xhat * gv), axis=0, keepdims=True)
        s_ref[2:3, :] += jnp.sum(dn * xhat, axis=0, keepdims=True)
        dxh = dn * gv
        dh = dhi_ref[...] + r * (dxh - xhat * jnp.mean(dxh * xhat, axis=-1, keepdims=True))
        dh_ref[...] = dh
        if below:
            mp_ref, gp_ref, dm_ref = rest[0], rest[1], rest[-1]
            dm_ref[...] = (dh * gp_ref[...]).astype(dm_ref.dtype)
            s_ref[3:4, :] += jnp.sum(dh * mp_ref[...].astype(F32), axis=0, keepdims=True)

    extra_in, extra_args = ([blk, vec], [m_prev, gate_prev]) if below else ([], [])
    return pl.pallas_call(body, name=name, grid=(S // tm,), in_specs=[blk, blk, blk, vec, vec] + extra_in,
                          out_specs=[blk, sums] + ([blk] if below else []),
                          out_shape=[jax.ShapeDtypeStruct((S, D), F32), jax.ShapeDtypeStruct((8, D), F32)]
                          + ([jax.ShapeDtypeStruct((S, D), BF16)] if below else []),
                          compiler_params=_cp("arbitrary"))(h, dhn, dh_in, g, scale, *extra_args)


def _loss_head(h, g, tgt, m_prev, gate_prev, *, name):
    S, D = h.shape
    tm = _tile(S, 512, 8)
    vec = pl.BlockSpec((1, D), lambda i: (0, 0))
    blk = pl.BlockSpec((tm, D), lambda i: (i, 0))
    lss = pl.BlockSpec((8, 128), lambda i: (0, 0))
    sums = pl.BlockSpec((8, D), lambda i: (0, 0))

    def body(h_ref, g_ref, t_ref, mp_ref, gp_ref, l_ref, dh_ref, s_ref, dm_ref):
        @pl.when(pl.program_id(0) == 0)
        def _():
            l_ref[...] = jnp.zeros_like(l_ref)
            s_ref[...] = jnp.zeros_like(s_ref)

        x = h_ref[...]
        r = lax.rsqrt(jnp.mean(x * x, axis=-1, keepdims=True) + EPS)
        xhat = x * r
        gv = g_ref[...]
        e = xhat * gv - t_ref[...]
        l_ref[...] += 0.5 * jnp.sum(jnp.mean(e * e, axis=-1, keepdims=True))
        dy = e * (1.0 / D)
        s_ref[0:1, :] += jnp.sum(dy * xhat, axis=0, keepdims=True)
        dxh = dy * gv
        dh = r * (dxh - xhat * jnp.mean(dxh * xhat, axis=-1, keepdims=True))
        dh_ref[...] = dh
        dm_ref[...] = (dh * gp_ref[...]).astype(dm_ref.dtype)
        s_ref[1:2, :] += jnp.sum(dh * mp_ref[...].astype(F32), axis=0, keepdims=True)

    return pl.pallas_call(body, name=name, grid=(S // tm,), in_specs=[blk, vec, blk, blk, vec], out_specs=[lss, blk, sums, blk],
                          out_shape=[jax.ShapeDtypeStruct((8, 128), F32), jax.ShapeDtypeStruct((S, D), F32),
                                     jax.ShapeDtypeStruct((8, D), F32), jax.ShapeDtypeStruct((S, D), BF16)],
                          compiler_params=_cp("arbitrary"))(h, g, tgt, m_prev, gate_prev)


def _shift_down(x, k, edge):
    tm = x.shape[0]
    rows = lax.broadcasted_iota(jnp.int32, x.shape, 0)
    out = pltpu.roll(x, k, 0)
    for j in range(k):
        out = jnp.where(rows == j, edge[8 - k + j:8 - k + j + 1, :], out)
    return out


def _shift_up(x, k, edge):
    tm = x.shape[0]
    rows = lax.broadcasted_iota(jnp.int32, x.shape, 0)
    out = pltpu.roll(x, tm - k, 0)
    for j in range(k):
        out = jnp.where(rows == tm - k + j, edge[j:j + 1, :], out)
    return out


def _conv_gate_fwd(up, cw, cb, *, name):
    _, S, F = up.shape
    tf = _tile(F, 1408)
    nf = F // tf
    tm = _tile(S, 512, 8)

    def body(a_ref, b_ref, w_ref, cb_ref, z_ref, edge_ref):
        @pl.when(pl.program_id(1) == 0)
        def _():
            edge_ref[...] = jnp.zeros_like(edge_ref)

        a = a_ref[...].astype(F32)
        edge = edge_ref[...]
        w = w_ref[...]
        ac = cb_ref[...] + w[2:3, :] * a + w[1:2, :] * _shift_down(a, 1, edge) + w[0:1, :] * _shift_down(a, 2, edge)
        edge_ref[...] = a[tm - 8:tm, :]
        z_ref[...] = (ac * jax.nn.sigmoid(ac) * b_ref[...].astype(F32)).astype(z_ref.dtype)

    return pl.pallas_call(
        body, name=name, grid=(nf, S // tm),
        in_specs=[pl.BlockSpec((None, tm, tf), lambda j, i: (0, i, j)), pl.BlockSpec((None, tm, tf), lambda j, i: (1, i, j)),
                  pl.BlockSpec((3, tf), lambda j, i: (0, j)), pl.BlockSpec((1, tf), lambda j, i: (0, j))],
        out_specs=pl.BlockSpec((tm, tf), lambda j, i: (i, j)),
        out_shape=jax.ShapeDtypeStruct((S, F), BF16), scratch_shapes=[pltpu.VMEM((8, tf), F32)],
        compiler_params=_cp("parallel", "arbitrary"))(up, up, cw, cb)


def _conv_gate_bwd(up, dz, cw, cb, *, name):
    _, S, F = up.shape
    tf = _tile(F, 1408)
    nf = F // tf
    tm = _tile(S, 512, 8)
    nt = S // tm
    hb = tm // 8

    def body(a_ref, ah_ref, b_ref, dz_ref, w_ref, cb_ref, d_ref, s_ref, edge_ref):
        i = pl.program_id(1)

        @pl.when(i == 0)
        def _():
            edge_ref[...] = jnp.zeros_like(edge_ref)
            s_ref[...] = jnp.zeros_like(s_ref)

        a = a_ref[...].astype(F32)
        halo = jnp.where(i == nt - 1, 0.0, ah_ref[...].astype(F32))
        w = w_ref[...]
        a1 = _shift_down(a, 1, halo)
        a2 = _shift_down(a, 2, halo)
        ac = cb_ref[...] + w[2:3, :] * a + w[1:2, :] * a1 + w[0:1, :] * a2
        sg = jax.nn.sigmoid(ac)
        dzv = dz_ref[...].astype(F32)
        si = ac * sg
        d_ref[1] = (dzv * si).astype(d_ref.dtype)
        dac = (dzv * b_ref[...].astype(F32)) * (sg + si * (1.0 - sg))
        s_ref[0:1, :] += jnp.sum(dac * a2, axis=0, keepdims=True)
        s_ref[1:2, :] += jnp.sum(dac * a1, axis=0, keepdims=True)
        s_ref[2:3, :] += jnp.sum(dac * a, axis=0, keepdims=True)
        s_ref[3:4, :] += jnp.sum(dac, axis=0, keepdims=True)
        edge = edge_ref[...]
        da = w[2:3, :] * dac + w[1:2, :] * _shift_up(dac, 1, edge) + w[0:1, :] * _shift_up(dac, 2, edge)
        edge_ref[...] = dac[0:8, :]
        d_ref[0] = da.astype(d_ref.dtype)

    tile = lambda hlf: pl.BlockSpec((None, tm, tf), lambda j, i: (hlf, nt - 1 - i, j))
    d_up, sums = pl.pallas_call(
        body, name=name, grid=(nf, nt),
        in_specs=[tile(0),
                  pl.BlockSpec((None, 8, tf), lambda j, i: (0, jnp.maximum((nt - 1 - i) * hb - 1, 0), j)),
                  tile(1), pl.BlockSpec((tm, tf), lambda j, i: (nt - 1 - i, j)),
                  pl.BlockSpec((3, tf), lambda j, i: (0, j)), pl.BlockSpec((1, tf), lambda j, i: (0, j))],
        out_specs=[pl.BlockSpec((2, tm, tf), lambda j, i: (0, nt - 1 - i, j)), pl.BlockSpec((8, tf), lambda j, i: (0, j))],
        out_shape=[jax.ShapeDtypeStruct((2, S, F), BF16), jax.ShapeDtypeStruct((8, F), F32)],
        scratch_shapes=[pltpu.VMEM((8, tf), F32)],
        compiler_params=_cp("parallel", "arbitrary"))(up, up, up, dz, cw, cb)
    return d_up, sums


def _log_sigmoid(x):
    return jnp.minimum(x, 0.0) - jnp.log(1.0 + jnp.exp(-jnp.abs(x)))


def _tri_ones(n, upper):
    r = lax.broadcasted_iota(jnp.int32, (n, n), 0)
    c = lax.broadcasted_iota(jnp.int32, (n, n), 1)
    return jnp.where((c >= r) if upper else (c <= r), 1.0, 0.0).astype(F32)


def _fgate_fwd(fl, bf, *, name):
    S, W = fl.shape
    tb = _tile(S, 256, 8)

    def body(fl_ref, b_ref, o_ref, carry_ref):
        @pl.when(pl.program_id(0) == 0)
        def _():
            carry_ref[...] = jnp.zeros_like(carry_ref)

        lf = _log_sigmoid(fl_ref[...] + b_ref[...])
        cs = jnp.dot(_tri_ones(tb, False), lf, precision=lax.Precision.HIGHEST, preferred_element_type=F32) + carry_ref[0:1, :]
        o_ref[...] = cs
        carry_ref[...] = jnp.broadcast_to(cs[tb - 1:tb, :], carry_ref.shape)

    blk = pl.BlockSpec((tb, W), lambda i: (i, 0))
    return pl.pallas_call(body, name=name, grid=(S // tb,), in_specs=[blk, pl.BlockSpec((1, W), lambda i: (0, 0))], out_specs=blk,
                          out_shape=jax.ShapeDtypeStruct((S, W), F32), scratch_shapes=[pltpu.VMEM((8, W), F32)],
                          compiler_params=_cp("arbitrary"))(fl, bf)


def _fgate_bwd(dF, fl, bf, *, name):
    S, W = fl.shape
    tb = _tile(S, 256, 8)
    nb = S // tb

    def body(d_ref, fl_ref, b_ref, o_ref, s_ref, carry_ref):
        @pl.when(pl.program_id(0) == 0)
        def _():
            carry_ref[...] = jnp.zeros_like(carry_ref)
            s_ref[...] = jnp.zeros_like(s_ref)

        rc = jnp.dot(_tri_ones(tb, True), d_ref[...], precision=lax.Precision.HIGHEST, preferred_element_type=F32) + carry_ref[0:1, :]
        carry_ref[...] = jnp.broadcast_to(rc[0:1, :], carry_ref.shape)
        dfl = rc * jax.nn.sigmoid(-(fl_ref[...] + b_ref[...]))
        o_ref[...] = dfl
        s_ref[...] += jnp.sum(dfl, axis=0, keepdims=True)

    blk = pl.BlockSpec((tb, W), lambda i: (nb - 1 - i, 0))
    vec = pl.BlockSpec((1, W), lambda i: (0, 0))
    return pl.pallas_call(body, name=name, grid=(nb,), in_specs=[blk, blk, vec], out_specs=[blk, vec],
                          out_shape=[jax.ShapeDtypeStruct((S, W), F32), jax.ShapeDtypeStruct((1, W), F32)],
                          scratch_shapes=[pltpu.VMEM((8, W), F32)], compiler_params=_cp("arbitrary"))(dF, fl, bf)


NT_DIMS = (((1,), (1,)), ((), ()))
TN_DIMS = (((0,), (0,)), ((), ()))
HP = HEADS // 2
Q_SCALE = HEAD_DIM ** -0.5


def _head_mask(x, hh):
    lanes = lax.broadcasted_iota(jnp.int32, x.shape, 1)
    return jnp.where((lanes >= hh * HEAD_DIM) & (lanes < (hh + 1) * HEAD_DIM), x, jnp.zeros_like(x))


def _lanes_from(cols, shape):
    lanes = lax.broadcasted_iota(jnp.int32, shape, 1)
    out = jnp.zeros(shape, F32)
    for i, cvec in enumerate(cols):
        out = jnp.where(lanes == i, cvec, out)
    return out


AUG_ROWS = 16


def _qa_pair(qt, aug):
    fill = jnp.zeros((HEAD_DIM - AUG_ROWS, qt.shape[1]), qt.dtype)
    return [jnp.concatenate([qt[0:HEAD_DIM], aug[0], fill], axis=0), jnp.concatenate([aug[1], fill, qt[HEAD_DIM:]], axis=0)]


def _rows_of_head(xt, hh):
    rows = lax.broadcasted_iota(jnp.int32, xt.shape, 0)
    return jnp.where((rows >= hh * HEAD_DIM) & (rows < (hh + 1) * HEAD_DIM), xt, jnp.zeros_like(xt))


def _put_rows(ref, rows):
    for i, r in enumerate(rows):
        ref[i:i + 1, :] = r
    ref[len(rows):, :] = jnp.zeros((ref.shape[0] - len(rows), ref.shape[1]), ref.dtype)


def _diag_mask_t(s):
    rows = lax.broadcasted_iota(jnp.int32, s.shape, 0)
    cols = lax.broadcasted_iota(jnp.int32, s.shape, 1)
    return jnp.where(rows <= cols, s, NEG)


def _foxt_fwd(qkvT, ka, vtb, augq, *, tb, name, comm=()):
    S = qkvT.shape[1]
    nb = S // tb

    def body(q_ref, ka_ref, v_ref, aq_ref, o_ref, lse_ref, m0, m1, l0, l1, acc0, acc1):
        qi = pl.program_id(1)
        qa = _qa_pair(q_ref[...], aq_ref[...])
        state = ((m0, l0, acc0), (m1, l1, acc1))
        for m_sc, l_sc, acc_sc in state:
            m_sc[...] = jnp.full_like(m_sc, NEG)
            l_sc[...] = jnp.zeros_like(l_sc)
            acc_sc[...] = jnp.zeros_like(acc_sc)

        def block(kj, masked):
            k0 = pl.multiple_of(kj * tb, tb)
            logits = [jnp.dot(ka_ref[hh, pl.ds(k0, tb), :], qa[hh], preferred_element_type=F32) for hh in range(2)]
            updates = []
            for hh, (m_sc, l_sc, acc_sc) in enumerate(state):
                s = _diag_mask_t(logits[hh]) if masked else logits[hh]
                m_old = m_sc[...]
                m_new = jnp.maximum(m_old, jnp.max(s, axis=0, keepdims=True))
                alpha = jnp.exp(m_old - m_new)
                p = jnp.exp(s - m_new)
                l_sc[...] = alpha * l_sc[...] + jnp.sum(p, axis=0, keepdims=True)
                m_sc[...] = m_new
                vt = v_ref[kj, HEAD_DIM * hh:HEAD_DIM * (hh + 1), :]
                updates.append((alpha, jnp.dot(vt, p.astype(BF16), preferred_element_type=F32)))
            for (alpha, pv), (_, _, acc_sc) in zip(updates, state):
                acc_sc[...] = alpha * acc_sc[...] + pv

        def off_diagonal(kj, c):
            block(kj, False)
            return c

        lax.fori_loop(0, qi, off_diagonal, 0)
        block(qi, True)
        for hh, (m_sc, l_sc, acc_sc) in enumerate(state):
            o_ref[HEAD_DIM * hh:HEAD_DIM * (hh + 1), :] = (acc_sc[...] / l_sc[...]).astype(o_ref.dtype)
        _put_rows(lse_ref, [m_sc[...] + jnp.log(l_sc[...]) for m_sc, l_sc, _ in state])

    return _call(
        body, name=name, grid=(HP, nb), comm=comm, sem=("parallel", "parallel"), args=(qkvT, ka, vtb, augq),
        in_specs=[pl.BlockSpec((128, tb), lambda hp, qi: (hp, qi)),
                  pl.BlockSpec((None, 2, S, 128), lambda hp, qi: (hp, 0, 0, 0)),
                  pl.BlockSpec((None, nb, 128, tb), lambda hp, qi: (hp, 0, 0, 0)),
                  pl.BlockSpec((None, 2, AUG_ROWS, tb), lambda hp, qi: (hp, 0, 0, qi))],
        out_specs=[pl.BlockSpec((128, tb), lambda hp, qi: (hp, qi)), pl.BlockSpec((None, 8, tb), lambda hp, qi: (hp, 0, qi))],
        out_shape=[jax.ShapeDtypeStruct((D_MODEL, S), BF16), jax.ShapeDtypeStruct((HP, 8, S), F32)],
        scratch_shapes=[pltpu.VMEM((1, tb), F32)] * 4 + [pltpu.VMEM((HEAD_DIM, tb), F32)] * 2)


def _foxt_bwd(qkvT, qkv, ka, qtb, dotb, do, augqb, lseb, otb, *, tb, name, comm=()):
    S = qkv.shape[0]
    nb = S // tb

    def body(ka_ref, v_ref, kt_ref, qt_ref, dot_ref, qn_ref, dn_ref, aq_ref, ls_ref, ot_ref,
             dq_ref, dk_ref, dv_ref, df_ref, dr_ref, dq_sc, dk_sc, dv_sc, ds_sc, dr_sc):
        kj = pl.program_id(1)

        @pl.when(kj == 0)
        def _():
            dq_sc[...] = jnp.zeros_like(dq_sc)
            dr_sc[...] = jnp.zeros_like(dr_sc)

        dk_sc[...] = jnp.zeros_like(dk_sc)
        dv_sc[...] = jnp.zeros_like(dv_sc)
        ds_sc[...] = jnp.zeros_like(ds_sc)
        v2 = v_ref[...]
        kth = [kt_ref[HEAD_DIM * hh:HEAD_DIM * (hh + 1), :] for hh in range(2)]

        def block(qi, masked):
            q0 = pl.multiple_of(qi * tb, tb)
            qa = _qa_pair(qt_ref[qi], aq_ref[qi])
            dot, ls = dot_ref[qi], ls_ref[qi]
            prod = ot_ref[qi].astype(F32) * dot.astype(F32)
            delta = [jnp.sum(prod[HEAD_DIM * hh:HEAD_DIM * (hh + 1)], axis=0, keepdims=True) for hh in range(2)]
            qn, dn = qn_ref[pl.ds(q0, tb), :], dn_ref[pl.ds(q0, tb), :]
            for hh in range(2):
                s = jnp.dot(ka_ref[hh], qa[hh], preferred_element_type=F32)
                if masked:
                    s = _diag_mask_t(s)
                p = jnp.exp(s - ls[hh:hh + 1, :])
                dp = jnp.dot(v2, _rows_of_head(dot, hh), preferred_element_type=F32)
                ds = p * (dp - delta[hh])
                dsb = ds.astype(BF16)
                dv_sc[...] += jnp.dot(p.astype(BF16), _head_mask(dn, hh), preferred_element_type=F32)
                dk_sc[...] += jnp.dot(dsb, _head_mask(qn, hh), preferred_element_type=F32)
                dq_sc[qi, HEAD_DIM * hh:HEAD_DIM * (hh + 1), :] += jnp.dot(kth[hh], dsb, preferred_element_type=F32)
                part = ds[:, 0:128]
                for j in range(1, tb // 128):
                    part = part + ds[:, 128 * j:128 * (j + 1)]
                ds_sc[hh] += part
                dr_sc[qi, hh:hh + 1, :] += jnp.sum(ds, axis=0, keepdims=True)

        def off_diagonal(i, c):
            block(kj + 1 + i, False)
            return c

        block(kj, True)
        lax.fori_loop(0, nb - 1 - kj, off_diagonal, 0)
        dk_ref[...] = dk_sc[...].astype(dk_ref.dtype)
        dv_ref[...] = dv_sc[...].astype(dv_ref.dtype)
        df_ref[...] = _lanes_from([-jnp.sum(ds_sc[hh], axis=1, keepdims=True) for hh in range(2)], (tb, 128))

        @pl.when(kj == nb - 1)
        def _():
            dq_ref[...] = (dq_sc[...] * Q_SCALE).astype(dq_ref.dtype)
            dr_ref[...] = dr_sc[...]

    resident = lambda rows: pl.BlockSpec((None, nb) + rows, lambda hp, kj: (hp,) + (0,) * (len(rows) + 1))
    whole = pl.BlockSpec((S, 128), lambda hp, kj: (0, hp))
    kblk = lambda off: pl.BlockSpec((tb, 128), lambda hp, kj: (kj, off + hp))
    return _call(
        body, name=name, grid=(HP, nb), comm=comm, sem=("parallel", "arbitrary"),
        args=(ka, qkv, qkvT, qtb, dotb, qkv, do, augqb, lseb, otb),
        in_specs=[pl.BlockSpec((None, 2, tb, 128), lambda hp, kj: (hp, 0, kj, 0)), kblk(2 * HP),
                  pl.BlockSpec((128, tb), lambda hp, kj: (HP + hp, kj)),
                  resident((128, tb)), resident((128, tb)), whole, whole, resident((2, AUG_ROWS, tb)), resident((8, tb)),
                  resident((128, tb))],
        out_specs=[resident((128, tb)), kblk(0), kblk(0), pl.BlockSpec((None, tb, 128), lambda hp, kj: (hp, kj, 0)),
                   resident((8, tb))],
        out_shape=[jax.ShapeDtypeStruct((HP, nb, 128, tb), BF16), jax.ShapeDtypeStruct((S, D_MODEL), BF16),
                   jax.ShapeDtypeStruct((S, D_MODEL), BF16), jax.ShapeDtypeStruct((HP, S, 128), F32),
                   jax.ShapeDtypeStruct((HP, nb, 8, tb), F32)],
        scratch_shapes=[pltpu.VMEM((nb, 128, tb), F32), pltpu.VMEM((tb, 128), F32), pltpu.VMEM((tb, 128), F32),
                        pltpu.VMEM((2, tb, 128), F32), pltpu.VMEM((nb, 8, tb), F32)])


def _split3(x):
    rnd = lambda v: lax.reduce_precision(v, exponent_bits=8, mantissa_bits=7)
    hi = rnd(x)
    mid = rnd(x - hi)
    lo = rnd(x - hi - mid)
    return hi.astype(BF16), mid.astype(BF16), lo.astype(BF16)


def _blocked(xt, nb, tb):
    lead = xt.shape[:-1]
    x = xt.reshape(lead + (nb, tb))
    return jnp.moveaxis(x, -2, 1)


def _aug_rows(bias):
    ones = jnp.ones(bias.shape, BF16)
    zeros = jnp.zeros(bias.shape, BF16)
    return jnp.stack(list(_split3(bias)) + [ones] * 3 + [zeros] * (AUG_ROWS - 6), axis=2)


def _fox_keys(qkv, fcol, *, name):
    S = qkv.shape[0]
    tr = _tile(S, 2048, 16)

    def body(k_ref, f_ref, o_ref):
        hp = pl.program_id(1)
        k2 = k_ref[...]
        lanes = lax.broadcasted_iota(jnp.int32, (tr, 128), 1)
        for hh in range(2):
            nf = -jnp.sum(jnp.where(lanes == 2 * hp + hh, f_ref[...], 0.0), axis=1, keepdims=True)
            hi = nf.astype(BF16).astype(F32)
            mid = (nf - hi).astype(BF16).astype(F32)
            lo = nf - hi - mid
            base = HEAD_DIM * (1 - hh)
            aug = jnp.where((lanes >= base) & (lanes < base + 3), 1.0, 0.0)
            for j, piece in enumerate((hi, mid, lo)):
                aug = jnp.where(lanes == base + 3 + j, piece, aug)
            mine = (lanes >= HEAD_DIM * hh) & (lanes < HEAD_DIM * (hh + 1))
            o_ref[hh] = jnp.where(mine, k2, aug.astype(BF16))

    return pl.pallas_call(
        body, name=name, grid=(S // tr, HP),
        in_specs=[pl.BlockSpec((tr, 128), lambda i, hp: (i, HP + hp)), pl.BlockSpec((tr, 128), lambda i, hp: (i, 0))],
        out_specs=pl.BlockSpec((None, 2, tr, 128), lambda i, hp: (hp, 0, i, 0)),
        out_shape=jax.ShapeDtypeStruct((HP, 2, S, 128), BF16), compiler_params=_cp("parallel", "parallel"))(qkv, fcol)


HALF = S5_SUB * S5_P
NCB = 2 * HALF // 128
RE, IM = slice(0, NCB // 2), slice(NCB // 2, NCB)


def _gelu(x):
    return 0.5 * x * (1.0 + jnp.tanh(GELU_K * (x + 0.044715 * x * x * x)))


def _gelu_grad(x):
    t = jnp.tanh(GELU_K * (x + 0.044715 * x * x * x))
    return 0.5 * (1.0 + t) + 0.5 * x * (1.0 - t * t) * GELU_K * (1.0 + 3.0 * 0.044715 * x * x)


def _s5_put(buf, s, val, lt):
    for cb in range(NCB):
        buf[cb, pl.ds(s, lt, stride=S5_SUB), :] = val[:, 128 * cb:128 * (cb + 1)]


def _s5_get(buf, s, lt):
    return jnp.concatenate([buf[cb, pl.ds(s, lt, stride=S5_SUB), :] for cb in range(NCB)], axis=1)


def _s5_project_in(u_ref, wb_ref, buf, lt):
    for s in range(S5_SUB):
        _s5_put(buf, s, jnp.dot(u_ref[:, 128 * s:128 * (s + 1)], wb_ref[s], preferred_element_type=F32), lt)


def _s5_scan(buf, lam_ref, h0, lt):
    a_re, a_im = lam_ref[RE], lam_ref[IM]

    def step(t, carry):
        hr, hi = carry
        r0 = pl.multiple_of(t * S5_SUB, S5_SUB)
        nr = a_re * hr - a_im * hi + buf[RE, pl.ds(r0, S5_SUB), :]
        ni = a_re * hi + a_im * hr + buf[IM, pl.ds(r0, S5_SUB), :]
        buf[RE, pl.ds(r0, S5_SUB), :] = nr
        buf[IM, pl.ds(r0, S5_SUB), :] = ni
        return nr, ni

    return lax.fori_loop(0, lt, step, (h0[RE], h0[IM]), unroll=4)


def _s5_fwd(u, wb, wc, lam, dskip, *, lt, name, comm=()):
    S, D = u.shape
    nt = S // lt

    def body(u_ref, wb_ref, wc_ref, lam_ref, d_ref, yp_ref, yg_ref, st_ref, buf, h_sc):
        @pl.when(pl.program_id(0) == 0)
        def _():
            h_sc[...] = jnp.zeros_like(h_sc)

        st_ref[...] = h_sc[...]
        _s5_project_in(u_ref, wb_ref, buf, lt)
        hr, hi = _s5_scan(buf, lam_ref, h_sc[...], lt)
        h_sc[RE] = hr
        h_sc[IM] = hi
        for s in range(S5_SUB):
            cols = slice(128 * s, 128 * (s + 1))
            hs = _s5_get(buf, s, lt).astype(BF16)
            yp = jnp.dot(hs, wc_ref[s], preferred_element_type=F32) + d_ref[:, cols] * u_ref[:, cols].astype(F32)
            yp_ref[:, cols] = yp.astype(yp_ref.dtype)
            yg_ref[:, cols] = _gelu(yp).astype(yg_ref.dtype)

    blk = pl.BlockSpec((lt, D), lambda i: (i, 0))
    full = lambda shp: pl.BlockSpec(shp, lambda i: (0,) * len(shp))
    state = (NCB, S5_SUB, 128)
    return _call(
        body, name=name, grid=(nt,), comm=comm, sem=("arbitrary",), args=(u, wb, wc, lam, dskip),
        in_specs=[blk, full(wb.shape), full(wc.shape), full(lam.shape), full(dskip.shape)],
        out_specs=[blk, blk, pl.BlockSpec((None,) + state, lambda i: (i, 0, 0, 0))],
        out_shape=[jax.ShapeDtypeStruct((S, D), BF16), jax.ShapeDtypeStruct((S, D), BF16), jax.ShapeDtypeStruct((nt,) + state, F32)],
        scratch_shapes=[pltpu.VMEM((NCB, lt * S5_SUB, 128), F32), pltpu.VMEM(state, F32)])


def _s5_bwd(u, dyp, st, wb, wc, lam, dskip, *, lt, name, comm=()):
    S, D = u.shape
    nt = S // lt

    def body(u_ref, dy_ref, st_ref, wb_ref, wc_ref, lam_ref, d_ref, du_ref, dwb_ref, dwc_ref, dlam_ref, hbuf, gbuf, g_sc):
        @pl.when(pl.program_id(0) == 0)
        def _():
            g_sc[...] = jnp.zeros_like(g_sc)
            dwb_ref[...] = jnp.zeros_like(dwb_ref)
            dwc_ref[...] = jnp.zeros_like(dwc_ref)
            dlam_ref[...] = jnp.zeros_like(dlam_ref)

        _s5_project_in(u_ref, wb_ref, hbuf, lt)
        _s5_scan(hbuf, lam_ref, st_ref[...], lt)
        for s in range(S5_SUB):
            dys = dy_ref[:, 128 * s:128 * (s + 1)]
            _s5_put(gbuf, s, lax.dot_general(dys, wc_ref[s], NT_DIMS, preferred_element_type=F32), lt)
        a_re, a_im = lam_ref[RE], lam_ref[IM]

        def one(t, carry, hp_re, hp_im):
            gr, gi, dar, dai = carry
            r0 = pl.multiple_of(t * S5_SUB, S5_SUB)
            nr = gbuf[RE, pl.ds(r0, S5_SUB), :] + a_re * gr + a_im * gi
            ni = gbuf[IM, pl.ds(r0, S5_SUB), :] + a_re * gi - a_im * gr
            gbuf[RE, pl.ds(r0, S5_SUB), :] = nr
            gbuf[IM, pl.ds(r0, S5_SUB), :] = ni
            return nr, ni, dar + nr * hp_re + ni * hp_im, dai + ni * hp_re - nr * hp_im

        def step(k, carry):
            t = lt - 1 - k
            p0 = pl.multiple_of((t - 1) * S5_SUB, S5_SUB)
            return one(t, carry, hbuf[RE, pl.ds(p0, S5_SUB), :], hbuf[IM, pl.ds(p0, S5_SUB), :])

        init = (g_sc[RE], g_sc[IM], dlam_ref[RE], dlam_ref[IM])
        carry = lax.fori_loop(0, lt - 1, step, init, unroll=2)
        gr, gi, dar, dai = one(0, carry, st_ref[RE], st_ref[IM])
        g_sc[RE] = gr
        g_sc[IM] = gi
        dlam_ref[RE] = dar
        dlam_ref[IM] = dai
        for s in range(S5_SUB):
            cols = slice(128 * s, 128 * (s + 1))
            gs = _s5_get(gbuf, s, lt).astype(BF16)
            hs = _s5_get(hbuf, s, lt).astype(BF16)
            us, dys = u_ref[:, cols], dy_ref[:, cols]
            du = lax.dot_general(gs, wb_ref[s], NT_DIMS, preferred_element_type=F32) + d_ref[:, cols] * dys.astype(F32)
            du_ref[:, cols] = du.astype(du_ref.dtype)
            dwb_ref[s] += lax.dot_general(us, gs, TN_DIMS, preferred_element_type=F32)
            dwc_ref[s] += lax.dot_general(hs, dys, TN_DIMS, preferred_element_type=F32)

    blk = pl.BlockSpec((lt, D), lambda i: (nt - 1 - i, 0))
    full = lambda shp: pl.BlockSpec(shp, lambda i: (0,) * len(shp))
    state = (NCB, S5_SUB, 128)
    words = pltpu.VMEM((NCB, lt * S5_SUB, 128), F32)
    return _call(
        body, name=name, grid=(nt,), comm=comm, sem=("arbitrary",), args=(u, dyp, st, wb, wc, lam, dskip),
        in_specs=[blk, blk, pl.BlockSpec((None,) + state, lambda i: (nt - 1 - i, 0, 0, 0)),
                  full(wb.shape), full(wc.shape), full(lam.shape), full(dskip.shape)],
        out_specs=[blk, full(wb.shape), full(wc.shape), full(lam.shape)],
        out_shape=[jax.ShapeDtypeStruct((S, D), BF16), jax.ShapeDtypeStruct(wb.shape, F32), jax.ShapeDtypeStruct(wc.shape, F32),
                   jax.ShapeDtypeStruct(lam.shape, F32)],
        scratch_shapes=[words, words, pltpu.VMEM(state, F32)])


def _s5_disc(lam_re, lam_im, log_dt, b_re, b_im):
    dt = jnp.exp(log_dt)[:, None]
    mag = jnp.exp(lam_re * dt)
    lb_re, lb_im = mag * jnp.cos(lam_im * dt), mag * jnp.sin(lam_im * dt)
    num_re = lb_re - 1.0
    den = lam_re * lam_re + lam_im * lam_im
    k_re = (num_re * lam_re + lb_im * lam_im) / den
    k_im = (lb_im * lam_re - num_re * lam_im) / den
    bb_re = k_re[..., None] * b_re - k_im[..., None] * b_im
    bb_im = k_re[..., None] * b_im + k_im[..., None] * b_re
    return lb_re, lb_im, bb_re, bb_im


def _s5_pack(lb_re, lb_im, bb_re, bb_im, c_re, c_im):
    eye = jnp.eye(S5_SUB, dtype=F32)
    tb = lambda x: jnp.einsum("sgcp,gh->sgchp", x.reshape(S5_SUB, S5_SUB, S5_P, S5_C).transpose(0, 1, 3, 2), eye).reshape(S5_SUB, 128, HALF)
    tc = lambda x: jnp.einsum("sgpc,gh->sgphc", x.reshape(S5_SUB, S5_SUB, S5_C, S5_P).transpose(0, 1, 3, 2), eye).reshape(S5_SUB, HALF, 128)
    wb = jnp.concatenate([tb(bb_re), tb(bb_im)], axis=2).astype(BF16)
    wc = jnp.concatenate([tc(c_re), -tc(c_im)], axis=1).astype(BF16)
    lam = jnp.concatenate([lb_re.reshape(S5_SUB, HALF), lb_im.reshape(S5_SUB, HALF)], axis=1)
    return wb, wc, lam.reshape(S5_SUB, NCB, 128).transpose(1, 0, 2)


def _s5_unpack(dwb, dwc, dlam):
    db = jnp.einsum("sgcrgp->rsgpc", dwb.reshape(S5_SUB, S5_SUB, S5_C, 2, S5_SUB, S5_P)).reshape(2, S5_G, S5_P, S5_C)
    dc = jnp.einsum("srgpgc->rsgcp", dwc.reshape(S5_SUB, 2, S5_SUB, S5_P, S5_SUB, S5_C)).reshape(2, S5_G, S5_C, S5_P)
    dlam = dlam.transpose(1, 0, 2).reshape(S5_SUB, 2 * HALF)
    return (dlam[:, :HALF].reshape(S5_G, S5_P), dlam[:, HALF:].reshape(S5_G, S5_P), db[0], db[1], dc[0], -dc[1])


def _glu_fwd(yg, gl, *, name):
    S, D = yg.shape
    tm = _tile(S, 512, 8)
    blk = pl.BlockSpec((tm, D), lambda i: (i, 0))

    def body(y_ref, g_ref, o_ref):
        o_ref[...] = (y_ref[...].astype(F32) * jax.nn.sigmoid(g_ref[...].astype(F32))).astype(o_ref.dtype)

    return pl.pallas_call(body, name=name, grid=(S // tm,), in_specs=[blk, blk], out_specs=blk,
                          out_shape=jax.ShapeDtypeStruct((S, D), BF16), compiler_params=_cp("parallel"))(yg, gl)


def _glu_bwd(dy2, yg, gl, *, name):
    S, D = yg.shape
    tm = _tile(S, 512, 8)
    blk = pl.BlockSpec((tm, D), lambda i: (i, 0))

    def body(d_ref, y_ref, g_ref, da_ref, dg_ref):
        d = d_ref[...].astype(F32)
        sg = jax.nn.sigmoid(g_ref[...].astype(F32))
        da_ref[...] = d * sg
        dg_ref[...] = (d * y_ref[...].astype(F32) * sg * (1.0 - sg)).astype(dg_ref.dtype)

    return pl.pallas_call(body, name=name, grid=(S // tm,), in_specs=[blk, blk, blk], out_specs=[blk, blk],
                          out_shape=[jax.ShapeDtypeStruct((S, D), F32), jax.ShapeDtypeStruct((S, D), BF16)],
                          compiler_params=_cp("parallel"))(dy2, yg, gl)


def _gelu_bwd(da, db, ypre, u, *, name):
    S, D = ypre.shape
    tm = _tile(S, 512, 8)
    blk = pl.BlockSpec((tm, D), lambda i: (i, 0))
    vec = pl.BlockSpec((1, D), lambda i: (0, 0))

    def body(a_ref, b_ref, y_ref, u_ref, o_ref, s_ref):
        @pl.when(pl.program_id(0) == 0)
        def _():
            s_ref[...] = jnp.zeros_like(s_ref)

        dy = (a_ref[...] + b_ref[...]) * _gelu_grad(y_ref[...].astype(F32))
        o_ref[...] = dy.astype(o_ref.dtype)
        s_ref[...] += jnp.sum(dy * u_ref[...].astype(F32), axis=0, keepdims=True)

    return pl.pallas_call(body, name=name, grid=(S // tm,), in_specs=[blk, blk, blk, blk], out_specs=[blk, vec],
                          out_shape=[jax.ShapeDtypeStruct((S, D), BF16), jax.ShapeDtypeStruct((1, D), F32)],
                          compiler_params=_cp("arbitrary"))(da, db, ypre, u)


def _my_index():
    return 4 * lax.axis_index("x") + 2 * lax.axis_index("y") + lax.axis_index("c")


def _comm_plan(payloads):
    n = len(payloads)
    hbm = pl.BlockSpec(memory_space=pl.ANY)
    shapes = [jax.ShapeDtypeStruct((N_DEV,) + (x.shape if g else x.shape[1:]), x.dtype) for x, g in payloads]
    sems = [pltpu.SemaphoreType.DMA((n, N_DEV - 1)), pltpu.SemaphoreType.DMA((n, N_DEV - 1)), pltpu.SemaphoreType.DMA((n,))] if n else []
    return [hbm] * n, [hbm] * n, shapes, sems


def _comm_copies(x_refs, o_refs, gathers, send_sems, recv_sems, local_sems):
    mx, my, mc = lax.axis_index("x"), lax.axis_index("y"), lax.axis_index("c")
    me = 4 * mx + 2 * my + mc
    copies = []
    for i, (x_ref, o_ref, gather) in enumerate(zip(x_refs, o_refs, gathers)):
        src = (lambda j, r=x_ref: r) if gather else (lambda j, r=x_ref: r.at[j])
        copies.append(pltpu.make_async_copy(src(me), o_ref.at[me], local_sems.at[i]))
        for k in range(1, N_DEV):
            px, py, pc = mx ^ (k >> 2), my ^ ((k >> 1) & 1), mc ^ (k & 1)
            copies.append(pltpu.make_async_remote_copy(src(4 * px + 2 * py + pc), o_ref.at[me], send_sems.at[i, k - 1],
                                                       recv_sems.at[i, k - 1], device_id=(px, py, pc), device_id_type=MESH_ID))
    return copies


def _call(body, *, name, grid, in_specs, out_specs, out_shape, scratch_shapes, sem, args, comm=()):
    if not comm:
        outs = pl.pallas_call(body, name=name, grid=grid, in_specs=in_specs, out_specs=out_specs, out_shape=out_shape,
                              scratch_shapes=scratch_shapes, compiler_params=_cp(*sem))(*args)
        return outs, ()
    n, n_in, n_out = len(comm), len(in_specs), len(out_specs)
    c_in, c_out, c_shapes, c_sems = _comm_plan(comm)
    gathers = [g for _, g in comm]

    def wrapped(*refs):
        own_in, x_refs = refs[:n_in], refs[n_in:n_in + n]
        own_out, o_refs = refs[n_in + n:n_in + n + n_out], refs[n_in + n + n_out:n_in + 2 * n + n_out]
        own_scratch, sems = refs[n_in + 2 * n + n_out:-3], refs[-3:]
        ids = [pl.program_id(a) for a in range(len(grid))]
        first = functools.reduce(jnp.logical_and, [i == 0 for i in ids])
        last = functools.reduce(jnp.logical_and, [i == g - 1 for i, g in zip(ids, grid)])

        @pl.when(first)
        def _():
            for cp in _comm_copies(x_refs, o_refs, gathers, *sems):
                cp.start()

        body(*own_in, *own_out, *own_scratch)

        @pl.when(last)
        def _():
            for cp in _comm_copies(x_refs, o_refs, gathers, *sems):
                cp.wait()

    outs = pl.pallas_call(wrapped, name=name, grid=grid, in_specs=list(in_specs) + c_in, out_specs=list(out_specs) + c_out,
                          out_shape=list(out_shape) + c_shapes, scratch_shapes=list(scratch_shapes) + c_sems,
                          compiler_params=_cp(*["arbitrary"] * len(grid)))(*args, *[x for x, _ in comm])
    return outs[:n_out], outs[n_out:]


def _exchange_many(payloads, *, name):
    n = len(payloads)
    in_specs, out_specs, shapes, sems = _comm_plan(payloads)
    gathers = [g for _, g in payloads]

    def body(*refs):
        copies = _comm_copies(refs[:n], refs[n:2 * n], gathers, *refs[2 * n:])
        for cp in copies:
            cp.start()
        for cp in copies:
            cp.wait()

    return pl.pallas_call(body, name=name, in_specs=in_specs, out_specs=out_specs, out_shape=shapes, scratch_shapes=sems,
                          compiler_params=pltpu.CompilerParams(has_side_effects=True))(*[x for x, _ in payloads])


def _exchange(x, *, gather, name):
    return _exchange_many([(x, gather)], name=name)[0]


def _sum8(x8, *, name):
    _, R, C = x8.shape
    tr = _tile(R, 256, 16)

    def body(x_ref, o_ref):
        acc = x_ref[0].astype(F32)
        for j in range(1, N_DEV):
            acc = acc + x_ref[j].astype(F32)
        o_ref[...] = acc

    return pl.pallas_call(body, name=name, grid=(R // tr,), in_specs=[pl.BlockSpec((N_DEV, tr, C), lambda i: (0, i, 0))],
                          out_specs=pl.BlockSpec((tr, C), lambda i: (i, 0)), out_shape=jax.ShapeDtypeStruct((R, C), F32),
                          compiler_params=_cp("parallel"))(x8)


def _pack(arrs, dtype, row_mult):
    flat = jnp.concatenate([a.reshape(-1).astype(dtype) for a in arrs])
    n = flat.shape[0]
    rows = -(-n // 1024)
    rows = -(-rows // row_mult) * row_mult
    return jnp.pad(flat, (0, rows * 1024 - n)).reshape(rows, 1024)


def _unpack(buf, shapes):
    lead = buf.shape[:-2]
    flat = buf.reshape(lead + (-1,))
    out, off = [], 0
    for s in shapes:
        n = math.prod(s)
        out.append(flat[..., off:off + n].reshape(lead + tuple(s)))
        off += n
    return out


def _adaln_fwd(c_all, w, b, *, name):
    nk, D, n = w.shape

    def body(c_ref, w_ref, b_ref, o_ref):
        cv = c_ref[...]
        sc = (cv * jax.nn.sigmoid(cv)).astype(BF16)
        o_ref[...] = jnp.dot(sc, w_ref[...].astype(BF16), preferred_element_type=F32) + b_ref[...]

    return pl.pallas_call(body, name=name, grid=(nk,),
                          in_specs=[pl.BlockSpec((N_DEV, D), lambda k: (0, 0)), pl.BlockSpec((None, D, n), lambda k: (k, 0, 0)),
                                    pl.BlockSpec((None, 1, n), lambda k: (k, 0, 0))],
                          out_specs=pl.BlockSpec((None, N_DEV, n), lambda k: (k, 0, 0)),
                          out_shape=jax.ShapeDtypeStruct((nk, N_DEV, n), F32), compiler_params=_cp("parallel"))(c_all, w, b)


def _adaln_bwd(c_all, dmod, *, name):
    nk, _, n = dmod.shape
    D = c_all.shape[1]

    def body(c_ref, d_ref, dw_ref, db_ref):
        cv = c_ref[...]
        sc = (cv * jax.nn.sigmoid(cv)).astype(BF16).astype(F32)
        dm = d_ref[...]
        dw_ref[...] = lax.dot_general(sc, dm.astype(BF16).astype(F32), TN_DIMS, precision=lax.Precision.HIGHEST,
                                      preferred_element_type=F32)
        db_ref[...] = jnp.sum(dm, axis=0, keepdims=True)

    return pl.pallas_call(body, name=name, grid=(nk,),
                          in_specs=[pl.BlockSpec((N_DEV, D), lambda k: (0, 0)), pl.BlockSpec((None, N_DEV, n), lambda k: (k, 0, 0))],
                          out_specs=[pl.BlockSpec((None, D, n), lambda k: (k, 0, 0)), pl.BlockSpec((None, 1, n), lambda k: (k, 0, 0))],
                          out_shape=[jax.ShapeDtypeStruct((nk, D, n), F32), jax.ShapeDtypeStruct((nk, 1, n), F32)],
                          compiler_params=_cp("parallel"))(c_all, dmod)


def _adamw(w, g, m, v, *, name):
    R, C = w.shape
    tr = _tile(R, 256, 8)
    blk = pl.BlockSpec((tr, C), lambda i: (i, 0))
    c1 = 1.0 / (1.0 - ADAM_B1 ** ADAM_STEP)
    c2 = 1.0 / (1.0 - ADAM_B2 ** ADAM_STEP)

    def body(w_ref, g_ref, m_ref, v_ref, d_ref, nm_ref, nv_ref):
        gv = g_ref[...]
        nm = ADAM_B1 * m_ref[...] + (1.0 - ADAM_B1) * gv
        nv = ADAM_B2 * v_ref[...] + (1.0 - ADAM_B2) * (gv * gv)
        nm_ref[...] = nm
        nv_ref[...] = nv
        d_ref[...] = -ADAM_LR * ((nm * c1) / (jnp.sqrt(nv * c2) + ADAM_EPS) + ADAM_WD * w_ref[...])

    sh = jax.ShapeDtypeStruct((R, C), F32)
    return pl.pallas_call(body, name=name, grid=(R // tr,), in_specs=[blk] * 4, out_specs=[blk] * 3, out_shape=[sh] * 3,
                          compiler_params=_cp("parallel"))(w, g, m, v)


def _adamw_nd(w, g, m, v, *, name):
    shp = w.shape
    two = (-1, shp[-1]) if w.ndim > 1 else (1, -1)
    outs = _adamw(w.reshape(two), g.reshape(two), m.reshape(two), v.reshape(two), name=name)
    return [o.reshape(shp) for o in outs]


S5_TILE = 256
ATTN_BLOCK = 512
ATTN_BLOCK_FWD = 1024
BIG = ("s5_w_in", "s5_w_glu", "s5_w_out", "fox_w_in", "fox_w_out", "ffn_w_up", "ffn_w_down")
WEIGHTS = ("norm_g", "ada_w", "ada_b", "s5_w_in", "s5_lam_re", "s5_lam_im", "s5_log_dt", "s5_b_re", "s5_b_im", "s5_c_re",
           "s5_c_im", "s5_d", "s5_w_glu", "s5_w_out", "fox_w_in", "fox_b_f", "fox_w_out", "ffn_w_up", "ffn_conv_w",
           "ffn_conv_b", "ffn_w_down", "final_g")
REPLICATED = ("s5_lam_re", "s5_lam_im", "s5_log_dt", "s5_b_re", "s5_b_im", "s5_c_re", "s5_c_im", "s5_d", "fox_b_f",
              "ffn_conv_b", "final_g")


def _row_group(arrs):
    return jnp.concatenate([a.reshape(-1, D_MODEL) for a in arrs], axis=0)


def _cols_join(g):
    return jnp.concatenate([g[j] for j in range(N_DEV)], axis=1)


def _cols_split(full):
    n = full.shape[1] // N_DEV
    return jnp.stack([full[:, j * n:(j + 1) * n] for j in range(N_DEV)])


def kernel(x, c, norm_g, ada_w, ada_b, s5_w_in, s5_lam_re, s5_lam_im, s5_log_dt, s5_b_re, s5_b_im, s5_c_re, s5_c_im, s5_d, s5_w_glu, s5_w_out, fox_w_in, fox_b_f, fox_w_out, ffn_w_up, ffn_conv_w, ffn_conv_b, ffn_w_down, final_g, loss_target, m_norm_g, m_ada_w, m_ada_b, m_s5_w_in, m_s5_lam_re, m_s5_lam_im, m_s5_log_dt, m_s5_b_re, m_s5_b_im, m_s5_c_re, m_s5_c_im, m_s5_d, m_s5_w_glu, m_s5_w_out, m_fox_w_in, m_fox_b_f, m_fox_w_out, m_ffn_w_up, m_ffn_conv_w, m_ffn_conv_b, m_ffn_w_down, m_final_g, v_norm_g, v_ada_w, v_ada_b, v_s5_w_in, v_s5_lam_re, v_s5_lam_im, v_s5_log_dt, v_s5_b_re, v_s5_b_im, v_s5_c_re, v_s5_c_im, v_s5_d, v_s5_w_glu, v_s5_w_out, v_fox_w_in, v_fox_b_f, v_fox_w_out, v_ffn_w_up, v_ffn_conv_w, v_ffn_conv_b, v_ffn_w_down, v_final_g):
    args = dict(locals())
    W = {n: args[n] for n in WEIGHTS}
    M = {n: args["m_" + n] for n in WEIGHTS}
    V = {n: args["v_" + n] for n in WEIGHTS}
    D, F = D_MODEL, D_FF
    me = _my_index()
    h0 = x[0]
    S = h0.shape[0]
    lt = min(S5_TILE, S)
    tb = min(ATTN_BLOCK, S)

    g0 = _exchange(_pack([c, norm_g, ffn_conv_w], F32, 8), gather=True, name="gather_small_in")
    c_all, ng_all, cw_all = _unpack(g0, [(D,), norm_g.shape, ffn_conv_w.shape])
    ng_full = ng_all.transpose(1, 2, 0, 3).reshape(2, 2, D)
    cw_full = cw_all.transpose(1, 2, 0, 3).reshape(2, 3, F)

    ncol = ada_w.shape[-1]
    modp = _adaln_fwd(c_all, ada_w.reshape(4, D, ncol), ada_b.reshape(4, 1, ncol), name="adaln_fwd")
    g1 = _exchange(_pack([modp], F32, 8), gather=True, name="gather_adaln")
    (mod_all,) = _unpack(g1, [modp.shape])
    mod = lax.dynamic_index_in_dim(mod_all, me, axis=2, keepdims=False).transpose(1, 0, 2).reshape(4, 3 * D)
    shift = [mod[k:k + 1, :D] for k in range(4)]
    scale = [mod[k:k + 1, D:2 * D] for k in range(4)]
    gate = [mod[k:k + 1, 2 * D:] for k in range(4)]
    gain = [_row(ng_full[k // 2, k % 2]) for k in range(4)]

    b16 = lambda a: a.astype(BF16)
    rows = D // N_DEV
    full = {"s5_w_in": _exchange(b16(s5_w_in[0]), gather=True, name="gather_s5_w_in").reshape(D, D)}
    bf_pad = jnp.pad(fox_b_f, ((0, 0), (0, 128 - HEADS)))

    def ffn_fwd(h, k, layer, comm=()):
        hn = _modulate(h, gain[k], shift[k], scale[k], name=f"modulate{k}")
        up = _mm(hn, full[f"ffn_w_up{layer}"], name=f"ffn_up{layer}", o_h=True, out_dtype=BF16, tn=1408, comm=comm)
        up, exchanged = up if comm else (up, ())
        z = _conv_gate_fwd(up, cw_full[layer], ffn_conv_b[layer:layer + 1], name=f"conv_gate{layer}")
        m, h_out = _mm(z, full[f"ffn_w_down{layer}"], name=f"ffn_down{layer}", res=h, gate=gate[k])
        return h_out, (hn, up, z, m), exchanged

    lb_re, lb_im, bb_re, bb_im = _s5_disc(s5_lam_re[0], s5_lam_im[0], s5_log_dt[0], s5_b_re[0], s5_b_im[0])
    wb, wc, lam = _s5_pack(lb_re, lb_im, bb_re, bb_im, s5_c_re[0], s5_c_im[0])
    hn0 = _modulate(h0, gain[0], shift[0], scale[0], name="modulate0")
    u = _mm(hn0, full["s5_w_in"], name="s5_in", out_dtype=BF16)
    (ypre, yg, st), (g_up0, g_rows0) = _s5_fwd(
        u, wb, wc, lam, s5_d, lt=lt, name="s5_scan",
        comm=[(b16(ffn_w_up[0]), True), (_row_group([b16(s5_w_glu), b16(s5_w_out), b16(ffn_w_down[0])]), True)])
    full["ffn_w_up0"] = _cols_join(g_up0)
    full["s5_w_glu"] = g_rows0[:, :rows].reshape(D, D)
    full["s5_w_out"] = g_rows0[:, rows:2 * rows].reshape(D, D)
    full["ffn_w_down0"] = g_rows0[:, 2 * rows:].reshape(F, D)
    gl = _mm(yg, full["s5_w_glu"], name="s5_glu", out_dtype=BF16)
    y2 = _glu_fwd(yg, gl, name="s5_glu_gate")
    m0, h1 = _mm(y2, full["s5_w_out"], name="s5_out", res=h0, gate=gate[0])
    h2, (hn1, up0, z0, m1), (g_fox, g_fox_out) = ffn_fwd(h1, 1, 0, comm=[(b16(fox_w_in[0]), True), (b16(fox_w_out[0]), True)])
    full["fox_w_in"] = _cols_join(g_fox)
    full["fox_w_out"] = g_fox_out.reshape(D, D)
    w_proj = jnp.pad(full["fox_w_in"], ((0, 0), (0, 3 * D + 128 - full["fox_w_in"].shape[1])))
    w_qkv, w_f = w_proj[:, :3 * D], w_proj[:, 3 * D:]

    hn2 = _modulate(h2, gain[2], shift[2], scale[2], name="modulate2")
    nb = S // tb
    w_qkv_s = jnp.concatenate([w_qkv[:, :D] * Q_SCALE, w_qkv[:, D:]], axis=1)
    qkv, qkvT = _mm(hn2, w_qkv_s, name="fox_qkv", out_dtype=BF16, with_t=True)
    fl = _mm(hn2, w_f, name="fox_f", out_dtype=F32)
    fcol = _fgate_fwd(fl, bf_pad, name="fox_fgate")
    f_heads = fcol[:, :HEADS].T.reshape(HP, 2, S)
    ka = _fox_keys(qkv, fcol, name="fox_keys")
    tbf = min(ATTN_BLOCK_FWD, S)
    vtb = _blocked(qkvT[2 * D:].reshape(HP, 128, S), S // tbf, tbf)
    aug = _aug_rows(f_heads)
    (oT, lse), (g_up1, g_down1) = _foxt_fwd(qkvT, ka, vtb, aug, tb=tbf, name="fox_attn",
                                            comm=[(b16(ffn_w_up[1]), True), (b16(ffn_w_down[1]), True)])
    full["ffn_w_up1"] = _cols_join(g_up1)
    full["ffn_w_down1"] = g_down1.reshape(F, D)
    m2, h3 = _mm(oT, full["fox_w_out"], name="fox_out", ta=True, res=h2, gate=gate[2])
    h4, (hn3, up1, z1, m3), _ = ffn_fwd(h3, 3, 1)

    lblk, dh, head_sums, dm = _loss_head(h4, _row(final_g), loss_target[0], m3, gate[3], name="loss_head")
    d_final_g, dgate = head_sums[0:1], head_sums[1:2]
    loss = lax.psum(lblk[0, 0], ("x", "y", "c"))

    G = {}
    dmod = [None] * 4
    mixer_out = [m0, m1, m2, m3]

    def norm_bwd(h, dhn, dh_in, k, dgate_k):
        if k == 0:
            dh_out, sums = _modulate_bwd(h, dhn, dh_in, gain[k], scale[k], name=f"modulate_bwd{k}")
            dm_below = dgate_below = None
        else:
            dh_out, sums, dm_below = _modulate_bwd(h, dhn, dh_in, gain[k], scale[k], mixer_out[k - 1], gate[k - 1],
                                                   name=f"modulate_bwd{k}")
            dgate_below = sums[3:4]
        dmod[k] = jnp.concatenate([sums[0:1], sums[1:2], dgate_k], axis=1)
        return dh_out, sums[2], dm_below, dgate_below

    def ffn_bwd(dh_in, dm, dgate_k, h, k, layer, saved):
        hn, up, z, m = saved
        dz = _mm(dm, full[f"ffn_w_down{layer}"], name=f"ffn_down_dx{layer}", tb=True, out_dtype=BF16, tn=1408)
        dw_down = _mm(z, dm, name=f"ffn_down_dw{layer}", ta=True, tm=1408, tk=2048)
        d_up, cs = _conv_gate_bwd(up, dz, cw_full[layer], ffn_conv_b[layer:layer + 1], name=f"conv_gate_bwd{layer}")
        dhn = _mm(d_up, full[f"ffn_w_up{layer}"], name=f"ffn_up_dx{layer}", tb=True, a_h=True, tk=2816)
        dw_up = _mm(hn, d_up, name=f"ffn_up_dw{layer}", ta=True, b_h=True, tm=1024, tn=1408, tk=2048)
        dh_out, dg, dm_below, dgate_below = norm_bwd(h, dhn, dh_in, k, dgate_k)
        return dh_out, dm_below, dgate_below, dg, dw_up, dw_down, cs[0:3], cs[3]

    dh, dm, dgate, dg3, dw_up1, dw_down1, dcw1, dcb1 = ffn_bwd(dh, dm, dgate, h3, 3, 1, (hn3, up1, z1, m3))

    do, doT = _mm(dm, full["fox_w_out"], name="fox_out_dx", tb=True, out_dtype=BF16, with_t=True)
    dw_fox_out = _mm(oT, dm, name="fox_out_dw", tk=2048)
    to_rows = lambda g: b16(g).reshape(N_DEV, -1, D)
    (dqtb, dk, dv, dfk, dfq), (r_up1, r_down1) = _foxt_bwd(
        qkvT, qkv, ka, _blocked(qkvT[:D].reshape(HP, 128, S), nb, tb), _blocked(doT.reshape(HP, 128, S), nb, tb), do,
        _blocked(aug, nb, tb), _blocked(lse, nb, tb), _blocked(oT.reshape(HP, 128, S), nb, tb), tb=tb, name="fox_attn_bwd",
        comm=[(_cols_split(b16(dw_up1)), False), (to_rows(dw_down1), False)])
    dq = dqtb.transpose(1, 3, 0, 2).reshape(S, D)
    dF = dfk[:, :, :2].transpose(1, 0, 2).reshape(S, HEADS) + dfq[:, :, :2, :].transpose(1, 3, 0, 2).reshape(S, HEADS)
    dF = jnp.pad(dF, ((0, 0), (0, 128 - HEADS)))
    dfl, dbf = _fgate_bwd(dF, fl, bf_pad, name="fox_fgate_bwd")
    dproj = jnp.concatenate([dq, dk, dv, dfl.astype(BF16)], axis=1)
    dhn = _mm(dproj, w_proj, name="fox_in_dx", tb=True)
    dw_proj = _mm(hn2, dproj, name="fox_in_dw", ta=True, tn=640, tk=2048)
    dw_fox_in = dw_proj[:, :full["fox_w_in"].shape[1]]
    dh, dg2, dm, dgate = norm_bwd(h2, dhn, dh, 2, dgate)

    dh, dm, dgate, dg1, dw_up0, dw_down0, dcw0, dcb0 = ffn_bwd(dh, dm, dgate, h1, 1, 0, (hn1, up0, z0, m1))

    dy2 = _mm(dm, full["s5_w_out"], name="s5_out_dx", tb=True, out_dtype=BF16)
    G["s5_w_out"] = _mm(y2, dm, name="s5_out_dw", ta=True, tk=2048)
    da, dgl = _glu_bwd(dy2, yg, gl, name="s5_glu_bwd")
    dyg_b = _mm(dgl, full["s5_w_glu"], name="s5_glu_dx", tb=True)
    G["s5_w_glu"] = _mm(yg, dgl, name="s5_glu_dw", ta=True, tk=2048)
    dyp, dd = _gelu_bwd(da, dyg_b, ypre, u, name="s5_gelu_bwd")
    row_grads = [dw_fox_out, dw_down0, G["s5_w_out"], G["s5_w_glu"]]
    (du, dwb, dwc, dlam), (r_up0, r_rows0, r_fox) = _s5_bwd(
        u, dyp, st, wb, wc, lam, s5_d, lt=lt, name="s5_scan_bwd",
        comm=[(_cols_split(b16(dw_up0)), False), (jnp.concatenate([to_rows(g) for g in row_grads], axis=1), False),
              (_cols_split(b16(dw_fox_in)), False)])
    dhn = _mm(du, full["s5_w_in"], name="s5_in_dx", tb=True)
    G["s5_w_in"] = _mm(hn0, du, name="s5_in_dw", ta=True, tk=2048)
    dh, dg0, _, _ = norm_bwd(h0, dhn, dh, 0, dgate)
    grad_x = dh[None]

    dlb_re, dlb_im, dbb_re, dbb_im, dc_re, dc_im = _s5_unpack(dwb, dwc, dlam)
    _, disc_vjp = jax.vjp(_s5_disc, s5_lam_re[0], s5_lam_im[0], s5_log_dt[0], s5_b_re[0], s5_b_im[0])
    dlam_re, dlam_im, dlog_dt, db_re, db_im = disc_vjp((dlb_re, dlb_im, dbb_re, dbb_im))

    g2 = _exchange(_pack([jnp.concatenate(dmod, axis=0)], F32, 8), gather=True, name="gather_dmod")
    (dmod_all,) = _unpack(g2, [(4, 3 * D)])
    dmod_mine = _columns_of_mod(dmod_all, me, ncol)
    d_ada_w, d_ada_b = _adaln_bwd(c_all, dmod_mine, name="adaln_bwd")
    G["ada_w"] = d_ada_w.reshape(ada_w.shape)
    G["ada_b"] = d_ada_b.reshape(ada_b.shape)

    small_full = {
        "norm_g": jnp.stack([dg0, dg1, dg2, dg3]).reshape(2, 2, D),
        "ffn_conv_w": jnp.stack([dcw0, dcw1]),
        "s5_lam_re": dlam_re[None], "s5_lam_im": dlam_im[None], "s5_log_dt": dlog_dt[None],
        "s5_b_re": db_re[None], "s5_b_im": db_im[None], "s5_c_re": dc_re[None], "s5_c_im": dc_im[None],
        "s5_d": dd, "fox_b_f": dbf[:, :HEADS], "ffn_conv_b": jnp.stack([dcb0, dcb1]), "final_g": d_final_g[0],
    }
    names = tuple(small_full)
    small_pack = _pack([small_full[n] for n in names], F32, 8 * N_DEV)
    r_in, r_small = _exchange_many([(to_rows(G["s5_w_in"]), False), (small_pack.reshape(N_DEV, -1, 1024), False)],
                                   name="scatter_s5_in_and_small_grads")
    G["s5_w_in"] = _sum8(r_in, name="sum_s5_in_grads")[None]
    rows0_sum = _sum8(r_rows0, name="sum_rows0_grads")
    down_rows = F // N_DEV
    G["fox_w_out"] = rows0_sum[:rows][None]
    G["s5_w_out"] = rows0_sum[rows + down_rows:2 * rows + down_rows][None]
    G["s5_w_glu"] = rows0_sum[2 * rows + down_rows:][None]
    G["ffn_w_down"] = jnp.stack([rows0_sum[rows:rows + down_rows], _sum8(r_down1, name="sum_down1_grads")])
    G["ffn_w_up"] = jnp.stack([_sum8(r_up0, name="sum_up0_grads"), _sum8(r_up1, name="sum_up1_grads")])
    G["fox_w_in"] = _sum8(r_fox, name="sum_fox_w_in")[None]

    g3 = _exchange(_sum8(r_small, name="sum_small_grads"), gather=True, name="gather_small_sums")
    summed = dict(zip(names, _unpack(g3.reshape(small_pack.shape), [small_full[n].shape for n in names])))
    G["norm_g"] = lax.dynamic_slice_in_dim(summed["norm_g"], me * norm_g.shape[-1], norm_g.shape[-1], axis=2)
    G["ffn_conv_w"] = lax.dynamic_slice_in_dim(summed["ffn_conv_w"], me * ffn_conv_w.shape[-1], ffn_conv_w.shape[-1], axis=2)
    for n in REPLICATED:
        G[n] = summed[n]

    delta, new_m, new_v = {}, {}, {}
    small = tuple(n for n in WEIGHTS if n not in BIG and n != "ada_w")
    for n in WEIGHTS:
        if n not in small:
            delta[n], new_m[n], new_v[n] = _adamw_nd(W[n], G[n], M[n], V[n], name=f"adamw_{n}")
    packed = [_pack([src[n] for n in small], F32, 8) for src in (W, G, M, V)]
    for dst, buf in zip((delta, new_m, new_v), _adamw(*packed, name="adamw_small")):
        dst.update(zip(small, _unpack(buf, [W[n].shape for n in small])))

    return (loss, grad_x, *[G[n] for n in WEIGHTS], *[delta[n] for n in WEIGHTS], *[new_m[n] for n in WEIGHTS],
            *[new_v[n] for n in WEIGHTS])


def _columns_of_mod(dmod_all, me, ncol):
    flat = lax.dynamic_slice_in_dim(dmod_all, me * ncol, ncol, axis=2)
    return flat.transpose(1, 0, 2)
```

```python
import functools
import math

import jax
import jax.numpy as jnp
from jax import lax
from jax.experimental import pallas as pl
from jax.experimental.pallas import tpu as pltpu

F32, BF16 = jnp.float32, jnp.bfloat16
EPS = 1e-6
N_DEV = 8
D_MODEL = 1024
D_FF = 2816
HEADS = 16
HEAD_DIM = 64
S5_G, S5_P, S5_C = 64, 64, 16
S5_SUB = 8
V7X_VMEM_LIMIT = 56 * 1024 * 1024
NEG = -1e30
ADAM_LR, ADAM_B1, ADAM_B2, ADAM_EPS, ADAM_WD, ADAM_STEP = 1e-3, 0.9, 0.999, 1e-8, 0.01, 10
GELU_K = math.sqrt(2.0 / math.pi)
MESH_ID = pl.DeviceIdType.MESH


def _cp(*sem):
    return pltpu.CompilerParams(dimension_semantics=sem, vmem_limit_bytes=V7X_VMEM_LIMIT)


def _tile(n, target, mult=128):
    if n <= target:
        return n
    t = (target // mult) * mult
    while t >= mult:
        if n % t == 0:
            return t
        t -= mult
    return n


def _row(v):
    return v.reshape(1, -1).astype(F32)


def _mm(a, b, *, name, ta=False, tb=False, out_dtype=F32, tm=1024, tn=1024, tk=None, res=None, gate=None,
        a_h=False, b_h=False, o_h=False, with_t=False, comm=()):
    if a_h:
        M, K = a.shape[1], 2 * a.shape[2]
    elif ta:
        K, M = a.shape
    else:
        M, K = a.shape
    if b_h:
        N = 2 * b.shape[2]
    else:
        N = b.shape[0] if tb else b.shape[1]
    half_n = N // 2
    tm = _tile(M, tm, 128 if ta else 8)
    tn = _tile(half_n if (b_h or o_h) else N, tn)
    tk = K if tk is None else _tile(K // 2 if a_h else K, tk)
    nk = K // tk
    nkh, nnh = (K // 2) // tk if a_h else 1, half_n // tn
    if a_h:
        a_spec = pl.BlockSpec((None, tm, tk), lambda i, j, k: (k // nkh, i, k % nkh))
    elif ta:
        a_spec = pl.BlockSpec((tk, tm), lambda i, j, k: (k, i))
    else:
        a_spec = pl.BlockSpec((tm, tk), lambda i, j, k: (i, k))
    if b_h:
        b_spec = pl.BlockSpec((None, tk, tn), lambda i, j, k: (j // nnh, k, j % nnh))
    elif tb:
        b_spec = pl.BlockSpec((tn, tk), lambda i, j, k: (j, k))
    else:
        b_spec = pl.BlockSpec((tk, tn), lambda i, j, k: (k, j))
    if o_h:
        o_spec = pl.BlockSpec((None, tm, tn), lambda i, j, k: (j // nnh, i, j % nnh))
    else:
        o_spec = pl.BlockSpec((tm, tn), lambda i, j, k: (i, j))
    dn = (((0 if ta else 1,), (1 if tb else 0,)), ((), ()))
    fused = res is not None

    def body(*refs):
        if fused:
            a_ref, b_ref, r_ref, g_ref, m_ref, o_ref, acc_ref = refs
        elif with_t:
            a_ref, b_ref, o_ref, t_ref, acc_ref = refs
        else:
            a_ref, b_ref, o_ref, acc_ref = refs
        p = lax.dot_general(a_ref[...].astype(BF16), b_ref[...].astype(BF16), dn, preferred_element_type=F32)

        def finish(acc):
            if fused:
                m_ref[...] = acc.astype(m_ref.dtype)
                o_ref[...] = r_ref[...] + g_ref[...] * acc
            else:
                o_ref[...] = acc.astype(o_ref.dtype)
                if with_t:
                    t_ref[...] = acc.T.astype(t_ref.dtype)

        if nk == 1:
            finish(p)
        else:
            k = pl.program_id(2)

            @pl.when(k == 0)
            def _():
                acc_ref[...] = p

            @pl.when(k > 0)
            def _():
                acc_ref[...] += p

            @pl.when(k == nk - 1)
            def _():
                finish(acc_ref[...])

    in_specs = [a_spec, b_spec]
    args = [a, b]
    if fused:
        in_specs += [o_spec, pl.BlockSpec((1, tn), lambda i, j, k: (0, j))]
        args += [res, gate]
        out_shape = [jax.ShapeDtypeStruct((M, N), BF16), jax.ShapeDtypeStruct((M, N), F32)]
        out_specs = [o_spec, o_spec]
    else:
        out_shape = [jax.ShapeDtypeStruct((2, M, half_n) if o_h else (M, N), out_dtype)]
        out_specs = [o_spec]
        if with_t:
            out_shape.append(jax.ShapeDtypeStruct((N, M), out_dtype))
            out_specs.append(pl.BlockSpec((tn, tm), lambda i, j, k: (j, i)))
    outs, exchanged = _call(
        body, name=name, grid=(M // tm, N // tn, nk), in_specs=in_specs, out_specs=out_specs, out_shape=out_shape,
        scratch_shapes=[pltpu.VMEM((tm, tn) if nk > 1 else (8, 128), F32)], sem=("parallel", "parallel", "arbitrary"),
        args=args, comm=comm)
    outs = tuple(outs) if (fused or with_t) else outs[0]
    return (outs, exchanged) if comm else outs


def _modulate(h, g, shift, scale, *, name):
    S, D = h.shape
    tm = _tile(S, 512, 8)
    vec = pl.BlockSpec((1, D), lambda i: (0, 0))
    blk = pl.BlockSpec((tm, D), lambda i: (i, 0))

    def body(h_ref, g_ref, sh_ref, sc_ref, o_ref):
        x = h_ref[...]
        r = lax.rsqrt(jnp.mean(x * x, axis=-1, keepdims=True) + EPS)
        o_ref[...] = ((x * r * g_ref[...]) * (1.0 + sc_ref[...]) + sh_ref[...]).astype(o_ref.dtype)

    return pl.pallas_call(body, name=name, grid=(S // tm,), in_specs=[blk, vec, vec, vec], out_specs=blk,
                          out_shape=jax.ShapeDtypeStruct((S, D), BF16), compiler_params=_cp("parallel"))(h, g, shift, scale)


def _modulate_bwd(h, dhn, dh_in, g, scale, m_prev=None, gate_prev=None, *, name):
    S, D = h.shape
    tm = _tile(S, 512, 8)
    vec = pl.BlockSpec((1, D), lambda i: (0, 0))
    blk = pl.BlockSpec((tm, D), lambda i: (i, 0))
    sums = pl.BlockSpec((8, D), lambda i: (0, 0))
    below = m_prev is not None

    def body(h_ref, dhn_ref, dhi_ref, g_ref, sc_ref, *rest):
        dh_ref, s_ref = rest[-3:-1] if below else rest[-2:]

        @pl.when(pl.program_id(0) == 0)
        def _():
            s_ref[...] = jnp.zeros_like(s_ref)

        x = h_ref[...]
        r = lax.rsqrt(jnp.mean(x * x, axis=-1, keepdims=True) + EPS)
        xhat = x * r
        gv = g_ref[...]
        dhn_v = dhn_ref[...].astype(F32)
        dn = dhn_v * (1.0 + sc_ref[...])
        s_ref[0:1, :] += jnp.sum(dhn_v, axis=0, keepdims=True)
        s_ref[1:2, :] += jnp.sum(dhn_v * (xhat * gv), axis=0, keepdims=True)
        s_ref[2:3, :] += jnp.sum(dn * xhat, axis=0, keepdims=True)
        dxh = dn * gv
        dh = dhi_ref[...] + r * (dxh - xhat * jnp.mean(dxh * xhat, axis=-1, keepdims=True))
        dh_ref[...] = dh
        if below:
            mp_ref, gp_ref, dm_ref = rest[0], rest[1], rest[-1]
            dm_ref[...] = (dh * gp_ref[...]).astype(dm_ref.dtype)
            s_ref[3:4, :] += jnp.sum(dh * mp_ref[...].astype(F32), axis=0, keepdims=True)

    extra_in, extra_args = ([blk, vec], [m_prev, gate_prev]) if below else ([], [])
    return pl.pallas_call(body, name=name, grid=(S // tm,), in_specs=[blk, blk, blk, vec, vec] + extra_in,
                          out_specs=[blk, sums] + ([blk] if below else []),
                          out_shape=[jax.ShapeDtypeStruct((S, D), F32), jax.ShapeDtypeStruct((8, D), F32)]
                          + ([jax.ShapeDtypeStruct((S, D), BF16)] if below else []),
                          compiler_params=_cp("arbitrary"))(h, dhn, dh_in, g, scale, *extra_args)


def _loss_head(h, g, tgt, m_prev, gate_prev, *, name):
    S, D = h.shape
    tm = _tile(S, 512, 8)
    vec = pl.BlockSpec((1, D), lambda i: (0, 0))
    blk = pl.BlockSpec((tm, D), lambda i: (i, 0))
    lss = pl.BlockSpec((8, 128), lambda i: (0, 0))
    sums = pl.BlockSpec((8, D), lambda i: (0, 0))

    def body(h_ref, g_ref, t_ref, mp_ref, gp_ref, l_ref, dh_ref, s_ref, dm_ref):
        @pl.when(pl.program_id(0) == 0)
        def _():
            l_ref[...] = jnp.zeros_like(l_ref)
            s_ref[...] = jnp.zeros_like(s_ref)

        x = h_ref[...]
        r = lax.rsqrt(jnp.mean(x * x, axis=-1, keepdims=True) + EPS)
        xhat = x * r
        gv = g_ref[...]
        e = xhat * gv - t_ref[...]
        l_ref[...] += 0.5 * jnp.sum(jnp.mean(e * e, axis=-1, keepdims=True))
        dy = e * (1.0 / D)
        s_ref[0:1, :] += jnp.sum(dy * xhat, axis=0, keepdims=True)
        dxh = dy * gv
        dh = r * (dxh - xhat * jnp.mean(dxh * xhat, axis=-1, keepdims=True))
        dh_ref[...] = dh
        dm_ref[...] = (dh * gp_ref[...]).astype(dm_ref.dtype)
        s_ref[1:2, :] += jnp.sum(dh * mp_ref[...].astype(F32), axis=0, keepdims=True)

    return pl.pallas_call(body, name=name, grid=(S // tm,), in_specs=[blk, vec, blk, blk, vec], out_specs=[lss, blk, sums, blk],
                          out_shape=[jax.ShapeDtypeStruct((8, 128), F32), jax.ShapeDtypeStruct((S, D), F32),
                                     jax.ShapeDtypeStruct((8, D), F32), jax.ShapeDtypeStruct((S, D), BF16)],
                          compiler_params=_cp("arbitrary"))(h, g, tgt, m_prev, gate_prev)


def _shift_down(x, k, edge):
    tm = x.shape[0]
    rows = lax.broadcasted_iota(jnp.int32, x.shape, 0)
    out = pltpu.roll(x, k, 0)
    for j in range(k):
        out = jnp.where(rows == j, edge[8 - k + j:8 - k + j + 1, :], out)
    return out


def _shift_up(x, k, edge):
    tm = x.shape[0]
    rows = lax.broadcasted_iota(jnp.int32, x.shape, 0)
    out = pltpu.roll(x, tm - k, 0)
    for j in range(k):
        out = jnp.where(rows == tm - k + j, edge[j:j + 1, :], out)
    return out


def _conv_gate_fwd(up, cw, cb, *, name):
    _, S, F = up.shape
    tf = _tile(F, 1408)
    nf = F // tf
    tm = _tile(S, 512, 8)

    def body(a_ref, b_ref, w_ref, cb_ref, z_ref, edge_ref):
        @pl.when(pl.program_id(1) == 0)
        def _():
            edge_ref[...] = jnp.zeros_like(edge_ref)

        a = a_ref[...].astype(F32)
        edge = edge_ref[...]
        w = w_ref[...]
        ac = cb_ref[...] + w[2:3, :] * a + w[1:2, :] * _shift_down(a, 1, edge) + w[0:1, :] * _shift_down(a, 2, edge)
        edge_ref[...] = a[tm - 8:tm, :]
        z_ref[...] = (ac * jax.nn.sigmoid(ac) * b_ref[...].astype(F32)).astype(z_ref.dtype)

    return pl.pallas_call(
        body, name=name, grid=(nf, S // tm),
        in_specs=[pl.BlockSpec((None, tm, tf), lambda j, i: (0, i, j)), pl.BlockSpec((None, tm, tf), lambda j, i: (1, i, j)),
                  pl.BlockSpec((3, tf), lambda j, i: (0, j)), pl.BlockSpec((1, tf), lambda j, i: (0, j))],
        out_specs=pl.BlockSpec((tm, tf), lambda j, i: (i, j)),
        out_shape=jax.ShapeDtypeStruct((S, F), BF16), scratch_shapes=[pltpu.VMEM((8, tf), F32)],
        compiler_params=_cp("parallel", "arbitrary"))(up, up, cw, cb)


def _conv_gate_bwd(up, dz, cw, cb, *, name):
    _, S, F = up.shape
    tf = _tile(F, 1408)
    nf = F // tf
    tm = _tile(S, 512, 8)
    nt = S // tm
    hb = tm // 8

    def body(a_ref, ah_ref, b_ref, dz_ref, w_ref, cb_ref, d_ref, s_ref, edge_ref):
        i = pl.program_id(1)

        @pl.when(i == 0)
        def _():
            edge_ref[...] = jnp.zeros_like(edge_ref)
            s_ref[...] = jnp.zeros_like(s_ref)

        a = a_ref[...].astype(F32)
        halo = jnp.where(i == nt - 1, 0.0, ah_ref[...].astype(F32))
        w = w_ref[...]
        a1 = _shift_down(a, 1, halo)
        a2 = _shift_down(a, 2, halo)
        ac = cb_ref[...] + w[2:3, :] * a + w[1:2, :] * a1 + w[0:1, :] * a2
        sg = jax.nn.sigmoid(ac)
        dzv = dz_ref[...].astype(F32)
        si = ac * sg
        d_ref[1] = (dzv * si).astype(d_ref.dtype)
        dac = (dzv * b_ref[...].astype(F32)) * (sg + si * (1.0 - sg))
        s_ref[0:1, :] += jnp.sum(dac * a2, axis=0, keepdims=True)
        s_ref[1:2, :] += jnp.sum(dac * a1, axis=0, keepdims=True)
        s_ref[2:3, :] += jnp.sum(dac * a, axis=0, keepdims=True)
        s_ref[3:4, :] += jnp.sum(dac, axis=0, keepdims=True)
        edge = edge_ref[...]
        da = w[2:3, :] * dac + w[1:2, :] * _shift_up(dac, 1, edge) + w[0:1, :] * _shift_up(dac, 2, edge)
        edge_ref[...] = dac[0:8, :]
        d_ref[0] = da.astype(d_ref.dtype)

    tile = lambda hlf: pl.BlockSpec((None, tm, tf), lambda j, i: (hlf, nt - 1 - i, j))
    d_up, sums = pl.pallas_call(
        body, name=name, grid=(nf, nt),
        in_specs=[tile(0),
                  pl.BlockSpec((None, 8, tf), lambda j, i: (0, jnp.maximum((nt - 1 - i) * hb - 1, 0), j)),
                  tile(1), pl.BlockSpec((tm, tf), lambda j, i: (nt - 1 - i, j)),
                  pl.BlockSpec((3, tf), lambda j, i: (0, j)), pl.BlockSpec((1, tf), lambda j, i: (0, j))],
        out_specs=[pl.BlockSpec((2, tm, tf), lambda j, i: (0, nt - 1 - i, j)), pl.BlockSpec((8, tf), lambda j, i: (0, j))],
        out_shape=[jax.ShapeDtypeStruct((2, S, F), BF16), jax.ShapeDtypeStruct((8, F), F32)],
        scratch_shapes=[pltpu.VMEM((8, tf), F32)],
        compiler_params=_cp("parallel", "arbitrary"))(up, up, up, dz, cw, cb)
    return d_up, sums


def _log_sigmoid(x):
    return jnp.minimum(x, 0.0) - jnp.log(1.0 + jnp.exp(-jnp.abs(x)))


def _tri_ones(n, upper):
    r = lax.broadcasted_iota(jnp.int32, (n, n), 0)
    c = lax.broadcasted_iota(jnp.int32, (n, n), 1)
    return jnp.where((c >= r) if upper else (c <= r), 1.0, 0.0).astype(F32)


def _fgate_fwd(fl, bf, *, name):
    S, W = fl.shape
    tb = _tile(S, 256, 8)

    def body(fl_ref, b_ref, o_ref, carry_ref):
        @pl.when(pl.program_id(0) == 0)
        def _():
            carry_ref[...] = jnp.zeros_like(carry_ref)

        lf = _log_sigmoid(fl_ref[...] + b_ref[...])
        cs = jnp.dot(_tri_ones(tb, False), lf, precision=lax.Precision.HIGHEST, preferred_element_type=F32) + carry_ref[0:1, :]
        o_ref[...] = cs
        carry_ref[...] = jnp.broadcast_to(cs[tb - 1:tb, :], carry_ref.shape)

    blk = pl.BlockSpec((tb, W), lambda i: (i, 0))
    return pl.pallas_call(body, name=name, grid=(S // tb,), in_specs=[blk, pl.BlockSpec((1, W), lambda i: (0, 0))], out_specs=blk,
                          out_shape=jax.ShapeDtypeStruct((S, W), F32), scratch_shapes=[pltpu.VMEM((8, W), F32)],
                          compiler_params=_cp("arbitrary"))(fl, bf)


def _fgate_bwd(dF, fl, bf, *, name):
    S, W = fl.shape
    tb = _tile(S, 256, 8)
    nb = S // tb

    def body(d_ref, fl_ref, b_ref, o_ref, s_ref, carry_ref):
        @pl.when(pl.program_id(0) == 0)
        def _():
            carry_ref[...] = jnp.zeros_like(carry_ref)
            s_ref[...] = jnp.zeros_like(s_ref)

        rc = jnp.dot(_tri_ones(tb, True), d_ref[...], precision=lax.Precision.HIGHEST, preferred_element_type=F32) + carry_ref[0:1, :]
        carry_ref[...] = jnp.broadcast_to(rc[0:1, :], carry_ref.shape)
        dfl = rc * jax.nn.sigmoid(-(fl_ref[...] + b_ref[...]))
        o_ref[...] = dfl
        s_ref[...] += jnp.sum(dfl, axis=0, keepdims=True)

    blk = pl.BlockSpec((tb, W), lambda i: (nb - 1 - i, 0))
    vec = pl.BlockSpec((1, W), lambda i: (0, 0))
    return pl.pallas_call(body, name=name, grid=(nb,), in_specs=[blk, blk, vec], out_specs=[blk, vec],
                          out_shape=[jax.ShapeDtypeStruct((S, W), F32), jax.ShapeDtypeStruct((1, W), F32)],
                          scratch_shapes=[pltpu.VMEM((8, W), F32)], compiler_params=_cp("arbitrary"))(dF, fl, bf)


NT_DIMS = (((1,), (1,)), ((), ()))
TN_DIMS = (((0,), (0,)), ((), ()))
HP = HEADS // 2
Q_SCALE = HEAD_DIM ** -0.5


def _head_mask(x, hh):
    lanes = lax.broadcasted_iota(jnp.int32, x.shape, 1)
    return jnp.where((lanes >= hh * HEAD_DIM) & (lanes < (hh + 1) * HEAD_DIM), x, jnp.zeros_like(x))


def _lanes_from(cols, shape):
    lanes = lax.broadcasted_iota(jnp.int32, shape, 1)
    out = jnp.zeros(shape, F32)
    for i, cvec in enumerate(cols):
        out = jnp.where(lanes == i, cvec, out)
    return out


AUG_ROWS = 16


def _qa_pair(qt, aug):
    fill = jnp.zeros((HEAD_DIM - AUG_ROWS, qt.shape[1]), qt.dtype)
    return [jnp.concatenate([qt[0:HEAD_DIM], aug[0], fill], axis=0), jnp.concatenate([aug[1], fill, qt[HEAD_DIM:]], axis=0)]


def _rows_of_head(xt, hh):
    rows = lax.broadcasted_iota(jnp.int32, xt.shape, 0)
    return jnp.where((rows >= hh * HEAD_DIM) & (rows < (hh + 1) * HEAD_DIM), xt, jnp.zeros_like(xt))


def _put_rows(ref, rows):
    for i, r in enumerate(rows):
        ref[i:i + 1, :] = r
    ref[len(rows):, :] = jnp.zeros((ref.shape[0] - len(rows), ref.shape[1]), ref.dtype)


def _diag_mask_t(s):
    rows = lax.broadcasted_iota(jnp.int32, s.shape, 0)
    cols = lax.broadcasted_iota(jnp.int32, s.shape, 1)
    return jnp.where(rows <= cols, s, NEG)


def _foxt_fwd(qkvT, ka, vtb, augq, *, tb, name, comm=()):
    S = qkvT.shape[1]
    nb = S // tb

    def body(q_ref, ka_ref, v_ref, aq_ref, o_ref, lse_ref, m0, m1, l0, l1, acc0, acc1):
        qi = pl.program_id(1)
        qa = _qa_pair(q_ref[...], aq_ref[...])
        state = ((m0, l0, acc0), (m1, l1, acc1))
        for m_sc, l_sc, acc_sc in state:
            m_sc[...] = jnp.full_like(m_sc, NEG)
            l_sc[...] = jnp.zeros_like(l_sc)
            acc_sc[...] = jnp.zeros_like(acc_sc)

        def block(kj, masked):
            k0 = pl.multiple_of(kj * tb, tb)
            logits = [jnp.dot(ka_ref[hh, pl.ds(k0, tb), :], qa[hh], preferred_element_type=F32) for hh in range(2)]
            updates = []
            for hh, (m_sc, l_sc, acc_sc) in enumerate(state):
                s = _diag_mask_t(logits[hh]) if masked else logits[hh]
                m_old = m_sc[...]
                m_new = jnp.maximum(m_old, jnp.max(s, axis=0, keepdims=True))
                alpha = jnp.exp(m_old - m_new)
                p = jnp.exp(s - m_new)
                l_sc[...] = alpha * l_sc[...] + jnp.sum(p, axis=0, keepdims=True)
                m_sc[...] = m_new
                vt = v_ref[kj, HEAD_DIM * hh:HEAD_DIM * (hh + 1), :]
                updates.append((alpha, jnp.dot(vt, p.astype(BF16), preferred_element_type=F32)))
            for (alpha, pv), (_, _, acc_sc) in zip(updates, state):
                acc_sc[...] = alpha * acc_sc[...] + pv

        def off_diagonal(kj, c):
            block(kj, False)
            return c

        lax.fori_loop(0, qi, off_diagonal, 0)
        block(qi, True)
        for hh, (m_sc, l_sc, acc_sc) in enumerate(state):
            o_ref[HEAD_DIM * hh:HEAD_DIM * (hh + 1), :] = (acc_sc[...] / l_sc[...]).astype(o_ref.dtype)
        _put_rows(lse_ref, [m_sc[...] + jnp.log(l_sc[...]) for m_sc, l_sc, _ in state])

    return _call(
        body, name=name, grid=(HP, nb), comm=comm, sem=("parallel", "parallel"), args=(qkvT, ka, vtb, augq),
        in_specs=[pl.BlockSpec((128, tb), lambda hp, qi: (hp, qi)),
                  pl.BlockSpec((None, 2, S, 128), lambda hp, qi: (hp, 0, 0, 0)),
                  pl.BlockSpec((None, nb, 128, tb), lambda hp, qi: (hp, 0, 0, 0)),
                  pl.BlockSpec((None, 2, AUG_ROWS, tb), lambda hp, qi: (hp, 0, 0, qi))],
        out_specs=[pl.BlockSpec((128, tb), lambda hp, qi: (hp, qi)), pl.BlockSpec((None, 8, tb), lambda hp, qi: (hp, 0, qi))],
        out_shape=[jax.ShapeDtypeStruct((D_MODEL, S), BF16), jax.ShapeDtypeStruct((HP, 8, S), F32)],
        scratch_shapes=[pltpu.VMEM((1, tb), F32)] * 4 + [pltpu.VMEM((HEAD_DIM, tb), F32)] * 2)


def _foxt_bwd(qkvT, qkv, ka, qtb, dotb, do, augqb, lseb, otb, *, tb, name, comm=()):
    S = qkv.shape[0]
    nb = S // tb

    def body(ka_ref, v_ref, kt_ref, qt_ref, dot_ref, qn_ref, dn_ref, aq_ref, ls_ref, ot_ref,
             dq_ref, dk_ref, dv_ref, df_ref, dr_ref, dq_sc, dk_sc, dv_sc, ds_sc, dr_sc):
        kj = pl.program_id(1)

        @pl.when(kj == 0)
        def _():
            dq_sc[...] = jnp.zeros_like(dq_sc)
            dr_sc[...] = jnp.zeros_like(dr_sc)

        dk_sc[...] = jnp.zeros_like(dk_sc)
        dv_sc[...] = jnp.zeros_like(dv_sc)
        ds_sc[...] = jnp.zeros_like(ds_sc)
        v2 = v_ref[...]
        kth = [kt_ref[HEAD_DIM * hh:HEAD_DIM * (hh + 1), :] for hh in range(2)]

        def block(qi, masked):
            q0 = pl.multiple_of(qi * tb, tb)
            qa = _qa_pair(qt_ref[qi], aq_ref[qi])
            dot, ls = dot_ref[qi], ls_ref[qi]
            prod = ot_ref[qi].astype(F32) * dot.astype(F32)
            delta = [jnp.sum(prod[HEAD_DIM * hh:HEAD_DIM * (hh + 1)], axis=0, keepdims=True) for hh in range(2)]
            qn, dn = qn_ref[pl.ds(q0, tb), :], dn_ref[pl.ds(q0, tb), :]
            for hh in range(2):
                s = jnp.dot(ka_ref[hh], qa[hh], preferred_element_type=F32)
                if masked:
                    s = _diag_mask_t(s)
                p = jnp.exp(s - ls[hh:hh + 1, :])
                dp = jnp.dot(v2, _rows_of_head(dot, hh), preferred_element_type=F32)
                ds = p * (dp - delta[hh])
                dsb = ds.astype(BF16)
                dv_sc[...] += jnp.dot(p.astype(BF16), _head_mask(dn, hh), preferred_element_type=F32)
                dk_sc[...] += jnp.dot(dsb, _head_mask(qn, hh), preferred_element_type=F32)
                dq_sc[qi, HEAD_DIM * hh:HEAD_DIM * (hh + 1), :] += jnp.dot(kth[hh], dsb, preferred_element_type=F32)
                part = ds[:, 0:128]
                for j in range(1, tb // 128):
                    part = part + ds[:, 128 * j:128 * (j + 1)]
                ds_sc[hh] += part
                dr_sc[qi, hh:hh + 1, :] += jnp.sum(ds, axis=0, keepdims=True)

        def off_diagonal(i, c):
            block(kj + 1 + i, False)
            return c

        block(kj, True)
        lax.fori_loop(0, nb - 1 - kj, off_diagonal, 0)
        dk_ref[...] = dk_sc[...].astype(dk_ref.dtype)
        dv_ref[...] = dv_sc[...].astype(dv_ref.dtype)
        df_ref[...] = _lanes_from([-jnp.sum(ds_sc[hh], axis=1, keepdims=True) for hh in range(2)], (tb, 128))

        @pl.when(kj == nb - 1)
        def _():
            dq_ref[...] = (dq_sc[...] * Q_SCALE).astype(dq_ref.dtype)
            dr_ref[...] = dr_sc[...]

    resident = lambda rows: pl.BlockSpec((None, nb) + rows, lambda hp, kj: (hp,) + (0,) * (len(rows) + 1))
    whole = pl.BlockSpec((S, 128), lambda hp, kj: (0, hp))
    kblk = lambda off: pl.BlockSpec((tb, 128), lambda hp, kj: (kj, off + hp))
    return _call(
        body, name=name, grid=(HP, nb), comm=comm, sem=("parallel", "arbitrary"),
        args=(ka, qkv, qkvT, qtb, dotb, qkv, do, augqb, lseb, otb),
        in_specs=[pl.BlockSpec((None, 2, tb, 128), lambda hp, kj: (hp, 0, kj, 0)), kblk(2 * HP),
                  pl.BlockSpec((128, tb), lambda hp, kj: (HP + hp, kj)),
                  resident((128, tb)), resident((128, tb)), whole, whole, resident((2, AUG_ROWS, tb)), resident((8, tb)),
                  resident((128, tb))],
        out_specs=[resident((128, tb)), kblk(0), kblk(0), pl.BlockSpec((None, tb, 128), lambda hp, kj: (hp, kj, 0)),
                   resident((8, tb))],
        out_shape=[jax.ShapeDtypeStruct((HP, nb, 128, tb), BF16), jax.ShapeDtypeStruct((S, D_MODEL), BF16),
                   jax.ShapeDtypeStruct((S, D_MODEL), BF16), jax.ShapeDtypeStruct((HP, S, 128), F32),
                   jax.ShapeDtypeStruct((HP, nb, 8, tb), F32)],
        scratch_shapes=[pltpu.VMEM((nb, 128, tb), F32), pltpu.VMEM((tb, 128), F32), pltpu.VMEM((tb, 128), F32),
                        pltpu.VMEM((2, tb, 128), F32), pltpu.VMEM((nb, 8, tb), F32)])


def _split3(x):
    rnd = lambda v: lax.reduce_precision(v, exponent_bits=8, mantissa_bits=7)
    hi = rnd(x)
    mid = rnd(x - hi)
    lo = rnd(x - hi - mid)
    return hi.astype(BF16), mid.astype(BF16), lo.astype(BF16)


def _blocked(xt, nb, tb):
    lead = xt.shape[:-1]
    x = xt.reshape(lead + (nb, tb))
    return jnp.moveaxis(x, -2, 1)


def _aug_rows(bias):
    ones = jnp.ones(bias.shape, BF16)
    zeros = jnp.zeros(bias.shape, BF16)
    return jnp.stack(list(_split3(bias)) + [ones] * 3 + [zeros] * (AUG_ROWS - 6), axis=2)


def _fox_keys(qkv, fcol, *, name):
    S = qkv.shape[0]
    tr = _tile(S, 2048, 16)

    def body(k_ref, f_ref, o_ref):
        hp = pl.program_id(1)
        k2 = k_ref[...]
        lanes = lax.broadcasted_iota(jnp.int32, (tr, 128), 1)
        for hh in range(2):
            nf = -jnp.sum(jnp.where(lanes == 2 * hp + hh, f_ref[...], 0.0), axis=1, keepdims=True)
            hi = nf.astype(BF16).astype(F32)
            mid = (nf - hi).astype(BF16).astype(F32)
            lo = nf - hi - mid
            base = HEAD_DIM * (1 - hh)
            aug = jnp.where((lanes >= base) & (lanes < base + 3), 1.0, 0.0)
            for j, piece in enumerate((hi, mid, lo)):
                aug = jnp.where(lanes == base + 3 + j, piece, aug)
            mine = (lanes >= HEAD_DIM * hh) & (lanes < HEAD_DIM * (hh + 1))
            o_ref[hh] = jnp.where(mine, k2, aug.astype(BF16))

    return pl.pallas_call(
        body, name=name, grid=(S // tr, HP),
        in_specs=[pl.BlockSpec((tr, 128), lambda i, hp: (i, HP + hp)), pl.BlockSpec((tr, 128), lambda i, hp: (i, 0))],
        out_specs=pl.BlockSpec((None, 2, tr, 128), lambda i, hp: (hp, 0, i, 0)),
        out_shape=jax.ShapeDtypeStruct((HP, 2, S, 128), BF16), compiler_params=_cp("parallel", "parallel"))(qkv, fcol)


HALF = S5_SUB * S5_P
NCB = 2 * HALF // 128
RE, IM = slice(0, NCB // 2), slice(NCB // 2, NCB)


def _gelu(x):
    return 0.5 * x * (1.0 + jnp.tanh(GELU_K * (x + 0.044715 * x * x * x)))


def _gelu_grad(x):
    t = jnp.tanh(GELU_K * (x + 0.044715 * x * x * x))
    return 0.5 * (1.0 + t) + 0.5 * x * (1.0 - t * t) * GELU_K * (1.0 + 3.0 * 0.044715 * x * x)


def _s5_put(buf, s, val, lt):
    for cb in range(NCB):
        buf[cb, pl.ds(s, lt, stride=S5_SUB), :] = val[:, 128 * cb:128 * (cb + 1)]


def _s5_get(buf, s, lt):
    return jnp.concatenate([buf[cb, pl.ds(s, lt, stride=S5_SUB), :] for cb in range(NCB)], axis=1)


def _s5_project_in(u_ref, wb_ref, buf, lt):
    for s in range(S5_SUB):
        _s5_put(buf, s, jnp.dot(u_ref[:, 128 * s:128 * (s + 1)], wb_ref[s], preferred_element_type=F32), lt)


def _s5_scan(buf, lam_ref, h0, lt):
    a_re, a_im = lam_ref[RE], lam_ref[IM]

    def step(t, carry):
        hr, hi = carry
        r0 = pl.multiple_of(t * S5_SUB, S5_SUB)
        nr = a_re * hr - a_im * hi + buf[RE, pl.ds(r0, S5_SUB), :]
        ni = a_re * hi + a_im * hr + buf[IM, pl.ds(r0, S5_SUB), :]
        buf[RE, pl.ds(r0, S5_SUB), :] = nr
        buf[IM, pl.ds(r0, S5_SUB), :] = ni
        return nr, ni

    return lax.fori_loop(0, lt, step, (h0[RE], h0[IM]), unroll=4)


def _s5_fwd(u, wb, wc, lam, dskip, *, lt, name, comm=()):
    S, D = u.shape
    nt = S // lt

    def body(u_ref, wb_ref, wc_ref, lam_ref, d_ref, yp_ref, yg_ref, st_ref, buf, h_sc):
        @pl.when(pl.program_id(0) == 0)
        def _():
            h_sc[...] = jnp.zeros_like(h_sc)

        st_ref[...] = h_sc[...]
        _s5_project_in(u_ref, wb_ref, buf, lt)
        hr, hi = _s5_scan(buf, lam_ref, h_sc[...], lt)
        h_sc[RE] = hr
        h_sc[IM] = hi
        for s in range(S5_SUB):
            cols = slice(128 * s, 128 * (s + 1))
            hs = _s5_get(buf, s, lt).astype(BF16)
            yp = jnp.dot(hs, wc_ref[s], preferred_element_type=F32) + d_ref[:, cols] * u_ref[:, cols].astype(F32)
            yp_ref[:, cols] = yp.astype(yp_ref.dtype)
            yg_ref[:, cols] = _gelu(yp).astype(yg_ref.dtype)

    blk = pl.BlockSpec((lt, D), lambda i: (i, 0))
    full = lambda shp: pl.BlockSpec(shp, lambda i: (0,) * len(shp))
    state = (NCB, S5_SUB, 128)
    return _call(
        body, name=name, grid=(nt,), comm=comm, sem=("arbitrary",), args=(u, wb, wc, lam, dskip),
        in_specs=[blk, full(wb.shape), full(wc.shape), full(lam.shape), full(dskip.shape)],
        out_specs=[blk, blk, pl.BlockSpec((None,) + state, lambda i: (i, 0, 0, 0))],
        out_shape=[jax.ShapeDtypeStruct((S, D), BF16), jax.ShapeDtypeStruct((S, D), BF16), jax.ShapeDtypeStruct((nt,) + state, F32)],
        scratch_shapes=[pltpu.VMEM((NCB, lt * S5_SUB, 128), F32), pltpu.VMEM(state, F32)])


def _s5_bwd(u, dyp, st, wb, wc, lam, dskip, *, lt, name, comm=()):
    S, D = u.shape
    nt = S // lt

    def body(u_ref, dy_ref, st_ref, wb_ref, wc_ref, lam_ref, d_ref, du_ref, dwb_ref, dwc_ref, dlam_ref, hbuf, gbuf, g_sc):
        @pl.when(pl.program_id(0) == 0)
        def _():
            g_sc[...] = jnp.zeros_like(g_sc)
            dwb_ref[...] = jnp.zeros_like(dwb_ref)
            dwc_ref[...] = jnp.zeros_like(dwc_ref)
            dlam_ref[...] = jnp.zeros_like(dlam_ref)

        _s5_project_in(u_ref, wb_ref, hbuf, lt)
        _s5_scan(hbuf, lam_ref, st_ref[...], lt)
        for s in range(S5_SUB):
            dys = dy_ref[:, 128 * s:128 * (s + 1)]
            _s5_put(gbuf, s, lax.dot_general(dys, wc_ref[s], NT_DIMS, preferred_element_type=F32), lt)
        a_re, a_im = lam_ref[RE], lam_ref[IM]

        def one(t, carry, hp_re, hp_im):
            gr, gi, dar, dai = carry
            r0 = pl.multiple_of(t * S5_SUB, S5_SUB)
            nr = gbuf[RE, pl.ds(r0, S5_SUB), :] + a_re * gr + a_im * gi
            ni = gbuf[IM, pl.ds(r0, S5_SUB), :] + a_re * gi - a_im * gr
            gbuf[RE, pl.ds(r0, S5_SUB), :] = nr
            gbuf[IM, pl.ds(r0, S5_SUB), :] = ni
            return nr, ni, dar + nr * hp_re + ni * hp_im, dai + ni * hp_re - nr * hp_im

        def step(k, carry):
            t = lt - 1 - k
            p0 = pl.multiple_of((t - 1) * S5_SUB, S5_SUB)
            return one(t, carry, hbuf[RE, pl.ds(p0, S5_SUB), :], hbuf[IM, pl.ds(p0, S5_SUB), :])

        init = (g_sc[RE], g_sc[IM], dlam_ref[RE], dlam_ref[IM])
        carry = lax.fori_loop(0, lt - 1, step, init, unroll=2)
        gr, gi, dar, dai = one(0, carry, st_ref[RE], st_ref[IM])
        g_sc[RE] = gr
        g_sc[IM] = gi
        dlam_ref[RE] = dar
        dlam_ref[IM] = dai
        for s in range(S5_SUB):
            cols = slice(128 * s, 128 * (s + 1))
            gs = _s5_get(gbuf, s, lt).astype(BF16)
            hs = _s5_get(hbuf, s, lt).astype(BF16)
            us, dys = u_ref[:, cols], dy_ref[:, cols]
            du = lax.dot_general(gs, wb_ref[s], NT_DIMS, preferred_element_type=F32) + d_ref[:, cols] * dys.astype(F32)
            du_ref[:, cols] = du.astype(du_ref.dtype)
            dwb_ref[s] += lax.dot_general(us, gs, TN_DIMS, preferred_element_type=F32)
            dwc_ref[s] += lax.dot_general(hs, dys, TN_DIMS, preferred_element_type=F32)

    blk = pl.BlockSpec((lt, D), lambda i: (nt - 1 - i, 0))
    full = lambda shp: pl.BlockSpec(shp, lambda i: (0,) * len(shp))
    state = (NCB, S5_SUB, 128)
    words = pltpu.VMEM((NCB, lt * S5_SUB, 128), F32)
    return _call(
        body, name=name, grid=(nt,), comm=comm, sem=("arbitrary",), args=(u, dyp, st, wb, wc, lam, dskip),
        in_specs=[blk, blk, pl.BlockSpec((None,) + state, lambda i: (nt - 1 - i, 0, 0, 0)),
                  full(wb.shape), full(wc.shape), full(lam.shape), full(dskip.shape)],
        out_specs=[blk, full(wb.shape), full(wc.shape), full(lam.shape)],
        out_shape=[jax.ShapeDtypeStruct((S, D), BF16), jax.ShapeDtypeStruct(wb.shape, F32), jax.ShapeDtypeStruct(wc.shape, F32),
                   jax.ShapeDtypeStruct(lam.shape, F32)],
        scratch_shapes=[words, words, pltpu.VMEM(state, F32)])


def _s5_disc(lam_re, lam_im, log_dt, b_re, b_im):
    dt = jnp.exp(log_dt)[:, None]
    mag = jnp.exp(lam_re * dt)
    lb_re, lb_im = mag * jnp.cos(lam_im * dt), mag * jnp.sin(lam_im * dt)
    num_re = lb_re - 1.0
    den = lam_re * lam_re + lam_im * lam_im
    k_re = (num_re * lam_re + lb_im * lam_im) / den
    k_im = (lb_im * lam_re - num_re * lam_im) / den
    bb_re = k_re[..., None] * b_re - k_im[..., None] * b_im
    bb_im = k_re[..., None] * b_im + k_im[..., None] * b_re
    return lb_re, lb_im, bb_re, bb_im


def _s5_pack(lb_re, lb_im, bb_re, bb_im, c_re, c_im):
    eye = jnp.eye(S5_SUB, dtype=F32)
    tb = lambda x: jnp.einsum("sgcp,gh->sgchp", x.reshape(S5_SUB, S5_SUB, S5_P, S5_C).transpose(0, 1, 3, 2), eye).reshape(S5_SUB, 128, HALF)
    tc = lambda x: jnp.einsum("sgpc,gh->sgphc", x.reshape(S5_SUB, S5_SUB, S5_C, S5_P).transpose(0, 1, 3, 2), eye).reshape(S5_SUB, HALF, 128)
    wb = jnp.concatenate([tb(bb_re), tb(bb_im)], axis=2).astype(BF16)
    wc = jnp.concatenate([tc(c_re), -tc(c_im)], axis=1).astype(BF16)
    lam = jnp.concatenate([lb_re.reshape(S5_SUB, HALF), lb_im.reshape(S5_SUB, HALF)], axis=1)
    return wb, wc, lam.reshape(S5_SUB, NCB, 128).transpose(1, 0, 2)


def _s5_unpack(dwb, dwc, dlam):
    db = jnp.einsum("sgcrgp->rsgpc", dwb.reshape(S5_SUB, S5_SUB, S5_C, 2, S5_SUB, S5_P)).reshape(2, S5_G, S5_P, S5_C)
    dc = jnp.einsum("srgpgc->rsgcp", dwc.reshape(S5_SUB, 2, S5_SUB, S5_P, S5_SUB, S5_C)).reshape(2, S5_G, S5_C, S5_P)
    dlam = dlam.transpose(1, 0, 2).reshape(S5_SUB, 2 * HALF)
    return (dlam[:, :HALF].reshape(S5_G, S5_P), dlam[:, HALF:].reshape(S5_G, S5_P), db[0], db[1], dc[0], -dc[1])


def _glu_fwd(yg, gl, *, name):
    S, D = yg.shape
    tm = _tile(S, 512, 8)
    blk = pl.BlockSpec((tm, D), lambda i: (i, 0))

    def body(y_ref, g_ref, o_ref):
        o_ref[...] = (y_ref[...].astype(F32) * jax.nn.sigmoid(g_ref[...].astype(F32))).astype(o_ref.dtype)

    return pl.pallas_call(body, name=name, grid=(S // tm,), in_specs=[blk, blk], out_specs=blk,
                          out_shape=jax.ShapeDtypeStruct((S, D), BF16), compiler_params=_cp("parallel"))(yg, gl)


def _glu_bwd(dy2, yg, gl, *, name):
    S, D = yg.shape
    tm = _tile(S, 512, 8)
    blk = pl.BlockSpec((tm, D), lambda i: (i, 0))

    def body(d_ref, y_ref, g_ref, da_ref, dg_ref):
        d = d_ref[...].astype(F32)
        sg = jax.nn.sigmoid(g_ref[...].astype(F32))
        da_ref[...] = d * sg
        dg_ref[...] = (d * y_ref[...].astype(F32) * sg * (1.0 - sg)).astype(dg_ref.dtype)

    return pl.pallas_call(body, name=name, grid=(S // tm,), in_specs=[blk, blk, blk], out_specs=[blk, blk],
                          out_shape=[jax.ShapeDtypeStruct((S, D), F32), jax.ShapeDtypeStruct((S, D), BF16)],
                          compiler_params=_cp("parallel"))(dy2, yg, gl)


def _gelu_bwd(da, db, ypre, u, *, name):
    S, D = ypre.shape
    tm = _tile(S, 512, 8)
    blk = pl.BlockSpec((tm, D), lambda i: (i, 0))
    vec = pl.BlockSpec((1, D), lambda i: (0, 0))

    def body(a_ref, b_ref, y_ref, u_ref, o_ref, s_ref):
        @pl.when(pl.program_id(0) == 0)
        def _():
            s_ref[...] = jnp.zeros_like(s_ref)

        dy = (a_ref[...] + b_ref[...]) * _gelu_grad(y_ref[...].astype(F32))
        o_ref[...] = dy.astype(o_ref.dtype)
        s_ref[...] += jnp.sum(dy * u_ref[...].astype(F32), axis=0, keepdims=True)

    return pl.pallas_call(body, name=name, grid=(S // tm,), in_specs=[blk, blk, blk, blk], out_specs=[blk, vec],
                          out_shape=[jax.ShapeDtypeStruct((S, D), BF16), jax.ShapeDtypeStruct((1, D), F32)],
                          compiler_params=_cp("arbitrary"))(da, db, ypre, u)


def _my_index():
    return 4 * lax.axis_index("x") + 2 * lax.axis_index("y") + lax.axis_index("c")


def _comm_plan(payloads):
    n = len(payloads)
    hbm = pl.BlockSpec(memory_space=pl.ANY)
    shapes = [jax.ShapeDtypeStruct((N_DEV,) + (x.shape if g else x.shape[1:]), x.dtype) for x, g in payloads]
    sems = [pltpu.SemaphoreType.DMA((n, N_DEV - 1)), pltpu.SemaphoreType.DMA((n, N_DEV - 1)), pltpu.SemaphoreType.DMA((n,))] if n else []
    return [hbm] * n, [hbm] * n, shapes, sems


def _comm_copies(x_refs, o_refs, gathers, send_sems, recv_sems, local_sems):
    mx, my, mc = lax.axis_index("x"), lax.axis_index("y"), lax.axis_index("c")
    me = 4 * mx + 2 * my + mc
    copies = []
    for i, (x_ref, o_ref, gather) in enumerate(zip(x_refs, o_refs, gathers)):
        src = (lambda j, r=x_ref: r) if gather else (lambda j, r=x_ref: r.at[j])
        copies.append(pltpu.make_async_copy(src(me), o_ref.at[me], local_sems.at[i]))
        for k in range(1, N_DEV):
            px, py, pc = mx ^ (k >> 2), my ^ ((k >> 1) & 1), mc ^ (k & 1)
            copies.append(pltpu.make_async_remote_copy(src(4 * px + 2 * py + pc), o_ref.at[me], send_sems.at[i, k - 1],
                                                       recv_sems.at[i, k - 1], device_id=(px, py, pc), device_id_type=MESH_ID))
    return copies


def _call(body, *, name, grid, in_specs, out_specs, out_shape, scratch_shapes, sem, args, comm=()):
    if not comm:
        outs = pl.pallas_call(body, name=name, grid=grid, in_specs=in_specs, out_specs=out_specs, out_shape=out_shape,
                              scratch_shapes=scratch_shapes, compiler_params=_cp(*sem))(*args)
        return outs, ()
    n, n_in, n_out = len(comm), len(in_specs), len(out_specs)
    c_in, c_out, c_shapes, c_sems = _comm_plan(comm)
    gathers = [g for _, g in comm]

    def wrapped(*refs):
        own_in, x_refs = refs[:n_in], refs[n_in:n_in + n]
        own_out, o_refs = refs[n_in + n:n_in + n + n_out], refs[n_in + n + n_out:n_in + 2 * n + n_out]
        own_scratch, sems = refs[n_in + 2 * n + n_out:-3], refs[-3:]
        ids = [pl.program_id(a) for a in range(len(grid))]
        first = functools.reduce(jnp.logical_and, [i == 0 for i in ids])
        last = functools.reduce(jnp.logical_and, [i == g - 1 for i, g in zip(ids, grid)])

        @pl.when(first)
        def _():
            for cp in _comm_copies(x_refs, o_refs, gathers, *sems):
                cp.start()

        body(*own_in, *own_out, *own_scratch)

        @pl.when(last)
        def _():
            for cp in _comm_copies(x_refs, o_refs, gathers, *sems):
                cp.wait()

    outs = pl.pallas_call(wrapped, name=name, grid=grid, in_specs=list(in_specs) + c_in, out_specs=list(out_specs) + c_out,
                          out_shape=list(out_shape) + c_shapes, scratch_shapes=list(scratch_shapes) + c_sems,
                          compiler_params=_cp(*["arbitrary"] * len(grid)))(*args, *[x for x, _ in comm])
    return outs[:n_out], outs[n_out:]


def _exchange_many(payloads, *, name):
    n = len(payloads)
    in_specs, out_specs, shapes, sems = _comm_plan(payloads)
    gathers = [g for _, g in payloads]

    def body(*refs):
        copies = _comm_copies(refs[:n], refs[n:2 * n], gathers, *refs[2 * n:])
        for cp in copies:
            cp.start()
        for cp in copies:
            cp.wait()

    return pl.pallas_call(body, name=name, in_specs=in_specs, out_specs=out_specs, out_shape=shapes, scratch_shapes=sems,
                          compiler_params=pltpu.CompilerParams(has_side_effects=True))(*[x for x, _ in payloads])


def _exchange(x, *, gather, name):
    return _exchange_many([(x, gather)], name=name)[0]


def _sum8(x8, *, name):
    _, R, C = x8.shape
    tr = _tile(R, 256, 16)

    def body(x_ref, o_ref):
        acc = x_ref[0].astype(F32)
        for j in range(1, N_DEV):
            acc = acc + x_ref[j].astype(F32)
        o_ref[...] = acc

    return pl.pallas_call(body, name=name, grid=(R // tr,), in_specs=[pl.BlockSpec((N_DEV, tr, C), lambda i: (0, i, 0))],
                          out_specs=pl.BlockSpec((tr, C), lambda i: (i, 0)), out_shape=jax.ShapeDtypeStruct((R, C), F32),
                          compiler_params=_cp("parallel"))(x8)


def _pack(arrs, dtype, row_mult):
    flat = jnp.concatenate([a.reshape(-1).astype(dtype) for a in arrs])
    n = flat.shape[0]
    rows = -(-n // 1024)
    rows = -(-rows // row_mult) * row_mult
    return jnp.pad(flat, (0, rows * 1024 - n)).reshape(rows, 1024)


def _unpack(buf, shapes):
    lead = buf.shape[:-2]
    flat = buf.reshape(lead + (-1,))
    out, off = [], 0
    for s in shapes:
        n = math.prod(s)
        out.append(flat[..., off:off + n].reshape(lead + tuple(s)))
        off += n
    return out


def _adaln_fwd(c_all, w, b, *, name):
    nk, D, n = w.shape

    def body(c_ref, w_ref, b_ref, o_ref):
        cv = c_ref[...]
        sc = (cv * jax.nn.sigmoid(cv)).astype(BF16)
        o_ref[...] = jnp.dot(sc, w_ref[...].astype(BF16), preferred_element_type=F32) + b_ref[...]

    return pl.pallas_call(body, name=name, grid=(nk,),
                          in_specs=[pl.BlockSpec((N_DEV, D), lambda k: (0, 0)), pl.BlockSpec((None, D, n), lambda k: (k, 0, 0)),
                                    pl.BlockSpec((None, 1, n), lambda k: (k, 0, 0))],
                          out_specs=pl.BlockSpec((None, N_DEV, n), lambda k: (k, 0, 0)),
                          out_shape=jax.ShapeDtypeStruct((nk, N_DEV, n), F32), compiler_params=_cp("parallel"))(c_all, w, b)


def _adaln_bwd(c_all, dmod, *, name):
    nk, _, n = dmod.shape
    D = c_all.shape[1]

    def body(c_ref, d_ref, dw_ref, db_ref):
        cv = c_ref[...]
        sc = (cv * jax.nn.sigmoid(cv)).astype(BF16).astype(F32)
        dm = d_ref[...]
        dw_ref[...] = lax.dot_general(sc, dm.astype(BF16).astype(F32), TN_DIMS, precision=lax.Precision.HIGHEST,
                                      preferred_element_type=F32)
        db_ref[...] = jnp.sum(dm, axis=0, keepdims=True)

    return pl.pallas_call(body, name=name, grid=(nk,),
                          in_specs=[pl.BlockSpec((N_DEV, D), lambda k: (0, 0)), pl.BlockSpec((None, N_DEV, n), lambda k: (k, 0, 0))],
                          out_specs=[pl.BlockSpec((None, D, n), lambda k: (k, 0, 0)), pl.BlockSpec((None, 1, n), lambda k: (k, 0, 0))],
                          out_shape=[jax.ShapeDtypeStruct((nk, D, n), F32), jax.ShapeDtypeStruct((nk, 1, n), F32)],
                          compiler_params=_cp("parallel"))(c_all, dmod)


def _adamw(w, g, m, v, *, name):
    R, C = w.shape
    tr = _tile(R, 256, 8)
    blk = pl.BlockSpec((tr, C), lambda i: (i, 0))
    c1 = 1.0 / (1.0 - ADAM_B1 ** ADAM_STEP)
    c2 = 1.0 / (1.0 - ADAM_B2 ** ADAM_STEP)

    def body(w_ref, g_ref, m_ref, v_ref, d_ref, nm_ref, nv_ref):
        gv = g_ref[...]
        nm = ADAM_B1 * m_ref[...] + (1.0 - ADAM_B1) * gv
        nv = ADAM_B2 * v_ref[...] + (1.0 - ADAM_B2) * (gv * gv)
        nm_ref[...] = nm
        nv_ref[...] = nv
        d_ref[...] = -ADAM_LR * ((nm * c1) / (jnp.sqrt(nv * c2) + ADAM_EPS) + ADAM_WD * w_ref[...])

    sh = jax.ShapeDtypeStruct((R, C), F32)
    return pl.pallas_call(body, name=name, grid=(R // tr,), in_specs=[blk] * 4, out_specs=[blk] * 3, out_shape=[sh] * 3,
                          compiler_params=_cp("parallel"))(w, g, m, v)


def _adamw_nd(w, g, m, v, *, name):
    shp = w.shape
    two = (-1, shp[-1]) if w.ndim > 1 else (1, -1)
    outs = _adamw(w.reshape(two), g.reshape(two), m.reshape(two), v.reshape(two), name=name)
    return [o.reshape(shp) for o in outs]


S5_TILE = 256
ATTN_BLOCK = 1024
BIG = ("s5_w_in", "s5_w_glu", "s5_w_out", "fox_w_in", "fox_w_out", "ffn_w_up", "ffn_w_down")
WEIGHTS = ("norm_g", "ada_w", "ada_b", "s5_w_in", "s5_lam_re", "s5_lam_im", "s5_log_dt", "s5_b_re", "s5_b_im", "s5_c_re",
           "s5_c_im", "s5_d", "s5_w_glu", "s5_w_out", "fox_w_in", "fox_b_f", "fox_w_out", "ffn_w_up", "ffn_conv_w",
           "ffn_conv_b", "ffn_w_down", "final_g")
REPLICATED = ("s5_lam_re", "s5_lam_im", "s5_log_dt", "s5_b_re", "s5_b_im", "s5_c_re", "s5_c_im", "s5_d", "fox_b_f",
              "ffn_conv_b", "final_g")


def _row_group(arrs):
    return jnp.concatenate([a.reshape(-1, D_MODEL) for a in arrs], axis=0)


def _cols_join(g):
    return jnp.concatenate([g[j] for j in range(N_DEV)], axis=1)


def _cols_split(full):
    n = full.shape[1] // N_DEV
    return jnp.stack([full[:, j * n:(j + 1) * n] for j in range(N_DEV)])


def kernel(x, c, norm_g, ada_w, ada_b, s5_w_in, s5_lam_re, s5_lam_im, s5_log_dt, s5_b_re, s5_b_im, s5_c_re, s5_c_im, s5_d, s5_w_glu, s5_w_out, fox_w_in, fox_b_f, fox_w_out, ffn_w_up, ffn_conv_w, ffn_conv_b, ffn_w_down, final_g, loss_target, m_norm_g, m_ada_w, m_ada_b, m_s5_w_in, m_s5_lam_re, m_s5_lam_im, m_s5_log_dt, m_s5_b_re, m_s5_b_im, m_s5_c_re, m_s5_c_im, m_s5_d, m_s5_w_glu, m_s5_w_out, m_fox_w_in, m_fox_b_f, m_fox_w_out, m_ffn_w_up, m_ffn_conv_w, m_ffn_conv_b, m_ffn_w_down, m_final_g, v_norm_g, v_ada_w, v_ada_b, v_s5_w_in, v_s5_lam_re, v_s5_lam_im, v_s5_log_dt, v_s5_b_re, v_s5_b_im, v_s5_c_re, v_s5_c_im, v_s5_d, v_s5_w_glu, v_s5_w_out, v_fox_w_in, v_fox_b_f, v_fox_w_out, v_ffn_w_up, v_ffn_conv_w, v_ffn_conv_b, v_ffn_w_down, v_final_g):
    args = dict(locals())
    W = {n: args[n] for n in WEIGHTS}
    M = {n: args["m_" + n] for n in WEIGHTS}
    V = {n: args["v_" + n] for n in WEIGHTS}
    D, F = D_MODEL, D_FF
    me = _my_index()
    h0 = x[0]
    S = h0.shape[0]
    lt = min(S5_TILE, S)
    tb = min(ATTN_BLOCK, S)

    g0 = _exchange(_pack([c, norm_g, ffn_conv_w], F32, 8), gather=True, name="gather_small_in")
    c_all, ng_all, cw_all = _unpack(g0, [(D,), norm_g.shape, ffn_conv_w.shape])
    ng_full = ng_all.transpose(1, 2, 0, 3).reshape(2, 2, D)
    cw_full = cw_all.transpose(1, 2, 0, 3).reshape(2, 3, F)

    ncol = ada_w.shape[-1]
    modp = _adaln_fwd(c_all, ada_w.reshape(4, D, ncol), ada_b.reshape(4, 1, ncol), name="adaln_fwd")
    g1 = _exchange(_pack([modp], F32, 8), gather=True, name="gather_adaln")
    (mod_all,) = _unpack(g1, [modp.shape])
    mod = lax.dynamic_index_in_dim(mod_all, me, axis=2, keepdims=False).transpose(1, 0, 2).reshape(4, 3 * D)
    shift = [mod[k:k + 1, :D] for k in range(4)]
    scale = [mod[k:k + 1, D:2 * D] for k in range(4)]
    gate = [mod[k:k + 1, 2 * D:] for k in range(4)]
    gain = [_row(ng_full[k // 2, k % 2]) for k in range(4)]

    b16 = lambda a: a.astype(BF16)
    rows = D // N_DEV
    full = {"s5_w_in": _exchange(b16(s5_w_in[0]), gather=True, name="gather_s5_w_in").reshape(D, D)}
    bf_pad = jnp.pad(fox_b_f, ((0, 0), (0, 128 - HEADS)))

    def ffn_fwd(h, k, layer, comm=()):
        hn = _modulate(h, gain[k], shift[k], scale[k], name=f"modulate{k}")
        up = _mm(hn, full[f"ffn_w_up{layer}"], name=f"ffn_up{layer}", o_h=True, out_dtype=BF16, tn=1408, comm=comm)
        up, exchanged = up if comm else (up, ())
        z = _conv_gate_fwd(up, cw_full[layer], ffn_conv_b[layer:layer + 1], name=f"conv_gate{layer}")
        m, h_out = _mm(z, full[f"ffn_w_down{layer}"], name=f"ffn_down{layer}", res=h, gate=gate[k])
        return h_out, (hn, up, z, m), exchanged

    lb_re, lb_im, bb_re, bb_im = _s5_disc(s5_lam_re[0], s5_lam_im[0], s5_log_dt[0], s5_b_re[0], s5_b_im[0])
    wb, wc, lam = _s5_pack(lb_re, lb_im, bb_re, bb_im, s5_c_re[0], s5_c_im[0])
    hn0 = _modulate(h0, gain[0], shift[0], scale[0], name="modulate0")
    u = _mm(hn0, full["s5_w_in"], name="s5_in", out_dtype=BF16)
    (ypre, yg, st), (g_up0, g_rows0) = _s5_fwd(
        u, wb, wc, lam, s5_d, lt=lt, name="s5_scan",
        comm=[(b16(ffn_w_up[0]), True), (_row_group([b16(s5_w_glu), b16(s5_w_out), b16(ffn_w_down[0])]), True)])
    full["ffn_w_up0"] = _cols_join(g_up0)
    full["s5_w_glu"] = g_rows0[:, :rows].reshape(D, D)
    full["s5_w_out"] = g_rows0[:, rows:2 * rows].reshape(D, D)
    full["ffn_w_down0"] = g_rows0[:, 2 * rows:].reshape(F, D)
    gl = _mm(yg, full["s5_w_glu"], name="s5_glu", out_dtype=BF16)
    y2 = _glu_fwd(yg, gl, name="s5_glu_gate")
    m0, h1 = _mm(y2, full["s5_w_out"], name="s5_out", res=h0, gate=gate[0])
    h2, (hn1, up0, z0, m1), (g_fox, g_fox_out) = ffn_fwd(h1, 1, 0, comm=[(b16(fox_w_in[0]), True), (b16(fox_w_out[0]), True)])
    full["fox_w_in"] = _cols_join(g_fox)
    full["fox_w_out"] = g_fox_out.reshape(D, D)
    w_proj = jnp.pad(full["fox_w_in"], ((0, 0), (0, 3 * D + 128 - full["fox_w_in"].shape[1])))
    w_qkv, w_f = w_proj[:, :3 * D], w_proj[:, 3 * D:]

    hn2 = _modulate(h2, gain[2], shift[2], scale[2], name="modulate2")
    nb = S // tb
    w_qkv_s = jnp.concatenate([w_qkv[:, :D] * Q_SCALE, w_qkv[:, D:]], axis=1)
    qkv, qkvT = _mm(hn2, w_qkv_s, name="fox_qkv", out_dtype=BF16, with_t=True)
    fl = _mm(hn2, w_f, name="fox_f", out_dtype=F32)
    fcol = _fgate_fwd(fl, bf_pad, name="fox_fgate")
    f_heads = fcol[:, :HEADS].T.reshape(HP, 2, S)
    ka = _fox_keys(qkv, fcol, name="fox_keys")
    vtb = _blocked(qkvT[2 * D:].reshape(HP, 128, S), nb, tb)
    aug = _aug_rows(f_heads)
    (oT, lse), (g_up1, g_down1) = _foxt_fwd(qkvT, ka, vtb, aug, tb=tb, name="fox_attn",
                                            comm=[(b16(ffn_w_up[1]), True), (b16(ffn_w_down[1]), True)])
    full["ffn_w_up1"] = _cols_join(g_up1)
    full["ffn_w_down1"] = g_down1.reshape(F, D)
    m2, h3 = _mm(oT, full["fox_w_out"], name="fox_out", ta=True, res=h2, gate=gate[2])
    h4, (hn3, up1, z1, m3), _ = ffn_fwd(h3, 3, 1)

    lblk, dh, head_sums, dm = _loss_head(h4, _row(final_g), loss_target[0], m3, gate[3], name="loss_head")
    d_final_g, dgate = head_sums[0:1], head_sums[1:2]
    loss = lax.psum(lblk[0, 0], ("x", "y", "c"))

    G = {}
    dmod = [None] * 4
    mixer_out = [m0, m1, m2, m3]

    def norm_bwd(h, dhn, dh_in, k, dgate_k):
        if k == 0:
            dh_out, sums = _modulate_bwd(h, dhn, dh_in, gain[k], scale[k], name=f"modulate_bwd{k}")
            dm_below = dgate_below = None
        else:
            dh_out, sums, dm_below = _modulate_bwd(h, dhn, dh_in, gain[k], scale[k], mixer_out[k - 1], gate[k - 1],
                                                   name=f"modulate_bwd{k}")
            dgate_below = sums[3:4]
        dmod[k] = jnp.concatenate([sums[0:1], sums[1:2], dgate_k], axis=1)
        return dh_out, sums[2], dm_below, dgate_below

    def ffn_bwd(dh_in, dm, dgate_k, h, k, layer, saved):
        hn, up, z, m = saved
        dz = _mm(dm, full[f"ffn_w_down{layer}"], name=f"ffn_down_dx{layer}", tb=True, out_dtype=BF16, tn=1408)
        dw_down = _mm(z, dm, name=f"ffn_down_dw{layer}", ta=True, tm=1408, tk=2048)
        d_up, cs = _conv_gate_bwd(up, dz, cw_full[layer], ffn_conv_b[layer:layer + 1], name=f"conv_gate_bwd{layer}")
        dhn = _mm(d_up, full[f"ffn_w_up{layer}"], name=f"ffn_up_dx{layer}", tb=True, a_h=True, tk=2816)
        dw_up = _mm(hn, d_up, name=f"ffn_up_dw{layer}", ta=True, b_h=True, tm=1024, tn=1408, tk=2048)
        dh_out, dg, dm_below, dgate_below = norm_bwd(h, dhn, dh_in, k, dgate_k)
        return dh_out, dm_below, dgate_below, dg, dw_up, dw_down, cs[0:3], cs[3]

    dh, dm, dgate, dg3, dw_up1, dw_down1, dcw1, dcb1 = ffn_bwd(dh, dm, dgate, h3, 3, 1, (hn3, up1, z1, m3))

    do, doT = _mm(dm, full["fox_w_out"], name="fox_out_dx", tb=True, out_dtype=BF16, with_t=True)
    dw_fox_out = _mm(oT, dm, name="fox_out_dw", tk=2048)
    to_rows = lambda g: b16(g).reshape(N_DEV, -1, D)
    (dqtb, dk, dv, dfk, dfq), (r_up1, r_down1) = _foxt_bwd(
        qkvT, qkv, ka, _blocked(qkvT[:D].reshape(HP, 128, S), nb, tb), _blocked(doT.reshape(HP, 128, S), nb, tb), do,
        _blocked(aug, nb, tb), _blocked(lse, nb, tb), _blocked(oT.reshape(HP, 128, S), nb, tb), tb=tb, name="fox_attn_bwd",
        comm=[(_cols_split(b16(dw_up1)), False), (to_rows(dw_down1), False)])
    dq = dqtb.transpose(1, 3, 0, 2).reshape(S, D)
    dF = dfk[:, :, :2].transpose(1, 0, 2).reshape(S, HEADS) + dfq[:, :, :2, :].transpose(1, 3, 0, 2).reshape(S, HEADS)
    dF = jnp.pad(dF, ((0, 0), (0, 128 - HEADS)))
    dfl, dbf = _fgate_bwd(dF, fl, bf_pad, name="fox_fgate_bwd")
    dproj = jnp.concatenate([dq, dk, dv, dfl.astype(BF16)], axis=1)
    dhn = _mm(dproj, w_proj, name="fox_in_dx", tb=True)
    dw_proj = _mm(hn2, dproj, name="fox_in_dw", ta=True, tn=640, tk=2048)
    dw_fox_in = dw_proj[:, :full["fox_w_in"].shape[1]]
    dh, dg2, dm, dgate = norm_bwd(h2, dhn, dh, 2, dgate)

    dh, dm, dgate, dg1, dw_up0, dw_down0, dcw0, dcb0 = ffn_bwd(dh, dm, dgate, h1, 1, 0, (hn1, up0, z0, m1))

    dy2 = _mm(dm, full["s5_w_out"], name="s5_out_dx", tb=True, out_dtype=BF16)
    G["s5_w_out"] = _mm(y2, dm, name="s5_out_dw", ta=True, tk=2048)
    da, dgl = _glu_bwd(dy2, yg, gl, name="s5_glu_bwd")
    dyg_b = _mm(dgl, full["s5_w_glu"], name="s5_glu_dx", tb=True)
    G["s5_w_glu"] = _mm(yg, dgl, name="s5_glu_dw", ta=True, tk=2048)
    dyp, dd = _gelu_bwd(da, dyg_b, ypre, u, name="s5_gelu_bwd")
    row_grads = [dw_fox_out, dw_down0, G["s5_w_out"], G["s5_w_glu"]]
    (du, dwb, dwc, dlam), (r_up0, r_rows0, r_fox) = _s5_bwd(
        u, dyp, st, wb, wc, lam, s5_d, lt=lt, name="s5_scan_bwd",
        comm=[(_cols_split(b16(dw_up0)), False), (jnp.concatenate([to_rows(g) for g in row_grads], axis=1), False),
              (_cols_split(b16(dw_fox_in)), False)])
    dhn = _mm(du, full["s5_w_in"], name="s5_in_dx", tb=True)
    G["s5_w_in"] = _mm(hn0, du, name="s5_in_dw", ta=True, tk=2048)
    dh, dg0, _, _ = norm_bwd(h0, dhn, dh, 0, dgate)
    grad_x = dh[None]

    dlb_re, dlb_im, dbb_re, dbb_im, dc_re, dc_im = _s5_unpack(dwb, dwc, dlam)
    _, disc_vjp = jax.vjp(_s5_disc, s5_lam_re[0], s5_lam_im[0], s5_log_dt[0], s5_b_re[0], s5_b_im[0])
    dlam_re, dlam_im, dlog_dt, db_re, db_im = disc_vjp((dlb_re, dlb_im, dbb_re, dbb_im))

    g2 = _exchange(_pack([jnp.concatenate(dmod, axis=0)], F32, 8), gather=True, name="gather_dmod")
    (dmod_all,) = _unpack(g2, [(4, 3 * D)])
    dmod_mine = _columns_of_mod(dmod_all, me, ncol)
    d_ada_w, d_ada_b = _adaln_bwd(c_all, dmod_mine, name="adaln_bwd")
    G["ada_w"] = d_ada_w.reshape(ada_w.shape)
    G["ada_b"] = d_ada_b.reshape(ada_b.shape)

    small_full = {
        "norm_g": jnp.stack([dg0, dg1, dg2, dg3]).reshape(2, 2, D),
        "ffn_conv_w": jnp.stack([dcw0, dcw1]),
        "s5_lam_re": dlam_re[None], "s5_lam_im": dlam_im[None], "s5_log_dt": dlog_dt[None],
        "s5_b_re": db_re[None], "s5_b_im": db_im[None], "s5_c_re": dc_re[None], "s5_c_im": dc_im[None],
        "s5_d": dd, "fox_b_f": dbf[:, :HEADS], "ffn_conv_b": jnp.stack([dcb0, dcb1]), "final_g": d_final_g[0],
    }
    names = tuple(small_full)
    small_pack = _pack([small_full[n] for n in names], F32, 8 * N_DEV)
    r_in, r_small = _exchange_many([(to_rows(G["s5_w_in"]), False), (small_pack.reshape(N_DEV, -1, 1024), False)],
                                   name="scatter_s5_in_and_small_grads")
    G["s5_w_in"] = _sum8(r_in, name="sum_s5_in_grads")[None]
    rows0_sum = _sum8(r_rows0, name="sum_rows0_grads")
    down_rows = F // N_DEV
    G["fox_w_out"] = rows0_sum[:rows][None]
    G["s5_w_out"] = rows0_sum[rows + down_rows:2 * rows + down_rows][None]
    G["s5_w_glu"] = rows0_sum[2 * rows + down_rows:][None]
    G["ffn_w_down"] = jnp.stack([rows0_sum[rows:rows + down_rows], _sum8(r_down1, name="sum_down1_grads")])
    G["ffn_w_up"] = jnp.stack([_sum8(r_up0, name="sum_up0_grads"), _sum8(r_up1, name="sum_up1_grads")])
    G["fox_w_in"] = _sum8(r_fox, name="sum_fox_w_in")[None]

    g3 = _exchange(_sum8(r_small, name="sum_small_grads"), gather=True, name="gather_small_sums")
    summed = dict(zip(names, _unpack(g3.reshape(small_pack.shape), [small_full[n].shape for n in names])))
    G["norm_g"] = lax.dynamic_slice_in_dim(summed["norm_g"], me * norm_g.shape[-1], norm_g.shape[-1], axis=2)
    G["ffn_conv_w"] = lax.dynamic_slice_in_dim(summed["ffn_conv_w"], me * ffn_conv_w.shape[-1], ffn_conv_w.shape[-1], axis=2)
    for n in REPLICATED:
        G[n] = summed[n]

    delta, new_m, new_v = {}, {}, {}
    small = tuple(n for n in WEIGHTS if n not in BIG and n != "ada_w")
    for n in WEIGHTS:
        if n not in small:
            delta[n], new_m[n], new_v[n] = _adamw_nd(W[n], G[n], M[n], V[n], name=f"adamw_{n}")
    packed = [_pack([src[n] for n in small], F32, 8) for src in (W, G, M, V)]
    for dst, buf in zip((delta, new_m, new_v), _adamw(*packed, name="adamw_small")):
        dst.update(zip(small, _unpack(buf, [W[n].shape for n in small])))

    return (loss, grad_x, *[G[n] for n in WEIGHTS], *[delta[n] for n in WEIGHTS], *[new_m[n] for n in WEIGHTS],
            *[new_v[n] for n in WEIGHTS])


def _columns_of_mod(dmod_all, me, ncol):
    flat = lax.dynamic_slice_in_dim(dmod_all, me * ncol, ncol, axis=2)
    return flat.transpose(1, 0, 2)
```

```python
import functools
import math

import jax
import jax.numpy as jnp
from jax import lax
from jax.experimental import pallas as pl
from jax.experimental.pallas import tpu as pltpu

F32, BF16 = jnp.float32, jnp.bfloat16
EPS = 1e-6
N_DEV = 8
D_MODEL = 1024
D_FF = 2816
HEADS = 16
HEAD_DIM = 64
S5_G, S5_P, S5_C = 64, 64, 16
S5_SUB = 8
V7X_VMEM_LIMIT = 56 * 1024 * 1024
NEG = -1e30
ADAM_LR, ADAM_B1, ADAM_B2, ADAM_EPS, ADAM_WD, ADAM_STEP = 1e-3, 0.9, 0.999, 1e-8, 0.01, 10
GELU_K = math.sqrt(2.0 / math.pi)
MESH_ID = pl.DeviceIdType.MESH


def _cp(*sem):
    return pltpu.CompilerParams(dimension_semantics=sem, vmem_limit_bytes=V7X_VMEM_LIMIT)


def _tile(n, target, mult=128):
    if n <= target:
        return n
    t = (target // mult) * mult
    while t >= mult:
        if n % t == 0:
            return t
        t -= mult
    return n


def _row(v):
    return v.reshape(1, -1).astype(F32)


def _mm(a, b, *, name, ta=False, tb=False, out_dtype=F32, tm=1024, tn=1024, tk=None, res=None, gate=None,
        a_h=False, b_h=False, o_h=False, with_t=False, comm=()):
    if a_h:
        M, K = a.shape[1], 2 * a.shape[2]
    elif ta:
        K, M = a.shape
    else:
        M, K = a.shape
    if b_h:
        N = 2 * b.shape[2]
    else:
        N = b.shape[0] if tb else b.shape[1]
    half_n = N // 2
    tm = _tile(M, tm, 128 if ta else 8)
    tn = _tile(half_n if (b_h or o_h) else N, tn)
    tk = K if tk is None else _tile(K // 2 if a_h else K, tk)
    nk = K // tk
    nkh, nnh = (K // 2) // tk if a_h else 1, half_n // tn
    if a_h:
        a_spec = pl.BlockSpec((None, tm, tk), lambda i, j, k: (k // nkh, i, k % nkh))
    elif ta:
        a_spec = pl.BlockSpec((tk, tm), lambda i, j, k: (k, i))
    else:
        a_spec = pl.BlockSpec((tm, tk), lambda i, j, k: (i, k))
    if b_h:
        b_spec = pl.BlockSpec((None, tk, tn), lambda i, j, k: (j // nnh, k, j % nnh))
    elif tb:
        b_spec = pl.BlockSpec((tn, tk), lambda i, j, k: (j, k))
    else:
        b_spec = pl.BlockSpec((tk, tn), lambda i, j, k: (k, j))
    if o_h:
        o_spec = pl.BlockSpec((None, tm, tn), lambda i, j, k: (j // nnh, i, j % nnh))
    else:
        o_spec = pl.BlockSpec((tm, tn), lambda i, j, k: (i, j))
    dn = (((0 if ta else 1,), (1 if tb else 0,)), ((), ()))
    fused = res is not None

    def body(*refs):
        if fused:
            a_ref, b_ref, r_ref, g_ref, m_ref, o_ref, acc_ref = refs
        elif with_t:
            a_ref, b_ref, o_ref, t_ref, acc_ref = refs
        else:
            a_ref, b_ref, o_ref, acc_ref = refs
        p = lax.dot_general(a_ref[...].astype(BF16), b_ref[...].astype(BF16), dn, preferred_element_type=F32)

        def finish(acc):
            if fused:
                m_ref[...] = acc.astype(m_ref.dtype)
                o_ref[...] = r_ref[...] + g_ref[...] * acc
            else:
                o_ref[...] = acc.astype(o_ref.dtype)
                if with_t:
                    t_ref[...] = acc.T.astype(t_ref.dtype)

        if nk == 1:
            finish(p)
        else:
            k = pl.program_id(2)

            @pl.when(k == 0)
            def _():
                acc_ref[...] = p

            @pl.when(k > 0)
            def _():
                acc_ref[...] += p

            @pl.when(k == nk - 1)
            def _():
                finish(acc_ref[...])

    in_specs = [a_spec, b_spec]
    args = [a, b]
    if fused:
        in_specs += [o_spec, pl.BlockSpec((1, tn), lambda i, j, k: (0, j))]
        args += [res, gate]
        out_shape = [jax.ShapeDtypeStruct((M, N), BF16), jax.ShapeDtypeStruct((M, N), F32)]
        out_specs = [o_spec, o_spec]
    else:
        out_shape = [jax.ShapeDtypeStruct((2, M, half_n) if o_h else (M, N), out_dtype)]
        out_specs = [o_spec]
        if with_t:
            out_shape.append(jax.ShapeDtypeStruct((N, M), out_dtype))
            out_specs.append(pl.BlockSpec((tn, tm), lambda i, j, k: (j, i)))
    outs, exchanged = _call(
        body, name=name, grid=(M // tm, N // tn, nk), in_specs=in_specs, out_specs=out_specs, out_shape=out_shape,
        scratch_shapes=[pltpu.VMEM((tm, tn) if nk > 1 else (8, 128), F32)], sem=("parallel", "parallel", "arbitrary"),
        args=args, comm=comm)
    outs = tuple(outs) if (fused or with_t) else outs[0]
    return (outs, exchanged) if comm else outs


def _modulate(h, g, shift, scale, *, name):
    S, D = h.shape
    tm = _tile(S, 512, 8)
    vec = pl.BlockSpec((1, D), lambda i: (0, 0))
    blk = pl.BlockSpec((tm, D), lambda i: (i, 0))

    def body(h_ref, g_ref, sh_ref, sc_ref, o_ref):
        x = h_ref[...]
        r = lax.rsqrt(jnp.mean(x * x, axis=-1, keepdims=True) + EPS)
        o_ref[...] = ((x * r * g_ref[...]) * (1.0 + sc_ref[...]) + sh_ref[...]).astype(o_ref.dtype)

    return pl.pallas_call(body, name=name, grid=(S // tm,), in_specs=[blk, vec, vec, vec], out_specs=blk,
                          out_shape=jax.ShapeDtypeStruct((S, D), BF16), compiler_params=_cp("parallel"))(h, g, shift, scale)


def _modulate_bwd(h, dhn, dh_in, g, scale, m_prev=None, gate_prev=None, *, name):
    S, D = h.shape
    tm = _tile(S, 512, 8)
    vec = pl.BlockSpec((1, D), lambda i: (0, 0))
    blk = pl.BlockSpec((tm, D), lambda i: (i, 0))
    sums = pl.BlockSpec((8, D), lambda i: (0, 0))
    below = m_prev is not None

    def body(h_ref, dhn_ref, dhi_ref, g_ref, sc_ref, *rest):
        dh_ref, s_ref = rest[-3:-1] if below else rest[-2:]

        @pl.when(pl.program_id(0) == 0)
        def _():
            s_ref[...] = jnp.zeros_like(s_ref)

        x = h_ref[...]
        r = lax.rsqrt(jnp.mean(x * x, axis=-1, keepdims=True) + EPS)
        xhat = x * r
        gv = g_ref[...]
        dhn_v = dhn_ref[...].astype(F32)
        dn = dhn_v * (1.0 + sc_ref[...])
        s_ref[0:1, :] += jnp.sum(dhn_v, axis=0, keepdims=True)
        s_ref[1:2, :] += jnp.sum(dhn_v * (xhat * gv), axis=0, keepdims=True)
        s_ref[2:3, :] += jnp.sum(dn * xhat, axis=0, keepdims=True)
        dxh = dn * gv
        dh = dhi_ref[...] + r * (dxh - xhat * jnp.mean(dxh * xhat, axis=-1, keepdims=True))
        dh_ref[...] = dh
        if below:
            mp_ref, gp_ref, dm_ref = rest[0], rest[1], rest[-1]
            dm_ref[...] = (dh * gp_ref[...]).astype(dm_ref.dtype)
            s_ref[3:4, :] += jnp.sum(dh * mp_ref[...].astype(F32), axis=0, keepdims=True)

    extra_in, extra_args = ([blk, vec], [m_prev, gate_prev]) if below else ([], [])
    return pl.pallas_call(body, name=name, grid=(S // tm,), in_specs=[blk, blk, blk, vec, vec] + extra_in,
                          out_specs=[blk, sums] + ([blk] if below else []),
                          out_shape=[jax.ShapeDtypeStruct((S, D), F32), jax.ShapeDtypeStruct((8, D), F32)]
                          + ([jax.ShapeDtypeStruct((S, D), BF16)] if below else []),
                          compiler_params=_cp("arbitrary"))(h, dhn, dh_in, g, scale, *extra_args)


def _loss_head(h, g, tgt, m_prev, gate_prev, *, name):
    S, D = h.shape
    tm = _tile(S, 512, 8)
    vec = pl.BlockSpec((1, D), lambda i: (0, 0))
    blk = pl.BlockSpec((tm, D), lambda i: (i, 0))
    lss = pl.BlockSpec((8, 128), lambda i: (0, 0))
    sums = pl.BlockSpec((8, D), lambda i: (0, 0))

    def body(h_ref, g_ref, t_ref, mp_ref, gp_ref, l_ref, dh_ref, s_ref, dm_ref):
        @pl.when(pl.program_id(0) == 0)
        def _():
            l_ref[...] = jnp.zeros_like(l_ref)
            s_ref[...] = jnp.zeros_like(s_ref)

        x = h_ref[...]
        r = lax.rsqrt(jnp.mean(x * x, axis=-1, keepdims=True) + EPS)
        xhat = x * r
        gv = g_ref[...]
        e = xhat * gv - t_ref[...]
        l_ref[...] += 0.5 * jnp.sum(jnp.mean(e * e, axis=-1, keepdims=True))
        dy = e * (1.0 / D)
        s_ref[0:1, :] += jnp.sum(dy * xhat, axis=0, keepdims=True)
        dxh = dy * gv
        dh = r * (dxh - xhat * jnp.mean(dxh * xhat, axis=-1, keepdims=True))
        dh_ref[...] = dh
        dm_ref[...] = (dh * gp_ref[...]).astype(dm_ref.dtype)
        s_ref[1:2, :] += jnp.sum(dh * mp_ref[...].astype(F32), axis=0, keepdims=True)

    return pl.pallas_call(body, name=name, grid=(S // tm,), in_specs=[blk, vec, blk, blk, vec], out_specs=[lss, blk, sums, blk],
                          out_shape=[jax.ShapeDtypeStruct((8, 128), F32), jax.ShapeDtypeStruct((S, D), F32),
                                     jax.ShapeDtypeStruct((8, D), F32), jax.ShapeDtypeStruct((S, D), BF16)],
                          compiler_params=_cp("arbitrary"))(h, g, tgt, m_prev, gate_prev)


def _shift_down(x, k, edge):
    tm = x.shape[0]
    rows = lax.broadcasted_iota(jnp.int32, x.shape, 0)
    out = pltpu.roll(x, k, 0)
    for j in range(k):
        out = jnp.where(rows == j, edge[8 - k + j:8 - k + j + 1, :], out)
    return out


def _shift_up(x, k, edge):
    tm = x.shape[0]
    rows = lax.broadcasted_iota(jnp.int32, x.shape, 0)
    out = pltpu.roll(x, tm - k, 0)
    for j in range(k):
        out = jnp.where(rows == tm - k + j, edge[j:j + 1, :], out)
    return out


def _conv_gate_fwd(up, cw, cb, *, name):
    _, S, F = up.shape
    tf = _tile(F, 1408)
    nf = F // tf
    tm = _tile(S, 512, 8)

    def body(a_ref, b_ref, w_ref, cb_ref, z_ref, edge_ref):
        @pl.when(pl.program_id(1) == 0)
        def _():
            edge_ref[...] = jnp.zeros_like(edge_ref)

        a = a_ref[...].astype(F32)
        edge = edge_ref[...]
        w = w_ref[...]
        ac = cb_ref[...] + w[2:3, :] * a + w[1:2, :] * _shift_down(a, 1, edge) + w[0:1, :] * _shift_down(a, 2, edge)
        edge_ref[...] = a[tm - 8:tm, :]
        z_ref[...] = (ac * jax.nn.sigmoid(ac) * b_ref[...].astype(F32)).astype(z_ref.dtype)

    return pl.pallas_call(
        body, name=name, grid=(nf, S // tm),
        in_specs=[pl.BlockSpec((None, tm, tf), lambda j, i: (0, i, j)), pl.BlockSpec((None, tm, tf), lambda j, i: (1, i, j)),
                  pl.BlockSpec((3, tf), lambda j, i: (0, j)), pl.BlockSpec((1, tf), lambda j, i: (0, j))],
        out_specs=pl.BlockSpec((tm, tf), lambda j, i: (i, j)),
        out_shape=jax.ShapeDtypeStruct((S, F), BF16), scratch_shapes=[pltpu.VMEM((8, tf), F32)],
        compiler_params=_cp("parallel", "arbitrary"))(up, up, cw, cb)


def _conv_gate_bwd(up, dz, cw, cb, *, name):
    _, S, F = up.shape
    tf = _tile(F, 1408)
    nf = F // tf
    tm = _tile(S, 512, 8)
    nt = S // tm
    hb = tm // 8

    def body(a_ref, ah_ref, b_ref, dz_ref, w_ref, cb_ref, d_ref, s_ref, edge_ref):
        i = pl.program_id(1)

        @pl.when(i == 0)
        def _():
            edge_ref[...] = jnp.zeros_like(edge_ref)
            s_ref[...] = jnp.zeros_like(s_ref)

        a = a_ref[...].astype(F32)
        halo = jnp.where(i == nt - 1, 0.0, ah_ref[...].astype(F32))
        w = w_ref[...]
        a1 = _shift_down(a, 1, halo)
        a2 = _shift_down(a, 2, halo)
        ac = cb_ref[...] + w[2:3, :] * a + w[1:2, :] * a1 + w[0:1, :] * a2
        sg = jax.nn.sigmoid(ac)
        dzv = dz_ref[...].astype(F32)
        si = ac * sg
        d_ref[1] = (dzv * si).astype(d_ref.dtype)
        dac = (dzv * b_ref[...].astype(F32)) * (sg + si * (1.0 - sg))
        s_ref[0:1, :] += jnp.sum(dac * a2, axis=0, keepdims=True)
        s_ref[1:2, :] += jnp.sum(dac * a1, axis=0, keepdims=True)
        s_ref[2:3, :] += jnp.sum(dac * a, axis=0, keepdims=True)
        s_ref[3:4, :] += jnp.sum(dac, axis=0, keepdims=True)
        edge = edge_ref[...]
        da = w[2:3, :] * dac + w[1:2, :] * _shift_up(dac, 1, edge) + w[0:1, :] * _shift_up(dac, 2, edge)
        edge_ref[...] = dac[0:8, :]
        d_ref[0] = da.astype(d_ref.dtype)

    tile = lambda hlf: pl.BlockSpec((None, tm, tf), lambda j, i: (hlf, nt - 1 - i, j))
    d_up, sums = pl.pallas_call(
        body, name=name, grid=(nf, nt),
        in_specs=[tile(0),
                  pl.BlockSpec((None, 8, tf), lambda j, i: (0, jnp.maximum((nt - 1 - i) * hb - 1, 0), j)),
                  tile(1), pl.BlockSpec((tm, tf), lambda j, i: (nt - 1 - i, j)),
                  pl.BlockSpec((3, tf), lambda j, i: (0, j)), pl.BlockSpec((1, tf), lambda j, i: (0, j))],
        out_specs=[pl.BlockSpec((2, tm, tf), lambda j, i: (0, nt - 1 - i, j)), pl.BlockSpec((8, tf), lambda j, i: (0, j))],
        out_shape=[jax.ShapeDtypeStruct((2, S, F), BF16), jax.ShapeDtypeStruct((8, F), F32)],
        scratch_shapes=[pltpu.VMEM((8, tf), F32)],
        compiler_params=_cp("parallel", "arbitrary"))(up, up, up, dz, cw, cb)
    return d_up, sums


def _log_sigmoid(x):
    return jnp.minimum(x, 0.0) - jnp.log(1.0 + jnp.exp(-jnp.abs(x)))


def _tri_ones(n, upper):
    r = lax.broadcasted_iota(jnp.int32, (n, n), 0)
    c = lax.broadcasted_iota(jnp.int32, (n, n), 1)
    return jnp.where((c >= r) if upper else (c <= r), 1.0, 0.0).astype(F32)


def _fgate_fwd(fl, bf, *, name):
    S, W = fl.shape
    tb = _tile(S, 256, 8)

    def body(fl_ref, b_ref, o_ref, carry_ref):
        @pl.when(pl.program_id(0) == 0)
        def _():
            carry_ref[...] = jnp.zeros_like(carry_ref)

        lf = _log_sigmoid(fl_ref[...] + b_ref[...])
        cs = jnp.dot(_tri_ones(tb, False), lf, precision=lax.Precision.HIGHEST, preferred_element_type=F32) + carry_ref[0:1, :]
        o_ref[...] = cs
        carry_ref[...] = jnp.broadcast_to(cs[tb - 1:tb, :], carry_ref.shape)

    blk = pl.BlockSpec((tb, W), lambda i: (i, 0))
    return pl.pallas_call(body, name=name, grid=(S // tb,), in_specs=[blk, pl.BlockSpec((1, W), lambda i: (0, 0))], out_specs=blk,
                          out_shape=jax.ShapeDtypeStruct((S, W), F32), scratch_shapes=[pltpu.VMEM((8, W), F32)],
                          compiler_params=_cp("arbitrary"))(fl, bf)


def _fgate_bwd(dF, fl, bf, *, name):
    S, W = fl.shape
    tb = _tile(S, 256, 8)
    nb = S // tb

    def body(d_ref, fl_ref, b_ref, o_ref, s_ref, carry_ref):
        @pl.when(pl.program_id(0) == 0)
        def _():
            carry_ref[...] = jnp.zeros_like(carry_ref)
            s_ref[...] = jnp.zeros_like(s_ref)

        rc = jnp.dot(_tri_ones(tb, True), d_ref[...], precision=lax.Precision.HIGHEST, preferred_element_type=F32) + carry_ref[0:1, :]
        carry_ref[...] = jnp.broadcast_to(rc[0:1, :], carry_ref.shape)
        dfl = rc * jax.nn.sigmoid(-(fl_ref[...] + b_ref[...]))
        o_ref[...] = dfl
        s_ref[...] += jnp.sum(dfl, axis=0, keepdims=True)

    blk = pl.BlockSpec((tb, W), lambda i: (nb - 1 - i, 0))
    vec = pl.BlockSpec((1, W), lambda i: (0, 0))
    return pl.pallas_call(body, name=name, grid=(nb,), in_specs=[blk, blk, vec], out_specs=[blk, vec],
                          out_shape=[jax.ShapeDtypeStruct((S, W), F32), jax.ShapeDtypeStruct((1, W), F32)],
                          scratch_shapes=[pltpu.VMEM((8, W), F32)], compiler_params=_cp("arbitrary"))(dF, fl, bf)


NT_DIMS = (((1,), (1,)), ((), ()))
TN_DIMS = (((0,), (0,)), ((), ()))
HP = HEADS // 2
Q_SCALE = HEAD_DIM ** -0.5


def _head_mask(x, hh):
    lanes = lax.broadcasted_iota(jnp.int32, x.shape, 1)
    return jnp.where((lanes >= hh * HEAD_DIM) & (lanes < (hh + 1) * HEAD_DIM), x, jnp.zeros_like(x))


def _lanes_from(cols, shape):
    lanes = lax.broadcasted_iota(jnp.int32, shape, 1)
    out = jnp.zeros(shape, F32)
    for i, cvec in enumerate(cols):
        out = jnp.where(lanes == i, cvec, out)
    return out


AUG_ROWS = 16
DIAGONAL_TILES = ((0, 0, True), (0, 1, False), (1, 1, True))


def _qa_pair(qt, aug):
    fill = jnp.zeros((HEAD_DIM - AUG_ROWS, qt.shape[1]), qt.dtype)
    return [jnp.concatenate([qt[0:HEAD_DIM], aug[0], fill], axis=0), jnp.concatenate([aug[1], fill, qt[HEAD_DIM:]], axis=0)]


def _rows_of_head(xt, hh):
    rows = lax.broadcasted_iota(jnp.int32, xt.shape, 0)
    return jnp.where((rows >= hh * HEAD_DIM) & (rows < (hh + 1) * HEAD_DIM), xt, jnp.zeros_like(xt))


def _put_rows(ref, rows):
    for i, r in enumerate(rows):
        ref[i:i + 1, :] = r
    ref[len(rows):, :] = jnp.zeros((ref.shape[0] - len(rows), ref.shape[1]), ref.dtype)


def _diag_mask_t(s):
    rows = lax.broadcasted_iota(jnp.int32, s.shape, 0)
    cols = lax.broadcasted_iota(jnp.int32, s.shape, 1)
    return jnp.where(rows <= cols, s, NEG)


def _foxt_fwd(qkvT, ka, vtb, augq, *, tb, name, comm=()):
    S = qkvT.shape[1]
    nb = S // tb

    def body(q_ref, ka_ref, v_ref, aq_ref, o_ref, lse_ref, m0, m1, l0, l1, acc0, acc1):
        qi = pl.program_id(1)
        qa = _qa_pair(q_ref[...], aq_ref[...])
        half = tb // 2
        state = ((m0, l0, acc0), (m1, l1, acc1))
        for m_sc, l_sc, acc_sc in state:
            m_sc[...] = jnp.full_like(m_sc, NEG)
            l_sc[...] = jnp.zeros_like(l_sc)
            acc_sc[...] = jnp.zeros_like(acc_sc)

        def tile(kj, ko, qo, n, masked):
            k0 = pl.multiple_of(kj * tb + ko, n)
            cols = slice(qo, qo + n)
            logits = [jnp.dot(ka_ref[hh, pl.ds(k0, n), :], qa[hh][:, cols], preferred_element_type=F32) for hh in range(2)]
            updates = []
            for hh, (m_sc, l_sc, acc_sc) in enumerate(state):
                s = _diag_mask_t(logits[hh]) if masked else logits[hh]
                m_old = m_sc[:, cols]
                m_new = jnp.maximum(m_old, jnp.max(s, axis=0, keepdims=True))
                alpha = jnp.exp(m_old - m_new)
                p = jnp.exp(s - m_new)
                l_sc[:, cols] = alpha * l_sc[:, cols] + jnp.sum(p, axis=0, keepdims=True)
                m_sc[:, cols] = m_new
                vt = v_ref[kj, HEAD_DIM * hh:HEAD_DIM * (hh + 1), ko:ko + n]
                updates.append((alpha, jnp.dot(vt, p.astype(BF16), preferred_element_type=F32)))
            for (alpha, pv), (_, _, acc_sc) in zip(updates, state):
                acc_sc[:, cols] = alpha * acc_sc[:, cols] + pv

        def off_diagonal(kj, c):
            tile(kj, 0, 0, tb, False)
            return c

        lax.fori_loop(0, qi, off_diagonal, 0)
        for ko, qo, masked in DIAGONAL_TILES:
            tile(qi, ko * half, qo * half, half, masked)
        for hh, (m_sc, l_sc, acc_sc) in enumerate(state):
            o_ref[HEAD_DIM * hh:HEAD_DIM * (hh + 1), :] = (acc_sc[...] / l_sc[...]).astype(o_ref.dtype)
        _put_rows(lse_ref, [m_sc[...] + jnp.log(l_sc[...]) for m_sc, l_sc, _ in state])

    return _call(
        body, name=name, grid=(HP, nb), comm=comm, sem=("parallel", "parallel"), args=(qkvT, ka, vtb, augq),
        in_specs=[pl.BlockSpec((128, tb), lambda hp, qi: (hp, qi)),
                  pl.BlockSpec((None, 2, S, 128), lambda hp, qi: (hp, 0, 0, 0)),
                  pl.BlockSpec((None, nb, 128, tb), lambda hp, qi: (hp, 0, 0, 0)),
                  pl.BlockSpec((None, 2, AUG_ROWS, tb), lambda hp, qi: (hp, 0, 0, qi))],
        out_specs=[pl.BlockSpec((128, tb), lambda hp, qi: (hp, qi)), pl.BlockSpec((None, 8, tb), lambda hp, qi: (hp, 0, qi))],
        out_shape=[jax.ShapeDtypeStruct((D_MODEL, S), BF16), jax.ShapeDtypeStruct((HP, 8, S), F32)],
        scratch_shapes=[pltpu.VMEM((1, tb), F32)] * 4 + [pltpu.VMEM((HEAD_DIM, tb), F32)] * 2)


def _foxt_bwd(qkvT, qkv, ka, qtb, dotb, do, augqb, lseb, otb, *, tb, name, comm=()):
    S = qkv.shape[0]
    nb = S // tb

    def body(ka_ref, v_ref, kt_ref, qt_ref, dot_ref, qn_ref, dn_ref, aq_ref, ls_ref, ot_ref,
             dq_ref, dk_ref, dv_ref, df_ref, dr_ref, dq_sc, dk_sc, dv_sc, ds_sc, dr_sc):
        kj = pl.program_id(1)

        @pl.when(kj == 0)
        def _():
            dq_sc[...] = jnp.zeros_like(dq_sc)
            dr_sc[...] = jnp.zeros_like(dr_sc)

        dk_sc[...] = jnp.zeros_like(dk_sc)
        dv_sc[...] = jnp.zeros_like(dv_sc)
        ds_sc[...] = jnp.zeros_like(ds_sc)
        v2 = v_ref[...]
        kth = [kt_ref[HEAD_DIM * hh:HEAD_DIM * (hh + 1), :] for hh in range(2)]

        def block(qi, tiles):
            q0 = pl.multiple_of(qi * tb, tb)
            qa = _qa_pair(qt_ref[qi], aq_ref[qi])
            dot, ls = dot_ref[qi], ls_ref[qi]
            prod = ot_ref[qi].astype(F32) * dot.astype(F32)
            delta = [jnp.sum(prod[HEAD_DIM * hh:HEAD_DIM * (hh + 1)], axis=0, keepdims=True) for hh in range(2)]
            qn, dn = qn_ref[pl.ds(q0, tb), :], dn_ref[pl.ds(q0, tb), :]
            for ko, qo, n, masked in tiles:
                keys, cols = slice(ko, ko + n), slice(qo, qo + n)
                for hh in range(2):
                    s = jnp.dot(ka_ref[hh, keys, :], qa[hh][:, cols], preferred_element_type=F32)
                    if masked:
                        s = _diag_mask_t(s)
                    p = jnp.exp(s - ls[hh:hh + 1, cols])
                    dp = jnp.dot(v2[keys], _rows_of_head(dot[:, cols], hh), preferred_element_type=F32)
                    ds = p * (dp - delta[hh][:, cols])
                    dsb = ds.astype(BF16)
                    dv_sc[keys, :] += jnp.dot(p.astype(BF16), _head_mask(dn[cols], hh), preferred_element_type=F32)
                    dk_sc[keys, :] += jnp.dot(dsb, _head_mask(qn[cols], hh), preferred_element_type=F32)
                    dq_sc[qi, HEAD_DIM * hh:HEAD_DIM * (hh + 1), cols] += jnp.dot(kth[hh][:, keys], dsb, preferred_element_type=F32)
                    part = ds[:, 0:128]
                    for j in range(1, n // 128):
                        part = part + ds[:, 128 * j:128 * (j + 1)]
                    ds_sc[hh, keys, :] += part
                    dr_sc[qi, hh:hh + 1, cols] += jnp.sum(ds, axis=0, keepdims=True)

        def off_diagonal(i, c):
            block(kj + 1 + i, [(0, 0, tb, False)])
            return c

        half = tb // 2
        block(kj, [(ko * half, qo * half, half, masked) for ko, qo, masked in DIAGONAL_TILES])
        lax.fori_loop(0, nb - 1 - kj, off_diagonal, 0)
        dk_ref[...] = dk_sc[...].astype(dk_ref.dtype)
        dv_ref[...] = dv_sc[...].astype(dv_ref.dtype)
        df_ref[...] = _lanes_from([-jnp.sum(ds_sc[hh], axis=1, keepdims=True) for hh in range(2)], (tb, 128))

        @pl.when(kj == nb - 1)
        def _():
            dq_ref[...] = (dq_sc[...] * Q_SCALE).astype(dq_ref.dtype)
            dr_ref[...] = dr_sc[...]

    resident = lambda rows: pl.BlockSpec((None, nb) + rows, lambda hp, kj: (hp,) + (0,) * (len(rows) + 1))
    whole = pl.BlockSpec((S, 128), lambda hp, kj: (0, hp))
    kblk = lambda off: pl.BlockSpec((tb, 128), lambda hp, kj: (kj, off + hp))
    return _call(
        body, name=name, grid=(HP, nb), comm=comm, sem=("parallel", "arbitrary"),
        args=(ka, qkv, qkvT, qtb, dotb, qkv, do, augqb, lseb, otb),
        in_specs=[pl.BlockSpec((None, 2, tb, 128), lambda hp, kj: (hp, 0, kj, 0)), kblk(2 * HP),
                  pl.BlockSpec((128, tb), lambda hp, kj: (HP + hp, kj)),
                  resident((128, tb)), resident((128, tb)), whole, whole, resident((2, AUG_ROWS, tb)), resident((8, tb)),
                  resident((128, tb))],
        out_specs=[resident((128, tb)), kblk(0), kblk(0), pl.BlockSpec((None, tb, 128), lambda hp, kj: (hp, kj, 0)),
                   resident((8, tb))],
        out_shape=[jax.ShapeDtypeStruct((HP, nb, 128, tb), BF16), jax.ShapeDtypeStruct((S, D_MODEL), BF16),
                   jax.ShapeDtypeStruct((S, D_MODEL), BF16), jax.ShapeDtypeStruct((HP, S, 128), F32),
                   jax.ShapeDtypeStruct((HP, nb, 8, tb), F32)],
        scratch_shapes=[pltpu.VMEM((nb, 128, tb), F32), pltpu.VMEM((tb, 128), F32), pltpu.VMEM((tb, 128), F32),
                        pltpu.VMEM((2, tb, 128), F32), pltpu.VMEM((nb, 8, tb), F32)])


def _split3(x):
    rnd = lambda v: lax.reduce_precision(v, exponent_bits=8, mantissa_bits=7)
    hi = rnd(x)
    mid = rnd(x - hi)
    lo = rnd(x - hi - mid)
    return hi.astype(BF16), mid.astype(BF16), lo.astype(BF16)


def _blocked(xt, nb, tb):
    lead = xt.shape[:-1]
    x = xt.reshape(lead + (nb, tb))
    return jnp.moveaxis(x, -2, 1)


def _aug_rows(bias):
    ones = jnp.ones(bias.shape, BF16)
    zeros = jnp.zeros(bias.shape, BF16)
    return jnp.stack(list(_split3(bias)) + [ones] * 3 + [zeros] * (AUG_ROWS - 6), axis=2)


def _fox_keys(qkv, fcol, *, name):
    S = qkv.shape[0]
    tr = _tile(S, 2048, 16)

    def body(k_ref, f_ref, o_ref):
        hp = pl.program_id(1)
        k2 = k_ref[...]
        lanes = lax.broadcasted_iota(jnp.int32, (tr, 128), 1)
        for hh in range(2):
            nf = -jnp.sum(jnp.where(lanes == 2 * hp + hh, f_ref[...], 0.0), axis=1, keepdims=True)
            hi = nf.astype(BF16).astype(F32)
            mid = (nf - hi).astype(BF16).astype(F32)
            lo = nf - hi - mid
            base = HEAD_DIM * (1 - hh)
            aug = jnp.where((lanes >= base) & (lanes < base + 3), 1.0, 0.0)
            for j, piece in enumerate((hi, mid, lo)):
                aug = jnp.where(lanes == base + 3 + j, piece, aug)
            mine = (lanes >= HEAD_DIM * hh) & (lanes < HEAD_DIM * (hh + 1))
            o_ref[hh] = jnp.where(mine, k2, aug.astype(BF16))

    return pl.pallas_call(
        body, name=name, grid=(S // tr, HP),
        in_specs=[pl.BlockSpec((tr, 128), lambda i, hp: (i, HP + hp)), pl.BlockSpec((tr, 128), lambda i, hp: (i, 0))],
        out_specs=pl.BlockSpec((None, 2, tr, 128), lambda i, hp: (hp, 0, i, 0)),
        out_shape=jax.ShapeDtypeStruct((HP, 2, S, 128), BF16), compiler_params=_cp("parallel", "parallel"))(qkv, fcol)


HALF = S5_SUB * S5_P
NCB = 2 * HALF // 128
RE, IM = slice(0, NCB // 2), slice(NCB // 2, NCB)


def _gelu(x):
    return 0.5 * x * (1.0 + jnp.tanh(GELU_K * (x + 0.044715 * x * x * x)))


def _gelu_grad(x):
    t = jnp.tanh(GELU_K * (x + 0.044715 * x * x * x))
    return 0.5 * (1.0 + t) + 0.5 * x * (1.0 - t * t) * GELU_K * (1.0 + 3.0 * 0.044715 * x * x)


def _s5_put(buf, s, val, lt):
    for cb in range(NCB):
        buf[cb, pl.ds(s, lt, stride=S5_SUB), :] = val[:, 128 * cb:128 * (cb + 1)]


def _s5_get(buf, s, lt):
    return jnp.concatenate([buf[cb, pl.ds(s, lt, stride=S5_SUB), :] for cb in range(NCB)], axis=1)


def _s5_project_in(u_ref, wb_ref, buf, lt):
    for s in range(S5_SUB):
        _s5_put(buf, s, jnp.dot(u_ref[:, 128 * s:128 * (s + 1)], wb_ref[s], preferred_element_type=F32), lt)


def _s5_scan(buf, lam_ref, h0, lt):
    a_re, a_im = lam_ref[RE], lam_ref[IM]

    def step(t, carry):
        hr, hi = carry
        r0 = pl.multiple_of(t * S5_SUB, S5_SUB)
        nr = a_re * hr - a_im * hi + buf[RE, pl.ds(r0, S5_SUB), :]
        ni = a_re * hi + a_im * hr + buf[IM, pl.ds(r0, S5_SUB), :]
        buf[RE, pl.ds(r0, S5_SUB), :] = nr
        buf[IM, pl.ds(r0, S5_SUB), :] = ni
        return nr, ni

    return lax.fori_loop(0, lt, step, (h0[RE], h0[IM]), unroll=4)


def _s5_fwd(u, wb, wc, lam, dskip, *, lt, name, comm=()):
    S, D = u.shape
    nt = S // lt

    def body(u_ref, wb_ref, wc_ref, lam_ref, d_ref, yp_ref, yg_ref, st_ref, buf, h_sc):
        @pl.when(pl.program_id(0) == 0)
        def _():
            h_sc[...] = jnp.zeros_like(h_sc)

        st_ref[...] = h_sc[...]
        _s5_project_in(u_ref, wb_ref, buf, lt)
        hr, hi = _s5_scan(buf, lam_ref, h_sc[...], lt)
        h_sc[RE] = hr
        h_sc[IM] = hi
        for s in range(S5_SUB):
            cols = slice(128 * s, 128 * (s + 1))
            hs = _s5_get(buf, s, lt).astype(BF16)
            yp = jnp.dot(hs, wc_ref[s], preferred_element_type=F32) + d_ref[:, cols] * u_ref[:, cols].astype(F32)
            yp_ref[:, cols] = yp.astype(yp_ref.dtype)
            yg_ref[:, cols] = _gelu(yp).astype(yg_ref.dtype)

    blk = pl.BlockSpec((lt, D), lambda i: (i, 0))
    full = lambda shp: pl.BlockSpec(shp, lambda i: (0,) * len(shp))
    state = (NCB, S5_SUB, 128)
    return _call(
        body, name=name, grid=(nt,), comm=comm, sem=("arbitrary",), args=(u, wb, wc, lam, dskip),
        in_specs=[blk, full(wb.shape), full(wc.shape), full(lam.shape), full(dskip.shape)],
        out_specs=[blk, blk, pl.BlockSpec((None,) + state, lambda i: (i, 0, 0, 0))],
        out_shape=[jax.ShapeDtypeStruct((S, D), BF16), jax.ShapeDtypeStruct((S, D), BF16), jax.ShapeDtypeStruct((nt,) + state, F32)],
        scratch_shapes=[pltpu.VMEM((NCB, lt * S5_SUB, 128), F32), pltpu.VMEM(state, F32)])


def _s5_bwd(u, dyp, st, wb, wc, lam, dskip, *, lt, name, comm=()):
    S, D = u.shape
    nt = S // lt

    def body(u_ref, dy_ref, st_ref, wb_ref, wc_ref, lam_ref, d_ref, du_ref, dwb_ref, dwc_ref, dlam_ref, hbuf, gbuf, g_sc):
        @pl.when(pl.program_id(0) == 0)
        def _():
            g_sc[...] = jnp.zeros_like(g_sc)
            dwb_ref[...] = jnp.zeros_like(dwb_ref)
            dwc_ref[...] = jnp.zeros_like(dwc_ref)
            dlam_ref[...] = jnp.zeros_like(dlam_ref)

        _s5_project_in(u_ref, wb_ref, hbuf, lt)
        _s5_scan(hbuf, lam_ref, st_ref[...], lt)
        for s in range(S5_SUB):
            dys = dy_ref[:, 128 * s:128 * (s + 1)]
            _s5_put(gbuf, s, lax.dot_general(dys, wc_ref[s], NT_DIMS, preferred_element_type=F32), lt)
        a_re, a_im = lam_ref[RE], lam_ref[IM]

        def one(t, carry, hp_re, hp_im):
            gr, gi, dar, dai = carry
            r0 = pl.multiple_of(t * S5_SUB, S5_SUB)
            nr = gbuf[RE, pl.ds(r0, S5_SUB), :] + a_re * gr + a_im * gi
            ni = gbuf[IM, pl.ds(r0, S5_SUB), :] + a_re * gi - a_im * gr
            gbuf[RE, pl.ds(r0, S5_SUB), :] = nr
            gbuf[IM, pl.ds(r0, S5_SUB), :] = ni
            return nr, ni, dar + nr * hp_re + ni * hp_im, dai + ni * hp_re - nr * hp_im

        def step(k, carry):
            t = lt - 1 - k
            p0 = pl.multiple_of((t - 1) * S5_SUB, S5_SUB)
            return one(t, carry, hbuf[RE, pl.ds(p0, S5_SUB), :], hbuf[IM, pl.ds(p0, S5_SUB), :])

        init = (g_sc[RE], g_sc[IM], dlam_ref[RE], dlam_ref[IM])
        carry = lax.fori_loop(0, lt - 1, step, init, unroll=2)
        gr, gi, dar, dai = one(0, carry, st_ref[RE], st_ref[IM])
        g_sc[RE] = gr
        g_sc[IM] = gi
        dlam_ref[RE] = dar
        dlam_ref[IM] = dai
        for s in range(S5_SUB):
            cols = slice(128 * s, 128 * (s + 1))
            gs = _s5_get(gbuf, s, lt).astype(BF16)
            hs = _s5_get(hbuf, s, lt).astype(BF16)
            us, dys = u_ref[:, cols], dy_ref[:, cols]
            du = lax.dot_general(gs, wb_ref[s], NT_DIMS, preferred_element_type=F32) + d_ref[:, cols] * dys.astype(F32)
            du_ref[:, cols] = du.astype(du_ref.dtype)
            dwb_ref[s] += lax.dot_general(us, gs, TN_DIMS, preferred_element_type=F32)
            dwc_ref[s] += lax.dot_general(hs, dys, TN_DIMS, preferred_element_type=F32)

    blk = pl.BlockSpec((lt, D), lambda i: (nt - 1 - i, 0))
    full = lambda shp: pl.BlockSpec(shp, lambda i: (0,) * len(shp))
    state = (NCB, S5_SUB, 128)
    words = pltpu.VMEM((NCB, lt * S5_SUB, 128), F32)
    return _call(
        body, name=name, grid=(nt,), comm=comm, sem=("arbitrary",), args=(u, dyp, st, wb, wc, lam, dskip),
        in_specs=[blk, blk, pl.BlockSpec((None,) + state, lambda i: (nt - 1 - i, 0, 0, 0)),
                  full(wb.shape), full(wc.shape), full(lam.shape), full(dskip.shape)],
        out_specs=[blk, full(wb.shape), full(wc.shape), full(lam.shape)],
        out_shape=[jax.ShapeDtypeStruct((S, D), BF16), jax.ShapeDtypeStruct(wb.shape, F32), jax.ShapeDtypeStruct(wc.shape, F32),
                   jax.ShapeDtypeStruct(lam.shape, F32)],
        scratch_shapes=[words, words, pltpu.VMEM(state, F32)])


def _s5_disc(lam_re, lam_im, log_dt, b_re, b_im):
    dt = jnp.exp(log_dt)[:, None]
    mag = jnp.exp(lam_re * dt)
    lb_re, lb_im = mag * jnp.cos(lam_im * dt), mag * jnp.sin(lam_im * dt)
    num_re = lb_re - 1.0
    den = lam_re * lam_re + lam_im * lam_im
    k_re = (num_re * lam_re + lb_im * lam_im) / den
    k_im = (lb_im * lam_re - num_re * lam_im) / den
    bb_re = k_re[..., None] * b_re - k_im[..., None] * b_im
    bb_im = k_re[..., None] * b_im + k_im[..., None] * b_re
    return lb_re, lb_im, bb_re, bb_im


def _s5_pack(lb_re, lb_im, bb_re, bb_im, c_re, c_im):
    eye = jnp.eye(S5_SUB, dtype=F32)
    tb = lambda x: jnp.einsum("sgcp,gh->sgchp", x.reshape(S5_SUB, S5_SUB, S5_P, S5_C).transpose(0, 1, 3, 2), eye).reshape(S5_SUB, 128, HALF)
    tc = lambda x: jnp.einsum("sgpc,gh->sgphc", x.reshape(S5_SUB, S5_SUB, S5_C, S5_P).transpose(0, 1, 3, 2), eye).reshape(S5_SUB, HALF, 128)
    wb = jnp.concatenate([tb(bb_re), tb(bb_im)], axis=2).astype(BF16)
    wc = jnp.concatenate([tc(c_re), -tc(c_im)], axis=1).astype(BF16)
    lam = jnp.concatenate([lb_re.reshape(S5_SUB, HALF), lb_im.reshape(S5_SUB, HALF)], axis=1)
    return wb, wc, lam.reshape(S5_SUB, NCB, 128).transpose(1, 0, 2)


def _s5_unpack(dwb, dwc, dlam):
    db = jnp.einsum("sgcrgp->rsgpc", dwb.reshape(S5_SUB, S5_SUB, S5_C, 2, S5_SUB, S5_P)).reshape(2, S5_G, S5_P, S5_C)
    dc = jnp.einsum("srgpgc->rsgcp", dwc.reshape(S5_SUB, 2, S5_SUB, S5_P, S5_SUB, S5_C)).reshape(2, S5_G, S5_C, S5_P)
    dlam = dlam.transpose(1, 0, 2).reshape(S5_SUB, 2 * HALF)
    return (dlam[:, :HALF].reshape(S5_G, S5_P), dlam[:, HALF:].reshape(S5_G, S5_P), db[0], db[1], dc[0], -dc[1])


def _glu_fwd(yg, gl, *, name):
    S, D = yg.shape
    tm = _tile(S, 512, 8)
    blk = pl.BlockSpec((tm, D), lambda i: (i, 0))

    def body(y_ref, g_ref, o_ref):
        o_ref[...] = (y_ref[...].astype(F32) * jax.nn.sigmoid(g_ref[...].astype(F32))).astype(o_ref.dtype)

    return pl.pallas_call(body, name=name, grid=(S // tm,), in_specs=[blk, blk], out_specs=blk,
                          out_shape=jax.ShapeDtypeStruct((S, D), BF16), compiler_params=_cp("parallel"))(yg, gl)


def _glu_bwd(dy2, yg, gl, *, name):
    S, D = yg.shape
    tm = _tile(S, 512, 8)
    blk = pl.BlockSpec((tm, D), lambda i: (i, 0))

    def body(d_ref, y_ref, g_ref, da_ref, dg_ref):
        d = d_ref[...].astype(F32)
        sg = jax.nn.sigmoid(g_ref[...].astype(F32))
        da_ref[...] = d * sg
        dg_ref[...] = (d * y_ref[...].astype(F32) * sg * (1.0 - sg)).astype(dg_ref.dtype)

    return pl.pallas_call(body, name=name, grid=(S // tm,), in_specs=[blk, blk, blk], out_specs=[blk, blk],
                          out_shape=[jax.ShapeDtypeStruct((S, D), F32), jax.ShapeDtypeStruct((S, D), BF16)],
                          compiler_params=_cp("parallel"))(dy2, yg, gl)


def _gelu_bwd(da, db, ypre, u, *, name):
    S, D = ypre.shape
    tm = _tile(S, 512, 8)
    blk = pl.BlockSpec((tm, D), lambda i: (i, 0))
    vec = pl.BlockSpec((1, D), lambda i: (0, 0))

    def body(a_ref, b_ref, y_ref, u_ref, o_ref, s_ref):
        @pl.when(pl.program_id(0) == 0)
        def _():
            s_ref[...] = jnp.zeros_like(s_ref)

        dy = (a_ref[...] + b_ref[...]) * _gelu_grad(y_ref[...].astype(F32))
        o_ref[...] = dy.astype(o_ref.dtype)
        s_ref[...] += jnp.sum(dy * u_ref[...].astype(F32), axis=0, keepdims=True)

    return pl.pallas_call(body, name=name, grid=(S // tm,), in_specs=[blk, blk, blk, blk], out_specs=[blk, vec],
                          out_shape=[jax.ShapeDtypeStruct((S, D), BF16), jax.ShapeDtypeStruct((1, D), F32)],
                          compiler_params=_cp("arbitrary"))(da, db, ypre, u)


def _my_index():
    return 4 * lax.axis_index("x") + 2 * lax.axis_index("y") + lax.axis_index("c")


def _comm_plan(payloads):
    n = len(payloads)
    hbm = pl.BlockSpec(memory_space=pl.ANY)
    shapes = [jax.ShapeDtypeStruct((N_DEV,) + (x.shape if g else x.shape[1:]), x.dtype) for x, g in payloads]
    sems = [pltpu.SemaphoreType.DMA((n, N_DEV - 1)), pltpu.SemaphoreType.DMA((n, N_DEV - 1)), pltpu.SemaphoreType.DMA((n,))] if n else []
    return [hbm] * n, [hbm] * n, shapes, sems


def _comm_copies(x_refs, o_refs, gathers, send_sems, recv_sems, local_sems):
    mx, my, mc = lax.axis_index("x"), lax.axis_index("y"), lax.axis_index("c")
    me = 4 * mx + 2 * my + mc
    copies = []
    for i, (x_ref, o_ref, gather) in enumerate(zip(x_refs, o_refs, gathers)):
        src = (lambda j, r=x_ref: r) if gather else (lambda j, r=x_ref: r.at[j])
        copies.append(pltpu.make_async_copy(src(me), o_ref.at[me], local_sems.at[i]))
        for k in range(1, N_DEV):
            px, py, pc = mx ^ (k >> 2), my ^ ((k >> 1) & 1), mc ^ (k & 1)
            copies.append(pltpu.make_async_remote_copy(src(4 * px + 2 * py + pc), o_ref.at[me], send_sems.at[i, k - 1],
                                                       recv_sems.at[i, k - 1], device_id=(px, py, pc), device_id_type=MESH_ID))
    return copies


def _call(body, *, name, grid, in_specs, out_specs, out_shape, scratch_shapes, sem, args, comm=()):
    if not comm:
        outs = pl.pallas_call(body, name=name, grid=grid, in_specs=in_specs, out_specs=out_specs, out_shape=out_shape,
                              scratch_shapes=scratch_shapes, compiler_params=_cp(*sem))(*args)
        return outs, ()
    n, n_in, n_out = len(comm), len(in_specs), len(out_specs)
    c_in, c_out, c_shapes, c_sems = _comm_plan(comm)
    gathers = [g for _, g in comm]

    def wrapped(*refs):
        own_in, x_refs = refs[:n_in], refs[n_in:n_in + n]
        own_out, o_refs = refs[n_in + n:n_in + n + n_out], refs[n_in + n + n_out:n_in + 2 * n + n_out]
        own_scratch, sems = refs[n_in + 2 * n + n_out:-3], refs[-3:]
        ids = [pl.program_id(a) for a in range(len(grid))]
        first = functools.reduce(jnp.logical_and, [i == 0 for i in ids])
        last = functools.reduce(jnp.logical_and, [i == g - 1 for i, g in zip(ids, grid)])

        @pl.when(first)
        def _():
            for cp in _comm_copies(x_refs, o_refs, gathers, *sems):
                cp.start()

        body(*own_in, *own_out, *own_scratch)

        @pl.when(last)
        def _():
            for cp in _comm_copies(x_refs, o_refs, gathers, *sems):
                cp.wait()

    outs = pl.pallas_call(wrapped, name=name, grid=grid, in_specs=list(in_specs) + c_in, out_specs=list(out_specs) + c_out,
                          out_shape=list(out_shape) + c_shapes, scratch_shapes=list(scratch_shapes) + c_sems,
                          compiler_params=_cp(*["arbitrary"] * len(grid)))(*args, *[x for x, _ in comm])
    return outs[:n_out], outs[n_out:]


def _exchange_many(payloads, *, name):
    n = len(payloads)
    in_specs, out_specs, shapes, sems = _comm_plan(payloads)
    gathers = [g for _, g in payloads]

    def body(*refs):
        copies = _comm_copies(refs[:n], refs[n:2 * n], gathers, *refs[2 * n:])
        for cp in copies:
            cp.start()
        for cp in copies:
            cp.wait()

    return pl.pallas_call(body, name=name, in_specs=in_specs, out_specs=out_specs, out_shape=shapes, scratch_shapes=sems,
                          compiler_params=pltpu.CompilerParams(has_side_effects=True))(*[x for x, _ in payloads])


def _exchange(x, *, gather, name):
    return _exchange_many([(x, gather)], name=name)[0]


def _sum8(x8, *, name):
    _, R, C = x8.shape
    tr = _tile(R, 256, 16)

    def body(x_ref, o_ref):
        acc = x_ref[0].astype(F32)
        for j in range(1, N_DEV):
            acc = acc + x_ref[j].astype(F32)
        o_ref[...] = acc

    return pl.pallas_call(body, name=name, grid=(R // tr,), in_specs=[pl.BlockSpec((N_DEV, tr, C), lambda i: (0, i, 0))],
                          out_specs=pl.BlockSpec((tr, C), lambda i: (i, 0)), out_shape=jax.ShapeDtypeStruct((R, C), F32),
                          compiler_params=_cp("parallel"))(x8)


def _pack(arrs, dtype, row_mult):
    flat = jnp.concatenate([a.reshape(-1).astype(dtype) for a in arrs])
    n = flat.shape[0]
    rows = -(-n // 1024)
    rows = -(-rows // row_mult) * row_mult
    return jnp.pad(flat, (0, rows * 1024 - n)).reshape(rows, 1024)


def _unpack(buf, shapes):
    lead = buf.shape[:-2]
    flat = buf.reshape(lead + (-1,))
    out, off = [], 0
    for s in shapes:
        n = math.prod(s)
        out.append(flat[..., off:off + n].reshape(lead + tuple(s)))
        off += n
    return out


def _adaln_fwd(c_all, w, b, *, name):
    nk, D, n = w.shape

    def body(c_ref, w_ref, b_ref, o_ref):
        cv = c_ref[...]
        sc = (cv * jax.nn.sigmoid(cv)).astype(BF16)
        o_ref[...] = jnp.dot(sc, w_ref[...].astype(BF16), preferred_element_type=F32) + b_ref[...]

    return pl.pallas_call(body, name=name, grid=(nk,),
                          in_specs=[pl.BlockSpec((N_DEV, D), lambda k: (0, 0)), pl.BlockSpec((None, D, n), lambda k: (k, 0, 0)),
                                    pl.BlockSpec((None, 1, n), lambda k: (k, 0, 0))],
                          out_specs=pl.BlockSpec((None, N_DEV, n), lambda k: (k, 0, 0)),
                          out_shape=jax.ShapeDtypeStruct((nk, N_DEV, n), F32), compiler_params=_cp("parallel"))(c_all, w, b)


def _adaln_bwd(c_all, dmod, *, name):
    nk, _, n = dmod.shape
    D = c_all.shape[1]

    def body(c_ref, d_ref, dw_ref, db_ref):
        cv = c_ref[...]
        sc = (cv * jax.nn.sigmoid(cv)).astype(BF16).astype(F32)
        dm = d_ref[...]
        dw_ref[...] = lax.dot_general(sc, dm.astype(BF16).astype(F32), TN_DIMS, precision=lax.Precision.HIGHEST,
                                      preferred_element_type=F32)
        db_ref[...] = jnp.sum(dm, axis=0, keepdims=True)

    return pl.pallas_call(body, name=name, grid=(nk,),
                          in_specs=[pl.BlockSpec((N_DEV, D), lambda k: (0, 0)), pl.BlockSpec((None, N_DEV, n), lambda k: (k, 0, 0))],
                          out_specs=[pl.BlockSpec((None, D, n), lambda k: (k, 0, 0)), pl.BlockSpec((None, 1, n), lambda k: (k, 0, 0))],
                          out_shape=[jax.ShapeDtypeStruct((nk, D, n), F32), jax.ShapeDtypeStruct((nk, 1, n), F32)],
                          compiler_params=_cp("parallel"))(c_all, dmod)


def _adamw(w, g, m, v, *, name):
    R, C = w.shape
    tr = _tile(R, 256, 8)
    blk = pl.BlockSpec((tr, C), lambda i: (i, 0))
    c1 = 1.0 / (1.0 - ADAM_B1 ** ADAM_STEP)
    c2 = 1.0 / (1.0 - ADAM_B2 ** ADAM_STEP)

    def body(w_ref, g_ref, m_ref, v_ref, d_ref, nm_ref, nv_ref):
        gv = g_ref[...]
        nm = ADAM_B1 * m_ref[...] + (1.0 - ADAM_B1) * gv
        nv = ADAM_B2 * v_ref[...] + (1.0 - ADAM_B2) * (gv * gv)
        nm_ref[...] = nm
        nv_ref[...] = nv
        d_ref[...] = -ADAM_LR * ((nm * c1) / (jnp.sqrt(nv * c2) + ADAM_EPS) + ADAM_WD * w_ref[...])

    sh = jax.ShapeDtypeStruct((R, C), F32)
    return pl.pallas_call(body, name=name, grid=(R // tr,), in_specs=[blk] * 4, out_specs=[blk] * 3, out_shape=[sh] * 3,
                          compiler_params=_cp("parallel"))(w, g, m, v)


def _adamw_nd(w, g, m, v, *, name):
    shp = w.shape
    two = (-1, shp[-1]) if w.ndim > 1 else (1, -1)
    outs = _adamw(w.reshape(two), g.reshape(two), m.reshape(two), v.reshape(two), name=name)
    return [o.reshape(shp) for o in outs]


S5_TILE = 256
ATTN_BLOCK = 1024
BIG = ("s5_w_in", "s5_w_glu", "s5_w_out", "fox_w_in", "fox_w_out", "ffn_w_up", "ffn_w_down")
WEIGHTS = ("norm_g", "ada_w", "ada_b", "s5_w_in", "s5_lam_re", "s5_lam_im", "s5_log_dt", "s5_b_re", "s5_b_im", "s5_c_re",
           "s5_c_im", "s5_d", "s5_w_glu", "s5_w_out", "fox_w_in", "fox_b_f", "fox_w_out", "ffn_w_up", "ffn_conv_w",
           "ffn_conv_b", "ffn_w_down", "final_g")
REPLICATED = ("s5_lam_re", "s5_lam_im", "s5_log_dt", "s5_b_re", "s5_b_im", "s5_c_re", "s5_c_im", "s5_d", "fox_b_f",
              "ffn_conv_b", "final_g")


def _row_group(arrs):
    return jnp.concatenate([a.reshape(-1, D_MODEL) for a in arrs], axis=0)


def _cols_join(g):
    return jnp.concatenate([g[j] for j in range(N_DEV)], axis=1)


def _cols_split(full):
    n = full.shape[1] // N_DEV
    return jnp.stack([full[:, j * n:(j + 1) * n] for j in range(N_DEV)])


def kernel(x, c, norm_g, ada_w, ada_b, s5_w_in, s5_lam_re, s5_lam_im, s5_log_dt, s5_b_re, s5_b_im, s5_c_re, s5_c_im, s5_d, s5_w_glu, s5_w_out, fox_w_in, fox_b_f, fox_w_out, ffn_w_up, ffn_conv_w, ffn_conv_b, ffn_w_down, final_g, loss_target, m_norm_g, m_ada_w, m_ada_b, m_s5_w_in, m_s5_lam_re, m_s5_lam_im, m_s5_log_dt, m_s5_b_re, m_s5_b_im, m_s5_c_re, m_s5_c_im, m_s5_d, m_s5_w_glu, m_s5_w_out, m_fox_w_in, m_fox_b_f, m_fox_w_out, m_ffn_w_up, m_ffn_conv_w, m_ffn_conv_b, m_ffn_w_down, m_final_g, v_norm_g, v_ada_w, v_ada_b, v_s5_w_in, v_s5_lam_re, v_s5_lam_im, v_s5_log_dt, v_s5_b_re, v_s5_b_im, v_s5_c_re, v_s5_c_im, v_s5_d, v_s5_w_glu, v_s5_w_out, v_fox_w_in, v_fox_b_f, v_fox_w_out, v_ffn_w_up, v_ffn_conv_w, v_ffn_conv_b, v_ffn_w_down, v_final_g):
    args = dict(locals())
    W = {n: args[n] for n in WEIGHTS}
    M = {n: args["m_" + n] for n in WEIGHTS}
    V = {n: args["v_" + n] for n in WEIGHTS}
    D, F = D_MODEL, D_FF
    me = _my_index()
    h0 = x[0]
    S = h0.shape[0]
    lt = min(S5_TILE, S)
    tb = min(ATTN_BLOCK, S)

    g0 = _exchange(_pack([c, norm_g, ffn_conv_w], F32, 8), gather=True, name="gather_small_in")
    c_all, ng_all, cw_all = _unpack(g0, [(D,), norm_g.shape, ffn_conv_w.shape])
    ng_full = ng_all.transpose(1, 2, 0, 3).reshape(2, 2, D)
    cw_full = cw_all.transpose(1, 2, 0, 3).reshape(2, 3, F)

    ncol = ada_w.shape[-1]
    modp = _adaln_fwd(c_all, ada_w.reshape(4, D, ncol), ada_b.reshape(4, 1, ncol), name="adaln_fwd")
    g1 = _exchange(_pack([modp], F32, 8), gather=True, name="gather_adaln")
    (mod_all,) = _unpack(g1, [modp.shape])
    mod = lax.dynamic_index_in_dim(mod_all, me, axis=2, keepdims=False).transpose(1, 0, 2).reshape(4, 3 * D)
    shift = [mod[k:k + 1, :D] for k in range(4)]
    scale = [mod[k:k + 1, D:2 * D] for k in range(4)]
    gate = [mod[k:k + 1, 2 * D:] for k in range(4)]
    gain = [_row(ng_full[k // 2, k % 2]) for k in range(4)]

    b16 = lambda a: a.astype(BF16)
    rows = D // N_DEV
    full = {"s5_w_in": _exchange(b16(s5_w_in[0]), gather=True, name="gather_s5_w_in").reshape(D, D)}
    bf_pad = jnp.pad(fox_b_f, ((0, 0), (0, 128 - HEADS)))

    def ffn_fwd(h, k, layer, comm=()):
        hn = _modulate(h, gain[k], shift[k], scale[k], name=f"modulate{k}")
        up = _mm(hn, full[f"ffn_w_up{layer}"], name=f"ffn_up{layer}", o_h=True, out_dtype=BF16, tn=1408, comm=comm)
        up, exchanged = up if comm else (up, ())
        z = _conv_gate_fwd(up, cw_full[layer], ffn_conv_b[layer:layer + 1], name=f"conv_gate{layer}")
        m, h_out = _mm(z, full[f"ffn_w_down{layer}"], name=f"ffn_down{layer}", res=h, gate=gate[k])
        return h_out, (hn, up, z, m), exchanged

    lb_re, lb_im, bb_re, bb_im = _s5_disc(s5_lam_re[0], s5_lam_im[0], s5_log_dt[0], s5_b_re[0], s5_b_im[0])
    wb, wc, lam = _s5_pack(lb_re, lb_im, bb_re, bb_im, s5_c_re[0], s5_c_im[0])
    hn0 = _modulate(h0, gain[0], shift[0], scale[0], name="modulate0")
    u = _mm(hn0, full["s5_w_in"], name="s5_in", out_dtype=BF16)
    (ypre, yg, st), (g_up0, g_rows0) = _s5_fwd(
        u, wb, wc, lam, s5_d, lt=lt, name="s5_scan",
        comm=[(b16(ffn_w_up[0]), True), (_row_group([b16(s5_w_glu), b16(s5_w_out), b16(ffn_w_down[0])]), True)])
    full["ffn_w_up0"] = _cols_join(g_up0)
    full["s5_w_glu"] = g_rows0[:, :rows].reshape(D, D)
    full["s5_w_out"] = g_rows0[:, rows:2 * rows].reshape(D, D)
    full["ffn_w_down0"] = g_rows0[:, 2 * rows:].reshape(F, D)
    gl = _mm(yg, full["s5_w_glu"], name="s5_glu", out_dtype=BF16)
    y2 = _glu_fwd(yg, gl, name="s5_glu_gate")
    m0, h1 = _mm(y2, full["s5_w_out"], name="s5_out", res=h0, gate=gate[0])
    h2, (hn1, up0, z0, m1), (g_fox, g_fox_out) = ffn_fwd(h1, 1, 0, comm=[(b16(fox_w_in[0]), True), (b16(fox_w_out[0]), True)])
    full["fox_w_in"] = _cols_join(g_fox)
    full["fox_w_out"] = g_fox_out.reshape(D, D)
    w_proj = jnp.pad(full["fox_w_in"], ((0, 0), (0, 3 * D + 128 - full["fox_w_in"].shape[1])))
    w_qkv, w_f = w_proj[:, :3 * D], w_proj[:, 3 * D:]

    hn2 = _modulate(h2, gain[2], shift[2], scale[2], name="modulate2")
    nb = S // tb
    w_qkv_s = jnp.concatenate([w_qkv[:, :D] * Q_SCALE, w_qkv[:, D:]], axis=1)
    qkv, qkvT = _mm(hn2, w_qkv_s, name="fox_qkv", out_dtype=BF16, with_t=True)
    fl = _mm(hn2, w_f, name="fox_f", out_dtype=F32)
    fcol = _fgate_fwd(fl, bf_pad, name="fox_fgate")
    f_heads = fcol[:, :HEADS].T.reshape(HP, 2, S)
    ka = _fox_keys(qkv, fcol, name="fox_keys")
    vtb = _blocked(qkvT[2 * D:].reshape(HP, 128, S), nb, tb)
    aug = _aug_rows(f_heads)
    (oT, lse), (g_up1, g_down1) = _foxt_fwd(qkvT, ka, vtb, aug, tb=tb, name="fox_attn",
                                            comm=[(b16(ffn_w_up[1]), True), (b16(ffn_w_down[1]), True)])
    full["ffn_w_up1"] = _cols_join(g_up1)
    full["ffn_w_down1"] = g_down1.reshape(F, D)
    m2, h3 = _mm(oT, full["fox_w_out"], name="fox_out", ta=True, res=h2, gate=gate[2])
    h4, (hn3, up1, z1, m3), _ = ffn_fwd(h3, 3, 1)

    lblk, dh, head_sums, dm = _loss_head(h4, _row(final_g), loss_target[0], m3, gate[3], name="loss_head")
    d_final_g, dgate = head_sums[0:1], head_sums[1:2]
    loss = lax.psum(lblk[0, 0], ("x", "y", "c"))

    G = {}
    dmod = [None] * 4
    mixer_out = [m0, m1, m2, m3]

    def norm_bwd(h, dhn, dh_in, k, dgate_k):
        if k == 0:
            dh_out, sums = _modulate_bwd(h, dhn, dh_in, gain[k], scale[k], name=f"modulate_bwd{k}")
            dm_below = dgate_below = None
        else:
            dh_out, sums, dm_below = _modulate_bwd(h, dhn, dh_in, gain[k], scale[k], mixer_out[k - 1], gate[k - 1],
                                                   name=f"modulate_bwd{k}")
            dgate_below = sums[3:4]
        dmod[k] = jnp.concatenate([sums[0:1], sums[1:2], dgate_k], axis=1)
        return dh_out, sums[2], dm_below, dgate_below

    def ffn_bwd(dh_in, dm, dgate_k, h, k, layer, saved):
        hn, up, z, m = saved
        dz = _mm(dm, full[f"ffn_w_down{layer}"], name=f"ffn_down_dx{layer}", tb=True, out_dtype=BF16, tn=1408)
        dw_down = _mm(z, dm, name=f"ffn_down_dw{layer}", ta=True, tm=1408, tk=2048)
        d_up, cs = _conv_gate_bwd(up, dz, cw_full[layer], ffn_conv_b[layer:layer + 1], name=f"conv_gate_bwd{layer}")
        dhn = _mm(d_up, full[f"ffn_w_up{layer}"], name=f"ffn_up_dx{layer}", tb=True, a_h=True, tk=2816)
        dw_up = _mm(hn, d_up, name=f"ffn_up_dw{layer}", ta=True, b_h=True, tm=1024, tn=1408, tk=2048)
        dh_out, dg, dm_below, dgate_below = norm_bwd(h, dhn, dh_in, k, dgate_k)
        return dh_out, dm_below, dgate_below, dg, dw_up, dw_down, cs[0:3], cs[3]

    dh, dm, dgate, dg3, dw_up1, dw_down1, dcw1, dcb1 = ffn_bwd(dh, dm, dgate, h3, 3, 1, (hn3, up1, z1, m3))

    do, doT = _mm(dm, full["fox_w_out"], name="fox_out_dx", tb=True, out_dtype=BF16, with_t=True)
    dw_fox_out = _mm(oT, dm, name="fox_out_dw", tk=2048)
    to_rows = lambda g: b16(g).reshape(N_DEV, -1, D)
    (dqtb, dk, dv, dfk, dfq), (r_up1, r_down1) = _foxt_bwd(
        qkvT, qkv, ka, _blocked(qkvT[:D].reshape(HP, 128, S), nb, tb), _blocked(doT.reshape(HP, 128, S), nb, tb), do,
        _blocked(aug, nb, tb), _blocked(lse, nb, tb), _blocked(oT.reshape(HP, 128, S), nb, tb), tb=tb, name="fox_attn_bwd",
        comm=[(_cols_split(b16(dw_up1)), False), (to_rows(dw_down1), False)])
    dq = dqtb.transpose(1, 3, 0, 2).reshape(S, D)
    dF = dfk[:, :, :2].transpose(1, 0, 2).reshape(S, HEADS) + dfq[:, :, :2, :].transpose(1, 3, 0, 2).reshape(S, HEADS)
    dF = jnp.pad(dF, ((0, 0), (0, 128 - HEADS)))
    dfl, dbf = _fgate_bwd(dF, fl, bf_pad, name="fox_fgate_bwd")
    dproj = jnp.concatenate([dq, dk, dv, dfl.astype(BF16)], axis=1)
    dhn = _mm(dproj, w_proj, name="fox_in_dx", tb=True)
    dw_proj = _mm(hn2, dproj, name="fox_in_dw", ta=True, tn=640, tk=2048)
    dw_fox_in = dw_proj[:, :full["fox_w_in"].shape[1]]
    dh, dg2, dm, dgate = norm_bwd(h2, dhn, dh, 2, dgate)

    dh, dm, dgate, dg1, dw_up0, dw_down0, dcw0, dcb0 = ffn_bwd(dh, dm, dgate, h1, 1, 0, (hn1, up0, z0, m1))

    dy2 = _mm(dm, full["s5_w_out"], name="s5_out_dx", tb=True, out_dtype=BF16)
    G["s5_w_out"] = _mm(y2, dm, name="s5_out_dw", ta=True, tk=2048)
    da, dgl = _glu_bwd(dy2, yg, gl, name="s5_glu_bwd")
    dyg_b = _mm(dgl, full["s5_w_glu"], name="s5_glu_dx", tb=True)
    G["s5_w_glu"] = _mm(yg, dgl, name="s5_glu_dw", ta=True, tk=2048)
    dyp, dd = _gelu_bwd(da, dyg_b, ypre, u, name="s5_gelu_bwd")
    row_grads = [dw_fox_out, dw_down0, G["s5_w_out"], G["s5_w_glu"]]
    (du, dwb, dwc, dlam), (r_up0, r_rows0, r_fox) = _s5_bwd(
        u, dyp, st, wb, wc, lam, s5_d, lt=lt, name="s5_scan_bwd",
        comm=[(_cols_split(b16(dw_up0)), False), (jnp.concatenate([to_rows(g) for g in row_grads], axis=1), False),
              (_cols_split(b16(dw_fox_in)), False)])
    dhn = _mm(du, full["s5_w_in"], name="s5_in_dx", tb=True)
    G["s5_w_in"] = _mm(hn0, du, name="s5_in_dw", ta=True, tk=2048)
    dh, dg0, _, _ = norm_bwd(h0, dhn, dh, 0, dgate)
    grad_x = dh[None]

    dlb_re, dlb_im, dbb_re, dbb_im, dc_re, dc_im = _s5_unpack(dwb, dwc, dlam)
    _, disc_vjp = jax.vjp(_s5_disc, s5_lam_re[0], s5_lam_im[0], s5_log_dt[0], s5_b_re[0], s5_b_im[0])
    dlam_re, dlam_im, dlog_dt, db_re, db_im = disc_vjp((dlb_re, dlb_im, dbb_re, dbb_im))

    g2 = _exchange(_pack([jnp.concatenate(dmod, axis=0)], F32, 8), gather=True, name="gather_dmod")
    (dmod_all,) = _unpack(g2, [(4, 3 * D)])
    dmod_mine = _columns_of_mod(dmod_all, me, ncol)
    d_ada_w, d_ada_b = _adaln_bwd(c_all, dmod_mine, name="adaln_bwd")
    G["ada_w"] = d_ada_w.reshape(ada_w.shape)
    G["ada_b"] = d_ada_b.reshape(ada_b.shape)

    small_full = {
        "norm_g": jnp.stack([dg0, dg1, dg2, dg3]).reshape(2, 2, D),
        "ffn_conv_w": jnp.stack([dcw0, dcw1]),
        "s5_lam_re": dlam_re[None], "s5_lam_im": dlam_im[None], "s5_log_dt": dlog_dt[None],
        "s5_b_re": db_re[None], "s5_b_im": db_im[None], "s5_c_re": dc_re[None], "s5_c_im": dc_im[None],
        "s5_d": dd, "fox_b_f": dbf[:, :HEADS], "ffn_conv_b": jnp.stack([dcb0, dcb1]), "final_g": d_final_g[0],
    }
    names = tuple(small_full)
    small_pack = _pack([small_full[n] for n in names], F32, 8 * N_DEV)
    r_in, r_small = _exchange_many([(to_rows(G["s5_w_in"]), False), (small_pack.reshape(N_DEV, -1, 1024), False)],
                                   name="scatter_s5_in_and_small_grads")
    G["s5_w_in"] = _sum8(r_in, name="sum_s5_in_grads")[None]
    rows0_sum = _sum8(r_rows0, name="sum_rows0_grads")
    down_rows = F // N_DEV
    G["fox_w_out"] = rows0_sum[:rows][None]
    G["s5_w_out"] = rows0_sum[rows + down_rows:2 * rows + down_rows][None]
    G["s5_w_glu"] = rows0_sum[2 * rows + down_rows:][None]
    G["ffn_w_down"] = jnp.stack([rows0_sum[rows:rows + down_rows], _sum8(r_down1, name="sum_down1_grads")])
    G["ffn_w_up"] = jnp.stack([_sum8(r_up0, name="sum_up0_grads"), _sum8(r_up1, name="sum_up1_grads")])
    G["fox_w_in"] = _sum8(r_fox, name="sum_fox_w_in")[None]

    g3 = _exchange(_sum8(r_small, name="sum_small_grads"), gather=True, name="gather_small_sums")
    summed = dict(zip(names, _unpack(g3.reshape(small_pack.shape), [small_full[n].shape for n in names])))
    G["norm_g"] = lax.dynamic_slice_in_dim(summed["norm_g"], me * norm_g.shape[-1], norm_g.shape[-1], axis=2)
    G["ffn_conv_w"] = lax.dynamic_slice_in_dim(summed["ffn_conv_w"], me * ffn_conv_w.shape[-1], ffn_conv_w.shape[-1], axis=2)
    for n in REPLICATED:
        G[n] = summed[n]

    delta, new_m, new_v = {}, {}, {}
    small = tuple(n for n in WEIGHTS if n not in BIG and n != "ada_w")
    for n in WEIGHTS:
        if n not in small:
            delta[n], new_m[n], new_v[n] = _adamw_nd(W[n], G[n], M[n], V[n], name=f"adamw_{n}")
    packed = [_pack([src[n] for n in small], F32, 8) for src in (W, G, M, V)]
    for dst, buf in zip((delta, new_m, new_v), _adamw(*packed, name="adamw_small")):
        dst.update(zip(small, _unpack(buf, [W[n].shape for n in small])))

    return (loss, grad_x, *[G[n] for n in WEIGHTS], *[delta[n] for n in WEIGHTS], *[new_m[n] for n in WEIGHTS],
            *[new_v[n] for n in WEIGHTS])


def _columns_of_mod(dmod_all, me, ncol):
    flat = lax.dynamic_slice_in_dim(dmod_all, me * ncol, ncol, axis=2)
    return flat.transpose(1, 0, 2)
```

```python
import functools
import math

import jax
import jax.numpy as jnp
from jax import lax
from jax.experimental import pallas as pl
from jax.experimental.pallas import tpu as pltpu

F32, BF16 = jnp.float32, jnp.bfloat16
EPS = 1e-6
N_DEV = 8
D_MODEL = 1024
D_FF = 2816
HEADS = 16
HEAD_DIM = 64
S5_G, S5_P, S5_C = 64, 64, 16
S5_SUB = 8
V7X_VMEM_LIMIT = 56 * 1024 * 1024
NEG = -1e30
ADAM_LR, ADAM_B1, ADAM_B2, ADAM_EPS, ADAM_WD, ADAM_STEP = 1e-3, 0.9, 0.999, 1e-8, 0.01, 10
GELU_K = math.sqrt(2.0 / math.pi)
MESH_ID = pl.DeviceIdType.MESH


def _cp(*sem):
    return pltpu.CompilerParams(dimension_semantics=sem, vmem_limit_bytes=V7X_VMEM_LIMIT)


def _tile(n, target, mult=128):
    if n <= target:
        return n
    t = (target // mult) * mult
    while t >= mult:
        if n % t == 0:
            return t
        t -= mult
    return n


def _row(v):
    return v.reshape(1, -1).astype(F32)


def _mm(a, b, *, name, ta=False, tb=False, out_dtype=F32, tm=1024, tn=1024, tk=None, res=None, gate=None,
        a_h=False, b_h=False, o_h=False, with_t=False, comm=()):
    if a_h:
        M, K = a.shape[1], 2 * a.shape[2]
    elif ta:
        K, M = a.shape
    else:
        M, K = a.shape
    if b_h:
        N = 2 * b.shape[2]
    else:
        N = b.shape[0] if tb else b.shape[1]
    half_n = N // 2
    tm = _tile(M, tm, 128 if ta else 8)
    tn = _tile(half_n if (b_h or o_h) else N, tn)
    tk = K if tk is None else _tile(K // 2 if a_h else K, tk)
    nk = K // tk
    nkh, nnh = (K // 2) // tk if a_h else 1, half_n // tn
    if a_h:
        a_spec = pl.BlockSpec((None, tm, tk), lambda i, j, k: (k // nkh, i, k % nkh))
    elif ta:
        a_spec = pl.BlockSpec((tk, tm), lambda i, j, k: (k, i))
    else:
        a_spec = pl.BlockSpec((tm, tk), lambda i, j, k: (i, k))
    if b_h:
        b_spec = pl.BlockSpec((None, tk, tn), lambda i, j, k: (j // nnh, k, j % nnh))
    elif tb:
        b_spec = pl.BlockSpec((tn, tk), lambda i, j, k: (j, k))
    else:
        b_spec = pl.BlockSpec((tk, tn), lambda i, j, k: (k, j))
    if o_h:
        o_spec = pl.BlockSpec((None, tm, tn), lambda i, j, k: (j // nnh, i, j % nnh))
    else:
        o_spec = pl.BlockSpec((tm, tn), lambda i, j, k: (i, j))
    dn = (((0 if ta else 1,), (1 if tb else 0,)), ((), ()))
    fused = res is not None

    def body(*refs):
        if fused:
            a_ref, b_ref, r_ref, g_ref, m_ref, o_ref, acc_ref = refs
        elif with_t:
            a_ref, b_ref, o_ref, t_ref, acc_ref = refs
        else:
            a_ref, b_ref, o_ref, acc_ref = refs
        p = lax.dot_general(a_ref[...].astype(BF16), b_ref[...].astype(BF16), dn, preferred_element_type=F32)

        def finish(acc):
            if fused:
                m_ref[...] = acc.astype(m_ref.dtype)
                o_ref[...] = r_ref[...] + g_ref[...] * acc
            else:
                o_ref[...] = acc.astype(o_ref.dtype)
                if with_t:
                    t_ref[...] = acc.T.astype(t_ref.dtype)

        if nk == 1:
            finish(p)
        else:
            k = pl.program_id(2)

            @pl.when(k == 0)
            def _():
                acc_ref[...] = p

            @pl.when(k > 0)
            def _():
                acc_ref[...] += p

            @pl.when(k == nk - 1)
            def _():
                finish(acc_ref[...])

    in_specs = [a_spec, b_spec]
    args = [a, b]
    if fused:
        in_specs += [o_spec, pl.BlockSpec((1, tn), lambda i, j, k: (0, j))]
        args += [res, gate]
        out_shape = [jax.ShapeDtypeStruct((M, N), BF16), jax.ShapeDtypeStruct((M, N), F32)]
        out_specs = [o_spec, o_spec]
    else:
        out_shape = [jax.ShapeDtypeStruct((2, M, half_n) if o_h else (M, N), out_dtype)]
        out_specs = [o_spec]
        if with_t:
            out_shape.append(jax.ShapeDtypeStruct((N, M), out_dtype))
            out_specs.append(pl.BlockSpec((tn, tm), lambda i, j, k: (j, i)))
    outs, exchanged = _call(
        body, name=name, grid=(M // tm, N // tn, nk), in_specs=in_specs, out_specs=out_specs, out_shape=out_shape,
        scratch_shapes=[pltpu.VMEM((tm, tn) if nk > 1 else (8, 128), F32)], sem=("parallel", "parallel", "arbitrary"),
        args=args, comm=comm)
    outs = tuple(outs) if (fused or with_t) else outs[0]
    return (outs, exchanged) if comm else outs


def _modulate(h, g, shift, scale, *, name):
    S, D = h.shape
    tm = _tile(S, 512, 8)
    vec = pl.BlockSpec((1, D), lambda i: (0, 0))
    blk = pl.BlockSpec((tm, D), lambda i: (i, 0))

    def body(h_ref, g_ref, sh_ref, sc_ref, o_ref):
        x = h_ref[...]
        r = lax.rsqrt(jnp.mean(x * x, axis=-1, keepdims=True) + EPS)
        o_ref[...] = ((x * r * g_ref[...]) * (1.0 + sc_ref[...]) + sh_ref[...]).astype(o_ref.dtype)

    return pl.pallas_call(body, name=name, grid=(S // tm,), in_specs=[blk, vec, vec, vec], out_specs=blk,
                          out_shape=jax.ShapeDtypeStruct((S, D), BF16), compiler_params=_cp("parallel"))(h, g, shift, scale)


def _modulate_bwd(h, dhn, dh_in, g, scale, m_prev=None, gate_prev=None, *, name):
    S, D = h.shape
    tm = _tile(S, 512, 8)
    vec = pl.BlockSpec((1, D), lambda i: (0, 0))
    blk = pl.BlockSpec((tm, D), lambda i: (i, 0))
    sums = pl.BlockSpec((8, D), lambda i: (0, 0))
    below = m_prev is not None

    def body(h_ref, dhn_ref, dhi_ref, g_ref, sc_ref, *rest):
        dh_ref, s_ref = rest[-3:-1] if below else rest[-2:]

        @pl.when(pl.program_id(0) == 0)
        def _():
            s_ref[...] = jnp.zeros_like(s_ref)

        x = h_ref[...]
        r = lax.rsqrt(jnp.mean(x * x, axis=-1, keepdims=True) + EPS)
        xhat = x * r
        gv = g_ref[...]
        dhn_v = dhn_ref[...].astype(F32)
        dn = dhn_v * (1.0 + sc_ref[...])
        s_ref[0:1, :] += jnp.sum(dhn_v, axis=0, keepdims=True)
        s_ref[1:2, :] += jnp.sum(dhn_v * (xhat * gv), axis=0, keepdims=True)
        s_ref[2:3, :] += jnp.sum(dn * xhat, axis=0, keepdims=True)
        dxh = dn * gv
        dh = dhi_ref[...] + r * (dxh - xhat * jnp.mean(dxh * xhat, axis=-1, keepdims=True))
        dh_ref[...] = dh
        if below:
            mp_ref, gp_ref, dm_ref = rest[0], rest[1], rest[-1]
            dm_ref[...] = (dh * gp_ref[...]).astype(dm_ref.dtype)
            s_ref[3:4, :] += jnp.sum(dh * mp_ref[...].astype(F32), axis=0, keepdims=True)

    extra_in, extra_args = ([blk, vec], [m_prev, gate_prev]) if below else ([], [])
    return pl.pallas_call(body, name=name, grid=(S // tm,), in_specs=[blk, blk, blk, vec, vec] + extra_in,
                          out_specs=[blk, sums] + ([blk] if below else []),
                          out_shape=[jax.ShapeDtypeStruct((S, D), F32), jax.ShapeDtypeStruct((8, D), F32)]
                          + ([jax.ShapeDtypeStruct((S, D), BF16)] if below else []),
                          compiler_params=_cp("arbitrary"))(h, dhn, dh_in, g, scale, *extra_args)


def _loss_head(h, g, tgt, m_prev, gate_prev, *, name):
    S, D = h.shape
    tm = _tile(S, 512, 8)
    vec = pl.BlockSpec((1, D), lambda i: (0, 0))
    blk = pl.BlockSpec((tm, D), lambda i: (i, 0))
    lss = pl.BlockSpec((8, 128), lambda i: (0, 0))
    sums = pl.BlockSpec((8, D), lambda i: (0, 0))

    def body(h_ref, g_ref, t_ref, mp_ref, gp_ref, l_ref, dh_ref, s_ref, dm_ref):
        @pl.when(pl.program_id(0) == 0)
        def _():
            l_ref[...] = jnp.zeros_like(l_ref)
            s_ref[...] = jnp.zeros_like(s_ref)

        x = h_ref[...]
        r = lax.rsqrt(jnp.mean(x * x, axis=-1, keepdims=True) + EPS)
        xhat = x * r
        gv = g_ref[...]
        e = xhat * gv - t_ref[...]
        l_ref[...] += 0.5 * jnp.sum(jnp.mean(e * e, axis=-1, keepdims=True))
        dy = e * (1.0 / D)
        s_ref[0:1, :] += jnp.sum(dy * xhat, axis=0, keepdims=True)
        dxh = dy * gv
        dh = r * (dxh - xhat * jnp.mean(dxh * xhat, axis=-1, keepdims=True))
        dh_ref[...] = dh
        dm_ref[...] = (dh * gp_ref[...]).astype(dm_ref.dtype)
        s_ref[1:2, :] += jnp.sum(dh * mp_ref[...].astype(F32), axis=0, keepdims=True)

    return pl.pallas_call(body, name=name, grid=(S // tm,), in_specs=[blk, vec, blk, blk, vec], out_specs=[lss, blk, sums, blk],
                          out_shape=[jax.ShapeDtypeStruct((8, 128), F32), jax.ShapeDtypeStruct((S, D), F32),
                                     jax.ShapeDtypeStruct((8, D), F32), jax.ShapeDtypeStruct((S, D), BF16)],
                          compiler_params=_cp("arbitrary"))(h, g, tgt, m_prev, gate_prev)


def _shift_down(x, k, edge):
    tm = x.shape[0]
    rows = lax.broadcasted_iota(jnp.int32, x.shape, 0)
    out = pltpu.roll(x, k, 0)
    for j in range(k):
        out = jnp.where(rows == j, edge[8 - k + j:8 - k + j + 1, :], out)
    return out


def _shift_up(x, k, edge):
    tm = x.shape[0]
    rows = lax.broadcasted_iota(jnp.int32, x.shape, 0)
    out = pltpu.roll(x, tm - k, 0)
    for j in range(k):
        out = jnp.where(rows == tm - k + j, edge[j:j + 1, :], out)
    return out


def _conv_gate_fwd(up, cw, cb, *, name):
    _, S, F = up.shape
    tf = _tile(F, 1408)
    nf = F // tf
    tm = _tile(S, 512, 8)

    def body(a_ref, b_ref, w_ref, cb_ref, z_ref, edge_ref):
        @pl.when(pl.program_id(1) == 0)
        def _():
            edge_ref[...] = jnp.zeros_like(edge_ref)

        a = a_ref[...].astype(F32)
        edge = edge_ref[...]
        w = w_ref[...]
        ac = cb_ref[...] + w[2:3, :] * a + w[1:2, :] * _shift_down(a, 1, edge) + w[0:1, :] * _shift_down(a, 2, edge)
        edge_ref[...] = a[tm - 8:tm, :]
        z_ref[...] = (ac * jax.nn.sigmoid(ac) * b_ref[...].astype(F32)).astype(z_ref.dtype)

    return pl.pallas_call(
        body, name=name, grid=(nf, S // tm),
        in_specs=[pl.BlockSpec((None, tm, tf), lambda j, i: (0, i, j)), pl.BlockSpec((None, tm, tf), lambda j, i: (1, i, j)),
                  pl.BlockSpec((3, tf), lambda j, i: (0, j)), pl.BlockSpec((1, tf), lambda j, i: (0, j))],
        out_specs=pl.BlockSpec((tm, tf), lambda j, i: (i, j)),
        out_shape=jax.ShapeDtypeStruct((S, F), BF16), scratch_shapes=[pltpu.VMEM((8, tf), F32)],
        compiler_params=_cp("parallel", "arbitrary"))(up, up, cw, cb)


def _conv_gate_bwd(up, dz, cw, cb, *, name):
    _, S, F = up.shape
    tf = _tile(F, 1408)
    nf = F // tf
    tm = _tile(S, 512, 8)
    nt = S // tm
    hb = tm // 8

    def body(a_ref, ah_ref, b_ref, dz_ref, w_ref, cb_ref, d_ref, s_ref, edge_ref):
        i = pl.program_id(1)

        @pl.when(i == 0)
        def _():
            edge_ref[...] = jnp.zeros_like(edge_ref)
            s_ref[...] = jnp.zeros_like(s_ref)

        a = a_ref[...].astype(F32)
        halo = jnp.where(i == nt - 1, 0.0, ah_ref[...].astype(F32))
        w = w_ref[...]
        a1 = _shift_down(a, 1, halo)
        a2 = _shift_down(a, 2, halo)
        ac = cb_ref[...] + w[2:3, :] * a + w[1:2, :] * a1 + w[0:1, :] * a2
        sg = jax.nn.sigmoid(ac)
        dzv = dz_ref[...].astype(F32)
        si = ac * sg
        d_ref[1] = (dzv * si).astype(d_ref.dtype)
        dac = (dzv * b_ref[...].astype(F32)) * (sg + si * (1.0 - sg))
        s_ref[0:1, :] += jnp.sum(dac * a2, axis=0, keepdims=True)
        s_ref[1:2, :] += jnp.sum(dac * a1, axis=0, keepdims=True)
        s_ref[2:3, :] += jnp.sum(dac * a, axis=0, keepdims=True)
        s_ref[3:4, :] += jnp.sum(dac, axis=0, keepdims=True)
        edge = edge_ref[...]
        da = w[2:3, :] * dac + w[1:2, :] * _shift_up(dac, 1, edge) + w[0:1, :] * _shift_up(dac, 2, edge)
        edge_ref[...] = dac[0:8, :]
        d_ref[0] = da.astype(d_ref.dtype)

    tile = lambda hlf: pl.BlockSpec((None, tm, tf), lambda j, i: (hlf, nt - 1 - i, j))
    d_up, sums = pl.pallas_call(
        body, name=name, grid=(nf, nt),
        in_specs=[tile(0),
                  pl.BlockSpec((None, 8, tf), lambda j, i: (0, jnp.maximum((nt - 1 - i) * hb - 1, 0), j)),
                  tile(1), pl.BlockSpec((tm, tf), lambda j, i: (nt - 1 - i, j)),
                  pl.BlockSpec((3, tf), lambda j, i: (0, j)), pl.BlockSpec((1, tf), lambda j, i: (0, j))],
        out_specs=[pl.BlockSpec((2, tm, tf), lambda j, i: (0, nt - 1 - i, j)), pl.BlockSpec((8, tf), lambda j, i: (0, j))],
        out_shape=[jax.ShapeDtypeStruct((2, S, F), BF16), jax.ShapeDtypeStruct((8, F), F32)],
        scratch_shapes=[pltpu.VMEM((8, tf), F32)],
        compiler_params=_cp("parallel", "arbitrary"))(up, up, up, dz, cw, cb)
    return d_up, sums


def _log_sigmoid(x):
    return jnp.minimum(x, 0.0) - jnp.log(1.0 + jnp.exp(-jnp.abs(x)))


def _tri_ones(n, upper):
    r = lax.broadcasted_iota(jnp.int32, (n, n), 0)
    c = lax.broadcasted_iota(jnp.int32, (n, n), 1)
    return jnp.where((c >= r) if upper else (c <= r), 1.0, 0.0).astype(F32)


def _fgate_fwd(fl, bf, *, name):
    S, W = fl.shape
    tb = _tile(S, 256, 8)

    def body(fl_ref, b_ref, o_ref, carry_ref):
        @pl.when(pl.program_id(0) == 0)
        def _():
            carry_ref[...] = jnp.zeros_like(carry_ref)

        lf = _log_sigmoid(fl_ref[...] + b_ref[...])
        cs = jnp.dot(_tri_ones(tb, False), lf, precision=lax.Precision.HIGHEST, preferred_element_type=F32) + carry_ref[0:1, :]
        o_ref[...] = cs
        carry_ref[...] = jnp.broadcast_to(cs[tb - 1:tb, :], carry_ref.shape)

    blk = pl.BlockSpec((tb, W), lambda i: (i, 0))
    return pl.pallas_call(body, name=name, grid=(S // tb,), in_specs=[blk, pl.BlockSpec((1, W), lambda i: (0, 0))], out_specs=blk,
                          out_shape=jax.ShapeDtypeStruct((S, W), F32), scratch_shapes=[pltpu.VMEM((8, W), F32)],
                          compiler_params=_cp("arbitrary"))(fl, bf)


def _fgate_bwd(dF, fl, bf, *, name):
    S, W = fl.shape
    tb = _tile(S, 256, 8)
    nb = S // tb

    def body(d_ref, fl_ref, b_ref, o_ref, s_ref, carry_ref):
        @pl.when(pl.program_id(0) == 0)
        def _():
            carry_ref[...] = jnp.zeros_like(carry_ref)
            s_ref[...] = jnp.zeros_like(s_ref)

        rc = jnp.dot(_tri_ones(tb, True), d_ref[...], precision=lax.Precision.HIGHEST, preferred_element_type=F32) + carry_ref[0:1, :]
        carry_ref[...] = jnp.broadcast_to(rc[0:1, :], carry_ref.shape)
        dfl = rc * jax.nn.sigmoid(-(fl_ref[...] + b_ref[...]))
        o_ref[...] = dfl
        s_ref[...] += jnp.sum(dfl, axis=0, keepdims=True)

    blk = pl.BlockSpec((tb, W), lambda i: (nb - 1 - i, 0))
    vec = pl.BlockSpec((1, W), lambda i: (0, 0))
    return pl.pallas_call(body, name=name, grid=(nb,), in_specs=[blk, blk, vec], out_specs=[blk, vec],
                          out_shape=[jax.ShapeDtypeStruct((S, W), F32), jax.ShapeDtypeStruct((1, W), F32)],
                          scratch_shapes=[pltpu.VMEM((8, W), F32)], compiler_params=_cp("arbitrary"))(dF, fl, bf)


NT_DIMS = (((1,), (1,)), ((), ()))
TN_DIMS = (((0,), (0,)), ((), ()))
HP = HEADS // 2
Q_SCALE = HEAD_DIM ** -0.5


def _head_mask(x, hh):
    lanes = lax.broadcasted_iota(jnp.int32, x.shape, 1)
    return jnp.where((lanes >= hh * HEAD_DIM) & (lanes < (hh + 1) * HEAD_DIM), x, jnp.zeros_like(x))


def _lanes_from(cols, shape):
    lanes = lax.broadcasted_iota(jnp.int32, shape, 1)
    out = jnp.zeros(shape, F32)
    for i, cvec in enumerate(cols):
        out = jnp.where(lanes == i, cvec, out)
    return out


AUG_ROWS = 16
DIAGONAL_TILES = ((0, 0, True), (0, 1, False), (1, 1, True))


def _qa_pair(qt, aug):
    fill = jnp.zeros((HEAD_DIM - AUG_ROWS, qt.shape[1]), qt.dtype)
    return [jnp.concatenate([qt[0:HEAD_DIM], aug[0], fill], axis=0), jnp.concatenate([aug[1], fill, qt[HEAD_DIM:]], axis=0)]


def _rows_of_head(xt, hh):
    rows = lax.broadcasted_iota(jnp.int32, xt.shape, 0)
    return jnp.where((rows >= hh * HEAD_DIM) & (rows < (hh + 1) * HEAD_DIM), xt, jnp.zeros_like(xt))


def _put_rows(ref, rows):
    for i, r in enumerate(rows):
        ref[i:i + 1, :] = r
    ref[len(rows):, :] = jnp.zeros((ref.shape[0] - len(rows), ref.shape[1]), ref.dtype)


def _diag_mask_t(s):
    rows = lax.broadcasted_iota(jnp.int32, s.shape, 0)
    cols = lax.broadcasted_iota(jnp.int32, s.shape, 1)
    return jnp.where(rows <= cols, s, NEG)


def _foxt_fwd(qkvT, ka, vtb, augq, *, tb, name, comm=()):
    S = qkvT.shape[1]
    nb = S // tb

    def body(q_ref, ka_ref, v_ref, aq_ref, o_ref, lse_ref, m0, m1, l0, l1, acc0, acc1):
        qi = pl.program_id(1)
        qa = _qa_pair(q_ref[...], aq_ref[...])
        half = tb // 2
        state = ((m0, l0, acc0), (m1, l1, acc1))
        for m_sc, l_sc, acc_sc in state:
            m_sc[...] = jnp.full_like(m_sc, NEG)
            l_sc[...] = jnp.zeros_like(l_sc)
            acc_sc[...] = jnp.zeros_like(acc_sc)

        def tile(kj, ko, qo, n, masked):
            k0 = pl.multiple_of(kj * tb + ko, n)
            cols = slice(qo, qo + n)
            logits = [jnp.dot(ka_ref[hh, pl.ds(k0, n), :], qa[hh][:, cols], preferred_element_type=F32) for hh in range(2)]
            updates = []
            for hh, (m_sc, l_sc, acc_sc) in enumerate(state):
                s = _diag_mask_t(logits[hh]) if masked else logits[hh]
                m_old = m_sc[:, cols]
                m_new = jnp.maximum(m_old, jnp.max(s, axis=0, keepdims=True))
                alpha = jnp.exp(m_old - m_new)
                p = jnp.exp(s - m_new)
                l_sc[:, cols] = alpha * l_sc[:, cols] + jnp.sum(p, axis=0, keepdims=True)
                m_sc[:, cols] = m_new
                vt = v_ref[kj, HEAD_DIM * hh:HEAD_DIM * (hh + 1), ko:ko + n]
                updates.append((alpha, jnp.dot(vt, p.astype(BF16), preferred_element_type=F32)))
            for (alpha, pv), (_, _, acc_sc) in zip(updates, state):
                acc_sc[:, cols] = alpha * acc_sc[:, cols] + pv

        def off_diagonal(kj, c):
            tile(kj, 0, 0, tb, False)
            return c

        lax.fori_loop(0, qi, off_diagonal, 0)
        for ko, qo, masked in DIAGONAL_TILES:
            tile(qi, ko * half, qo * half, half, masked)
        for hh, (m_sc, l_sc, acc_sc) in enumerate(state):
            o_ref[HEAD_DIM * hh:HEAD_DIM * (hh + 1), :] = (acc_sc[...] / l_sc[...]).astype(o_ref.dtype)
        _put_rows(lse_ref, [m_sc[...] + jnp.log(l_sc[...]) for m_sc, l_sc, _ in state])

    return _call(
        body, name=name, grid=(HP, nb), comm=comm, sem=("parallel", "parallel"), args=(qkvT, ka, vtb, augq),
        in_specs=[pl.BlockSpec((128, tb), lambda hp, qi: (hp, qi)),
                  pl.BlockSpec((None, 2, S, 128), lambda hp, qi: (hp, 0, 0, 0)),
                  pl.BlockSpec((None, nb, 128, tb), lambda hp, qi: (hp, 0, 0, 0)),
                  pl.BlockSpec((None, 2, AUG_ROWS, tb), lambda hp, qi: (hp, 0, 0, qi))],
        out_specs=[pl.BlockSpec((128, tb), lambda hp, qi: (hp, qi)), pl.BlockSpec((None, 8, tb), lambda hp, qi: (hp, 0, qi))],
        out_shape=[jax.ShapeDtypeStruct((D_MODEL, S), BF16), jax.ShapeDtypeStruct((HP, 8, S), F32)],
        scratch_shapes=[pltpu.VMEM((1, tb), F32)] * 4 + [pltpu.VMEM((HEAD_DIM, tb), F32)] * 2)


def _foxt_bwd(qkvT, qkv, ka, qtb, dotb, do, augqb, lseb, otb, *, tb, name, comm=()):
    S = qkv.shape[0]
    nb = S // tb

    def body(ka_ref, v_ref, kt_ref, qt_ref, dot_ref, qn_ref, dn_ref, aq_ref, ls_ref, ot_ref,
             dq_ref, dk_ref, dv_ref, df_ref, dr_ref, dq_sc, dk_sc, dv_sc, ds_sc, dr_sc):
        kj = pl.program_id(1)

        @pl.when(kj == 0)
        def _():
            dq_sc[...] = jnp.zeros_like(dq_sc)
            dr_sc[...] = jnp.zeros_like(dr_sc)

        dk_sc[...] = jnp.zeros_like(dk_sc)
        dv_sc[...] = jnp.zeros_like(dv_sc)
        ds_sc[...] = jnp.zeros_like(ds_sc)
        v2 = v_ref[...]
        kth = [kt_ref[HEAD_DIM * hh:HEAD_DIM * (hh + 1), :] for hh in range(2)]

        def block(qi, tiles):
            q0 = pl.multiple_of(qi * tb, tb)
            qa = _qa_pair(qt_ref[qi], aq_ref[qi])
            dot, ls = dot_ref[qi], ls_ref[qi]
            prod = ot_ref[qi].astype(F32) * dot.astype(F32)
            delta = [jnp.sum(prod[HEAD_DIM * hh:HEAD_DIM * (hh + 1)], axis=0, keepdims=True) for hh in range(2)]
            qn, dn = qn_ref[pl.ds(q0, tb), :], dn_ref[pl.ds(q0, tb), :]
            for ko, qo, n, masked in tiles:
                keys, cols = slice(ko, ko + n), slice(qo, qo + n)
                for hh in range(2):
                    s = jnp.dot(ka_ref[hh, keys, :], qa[hh][:, cols], preferred_element_type=F32)
                    if masked:
                        s = _diag_mask_t(s)
                    p = jnp.exp(s - ls[hh:hh + 1, cols])
                    dp = jnp.dot(v2[keys], _rows_of_head(dot[:, cols], hh), preferred_element_type=F32)
                    ds = p * (dp - delta[hh][:, cols])
                    dsb = ds.astype(BF16)
                    dv_sc[keys, :] += jnp.dot(p.astype(BF16), _head_mask(dn[cols], hh), preferred_element_type=F32)
                    dk_sc[keys, :] += jnp.dot(dsb, _head_mask(qn[cols], hh), preferred_element_type=F32)
                    dq_sc[qi, HEAD_DIM * hh:HEAD_DIM * (hh + 1), cols] += jnp.dot(kth[hh][:, keys], dsb, preferred_element_type=F32)
                    part = ds[:, 0:128]
                    for j in range(1, n // 128):
                        part = part + ds[:, 128 * j:128 * (j + 1)]
                    ds_sc[hh, keys, :] += part
                    dr_sc[qi, hh:hh + 1, cols] += jnp.sum(ds, axis=0, keepdims=True)

        def off_diagonal(i, c):
            block(kj + 1 + i, [(0, 0, tb, False)])
            return c

        half = tb // 2
        block(kj, [(ko * half, qo * half, half, masked) for ko, qo, masked in DIAGONAL_TILES])
        lax.fori_loop(0, nb - 1 - kj, off_diagonal, 0)
        dk_ref[...] = dk_sc[...].astype(dk_ref.dtype)
        dv_ref[...] = dv_sc[...].astype(dv_ref.dtype)
        df_ref[...] = _lanes_from([-jnp.sum(ds_sc[hh], axis=1, keepdims=True) for hh in range(2)], (tb, 128))

        @pl.when(kj == nb - 1)
        def _():
            dq_ref[...] = (dq_sc[...] * Q_SCALE).astype(dq_ref.dtype)
            dr_ref[...] = dr_sc[...]

    resident = lambda rows: pl.BlockSpec((None, nb) + rows, lambda hp, kj: (hp,) + (0,) * (len(rows) + 1))
    whole = pl.BlockSpec((S, 128), lambda hp, kj: (0, hp))
    kblk = lambda off: pl.BlockSpec((tb, 128), lambda hp, kj: (kj, off + hp))
    return _call(
        body, name=name, grid=(HP, nb), comm=comm, sem=("parallel", "arbitrary"),
        args=(ka, qkv, qkvT, qtb, dotb, qkv, do, augqb, lseb, otb),
        in_specs=[pl.BlockSpec((None, 2, tb, 128), lambda hp, kj: (hp, 0, kj, 0)), kblk(2 * HP),
                  pl.BlockSpec((128, tb), lambda hp, kj: (HP + hp, kj)),
                  resident((128, tb)), resident((128, tb)), whole, whole, resident((2, AUG_ROWS, tb)), resident((8, tb)),
                  resident((128, tb))],
        out_specs=[resident((128, tb)), kblk(0), kblk(0), pl.BlockSpec((None, tb, 128), lambda hp, kj: (hp, kj, 0)),
                   resident((8, tb))],
        out_shape=[jax.ShapeDtypeStruct((HP, nb, 128, tb), BF16), jax.ShapeDtypeStruct((S, D_MODEL), BF16),
                   jax.ShapeDtypeStruct((S, D_MODEL), BF16), jax.ShapeDtypeStruct((HP, S, 128), F32),
                   jax.ShapeDtypeStruct((HP, nb, 8, tb), F32)],
        scratch_shapes=[pltpu.VMEM((nb, 128, tb), F32), pltpu.VMEM((tb, 128), F32), pltpu.VMEM((tb, 128), F32),
                        pltpu.VMEM((2, tb, 128), F32), pltpu.VMEM((nb, 8, tb), F32)])


def _split3(x):
    rnd = lambda v: lax.reduce_precision(v, exponent_bits=8, mantissa_bits=7)
    hi = rnd(x)
    mid = rnd(x - hi)
    lo = rnd(x - hi - mid)
    return hi.astype(BF16), mid.astype(BF16), lo.astype(BF16)


def _blocked(xt, nb, tb):
    lead = xt.shape[:-1]
    x = xt.reshape(lead + (nb, tb))
    return jnp.moveaxis(x, -2, 1)


def _aug_rows(bias):
    ones = jnp.ones(bias.shape, BF16)
    zeros = jnp.zeros(bias.shape, BF16)
    return jnp.stack(list(_split3(bias)) + [ones] * 3 + [zeros] * (AUG_ROWS - 6), axis=2)


def _fox_keys(qkv, fcol, *, name):
    S = qkv.shape[0]
    tr = _tile(S, 2048, 16)

    def body(k_ref, f_ref, o_ref):
        hp = pl.program_id(1)
        k2 = k_ref[...]
        lanes = lax.broadcasted_iota(jnp.int32, (tr, 128), 1)
        for hh in range(2):
            nf = -jnp.sum(jnp.where(lanes == 2 * hp + hh, f_ref[...], 0.0), axis=1, keepdims=True)
            hi = nf.astype(BF16).astype(F32)
            mid = (nf - hi).astype(BF16).astype(F32)
            lo = nf - hi - mid
            base = HEAD_DIM * (1 - hh)
            aug = jnp.where((lanes >= base) & (lanes < base + 3), 1.0, 0.0)
            for j, piece in enumerate((hi, mid, lo)):
                aug = jnp.where(lanes == base + 3 + j, piece, aug)
            mine = (lanes >= HEAD_DIM * hh) & (lanes < HEAD_DIM * (hh + 1))
            o_ref[hh] = jnp.where(mine, k2, aug.astype(BF16))

    return pl.pallas_call(
        body, name=name, grid=(S // tr, HP),
        in_specs=[pl.BlockSpec((tr, 128), lambda i, hp: (i, HP + hp)), pl.BlockSpec((tr, 128), lambda i, hp: (i, 0))],
        out_specs=pl.BlockSpec((None, 2, tr, 128), lambda i, hp: (hp, 0, i, 0)),
        out_shape=jax.ShapeDtypeStruct((HP, 2, S, 128), BF16), compiler_params=_cp("parallel", "parallel"))(qkv, fcol)


HALF = S5_SUB * S5_P
NCB = 2 * HALF // 128
RE, IM = slice(0, NCB // 2), slice(NCB // 2, NCB)


def _gelu(x):
    return 0.5 * x * (1.0 + jnp.tanh(GELU_K * (x + 0.044715 * x * x * x)))


def _gelu_grad(x):
    t = jnp.tanh(GELU_K * (x + 0.044715 * x * x * x))
    return 0.5 * (1.0 + t) + 0.5 * x * (1.0 - t * t) * GELU_K * (1.0 + 3.0 * 0.044715 * x * x)


def _s5_put(buf, s, val, lt):
    for cb in range(NCB):
        buf[cb, pl.ds(s, lt, stride=S5_SUB), :] = val[:, 128 * cb:128 * (cb + 1)]


def _s5_get(buf, s, lt):
    return jnp.concatenate([buf[cb, pl.ds(s, lt, stride=S5_SUB), :] for cb in range(NCB)], axis=1)


def _s5_project_in(u_ref, wb_ref, buf, lt):
    for s in range(S5_SUB):
        _s5_put(buf, s, jnp.dot(u_ref[:, 128 * s:128 * (s + 1)], wb_ref[s], preferred_element_type=F32), lt)


def _s5_scan(buf, lam_ref, h0, lt):
    a_re, a_im = lam_ref[RE], lam_ref[IM]

    def step(t, carry):
        hr, hi = carry
        r0 = pl.multiple_of(t * S5_SUB, S5_SUB)
        nr = a_re * hr - a_im * hi + buf[RE, pl.ds(r0, S5_SUB), :]
        ni = a_re * hi + a_im * hr + buf[IM, pl.ds(r0, S5_SUB), :]
        buf[RE, pl.ds(r0, S5_SUB), :] = nr
        buf[IM, pl.ds(r0, S5_SUB), :] = ni
        return nr, ni

    return lax.fori_loop(0, lt, step, (h0[RE], h0[IM]), unroll=4)


def _s5_fwd(u, wb, wc, lam, dskip, *, lt, name, comm=()):
    S, D = u.shape
    nt = S // lt

    def body(u_ref, wb_ref, wc_ref, lam_ref, d_ref, yp_ref, yg_ref, st_ref, buf, h_sc):
        @pl.when(pl.program_id(0) == 0)
        def _():
            h_sc[...] = jnp.zeros_like(h_sc)

        st_ref[...] = h_sc[...]
        _s5_project_in(u_ref, wb_ref, buf, lt)
        hr, hi = _s5_scan(buf, lam_ref, h_sc[...], lt)
        h_sc[RE] = hr
        h_sc[IM] = hi
        for s in range(S5_SUB):
            cols = slice(128 * s, 128 * (s + 1))
            hs = _s5_get(buf, s, lt).astype(BF16)
            yp = jnp.dot(hs, wc_ref[s], preferred_element_type=F32) + d_ref[:, cols] * u_ref[:, cols].astype(F32)
            yp_ref[:, cols] = yp.astype(yp_ref.dtype)
            yg_ref[:, cols] = _gelu(yp).astype(yg_ref.dtype)

    blk = pl.BlockSpec((lt, D), lambda i: (i, 0))
    full = lambda shp: pl.BlockSpec(shp, lambda i: (0,) * len(shp))
    state = (NCB, S5_SUB, 128)
    return _call(
        body, name=name, grid=(nt,), comm=comm, sem=("arbitrary",), args=(u, wb, wc, lam, dskip),
        in_specs=[blk, full(wb.shape), full(wc.shape), full(lam.shape), full(dskip.shape)],
        out_specs=[blk, blk, pl.BlockSpec((None,) + state, lambda i: (i, 0, 0, 0))],
        out_shape=[jax.ShapeDtypeStruct((S, D), BF16), jax.ShapeDtypeStruct((S, D), BF16), jax.ShapeDtypeStruct((nt,) + state, F32)],
        scratch_shapes=[pltpu.VMEM((NCB, lt * S5_SUB, 128), F32), pltpu.VMEM(state, F32)])


def _s5_bwd(u, dyp, st, wb, wc, lam, dskip, *, lt, name, comm=()):
    S, D = u.shape
    nt = S // lt

    def body(u_ref, dy_ref, st_ref, wb_ref, wc_ref, lam_ref, d_ref, du_ref, dwb_ref, dwc_ref, dlam_ref, hbuf, gbuf, g_sc):
        @pl.when(pl.program_id(0) == 0)
        def _():
            g_sc[...] = jnp.zeros_like(g_sc)
            dwb_ref[...] = jnp.zeros_like(dwb_ref)
            dwc_ref[...] = jnp.zeros_like(dwc_ref)
            dlam_ref[...] = jnp.zeros_like(dlam_ref)

        _s5_project_in(u_ref, wb_ref, hbuf, lt)
        _s5_scan(hbuf, lam_ref, st_ref[...], lt)
        for s in range(S5_SUB):
            dys = dy_ref[:, 128 * s:128 * (s + 1)]
            _s5_put(gbuf, s, lax.dot_general(dys, wc_ref[s], NT_DIMS, preferred_element_type=F32), lt)
        a_re, a_im = lam_ref[RE], lam_ref[IM]

        def one(t, carry, hp_re, hp_im):
            gr, gi, dar, dai = carry
            r0 = pl.multiple_of(t * S5_SUB, S5_SUB)
            nr = gbuf[RE, pl.ds(r0, S5_SUB), :] + a_re * gr + a_im * gi
            ni = gbuf[IM, pl.ds(r0, S5_SUB), :] + a_re * gi - a_im * gr
            gbuf[RE, pl.ds(r0, S5_SUB), :] = nr
            gbuf[IM, pl.ds(r0, S5_SUB), :] = ni
            return nr, ni, dar + nr * hp_re + ni * hp_im, dai + ni * hp_re - nr * hp_im

        def step(k, carry):
            t = lt - 1 - k
            p0 = pl.multiple_of((t - 1) * S5_SUB, S5_SUB)
            return one(t, carry, hbuf[RE, pl.ds(p0, S5_SUB), :], hbuf[IM, pl.ds(p0, S5_SUB), :])

        init = (g_sc[RE], g_sc[IM], dlam_ref[RE], dlam_ref[IM])
        carry = lax.fori_loop(0, lt - 1, step, init, unroll=2)
        gr, gi, dar, dai = one(0, carry, st_ref[RE], st_ref[IM])
        g_sc[RE] = gr
        g_sc[IM] = gi
        dlam_ref[RE] = dar
        dlam_ref[IM] = dai
        for s in range(S5_SUB):
            cols = slice(128 * s, 128 * (s + 1))
            gs = _s5_get(gbuf, s, lt).astype(BF16)
            hs = _s5_get(hbuf, s, lt).astype(BF16)
            us, dys = u_ref[:, cols], dy_ref[:, cols]
            du = lax.dot_general(gs, wb_ref[s], NT_DIMS, preferred_element_type=F32) + d_ref[:, cols] * dys.astype(F32)
            du_ref[:, cols] = du.astype(du_ref.dtype)
            dwb_ref[s] += lax.dot_general(us, gs, TN_DIMS, preferred_element_type=F32)
            dwc_ref[s] += lax.dot_general(hs, dys, TN_DIMS, preferred_element_type=F32)

    blk = pl.BlockSpec((lt, D), lambda i: (nt - 1 - i, 0))
    full = lambda shp: pl.BlockSpec(shp, lambda i: (0,) * len(shp))
    state = (NCB, S5_SUB, 128)
    words = pltpu.VMEM((NCB, lt * S5_SUB, 128), F32)
    return _call(
        body, name=name, grid=(nt,), comm=comm, sem=("arbitrary",), args=(u, dyp, st, wb, wc, lam, dskip),
        in_specs=[blk, blk, pl.BlockSpec((None,) + state, lambda i: (nt - 1 - i, 0, 0, 0)),
                  full(wb.shape), full(wc.shape), full(lam.shape), full(dskip.shape)],
        out_specs=[blk, full(wb.shape), full(wc.shape), full(lam.shape)],
        out_shape=[jax.ShapeDtypeStruct((S, D), BF16), jax.ShapeDtypeStruct(wb.shape, F32), jax.ShapeDtypeStruct(wc.shape, F32),
                   jax.ShapeDtypeStruct(lam.shape, F32)],
        scratch_shapes=[words, words, pltpu.VMEM(state, F32)])


def _s5_disc(lam_re, lam_im, log_dt, b_re, b_im):
    dt = jnp.exp(log_dt)[:, None]
    mag = jnp.exp(lam_re * dt)
    lb_re, lb_im = mag * jnp.cos(lam_im * dt), mag * jnp.sin(lam_im * dt)
    num_re = lb_re - 1.0
    den = lam_re * lam_re + lam_im * lam_im
    k_re = (num_re * lam_re + lb_im * lam_im) / den
    k_im = (lb_im * lam_re - num_re * lam_im) / den
    bb_re = k_re[..., None] * b_re - k_im[..., None] * b_im
    bb_im = k_re[..., None] * b_im + k_im[..., None] * b_re
    return lb_re, lb_im, bb_re, bb_im


def _s5_pack(lb_re, lb_im, bb_re, bb_im, c_re, c_im):
    eye = jnp.eye(S5_SUB, dtype=F32)
    tb = lambda x: jnp.einsum("sgcp,gh->sgchp", x.reshape(S5_SUB, S5_SUB, S5_P, S5_C).transpose(0, 1, 3, 2), eye).reshape(S5_SUB, 128, HALF)
    tc = lambda x: jnp.einsum("sgpc,gh->sgphc", x.reshape(S5_SUB, S5_SUB, S5_C, S5_P).transpose(0, 1, 3, 2), eye).reshape(S5_SUB, HALF, 128)
    wb = jnp.concatenate([tb(bb_re), tb(bb_im)], axis=2).astype(BF16)
    wc = jnp.concatenate([tc(c_re), -tc(c_im)], axis=1).astype(BF16)
    lam = jnp.concatenate([lb_re.reshape(S5_SUB, HALF), lb_im.reshape(S5_SUB, HALF)], axis=1)
    return wb, wc, lam.reshape(S5_SUB, NCB, 128).transpose(1, 0, 2)


def _s5_unpack(dwb, dwc, dlam):
    db = jnp.einsum("sgcrgp->rsgpc", dwb.reshape(S5_SUB, S5_SUB, S5_C, 2, S5_SUB, S5_P)).reshape(2, S5_G, S5_P, S5_C)
    dc = jnp.einsum("srgpgc->rsgcp", dwc.reshape(S5_SUB, 2, S5_SUB, S5_P, S5_SUB, S5_C)).reshape(2, S5_G, S5_C, S5_P)
    dlam = dlam.transpose(1, 0, 2).reshape(S5_SUB, 2 * HALF)
    return (dlam[:, :HALF].reshape(S5_G, S5_P), dlam[:, HALF:].reshape(S5_G, S5_P), db[0], db[1], dc[0], -dc[1])


def _glu_fwd(yg, gl, *, name):
    S, D = yg.shape
    tm = _tile(S, 512, 8)
    blk = pl.BlockSpec((tm, D), lambda i: (i, 0))

    def body(y_ref, g_ref, o_ref):
        o_ref[...] = (y_ref[...].astype(F32) * jax.nn.sigmoid(g_ref[...].astype(F32))).astype(o_ref.dtype)

    return pl.pallas_call(body, name=name, grid=(S // tm,), in_specs=[blk, blk], out_specs=blk,
                          out_shape=jax.ShapeDtypeStruct((S, D), BF16), compiler_params=_cp("parallel"))(yg, gl)


def _glu_bwd(dy2, yg, gl, *, name):
    S, D = yg.shape
    tm = _tile(S, 512, 8)
    blk = pl.BlockSpec((tm, D), lambda i: (i, 0))

    def body(d_ref, y_ref, g_ref, da_ref, dg_ref):
        d = d_ref[...].astype(F32)
        sg = jax.nn.sigmoid(g_ref[...].astype(F32))
        da_ref[...] = d * sg
        dg_ref[...] = (d * y_ref[...].astype(F32) * sg * (1.0 - sg)).astype(dg_ref.dtype)

    return pl.pallas_call(body, name=name, grid=(S // tm,), in_specs=[blk, blk, blk], out_specs=[blk, blk],
                          out_shape=[jax.ShapeDtypeStruct((S, D), F32), jax.ShapeDtypeStruct((S, D), BF16)],
                          compiler_params=_cp("parallel"))(dy2, yg, gl)


def _gelu_bwd(da, db, ypre, u, *, name):
    S, D = ypre.shape
    tm = _tile(S, 512, 8)
    blk = pl.BlockSpec((tm, D), lambda i: (i, 0))
    vec = pl.BlockSpec((1, D), lambda i: (0, 0))

    def body(a_ref, b_ref, y_ref, u_ref, o_ref, s_ref):
        @pl.when(pl.program_id(0) == 0)
        def _():
            s_ref[...] = jnp.zeros_like(s_ref)

        dy = (a_ref[...] + b_ref[...]) * _gelu_grad(y_ref[...].astype(F32))
        o_ref[...] = dy.astype(o_ref.dtype)
        s_ref[...] += jnp.sum(dy * u_ref[...].astype(F32), axis=0, keepdims=True)

    return pl.pallas_call(body, name=name, grid=(S // tm,), in_specs=[blk, blk, blk, blk], out_specs=[blk, vec],
                          out_shape=[jax.ShapeDtypeStruct((S, D), BF16), jax.ShapeDtypeStruct((1, D), F32)],
                          compiler_params=_cp("arbitrary"))(da, db, ypre, u)


def _my_index():
    return 4 * lax.axis_index("x") + 2 * lax.axis_index("y") + lax.axis_index("c")


def _comm_plan(payloads):
    n = len(payloads)
    hbm = pl.BlockSpec(memory_space=pl.ANY)
    shapes = [jax.ShapeDtypeStruct((N_DEV,) + (x.shape if g else x.shape[1:]), x.dtype) for x, g in payloads]
    sems = [pltpu.SemaphoreType.DMA((n, N_DEV - 1)), pltpu.SemaphoreType.DMA((n, N_DEV - 1)), pltpu.SemaphoreType.DMA((n,))] if n else []
    return [hbm] * n, [hbm] * n, shapes, sems


def _comm_copies(x_refs, o_refs, gathers, send_sems, recv_sems, local_sems):
    mx, my, mc = lax.axis_index("x"), lax.axis_index("y"), lax.axis_index("c")
    me = 4 * mx + 2 * my + mc
    copies = []
    for i, (x_ref, o_ref, gather) in enumerate(zip(x_refs, o_refs, gathers)):
        src = (lambda j, r=x_ref: r) if gather else (lambda j, r=x_ref: r.at[j])
        copies.append(pltpu.make_async_copy(src(me), o_ref.at[me], local_sems.at[i]))
        for k in range(1, N_DEV):
            px, py, pc = mx ^ (k >> 2), my ^ ((k >> 1) & 1), mc ^ (k & 1)
            copies.append(pltpu.make_async_remote_copy(src(4 * px + 2 * py + pc), o_ref.at[me], send_sems.at[i, k - 1],
                                                       recv_sems.at[i, k - 1], device_id=(px, py, pc), device_id_type=MESH_ID))
    return copies


def _call(body, *, name, grid, in_specs, out_specs, out_shape, scratch_shapes, sem, args, comm=()):
    if not comm:
        outs = pl.pallas_call(body, name=name, grid=grid, in_specs=in_specs, out_specs=out_specs, out_shape=out_shape,
                              scratch_shapes=scratch_shapes, compiler_params=_cp(*sem))(*args)
        return outs, ()
    n, n_in, n_out = len(comm), len(in_specs), len(out_specs)
    c_in, c_out, c_shapes, c_sems = _comm_plan(comm)
    gathers = [g for _, g in comm]

    def wrapped(*refs):
        own_in, x_refs = refs[:n_in], refs[n_in:n_in + n]
        own_out, o_refs = refs[n_in + n:n_in + n + n_out], refs[n_in + n + n_out:n_in + 2 * n + n_out]
        own_scratch, sems = refs[n_in + 2 * n + n_out:-3], refs[-3:]
        ids = [pl.program_id(a) for a in range(len(grid))]
        first = functools.reduce(jnp.logical_and, [i == 0 for i in ids])
        last = functools.reduce(jnp.logical_and, [i == g - 1 for i, g in zip(ids, grid)])

        @pl.when(first)
        def _():
            for cp in _comm_copies(x_refs, o_refs, gathers, *sems):
                cp.start()

        body(*own_in, *own_out, *own_scratch)

        @pl.when(last)
        def _():
            for cp in _comm_copies(x_refs, o_refs, gathers, *sems):
                cp.wait()

    outs = pl.pallas_call(wrapped, name=name, grid=grid, in_specs=list(in_specs) + c_in, out_specs=list(out_specs) + c_out,
                          out_shape=list(out_shape) + c_shapes, scratch_shapes=list(scratch_shapes) + c_sems,
                          compiler_params=_cp(*["arbitrary"] * len(grid)))(*args, *[x for x, _ in comm])
    return outs[:n_out], outs[n_out:]


def _exchange_many(payloads, *, name):
    n = len(payloads)
    in_specs, out_specs, shapes, sems = _comm_plan(payloads)
    gathers = [g for _, g in payloads]

    def body(*refs):
        copies = _comm_copies(refs[:n], refs[n:2 * n], gathers, *refs[2 * n:])
        for cp in copies:
            cp.start()
        for cp in copies:
            cp.wait()

    return pl.pallas_call(body, name=name, in_specs=in_specs, out_specs=out_specs, out_shape=shapes, scratch_shapes=sems,
                          compiler_params=pltpu.CompilerParams(has_side_effects=True))(*[x for x, _ in payloads])


def _exchange(x, *, gather, name):
    return _exchange_many([(x, gather)], name=name)[0]


def _sum8(x8, *, name):
    _, R, C = x8.shape
    tr = _tile(R, 256, 16)

    def body(x_ref, o_ref):
        acc = x_ref[0].astype(F32)
        for j in range(1, N_DEV):
            acc = acc + x_ref[j].astype(F32)
        o_ref[...] = acc

    return pl.pallas_call(body, name=name, grid=(R // tr,), in_specs=[pl.BlockSpec((N_DEV, tr, C), lambda i: (0, i, 0))],
                          out_specs=pl.BlockSpec((tr, C), lambda i: (i, 0)), out_shape=jax.ShapeDtypeStruct((R, C), F32),
                          compiler_params=_cp("parallel"))(x8)


def _pack(arrs, dtype, row_mult):
    flat = jnp.concatenate([a.reshape(-1).astype(dtype) for a in arrs])
    n = flat.shape[0]
    rows = -(-n // 1024)
    rows = -(-rows // row_mult) * row_mult
    return jnp.pad(flat, (0, rows * 1024 - n)).reshape(rows, 1024)


def _unpack(buf, shapes):
    lead = buf.shape[:-2]
    flat = buf.reshape(lead + (-1,))
    out, off = [], 0
    for s in shapes:
        n = math.prod(s)
        out.append(flat[..., off:off + n].reshape(lead + tuple(s)))
        off += n
    return out


def _adaln_fwd(c_all, w, b, *, name):
    nk, D, n = w.shape

    def body(c_ref, w_ref, b_ref, o_ref):
        cv = c_ref[...]
        sc = (cv * jax.nn.sigmoid(cv)).astype(BF16)
        o_ref[...] = jnp.dot(sc, w_ref[...].astype(BF16), preferred_element_type=F32) + b_ref[...]

    return pl.pallas_call(body, name=name, grid=(nk,),
                          in_specs=[pl.BlockSpec((N_DEV, D), lambda k: (0, 0)), pl.BlockSpec((None, D, n), lambda k: (k, 0, 0)),
                                    pl.BlockSpec((None, 1, n), lambda k: (k, 0, 0))],
                          out_specs=pl.BlockSpec((None, N_DEV, n), lambda k: (k, 0, 0)),
                          out_shape=jax.ShapeDtypeStruct((nk, N_DEV, n), F32), compiler_params=_cp("parallel"))(c_all, w, b)


def _adaln_bwd(c_all, dmod, *, name):
    nk, _, n = dmod.shape
    D = c_all.shape[1]

    def body(c_ref, d_ref, dw_ref, db_ref):
        cv = c_ref[...]
        sc = (cv * jax.nn.sigmoid(cv)).astype(BF16).astype(F32)
        dm = d_ref[...]
        dw_ref[...] = lax.dot_general(sc, dm.astype(BF16).astype(F32), TN_DIMS, precision=lax.Precision.HIGHEST,
                                      preferred_element_type=F32)
        db_ref[...] = jnp.sum(dm, axis=0, keepdims=True)

    return pl.pallas_call(body, name=name, grid=(nk,),
                          in_specs=[pl.BlockSpec((N_DEV, D), lambda k: (0, 0)), pl.BlockSpec((None, N_DEV, n), lambda k: (k, 0, 0))],
                          out_specs=[pl.BlockSpec((None, D, n), lambda k: (k, 0, 0)), pl.BlockSpec((None, 1, n), lambda k: (k, 0, 0))],
                          out_shape=[jax.ShapeDtypeStruct((nk, D, n), F32), jax.ShapeDtypeStruct((nk, 1, n), F32)],
                          compiler_params=_cp("parallel"))(c_all, dmod)


def _adamw(w, g, m, v, *, name):
    R, C = w.shape
    tr = _tile(R, 256, 8)
    blk = pl.BlockSpec((tr, C), lambda i: (i, 0))
    c1 = 1.0 / (1.0 - ADAM_B1 ** ADAM_STEP)
    c2 = 1.0 / (1.0 - ADAM_B2 ** ADAM_STEP)

    def body(w_ref, g_ref, m_ref, v_ref, d_ref, nm_ref, nv_ref):
        gv = g_ref[...]
        nm = ADAM_B1 * m_ref[...] + (1.0 - ADAM_B1) * gv
        nv = ADAM_B2 * v_ref[...] + (1.0 - ADAM_B2) * (gv * gv)
        nm_ref[...] = nm
        nv_ref[...] = nv
        d_ref[...] = -ADAM_LR * ((nm * c1) / (jnp.sqrt(nv * c2) + ADAM_EPS) + ADAM_WD * w_ref[...])

    sh = jax.ShapeDtypeStruct((R, C), F32)
    return pl.pallas_call(body, name=name, grid=(R // tr,), in_specs=[blk] * 4, out_specs=[blk] * 3, out_shape=[sh] * 3,
                          compiler_params=_cp("parallel"))(w, g, m, v)


def _adamw_nd(w, g, m, v, *, name):
    shp = w.shape
    two = (-1, shp[-1]) if w.ndim > 1 else (1, -1)
    outs = _adamw(w.reshape(two), g.reshape(two), m.reshape(two), v.reshape(two), name=name)
    return [o.reshape(shp) for o in outs]


S5_TILE = 256
ATTN_BLOCK = 1024
BIG = ("s5_w_in", "s5_w_glu", "s5_w_out", "fox_w_in", "fox_w_out", "ffn_w_up", "ffn_w_down")
WEIGHTS = ("norm_g", "ada_w", "ada_b", "s5_w_in", "s5_lam_re", "s5_lam_im", "s5_log_dt", "s5_b_re", "s5_b_im", "s5_c_re",
           "s5_c_im", "s5_d", "s5_w_glu", "s5_w_out", "fox_w_in", "fox_b_f", "fox_w_out", "ffn_w_up", "ffn_conv_w",
           "ffn_conv_b", "ffn_w_down", "final_g")
REPLICATED = ("s5_lam_re", "s5_lam_im", "s5_log_dt", "s5_b_re", "s5_b_im", "s5_c_re", "s5_c_im", "s5_d", "fox_b_f",
              "ffn_conv_b", "final_g")


def _row_group(arrs):
    return jnp.concatenate([a.reshape(-1, D_MODEL) for a in arrs], axis=0)


def _cols_join(g):
    return jnp.concatenate([g[j] for j in range(N_DEV)], axis=1)


def _cols_split(full):
    n = full.shape[1] // N_DEV
    return jnp.stack([full[:, j * n:(j + 1) * n] for j in range(N_DEV)])


def kernel(x, c, norm_g, ada_w, ada_b, s5_w_in, s5_lam_re, s5_lam_im, s5_log_dt, s5_b_re, s5_b_im, s5_c_re, s5_c_im, s5_d, s5_w_glu, s5_w_out, fox_w_in, fox_b_f, fox_w_out, ffn_w_up, ffn_conv_w, ffn_conv_b, ffn_w_down, final_g, loss_target, m_norm_g, m_ada_w, m_ada_b, m_s5_w_in, m_s5_lam_re, m_s5_lam_im, m_s5_log_dt, m_s5_b_re, m_s5_b_im, m_s5_c_re, m_s5_c_im, m_s5_d, m_s5_w_glu, m_s5_w_out, m_fox_w_in, m_fox_b_f, m_fox_w_out, m_ffn_w_up, m_ffn_conv_w, m_ffn_conv_b, m_ffn_w_down, m_final_g, v_norm_g, v_ada_w, v_ada_b, v_s5_w_in, v_s5_lam_re, v_s5_lam_im, v_s5_log_dt, v_s5_b_re, v_s5_b_im, v_s5_c_re, v_s5_c_im, v_s5_d, v_s5_w_glu, v_s5_w_out, v_fox_w_in, v_fox_b_f, v_fox_w_out, v_ffn_w_up, v_ffn_conv_w, v_ffn_conv_b, v_ffn_w_down, v_final_g):
    args = dict(locals())
    W = {n: args[n] for n in WEIGHTS}
    M = {n: args["m_" + n] for n in WEIGHTS}
    V = {n: args["v_" + n] for n in WEIGHTS}
    D, F = D_MODEL, D_FF
    me = _my_index()
    h0 = x[0]
    S = h0.shape[0]
    lt = min(S5_TILE, S)
    tb = min(ATTN_BLOCK, S)

    b16 = lambda a: a.astype(BF16)
    g0, g_in = _exchange_many([(_pack([c, norm_g, ffn_conv_w], F32, 8), True), (b16(s5_w_in[0]), True)], name="gather_first")
    c_all, ng_all, cw_all = _unpack(g0, [(D,), norm_g.shape, ffn_conv_w.shape])
    ng_full = ng_all.transpose(1, 2, 0, 3).reshape(2, 2, D)
    cw_full = cw_all.transpose(1, 2, 0, 3).reshape(2, 3, F)

    ncol = ada_w.shape[-1]
    modp = _adaln_fwd(c_all, ada_w.reshape(4, D, ncol), ada_b.reshape(4, 1, ncol), name="adaln_fwd")
    g1 = _exchange(_pack([modp], F32, 8), gather=True, name="gather_adaln")
    (mod_all,) = _unpack(g1, [modp.shape])
    mod = lax.dynamic_index_in_dim(mod_all, me, axis=2, keepdims=False).transpose(1, 0, 2).reshape(4, 3 * D)
    shift = [mod[k:k + 1, :D] for k in range(4)]
    scale = [mod[k:k + 1, D:2 * D] for k in range(4)]
    gate = [mod[k:k + 1, 2 * D:] for k in range(4)]
    gain = [_row(ng_full[k // 2, k % 2]) for k in range(4)]

    rows = D // N_DEV
    full = {"s5_w_in": g_in.reshape(D, D)}
    bf_pad = jnp.pad(fox_b_f, ((0, 0), (0, 128 - HEADS)))

    def ffn_fwd(h, k, layer, comm=()):
        hn = _modulate(h, gain[k], shift[k], scale[k], name=f"modulate{k}")
        up = _mm(hn, full[f"ffn_w_up{layer}"], name=f"ffn_up{layer}", o_h=True, out_dtype=BF16, tn=1408, comm=comm)
        up, exchanged = up if comm else (up, ())
        z = _conv_gate_fwd(up, cw_full[layer], ffn_conv_b[layer:layer + 1], name=f"conv_gate{layer}")
        m, h_out = _mm(z, full[f"ffn_w_down{layer}"], name=f"ffn_down{layer}", res=h, gate=gate[k])
        return h_out, (hn, up, z, m), exchanged

    lb_re, lb_im, bb_re, bb_im = _s5_disc(s5_lam_re[0], s5_lam_im[0], s5_log_dt[0], s5_b_re[0], s5_b_im[0])
    wb, wc, lam = _s5_pack(lb_re, lb_im, bb_re, bb_im, s5_c_re[0], s5_c_im[0])
    hn0 = _modulate(h0, gain[0], shift[0], scale[0], name="modulate0")
    u = _mm(hn0, full["s5_w_in"], name="s5_in", out_dtype=BF16)
    (ypre, yg, st), (g_up0, g_rows0) = _s5_fwd(
        u, wb, wc, lam, s5_d, lt=lt, name="s5_scan",
        comm=[(b16(ffn_w_up[0]), True), (_row_group([b16(s5_w_glu), b16(s5_w_out), b16(ffn_w_down[0])]), True)])
    full["ffn_w_up0"] = _cols_join(g_up0)
    full["s5_w_glu"] = g_rows0[:, :rows].reshape(D, D)
    full["s5_w_out"] = g_rows0[:, rows:2 * rows].reshape(D, D)
    full["ffn_w_down0"] = g_rows0[:, 2 * rows:].reshape(F, D)
    gl = _mm(yg, full["s5_w_glu"], name="s5_glu", out_dtype=BF16)
    y2 = _glu_fwd(yg, gl, name="s5_glu_gate")
    m0, h1 = _mm(y2, full["s5_w_out"], name="s5_out", res=h0, gate=gate[0])
    h2, (hn1, up0, z0, m1), (g_fox, g_fox_out) = ffn_fwd(h1, 1, 0, comm=[(b16(fox_w_in[0]), True), (b16(fox_w_out[0]), True)])
    full["fox_w_in"] = _cols_join(g_fox)
    full["fox_w_out"] = g_fox_out.reshape(D, D)
    w_proj = jnp.pad(full["fox_w_in"], ((0, 0), (0, 3 * D + 128 - full["fox_w_in"].shape[1])))
    w_qkv, w_f = w_proj[:, :3 * D], w_proj[:, 3 * D:]

    hn2 = _modulate(h2, gain[2], shift[2], scale[2], name="modulate2")
    nb = S // tb
    w_qkv_s = jnp.concatenate([w_qkv[:, :D] * Q_SCALE, w_qkv[:, D:]], axis=1)
    qkv, qkvT = _mm(hn2, w_qkv_s, name="fox_qkv", out_dtype=BF16, with_t=True)
    fl = _mm(hn2, w_f, name="fox_f", out_dtype=F32)
    fcol = _fgate_fwd(fl, bf_pad, name="fox_fgate")
    f_heads = fcol[:, :HEADS].T.reshape(HP, 2, S)
    ka = _fox_keys(qkv, fcol, name="fox_keys")
    vtb = _blocked(qkvT[2 * D:].reshape(HP, 128, S), nb, tb)
    aug = _aug_rows(f_heads)
    (oT, lse), (g_up1, g_down1) = _foxt_fwd(qkvT, ka, vtb, aug, tb=tb, name="fox_attn",
                                            comm=[(b16(ffn_w_up[1]), True), (b16(ffn_w_down[1]), True)])
    full["ffn_w_up1"] = _cols_join(g_up1)
    full["ffn_w_down1"] = g_down1.reshape(F, D)
    m2, h3 = _mm(oT, full["fox_w_out"], name="fox_out", ta=True, res=h2, gate=gate[2])
    h4, (hn3, up1, z1, m3), _ = ffn_fwd(h3, 3, 1)

    lblk, dh, head_sums, dm = _loss_head(h4, _row(final_g), loss_target[0], m3, gate[3], name="loss_head")
    d_final_g, dgate = head_sums[0:1], head_sums[1:2]
    loss = lax.psum(lblk[0, 0], ("x", "y", "c"))

    G = {}
    dmod = [None] * 4
    mixer_out = [m0, m1, m2, m3]

    def norm_bwd(h, dhn, dh_in, k, dgate_k):
        if k == 0:
            dh_out, sums = _modulate_bwd(h, dhn, dh_in, gain[k], scale[k], name=f"modulate_bwd{k}")
            dm_below = dgate_below = None
        else:
            dh_out, sums, dm_below = _modulate_bwd(h, dhn, dh_in, gain[k], scale[k], mixer_out[k - 1], gate[k - 1],
                                                   name=f"modulate_bwd{k}")
            dgate_below = sums[3:4]
        dmod[k] = jnp.concatenate([sums[0:1], sums[1:2], dgate_k], axis=1)
        return dh_out, sums[2], dm_below, dgate_below

    def ffn_bwd(dh_in, dm, dgate_k, h, k, layer, saved):
        hn, up, z, m = saved
        dz = _mm(dm, full[f"ffn_w_down{layer}"], name=f"ffn_down_dx{layer}", tb=True, out_dtype=BF16, tn=1408)
        dw_down = _mm(z, dm, name=f"ffn_down_dw{layer}", ta=True, tm=1408, tk=2048)
        d_up, cs = _conv_gate_bwd(up, dz, cw_full[layer], ffn_conv_b[layer:layer + 1], name=f"conv_gate_bwd{layer}")
        dhn = _mm(d_up, full[f"ffn_w_up{layer}"], name=f"ffn_up_dx{layer}", tb=True, a_h=True, tk=2816)
        dw_up = _mm(hn, d_up, name=f"ffn_up_dw{layer}", ta=True, b_h=True, tm=1024, tn=1408, tk=2048)
        dh_out, dg, dm_below, dgate_below = norm_bwd(h, dhn, dh_in, k, dgate_k)
        return dh_out, dm_below, dgate_below, dg, dw_up, dw_down, cs[0:3], cs[3]

    dh, dm, dgate, dg3, dw_up1, dw_down1, dcw1, dcb1 = ffn_bwd(dh, dm, dgate, h3, 3, 1, (hn3, up1, z1, m3))

    do, doT = _mm(dm, full["fox_w_out"], name="fox_out_dx", tb=True, out_dtype=BF16, with_t=True)
    dw_fox_out = _mm(oT, dm, name="fox_out_dw", tk=2048)
    to_rows = lambda g: b16(g).reshape(N_DEV, -1, D)
    (dqtb, dk, dv, dfk, dfq), (r_up1, r_down1) = _foxt_bwd(
        qkvT, qkv, ka, _blocked(qkvT[:D].reshape(HP, 128, S), nb, tb), _blocked(doT.reshape(HP, 128, S), nb, tb), do,
        _blocked(aug, nb, tb), _blocked(lse, nb, tb), _blocked(oT.reshape(HP, 128, S), nb, tb), tb=tb, name="fox_attn_bwd",
        comm=[(_cols_split(b16(dw_up1)), False), (to_rows(dw_down1), False)])
    dq = dqtb.transpose(1, 3, 0, 2).reshape(S, D)
    dF = dfk[:, :, :2].transpose(1, 0, 2).reshape(S, HEADS) + dfq[:, :, :2, :].transpose(1, 3, 0, 2).reshape(S, HEADS)
    dF = jnp.pad(dF, ((0, 0), (0, 128 - HEADS)))
    dfl, dbf = _fgate_bwd(dF, fl, bf_pad, name="fox_fgate_bwd")
    dproj = jnp.concatenate([dq, dk, dv, dfl.astype(BF16)], axis=1)
    dhn = _mm(dproj, w_proj, name="fox_in_dx", tb=True)
    dw_proj = _mm(hn2, dproj, name="fox_in_dw", ta=True, tn=640, tk=2048)
    dw_fox_in = dw_proj[:, :full["fox_w_in"].shape[1]]
    dh, dg2, dm, dgate = norm_bwd(h2, dhn, dh, 2, dgate)

    dh, dm, dgate, dg1, dw_up0, dw_down0, dcw0, dcb0 = ffn_bwd(dh, dm, dgate, h1, 1, 0, (hn1, up0, z0, m1))

    dy2 = _mm(dm, full["s5_w_out"], name="s5_out_dx", tb=True, out_dtype=BF16)
    G["s5_w_out"] = _mm(y2, dm, name="s5_out_dw", ta=True, tk=2048)
    da, dgl = _glu_bwd(dy2, yg, gl, name="s5_glu_bwd")
    dyg_b = _mm(dgl, full["s5_w_glu"], name="s5_glu_dx", tb=True)
    G["s5_w_glu"] = _mm(yg, dgl, name="s5_glu_dw", ta=True, tk=2048)
    dyp, dd = _gelu_bwd(da, dyg_b, ypre, u, name="s5_gelu_bwd")
    row_grads = [dw_fox_out, dw_down0, G["s5_w_out"], G["s5_w_glu"]]
    (du, dwb, dwc, dlam), (r_up0, r_rows0, r_fox) = _s5_bwd(
        u, dyp, st, wb, wc, lam, s5_d, lt=lt, name="s5_scan_bwd",
        comm=[(_cols_split(b16(dw_up0)), False), (jnp.concatenate([to_rows(g) for g in row_grads], axis=1), False),
              (_cols_split(b16(dw_fox_in)), False)])
    dhn = _mm(du, full["s5_w_in"], name="s5_in_dx", tb=True)
    G["s5_w_in"] = _mm(hn0, du, name="s5_in_dw", ta=True, tk=2048)
    dh, dg0, _, _ = norm_bwd(h0, dhn, dh, 0, dgate)
    grad_x = dh[None]

    dlb_re, dlb_im, dbb_re, dbb_im, dc_re, dc_im = _s5_unpack(dwb, dwc, dlam)
    _, disc_vjp = jax.vjp(_s5_disc, s5_lam_re[0], s5_lam_im[0], s5_log_dt[0], s5_b_re[0], s5_b_im[0])
    dlam_re, dlam_im, dlog_dt, db_re, db_im = disc_vjp((dlb_re, dlb_im, dbb_re, dbb_im))

    small_full = {
        "norm_g": jnp.stack([dg0, dg1, dg2, dg3]).reshape(2, 2, D),
        "ffn_conv_w": jnp.stack([dcw0, dcw1]),
        "s5_lam_re": dlam_re[None], "s5_lam_im": dlam_im[None], "s5_log_dt": dlog_dt[None],
        "s5_b_re": db_re[None], "s5_b_im": db_im[None], "s5_c_re": dc_re[None], "s5_c_im": dc_im[None],
        "s5_d": dd, "fox_b_f": dbf[:, :HEADS], "ffn_conv_b": jnp.stack([dcb0, dcb1]), "final_g": d_final_g[0],
    }
    names = tuple(small_full)
    small_pack = _pack([small_full[n] for n in names], F32, 8 * N_DEV)
    r_in, r_small, g2 = _exchange_many([(to_rows(G["s5_w_in"]), False), (small_pack.reshape(N_DEV, -1, 1024), False),
                                        (_pack([jnp.concatenate(dmod, axis=0)], F32, 8), True)], name="last_exchange")
    G["s5_w_in"] = _sum8(r_in, name="sum_s5_in_grads")[None]
    (dmod_all,) = _unpack(g2, [(4, 3 * D)])
    d_ada_w, d_ada_b = _adaln_bwd(c_all, _columns_of_mod(dmod_all, me, ncol), name="adaln_bwd")
    G["ada_w"] = d_ada_w.reshape(ada_w.shape)
    G["ada_b"] = d_ada_b.reshape(ada_b.shape)
    rows0_sum = _sum8(r_rows0, name="sum_rows0_grads")
    down_rows = F // N_DEV
    G["fox_w_out"] = rows0_sum[:rows][None]
    G["s5_w_out"] = rows0_sum[rows + down_rows:2 * rows + down_rows][None]
    G["s5_w_glu"] = rows0_sum[2 * rows + down_rows:][None]
    G["ffn_w_down"] = jnp.stack([rows0_sum[rows:rows + down_rows], _sum8(r_down1, name="sum_down1_grads")])
    G["ffn_w_up"] = jnp.stack([_sum8(r_up0, name="sum_up0_grads"), _sum8(r_up1, name="sum_up1_grads")])
    G["fox_w_in"] = _sum8(r_fox, name="sum_fox_w_in")[None]

    g3 = _exchange(_sum8(r_small, name="sum_small_grads"), gather=True, name="gather_small_sums")
    summed = dict(zip(names, _unpack(g3.reshape(small_pack.shape), [small_full[n].shape for n in names])))
    G["norm_g"] = lax.dynamic_slice_in_dim(summed["norm_g"], me * norm_g.shape[-1], norm_g.shape[-1], axis=2)
    G["ffn_conv_w"] = lax.dynamic_slice_in_dim(summed["ffn_conv_w"], me * ffn_conv_w.shape[-1], ffn_conv_w.shape[-1], axis=2)
    for n in REPLICATED:
        G[n] = summed[n]

    delta, new_m, new_v = {}, {}, {}
    small = tuple(n for n in WEIGHTS if n not in BIG and n != "ada_w")
    for n in WEIGHTS:
        if n not in small:
            delta[n], new_m[n], new_v[n] = _adamw_nd(W[n], G[n], M[n], V[n], name=f"adamw_{n}")
    packed = [_pack([src[n] for n in small], F32, 8) for src in (W, G, M, V)]
    for dst, buf in zip((delta, new_m, new_v), _adamw(*packed, name="adamw_small")):
        dst.update(zip(small, _unpack(buf, [W[n].shape for n in small])))

    return (loss, grad_x, *[G[n] for n in WEIGHTS], *[delta[n] for n in WEIGHTS], *[new_m[n] for n in WEIGHTS],
            *[new_v[n] for n in WEIGHTS])


def _columns_of_mod(dmod_all, me, ncol):
    flat = lax.dynamic_slice_in_dim(dmod_all, me * ncol, ncol, axis=2)
    return flat.transpose(1, 0, 2)
```

```python
import functools
import math

import jax
import jax.numpy as jnp
from jax import lax
from jax.experimental import pallas as pl
from jax.experimental.pallas import tpu as pltpu

F32, BF16 = jnp.float32, jnp.bfloat16
EPS = 1e-6
N_DEV = 8
D_MODEL = 1024
D_FF = 2816
HEADS = 16
HEAD_DIM = 64
S5_G, S5_P, S5_C = 64, 64, 16
S5_SUB = 8
V7X_VMEM_LIMIT = 56 * 1024 * 1024
NEG = -1e30
ADAM_LR, ADAM_B1, ADAM_B2, ADAM_EPS, ADAM_WD, ADAM_STEP = 1e-3, 0.9, 0.999, 1e-8, 0.01, 10
GELU_K = math.sqrt(2.0 / math.pi)
MESH_ID = pl.DeviceIdType.MESH


def _cp(*sem):
    return pltpu.CompilerParams(dimension_semantics=sem, vmem_limit_bytes=V7X_VMEM_LIMIT)


def _tile(n, target, mult=128):
    if n <= target:
        return n
    t = (target // mult) * mult
    while t >= mult:
        if n % t == 0:
            return t
        t -= mult
    return n


def _row(v):
    return v.reshape(1, -1).astype(F32)


def _mm(a, b, *, name, ta=False, tb=False, out_dtype=F32, tm=1024, tn=1024, tk=None, res=None, gate=None,
        a_h=False, b_h=False, o_h=False, with_t=False, comm=()):
    if a_h:
        M, K = a.shape[1], 2 * a.shape[2]
    elif ta:
        K, M = a.shape
    else:
        M, K = a.shape
    if b_h:
        N = 2 * b.shape[2]
    else:
        N = b.shape[0] if tb else b.shape[1]
    half_n = N // 2
    tm = _tile(M, tm, 128 if ta else 8)
    tn = _tile(half_n if (b_h or o_h) else N, tn)
    tk = K if tk is None else _tile(K // 2 if a_h else K, tk)
    nk = K // tk
    nkh, nnh = (K // 2) // tk if a_h else 1, half_n // tn
    if a_h:
        a_spec = pl.BlockSpec((None, tm, tk), lambda i, j, k: (k // nkh, i, k % nkh))
    elif ta:
        a_spec = pl.BlockSpec((tk, tm), lambda i, j, k: (k, i))
    else:
        a_spec = pl.BlockSpec((tm, tk), lambda i, j, k: (i, k))
    if b_h:
        b_spec = pl.BlockSpec((None, tk, tn), lambda i, j, k: (j // nnh, k, j % nnh))
    elif tb:
        b_spec = pl.BlockSpec((tn, tk), lambda i, j, k: (j, k))
    else:
        b_spec = pl.BlockSpec((tk, tn), lambda i, j, k: (k, j))
    if o_h:
        o_spec = pl.BlockSpec((None, tm, tn), lambda i, j, k: (j // nnh, i, j % nnh))
    else:
        o_spec = pl.BlockSpec((tm, tn), lambda i, j, k: (i, j))
    dn = (((0 if ta else 1,), (1 if tb else 0,)), ((), ()))
    fused = res is not None

    def body(*refs):
        if fused:
            a_ref, b_ref, r_ref, g_ref, m_ref, o_ref, acc_ref = refs
        elif with_t:
            a_ref, b_ref, o_ref, t_ref, acc_ref = refs
        else:
            a_ref, b_ref, o_ref, acc_ref = refs
        p = lax.dot_general(a_ref[...].astype(BF16), b_ref[...].astype(BF16), dn, preferred_element_type=F32)

        def finish(acc):
            if fused:
                m_ref[...] = acc.astype(m_ref.dtype)
                o_ref[...] = r_ref[...] + g_ref[...] * acc
            else:
                o_ref[...] = acc.astype(o_ref.dtype)
                if with_t:
                    t_ref[...] = acc.T.astype(t_ref.dtype)

        if nk == 1:
            finish(p)
        else:
            k = pl.program_id(2)

            @pl.when(k == 0)
            def _():
                acc_ref[...] = p

            @pl.when(k > 0)
            def _():
                acc_ref[...] += p

            @pl.when(k == nk - 1)
            def _():
                finish(acc_ref[...])

    in_specs = [a_spec, b_spec]
    args = [a, b]
    if fused:
        in_specs += [o_spec, pl.BlockSpec((1, tn), lambda i, j, k: (0, j))]
        args += [res, gate]
        out_shape = [jax.ShapeDtypeStruct((M, N), BF16), jax.ShapeDtypeStruct((M, N), F32)]
        out_specs = [o_spec, o_spec]
    else:
        out_shape = [jax.ShapeDtypeStruct((2, M, half_n) if o_h else (M, N), out_dtype)]
        out_specs = [o_spec]
        if with_t:
            out_shape.append(jax.ShapeDtypeStruct((N, M), out_dtype))
            out_specs.append(pl.BlockSpec((tn, tm), lambda i, j, k: (j, i)))
    outs, exchanged = _call(
        body, name=name, grid=(M // tm, N // tn, nk), in_specs=in_specs, out_specs=out_specs, out_shape=out_shape,
        scratch_shapes=[pltpu.VMEM((tm, tn) if nk > 1 else (8, 128), F32)], sem=("parallel", "parallel", "arbitrary"),
        args=args, comm=comm)
    outs = tuple(outs) if (fused or with_t) else outs[0]
    return (outs, exchanged) if comm else outs


def _modulate(h, g, shift, scale, *, name):
    S, D = h.shape
    tm = _tile(S, 512, 8)
    vec = pl.BlockSpec((1, D), lambda i: (0, 0))
    blk = pl.BlockSpec((tm, D), lambda i: (i, 0))

    def body(h_ref, g_ref, sh_ref, sc_ref, o_ref):
        x = h_ref[...]
        r = lax.rsqrt(jnp.mean(x * x, axis=-1, keepdims=True) + EPS)
        o_ref[...] = ((x * r * g_ref[...]) * (1.0 + sc_ref[...]) + sh_ref[...]).astype(o_ref.dtype)

    return pl.pallas_call(body, name=name, grid=(S // tm,), in_specs=[blk, vec, vec, vec], out_specs=blk,
                          out_shape=jax.ShapeDtypeStruct((S, D), BF16), compiler_params=_cp("parallel"))(h, g, shift, scale)


def _modulate_bwd(h, dhn, dh_in, g, scale, m_prev=None, gate_prev=None, *, name):
    S, D = h.shape
    tm = _tile(S, 512, 8)
    vec = pl.BlockSpec((1, D), lambda i: (0, 0))
    blk = pl.BlockSpec((tm, D), lambda i: (i, 0))
    sums = pl.BlockSpec((8, D), lambda i: (0, 0))
    below = m_prev is not None

    def body(h_ref, dhn_ref, dhi_ref, g_ref, sc_ref, *rest):
        dh_ref, s_ref = rest[-3:-1] if below else rest[-2:]

        @pl.when(pl.program_id(0) == 0)
        def _():
            s_ref[...] = jnp.zeros_like(s_ref)

        x = h_ref[...]
        r = lax.rsqrt(jnp.mean(x * x, axis=-1, keepdims=True) + EPS)
        xhat = x * r
        gv = g_ref[...]
        dhn_v = dhn_ref[...].astype(F32)
        dn = dhn_v * (1.0 + sc_ref[...])
        s_ref[0:1, :] += jnp.sum(dhn_v, axis=0, keepdims=True)
        s_ref[1:2, :] += jnp.sum(dhn_v * (xhat * gv), axis=0, keepdims=True)
        s_ref[2:3, :] += jnp.sum(dn * xhat, axis=0, keepdims=True)
        dxh = dn * gv
        dh = dhi_ref[...] + r * (dxh - xhat * jnp.mean(dxh * xhat, axis=-1, keepdims=True))
        dh_ref[...] = dh
        if below:
            mp_ref, gp_ref, dm_ref = rest[0], rest[1], rest[-1]
            dm_ref[...] = (dh * gp_ref[...]).astype(dm_ref.dtype)
            s_ref[3:4, :] += jnp.sum(dh * mp_ref[...].astype(F32), axis=0, keepdims=True)

    extra_in, extra_args = ([blk, vec], [m_prev, gate_prev]) if below else ([], [])
    return pl.pallas_call(body, name=name, grid=(S // tm,), in_specs=[blk, blk, blk, vec, vec] + extra_in,
                          out_specs=[blk, sums] + ([blk] if below else []),
                          out_shape=[jax.ShapeDtypeStruct((S, D), F32), jax.ShapeDtypeStruct((8, D), F32)]
                          + ([jax.ShapeDtypeStruct((S, D), BF16)] if below else []),
                          compiler_params=_cp("arbitrary"))(h, dhn, dh_in, g, scale, *extra_args)


def _loss_head(h, g, tgt, m_prev, gate_prev, *, name):
    S, D = h.shape
    tm = _tile(S, 512, 8)
    vec = pl.BlockSpec((1, D), lambda i: (0, 0))
    blk = pl.BlockSpec((tm, D), lambda i: (i, 0))
    lss = pl.BlockSpec((8, 128), lambda i: (0, 0))
    sums = pl.BlockSpec((8, D), lambda i: (0, 0))

    def body(h_ref, g_ref, t_ref, mp_ref, gp_ref, l_ref, dh_ref, s_ref, dm_ref):
        @pl.when(pl.program_id(0) == 0)
        def _():
            l_ref[...] = jnp.zeros_like(l_ref)
            s_ref[...] = jnp.zeros_like(s_ref)

        x = h_ref[...]
        r = lax.rsqrt(jnp.mean(x * x, axis=-1, keepdims=True) + EPS)
        xhat = x * r
        gv = g_ref[...]
        e = xhat * gv - t_ref[...]
        l_ref[...] += 0.5 * jnp.sum(jnp.mean(e * e, axis=-1, keepdims=True))
        dy = e * (1.0 / D)
        s_ref[0:1, :] += jnp.sum(dy * xhat, axis=0, keepdims=True)
        dxh = dy * gv
        dh = r * (dxh - xhat * jnp.mean(dxh * xhat, axis=-1, keepdims=True))
        dh_ref[...] = dh
        dm_ref[...] = (dh * gp_ref[...]).astype(dm_ref.dtype)
        s_ref[1:2, :] += jnp.sum(dh * mp_ref[...].astype(F32), axis=0, keepdims=True)

    return pl.pallas_call(body, name=name, grid=(S // tm,), in_specs=[blk, vec, blk, blk, vec], out_specs=[lss, blk, sums, blk],
                          out_shape=[jax.ShapeDtypeStruct((8, 128), F32), jax.ShapeDtypeStruct((S, D), F32),
                                     jax.ShapeDtypeStruct((8, D), F32), jax.ShapeDtypeStruct((S, D), BF16)],
                          compiler_params=_cp("arbitrary"))(h, g, tgt, m_prev, gate_prev)


def _shift_down(x, k, edge):
    tm = x.shape[0]
    rows = lax.broadcasted_iota(jnp.int32, x.shape, 0)
    out = pltpu.roll(x, k, 0)
    for j in range(k):
        out = jnp.where(rows == j, edge[8 - k + j:8 - k + j + 1, :], out)
    return out


def _shift_up(x, k, edge):
    tm = x.shape[0]
    rows = lax.broadcasted_iota(jnp.int32, x.shape, 0)
    out = pltpu.roll(x, tm - k, 0)
    for j in range(k):
        out = jnp.where(rows == tm - k + j, edge[j:j + 1, :], out)
    return out


def _conv_gate_fwd(up, cw, cb, *, name):
    _, S, F = up.shape
    tf = _tile(F, 1408)
    nf = F // tf
    tm = _tile(S, 512, 8)

    def body(a_ref, b_ref, w_ref, cb_ref, z_ref, edge_ref):
        @pl.when(pl.program_id(1) == 0)
        def _():
            edge_ref[...] = jnp.zeros_like(edge_ref)

        a = a_ref[...].astype(F32)
        edge = edge_ref[...]
        w = w_ref[...]
        ac = cb_ref[...] + w[2:3, :] * a + w[1:2, :] * _shift_down(a, 1, edge) + w[0:1, :] * _shift_down(a, 2, edge)
        edge_ref[...] = a[tm - 8:tm, :]
        z_ref[...] = (ac * jax.nn.sigmoid(ac) * b_ref[...].astype(F32)).astype(z_ref.dtype)

    return pl.pallas_call(
        body, name=name, grid=(nf, S // tm),
        in_specs=[pl.BlockSpec((None, tm, tf), lambda j, i: (0, i, j)), pl.BlockSpec((None, tm, tf), lambda j, i: (1, i, j)),
                  pl.BlockSpec((3, tf), lambda j, i: (0, j)), pl.BlockSpec((1, tf), lambda j, i: (0, j))],
        out_specs=pl.BlockSpec((tm, tf), lambda j, i: (i, j)),
        out_shape=jax.ShapeDtypeStruct((S, F), BF16), scratch_shapes=[pltpu.VMEM((8, tf), F32)],
        compiler_params=_cp("parallel", "arbitrary"))(up, up, cw, cb)


def _conv_gate_bwd(up, dz, cw, cb, *, name):
    _, S, F = up.shape
    tf = _tile(F, 1408)
    nf = F // tf
    tm = _tile(S, 512, 8)
    nt = S // tm
    hb = tm // 8

    def body(a_ref, ah_ref, b_ref, dz_ref, w_ref, cb_ref, d_ref, s_ref, edge_ref):
        i = pl.program_id(1)

        @pl.when(i == 0)
        def _():
            edge_ref[...] = jnp.zeros_like(edge_ref)
            s_ref[...] = jnp.zeros_like(s_ref)

        a = a_ref[...].astype(F32)
        halo = jnp.where(i == nt - 1, 0.0, ah_ref[...].astype(F32))
        w = w_ref[...]
        a1 = _shift_down(a, 1, halo)
        a2 = _shift_down(a, 2, halo)
        ac = cb_ref[...] + w[2:3, :] * a + w[1:2, :] * a1 + w[0:1, :] * a2
        sg = jax.nn.sigmoid(ac)
        dzv = dz_ref[...].astype(F32)
        si = ac * sg
        d_ref[1] = (dzv * si).astype(d_ref.dtype)
        dac = (dzv * b_ref[...].astype(F32)) * (sg + si * (1.0 - sg))
        s_ref[0:1, :] += jnp.sum(dac * a2, axis=0, keepdims=True)
        s_ref[1:2, :] += jnp.sum(dac * a1, axis=0, keepdims=True)
        s_ref[2:3, :] += jnp.sum(dac * a, axis=0, keepdims=True)
        s_ref[3:4, :] += jnp.sum(dac, axis=0, keepdims=True)
        edge = edge_ref[...]
        da = w[2:3, :] * dac + w[1:2, :] * _shift_up(dac, 1, edge) + w[0:1, :] * _shift_up(dac, 2, edge)
        edge_ref[...] = dac[0:8, :]
        d_ref[0] = da.astype(d_ref.dtype)

    tile = lambda hlf: pl.BlockSpec((None, tm, tf), lambda j, i: (hlf, nt - 1 - i, j))
    d_up, sums = pl.pallas_call(
        body, name=name, grid=(nf, nt),
        in_specs=[tile(0),
                  pl.BlockSpec((None, 8, tf), lambda j, i: (0, jnp.maximum((nt - 1 - i) * hb - 1, 0), j)),
                  tile(1), pl.BlockSpec((tm, tf), lambda j, i: (nt - 1 - i, j)),
                  pl.BlockSpec((3, tf), lambda j, i: (0, j)), pl.BlockSpec((1, tf), lambda j, i: (0, j))],
        out_specs=[pl.BlockSpec((2, tm, tf), lambda j, i: (0, nt - 1 - i, j)), pl.BlockSpec((8, tf), lambda j, i: (0, j))],
        out_shape=[jax.ShapeDtypeStruct((2, S, F), BF16), jax.ShapeDtypeStruct((8, F), F32)],
        scratch_shapes=[pltpu.VMEM((8, tf), F32)],
        compiler_params=_cp("parallel", "arbitrary"))(up, up, up, dz, cw, cb)
    return d_up, sums


def _log_sigmoid(x):
    return jnp.minimum(x, 0.0) - jnp.log(1.0 + jnp.exp(-jnp.abs(x)))


def _tri_ones(n, upper):
    r = lax.broadcasted_iota(jnp.int32, (n, n), 0)
    c = lax.broadcasted_iota(jnp.int32, (n, n), 1)
    return jnp.where((c >= r) if upper else (c <= r), 1.0, 0.0).astype(F32)


def _fgate_fwd(fl, bf, *, name):
    S, W = fl.shape
    tb = _tile(S, 256, 8)

    def body(fl_ref, b_ref, o_ref, carry_ref):
        @pl.when(pl.program_id(0) == 0)
        def _():
            carry_ref[...] = jnp.zeros_like(carry_ref)

        lf = _log_sigmoid(fl_ref[...] + b_ref[...])
        cs = jnp.dot(_tri_ones(tb, False), lf, precision=lax.Precision.HIGHEST, preferred_element_type=F32) + carry_ref[0:1, :]
        o_ref[...] = cs
        carry_ref[...] = jnp.broadcast_to(cs[tb - 1:tb, :], carry_ref.shape)

    blk = pl.BlockSpec((tb, W), lambda i: (i, 0))
    return pl.pallas_call(body, name=name, grid=(S // tb,), in_specs=[blk, pl.BlockSpec((1, W), lambda i: (0, 0))], out_specs=blk,
                          out_shape=jax.ShapeDtypeStruct((S, W), F32), scratch_shapes=[pltpu.VMEM((8, W), F32)],
                          compiler_params=_cp("arbitrary"))(fl, bf)


def _fgate_bwd(dF, fl, bf, *, name):
    S, W = fl.shape
    tb = _tile(S, 256, 8)
    nb = S // tb

    def body(d_ref, fl_ref, b_ref, o_ref, s_ref, carry_ref):
        @pl.when(pl.program_id(0) == 0)
        def _():
            carry_ref[...] = jnp.zeros_like(carry_ref)
            s_ref[...] = jnp.zeros_like(s_ref)

        rc = jnp.dot(_tri_ones(tb, True), d_ref[...], precision=lax.Precision.HIGHEST, preferred_element_type=F32) + carry_ref[0:1, :]
        carry_ref[...] = jnp.broadcast_to(rc[0:1, :], carry_ref.shape)
        dfl = rc * jax.nn.sigmoid(-(fl_ref[...] + b_ref[...]))
        o_ref[...] = dfl
        s_ref[...] += jnp.sum(dfl, axis=0, keepdims=True)

    blk = pl.BlockSpec((tb, W), lambda i: (nb - 1 - i, 0))
    vec = pl.BlockSpec((1, W), lambda i: (0, 0))
    return pl.pallas_call(body, name=name, grid=(nb,), in_specs=[blk, blk, vec], out_specs=[blk, vec],
                          out_shape=[jax.ShapeDtypeStruct((S, W), F32), jax.ShapeDtypeStruct((1, W), F32)],
                          scratch_shapes=[pltpu.VMEM((8, W), F32)], compiler_params=_cp("arbitrary"))(dF, fl, bf)


NT_DIMS = (((1,), (1,)), ((), ()))
TN_DIMS = (((0,), (0,)), ((), ()))
HP = HEADS // 2
Q_SCALE = HEAD_DIM ** -0.5


def _head_mask(x, hh):
    lanes = lax.broadcasted_iota(jnp.int32, x.shape, 1)
    return jnp.where((lanes >= hh * HEAD_DIM) & (lanes < (hh + 1) * HEAD_DIM), x, jnp.zeros_like(x))


def _lanes_from(cols, shape):
    lanes = lax.broadcasted_iota(jnp.int32, shape, 1)
    out = jnp.zeros(shape, F32)
    for i, cvec in enumerate(cols):
        out = jnp.where(lanes == i, cvec, out)
    return out


AUG_ROWS = 16
DIAGONAL_TILES = ((0, 0, True), (0, 1, False), (1, 1, True))


def _qa_pair(qt, aug):
    fill = jnp.zeros((HEAD_DIM - AUG_ROWS, qt.shape[1]), qt.dtype)
    return [jnp.concatenate([qt[0:HEAD_DIM], aug[0], fill], axis=0), jnp.concatenate([aug[1], fill, qt[HEAD_DIM:]], axis=0)]


def _rows_of_head(xt, hh):
    rows = lax.broadcasted_iota(jnp.int32, xt.shape, 0)
    return jnp.where((rows >= hh * HEAD_DIM) & (rows < (hh + 1) * HEAD_DIM), xt, jnp.zeros_like(xt))


def _put_rows(ref, rows):
    for i, r in enumerate(rows):
        ref[i:i + 1, :] = r
    ref[len(rows):, :] = jnp.zeros((ref.shape[0] - len(rows), ref.shape[1]), ref.dtype)


def _diag_mask_t(s):
    rows = lax.broadcasted_iota(jnp.int32, s.shape, 0)
    cols = lax.broadcasted_iota(jnp.int32, s.shape, 1)
    return jnp.where(rows <= cols, s, NEG)


def _foxt_fwd(qkvT, ka, vtb, augq, *, tb, name, comm=()):
    S = qkvT.shape[1]
    nb = S // tb

    def body(q_ref, ka_ref, v_ref, aq_ref, o_ref, lse_ref, m0, m1, l0, l1, acc0, acc1):
        qi = pl.program_id(1)
        qa = _qa_pair(q_ref[...], aq_ref[...])
        half = tb // 2
        state = ((m0, l0, acc0), (m1, l1, acc1))
        for m_sc, l_sc, acc_sc in state:
            m_sc[...] = jnp.full_like(m_sc, NEG)
            l_sc[...] = jnp.zeros_like(l_sc)
            acc_sc[...] = jnp.zeros_like(acc_sc)

        def tile(kj, ko, qo, n, masked):
            k0 = pl.multiple_of(kj * tb + ko, n)
            cols = slice(qo, qo + n)
            logits = [jnp.dot(ka_ref[hh, pl.ds(k0, n), :], qa[hh][:, cols], preferred_element_type=F32) for hh in range(2)]
            updates = []
            for hh, (m_sc, l_sc, acc_sc) in enumerate(state):
                s = _diag_mask_t(logits[hh]) if masked else logits[hh]
                m_old = m_sc[:, cols]
                m_new = jnp.maximum(m_old, jnp.max(s, axis=0, keepdims=True))
                alpha = jnp.exp(m_old - m_new)
                p = jnp.exp(s - m_new)
                l_sc[:, cols] = alpha * l_sc[:, cols] + jnp.sum(p, axis=0, keepdims=True)
                m_sc[:, cols] = m_new
                vt = v_ref[kj, HEAD_DIM * hh:HEAD_DIM * (hh + 1), ko:ko + n]
                updates.append((alpha, jnp.dot(vt, p.astype(BF16), preferred_element_type=F32)))
            for (alpha, pv), (_, _, acc_sc) in zip(updates, state):
                acc_sc[:, cols] = alpha * acc_sc[:, cols] + pv

        def off_diagonal(kj, c):
            tile(kj, 0, 0, tb, False)
            return c

        lax.fori_loop(0, qi, off_diagonal, 0)
        for ko, qo, masked in DIAGONAL_TILES:
            tile(qi, ko * half, qo * half, half, masked)
        for hh, (m_sc, l_sc, acc_sc) in enumerate(state):
            o_ref[HEAD_DIM * hh:HEAD_DIM * (hh + 1), :] = (acc_sc[...] / l_sc[...]).astype(o_ref.dtype)
        _put_rows(lse_ref, [m_sc[...] + jnp.log(l_sc[...]) for m_sc, l_sc, _ in state])

    return _call(
        body, name=name, grid=(HP, nb), comm=comm, sem=("parallel", "parallel"), args=(qkvT, ka, vtb, augq),
        in_specs=[pl.BlockSpec((128, tb), lambda hp, qi: (hp, qi)),
                  pl.BlockSpec((None, 2, S, 128), lambda hp, qi: (hp, 0, 0, 0)),
                  pl.BlockSpec((None, nb, 128, tb), lambda hp, qi: (hp, 0, 0, 0)),
                  pl.BlockSpec((None, 2, AUG_ROWS, tb), lambda hp, qi: (hp, 0, 0, qi))],
        out_specs=[pl.BlockSpec((128, tb), lambda hp, qi: (hp, qi)), pl.BlockSpec((None, 8, tb), lambda hp, qi: (hp, 0, qi))],
        out_shape=[jax.ShapeDtypeStruct((D_MODEL, S), BF16), jax.ShapeDtypeStruct((HP, 8, S), F32)],
        scratch_shapes=[pltpu.VMEM((1, tb), F32)] * 4 + [pltpu.VMEM((HEAD_DIM, tb), F32)] * 2)


def _foxt_bwd(qkvT, qkv, ka, qtb, dotb, do, augqb, lseb, otb, *, tb, name, comm=()):
    S = qkv.shape[0]
    nb = S // tb

    def body(ka_ref, v_ref, kt_ref, qt_ref, dot_ref, qn_ref, dn_ref, aq_ref, ls_ref, ot_ref,
             dq_ref, dk_ref, dv_ref, df_ref, dr_ref, dq_sc, dk_sc, dv_sc, ds_sc, dr_sc):
        kj = pl.program_id(1)

        @pl.when(kj == 0)
        def _():
            dq_sc[...] = jnp.zeros_like(dq_sc)
            dr_sc[...] = jnp.zeros_like(dr_sc)

        dk_sc[...] = jnp.zeros_like(dk_sc)
        dv_sc[...] = jnp.zeros_like(dv_sc)
        ds_sc[...] = jnp.zeros_like(ds_sc)
        v2 = v_ref[...]
        kth = [kt_ref[HEAD_DIM * hh:HEAD_DIM * (hh + 1), :] for hh in range(2)]

        def block(qi, tiles):
            q0 = pl.multiple_of(qi * tb, tb)
            qa = _qa_pair(qt_ref[qi], aq_ref[qi])
            dot, ls = dot_ref[qi], ls_ref[qi]
            prod = ot_ref[qi].astype(F32) * dot.astype(F32)
            delta = [jnp.sum(prod[HEAD_DIM * hh:HEAD_DIM * (hh + 1)], axis=0, keepdims=True) for hh in range(2)]
            qn, dn = qn_ref[pl.ds(q0, tb), :], dn_ref[pl.ds(q0, tb), :]
            for ko, qo, n, masked in tiles:
                keys, cols = slice(ko, ko + n), slice(qo, qo + n)
                for hh in range(2):
                    s = jnp.dot(ka_ref[hh, keys, :], qa[hh][:, cols], preferred_element_type=F32)
                    if masked:
                        s = _diag_mask_t(s)
                    p = jnp.exp(s - ls[hh:hh + 1, cols])
                    dp = jnp.dot(v2[keys], _rows_of_head(dot[:, cols], hh), preferred_element_type=F32)
                    ds = p * (dp - delta[hh][:, cols])
                    dsb = ds.astype(BF16)
                    dv_sc[keys, :] += jnp.dot(p.astype(BF16), _head_mask(dn[cols], hh), preferred_element_type=F32)
                    dk_sc[keys, :] += jnp.dot(dsb, _head_mask(qn[cols], hh), preferred_element_type=F32)
                    dq_sc[qi, HEAD_DIM * hh:HEAD_DIM * (hh + 1), cols] += jnp.dot(kth[hh][:, keys], dsb, preferred_element_type=F32)
                    part = ds[:, 0:128]
                    for j in range(1, n // 128):
                        part = part + ds[:, 128 * j:128 * (j + 1)]
                    ds_sc[hh, keys, :] += part
                    dr_sc[qi, hh:hh + 1, cols] += jnp.sum(ds, axis=0, keepdims=True)

        def off_diagonal(i, c):
            block(kj + 1 + i, [(0, 0, tb, False)])
            return c

        half = tb // 2
        block(kj, [(ko * half, qo * half, half, masked) for ko, qo, masked in DIAGONAL_TILES])
        lax.fori_loop(0, nb - 1 - kj, off_diagonal, 0)
        dk_ref[...] = dk_sc[...].astype(dk_ref.dtype)
        dv_ref[...] = dv_sc[...].astype(dv_ref.dtype)
        df_ref[...] = _lanes_from([-jnp.sum(ds_sc[hh], axis=1, keepdims=True) for hh in range(2)], (tb, 128))

        @pl.when(kj == nb - 1)
        def _():
            dq_ref[...] = (dq_sc[...] * Q_SCALE).astype(dq_ref.dtype)
            dr_ref[...] = dr_sc[...]

    resident = lambda rows: pl.BlockSpec((None, nb) + rows, lambda hp, kj: (hp,) + (0,) * (len(rows) + 1))
    whole = pl.BlockSpec((S, 128), lambda hp, kj: (0, hp))
    kblk = lambda off: pl.BlockSpec((tb, 128), lambda hp, kj: (kj, off + hp))
    return _call(
        body, name=name, grid=(HP, nb), comm=comm, sem=("parallel", "arbitrary"),
        args=(ka, qkv, qkvT, qtb, dotb, qkv, do, augqb, lseb, otb),
        in_specs=[pl.BlockSpec((None, 2, tb, 128), lambda hp, kj: (hp, 0, kj, 0)), kblk(2 * HP),
                  pl.BlockSpec((128, tb), lambda hp, kj: (HP + hp, kj)),
                  resident((128, tb)), resident((128, tb)), whole, whole, resident((2, AUG_ROWS, tb)), resident((8, tb)),
                  resident((128, tb))],
        out_specs=[resident((128, tb)), kblk(0), kblk(0), pl.BlockSpec((None, tb, 128), lambda hp, kj: (hp, kj, 0)),
                   resident((8, tb))],
        out_shape=[jax.ShapeDtypeStruct((HP, nb, 128, tb), BF16), jax.ShapeDtypeStruct((S, D_MODEL), BF16),
                   jax.ShapeDtypeStruct((S, D_MODEL), BF16), jax.ShapeDtypeStruct((HP, S, 128), F32),
                   jax.ShapeDtypeStruct((HP, nb, 8, tb), F32)],
        scratch_shapes=[pltpu.VMEM((nb, 128, tb), F32), pltpu.VMEM((tb, 128), F32), pltpu.VMEM((tb, 128), F32),
                        pltpu.VMEM((2, tb, 128), F32), pltpu.VMEM((nb, 8, tb), F32)])


def _split3(x):
    rnd = lambda v: lax.reduce_precision(v, exponent_bits=8, mantissa_bits=7)
    hi = rnd(x)
    mid = rnd(x - hi)
    lo = rnd(x - hi - mid)
    return hi.astype(BF16), mid.astype(BF16), lo.astype(BF16)


def _blocked(xt, nb, tb):
    lead = xt.shape[:-1]
    x = xt.reshape(lead + (nb, tb))
    return jnp.moveaxis(x, -2, 1)


def _aug_rows(bias):
    ones = jnp.ones(bias.shape, BF16)
    zeros = jnp.zeros(bias.shape, BF16)
    return jnp.stack(list(_split3(bias)) + [ones] * 3 + [zeros] * (AUG_ROWS - 6), axis=2)


def _fox_keys(qkv, fcol, *, name):
    S = qkv.shape[0]
    tr = _tile(S, 2048, 16)

    def body(k_ref, f_ref, o_ref):
        hp = pl.program_id(1)
        k2 = k_ref[...]
        lanes = lax.broadcasted_iota(jnp.int32, (tr, 128), 1)
        for hh in range(2):
            nf = -jnp.sum(jnp.where(lanes == 2 * hp + hh, f_ref[...], 0.0), axis=1, keepdims=True)
            hi = nf.astype(BF16).astype(F32)
            mid = (nf - hi).astype(BF16).astype(F32)
            lo = nf - hi - mid
            base = HEAD_DIM * (1 - hh)
            aug = jnp.where((lanes >= base) & (lanes < base + 3), 1.0, 0.0)
            for j, piece in enumerate((hi, mid, lo)):
                aug = jnp.where(lanes == base + 3 + j, piece, aug)
            mine = (lanes >= HEAD_DIM * hh) & (lanes < HEAD_DIM * (hh + 1))
            o_ref[hh] = jnp.where(mine, k2, aug.astype(BF16))

    return pl.pallas_call(
        body, name=name, grid=(S // tr, HP),
        in_specs=[pl.BlockSpec((tr, 128), lambda i, hp: (i, HP + hp)), pl.BlockSpec((tr, 128), lambda i, hp: (i, 0))],
        out_specs=pl.BlockSpec((None, 2, tr, 128), lambda i, hp: (hp, 0, i, 0)),
        out_shape=jax.ShapeDtypeStruct((HP, 2, S, 128), BF16), compiler_params=_cp("parallel", "parallel"))(qkv, fcol)


HALF = S5_SUB * S5_P
NCB = 2 * HALF // 128
RE, IM = slice(0, NCB // 2), slice(NCB // 2, NCB)


def _gelu(x):
    return 0.5 * x * (1.0 + jnp.tanh(GELU_K * (x + 0.044715 * x * x * x)))


def _gelu_grad(x):
    t = jnp.tanh(GELU_K * (x + 0.044715 * x * x * x))
    return 0.5 * (1.0 + t) + 0.5 * x * (1.0 - t * t) * GELU_K * (1.0 + 3.0 * 0.044715 * x * x)


def _s5_put(buf, s, val, lt):
    for cb in range(NCB):
        buf[cb, pl.ds(s, lt, stride=S5_SUB), :] = val[:, 128 * cb:128 * (cb + 1)]


def _s5_get(buf, s, lt):
    return jnp.concatenate([buf[cb, pl.ds(s, lt, stride=S5_SUB), :] for cb in range(NCB)], axis=1)


def _s5_project_in(u_ref, wb_ref, buf, lt):
    for s in range(S5_SUB):
        _s5_put(buf, s, jnp.dot(u_ref[:, 128 * s:128 * (s + 1)], wb_ref[s], preferred_element_type=F32), lt)


def _s5_scan(buf, lam_ref, h0, lt):
    a_re, a_im = lam_ref[RE], lam_ref[IM]
    a2_re, a2_im = a_re * a_re - a_im * a_im, 2.0 * a_re * a_im

    def step(i, carry):
        hr, hi = carry
        r0 = pl.multiple_of(i * 2 * S5_SUB, 2 * S5_SUB)
        r1 = r0 + S5_SUB
        b0r, b0i = buf[RE, pl.ds(r0, S5_SUB), :], buf[IM, pl.ds(r0, S5_SUB), :]
        cr = a_re * b0r - a_im * b0i + buf[RE, pl.ds(r1, S5_SUB), :]
        ci = a_re * b0i + a_im * b0r + buf[IM, pl.ds(r1, S5_SUB), :]
        buf[RE, pl.ds(r0, S5_SUB), :] = a_re * hr - a_im * hi + b0r
        buf[IM, pl.ds(r0, S5_SUB), :] = a_re * hi + a_im * hr + b0i
        nr = a2_re * hr - a2_im * hi + cr
        ni = a2_re * hi + a2_im * hr + ci
        buf[RE, pl.ds(r1, S5_SUB), :] = nr
        buf[IM, pl.ds(r1, S5_SUB), :] = ni
        return nr, ni

    return lax.fori_loop(0, lt // 2, step, (h0[RE], h0[IM]))


def _s5_fwd(u, wb, wc, lam, dskip, *, lt, name, comm=()):
    S, D = u.shape
    nt = S // lt

    def body(u_ref, wb_ref, wc_ref, lam_ref, d_ref, yp_ref, yg_ref, st_ref, buf, h_sc):
        @pl.when(pl.program_id(0) == 0)
        def _():
            h_sc[...] = jnp.zeros_like(h_sc)

        st_ref[...] = h_sc[...]
        _s5_project_in(u_ref, wb_ref, buf, lt)
        hr, hi = _s5_scan(buf, lam_ref, h_sc[...], lt)
        h_sc[RE] = hr
        h_sc[IM] = hi
        for s in range(S5_SUB):
            cols = slice(128 * s, 128 * (s + 1))
            hs = _s5_get(buf, s, lt).astype(BF16)
            yp = jnp.dot(hs, wc_ref[s], preferred_element_type=F32) + d_ref[:, cols] * u_ref[:, cols].astype(F32)
            yp_ref[:, cols] = yp.astype(yp_ref.dtype)
            yg_ref[:, cols] = _gelu(yp).astype(yg_ref.dtype)

    blk = pl.BlockSpec((lt, D), lambda i: (i, 0))
    full = lambda shp: pl.BlockSpec(shp, lambda i: (0,) * len(shp))
    state = (NCB, S5_SUB, 128)
    return _call(
        body, name=name, grid=(nt,), comm=comm, sem=("arbitrary",), args=(u, wb, wc, lam, dskip),
        in_specs=[blk, full(wb.shape), full(wc.shape), full(lam.shape), full(dskip.shape)],
        out_specs=[blk, blk, pl.BlockSpec((None,) + state, lambda i: (i, 0, 0, 0))],
        out_shape=[jax.ShapeDtypeStruct((S, D), BF16), jax.ShapeDtypeStruct((S, D), BF16), jax.ShapeDtypeStruct((nt,) + state, F32)],
        scratch_shapes=[pltpu.VMEM((NCB, lt * S5_SUB, 128), F32), pltpu.VMEM(state, F32)])


def _s5_bwd(u, dyp, st, wb, wc, lam, dskip, *, lt, name, comm=()):
    S, D = u.shape
    nt = S // lt

    def body(u_ref, dy_ref, st_ref, wb_ref, wc_ref, lam_ref, d_ref, du_ref, dwb_ref, dwc_ref, dlam_ref, hbuf, gbuf, g_sc):
        @pl.when(pl.program_id(0) == 0)
        def _():
            g_sc[...] = jnp.zeros_like(g_sc)
            dwb_ref[...] = jnp.zeros_like(dwb_ref)
            dwc_ref[...] = jnp.zeros_like(dwc_ref)
            dlam_ref[...] = jnp.zeros_like(dlam_ref)

        _s5_project_in(u_ref, wb_ref, hbuf, lt)
        _s5_scan(hbuf, lam_ref, st_ref[...], lt)
        for s in range(S5_SUB):
            dys = dy_ref[:, 128 * s:128 * (s + 1)]
            _s5_put(gbuf, s, lax.dot_general(dys, wc_ref[s], NT_DIMS, preferred_element_type=F32), lt)
        a_re, a_im = lam_ref[RE], lam_ref[IM]

        def one(t, carry, hp_re, hp_im):
            gr, gi, dar, dai = carry
            r0 = pl.multiple_of(t * S5_SUB, S5_SUB)
            nr = gbuf[RE, pl.ds(r0, S5_SUB), :] + a_re * gr + a_im * gi
            ni = gbuf[IM, pl.ds(r0, S5_SUB), :] + a_re * gi - a_im * gr
            gbuf[RE, pl.ds(r0, S5_SUB), :] = nr
            gbuf[IM, pl.ds(r0, S5_SUB), :] = ni
            return nr, ni, dar + nr * hp_re + ni * hp_im, dai + ni * hp_re - nr * hp_im

        def step(k, carry):
            t = lt - 1 - k
            p0 = pl.multiple_of((t - 1) * S5_SUB, S5_SUB)
            return one(t, carry, hbuf[RE, pl.ds(p0, S5_SUB), :], hbuf[IM, pl.ds(p0, S5_SUB), :])

        init = (g_sc[RE], g_sc[IM], dlam_ref[RE], dlam_ref[IM])
        carry = lax.fori_loop(0, lt - 1, step, init, unroll=2)
        gr, gi, dar, dai = one(0, carry, st_ref[RE], st_ref[IM])
        g_sc[RE] = gr
        g_sc[IM] = gi
        dlam_ref[RE] = dar
        dlam_ref[IM] = dai
        for s in range(S5_SUB):
            cols = slice(128 * s, 128 * (s + 1))
            gs = _s5_get(gbuf, s, lt).astype(BF16)
            hs = _s5_get(hbuf, s, lt).astype(BF16)
            us, dys = u_ref[:, cols], dy_ref[:, cols]
            du = lax.dot_general(gs, wb_ref[s], NT_DIMS, preferred_element_type=F32) + d_ref[:, cols] * dys.astype(F32)
            du_ref[:, cols] = du.astype(du_ref.dtype)
            dwb_ref[s] += lax.dot_general(us, gs, TN_DIMS, preferred_element_type=F32)
            dwc_ref[s] += lax.dot_general(hs, dys, TN_DIMS, preferred_element_type=F32)

    blk = pl.BlockSpec((lt, D), lambda i: (nt - 1 - i, 0))
    full = lambda shp: pl.BlockSpec(shp, lambda i: (0,) * len(shp))
    state = (NCB, S5_SUB, 128)
    words = pltpu.VMEM((NCB, lt * S5_SUB, 128), F32)
    return _call(
        body, name=name, grid=(nt,), comm=comm, sem=("arbitrary",), args=(u, dyp, st, wb, wc, lam, dskip),
        in_specs=[blk, blk, pl.BlockSpec((None,) + state, lambda i: (nt - 1 - i, 0, 0, 0)),
                  full(wb.shape), full(wc.shape), full(lam.shape), full(dskip.shape)],
        out_specs=[blk, full(wb.shape), full(wc.shape), full(lam.shape)],
        out_shape=[jax.ShapeDtypeStruct((S, D), BF16), jax.ShapeDtypeStruct(wb.shape, F32), jax.ShapeDtypeStruct(wc.shape, F32),
                   jax.ShapeDtypeStruct(lam.shape, F32)],
        scratch_shapes=[words, words, pltpu.VMEM(state, F32)])


def _s5_disc(lam_re, lam_im, log_dt, b_re, b_im):
    dt = jnp.exp(log_dt)[:, None]
    mag = jnp.exp(lam_re * dt)
    lb_re, lb_im = mag * jnp.cos(lam_im * dt), mag * jnp.sin(lam_im * dt)
    num_re = lb_re - 1.0
    den = lam_re * lam_re + lam_im * lam_im
    k_re = (num_re * lam_re + lb_im * lam_im) / den
    k_im = (lb_im * lam_re - num_re * lam_im) / den
    bb_re = k_re[..., None] * b_re - k_im[..., None] * b_im
    bb_im = k_re[..., None] * b_im + k_im[..., None] * b_re
    return lb_re, lb_im, bb_re, bb_im


def _s5_pack(lb_re, lb_im, bb_re, bb_im, c_re, c_im):
    eye = jnp.eye(S5_SUB, dtype=F32)
    tb = lambda x: jnp.einsum("sgcp,gh->sgchp", x.reshape(S5_SUB, S5_SUB, S5_P, S5_C).transpose(0, 1, 3, 2), eye).reshape(S5_SUB, 128, HALF)
    tc = lambda x: jnp.einsum("sgpc,gh->sgphc", x.reshape(S5_SUB, S5_SUB, S5_C, S5_P).transpose(0, 1, 3, 2), eye).reshape(S5_SUB, HALF, 128)
    wb = jnp.concatenate([tb(bb_re), tb(bb_im)], axis=2).astype(BF16)
    wc = jnp.concatenate([tc(c_re), -tc(c_im)], axis=1).astype(BF16)
    lam = jnp.concatenate([lb_re.reshape(S5_SUB, HALF), lb_im.reshape(S5_SUB, HALF)], axis=1)
    return wb, wc, lam.reshape(S5_SUB, NCB, 128).transpose(1, 0, 2)


def _s5_unpack(dwb, dwc, dlam):
    db = jnp.einsum("sgcrgp->rsgpc", dwb.reshape(S5_SUB, S5_SUB, S5_C, 2, S5_SUB, S5_P)).reshape(2, S5_G, S5_P, S5_C)
    dc = jnp.einsum("srgpgc->rsgcp", dwc.reshape(S5_SUB, 2, S5_SUB, S5_P, S5_SUB, S5_C)).reshape(2, S5_G, S5_C, S5_P)
    dlam = dlam.transpose(1, 0, 2).reshape(S5_SUB, 2 * HALF)
    return (dlam[:, :HALF].reshape(S5_G, S5_P), dlam[:, HALF:].reshape(S5_G, S5_P), db[0], db[1], dc[0], -dc[1])


def _glu_fwd(yg, gl, *, name):
    S, D = yg.shape
    tm = _tile(S, 512, 8)
    blk = pl.BlockSpec((tm, D), lambda i: (i, 0))

    def body(y_ref, g_ref, o_ref):
        o_ref[...] = (y_ref[...].astype(F32) * jax.nn.sigmoid(g_ref[...].astype(F32))).astype(o_ref.dtype)

    return pl.pallas_call(body, name=name, grid=(S // tm,), in_specs=[blk, blk], out_specs=blk,
                          out_shape=jax.ShapeDtypeStruct((S, D), BF16), compiler_params=_cp("parallel"))(yg, gl)


def _glu_bwd(dy2, yg, gl, *, name):
    S, D = yg.shape
    tm = _tile(S, 512, 8)
    blk = pl.BlockSpec((tm, D), lambda i: (i, 0))

    def body(d_ref, y_ref, g_ref, da_ref, dg_ref):
        d = d_ref[...].astype(F32)
        sg = jax.nn.sigmoid(g_ref[...].astype(F32))
        da_ref[...] = d * sg
        dg_ref[...] = (d * y_ref[...].astype(F32) * sg * (1.0 - sg)).astype(dg_ref.dtype)

    return pl.pallas_call(body, name=name, grid=(S // tm,), in_specs=[blk, blk, blk], out_specs=[blk, blk],
                          out_shape=[jax.ShapeDtypeStruct((S, D), F32), jax.ShapeDtypeStruct((S, D), BF16)],
                          compiler_params=_cp("parallel"))(dy2, yg, gl)


def _gelu_bwd(da, db, ypre, u, *, name):
    S, D = ypre.shape
    tm = _tile(S, 512, 8)
    blk = pl.BlockSpec((tm, D), lambda i: (i, 0))
    vec = pl.BlockSpec((1, D), lambda i: (0, 0))

    def body(a_ref, b_ref, y_ref, u_ref, o_ref, s_ref):
        @pl.when(pl.program_id(0) == 0)
        def _():
            s_ref[...] = jnp.zeros_like(s_ref)

        dy = (a_ref[...] + b_ref[...]) * _gelu_grad(y_ref[...].astype(F32))
        o_ref[...] = dy.astype(o_ref.dtype)
        s_ref[...] += jnp.sum(dy * u_ref[...].astype(F32), axis=0, keepdims=True)

    return pl.pallas_call(body, name=name, grid=(S // tm,), in_specs=[blk, blk, blk, blk], out_specs=[blk, vec],
                          out_shape=[jax.ShapeDtypeStruct((S, D), BF16), jax.ShapeDtypeStruct((1, D), F32)],
                          compiler_params=_cp("arbitrary"))(da, db, ypre, u)


def _my_index():
    return 4 * lax.axis_index("x") + 2 * lax.axis_index("y") + lax.axis_index("c")


def _comm_plan(payloads):
    n = len(payloads)
    hbm = pl.BlockSpec(memory_space=pl.ANY)
    shapes = [jax.ShapeDtypeStruct((N_DEV,) + (x.shape if g else x.shape[1:]), x.dtype) for x, g in payloads]
    sems = [pltpu.SemaphoreType.DMA((n, N_DEV - 1)), pltpu.SemaphoreType.DMA((n, N_DEV - 1)), pltpu.SemaphoreType.DMA((n,))] if n else []
    return [hbm] * n, [hbm] * n, shapes, sems


def _comm_copies(x_refs, o_refs, gathers, send_sems, recv_sems, local_sems):
    mx, my, mc = lax.axis_index("x"), lax.axis_index("y"), lax.axis_index("c")
    me = 4 * mx + 2 * my + mc
    copies = []
    for i, (x_ref, o_ref, gather) in enumerate(zip(x_refs, o_refs, gathers)):
        src = (lambda j, r=x_ref: r) if gather else (lambda j, r=x_ref: r.at[j])
        copies.append(pltpu.make_async_copy(src(me), o_ref.at[me], local_sems.at[i]))
        for k in range(1, N_DEV):
            px, py, pc = mx ^ (k >> 2), my ^ ((k >> 1) & 1), mc ^ (k & 1)
            copies.append(pltpu.make_async_remote_copy(src(4 * px + 2 * py + pc), o_ref.at[me], send_sems.at[i, k - 1],
                                                       recv_sems.at[i, k - 1], device_id=(px, py, pc), device_id_type=MESH_ID))
    return copies


def _call(body, *, name, grid, in_specs, out_specs, out_shape, scratch_shapes, sem, args, comm=()):
    if not comm:
        outs = pl.pallas_call(body, name=name, grid=grid, in_specs=in_specs, out_specs=out_specs, out_shape=out_shape,
                              scratch_shapes=scratch_shapes, compiler_params=_cp(*sem))(*args)
        return outs, ()
    n, n_in, n_out = len(comm), len(in_specs), len(out_specs)
    c_in, c_out, c_shapes, c_sems = _comm_plan(comm)
    gathers = [g for _, g in comm]

    def wrapped(*refs):
        own_in, x_refs = refs[:n_in], refs[n_in:n_in + n]
        own_out, o_refs = refs[n_in + n:n_in + n + n_out], refs[n_in + n + n_out:n_in + 2 * n + n_out]
        own_scratch, sems = refs[n_in + 2 * n + n_out:-3], refs[-3:]
        ids = [pl.program_id(a) for a in range(len(grid))]
        first = functools.reduce(jnp.logical_and, [i == 0 for i in ids])
        last = functools.reduce(jnp.logical_and, [i == g - 1 for i, g in zip(ids, grid)])

        @pl.when(first)
        def _():
            for cp in _comm_copies(x_refs, o_refs, gathers, *sems):
                cp.start()

        body(*own_in, *own_out, *own_scratch)

        @pl.when(last)
        def _():
            for cp in _comm_copies(x_refs, o_refs, gathers, *sems):
                cp.wait()

    outs = pl.pallas_call(wrapped, name=name, grid=grid, in_specs=list(in_specs) + c_in, out_specs=list(out_specs) + c_out,
                          out_shape=list(out_shape) + c_shapes, scratch_shapes=list(scratch_shapes) + c_sems,
                          compiler_params=_cp(*["arbitrary"] * len(grid)))(*args, *[x for x, _ in comm])
    return outs[:n_out], outs[n_out:]


def _exchange_many(payloads, *, name):
    n = len(payloads)
    in_specs, out_specs, shapes, sems = _comm_plan(payloads)
    gathers = [g for _, g in payloads]

    def body(*refs):
        copies = _comm_copies(refs[:n], refs[n:2 * n], gathers, *refs[2 * n:])
        for cp in copies:
            cp.start()
        for cp in copies:
            cp.wait()

    return pl.pallas_call(body, name=name, in_specs=in_specs, out_specs=out_specs, out_shape=shapes, scratch_shapes=sems,
                          compiler_params=pltpu.CompilerParams(has_side_effects=True))(*[x for x, _ in payloads])


def _exchange(x, *, gather, name):
    return _exchange_many([(x, gather)], name=name)[0]


def _sum8(x8, *, name):
    _, R, C = x8.shape
    tr = _tile(R, 256, 16)

    def body(x_ref, o_ref):
        acc = x_ref[0].astype(F32)
        for j in range(1, N_DEV):
            acc = acc + x_ref[j].astype(F32)
        o_ref[...] = acc

    return pl.pallas_call(body, name=name, grid=(R // tr,), in_specs=[pl.BlockSpec((N_DEV, tr, C), lambda i: (0, i, 0))],
                          out_specs=pl.BlockSpec((tr, C), lambda i: (i, 0)), out_shape=jax.ShapeDtypeStruct((R, C), F32),
                          compiler_params=_cp("parallel"))(x8)


def _pack(arrs, dtype, row_mult):
    flat = jnp.concatenate([a.reshape(-1).astype(dtype) for a in arrs])
    n = flat.shape[0]
    rows = -(-n // 1024)
    rows = -(-rows // row_mult) * row_mult
    return jnp.pad(flat, (0, rows * 1024 - n)).reshape(rows, 1024)


def _unpack(buf, shapes):
    lead = buf.shape[:-2]
    flat = buf.reshape(lead + (-1,))
    out, off = [], 0
    for s in shapes:
        n = math.prod(s)
        out.append(flat[..., off:off + n].reshape(lead + tuple(s)))
        off += n
    return out


def _adaln_fwd(c_all, w, b, *, name):
    nk, D, n = w.shape

    def body(c_ref, w_ref, b_ref, o_ref):
        cv = c_ref[...]
        sc = (cv * jax.nn.sigmoid(cv)).astype(BF16)
        o_ref[...] = jnp.dot(sc, w_ref[...].astype(BF16), preferred_element_type=F32) + b_ref[...]

    return pl.pallas_call(body, name=name, grid=(nk,),
                          in_specs=[pl.BlockSpec((N_DEV, D), lambda k: (0, 0)), pl.BlockSpec((None, D, n), lambda k: (k, 0, 0)),
                                    pl.BlockSpec((None, 1, n), lambda k: (k, 0, 0))],
                          out_specs=pl.BlockSpec((None, N_DEV, n), lambda k: (k, 0, 0)),
                          out_shape=jax.ShapeDtypeStruct((nk, N_DEV, n), F32), compiler_params=_cp("parallel"))(c_all, w, b)


def _adaln_bwd(c_all, dmod, *, name):
    nk, _, n = dmod.shape
    D = c_all.shape[1]

    def body(c_ref, d_ref, dw_ref, db_ref):
        cv = c_ref[...]
        sc = (cv * jax.nn.sigmoid(cv)).astype(BF16).astype(F32)
        dm = d_ref[...]
        dw_ref[...] = lax.dot_general(sc, dm.astype(BF16).astype(F32), TN_DIMS, precision=lax.Precision.HIGHEST,
                                      preferred_element_type=F32)
        db_ref[...] = jnp.sum(dm, axis=0, keepdims=True)

    return pl.pallas_call(body, name=name, grid=(nk,),
                          in_specs=[pl.BlockSpec((N_DEV, D), lambda k: (0, 0)), pl.BlockSpec((None, N_DEV, n), lambda k: (k, 0, 0))],
                          out_specs=[pl.BlockSpec((None, D, n), lambda k: (k, 0, 0)), pl.BlockSpec((None, 1, n), lambda k: (k, 0, 0))],
                          out_shape=[jax.ShapeDtypeStruct((nk, D, n), F32), jax.ShapeDtypeStruct((nk, 1, n), F32)],
                          compiler_params=_cp("parallel"))(c_all, dmod)


def _adamw(w, g, m, v, *, name):
    R, C = w.shape
    tr = _tile(R, 256, 8)
    blk = pl.BlockSpec((tr, C), lambda i: (i, 0))
    c1 = 1.0 / (1.0 - ADAM_B1 ** ADAM_STEP)
    c2 = 1.0 / (1.0 - ADAM_B2 ** ADAM_STEP)

    def body(w_ref, g_ref, m_ref, v_ref, d_ref, nm_ref, nv_ref):
        gv = g_ref[...]
        nm = ADAM_B1 * m_ref[...] + (1.0 - ADAM_B1) * gv
        nv = ADAM_B2 * v_ref[...] + (1.0 - ADAM_B2) * (gv * gv)
        nm_ref[...] = nm
        nv_ref[...] = nv
        d_ref[...] = -ADAM_LR * ((nm * c1) / (jnp.sqrt(nv * c2) + ADAM_EPS) + ADAM_WD * w_ref[...])

    sh = jax.ShapeDtypeStruct((R, C), F32)
    return pl.pallas_call(body, name=name, grid=(R // tr,), in_specs=[blk] * 4, out_specs=[blk] * 3, out_shape=[sh] * 3,
                          compiler_params=_cp("parallel"))(w, g, m, v)


def _adamw_nd(w, g, m, v, *, name):
    shp = w.shape
    two = (-1, shp[-1]) if w.ndim > 1 else (1, -1)
    outs = _adamw(w.reshape(two), g.reshape(two), m.reshape(two), v.reshape(two), name=name)
    return [o.reshape(shp) for o in outs]


S5_TILE = 256
ATTN_BLOCK = 1024
BIG = ("s5_w_in", "s5_w_glu", "s5_w_out", "fox_w_in", "fox_w_out", "ffn_w_up", "ffn_w_down")
WEIGHTS = ("norm_g", "ada_w", "ada_b", "s5_w_in", "s5_lam_re", "s5_lam_im", "s5_log_dt", "s5_b_re", "s5_b_im", "s5_c_re",
           "s5_c_im", "s5_d", "s5_w_glu", "s5_w_out", "fox_w_in", "fox_b_f", "fox_w_out", "ffn_w_up", "ffn_conv_w",
           "ffn_conv_b", "ffn_w_down", "final_g")
REPLICATED = ("s5_lam_re", "s5_lam_im", "s5_log_dt", "s5_b_re", "s5_b_im", "s5_c_re", "s5_c_im", "s5_d", "fox_b_f",
              "ffn_conv_b", "final_g")


def _row_group(arrs):
    return jnp.concatenate([a.reshape(-1, D_MODEL) for a in arrs], axis=0)


def _cols_join(g):
    return jnp.concatenate([g[j] for j in range(N_DEV)], axis=1)


def _cols_split(full):
    n = full.shape[1] // N_DEV
    return jnp.stack([full[:, j * n:(j + 1) * n] for j in range(N_DEV)])


def kernel(x, c, norm_g, ada_w, ada_b, s5_w_in, s5_lam_re, s5_lam_im, s5_log_dt, s5_b_re, s5_b_im, s5_c_re, s5_c_im, s5_d, s5_w_glu, s5_w_out, fox_w_in, fox_b_f, fox_w_out, ffn_w_up, ffn_conv_w, ffn_conv_b, ffn_w_down, final_g, loss_target, m_norm_g, m_ada_w, m_ada_b, m_s5_w_in, m_s5_lam_re, m_s5_lam_im, m_s5_log_dt, m_s5_b_re, m_s5_b_im, m_s5_c_re, m_s5_c_im, m_s5_d, m_s5_w_glu, m_s5_w_out, m_fox_w_in, m_fox_b_f, m_fox_w_out, m_ffn_w_up, m_ffn_conv_w, m_ffn_conv_b, m_ffn_w_down, m_final_g, v_norm_g, v_ada_w, v_ada_b, v_s5_w_in, v_s5_lam_re, v_s5_lam_im, v_s5_log_dt, v_s5_b_re, v_s5_b_im, v_s5_c_re, v_s5_c_im, v_s5_d, v_s5_w_glu, v_s5_w_out, v_fox_w_in, v_fox_b_f, v_fox_w_out, v_ffn_w_up, v_ffn_conv_w, v_ffn_conv_b, v_ffn_w_down, v_final_g):
    args = dict(locals())
    W = {n: args[n] for n in WEIGHTS}
    M = {n: args["m_" + n] for n in WEIGHTS}
    V = {n: args["v_" + n] for n in WEIGHTS}
    D, F = D_MODEL, D_FF
    me = _my_index()
    h0 = x[0]
    S = h0.shape[0]
    lt = min(S5_TILE, S)
    tb = min(ATTN_BLOCK, S)

    b16 = lambda a: a.astype(BF16)
    g0, g_in = _exchange_many([(_pack([c, norm_g, ffn_conv_w], F32, 8), True), (b16(s5_w_in[0]), True)], name="gather_first")
    c_all, ng_all, cw_all = _unpack(g0, [(D,), norm_g.shape, ffn_conv_w.shape])
    ng_full = ng_all.transpose(1, 2, 0, 3).reshape(2, 2, D)
    cw_full = cw_all.transpose(1, 2, 0, 3).reshape(2, 3, F)

    ncol = ada_w.shape[-1]
    modp = _adaln_fwd(c_all, ada_w.reshape(4, D, ncol), ada_b.reshape(4, 1, ncol), name="adaln_fwd")
    g1 = _exchange(_pack([modp], F32, 8), gather=True, name="gather_adaln")
    (mod_all,) = _unpack(g1, [modp.shape])
    mod = lax.dynamic_index_in_dim(mod_all, me, axis=2, keepdims=False).transpose(1, 0, 2).reshape(4, 3 * D)
    shift = [mod[k:k + 1, :D] for k in range(4)]
    scale = [mod[k:k + 1, D:2 * D] for k in range(4)]
    gate = [mod[k:k + 1, 2 * D:] for k in range(4)]
    gain = [_row(ng_full[k // 2, k % 2]) for k in range(4)]

    rows = D // N_DEV
    full = {"s5_w_in": g_in.reshape(D, D)}
    bf_pad = jnp.pad(fox_b_f, ((0, 0), (0, 128 - HEADS)))

    def ffn_fwd(h, k, layer, comm=()):
        hn = _modulate(h, gain[k], shift[k], scale[k], name=f"modulate{k}")
        up = _mm(hn, full[f"ffn_w_up{layer}"], name=f"ffn_up{layer}", o_h=True, out_dtype=BF16, tn=1408, comm=comm)
        up, exchanged = up if comm else (up, ())
        z = _conv_gate_fwd(up, cw_full[layer], ffn_conv_b[layer:layer + 1], name=f"conv_gate{layer}")
        m, h_out = _mm(z, full[f"ffn_w_down{layer}"], name=f"ffn_down{layer}", res=h, gate=gate[k])
        return h_out, (hn, up, z, m), exchanged

    lb_re, lb_im, bb_re, bb_im = _s5_disc(s5_lam_re[0], s5_lam_im[0], s5_log_dt[0], s5_b_re[0], s5_b_im[0])
    wb, wc, lam = _s5_pack(lb_re, lb_im, bb_re, bb_im, s5_c_re[0], s5_c_im[0])
    hn0 = _modulate(h0, gain[0], shift[0], scale[0], name="modulate0")
    u = _mm(hn0, full["s5_w_in"], name="s5_in", out_dtype=BF16)
    (ypre, yg, st), (g_up0, g_rows0) = _s5_fwd(
        u, wb, wc, lam, s5_d, lt=lt, name="s5_scan",
        comm=[(b16(ffn_w_up[0]), True), (_row_group([b16(s5_w_glu), b16(s5_w_out), b16(ffn_w_down[0])]), True)])
    full["ffn_w_up0"] = _cols_join(g_up0)
    full["s5_w_glu"] = g_rows0[:, :rows].reshape(D, D)
    full["s5_w_out"] = g_rows0[:, rows:2 * rows].reshape(D, D)
    full["ffn_w_down0"] = g_rows0[:, 2 * rows:].reshape(F, D)
    gl = _mm(yg, full["s5_w_glu"], name="s5_glu", out_dtype=BF16)
    y2 = _glu_fwd(yg, gl, name="s5_glu_gate")
    m0, h1 = _mm(y2, full["s5_w_out"], name="s5_out", res=h0, gate=gate[0])
    h2, (hn1, up0, z0, m1), (g_fox, g_fox_out) = ffn_fwd(h1, 1, 0, comm=[(b16(fox_w_in[0]), True), (b16(fox_w_out[0]), True)])
    full["fox_w_in"] = _cols_join(g_fox)
    full["fox_w_out"] = g_fox_out.reshape(D, D)
    w_proj = jnp.pad(full["fox_w_in"], ((0, 0), (0, 3 * D + 128 - full["fox_w_in"].shape[1])))
    w_qkv, w_f = w_proj[:, :3 * D], w_proj[:, 3 * D:]

    hn2 = _modulate(h2, gain[2], shift[2], scale[2], name="modulate2")
    nb = S // tb
    w_qkv_s = jnp.concatenate([w_qkv[:, :D] * Q_SCALE, w_qkv[:, D:]], axis=1)
    qkv, qkvT = _mm(hn2, w_qkv_s, name="fox_qkv", out_dtype=BF16, with_t=True)
    fl = _mm(hn2, w_f, name="fox_f", out_dtype=F32)
    fcol = _fgate_fwd(fl, bf_pad, name="fox_fgate")
    f_heads = fcol[:, :HEADS].T.reshape(HP, 2, S)
    ka = _fox_keys(qkv, fcol, name="fox_keys")
    vtb = _blocked(qkvT[2 * D:].reshape(HP, 128, S), nb, tb)
    aug = _aug_rows(f_heads)
    (oT, lse), (g_up1, g_down1) = _foxt_fwd(qkvT, ka, vtb, aug, tb=tb, name="fox_attn",
                                            comm=[(b16(ffn_w_up[1]), True), (b16(ffn_w_down[1]), True)])
    full["ffn_w_up1"] = _cols_join(g_up1)
    full["ffn_w_down1"] = g_down1.reshape(F, D)
    m2, h3 = _mm(oT, full["fox_w_out"], name="fox_out", ta=True, res=h2, gate=gate[2])
    h4, (hn3, up1, z1, m3), _ = ffn_fwd(h3, 3, 1)

    lblk, dh, head_sums, dm = _loss_head(h4, _row(final_g), loss_target[0], m3, gate[3], name="loss_head")
    d_final_g, dgate = head_sums[0:1], head_sums[1:2]
    loss = lax.psum(lblk[0, 0], ("x", "y", "c"))

    G = {}
    dmod = [None] * 4
    mixer_out = [m0, m1, m2, m3]

    def norm_bwd(h, dhn, dh_in, k, dgate_k):
        if k == 0:
            dh_out, sums = _modulate_bwd(h, dhn, dh_in, gain[k], scale[k], name=f"modulate_bwd{k}")
            dm_below = dgate_below = None
        else:
            dh_out, sums, dm_below = _modulate_bwd(h, dhn, dh_in, gain[k], scale[k], mixer_out[k - 1], gate[k - 1],
                                                   name=f"modulate_bwd{k}")
            dgate_below = sums[3:4]
        dmod[k] = jnp.concatenate([sums[0:1], sums[1:2], dgate_k], axis=1)
        return dh_out, sums[2], dm_below, dgate_below

    def ffn_bwd(dh_in, dm, dgate_k, h, k, layer, saved):
        hn, up, z, m = saved
        dz = _mm(dm, full[f"ffn_w_down{layer}"], name=f"ffn_down_dx{layer}", tb=True, out_dtype=BF16, tn=1408)
        dw_down = _mm(z, dm, name=f"ffn_down_dw{layer}", ta=True, tm=1408, tk=2048)
        d_up, cs = _conv_gate_bwd(up, dz, cw_full[layer], ffn_conv_b[layer:layer + 1], name=f"conv_gate_bwd{layer}")
        dhn = _mm(d_up, full[f"ffn_w_up{layer}"], name=f"ffn_up_dx{layer}", tb=True, a_h=True, tk=2816)
        dw_up = _mm(hn, d_up, name=f"ffn_up_dw{layer}", ta=True, b_h=True, tm=1024, tn=1408, tk=2048)
        dh_out, dg, dm_below, dgate_below = norm_bwd(h, dhn, dh_in, k, dgate_k)
        return dh_out, dm_below, dgate_below, dg, dw_up, dw_down, cs[0:3], cs[3]

    dh, dm, dgate, dg3, dw_up1, dw_down1, dcw1, dcb1 = ffn_bwd(dh, dm, dgate, h3, 3, 1, (hn3, up1, z1, m3))

    do, doT = _mm(dm, full["fox_w_out"], name="fox_out_dx", tb=True, out_dtype=BF16, with_t=True)
    dw_fox_out = _mm(oT, dm, name="fox_out_dw", tk=2048)
    to_rows = lambda g: b16(g).reshape(N_DEV, -1, D)
    (dqtb, dk, dv, dfk, dfq), (r_up1, r_down1) = _foxt_bwd(
        qkvT, qkv, ka, _blocked(qkvT[:D].reshape(HP, 128, S), nb, tb), _blocked(doT.reshape(HP, 128, S), nb, tb), do,
        _blocked(aug, nb, tb), _blocked(lse, nb, tb), _blocked(oT.reshape(HP, 128, S), nb, tb), tb=tb, name="fox_attn_bwd",
        comm=[(_cols_split(b16(dw_up1)), False), (to_rows(dw_down1), False)])
    dq = dqtb.transpose(1, 3, 0, 2).reshape(S, D)
    dF = dfk[:, :, :2].transpose(1, 0, 2).reshape(S, HEADS) + dfq[:, :, :2, :].transpose(1, 3, 0, 2).reshape(S, HEADS)
    dF = jnp.pad(dF, ((0, 0), (0, 128 - HEADS)))
    dfl, dbf = _fgate_bwd(dF, fl, bf_pad, name="fox_fgate_bwd")
    dproj = jnp.concatenate([dq, dk, dv, dfl.astype(BF16)], axis=1)
    dhn = _mm(dproj, w_proj, name="fox_in_dx", tb=True)
    dw_proj = _mm(hn2, dproj, name="fox_in_dw", ta=True, tn=640, tk=2048)
    dw_fox_in = dw_proj[:, :full["fox_w_in"].shape[1]]
    dh, dg2, dm, dgate = norm_bwd(h2, dhn, dh, 2, dgate)

    dh, dm, dgate, dg1, dw_up0, dw_down0, dcw0, dcb0 = ffn_bwd(dh, dm, dgate, h1, 1, 0, (hn1, up0, z0, m1))

    dy2 = _mm(dm, full["s5_w_out"], name="s5_out_dx", tb=True, out_dtype=BF16)
    G["s5_w_out"] = _mm(y2, dm, name="s5_out_dw", ta=True, tk=2048)
    da, dgl = _glu_bwd(dy2, yg, gl, name="s5_glu_bwd")
    dyg_b = _mm(dgl, full["s5_w_glu"], name="s5_glu_dx", tb=True)
    G["s5_w_glu"] = _mm(yg, dgl, name="s5_glu_dw", ta=True, tk=2048)
    dyp, dd = _gelu_bwd(da, dyg_b, ypre, u, name="s5_gelu_bwd")
    row_grads = [dw_fox_out, dw_down0, G["s5_w_out"], G["s5_w_glu"]]
    (du, dwb, dwc, dlam), (r_up0, r_rows0, r_fox) = _s5_bwd(
        u, dyp, st, wb, wc, lam, s5_d, lt=lt, name="s5_scan_bwd",
        comm=[(_cols_split(b16(dw_up0)), False), (jnp.concatenate([to_rows(g) for g in row_grads], axis=1), False),
              (_cols_split(b16(dw_fox_in)), False)])
    dhn = _mm(du, full["s5_w_in"], name="s5_in_dx", tb=True)
    G["s5_w_in"] = _mm(hn0, du, name="s5_in_dw", ta=True, tk=2048)
    dh, dg0, _, _ = norm_bwd(h0, dhn, dh, 0, dgate)
    grad_x = dh[None]

    dlb_re, dlb_im, dbb_re, dbb_im, dc_re, dc_im = _s5_unpack(dwb, dwc, dlam)
    _, disc_vjp = jax.vjp(_s5_disc, s5_lam_re[0], s5_lam_im[0], s5_log_dt[0], s5_b_re[0], s5_b_im[0])
    dlam_re, dlam_im, dlog_dt, db_re, db_im = disc_vjp((dlb_re, dlb_im, dbb_re, dbb_im))

    small_full = {
        "norm_g": jnp.stack([dg0, dg1, dg2, dg3]).reshape(2, 2, D),
        "ffn_conv_w": jnp.stack([dcw0, dcw1]),
        "s5_lam_re": dlam_re[None], "s5_lam_im": dlam_im[None], "s5_log_dt": dlog_dt[None],
        "s5_b_re": db_re[None], "s5_b_im": db_im[None], "s5_c_re": dc_re[None], "s5_c_im": dc_im[None],
        "s5_d": dd, "fox_b_f": dbf[:, :HEADS], "ffn_conv_b": jnp.stack([dcb0, dcb1]), "final_g": d_final_g[0],
    }
    names = tuple(small_full)
    small_pack = _pack([small_full[n] for n in names], F32, 8 * N_DEV)
    r_in, r_small, g2 = _exchange_many([(to_rows(G["s5_w_in"]), False), (small_pack.reshape(N_DEV, -1, 1024), False),
                                        (_pack([jnp.concatenate(dmod, axis=0)], F32, 8), True)], name="last_exchange")
    G["s5_w_in"] = _sum8(r_in, name="sum_s5_in_grads")[None]
    (dmod_all,) = _unpack(g2, [(4, 3 * D)])
    d_ada_w, d_ada_b = _adaln_bwd(c_all, _columns_of_mod(dmod_all, me, ncol), name="adaln_bwd")
    G["ada_w"] = d_ada_w.reshape(ada_w.shape)
    G["ada_b"] = d_ada_b.reshape(ada_b.shape)
    rows0_sum = _sum8(r_rows0, name="sum_rows0_grads")
    down_rows = F // N_DEV
    G["fox_w_out"] = rows0_sum[:rows][None]
    G["s5_w_out"] = rows0_sum[rows + down_rows:2 * rows + down_rows][None]
    G["s5_w_glu"] = rows0_sum[2 * rows + down_rows:][None]
    G["ffn_w_down"] = jnp.stack([rows0_sum[rows:rows + down_rows], _sum8(r_down1, name="sum_down1_grads")])
    G["ffn_w_up"] = jnp.stack([_sum8(r_up0, name="sum_up0_grads"), _sum8(r_up1, name="sum_up1_grads")])
    G["fox_w_in"] = _sum8(r_fox, name="sum_fox_w_in")[None]

    g3 = _exchange(_sum8(r_small, name="sum_small_grads"), gather=True, name="gather_small_sums")
    summed = dict(zip(names, _unpack(g3.reshape(small_pack.shape), [small_full[n].shape for n in names])))
    G["norm_g"] = lax.dynamic_slice_in_dim(summed["norm_g"], me * norm_g.shape[-1], norm_g.shape[-1], axis=2)
    G["ffn_conv_w"] = lax.dynamic_slice_in_dim(summed["ffn_conv_w"], me * ffn_conv_w.shape[-1], ffn_conv_w.shape[-1], axis=2)
    for n in REPLICATED:
        G[n] = summed[n]

    delta, new_m, new_v = {}, {}, {}
    small = tuple(n for n in WEIGHTS if n not in BIG and n != "ada_w")
    for n in WEIGHTS:
        if n not in small:
            delta[n], new_m[n], new_v[n] = _adamw_nd(W[n], G[n], M[n], V[n], name=f"adamw_{n}")
    packed = [_pack([src[n] for n in small], F32, 8) for src in (W, G, M, V)]
    for dst, buf in zip((delta, new_m, new_v), _adamw(*packed, name="adamw_small")):
        dst.update(zip(small, _unpack(buf, [W[n].shape for n in small])))

    return (loss, grad_x, *[G[n] for n in WEIGHTS], *[delta[n] for n in WEIGHTS], *[new_m[n] for n in WEIGHTS],
            *[new_v[n] for n in WEIGHTS])


def _columns_of_mod(dmod_all, me, ncol):
    flat = lax.dynamic_slice_in_dim(dmod_all, me * ncol, ncol, axis=2)
    return flat.transpose(1, 0, 2)
```

```python
import functools
import math

import jax
import jax.numpy as jnp
from jax import lax
from jax.experimental import pallas as pl
from jax.experimental.pallas import tpu as pltpu

F32, BF16 = jnp.float32, jnp.bfloat16
EPS = 1e-6
N_DEV = 8
D_MODEL = 1024
D_FF = 2816
HEADS = 16
HEAD_DIM = 64
S5_G, S5_P, S5_C = 64, 64, 16
S5_SUB = 8
V7X_VMEM_LIMIT = 56 * 1024 * 1024
NEG = -1e30
ADAM_LR, ADAM_B1, ADAM_B2, ADAM_EPS, ADAM_WD, ADAM_STEP = 1e-3, 0.9, 0.999, 1e-8, 0.01, 10
GELU_K = math.sqrt(2.0 / math.pi)
MESH_ID = pl.DeviceIdType.MESH


def _cp(*sem):
    return pltpu.CompilerParams(dimension_semantics=sem, vmem_limit_bytes=V7X_VMEM_LIMIT)


def _tile(n, target, mult=128):
    if n <= target:
        return n
    t = (target // mult) * mult
    while t >= mult:
        if n % t == 0:
            return t
        t -= mult
    return n


def _row(v):
    return v.reshape(1, -1).astype(F32)


def _mm(a, b, *, name, ta=False, tb=False, out_dtype=F32, tm=1024, tn=1024, tk=None, res=None, gate=None,
        a_h=False, b_h=False, o_h=False, with_t=False, comm=()):
    if a_h:
        M, K = a.shape[1], 2 * a.shape[2]
    elif ta:
        K, M = a.shape
    else:
        M, K = a.shape
    if b_h:
        N = 2 * b.shape[2]
    else:
        N = b.shape[0] if tb else b.shape[1]
    half_n = N // 2
    tm = _tile(M, tm, 128 if ta else 8)
    tn = _tile(half_n if (b_h or o_h) else N, tn)
    tk = K if tk is None else _tile(K // 2 if a_h else K, tk)
    nk = K // tk
    nkh, nnh = (K // 2) // tk if a_h else 1, half_n // tn
    if a_h:
        a_spec = pl.BlockSpec((None, tm, tk), lambda i, j, k: (k // nkh, i, k % nkh))
    elif ta:
        a_spec = pl.BlockSpec((tk, tm), lambda i, j, k: (k, i))
    else:
        a_spec = pl.BlockSpec((tm, tk), lambda i, j, k: (i, k))
    if b_h:
        b_spec = pl.BlockSpec((None, tk, tn), lambda i, j, k: (j // nnh, k, j % nnh))
    elif tb:
        b_spec = pl.BlockSpec((tn, tk), lambda i, j, k: (j, k))
    else:
        b_spec = pl.BlockSpec((tk, tn), lambda i, j, k: (k, j))
    if o_h:
        o_spec = pl.BlockSpec((None, tm, tn), lambda i, j, k: (j // nnh, i, j % nnh))
    else:
        o_spec = pl.BlockSpec((tm, tn), lambda i, j, k: (i, j))
    dn = (((0 if ta else 1,), (1 if tb else 0,)), ((), ()))
    fused = res is not None

    def body(*refs):
        if fused:
            a_ref, b_ref, r_ref, g_ref, m_ref, o_ref, acc_ref = refs
        elif with_t:
            a_ref, b_ref, o_ref, t_ref, acc_ref = refs
        else:
            a_ref, b_ref, o_ref, acc_ref = refs
        p = lax.dot_general(a_ref[...].astype(BF16), b_ref[...].astype(BF16), dn, preferred_element_type=F32)

        def finish(acc):
            if fused:
                m_ref[...] = acc.astype(m_ref.dtype)
                o_ref[...] = r_ref[...] + g_ref[...] * acc
            else:
                o_ref[...] = acc.astype(o_ref.dtype)
                if with_t:
                    t_ref[...] = acc.T.astype(t_ref.dtype)

        if nk == 1:
            finish(p)
        else:
            k = pl.program_id(2)

            @pl.when(k == 0)
            def _():
                acc_ref[...] = p

            @pl.when(k > 0)
            def _():
                acc_ref[...] += p

            @pl.when(k == nk - 1)
            def _():
                finish(acc_ref[...])

    in_specs = [a_spec, b_spec]
    args = [a, b]
    if fused:
        in_specs += [o_spec, pl.BlockSpec((1, tn), lambda i, j, k: (0, j))]
        args += [res, gate]
        out_shape = [jax.ShapeDtypeStruct((M, N), BF16), jax.ShapeDtypeStruct((M, N), F32)]
        out_specs = [o_spec, o_spec]
    else:
        out_shape = [jax.ShapeDtypeStruct((2, M, half_n) if o_h else (M, N), out_dtype)]
        out_specs = [o_spec]
        if with_t:
            out_shape.append(jax.ShapeDtypeStruct((N, M), out_dtype))
            out_specs.append(pl.BlockSpec((tn, tm), lambda i, j, k: (j, i)))
    outs, exchanged = _call(
        body, name=name, grid=(M // tm, N // tn, nk), in_specs=in_specs, out_specs=out_specs, out_shape=out_shape,
        scratch_shapes=[pltpu.VMEM((tm, tn) if nk > 1 else (8, 128), F32)], sem=("parallel", "parallel", "arbitrary"),
        args=args, comm=comm)
    outs = tuple(outs) if (fused or with_t) else outs[0]
    return (outs, exchanged) if comm else outs


def _modulate(h, g, shift, scale, *, name):
    S, D = h.shape
    tm = _tile(S, 512, 8)
    vec = pl.BlockSpec((1, D), lambda i: (0, 0))
    blk = pl.BlockSpec((tm, D), lambda i: (i, 0))

    def body(h_ref, g_ref, sh_ref, sc_ref, o_ref):
        x = h_ref[...]
        r = lax.rsqrt(jnp.mean(x * x, axis=-1, keepdims=True) + EPS)
        o_ref[...] = ((x * r * g_ref[...]) * (1.0 + sc_ref[...]) + sh_ref[...]).astype(o_ref.dtype)

    return pl.pallas_call(body, name=name, grid=(S // tm,), in_specs=[blk, vec, vec, vec], out_specs=blk,
                          out_shape=jax.ShapeDtypeStruct((S, D), BF16), compiler_params=_cp("parallel"))(h, g, shift, scale)


def _modulate_bwd(h, dhn, dh_in, g, scale, m_prev=None, gate_prev=None, *, name):
    S, D = h.shape
    tm = _tile(S, 512, 8)
    vec = pl.BlockSpec((1, D), lambda i: (0, 0))
    blk = pl.BlockSpec((tm, D), lambda i: (i, 0))
    sums = pl.BlockSpec((8, D), lambda i: (0, 0))
    below = m_prev is not None

    def body(h_ref, dhn_ref, dhi_ref, g_ref, sc_ref, *rest):
        dh_ref, s_ref = rest[-3:-1] if below else rest[-2:]

        @pl.when(pl.program_id(0) == 0)
        def _():
            s_ref[...] = jnp.zeros_like(s_ref)

        x = h_ref[...]
        r = lax.rsqrt(jnp.mean(x * x, axis=-1, keepdims=True) + EPS)
        xhat = x * r
        gv = g_ref[...]
        dhn_v = dhn_ref[...].astype(F32)
        dn = dhn_v * (1.0 + sc_ref[...])
        s_ref[0:1, :] += jnp.sum(dhn_v, axis=0, keepdims=True)
        s_ref[1:2, :] += jnp.sum(dhn_v * (xhat * gv), axis=0, keepdims=True)
        s_ref[2:3, :] += jnp.sum(dn * xhat, axis=0, keepdims=True)
        dxh = dn * gv
        dh = dhi_ref[...] + r * (dxh - xhat * jnp.mean(dxh * xhat, axis=-1, keepdims=True))
        dh_ref[...] = dh
        if below:
            mp_ref, gp_ref, dm_ref = rest[0], rest[1], rest[-1]
            dm_ref[...] = (dh * gp_ref[...]).astype(dm_ref.dtype)
            s_ref[3:4, :] += jnp.sum(dh * mp_ref[...].astype(F32), axis=0, keepdims=True)

    extra_in, extra_args = ([blk, vec], [m_prev, gate_prev]) if below else ([], [])
    return pl.pallas_call(body, name=name, grid=(S // tm,), in_specs=[blk, blk, blk, vec, vec] + extra_in,
                          out_specs=[blk, sums] + ([blk] if below else []),
                          out_shape=[jax.ShapeDtypeStruct((S, D), F32), jax.ShapeDtypeStruct((8, D), F32)]
                          + ([jax.ShapeDtypeStruct((S, D), BF16)] if below else []),
                          compiler_params=_cp("arbitrary"))(h, dhn, dh_in, g, scale, *extra_args)


def _loss_head(h, g, tgt, m_prev, gate_prev, *, name):
    S, D = h.shape
    tm = _tile(S, 512, 8)
    vec = pl.BlockSpec((1, D), lambda i: (0, 0))
    blk = pl.BlockSpec((tm, D), lambda i: (i, 0))
    lss = pl.BlockSpec((8, 128), lambda i: (0, 0))
    sums = pl.BlockSpec((8, D), lambda i: (0, 0))

    def body(h_ref, g_ref, t_ref, mp_ref, gp_ref, l_ref, dh_ref, s_ref, dm_ref):
        @pl.when(pl.program_id(0) == 0)
        def _():
            l_ref[...] = jnp.zeros_like(l_ref)
            s_ref[...] = jnp.zeros_like(s_ref)

        x = h_ref[...]
        r = lax.rsqrt(jnp.mean(x * x, axis=-1, keepdims=True) + EPS)
        xhat = x * r
        gv = g_ref[...]
        e = xhat * gv - t_ref[...]
        l_ref[...] += 0.5 * jnp.sum(jnp.mean(e * e, axis=-1, keepdims=True))
        dy = e * (1.0 / D)
        s_ref[0:1, :] += jnp.sum(dy * xhat, axis=0, keepdims=True)
        dxh = dy * gv
        dh = r * (dxh - xhat * jnp.mean(dxh * xhat, axis=-1, keepdims=True))
        dh_ref[...] = dh
        dm_ref[...] = (dh * gp_ref[...]).astype(dm_ref.dtype)
        s_ref[1:2, :] += jnp.sum(dh * mp_ref[...].astype(F32), axis=0, keepdims=True)

    return pl.pallas_call(body, name=name, grid=(S // tm,), in_specs=[blk, vec, blk, blk, vec], out_specs=[lss, blk, sums, blk],
                          out_shape=[jax.ShapeDtypeStruct((8, 128), F32), jax.ShapeDtypeStruct((S, D), F32),
                                     jax.ShapeDtypeStruct((8, D), F32), jax.ShapeDtypeStruct((S, D), BF16)],
                          compiler_params=_cp("arbitrary"))(h, g, tgt, m_prev, gate_prev)


def _shift_down(x, k, edge):
    tm = x.shape[0]
    rows = lax.broadcasted_iota(jnp.int32, x.shape, 0)
    out = pltpu.roll(x, k, 0)
    for j in range(k):
        out = jnp.where(rows == j, edge[8 - k + j:8 - k + j + 1, :], out)
    return out


def _shift_up(x, k, edge):
    tm = x.shape[0]
    rows = lax.broadcasted_iota(jnp.int32, x.shape, 0)
    out = pltpu.roll(x, tm - k, 0)
    for j in range(k):
        out = jnp.where(rows == tm - k + j, edge[j:j + 1, :], out)
    return out


def _conv_gate_fwd(up, cw, cb, *, name, comm=()):
    _, S, F = up.shape
    tf = _tile(F, 1408)
    nf = F // tf
    tm = _tile(S, 512, 8)

    def body(a_ref, b_ref, w_ref, cb_ref, z_ref, edge_ref):
        @pl.when(pl.program_id(1) == 0)
        def _():
            edge_ref[...] = jnp.zeros_like(edge_ref)

        a = a_ref[...].astype(F32)
        edge = edge_ref[...]
        w = w_ref[...]
        ac = cb_ref[...] + w[2:3, :] * a + w[1:2, :] * _shift_down(a, 1, edge) + w[0:1, :] * _shift_down(a, 2, edge)
        edge_ref[...] = a[tm - 8:tm, :]
        z_ref[...] = (ac * jax.nn.sigmoid(ac) * b_ref[...].astype(F32)).astype(z_ref.dtype)

    return _call(
        body, name=name, grid=(nf, S // tm), comm=comm, sem=("parallel", "arbitrary"), args=(up, up, cw, cb),
        in_specs=[pl.BlockSpec((None, tm, tf), lambda j, i: (0, i, j)), pl.BlockSpec((None, tm, tf), lambda j, i: (1, i, j)),
                  pl.BlockSpec((3, tf), lambda j, i: (0, j)), pl.BlockSpec((1, tf), lambda j, i: (0, j))],
        out_specs=[pl.BlockSpec((tm, tf), lambda j, i: (i, j))],
        out_shape=[jax.ShapeDtypeStruct((S, F), BF16)], scratch_shapes=[pltpu.VMEM((8, tf), F32)])


def _conv_gate_bwd(up, dz, cw, cb, *, name):
    _, S, F = up.shape
    tf = _tile(F, 1408)
    nf = F // tf
    tm = _tile(S, 512, 8)
    nt = S // tm
    hb = tm // 8

    def body(a_ref, ah_ref, b_ref, dz_ref, w_ref, cb_ref, d_ref, s_ref, edge_ref):
        i = pl.program_id(1)

        @pl.when(i == 0)
        def _():
            edge_ref[...] = jnp.zeros_like(edge_ref)
            s_ref[...] = jnp.zeros_like(s_ref)

        a = a_ref[...].astype(F32)
        halo = jnp.where(i == nt - 1, 0.0, ah_ref[...].astype(F32))
        w = w_ref[...]
        a1 = _shift_down(a, 1, halo)
        a2 = _shift_down(a, 2, halo)
        ac = cb_ref[...] + w[2:3, :] * a + w[1:2, :] * a1 + w[0:1, :] * a2
        sg = jax.nn.sigmoid(ac)
        dzv = dz_ref[...].astype(F32)
        si = ac * sg
        d_ref[1] = (dzv * si).astype(d_ref.dtype)
        dac = (dzv * b_ref[...].astype(F32)) * (sg + si * (1.0 - sg))
        s_ref[0:1, :] += jnp.sum(dac * a2, axis=0, keepdims=True)
        s_ref[1:2, :] += jnp.sum(dac * a1, axis=0, keepdims=True)
        s_ref[2:3, :] += jnp.sum(dac * a, axis=0, keepdims=True)
        s_ref[3:4, :] += jnp.sum(dac, axis=0, keepdims=True)
        edge = edge_ref[...]
        da = w[2:3, :] * dac + w[1:2, :] * _shift_up(dac, 1, edge) + w[0:1, :] * _shift_up(dac, 2, edge)
        edge_ref[...] = dac[0:8, :]
        d_ref[0] = da.astype(d_ref.dtype)

    tile = lambda hlf: pl.BlockSpec((None, tm, tf), lambda j, i: (hlf, nt - 1 - i, j))
    d_up, sums = pl.pallas_call(
        body, name=name, grid=(nf, nt),
        in_specs=[tile(0),
                  pl.BlockSpec((None, 8, tf), lambda j, i: (0, jnp.maximum((nt - 1 - i) * hb - 1, 0), j)),
                  tile(1), pl.BlockSpec((tm, tf), lambda j, i: (nt - 1 - i, j)),
                  pl.BlockSpec((3, tf), lambda j, i: (0, j)), pl.BlockSpec((1, tf), lambda j, i: (0, j))],
        out_specs=[pl.BlockSpec((2, tm, tf), lambda j, i: (0, nt - 1 - i, j)), pl.BlockSpec((8, tf), lambda j, i: (0, j))],
        out_shape=[jax.ShapeDtypeStruct((2, S, F), BF16), jax.ShapeDtypeStruct((8, F), F32)],
        scratch_shapes=[pltpu.VMEM((8, tf), F32)],
        compiler_params=_cp("parallel", "arbitrary"))(up, up, up, dz, cw, cb)
    return d_up, sums


def _log_sigmoid(x):
    return jnp.minimum(x, 0.0) - jnp.log(1.0 + jnp.exp(-jnp.abs(x)))


def _tri_ones(n, upper):
    r = lax.broadcasted_iota(jnp.int32, (n, n), 0)
    c = lax.broadcasted_iota(jnp.int32, (n, n), 1)
    return jnp.where((c >= r) if upper else (c <= r), 1.0, 0.0).astype(F32)


def _fgate_fwd(fl, bf, *, name):
    S, W = fl.shape
    tb = _tile(S, 256, 8)

    def body(fl_ref, b_ref, o_ref, carry_ref):
        @pl.when(pl.program_id(0) == 0)
        def _():
            carry_ref[...] = jnp.zeros_like(carry_ref)

        lf = _log_sigmoid(fl_ref[...] + b_ref[...])
        cs = jnp.dot(_tri_ones(tb, False), lf, precision=lax.Precision.HIGHEST, preferred_element_type=F32) + carry_ref[0:1, :]
        o_ref[...] = cs
        carry_ref[...] = jnp.broadcast_to(cs[tb - 1:tb, :], carry_ref.shape)

    blk = pl.BlockSpec((tb, W), lambda i: (i, 0))
    return pl.pallas_call(body, name=name, grid=(S // tb,), in_specs=[blk, pl.BlockSpec((1, W), lambda i: (0, 0))], out_specs=blk,
                          out_shape=jax.ShapeDtypeStruct((S, W), F32), scratch_shapes=[pltpu.VMEM((8, W), F32)],
                          compiler_params=_cp("arbitrary"))(fl, bf)


def _fgate_bwd(dF, fl, bf, *, name):
    S, W = fl.shape
    tb = _tile(S, 256, 8)
    nb = S // tb

    def body(d_ref, fl_ref, b_ref, o_ref, s_ref, carry_ref):
        @pl.when(pl.program_id(0) == 0)
        def _():
            carry_ref[...] = jnp.zeros_like(carry_ref)
            s_ref[...] = jnp.zeros_like(s_ref)

        rc = jnp.dot(_tri_ones(tb, True), d_ref[...], precision=lax.Precision.HIGHEST, preferred_element_type=F32) + carry_ref[0:1, :]
        carry_ref[...] = jnp.broadcast_to(rc[0:1, :], carry_ref.shape)
        dfl = rc * jax.nn.sigmoid(-(fl_ref[...] + b_ref[...]))
        o_ref[...] = dfl
        s_ref[...] += jnp.sum(dfl, axis=0, keepdims=True)

    blk = pl.BlockSpec((tb, W), lambda i: (nb - 1 - i, 0))
    vec = pl.BlockSpec((1, W), lambda i: (0, 0))
    return pl.pallas_call(body, name=name, grid=(nb,), in_specs=[blk, blk, vec], out_specs=[blk, vec],
                          out_shape=[jax.ShapeDtypeStruct((S, W), F32), jax.ShapeDtypeStruct((1, W), F32)],
                          scratch_shapes=[pltpu.VMEM((8, W), F32)], compiler_params=_cp("arbitrary"))(dF, fl, bf)


NT_DIMS = (((1,), (1,)), ((), ()))
TN_DIMS = (((0,), (0,)), ((), ()))
HP = HEADS // 2
Q_SCALE = HEAD_DIM ** -0.5


def _head_mask(x, hh):
    lanes = lax.broadcasted_iota(jnp.int32, x.shape, 1)
    return jnp.where((lanes >= hh * HEAD_DIM) & (lanes < (hh + 1) * HEAD_DIM), x, jnp.zeros_like(x))


def _lanes_from(cols, shape):
    lanes = lax.broadcasted_iota(jnp.int32, shape, 1)
    out = jnp.zeros(shape, F32)
    for i, cvec in enumerate(cols):
        out = jnp.where(lanes == i, cvec, out)
    return out


AUG_ROWS = 16
DIAGONAL_TILES = ((0, 0, True), (0, 1, False), (1, 1, True))


def _qa_pair(qt, aug):
    fill = jnp.zeros((HEAD_DIM - AUG_ROWS, qt.shape[1]), qt.dtype)
    return [jnp.concatenate([qt[0:HEAD_DIM], aug[0], fill], axis=0), jnp.concatenate([aug[1], fill, qt[HEAD_DIM:]], axis=0)]


def _rows_of_head(xt, hh):
    rows = lax.broadcasted_iota(jnp.int32, xt.shape, 0)
    return jnp.where((rows >= hh * HEAD_DIM) & (rows < (hh + 1) * HEAD_DIM), xt, jnp.zeros_like(xt))


def _put_rows(ref, rows):
    for i, r in enumerate(rows):
        ref[i:i + 1, :] = r
    ref[len(rows):, :] = jnp.zeros((ref.shape[0] - len(rows), ref.shape[1]), ref.dtype)


def _diag_mask_t(s):
    rows = lax.broadcasted_iota(jnp.int32, s.shape, 0)
    cols = lax.broadcasted_iota(jnp.int32, s.shape, 1)
    return jnp.where(rows <= cols, s, NEG)


def _foxt_fwd(qkvT, ka, vtb, augq, *, tb, name, comm=()):
    S = qkvT.shape[1]
    nb = S // tb

    def body(q_ref, ka_ref, v_ref, aq_ref, o_ref, lse_ref, m0, m1, l0, l1, acc0, acc1):
        qi = pl.program_id(1)
        qa = _qa_pair(q_ref[...], aq_ref[...])
        half = tb // 2
        state = ((m0, l0, acc0), (m1, l1, acc1))
        for m_sc, l_sc, acc_sc in state:
            m_sc[...] = jnp.full_like(m_sc, NEG)
            l_sc[...] = jnp.zeros_like(l_sc)
            acc_sc[...] = jnp.zeros_like(acc_sc)

        def tile(kj, ko, qo, n, masked):
            k0 = pl.multiple_of(kj * tb + ko, n)
            cols = slice(qo, qo + n)
            logits = [jnp.dot(ka_ref[hh, pl.ds(k0, n), :], qa[hh][:, cols], preferred_element_type=F32) for hh in range(2)]
            updates = []
            for hh, (m_sc, l_sc, acc_sc) in enumerate(state):
                s = _diag_mask_t(logits[hh]) if masked else logits[hh]
                m_old = m_sc[:, cols]
                m_new = jnp.maximum(m_old, jnp.max(s, axis=0, keepdims=True))
                alpha = jnp.exp(m_old - m_new)
                p = jnp.exp(s - m_new)
                l_sc[:, cols] = alpha * l_sc[:, cols] + jnp.sum(p, axis=0, keepdims=True)
                m_sc[:, cols] = m_new
                vt = v_ref[kj, HEAD_DIM * hh:HEAD_DIM * (hh + 1), ko:ko + n]
                updates.append((alpha, jnp.dot(vt, p.astype(BF16), preferred_element_type=F32)))
            for (alpha, pv), (_, _, acc_sc) in zip(updates, state):
                acc_sc[:, cols] = alpha * acc_sc[:, cols] + pv

        def off_diagonal(kj, c):
            tile(kj, 0, 0, tb, False)
            return c

        lax.fori_loop(0, qi, off_diagonal, 0)
        for ko, qo, masked in DIAGONAL_TILES:
            tile(qi, ko * half, qo * half, half, masked)
        for hh, (m_sc, l_sc, acc_sc) in enumerate(state):
            o_ref[HEAD_DIM * hh:HEAD_DIM * (hh + 1), :] = (acc_sc[...] / l_sc[...]).astype(o_ref.dtype)
        _put_rows(lse_ref, [m_sc[...] + jnp.log(l_sc[...]) for m_sc, l_sc, _ in state])

    return _call(
        body, name=name, grid=(HP, nb), comm=comm, sem=("parallel", "parallel"), args=(qkvT, ka, vtb, augq),
        in_specs=[pl.BlockSpec((128, tb), lambda hp, qi: (hp, qi)),
                  pl.BlockSpec((None, 2, S, 128), lambda hp, qi: (hp, 0, 0, 0)),
                  pl.BlockSpec((None, nb, 128, tb), lambda hp, qi: (hp, 0, 0, 0)),
                  pl.BlockSpec((None, 2, AUG_ROWS, tb), lambda hp, qi: (hp, 0, 0, qi))],
        out_specs=[pl.BlockSpec((128, tb), lambda hp, qi: (hp, qi)), pl.BlockSpec((None, 8, tb), lambda hp, qi: (hp, 0, qi))],
        out_shape=[jax.ShapeDtypeStruct((D_MODEL, S), BF16), jax.ShapeDtypeStruct((HP, 8, S), F32)],
        scratch_shapes=[pltpu.VMEM((1, tb), F32)] * 4 + [pltpu.VMEM((HEAD_DIM, tb), F32)] * 2)


def _foxt_bwd(qkvT, qkv, ka, qtb, dotb, do, augqb, lseb, otb, *, tb, name, comm=()):
    S = qkv.shape[0]
    nb = S // tb

    def body(ka_ref, v_ref, kt_ref, qt_ref, dot_ref, qn_ref, dn_ref, aq_ref, ls_ref, ot_ref,
             dq_ref, dk_ref, dv_ref, df_ref, dr_ref, dq_sc, dk_sc, dv_sc, ds_sc, dr_sc):
        kj = pl.program_id(1)

        @pl.when(kj == 0)
        def _():
            dq_sc[...] = jnp.zeros_like(dq_sc)
            dr_sc[...] = jnp.zeros_like(dr_sc)

        dk_sc[...] = jnp.zeros_like(dk_sc)
        dv_sc[...] = jnp.zeros_like(dv_sc)
        ds_sc[...] = jnp.zeros_like(ds_sc)
        v2 = v_ref[...]
        kth = [kt_ref[HEAD_DIM * hh:HEAD_DIM * (hh + 1), :] for hh in range(2)]

        def block(qi, tiles):
            q0 = pl.multiple_of(qi * tb, tb)
            qa = _qa_pair(qt_ref[qi], aq_ref[qi])
            dot, ls = dot_ref[qi], ls_ref[qi]
            prod = ot_ref[qi].astype(F32) * dot.astype(F32)
            delta = [jnp.sum(prod[HEAD_DIM * hh:HEAD_DIM * (hh + 1)], axis=0, keepdims=True) for hh in range(2)]
            qn, dn = qn_ref[pl.ds(q0, tb), :], dn_ref[pl.ds(q0, tb), :]
            for ko, qo, n, masked in tiles:
                keys, cols = slice(ko, ko + n), slice(qo, qo + n)
                for hh in range(2):
                    s = jnp.dot(ka_ref[hh, keys, :], qa[hh][:, cols], preferred_element_type=F32)
                    if masked:
                        s = _diag_mask_t(s)
                    p = jnp.exp(s - ls[hh:hh + 1, cols])
                    dp = jnp.dot(v2[keys], _rows_of_head(dot[:, cols], hh), preferred_element_type=F32)
                    ds = p * (dp - delta[hh][:, cols])
                    dsb = ds.astype(BF16)
                    dv_sc[keys, :] += jnp.dot(p.astype(BF16), _head_mask(dn[cols], hh), preferred_element_type=F32)
                    dk_sc[keys, :] += jnp.dot(dsb, _head_mask(qn[cols], hh), preferred_element_type=F32)
                    dq_sc[qi, HEAD_DIM * hh:HEAD_DIM * (hh + 1), cols] += jnp.dot(kth[hh][:, keys], dsb, preferred_element_type=F32)
                    part = ds[:, 0:128]
                    for j in range(1, n // 128):
                        part = part + ds[:, 128 * j:128 * (j + 1)]
                    ds_sc[hh, keys, :] += part
                    dr_sc[qi, hh:hh + 1, cols] += jnp.sum(ds, axis=0, keepdims=True)

        def off_diagonal(i, c):
            block(kj + 1 + i, [(0, 0, tb, False)])
            return c

        half = tb // 2
        block(kj, [(ko * half, qo * half, half, masked) for ko, qo, masked in DIAGONAL_TILES])
        lax.fori_loop(0, nb - 1 - kj, off_diagonal, 0)
        dk_ref[...] = dk_sc[...].astype(dk_ref.dtype)
        dv_ref[...] = dv_sc[...].astype(dv_ref.dtype)
        df_ref[...] = _lanes_from([-jnp.sum(ds_sc[hh], axis=1, keepdims=True) for hh in range(2)], (tb, 128))

        @pl.when(kj == nb - 1)
        def _():
            dq_ref[...] = (dq_sc[...] * Q_SCALE).astype(dq_ref.dtype)
            dr_ref[...] = dr_sc[...]

    resident = lambda rows: pl.BlockSpec((None, nb) + rows, lambda hp, kj: (hp,) + (0,) * (len(rows) + 1))
    whole = pl.BlockSpec((S, 128), lambda hp, kj: (0, hp))
    kblk = lambda off: pl.BlockSpec((tb, 128), lambda hp, kj: (kj, off + hp))
    return _call(
        body, name=name, grid=(HP, nb), comm=comm, sem=("parallel", "arbitrary"),
        args=(ka, qkv, qkvT, qtb, dotb, qkv, do, augqb, lseb, otb),
        in_specs=[pl.BlockSpec((None, 2, tb, 128), lambda hp, kj: (hp, 0, kj, 0)), kblk(2 * HP),
                  pl.BlockSpec((128, tb), lambda hp, kj: (HP + hp, kj)),
                  resident((128, tb)), resident((128, tb)), whole, whole, resident((2, AUG_ROWS, tb)), resident((8, tb)),
                  resident((128, tb))],
        out_specs=[resident((128, tb)), kblk(0), kblk(0), pl.BlockSpec((None, tb, 128), lambda hp, kj: (hp, kj, 0)),
                   resident((8, tb))],
        out_shape=[jax.ShapeDtypeStruct((HP, nb, 128, tb), BF16), jax.ShapeDtypeStruct((S, D_MODEL), BF16),
                   jax.ShapeDtypeStruct((S, D_MODEL), BF16), jax.ShapeDtypeStruct((HP, S, 128), F32),
                   jax.ShapeDtypeStruct((HP, nb, 8, tb), F32)],
        scratch_shapes=[pltpu.VMEM((nb, 128, tb), F32), pltpu.VMEM((tb, 128), F32), pltpu.VMEM((tb, 128), F32),
                        pltpu.VMEM((2, tb, 128), F32), pltpu.VMEM((nb, 8, tb), F32)])


def _split3(x):
    rnd = lambda v: lax.reduce_precision(v, exponent_bits=8, mantissa_bits=7)
    hi = rnd(x)
    mid = rnd(x - hi)
    lo = rnd(x - hi - mid)
    return hi.astype(BF16), mid.astype(BF16), lo.astype(BF16)


def _blocked(xt, nb, tb):
    lead = xt.shape[:-1]
    x = xt.reshape(lead + (nb, tb))
    return jnp.moveaxis(x, -2, 1)


def _aug_rows(bias):
    ones = jnp.ones(bias.shape, BF16)
    zeros = jnp.zeros(bias.shape, BF16)
    return jnp.stack(list(_split3(bias)) + [ones] * 3 + [zeros] * (AUG_ROWS - 6), axis=2)


def _fox_keys(qkv, fcol, *, name):
    S = qkv.shape[0]
    tr = _tile(S, 2048, 16)

    def body(k_ref, f_ref, o_ref):
        hp = pl.program_id(1)
        k2 = k_ref[...]
        lanes = lax.broadcasted_iota(jnp.int32, (tr, 128), 1)
        for hh in range(2):
            nf = -jnp.sum(jnp.where(lanes == 2 * hp + hh, f_ref[...], 0.0), axis=1, keepdims=True)
            hi = nf.astype(BF16).astype(F32)
            mid = (nf - hi).astype(BF16).astype(F32)
            lo = nf - hi - mid
            base = HEAD_DIM * (1 - hh)
            aug = jnp.where((lanes >= base) & (lanes < base + 3), 1.0, 0.0)
            for j, piece in enumerate((hi, mid, lo)):
                aug = jnp.where(lanes == base + 3 + j, piece, aug)
            mine = (lanes >= HEAD_DIM * hh) & (lanes < HEAD_DIM * (hh + 1))
            o_ref[hh] = jnp.where(mine, k2, aug.astype(BF16))

    return pl.pallas_call(
        body, name=name, grid=(S // tr, HP),
        in_specs=[pl.BlockSpec((tr, 128), lambda i, hp: (i, HP + hp)), pl.BlockSpec((tr, 128), lambda i, hp: (i, 0))],
        out_specs=pl.BlockSpec((None, 2, tr, 128), lambda i, hp: (hp, 0, i, 0)),
        out_shape=jax.ShapeDtypeStruct((HP, 2, S, 128), BF16), compiler_params=_cp("parallel", "parallel"))(qkv, fcol)


HALF = S5_SUB * S5_P
NCB = 2 * HALF // 128
RE, IM = slice(0, NCB // 2), slice(NCB // 2, NCB)


def _gelu(x):
    return 0.5 * x * (1.0 + jnp.tanh(GELU_K * (x + 0.044715 * x * x * x)))


def _gelu_grad(x):
    t = jnp.tanh(GELU_K * (x + 0.044715 * x * x * x))
    return 0.5 * (1.0 + t) + 0.5 * x * (1.0 - t * t) * GELU_K * (1.0 + 3.0 * 0.044715 * x * x)


def _s5_put(buf, s, val, lt):
    for cb in range(NCB):
        buf[cb, pl.ds(s, lt, stride=S5_SUB), :] = val[:, 128 * cb:128 * (cb + 1)]


def _s5_get(buf, s, lt):
    return jnp.concatenate([buf[cb, pl.ds(s, lt, stride=S5_SUB), :] for cb in range(NCB)], axis=1)


def _s5_project_in(u_ref, wb_ref, buf, lt):
    for s in range(S5_SUB):
        _s5_put(buf, s, jnp.dot(u_ref[:, 128 * s:128 * (s + 1)], wb_ref[s], preferred_element_type=F32), lt)


def _s5_scan(buf, lam_ref, h0, lt):
    a_re, a_im = lam_ref[RE], lam_ref[IM]
    a2_re, a2_im = a_re * a_re - a_im * a_im, 2.0 * a_re * a_im

    def step(i, carry):
        hr, hi = carry
        r0 = pl.multiple_of(i * 2 * S5_SUB, 2 * S5_SUB)
        r1 = r0 + S5_SUB
        b0r, b0i = buf[RE, pl.ds(r0, S5_SUB), :], buf[IM, pl.ds(r0, S5_SUB), :]
        cr = a_re * b0r - a_im * b0i + buf[RE, pl.ds(r1, S5_SUB), :]
        ci = a_re * b0i + a_im * b0r + buf[IM, pl.ds(r1, S5_SUB), :]
        buf[RE, pl.ds(r0, S5_SUB), :] = a_re * hr - a_im * hi + b0r
        buf[IM, pl.ds(r0, S5_SUB), :] = a_re * hi + a_im * hr + b0i
        nr = a2_re * hr - a2_im * hi + cr
        ni = a2_re * hi + a2_im * hr + ci
        buf[RE, pl.ds(r1, S5_SUB), :] = nr
        buf[IM, pl.ds(r1, S5_SUB), :] = ni
        return nr, ni

    return lax.fori_loop(0, lt // 2, step, (h0[RE], h0[IM]))


def _s5_fwd(u, wb, wc, lam, dskip, *, lt, name, comm=()):
    S, D = u.shape
    nt = S // lt

    def body(u_ref, wb_ref, wc_ref, lam_ref, d_ref, yp_ref, yg_ref, st_ref, buf, h_sc):
        @pl.when(pl.program_id(0) == 0)
        def _():
            h_sc[...] = jnp.zeros_like(h_sc)

        st_ref[...] = h_sc[...]
        _s5_project_in(u_ref, wb_ref, buf, lt)
        hr, hi = _s5_scan(buf, lam_ref, h_sc[...], lt)
        h_sc[RE] = hr
        h_sc[IM] = hi
        for s in range(S5_SUB):
            cols = slice(128 * s, 128 * (s + 1))
            hs = _s5_get(buf, s, lt).astype(BF16)
            yp = jnp.dot(hs, wc_ref[s], preferred_element_type=F32) + d_ref[:, cols] * u_ref[:, cols].astype(F32)
            yp_ref[:, cols] = yp.astype(yp_ref.dtype)
            yg_ref[:, cols] = _gelu(yp).astype(yg_ref.dtype)

    blk = pl.BlockSpec((lt, D), lambda i: (i, 0))
    full = lambda shp: pl.BlockSpec(shp, lambda i: (0,) * len(shp))
    state = (NCB, S5_SUB, 128)
    return _call(
        body, name=name, grid=(nt,), comm=comm, sem=("arbitrary",), args=(u, wb, wc, lam, dskip),
        in_specs=[blk, full(wb.shape), full(wc.shape), full(lam.shape), full(dskip.shape)],
        out_specs=[blk, blk, pl.BlockSpec((None,) + state, lambda i: (i, 0, 0, 0))],
        out_shape=[jax.ShapeDtypeStruct((S, D), BF16), jax.ShapeDtypeStruct((S, D), BF16), jax.ShapeDtypeStruct((nt,) + state, F32)],
        scratch_shapes=[pltpu.VMEM((NCB, lt * S5_SUB, 128), F32), pltpu.VMEM(state, F32)])


def _s5_bwd(u, dyp, st, wb, wc, lam, dskip, *, lt, name, comm=()):
    S, D = u.shape
    nt = S // lt

    def body(u_ref, dy_ref, st_ref, wb_ref, wc_ref, lam_ref, d_ref, du_ref, dwb_ref, dwc_ref, dlam_ref, hbuf, gbuf, g_sc):
        @pl.when(pl.program_id(0) == 0)
        def _():
            g_sc[...] = jnp.zeros_like(g_sc)
            dwb_ref[...] = jnp.zeros_like(dwb_ref)
            dwc_ref[...] = jnp.zeros_like(dwc_ref)
            dlam_ref[...] = jnp.zeros_like(dlam_ref)

        _s5_project_in(u_ref, wb_ref, hbuf, lt)
        _s5_scan(hbuf, lam_ref, st_ref[...], lt)
        for s in range(S5_SUB):
            dys = dy_ref[:, 128 * s:128 * (s + 1)]
            _s5_put(gbuf, s, lax.dot_general(dys, wc_ref[s], NT_DIMS, preferred_element_type=F32), lt)
        a_re, a_im = lam_ref[RE], lam_ref[IM]

        def one(t, carry, hp_re, hp_im):
            gr, gi, dar, dai = carry
            r0 = pl.multiple_of(t * S5_SUB, S5_SUB)
            nr = gbuf[RE, pl.ds(r0, S5_SUB), :] + a_re * gr + a_im * gi
            ni = gbuf[IM, pl.ds(r0, S5_SUB), :] + a_re * gi - a_im * gr
            gbuf[RE, pl.ds(r0, S5_SUB), :] = nr
            gbuf[IM, pl.ds(r0, S5_SUB), :] = ni
            return nr, ni, dar + nr * hp_re + ni * hp_im, dai + ni * hp_re - nr * hp_im

        def step(k, carry):
            t = lt - 1 - k
            p0 = pl.multiple_of((t - 1) * S5_SUB, S5_SUB)
            return one(t, carry, hbuf[RE, pl.ds(p0, S5_SUB), :], hbuf[IM, pl.ds(p0, S5_SUB), :])

        init = (g_sc[RE], g_sc[IM], dlam_ref[RE], dlam_ref[IM])
        carry = lax.fori_loop(0, lt - 1, step, init, unroll=2)
        gr, gi, dar, dai = one(0, carry, st_ref[RE], st_ref[IM])
        g_sc[RE] = gr
        g_sc[IM] = gi
        dlam_ref[RE] = dar
        dlam_ref[IM] = dai
        for s in range(S5_SUB):
            cols = slice(128 * s, 128 * (s + 1))
            gs = _s5_get(gbuf, s, lt).astype(BF16)
            hs = _s5_get(hbuf, s, lt).astype(BF16)
            us, dys = u_ref[:, cols], dy_ref[:, cols]
            du = lax.dot_general(gs, wb_ref[s], NT_DIMS, preferred_element_type=F32) + d_ref[:, cols] * dys.astype(F32)
            du_ref[:, cols] = du.astype(du_ref.dtype)
            dwb_ref[s] += lax.dot_general(us, gs, TN_DIMS, preferred_element_type=F32)
            dwc_ref[s] += lax.dot_general(hs, dys, TN_DIMS, preferred_element_type=F32)

    blk = pl.BlockSpec((lt, D), lambda i: (nt - 1 - i, 0))
    full = lambda shp: pl.BlockSpec(shp, lambda i: (0,) * len(shp))
    state = (NCB, S5_SUB, 128)
    words = pltpu.VMEM((NCB, lt * S5_SUB, 128), F32)
    return _call(
        body, name=name, grid=(nt,), comm=comm, sem=("arbitrary",), args=(u, dyp, st, wb, wc, lam, dskip),
        in_specs=[blk, blk, pl.BlockSpec((None,) + state, lambda i: (nt - 1 - i, 0, 0, 0)),
                  full(wb.shape), full(wc.shape), full(lam.shape), full(dskip.shape)],
        out_specs=[blk, full(wb.shape), full(wc.shape), full(lam.shape)],
        out_shape=[jax.ShapeDtypeStruct((S, D), BF16), jax.ShapeDtypeStruct(wb.shape, F32), jax.ShapeDtypeStruct(wc.shape, F32),
                   jax.ShapeDtypeStruct(lam.shape, F32)],
        scratch_shapes=[words, words, pltpu.VMEM(state, F32)])


def _s5_disc(lam_re, lam_im, log_dt, b_re, b_im):
    dt = jnp.exp(log_dt)[:, None]
    mag = jnp.exp(lam_re * dt)
    lb_re, lb_im = mag * jnp.cos(lam_im * dt), mag * jnp.sin(lam_im * dt)
    num_re = lb_re - 1.0
    den = lam_re * lam_re + lam_im * lam_im
    k_re = (num_re * lam_re + lb_im * lam_im) / den
    k_im = (lb_im * lam_re - num_re * lam_im) / den
    bb_re = k_re[..., None] * b_re - k_im[..., None] * b_im
    bb_im = k_re[..., None] * b_im + k_im[..., None] * b_re
    return lb_re, lb_im, bb_re, bb_im


def _s5_pack(lb_re, lb_im, bb_re, bb_im, c_re, c_im):
    eye = jnp.eye(S5_SUB, dtype=F32)
    tb = lambda x: jnp.einsum("sgcp,gh->sgchp", x.reshape(S5_SUB, S5_SUB, S5_P, S5_C).transpose(0, 1, 3, 2), eye).reshape(S5_SUB, 128, HALF)
    tc = lambda x: jnp.einsum("sgpc,gh->sgphc", x.reshape(S5_SUB, S5_SUB, S5_C, S5_P).transpose(0, 1, 3, 2), eye).reshape(S5_SUB, HALF, 128)
    wb = jnp.concatenate([tb(bb_re), tb(bb_im)], axis=2).astype(BF16)
    wc = jnp.concatenate([tc(c_re), -tc(c_im)], axis=1).astype(BF16)
    lam = jnp.concatenate([lb_re.reshape(S5_SUB, HALF), lb_im.reshape(S5_SUB, HALF)], axis=1)
    return wb, wc, lam.reshape(S5_SUB, NCB, 128).transpose(1, 0, 2)


def _s5_unpack(dwb, dwc, dlam):
    db = jnp.einsum("sgcrgp->rsgpc", dwb.reshape(S5_SUB, S5_SUB, S5_C, 2, S5_SUB, S5_P)).reshape(2, S5_G, S5_P, S5_C)
    dc = jnp.einsum("srgpgc->rsgcp", dwc.reshape(S5_SUB, 2, S5_SUB, S5_P, S5_SUB, S5_C)).reshape(2, S5_G, S5_C, S5_P)
    dlam = dlam.transpose(1, 0, 2).reshape(S5_SUB, 2 * HALF)
    return (dlam[:, :HALF].reshape(S5_G, S5_P), dlam[:, HALF:].reshape(S5_G, S5_P), db[0], db[1], dc[0], -dc[1])


def _glu_fwd(yg, gl, *, name):
    S, D = yg.shape
    tm = _tile(S, 512, 8)
    blk = pl.BlockSpec((tm, D), lambda i: (i, 0))

    def body(y_ref, g_ref, o_ref):
        o_ref[...] = (y_ref[...].astype(F32) * jax.nn.sigmoid(g_ref[...].astype(F32))).astype(o_ref.dtype)

    return pl.pallas_call(body, name=name, grid=(S // tm,), in_specs=[blk, blk], out_specs=blk,
                          out_shape=jax.ShapeDtypeStruct((S, D), BF16), compiler_params=_cp("parallel"))(yg, gl)


def _glu_bwd(dy2, yg, gl, *, name):
    S, D = yg.shape
    tm = _tile(S, 512, 8)
    blk = pl.BlockSpec((tm, D), lambda i: (i, 0))

    def body(d_ref, y_ref, g_ref, da_ref, dg_ref):
        d = d_ref[...].astype(F32)
        sg = jax.nn.sigmoid(g_ref[...].astype(F32))
        da_ref[...] = d * sg
        dg_ref[...] = (d * y_ref[...].astype(F32) * sg * (1.0 - sg)).astype(dg_ref.dtype)

    return pl.pallas_call(body, name=name, grid=(S // tm,), in_specs=[blk, blk, blk], out_specs=[blk, blk],
                          out_shape=[jax.ShapeDtypeStruct((S, D), F32), jax.ShapeDtypeStruct((S, D), BF16)],
                          compiler_params=_cp("parallel"))(dy2, yg, gl)


def _gelu_bwd(da, db, ypre, u, *, name):
    S, D = ypre.shape
    tm = _tile(S, 512, 8)
    blk = pl.BlockSpec((tm, D), lambda i: (i, 0))
    vec = pl.BlockSpec((1, D), lambda i: (0, 0))

    def body(a_ref, b_ref, y_ref, u_ref, o_ref, s_ref):
        @pl.when(pl.program_id(0) == 0)
        def _():
            s_ref[...] = jnp.zeros_like(s_ref)

        dy = (a_ref[...] + b_ref[...]) * _gelu_grad(y_ref[...].astype(F32))
        o_ref[...] = dy.astype(o_ref.dtype)
        s_ref[...] += jnp.sum(dy * u_ref[...].astype(F32), axis=0, keepdims=True)

    return pl.pallas_call(body, name=name, grid=(S // tm,), in_specs=[blk, blk, blk, blk], out_specs=[blk, vec],
                          out_shape=[jax.ShapeDtypeStruct((S, D), BF16), jax.ShapeDtypeStruct((1, D), F32)],
                          compiler_params=_cp("arbitrary"))(da, db, ypre, u)


def _my_index():
    return 4 * lax.axis_index("x") + 2 * lax.axis_index("y") + lax.axis_index("c")


def _comm_plan(payloads):
    n = len(payloads)
    hbm = pl.BlockSpec(memory_space=pl.ANY)
    shapes = [jax.ShapeDtypeStruct((N_DEV,) + (x.shape if g else x.shape[1:]), x.dtype) for x, g in payloads]
    sems = [pltpu.SemaphoreType.DMA((n, N_DEV - 1)), pltpu.SemaphoreType.DMA((n, N_DEV - 1)), pltpu.SemaphoreType.DMA((n,))] if n else []
    return [hbm] * n, [hbm] * n, shapes, sems


def _comm_copies(x_refs, o_refs, gathers, send_sems, recv_sems, local_sems):
    mx, my, mc = lax.axis_index("x"), lax.axis_index("y"), lax.axis_index("c")
    me = 4 * mx + 2 * my + mc
    copies = []
    for i, (x_ref, o_ref, gather) in enumerate(zip(x_refs, o_refs, gathers)):
        src = (lambda j, r=x_ref: r) if gather else (lambda j, r=x_ref: r.at[j])
        copies.append(pltpu.make_async_copy(src(me), o_ref.at[me], local_sems.at[i]))
        for k in range(1, N_DEV):
            px, py, pc = mx ^ (k >> 2), my ^ ((k >> 1) & 1), mc ^ (k & 1)
            copies.append(pltpu.make_async_remote_copy(src(4 * px + 2 * py + pc), o_ref.at[me], send_sems.at[i, k - 1],
                                                       recv_sems.at[i, k - 1], device_id=(px, py, pc), device_id_type=MESH_ID))
    return copies


def _call(body, *, name, grid, in_specs, out_specs, out_shape, scratch_shapes, sem, args, comm=()):
    if not comm:
        outs = pl.pallas_call(body, name=name, grid=grid, in_specs=in_specs, out_specs=out_specs, out_shape=out_shape,
                              scratch_shapes=scratch_shapes, compiler_params=_cp(*sem))(*args)
        return outs, ()
    n, n_in, n_out = len(comm), len(in_specs), len(out_specs)
    c_in, c_out, c_shapes, c_sems = _comm_plan(comm)
    gathers = [g for _, g in comm]

    def wrapped(*refs):
        own_in, x_refs = refs[:n_in], refs[n_in:n_in + n]
        own_out, o_refs = refs[n_in + n:n_in + n + n_out], refs[n_in + n + n_out:n_in + 2 * n + n_out]
        own_scratch, sems = refs[n_in + 2 * n + n_out:-3], refs[-3:]
        ids = [pl.program_id(a) for a in range(len(grid))]
        first = functools.reduce(jnp.logical_and, [i == 0 for i in ids])
        last = functools.reduce(jnp.logical_and, [i == g - 1 for i, g in zip(ids, grid)])

        @pl.when(first)
        def _():
            for cp in _comm_copies(x_refs, o_refs, gathers, *sems):
                cp.start()

        body(*own_in, *own_out, *own_scratch)

        @pl.when(last)
        def _():
            for cp in _comm_copies(x_refs, o_refs, gathers, *sems):
                cp.wait()

    outs = pl.pallas_call(wrapped, name=name, grid=grid, in_specs=list(in_specs) + c_in, out_specs=list(out_specs) + c_out,
                          out_shape=list(out_shape) + c_shapes, scratch_shapes=list(scratch_shapes) + c_sems,
                          compiler_params=_cp(*["arbitrary"] * len(grid)))(*args, *[x for x, _ in comm])
    return outs[:n_out], outs[n_out:]


def _exchange_many(payloads, *, name):
    n = len(payloads)
    in_specs, out_specs, shapes, sems = _comm_plan(payloads)
    gathers = [g for _, g in payloads]

    def body(*refs):
        copies = _comm_copies(refs[:n], refs[n:2 * n], gathers, *refs[2 * n:])
        for cp in copies:
            cp.start()
        for cp in copies:
            cp.wait()

    return pl.pallas_call(body, name=name, in_specs=in_specs, out_specs=out_specs, out_shape=shapes, scratch_shapes=sems,
                          compiler_params=pltpu.CompilerParams(has_side_effects=True))(*[x for x, _ in payloads])


def _exchange(x, *, gather, name):
    return _exchange_many([(x, gather)], name=name)[0]


def _sum8(x8, *, name):
    _, R, C = x8.shape
    tr = _tile(R, 256, 16)

    def body(x_ref, o_ref):
        acc = x_ref[0].astype(F32)
        for j in range(1, N_DEV):
            acc = acc + x_ref[j].astype(F32)
        o_ref[...] = acc

    return pl.pallas_call(body, name=name, grid=(R // tr,), in_specs=[pl.BlockSpec((N_DEV, tr, C), lambda i: (0, i, 0))],
                          out_specs=pl.BlockSpec((tr, C), lambda i: (i, 0)), out_shape=jax.ShapeDtypeStruct((R, C), F32),
                          compiler_params=_cp("parallel"))(x8)


def _pack(arrs, dtype, row_mult):
    flat = jnp.concatenate([a.reshape(-1).astype(dtype) for a in arrs])
    n = flat.shape[0]
    rows = -(-n // 1024)
    rows = -(-rows // row_mult) * row_mult
    return jnp.pad(flat, (0, rows * 1024 - n)).reshape(rows, 1024)


def _unpack(buf, shapes):
    lead = buf.shape[:-2]
    flat = buf.reshape(lead + (-1,))
    out, off = [], 0
    for s in shapes:
        n = math.prod(s)
        out.append(flat[..., off:off + n].reshape(lead + tuple(s)))
        off += n
    return out


def _adaln_fwd(c_all, w, b, *, name):
    nk, D, n = w.shape

    def body(c_ref, w_ref, b_ref, o_ref):
        cv = c_ref[...]
        sc = (cv * jax.nn.sigmoid(cv)).astype(BF16)
        o_ref[...] = jnp.dot(sc, w_ref[...].astype(BF16), preferred_element_type=F32) + b_ref[...]

    return pl.pallas_call(body, name=name, grid=(nk,),
                          in_specs=[pl.BlockSpec((N_DEV, D), lambda k: (0, 0)), pl.BlockSpec((None, D, n), lambda k: (k, 0, 0)),
                                    pl.BlockSpec((None, 1, n), lambda k: (k, 0, 0))],
                          out_specs=pl.BlockSpec((None, N_DEV, n), lambda k: (k, 0, 0)),
                          out_shape=jax.ShapeDtypeStruct((nk, N_DEV, n), F32), compiler_params=_cp("parallel"))(c_all, w, b)


def _adaln_bwd(c_all, dmod, *, name):
    nk, _, n = dmod.shape
    D = c_all.shape[1]

    def body(c_ref, d_ref, dw_ref, db_ref):
        cv = c_ref[...]
        sc = (cv * jax.nn.sigmoid(cv)).astype(BF16).astype(F32)
        dm = d_ref[...]
        dw_ref[...] = lax.dot_general(sc, dm.astype(BF16).astype(F32), TN_DIMS, precision=lax.Precision.HIGHEST,
                                      preferred_element_type=F32)
        db_ref[...] = jnp.sum(dm, axis=0, keepdims=True)

    return pl.pallas_call(body, name=name, grid=(nk,),
                          in_specs=[pl.BlockSpec((N_DEV, D), lambda k: (0, 0)), pl.BlockSpec((None, N_DEV, n), lambda k: (k, 0, 0))],
                          out_specs=[pl.BlockSpec((None, D, n), lambda k: (k, 0, 0)), pl.BlockSpec((None, 1, n), lambda k: (k, 0, 0))],
                          out_shape=[jax.ShapeDtypeStruct((nk, D, n), F32), jax.ShapeDtypeStruct((nk, 1, n), F32)],
                          compiler_params=_cp("parallel"))(c_all, dmod)


def _adamw(w, g, m, v, *, name):
    R, C = w.shape
    tr = _tile(R, 256, 8)
    blk = pl.BlockSpec((tr, C), lambda i: (i, 0))
    c1 = 1.0 / (1.0 - ADAM_B1 ** ADAM_STEP)
    c2 = 1.0 / (1.0 - ADAM_B2 ** ADAM_STEP)

    def body(w_ref, g_ref, m_ref, v_ref, d_ref, nm_ref, nv_ref):
        gv = g_ref[...]
        nm = ADAM_B1 * m_ref[...] + (1.0 - ADAM_B1) * gv
        nv = ADAM_B2 * v_ref[...] + (1.0 - ADAM_B2) * (gv * gv)
        nm_ref[...] = nm
        nv_ref[...] = nv
        d_ref[...] = -ADAM_LR * ((nm * c1) / (jnp.sqrt(nv * c2) + ADAM_EPS) + ADAM_WD * w_ref[...])

    sh = jax.ShapeDtypeStruct((R, C), F32)
    return pl.pallas_call(body, name=name, grid=(R // tr,), in_specs=[blk] * 4, out_specs=[blk] * 3, out_shape=[sh] * 3,
                          compiler_params=_cp("parallel"))(w, g, m, v)


def _adamw_nd(w, g, m, v, *, name):
    shp = w.shape
    two = (-1, shp[-1]) if w.ndim > 1 else (1, -1)
    outs = _adamw(w.reshape(two), g.reshape(two), m.reshape(two), v.reshape(two), name=name)
    return [o.reshape(shp) for o in outs]


S5_TILE = 256
ATTN_BLOCK = 1024
BIG = ("s5_w_in", "s5_w_glu", "s5_w_out", "fox_w_in", "fox_w_out", "ffn_w_up", "ffn_w_down")
WEIGHTS = ("norm_g", "ada_w", "ada_b", "s5_w_in", "s5_lam_re", "s5_lam_im", "s5_log_dt", "s5_b_re", "s5_b_im", "s5_c_re",
           "s5_c_im", "s5_d", "s5_w_glu", "s5_w_out", "fox_w_in", "fox_b_f", "fox_w_out", "ffn_w_up", "ffn_conv_w",
           "ffn_conv_b", "ffn_w_down", "final_g")
REPLICATED = ("s5_lam_re", "s5_lam_im", "s5_log_dt", "s5_b_re", "s5_b_im", "s5_c_re", "s5_c_im", "s5_d", "fox_b_f",
              "ffn_conv_b", "final_g")


def _row_group(arrs):
    return jnp.concatenate([a.reshape(-1, D_MODEL) for a in arrs], axis=0)


def _cols_join(g):
    return jnp.concatenate([g[j] for j in range(N_DEV)], axis=1)


def _cols_split(full):
    n = full.shape[1] // N_DEV
    return jnp.stack([full[:, j * n:(j + 1) * n] for j in range(N_DEV)])


def kernel(x, c, norm_g, ada_w, ada_b, s5_w_in, s5_lam_re, s5_lam_im, s5_log_dt, s5_b_re, s5_b_im, s5_c_re, s5_c_im, s5_d, s5_w_glu, s5_w_out, fox_w_in, fox_b_f, fox_w_out, ffn_w_up, ffn_conv_w, ffn_conv_b, ffn_w_down, final_g, loss_target, m_norm_g, m_ada_w, m_ada_b, m_s5_w_in, m_s5_lam_re, m_s5_lam_im, m_s5_log_dt, m_s5_b_re, m_s5_b_im, m_s5_c_re, m_s5_c_im, m_s5_d, m_s5_w_glu, m_s5_w_out, m_fox_w_in, m_fox_b_f, m_fox_w_out, m_ffn_w_up, m_ffn_conv_w, m_ffn_conv_b, m_ffn_w_down, m_final_g, v_norm_g, v_ada_w, v_ada_b, v_s5_w_in, v_s5_lam_re, v_s5_lam_im, v_s5_log_dt, v_s5_b_re, v_s5_b_im, v_s5_c_re, v_s5_c_im, v_s5_d, v_s5_w_glu, v_s5_w_out, v_fox_w_in, v_fox_b_f, v_fox_w_out, v_ffn_w_up, v_ffn_conv_w, v_ffn_conv_b, v_ffn_w_down, v_final_g):
    args = dict(locals())
    W = {n: args[n] for n in WEIGHTS}
    M = {n: args["m_" + n] for n in WEIGHTS}
    V = {n: args["v_" + n] for n in WEIGHTS}
    D, F = D_MODEL, D_FF
    me = _my_index()
    h0 = x[0]
    S = h0.shape[0]
    lt = min(S5_TILE, S)
    tb = min(ATTN_BLOCK, S)

    b16 = lambda a: a.astype(BF16)
    g0, g_in = _exchange_many([(_pack([c, norm_g, ffn_conv_w], F32, 8), True), (b16(s5_w_in[0]), True)], name="gather_first")
    c_all, ng_all, cw_all = _unpack(g0, [(D,), norm_g.shape, ffn_conv_w.shape])
    ng_full = ng_all.transpose(1, 2, 0, 3).reshape(2, 2, D)
    cw_full = cw_all.transpose(1, 2, 0, 3).reshape(2, 3, F)

    ncol = ada_w.shape[-1]
    modp = _adaln_fwd(c_all, ada_w.reshape(4, D, ncol), ada_b.reshape(4, 1, ncol), name="adaln_fwd")
    g1 = _exchange(_pack([modp], F32, 8), gather=True, name="gather_adaln")
    (mod_all,) = _unpack(g1, [modp.shape])
    mod = lax.dynamic_index_in_dim(mod_all, me, axis=2, keepdims=False).transpose(1, 0, 2).reshape(4, 3 * D)
    shift = [mod[k:k + 1, :D] for k in range(4)]
    scale = [mod[k:k + 1, D:2 * D] for k in range(4)]
    gate = [mod[k:k + 1, 2 * D:] for k in range(4)]
    gain = [_row(ng_full[k // 2, k % 2]) for k in range(4)]

    rows = D // N_DEV
    full = {"s5_w_in": g_in.reshape(D, D)}
    bf_pad = jnp.pad(fox_b_f, ((0, 0), (0, 128 - HEADS)))

    def ffn_fwd(h, k, layer, comm=(), down_shard=None):
        hn = _modulate(h, gain[k], shift[k], scale[k], name=f"modulate{k}")
        up = _mm(hn, full[f"ffn_w_up{layer}"], name=f"ffn_up{layer}", o_h=True, out_dtype=BF16, tn=1408, comm=comm)
        up, exchanged = up if comm else (up, ())
        (z,), gathered = _conv_gate_fwd(up, cw_full[layer], ffn_conv_b[layer:layer + 1], name=f"conv_gate{layer}",
                                        comm=() if down_shard is None else [(down_shard, True)])
        if down_shard is not None:
            full[f"ffn_w_down{layer}"] = gathered[0].reshape(F, D)
        m, h_out = _mm(z, full[f"ffn_w_down{layer}"], name=f"ffn_down{layer}", res=h, gate=gate[k])
        return h_out, (hn, up, z, m), exchanged

    lb_re, lb_im, bb_re, bb_im = _s5_disc(s5_lam_re[0], s5_lam_im[0], s5_log_dt[0], s5_b_re[0], s5_b_im[0])
    wb, wc, lam = _s5_pack(lb_re, lb_im, bb_re, bb_im, s5_c_re[0], s5_c_im[0])
    hn0 = _modulate(h0, gain[0], shift[0], scale[0], name="modulate0")
    u = _mm(hn0, full["s5_w_in"], name="s5_in", out_dtype=BF16)
    (ypre, yg, st), (g_up0, g_rows0) = _s5_fwd(
        u, wb, wc, lam, s5_d, lt=lt, name="s5_scan",
        comm=[(b16(ffn_w_up[0]), True), (_row_group([b16(s5_w_glu), b16(s5_w_out)]), True)])
    full["ffn_w_up0"] = _cols_join(g_up0)
    full["s5_w_glu"] = g_rows0[:, :rows].reshape(D, D)
    full["s5_w_out"] = g_rows0[:, rows:].reshape(D, D)
    gl = _mm(yg, full["s5_w_glu"], name="s5_glu", out_dtype=BF16)
    y2 = _glu_fwd(yg, gl, name="s5_glu_gate")
    m0, h1 = _mm(y2, full["s5_w_out"], name="s5_out", res=h0, gate=gate[0])
    h2, (hn1, up0, z0, m1), (g_fox, g_fox_out) = ffn_fwd(h1, 1, 0, comm=[(b16(fox_w_in[0]), True), (b16(fox_w_out[0]), True)],
                                                         down_shard=b16(ffn_w_down[0]))
    full["fox_w_in"] = _cols_join(g_fox)
    full["fox_w_out"] = g_fox_out.reshape(D, D)
    w_proj = jnp.pad(full["fox_w_in"], ((0, 0), (0, 3 * D + 128 - full["fox_w_in"].shape[1])))
    w_qkv, w_f = w_proj[:, :3 * D], w_proj[:, 3 * D:]

    hn2 = _modulate(h2, gain[2], shift[2], scale[2], name="modulate2")
    nb = S // tb
    w_qkv_s = jnp.concatenate([w_qkv[:, :D] * Q_SCALE, w_qkv[:, D:]], axis=1)
    qkv, qkvT = _mm(hn2, w_qkv_s, name="fox_qkv", out_dtype=BF16, with_t=True)
    fl = _mm(hn2, w_f, name="fox_f", out_dtype=F32)
    fcol = _fgate_fwd(fl, bf_pad, name="fox_fgate")
    f_heads = fcol[:, :HEADS].T.reshape(HP, 2, S)
    ka = _fox_keys(qkv, fcol, name="fox_keys")
    vtb = _blocked(qkvT[2 * D:].reshape(HP, 128, S), nb, tb)
    aug = _aug_rows(f_heads)
    (oT, lse), (g_up1, g_down1) = _foxt_fwd(qkvT, ka, vtb, aug, tb=tb, name="fox_attn",
                                            comm=[(b16(ffn_w_up[1]), True), (b16(ffn_w_down[1]), True)])
    full["ffn_w_up1"] = _cols_join(g_up1)
    full["ffn_w_down1"] = g_down1.reshape(F, D)
    m2, h3 = _mm(oT, full["fox_w_out"], name="fox_out", ta=True, res=h2, gate=gate[2])
    h4, (hn3, up1, z1, m3), _ = ffn_fwd(h3, 3, 1)

    lblk, dh, head_sums, dm = _loss_head(h4, _row(final_g), loss_target[0], m3, gate[3], name="loss_head")
    d_final_g, dgate = head_sums[0:1], head_sums[1:2]
    loss = lax.psum(lblk[0, 0], ("x", "y", "c"))

    G = {}
    dmod = [None] * 4
    mixer_out = [m0, m1, m2, m3]

    def norm_bwd(h, dhn, dh_in, k, dgate_k):
        if k == 0:
            dh_out, sums = _modulate_bwd(h, dhn, dh_in, gain[k], scale[k], name=f"modulate_bwd{k}")
            dm_below = dgate_below = None
        else:
            dh_out, sums, dm_below = _modulate_bwd(h, dhn, dh_in, gain[k], scale[k], mixer_out[k - 1], gate[k - 1],
                                                   name=f"modulate_bwd{k}")
            dgate_below = sums[3:4]
        dmod[k] = jnp.concatenate([sums[0:1], sums[1:2], dgate_k], axis=1)
        return dh_out, sums[2], dm_below, dgate_below

    def ffn_bwd(dh_in, dm, dgate_k, h, k, layer, saved):
        hn, up, z, m = saved
        dz = _mm(dm, full[f"ffn_w_down{layer}"], name=f"ffn_down_dx{layer}", tb=True, out_dtype=BF16, tn=1408)
        dw_down = _mm(z, dm, name=f"ffn_down_dw{layer}", ta=True, tm=1408, tk=2048)
        d_up, cs = _conv_gate_bwd(up, dz, cw_full[layer], ffn_conv_b[layer:layer + 1], name=f"conv_gate_bwd{layer}")
        dhn = _mm(d_up, full[f"ffn_w_up{layer}"], name=f"ffn_up_dx{layer}", tb=True, a_h=True, tk=2816)
        dw_up = _mm(hn, d_up, name=f"ffn_up_dw{layer}", ta=True, b_h=True, tm=1024, tn=1408, tk=2048)
        dh_out, dg, dm_below, dgate_below = norm_bwd(h, dhn, dh_in, k, dgate_k)
        return dh_out, dm_below, dgate_below, dg, dw_up, dw_down, cs[0:3], cs[3]

    dh, dm, dgate, dg3, dw_up1, dw_down1, dcw1, dcb1 = ffn_bwd(dh, dm, dgate, h3, 3, 1, (hn3, up1, z1, m3))

    do, doT = _mm(dm, full["fox_w_out"], name="fox_out_dx", tb=True, out_dtype=BF16, with_t=True)
    dw_fox_out = _mm(oT, dm, name="fox_out_dw", tk=2048)
    to_rows = lambda g: b16(g).reshape(N_DEV, -1, D)
    (dqtb, dk, dv, dfk, dfq), (r_up1, r_down1) = _foxt_bwd(
        qkvT, qkv, ka, _blocked(qkvT[:D].reshape(HP, 128, S), nb, tb), _blocked(doT.reshape(HP, 128, S), nb, tb), do,
        _blocked(aug, nb, tb), _blocked(lse, nb, tb), _blocked(oT.reshape(HP, 128, S), nb, tb), tb=tb, name="fox_attn_bwd",
        comm=[(_cols_split(b16(dw_up1)), False), (to_rows(dw_down1), False)])
    dq = dqtb.transpose(1, 3, 0, 2).reshape(S, D)
    dF = dfk[:, :, :2].transpose(1, 0, 2).reshape(S, HEADS) + dfq[:, :, :2, :].transpose(1, 3, 0, 2).reshape(S, HEADS)
    dF = jnp.pad(dF, ((0, 0), (0, 128 - HEADS)))
    dfl, dbf = _fgate_bwd(dF, fl, bf_pad, name="fox_fgate_bwd")
    dproj = jnp.concatenate([dq, dk, dv, dfl.astype(BF16)], axis=1)
    dhn = _mm(dproj, w_proj, name="fox_in_dx", tb=True)
    dw_proj = _mm(hn2, dproj, name="fox_in_dw", ta=True, tn=640, tk=2048)
    dw_fox_in = dw_proj[:, :full["fox_w_in"].shape[1]]
    dh, dg2, dm, dgate = norm_bwd(h2, dhn, dh, 2, dgate)

    dh, dm, dgate, dg1, dw_up0, dw_down0, dcw0, dcb0 = ffn_bwd(dh, dm, dgate, h1, 1, 0, (hn1, up0, z0, m1))

    dy2 = _mm(dm, full["s5_w_out"], name="s5_out_dx", tb=True, out_dtype=BF16)
    G["s5_w_out"] = _mm(y2, dm, name="s5_out_dw", ta=True, tk=2048)
    da, dgl = _glu_bwd(dy2, yg, gl, name="s5_glu_bwd")
    dyg_b = _mm(dgl, full["s5_w_glu"], name="s5_glu_dx", tb=True)
    G["s5_w_glu"] = _mm(yg, dgl, name="s5_glu_dw", ta=True, tk=2048)
    dyp, dd = _gelu_bwd(da, dyg_b, ypre, u, name="s5_gelu_bwd")
    row_grads = [dw_fox_out, dw_down0, G["s5_w_out"], G["s5_w_glu"]]
    (du, dwb, dwc, dlam), (r_up0, r_rows0, r_fox) = _s5_bwd(
        u, dyp, st, wb, wc, lam, s5_d, lt=lt, name="s5_scan_bwd",
        comm=[(_cols_split(b16(dw_up0)), False), (jnp.concatenate([to_rows(g) for g in row_grads], axis=1), False),
              (_cols_split(b16(dw_fox_in)), False)])
    dhn = _mm(du, full["s5_w_in"], name="s5_in_dx", tb=True)
    G["s5_w_in"] = _mm(hn0, du, name="s5_in_dw", ta=True, tk=2048)
    dh, dg0, _, _ = norm_bwd(h0, dhn, dh, 0, dgate)
    grad_x = dh[None]

    dlb_re, dlb_im, dbb_re, dbb_im, dc_re, dc_im = _s5_unpack(dwb, dwc, dlam)
    _, disc_vjp = jax.vjp(_s5_disc, s5_lam_re[0], s5_lam_im[0], s5_log_dt[0], s5_b_re[0], s5_b_im[0])
    dlam_re, dlam_im, dlog_dt, db_re, db_im = disc_vjp((dlb_re, dlb_im, dbb_re, dbb_im))

    small_full = {
        "norm_g": jnp.stack([dg0, dg1, dg2, dg3]).reshape(2, 2, D),
        "ffn_conv_w": jnp.stack([dcw0, dcw1]),
        "s5_lam_re": dlam_re[None], "s5_lam_im": dlam_im[None], "s5_log_dt": dlog_dt[None],
        "s5_b_re": db_re[None], "s5_b_im": db_im[None], "s5_c_re": dc_re[None], "s5_c_im": dc_im[None],
        "s5_d": dd, "fox_b_f": dbf[:, :HEADS], "ffn_conv_b": jnp.stack([dcb0, dcb1]), "final_g": d_final_g[0],
    }
    names = tuple(small_full)
    small_pack = _pack([small_full[n] for n in names], F32, 8 * N_DEV)
    r_in, r_small, g2 = _exchange_many([(to_rows(G["s5_w_in"]), False), (small_pack.reshape(N_DEV, -1, 1024), False),
                                        (_pack([jnp.concatenate(dmod, axis=0)], F32, 8), True)], name="last_exchange")
    G["s5_w_in"] = _sum8(r_in, name="sum_s5_in_grads")[None]
    (dmod_all,) = _unpack(g2, [(4, 3 * D)])
    d_ada_w, d_ada_b = _adaln_bwd(c_all, _columns_of_mod(dmod_all, me, ncol), name="adaln_bwd")
    G["ada_w"] = d_ada_w.reshape(ada_w.shape)
    G["ada_b"] = d_ada_b.reshape(ada_b.shape)
    rows0_sum = _sum8(r_rows0, name="sum_rows0_grads")
    down_rows = F // N_DEV
    G["fox_w_out"] = rows0_sum[:rows][None]
    G["s5_w_out"] = rows0_sum[rows + down_rows:2 * rows + down_rows][None]
    G["s5_w_glu"] = rows0_sum[2 * rows + down_rows:][None]
    G["ffn_w_down"] = jnp.stack([rows0_sum[rows:rows + down_rows], _sum8(r_down1, name="sum_down1_grads")])
    G["ffn_w_up"] = jnp.stack([_sum8(r_up0, name="sum_up0_grads"), _sum8(r_up1, name="sum_up1_grads")])
    G["fox_w_in"] = _sum8(r_fox, name="sum_fox_w_in")[None]

    g3 = _exchange(_sum8(r_small, name="sum_small_grads"), gather=True, name="gather_small_sums")
    summed = dict(zip(names, _unpack(g3.reshape(small_pack.shape), [small_full[n].shape for n in names])))
    G["norm_g"] = lax.dynamic_slice_in_dim(summed["norm_g"], me * norm_g.shape[-1], norm_g.shape[-1], axis=2)
    G["ffn_conv_w"] = lax.dynamic_slice_in_dim(summed["ffn_conv_w"], me * ffn_conv_w.shape[-1], ffn_conv_w.shape[-1], axis=2)
    for n in REPLICATED:
        G[n] = summed[n]

    delta, new_m, new_v = {}, {}, {}
    small = tuple(n for n in WEIGHTS if n not in BIG and n != "ada_w")
    for n in WEIGHTS:
        if n not in small:
            delta[n], new_m[n], new_v[n] = _adamw_nd(W[n], G[n], M[n], V[n], name=f"adamw_{n}")
    packed = [_pack([src[n] for n in small], F32, 8) for src in (W, G, M, V)]
    for dst, buf in zip((delta, new_m, new_v), _adamw(*packed, name="adamw_small")):
        dst.update(zip(small, _unpack(buf, [W[n].shape for n in small])))

    return (loss, grad_x, *[G[n] for n in WEIGHTS], *[delta[n] for n in WEIGHTS], *[new_m[n] for n in WEIGHTS],
            *[new_v[n] for n in WEIGHTS])


def _columns_of_mod(dmod_all, me, ncol):
    flat = lax.dynamic_slice_in_dim(dmod_all, me * ncol, ncol, axis=2)
    return flat.transpose(1, 0, 2)
```

```python
import functools
import math

import jax
import jax.numpy as jnp
from jax import lax
from jax.experimental import pallas as pl
from jax.experimental.pallas import tpu as pltpu

F32, BF16 = jnp.float32, jnp.bfloat16
EPS = 1e-6
N_DEV = 8
D_MODEL = 1024
D_FF = 2816
HEADS = 16
HEAD_DIM = 64
S5_G, S5_P, S5_C = 64, 64, 16
S5_SUB = 8
V7X_VMEM_LIMIT = 56 * 1024 * 1024
NEG = -1e30
ADAM_LR, ADAM_B1, ADAM_B2, ADAM_EPS, ADAM_WD, ADAM_STEP = 1e-3, 0.9, 0.999, 1e-8, 0.01, 10
GELU_K = math.sqrt(2.0 / math.pi)
MESH_ID = pl.DeviceIdType.MESH


def _cp(*sem):
    return pltpu.CompilerParams(dimension_semantics=sem, vmem_limit_bytes=V7X_VMEM_LIMIT)


def _tile(n, target, mult=128):
    if n <= target:
        return n
    t = (target // mult) * mult
    while t >= mult:
        if n % t == 0:
            return t
        t -= mult
    return n


def _row(v):
    return v.reshape(1, -1).astype(F32)


def _mm(a, b, *, name, ta=False, tb=False, out_dtype=F32, tm=1024, tn=1024, tk=None, res=None, gate=None,
        a_h=False, b_h=False, o_h=False, with_t=False, comm=()):
    if a_h:
        M, K = a.shape[1], 2 * a.shape[2]
    elif ta:
        K, M = a.shape
    else:
        M, K = a.shape
    if b_h:
        N = 2 * b.shape[2]
    else:
        N = b.shape[0] if tb else b.shape[1]
    half_n = N // 2
    tm = _tile(M, tm, 128 if ta else 8)
    tn = _tile(half_n if (b_h or o_h) else N, tn)
    tk = K if tk is None else _tile(K // 2 if a_h else K, tk)
    nk = K // tk
    nkh, nnh = (K // 2) // tk if a_h else 1, half_n // tn
    if a_h:
        a_spec = pl.BlockSpec((None, tm, tk), lambda i, j, k: (k // nkh, i, k % nkh))
    elif ta:
        a_spec = pl.BlockSpec((tk, tm), lambda i, j, k: (k, i))
    else:
        a_spec = pl.BlockSpec((tm, tk), lambda i, j, k: (i, k))
    if b_h:
        b_spec = pl.BlockSpec((None, tk, tn), lambda i, j, k: (j // nnh, k, j % nnh))
    elif tb:
        b_spec = pl.BlockSpec((tn, tk), lambda i, j, k: (j, k))
    else:
        b_spec = pl.BlockSpec((tk, tn), lambda i, j, k: (k, j))
    if o_h:
        o_spec = pl.BlockSpec((None, tm, tn), lambda i, j, k: (j // nnh, i, j % nnh))
    else:
        o_spec = pl.BlockSpec((tm, tn), lambda i, j, k: (i, j))
    dn = (((0 if ta else 1,), (1 if tb else 0,)), ((), ()))
    fused = res is not None

    def body(*refs):
        if fused:
            a_ref, b_ref, r_ref, g_ref, m_ref, o_ref, acc_ref = refs
        elif with_t:
            a_ref, b_ref, o_ref, t_ref, acc_ref = refs
        else:
            a_ref, b_ref, o_ref, acc_ref = refs
        p = lax.dot_general(a_ref[...].astype(BF16), b_ref[...].astype(BF16), dn, preferred_element_type=F32)

        def finish(acc):
            if fused:
                m_ref[...] = acc.astype(m_ref.dtype)
                o_ref[...] = r_ref[...] + g_ref[...] * acc
            else:
                o_ref[...] = acc.astype(o_ref.dtype)
                if with_t:
                    t_ref[...] = acc.T.astype(t_ref.dtype)

        if nk == 1:
            finish(p)
        else:
            k = pl.program_id(2)

            @pl.when(k == 0)
            def _():
                acc_ref[...] = p

            @pl.when(k > 0)
            def _():
                acc_ref[...] += p

            @pl.when(k == nk - 1)
            def _():
                finish(acc_ref[...])

    in_specs = [a_spec, b_spec]
    args = [a, b]
    if fused:
        in_specs += [o_spec, pl.BlockSpec((1, tn), lambda i, j, k: (0, j))]
        args += [res, gate]
        out_shape = [jax.ShapeDtypeStruct((M, N), BF16), jax.ShapeDtypeStruct((M, N), F32)]
        out_specs = [o_spec, o_spec]
    else:
        out_shape = [jax.ShapeDtypeStruct((2, M, half_n) if o_h else (M, N), out_dtype)]
        out_specs = [o_spec]
        if with_t:
            out_shape.append(jax.ShapeDtypeStruct((N, M), out_dtype))
            out_specs.append(pl.BlockSpec((tn, tm), lambda i, j, k: (j, i)))
    outs, exchanged = _call(
        body, name=name, grid=(M // tm, N // tn, nk), in_specs=in_specs, out_specs=out_specs, out_shape=out_shape,
        scratch_shapes=[pltpu.VMEM((tm, tn) if nk > 1 else (8, 128), F32)], sem=("parallel", "parallel", "arbitrary"),
        args=args, comm=comm)
    outs = tuple(outs) if (fused or with_t) else outs[0]
    return (outs, exchanged) if comm else outs


def _modulate(h, g, shift, scale, *, name):
    S, D = h.shape
    tm = _tile(S, 512, 8)
    vec = pl.BlockSpec((1, D), lambda i: (0, 0))
    blk = pl.BlockSpec((tm, D), lambda i: (i, 0))

    def body(h_ref, g_ref, sh_ref, sc_ref, o_ref):
        x = h_ref[...]
        r = lax.rsqrt(jnp.mean(x * x, axis=-1, keepdims=True) + EPS)
        o_ref[...] = ((x * r * g_ref[...]) * (1.0 + sc_ref[...]) + sh_ref[...]).astype(o_ref.dtype)

    return pl.pallas_call(body, name=name, grid=(S // tm,), in_specs=[blk, vec, vec, vec], out_specs=blk,
                          out_shape=jax.ShapeDtypeStruct((S, D), BF16), compiler_params=_cp("parallel"))(h, g, shift, scale)


def _modulate_bwd(h, dhn, dh_in, g, scale, m_prev=None, gate_prev=None, *, name):
    S, D = h.shape
    tm = _tile(S, 512, 8)
    vec = pl.BlockSpec((1, D), lambda i: (0, 0))
    blk = pl.BlockSpec((tm, D), lambda i: (i, 0))
    sums = pl.BlockSpec((8, D), lambda i: (0, 0))
    below = m_prev is not None

    def body(h_ref, dhn_ref, dhi_ref, g_ref, sc_ref, *rest):
        dh_ref, s_ref = rest[-3:-1] if below else rest[-2:]

        @pl.when(pl.program_id(0) == 0)
        def _():
            s_ref[...] = jnp.zeros_like(s_ref)

        x = h_ref[...]
        r = lax.rsqrt(jnp.mean(x * x, axis=-1, keepdims=True) + EPS)
        xhat = x * r
        gv = g_ref[...]
        dhn_v = dhn_ref[...].astype(F32)
        dn = dhn_v * (1.0 + sc_ref[...])
        s_ref[0:1, :] += jnp.sum(dhn_v, axis=0, keepdims=True)
        s_ref[1:2, :] += jnp.sum(dhn_v * (xhat * gv), axis=0, keepdims=True)
        s_ref[2:3, :] += jnp.sum(dn * xhat, axis=0, keepdims=True)
        dxh = dn * gv
        dh = dhi_ref[...] + r * (dxh - xhat * jnp.mean(dxh * xhat, axis=-1, keepdims=True))
        dh_ref[...] = dh
        if below:
            mp_ref, gp_ref, dm_ref = rest[0], rest[1], rest[-1]
            dm_ref[...] = (dh * gp_ref[...]).astype(dm_ref.dtype)
            s_ref[3:4, :] += jnp.sum(dh * mp_ref[...].astype(F32), axis=0, keepdims=True)

    extra_in, extra_args = ([blk, vec], [m_prev, gate_prev]) if below else ([], [])
    return pl.pallas_call(body, name=name, grid=(S // tm,), in_specs=[blk, blk, blk, vec, vec] + extra_in,
                          out_specs=[blk, sums] + ([blk] if below else []),
                          out_shape=[jax.ShapeDtypeStruct((S, D), F32), jax.ShapeDtypeStruct((8, D), F32)]
                          + ([jax.ShapeDtypeStruct((S, D), BF16)] if below else []),
                          compiler_params=_cp("arbitrary"))(h, dhn, dh_in, g, scale, *extra_args)


def _loss_head(h, g, tgt, m_prev, gate_prev, *, name):
    S, D = h.shape
    tm = _tile(S, 512, 8)
    vec = pl.BlockSpec((1, D), lambda i: (0, 0))
    blk = pl.BlockSpec((tm, D), lambda i: (i, 0))
    lss = pl.BlockSpec((8, 128), lambda i: (0, 0))
    sums = pl.BlockSpec((8, D), lambda i: (0, 0))

    def body(h_ref, g_ref, t_ref, mp_ref, gp_ref, l_ref, dh_ref, s_ref, dm_ref):
        @pl.when(pl.program_id(0) == 0)
        def _():
            l_ref[...] = jnp.zeros_like(l_ref)
            s_ref[...] = jnp.zeros_like(s_ref)

        x = h_ref[...]
        r = lax.rsqrt(jnp.mean(x * x, axis=-1, keepdims=True) + EPS)
        xhat = x * r
        gv = g_ref[...]
        e = xhat * gv - t_ref[...]
        l_ref[...] += 0.5 * jnp.sum(jnp.mean(e * e, axis=-1, keepdims=True))
        dy = e * (1.0 / D)
        s_ref[0:1, :] += jnp.sum(dy * xhat, axis=0, keepdims=True)
        dxh = dy * gv
        dh = r * (dxh - xhat * jnp.mean(dxh * xhat, axis=-1, keepdims=True))
        dh_ref[...] = dh
        dm_ref[...] = (dh * gp_ref[...]).astype(dm_ref.dtype)
        s_ref[1:2, :] += jnp.sum(dh * mp_ref[...].astype(F32), axis=0, keepdims=True)

    return pl.pallas_call(body, name=name, grid=(S // tm,), in_specs=[blk, vec, blk, blk, vec], out_specs=[lss, blk, sums, blk],
                          out_shape=[jax.ShapeDtypeStruct((8, 128), F32), jax.ShapeDtypeStruct((S, D), F32),
                                     jax.ShapeDtypeStruct((8, D), F32), jax.ShapeDtypeStruct((S, D), BF16)],
                          compiler_params=_cp("arbitrary"))(h, g, tgt, m_prev, gate_prev)


def _shift_down(x, k, edge):
    tm = x.shape[0]
    rows = lax.broadcasted_iota(jnp.int32, x.shape, 0)
    out = pltpu.roll(x, k, 0)
    for j in range(k):
        out = jnp.where(rows == j, edge[8 - k + j:8 - k + j + 1, :], out)
    return out


def _shift_up(x, k, edge):
    tm = x.shape[0]
    rows = lax.broadcasted_iota(jnp.int32, x.shape, 0)
    out = pltpu.roll(x, tm - k, 0)
    for j in range(k):
        out = jnp.where(rows == tm - k + j, edge[j:j + 1, :], out)
    return out


def _conv_gate_fwd(up, cw, cb, *, name, comm=()):
    _, S, F = up.shape
    tf = _tile(F, 1408)
    nf = F // tf
    tm = _tile(S, 512, 8)

    def body(a_ref, b_ref, w_ref, cb_ref, z_ref, edge_ref):
        @pl.when(pl.program_id(1) == 0)
        def _():
            edge_ref[...] = jnp.zeros_like(edge_ref)

        a = a_ref[...].astype(F32)
        edge = edge_ref[...]
        w = w_ref[...]
        ac = cb_ref[...] + w[2:3, :] * a + w[1:2, :] * _shift_down(a, 1, edge) + w[0:1, :] * _shift_down(a, 2, edge)
        edge_ref[...] = a[tm - 8:tm, :]
        z_ref[...] = (ac * jax.nn.sigmoid(ac) * b_ref[...].astype(F32)).astype(z_ref.dtype)

    return _call(
        body, name=name, grid=(nf, S // tm), comm=comm, sem=("parallel", "arbitrary"), args=(up, up, cw, cb),
        in_specs=[pl.BlockSpec((None, tm, tf), lambda j, i: (0, i, j)), pl.BlockSpec((None, tm, tf), lambda j, i: (1, i, j)),
                  pl.BlockSpec((3, tf), lambda j, i: (0, j)), pl.BlockSpec((1, tf), lambda j, i: (0, j))],
        out_specs=[pl.BlockSpec((tm, tf), lambda j, i: (i, j))],
        out_shape=[jax.ShapeDtypeStruct((S, F), BF16)], scratch_shapes=[pltpu.VMEM((8, tf), F32)])


def _conv_gate_bwd(up, dz, cw, cb, *, name):
    _, S, F = up.shape
    tf = _tile(F, 1408)
    nf = F // tf
    tm = _tile(S, 512, 8)
    nt = S // tm
    hb = tm // 8

    def body(a_ref, ah_ref, b_ref, dz_ref, w_ref, cb_ref, d_ref, s_ref, edge_ref):
        i = pl.program_id(1)

        @pl.when(i == 0)
        def _():
            edge_ref[...] = jnp.zeros_like(edge_ref)
            s_ref[...] = jnp.zeros_like(s_ref)

        a = a_ref[...].astype(F32)
        halo = jnp.where(i == nt - 1, 0.0, ah_ref[...].astype(F32))
        w = w_ref[...]
        a1 = _shift_down(a, 1, halo)
        a2 = _shift_down(a, 2, halo)
        ac = cb_ref[...] + w[2:3, :] * a + w[1:2, :] * a1 + w[0:1, :] * a2
        sg = jax.nn.sigmoid(ac)
        dzv = dz_ref[...].astype(F32)
        si = ac * sg
        d_ref[1] = (dzv * si).astype(d_ref.dtype)
        dac = (dzv * b_ref[...].astype(F32)) * (sg + si * (1.0 - sg))
        s_ref[0:1, :] += jnp.sum(dac * a2, axis=0, keepdims=True)
        s_ref[1:2, :] += jnp.sum(dac * a1, axis=0, keepdims=True)
        s_ref[2:3, :] += jnp.sum(dac * a, axis=0, keepdims=True)
        s_ref[3:4, :] += jnp.sum(dac, axis=0, keepdims=True)
        edge = edge_ref[...]
        da = w[2:3, :] * dac + w[1:2, :] * _shift_up(dac, 1, edge) + w[0:1, :] * _shift_up(dac, 2, edge)
        edge_ref[...] = dac[0:8, :]
        d_ref[0] = da.astype(d_ref.dtype)

    tile = lambda hlf: pl.BlockSpec((None, tm, tf), lambda j, i: (hlf, nt - 1 - i, j))
    d_up, sums = pl.pallas_call(
        body, name=name, grid=(nf, nt),
        in_specs=[tile(0),
                  pl.BlockSpec((None, 8, tf), lambda j, i: (0, jnp.maximum((nt - 1 - i) * hb - 1, 0), j)),
                  tile(1), pl.BlockSpec((tm, tf), lambda j, i: (nt - 1 - i, j)),
                  pl.BlockSpec((3, tf), lambda j, i: (0, j)), pl.BlockSpec((1, tf), lambda j, i: (0, j))],
        out_specs=[pl.BlockSpec((2, tm, tf), lambda j, i: (0, nt - 1 - i, j)), pl.BlockSpec((8, tf), lambda j, i: (0, j))],
        out_shape=[jax.ShapeDtypeStruct((2, S, F), BF16), jax.ShapeDtypeStruct((8, F), F32)],
        scratch_shapes=[pltpu.VMEM((8, tf), F32)],
        compiler_params=_cp("parallel", "arbitrary"))(up, up, up, dz, cw, cb)
    return d_up, sums


def _log_sigmoid(x):
    return jnp.minimum(x, 0.0) - jnp.log(1.0 + jnp.exp(-jnp.abs(x)))


def _tri_ones(n, upper):
    r = lax.broadcasted_iota(jnp.int32, (n, n), 0)
    c = lax.broadcasted_iota(jnp.int32, (n, n), 1)
    return jnp.where((c >= r) if upper else (c <= r), 1.0, 0.0).astype(F32)


def _fgate_fwd(fl, bf, *, name):
    S, W = fl.shape
    tb = _tile(S, 256, 8)

    def body(fl_ref, b_ref, o_ref, carry_ref):
        @pl.when(pl.program_id(0) == 0)
        def _():
            carry_ref[...] = jnp.zeros_like(carry_ref)

        lf = _log_sigmoid(fl_ref[...] + b_ref[...])
        cs = jnp.dot(_tri_ones(tb, False), lf, precision=lax.Precision.HIGHEST, preferred_element_type=F32) + carry_ref[0:1, :]
        o_ref[...] = cs
        carry_ref[...] = jnp.broadcast_to(cs[tb - 1:tb, :], carry_ref.shape)

    blk = pl.BlockSpec((tb, W), lambda i: (i, 0))
    return pl.pallas_call(body, name=name, grid=(S // tb,), in_specs=[blk, pl.BlockSpec((1, W), lambda i: (0, 0))], out_specs=blk,
                          out_shape=jax.ShapeDtypeStruct((S, W), F32), scratch_shapes=[pltpu.VMEM((8, W), F32)],
                          compiler_params=_cp("arbitrary"))(fl, bf)


def _fgate_bwd(dF, fl, bf, *, name):
    S, W = fl.shape
    tb = _tile(S, 256, 8)
    nb = S // tb

    def body(d_ref, fl_ref, b_ref, o_ref, s_ref, carry_ref):
        @pl.when(pl.program_id(0) == 0)
        def _():
            carry_ref[...] = jnp.zeros_like(carry_ref)
            s_ref[...] = jnp.zeros_like(s_ref)

        rc = jnp.dot(_tri_ones(tb, True), d_ref[...], precision=lax.Precision.HIGHEST, preferred_element_type=F32) + carry_ref[0:1, :]
        carry_ref[...] = jnp.broadcast_to(rc[0:1, :], carry_ref.shape)
        dfl = rc * jax.nn.sigmoid(-(fl_ref[...] + b_ref[...]))
        o_ref[...] = dfl
        s_ref[...] += jnp.sum(dfl, axis=0, keepdims=True)

    blk = pl.BlockSpec((tb, W), lambda i: (nb - 1 - i, 0))
    vec = pl.BlockSpec((1, W), lambda i: (0, 0))
    return pl.pallas_call(body, name=name, grid=(nb,), in_specs=[blk, blk, vec], out_specs=[blk, vec],
                          out_shape=[jax.ShapeDtypeStruct((S, W), F32), jax.ShapeDtypeStruct((1, W), F32)],
                          scratch_shapes=[pltpu.VMEM((8, W), F32)], compiler_params=_cp("arbitrary"))(dF, fl, bf)


NT_DIMS = (((1,), (1,)), ((), ()))
TN_DIMS = (((0,), (0,)), ((), ()))
HP = HEADS // 2
Q_SCALE = HEAD_DIM ** -0.5


def _head_mask(x, hh):
    lanes = lax.broadcasted_iota(jnp.int32, x.shape, 1)
    return jnp.where((lanes >= hh * HEAD_DIM) & (lanes < (hh + 1) * HEAD_DIM), x, jnp.zeros_like(x))


def _lanes_from(cols, shape):
    lanes = lax.broadcasted_iota(jnp.int32, shape, 1)
    out = jnp.zeros(shape, F32)
    for i, cvec in enumerate(cols):
        out = jnp.where(lanes == i, cvec, out)
    return out


AUG_ROWS = 16
DIAGONAL_TILES = ((0, 0, True), (0, 1, False), (1, 1, True))


def _qa_pair(qt, aug):
    fill = jnp.zeros((HEAD_DIM - AUG_ROWS, qt.shape[1]), qt.dtype)
    return [jnp.concatenate([qt[0:HEAD_DIM], aug[0], fill], axis=0), jnp.concatenate([aug[1], fill, qt[HEAD_DIM:]], axis=0)]


def _rows_of_head(xt, hh):
    rows = lax.broadcasted_iota(jnp.int32, xt.shape, 0)
    return jnp.where((rows >= hh * HEAD_DIM) & (rows < (hh + 1) * HEAD_DIM), xt, jnp.zeros_like(xt))


def _put_rows(ref, rows):
    for i, r in enumerate(rows):
        ref[i:i + 1, :] = r
    ref[len(rows):, :] = jnp.zeros((ref.shape[0] - len(rows), ref.shape[1]), ref.dtype)


def _diag_mask_t(s):
    rows = lax.broadcasted_iota(jnp.int32, s.shape, 0)
    cols = lax.broadcasted_iota(jnp.int32, s.shape, 1)
    return jnp.where(rows <= cols, s, NEG)


def _foxt_fwd(qkvT, ka, vtb, augq, *, tb, name, comm=()):
    S = qkvT.shape[1]
    nb = S // tb

    def body(q_ref, ka_ref, v_ref, aq_ref, o_ref, lse_ref, m0, m1, l0, l1, acc0, acc1):
        qi = pl.program_id(1)
        qa = _qa_pair(q_ref[...], aq_ref[...])
        half = tb // 2
        state = ((m0, l0, acc0), (m1, l1, acc1))
        for m_sc, l_sc, acc_sc in state:
            m_sc[...] = jnp.full_like(m_sc, NEG)
            l_sc[...] = jnp.zeros_like(l_sc)
            acc_sc[...] = jnp.zeros_like(acc_sc)

        def tile(kj, ko, qo, n, masked):
            k0 = pl.multiple_of(kj * tb + ko, n)
            cols = slice(qo, qo + n)
            logits = [jnp.dot(ka_ref[hh, pl.ds(k0, n), :], qa[hh][:, cols], preferred_element_type=F32) for hh in range(2)]
            updates = []
            for hh, (m_sc, l_sc, acc_sc) in enumerate(state):
                s = _diag_mask_t(logits[hh]) if masked else logits[hh]
                m_old = m_sc[:, cols]
                m_new = jnp.maximum(m_old, jnp.max(s, axis=0, keepdims=True))
                alpha = jnp.exp(m_old - m_new)
                p = jnp.exp(s - m_new)
                l_sc[:, cols] = alpha * l_sc[:, cols] + jnp.sum(p, axis=0, keepdims=True)
                m_sc[:, cols] = m_new
                vt = v_ref[kj, HEAD_DIM * hh:HEAD_DIM * (hh + 1), ko:ko + n]
                updates.append((alpha, jnp.dot(vt, p.astype(BF16), preferred_element_type=F32)))
            for (alpha, pv), (_, _, acc_sc) in zip(updates, state):
                acc_sc[:, cols] = alpha * acc_sc[:, cols] + pv

        def off_diagonal(kj, c):
            tile(kj, 0, 0, tb, False)
            return c

        lax.fori_loop(0, qi, off_diagonal, 0)
        for ko, qo, masked in DIAGONAL_TILES:
            tile(qi, ko * half, qo * half, half, masked)
        for hh, (m_sc, l_sc, acc_sc) in enumerate(state):
            o_ref[HEAD_DIM * hh:HEAD_DIM * (hh + 1), :] = (acc_sc[...] / l_sc[...]).astype(o_ref.dtype)
        _put_rows(lse_ref, [m_sc[...] + jnp.log(l_sc[...]) for m_sc, l_sc, _ in state])

    return _call(
        body, name=name, grid=(HP, nb), comm=comm, sem=("parallel", "parallel"), args=(qkvT, ka, vtb, augq),
        in_specs=[pl.BlockSpec((128, tb), lambda hp, qi: (hp, qi)),
                  pl.BlockSpec((None, 2, S, 128), lambda hp, qi: (hp, 0, 0, 0)),
                  pl.BlockSpec((None, nb, 128, tb), lambda hp, qi: (hp, 0, 0, 0)),
                  pl.BlockSpec((None, 2, AUG_ROWS, tb), lambda hp, qi: (hp, 0, 0, qi))],
        out_specs=[pl.BlockSpec((128, tb), lambda hp, qi: (hp, qi)), pl.BlockSpec((None, 8, tb), lambda hp, qi: (hp, 0, qi))],
        out_shape=[jax.ShapeDtypeStruct((D_MODEL, S), BF16), jax.ShapeDtypeStruct((HP, 8, S), F32)],
        scratch_shapes=[pltpu.VMEM((1, tb), F32)] * 4 + [pltpu.VMEM((HEAD_DIM, tb), F32)] * 2)


def _foxt_bwd(qkvT, qkv, ka, qtb, dotb, do, augqb, lseb, otb, *, tb, name, comm=()):
    S = qkv.shape[0]
    nb = S // tb

    def body(ka_ref, v_ref, kt_ref, qt_ref, dot_ref, qn_ref, dn_ref, aq_ref, ls_ref, ot_ref,
             dq_ref, dk_ref, dv_ref, df_ref, dr_ref, dq_sc, dk_sc, dv_sc, ds_sc, dr_sc):
        kj = pl.program_id(1)

        @pl.when(kj == 0)
        def _():
            dq_sc[...] = jnp.zeros_like(dq_sc)
            dr_sc[...] = jnp.zeros_like(dr_sc)

        dk_sc[...] = jnp.zeros_like(dk_sc)
        dv_sc[...] = jnp.zeros_like(dv_sc)
        ds_sc[...] = jnp.zeros_like(ds_sc)
        v2 = v_ref[...]
        kth = [kt_ref[HEAD_DIM * hh:HEAD_DIM * (hh + 1), :] for hh in range(2)]

        def block(qi, tiles):
            q0 = pl.multiple_of(qi * tb, tb)
            qa = _qa_pair(qt_ref[qi], aq_ref[qi])
            dot, ls = dot_ref[qi], ls_ref[qi]
            prod = ot_ref[qi].astype(F32) * dot.astype(F32)
            delta = [jnp.sum(prod[HEAD_DIM * hh:HEAD_DIM * (hh + 1)], axis=0, keepdims=True) for hh in range(2)]
            qn, dn = qn_ref[pl.ds(q0, tb), :], dn_ref[pl.ds(q0, tb), :]
            for ko, qo, n, masked in tiles:
                keys, cols = slice(ko, ko + n), slice(qo, qo + n)
                for hh in range(2):
                    s = jnp.dot(ka_ref[hh, keys, :], qa[hh][:, cols], preferred_element_type=F32)
                    if masked:
                        s = _diag_mask_t(s)
                    p = jnp.exp(s - ls[hh:hh + 1, cols])
                    dp = jnp.dot(v2[keys], _rows_of_head(dot[:, cols], hh), preferred_element_type=F32)
                    ds = p * (dp - delta[hh][:, cols])
                    dsb = ds.astype(BF16)
                    dv_sc[keys, :] += jnp.dot(p.astype(BF16), _head_mask(dn[cols], hh), preferred_element_type=F32)
                    dk_sc[keys, :] += jnp.dot(dsb, _head_mask(qn[cols], hh), preferred_element_type=F32)
                    dq_sc[qi, HEAD_DIM * hh:HEAD_DIM * (hh + 1), cols] += jnp.dot(kth[hh][:, keys], dsb, preferred_element_type=F32)
                    part = ds[:, 0:128]
                    for j in range(1, n // 128):
                        part = part + ds[:, 128 * j:128 * (j + 1)]
                    ds_sc[hh, keys, :] += part
                    dr_sc[qi, hh:hh + 1, cols] += jnp.sum(ds, axis=0, keepdims=True)

        def off_diagonal(i, c):
            block(kj + 1 + i, [(0, 0, tb, False)])
            return c

        half = tb // 2
        block(kj, [(ko * half, qo * half, half, masked) for ko, qo, masked in DIAGONAL_TILES])
        lax.fori_loop(0, nb - 1 - kj, off_diagonal, 0)
        dk_ref[...] = dk_sc[...].astype(dk_ref.dtype)
        dv_ref[...] = dv_sc[...].astype(dv_ref.dtype)
        df_ref[...] = _lanes_from([-jnp.sum(ds_sc[hh], axis=1, keepdims=True) for hh in range(2)], (tb, 128))

        @pl.when(kj == nb - 1)
        def _():
            dq_ref[...] = (dq_sc[...] * Q_SCALE).astype(dq_ref.dtype)
            dr_ref[...] = dr_sc[...]

    resident = lambda rows: pl.BlockSpec((None, nb) + rows, lambda hp, kj: (hp,) + (0,) * (len(rows) + 1))
    whole = pl.BlockSpec((S, 128), lambda hp, kj: (0, hp))
    kblk = lambda off: pl.BlockSpec((tb, 128), lambda hp, kj: (kj, off + hp))
    return _call(
        body, name=name, grid=(HP, nb), comm=comm, sem=("parallel", "arbitrary"),
        args=(ka, qkv, qkvT, qtb, dotb, qkv, do, augqb, lseb, otb),
        in_specs=[pl.BlockSpec((None, 2, tb, 128), lambda hp, kj: (hp, 0, kj, 0)), kblk(2 * HP),
                  pl.BlockSpec((128, tb), lambda hp, kj: (HP + hp, kj)),
                  resident((128, tb)), resident((128, tb)), whole, whole, resident((2, AUG_ROWS, tb)), resident((8, tb)),
                  resident((128, tb))],
        out_specs=[resident((128, tb)), kblk(0), kblk(0), pl.BlockSpec((None, tb, 128), lambda hp, kj: (hp, kj, 0)),
                   resident((8, tb))],
        out_shape=[jax.ShapeDtypeStruct((HP, nb, 128, tb), BF16), jax.ShapeDtypeStruct((S, D_MODEL), BF16),
                   jax.ShapeDtypeStruct((S, D_MODEL), BF16), jax.ShapeDtypeStruct((HP, S, 128), F32),
                   jax.ShapeDtypeStruct((HP, nb, 8, tb), F32)],
        scratch_shapes=[pltpu.VMEM((nb, 128, tb), F32), pltpu.VMEM((tb, 128), F32), pltpu.VMEM((tb, 128), F32),
                        pltpu.VMEM((2, tb, 128), F32), pltpu.VMEM((nb, 8, tb), F32)])


def _split3(x):
    rnd = lambda v: lax.reduce_precision(v, exponent_bits=8, mantissa_bits=7)
    hi = rnd(x)
    mid = rnd(x - hi)
    lo = rnd(x - hi - mid)
    return hi.astype(BF16), mid.astype(BF16), lo.astype(BF16)


def _blocked(xt, nb, tb):
    lead = xt.shape[:-1]
    x = xt.reshape(lead + (nb, tb))
    return jnp.moveaxis(x, -2, 1)


def _aug_rows(bias):
    ones = jnp.ones(bias.shape, BF16)
    zeros = jnp.zeros(bias.shape, BF16)
    return jnp.stack(list(_split3(bias)) + [ones] * 3 + [zeros] * (AUG_ROWS - 6), axis=2)


def _fox_keys(qkv, fcol, *, name):
    S = qkv.shape[0]
    tr = _tile(S, 2048, 16)

    def body(k_ref, f_ref, o_ref):
        hp = pl.program_id(1)
        k2 = k_ref[...]
        lanes = lax.broadcasted_iota(jnp.int32, (tr, 128), 1)
        for hh in range(2):
            nf = -jnp.sum(jnp.where(lanes == 2 * hp + hh, f_ref[...], 0.0), axis=1, keepdims=True)
            hi = nf.astype(BF16).astype(F32)
            mid = (nf - hi).astype(BF16).astype(F32)
            lo = nf - hi - mid
            base = HEAD_DIM * (1 - hh)
            aug = jnp.where((lanes >= base) & (lanes < base + 3), 1.0, 0.0)
            for j, piece in enumerate((hi, mid, lo)):
                aug = jnp.where(lanes == base + 3 + j, piece, aug)
            mine = (lanes >= HEAD_DIM * hh) & (lanes < HEAD_DIM * (hh + 1))
            o_ref[hh] = jnp.where(mine, k2, aug.astype(BF16))

    return pl.pallas_call(
        body, name=name, grid=(S // tr, HP),
        in_specs=[pl.BlockSpec((tr, 128), lambda i, hp: (i, HP + hp)), pl.BlockSpec((tr, 128), lambda i, hp: (i, 0))],
        out_specs=pl.BlockSpec((None, 2, tr, 128), lambda i, hp: (hp, 0, i, 0)),
        out_shape=jax.ShapeDtypeStruct((HP, 2, S, 128), BF16), compiler_params=_cp("parallel", "parallel"))(qkv, fcol)


HALF = S5_SUB * S5_P
NCB = 2 * HALF // 128
RE, IM = slice(0, NCB // 2), slice(NCB // 2, NCB)


def _gelu(x):
    return 0.5 * x * (1.0 + jnp.tanh(GELU_K * (x + 0.044715 * x * x * x)))


def _gelu_grad(x):
    t = jnp.tanh(GELU_K * (x + 0.044715 * x * x * x))
    return 0.5 * (1.0 + t) + 0.5 * x * (1.0 - t * t) * GELU_K * (1.0 + 3.0 * 0.044715 * x * x)


def _s5_put(buf, s, val, lt):
    for cb in range(NCB):
        buf[cb, pl.ds(s, lt, stride=S5_SUB), :] = val[:, 128 * cb:128 * (cb + 1)]


def _s5_get(buf, s, lt):
    return jnp.concatenate([buf[cb, pl.ds(s, lt, stride=S5_SUB), :] for cb in range(NCB)], axis=1)


def _s5_project_in(u_ref, wb_ref, buf, lt):
    for s in range(S5_SUB):
        _s5_put(buf, s, jnp.dot(u_ref[:, 128 * s:128 * (s + 1)], wb_ref[s], preferred_element_type=F32), lt)


def _s5_scan(buf, lam_ref, h0, lt):
    a_re, a_im = lam_ref[RE], lam_ref[IM]
    a2_re, a2_im = a_re * a_re - a_im * a_im, 2.0 * a_re * a_im

    def step(i, carry):
        hr, hi = carry
        r0 = pl.multiple_of(i * 2 * S5_SUB, 2 * S5_SUB)
        r1 = r0 + S5_SUB
        b0r, b0i = buf[RE, pl.ds(r0, S5_SUB), :], buf[IM, pl.ds(r0, S5_SUB), :]
        cr = a_re * b0r - a_im * b0i + buf[RE, pl.ds(r1, S5_SUB), :]
        ci = a_re * b0i + a_im * b0r + buf[IM, pl.ds(r1, S5_SUB), :]
        buf[RE, pl.ds(r0, S5_SUB), :] = a_re * hr - a_im * hi + b0r
        buf[IM, pl.ds(r0, S5_SUB), :] = a_re * hi + a_im * hr + b0i
        nr = a2_re * hr - a2_im * hi + cr
        ni = a2_re * hi + a2_im * hr + ci
        buf[RE, pl.ds(r1, S5_SUB), :] = nr
        buf[IM, pl.ds(r1, S5_SUB), :] = ni
        return nr, ni

    return lax.fori_loop(0, lt // 2, step, (h0[RE], h0[IM]))


def _s5_fwd(u, wb, wc, lam, dskip, *, lt, name, comm=()):
    S, D = u.shape
    nt = S // lt

    def body(u_ref, wb_ref, wc_ref, lam_ref, d_ref, yp_ref, yg_ref, st_ref, buf, h_sc):
        @pl.when(pl.program_id(0) == 0)
        def _():
            h_sc[...] = jnp.zeros_like(h_sc)

        st_ref[...] = h_sc[...]
        _s5_project_in(u_ref, wb_ref, buf, lt)
        hr, hi = _s5_scan(buf, lam_ref, h_sc[...], lt)
        h_sc[RE] = hr
        h_sc[IM] = hi
        for s in range(S5_SUB):
            cols = slice(128 * s, 128 * (s + 1))
            hs = _s5_get(buf, s, lt).astype(BF16)
            yp = jnp.dot(hs, wc_ref[s], preferred_element_type=F32) + d_ref[:, cols] * u_ref[:, cols].astype(F32)
            yp_ref[:, cols] = yp.astype(yp_ref.dtype)
            yg_ref[:, cols] = _gelu(yp).astype(yg_ref.dtype)

    blk = pl.BlockSpec((lt, D), lambda i: (i, 0))
    full = lambda shp: pl.BlockSpec(shp, lambda i: (0,) * len(shp))
    state = (NCB, S5_SUB, 128)
    return _call(
        body, name=name, grid=(nt,), comm=comm, sem=("arbitrary",), args=(u, wb, wc, lam, dskip),
        in_specs=[blk, full(wb.shape), full(wc.shape), full(lam.shape), full(dskip.shape)],
        out_specs=[blk, blk, pl.BlockSpec((None,) + state, lambda i: (i, 0, 0, 0))],
        out_shape=[jax.ShapeDtypeStruct((S, D), BF16), jax.ShapeDtypeStruct((S, D), BF16), jax.ShapeDtypeStruct((nt,) + state, F32)],
        scratch_shapes=[pltpu.VMEM((NCB, lt * S5_SUB, 128), F32), pltpu.VMEM(state, F32)])


def _s5_bwd(u, dyp, st, wb, wc, lam, dskip, *, lt, name, comm=()):
    S, D = u.shape
    nt = S // lt

    def body(u_ref, dy_ref, st_ref, wb_ref, wc_ref, lam_ref, d_ref, du_ref, dwb_ref, dwc_ref, dlam_ref, hbuf, gbuf, g_sc):
        @pl.when(pl.program_id(0) == 0)
        def _():
            g_sc[...] = jnp.zeros_like(g_sc)
            dwb_ref[...] = jnp.zeros_like(dwb_ref)
            dwc_ref[...] = jnp.zeros_like(dwc_ref)
            dlam_ref[...] = jnp.zeros_like(dlam_ref)

        _s5_project_in(u_ref, wb_ref, hbuf, lt)
        _s5_scan(hbuf, lam_ref, st_ref[...], lt)
        for s in range(S5_SUB):
            dys = dy_ref[:, 128 * s:128 * (s + 1)]
            _s5_put(gbuf, s, lax.dot_general(dys, wc_ref[s], NT_DIMS, preferred_element_type=F32), lt)
        a_re, a_im = lam_ref[RE], lam_ref[IM]

        def one(t, carry, hp_re, hp_im):
            gr, gi, dar, dai = carry
            r0 = pl.multiple_of(t * S5_SUB, S5_SUB)
            nr = gbuf[RE, pl.ds(r0, S5_SUB), :] + a_re * gr + a_im * gi
            ni = gbuf[IM, pl.ds(r0, S5_SUB), :] + a_re * gi - a_im * gr
            gbuf[RE, pl.ds(r0, S5_SUB), :] = nr
            gbuf[IM, pl.ds(r0, S5_SUB), :] = ni
            return nr, ni, dar + nr * hp_re + ni * hp_im, dai + ni * hp_re - nr * hp_im

        def step(k, carry):
            t = lt - 1 - k
            p0 = pl.multiple_of((t - 1) * S5_SUB, S5_SUB)
            return one(t, carry, hbuf[RE, pl.ds(p0, S5_SUB), :], hbuf[IM, pl.ds(p0, S5_SUB), :])

        init = (g_sc[RE], g_sc[IM], dlam_ref[RE], dlam_ref[IM])
        carry = lax.fori_loop(0, lt - 1, step, init, unroll=2)
        gr, gi, dar, dai = one(0, carry, st_ref[RE], st_ref[IM])
        g_sc[RE] = gr
        g_sc[IM] = gi
        dlam_ref[RE] = dar
        dlam_ref[IM] = dai
        for s in range(S5_SUB):
            cols = slice(128 * s, 128 * (s + 1))
            gs = _s5_get(gbuf, s, lt).astype(BF16)
            hs = _s5_get(hbuf, s, lt).astype(BF16)
            us, dys = u_ref[:, cols], dy_ref[:, cols]
            du = lax.dot_general(gs, wb_ref[s], NT_DIMS, preferred_element_type=F32) + d_ref[:, cols] * dys.astype(F32)
            du_ref[:, cols] = du.astype(du_ref.dtype)
            dwb_ref[s] += lax.dot_general(us, gs, TN_DIMS, preferred_element_type=F32)
            dwc_ref[s] += lax.dot_general(hs, dys, TN_DIMS, preferred_element_type=F32)

    blk = pl.BlockSpec((lt, D), lambda i: (nt - 1 - i, 0))
    full = lambda shp: pl.BlockSpec(shp, lambda i: (0,) * len(shp))
    state = (NCB, S5_SUB, 128)
    words = pltpu.VMEM((NCB, lt * S5_SUB, 128), F32)
    return _call(
        body, name=name, grid=(nt,), comm=comm, sem=("arbitrary",), args=(u, dyp, st, wb, wc, lam, dskip),
        in_specs=[blk, blk, pl.BlockSpec((None,) + state, lambda i: (nt - 1 - i, 0, 0, 0)),
                  full(wb.shape), full(wc.shape), full(lam.shape), full(dskip.shape)],
        out_specs=[blk, full(wb.shape), full(wc.shape), full(lam.shape)],
        out_shape=[jax.ShapeDtypeStruct((S, D), BF16), jax.ShapeDtypeStruct(wb.shape, F32), jax.ShapeDtypeStruct(wc.shape, F32),
                   jax.ShapeDtypeStruct(lam.shape, F32)],
        scratch_shapes=[words, words, pltpu.VMEM(state, F32)])


def _s5_disc(lam_re, lam_im, log_dt, b_re, b_im):
    dt = jnp.exp(log_dt)[:, None]
    mag = jnp.exp(lam_re * dt)
    lb_re, lb_im = mag * jnp.cos(lam_im * dt), mag * jnp.sin(lam_im * dt)
    num_re = lb_re - 1.0
    den = lam_re * lam_re + lam_im * lam_im
    k_re = (num_re * lam_re + lb_im * lam_im) / den
    k_im = (lb_im * lam_re - num_re * lam_im) / den
    bb_re = k_re[..., None] * b_re - k_im[..., None] * b_im
    bb_im = k_re[..., None] * b_im + k_im[..., None] * b_re
    return lb_re, lb_im, bb_re, bb_im


def _s5_pack(lb_re, lb_im, bb_re, bb_im, c_re, c_im):
    eye = jnp.eye(S5_SUB, dtype=F32)
    tb = lambda x: jnp.einsum("sgcp,gh->sgchp", x.reshape(S5_SUB, S5_SUB, S5_P, S5_C).transpose(0, 1, 3, 2), eye).reshape(S5_SUB, 128, HALF)
    tc = lambda x: jnp.einsum("sgpc,gh->sgphc", x.reshape(S5_SUB, S5_SUB, S5_C, S5_P).transpose(0, 1, 3, 2), eye).reshape(S5_SUB, HALF, 128)
    wb = jnp.concatenate([tb(bb_re), tb(bb_im)], axis=2).astype(BF16)
    wc = jnp.concatenate([tc(c_re), -tc(c_im)], axis=1).astype(BF16)
    lam = jnp.concatenate([lb_re.reshape(S5_SUB, HALF), lb_im.reshape(S5_SUB, HALF)], axis=1)
    return wb, wc, lam.reshape(S5_SUB, NCB, 128).transpose(1, 0, 2)


def _s5_unpack(dwb, dwc, dlam):
    db = jnp.einsum("sgcrgp->rsgpc", dwb.reshape(S5_SUB, S5_SUB, S5_C, 2, S5_SUB, S5_P)).reshape(2, S5_G, S5_P, S5_C)
    dc = jnp.einsum("srgpgc->rsgcp", dwc.reshape(S5_SUB, 2, S5_SUB, S5_P, S5_SUB, S5_C)).reshape(2, S5_G, S5_C, S5_P)
    dlam = dlam.transpose(1, 0, 2).reshape(S5_SUB, 2 * HALF)
    return (dlam[:, :HALF].reshape(S5_G, S5_P), dlam[:, HALF:].reshape(S5_G, S5_P), db[0], db[1], dc[0], -dc[1])


def _glu_fwd(yg, gl, *, name):
    S, D = yg.shape
    tm = _tile(S, 512, 8)
    blk = pl.BlockSpec((tm, D), lambda i: (i, 0))

    def body(y_ref, g_ref, o_ref):
        o_ref[...] = (y_ref[...].astype(F32) * jax.nn.sigmoid(g_ref[...].astype(F32))).astype(o_ref.dtype)

    return pl.pallas_call(body, name=name, grid=(S // tm,), in_specs=[blk, blk], out_specs=blk,
                          out_shape=jax.ShapeDtypeStruct((S, D), BF16), compiler_params=_cp("parallel"))(yg, gl)


def _glu_bwd(dy2, yg, gl, *, name):
    S, D = yg.shape
    tm = _tile(S, 512, 8)
    blk = pl.BlockSpec((tm, D), lambda i: (i, 0))

    def body(d_ref, y_ref, g_ref, da_ref, dg_ref):
        d = d_ref[...].astype(F32)
        sg = jax.nn.sigmoid(g_ref[...].astype(F32))
        da_ref[...] = d * sg
        dg_ref[...] = (d * y_ref[...].astype(F32) * sg * (1.0 - sg)).astype(dg_ref.dtype)

    return pl.pallas_call(body, name=name, grid=(S // tm,), in_specs=[blk, blk, blk], out_specs=[blk, blk],
                          out_shape=[jax.ShapeDtypeStruct((S, D), F32), jax.ShapeDtypeStruct((S, D), BF16)],
                          compiler_params=_cp("parallel"))(dy2, yg, gl)


def _gelu_bwd(da, db, ypre, u, *, name):
    S, D = ypre.shape
    tm = _tile(S, 512, 8)
    blk = pl.BlockSpec((tm, D), lambda i: (i, 0))
    vec = pl.BlockSpec((1, D), lambda i: (0, 0))

    def body(a_ref, b_ref, y_ref, u_ref, o_ref, s_ref):
        @pl.when(pl.program_id(0) == 0)
        def _():
            s_ref[...] = jnp.zeros_like(s_ref)

        dy = (a_ref[...] + b_ref[...]) * _gelu_grad(y_ref[...].astype(F32))
        o_ref[...] = dy.astype(o_ref.dtype)
        s_ref[...] += jnp.sum(dy * u_ref[...].astype(F32), axis=0, keepdims=True)

    return pl.pallas_call(body, name=name, grid=(S // tm,), in_specs=[blk, blk, blk, blk], out_specs=[blk, vec],
                          out_shape=[jax.ShapeDtypeStruct((S, D), BF16), jax.ShapeDtypeStruct((1, D), F32)],
                          compiler_params=_cp("arbitrary"))(da, db, ypre, u)


def _my_index():
    return 4 * lax.axis_index("x") + 2 * lax.axis_index("y") + lax.axis_index("c")


def _comm_plan(payloads):
    n = len(payloads)
    hbm = pl.BlockSpec(memory_space=pl.ANY)
    shapes = [jax.ShapeDtypeStruct((N_DEV,) + (x.shape if g else x.shape[1:]), x.dtype) for x, g in payloads]
    sems = [pltpu.SemaphoreType.DMA((n, N_DEV - 1)), pltpu.SemaphoreType.DMA((n, N_DEV - 1)), pltpu.SemaphoreType.DMA((n,))] if n else []
    return [hbm] * n, [hbm] * n, shapes, sems


def _comm_copies(x_refs, o_refs, gathers, send_sems, recv_sems, local_sems):
    mx, my, mc = lax.axis_index("x"), lax.axis_index("y"), lax.axis_index("c")
    me = 4 * mx + 2 * my + mc
    copies = []
    for i, (x_ref, o_ref, gather) in enumerate(zip(x_refs, o_refs, gathers)):
        src = (lambda j, r=x_ref: r) if gather else (lambda j, r=x_ref: r.at[j])
        copies.append(pltpu.make_async_copy(src(me), o_ref.at[me], local_sems.at[i]))
        for k in range(1, N_DEV):
            px, py, pc = mx ^ (k >> 2), my ^ ((k >> 1) & 1), mc ^ (k & 1)
            copies.append(pltpu.make_async_remote_copy(src(4 * px + 2 * py + pc), o_ref.at[me], send_sems.at[i, k - 1],
                                                       recv_sems.at[i, k - 1], device_id=(px, py, pc), device_id_type=MESH_ID))
    return copies


def _call(body, *, name, grid, in_specs, out_specs, out_shape, scratch_shapes, sem, args, comm=()):
    if not comm:
        outs = pl.pallas_call(body, name=name, grid=grid, in_specs=in_specs, out_specs=out_specs, out_shape=out_shape,
                              scratch_shapes=scratch_shapes, compiler_params=_cp(*sem))(*args)
        return outs, ()
    n, n_in, n_out = len(comm), len(in_specs), len(out_specs)
    c_in, c_out, c_shapes, c_sems = _comm_plan(comm)
    gathers = [g for _, g in comm]

    def wrapped(*refs):
        own_in, x_refs = refs[:n_in], refs[n_in:n_in + n]
        own_out, o_refs = refs[n_in + n:n_in + n + n_out], refs[n_in + n + n_out:n_in + 2 * n + n_out]
        own_scratch, sems = refs[n_in + 2 * n + n_out:-3], refs[-3:]
        ids = [pl.program_id(a) for a in range(len(grid))]
        first = functools.reduce(jnp.logical_and, [i == 0 for i in ids])
        last = functools.reduce(jnp.logical_and, [i == g - 1 for i, g in zip(ids, grid)])

        @pl.when(first)
        def _():
            for cp in _comm_copies(x_refs, o_refs, gathers, *sems):
                cp.start()

        body(*own_in, *own_out, *own_scratch)

        @pl.when(last)
        def _():
            for cp in _comm_copies(x_refs, o_refs, gathers, *sems):
                cp.wait()

    outs = pl.pallas_call(wrapped, name=name, grid=grid, in_specs=list(in_specs) + c_in, out_specs=list(out_specs) + c_out,
                          out_shape=list(out_shape) + c_shapes, scratch_shapes=list(scratch_shapes) + c_sems,
                          compiler_params=_cp(*["arbitrary"] * len(grid)))(*args, *[x for x, _ in comm])
    return outs[:n_out], outs[n_out:]


def _exchange_many(payloads, *, name):
    n = len(payloads)
    in_specs, out_specs, shapes, sems = _comm_plan(payloads)
    gathers = [g for _, g in payloads]

    def body(*refs):
        copies = _comm_copies(refs[:n], refs[n:2 * n], gathers, *refs[2 * n:])
        for cp in copies:
            cp.start()
        for cp in copies:
            cp.wait()

    return pl.pallas_call(body, name=name, in_specs=in_specs, out_specs=out_specs, out_shape=shapes, scratch_shapes=sems,
                          compiler_params=pltpu.CompilerParams(has_side_effects=True))(*[x for x, _ in payloads])


def _exchange(x, *, gather, name):
    return _exchange_many([(x, gather)], name=name)[0]


def _sum8(x8, *, name):
    _, R, C = x8.shape
    tr = _tile(R, 256, 16)

    def body(x_ref, o_ref):
        acc = x_ref[0].astype(F32)
        for j in range(1, N_DEV):
            acc = acc + x_ref[j].astype(F32)
        o_ref[...] = acc

    return pl.pallas_call(body, name=name, grid=(R // tr,), in_specs=[pl.BlockSpec((N_DEV, tr, C), lambda i: (0, i, 0))],
                          out_specs=pl.BlockSpec((tr, C), lambda i: (i, 0)), out_shape=jax.ShapeDtypeStruct((R, C), F32),
                          compiler_params=_cp("parallel"))(x8)


def _pack(arrs, dtype, row_mult):
    flat = jnp.concatenate([a.reshape(-1).astype(dtype) for a in arrs])
    n = flat.shape[0]
    rows = -(-n // 1024)
    rows = -(-rows // row_mult) * row_mult
    return jnp.pad(flat, (0, rows * 1024 - n)).reshape(rows, 1024)


def _unpack(buf, shapes):
    lead = buf.shape[:-2]
    flat = buf.reshape(lead + (-1,))
    out, off = [], 0
    for s in shapes:
        n = math.prod(s)
        out.append(flat[..., off:off + n].reshape(lead + tuple(s)))
        off += n
    return out


def _adaln_fwd(c_all, w, b, *, name):
    nk, D, n = w.shape

    def body(c_ref, w_ref, b_ref, o_ref):
        cv = c_ref[...]
        sc = (cv * jax.nn.sigmoid(cv)).astype(BF16)
        o_ref[...] = jnp.dot(sc, w_ref[...].astype(BF16), preferred_element_type=F32) + b_ref[...]

    return pl.pallas_call(body, name=name, grid=(nk,),
                          in_specs=[pl.BlockSpec((N_DEV, D), lambda k: (0, 0)), pl.BlockSpec((None, D, n), lambda k: (k, 0, 0)),
                                    pl.BlockSpec((None, 1, n), lambda k: (k, 0, 0))],
                          out_specs=pl.BlockSpec((None, N_DEV, n), lambda k: (k, 0, 0)),
                          out_shape=jax.ShapeDtypeStruct((nk, N_DEV, n), F32), compiler_params=_cp("parallel"))(c_all, w, b)


def _adaln_bwd(c_all, dmod, *, name):
    nk, _, n = dmod.shape
    D = c_all.shape[1]

    def body(c_ref, d_ref, dw_ref, db_ref):
        cv = c_ref[...]
        sc = (cv * jax.nn.sigmoid(cv)).astype(BF16).astype(F32)
        dm = d_ref[...]
        dw_ref[...] = lax.dot_general(sc, dm.astype(BF16).astype(F32), TN_DIMS, precision=lax.Precision.HIGHEST,
                                      preferred_element_type=F32)
        db_ref[...] = jnp.sum(dm, axis=0, keepdims=True)

    return pl.pallas_call(body, name=name, grid=(nk,),
                          in_specs=[pl.BlockSpec((N_DEV, D), lambda k: (0, 0)), pl.BlockSpec((None, N_DEV, n), lambda k: (k, 0, 0))],
                          out_specs=[pl.BlockSpec((None, D, n), lambda k: (k, 0, 0)), pl.BlockSpec((None, 1, n), lambda k: (k, 0, 0))],
                          out_shape=[jax.ShapeDtypeStruct((nk, D, n), F32), jax.ShapeDtypeStruct((nk, 1, n), F32)],
                          compiler_params=_cp("parallel"))(c_all, dmod)


def _adamw(w, g, m, v, *, name):
    R, C = w.shape
    tr = _tile(R, 256, 8)
    blk = pl.BlockSpec((tr, C), lambda i: (i, 0))
    c1 = 1.0 / (1.0 - ADAM_B1 ** ADAM_STEP)
    c2 = 1.0 / (1.0 - ADAM_B2 ** ADAM_STEP)

    def body(w_ref, g_ref, m_ref, v_ref, d_ref, nm_ref, nv_ref):
        gv = g_ref[...]
        nm = ADAM_B1 * m_ref[...] + (1.0 - ADAM_B1) * gv
        nv = ADAM_B2 * v_ref[...] + (1.0 - ADAM_B2) * (gv * gv)
        nm_ref[...] = nm
        nv_ref[...] = nv
        d_ref[...] = -ADAM_LR * ((nm * c1) / (jnp.sqrt(nv * c2) + ADAM_EPS) + ADAM_WD * w_ref[...])

    sh = jax.ShapeDtypeStruct((R, C), F32)
    return pl.pallas_call(body, name=name, grid=(R // tr,), in_specs=[blk] * 4, out_specs=[blk] * 3, out_shape=[sh] * 3,
                          compiler_params=_cp("parallel"))(w, g, m, v)


def _adamw_nd(w, g, m, v, *, name):
    shp = w.shape
    two = (-1, shp[-1]) if w.ndim > 1 else (1, -1)
    outs = _adamw(w.reshape(two), g.reshape(two), m.reshape(two), v.reshape(two), name=name)
    return [o.reshape(shp) for o in outs]


S5_TILE = 256
ATTN_BLOCK = 1024
BIG = ("s5_w_in", "s5_w_glu", "s5_w_out", "fox_w_in", "fox_w_out", "ffn_w_up", "ffn_w_down")
WEIGHTS = ("norm_g", "ada_w", "ada_b", "s5_w_in", "s5_lam_re", "s5_lam_im", "s5_log_dt", "s5_b_re", "s5_b_im", "s5_c_re",
           "s5_c_im", "s5_d", "s5_w_glu", "s5_w_out", "fox_w_in", "fox_b_f", "fox_w_out", "ffn_w_up", "ffn_conv_w",
           "ffn_conv_b", "ffn_w_down", "final_g")
REPLICATED = ("s5_lam_re", "s5_lam_im", "s5_log_dt", "s5_b_re", "s5_b_im", "s5_c_re", "s5_c_im", "s5_d", "fox_b_f",
              "ffn_conv_b", "final_g")


def _row_group(arrs):
    return jnp.concatenate([a.reshape(-1, D_MODEL) for a in arrs], axis=0)


def _cols_join(g):
    return jnp.concatenate([g[j] for j in range(N_DEV)], axis=1)


def _cols_split(full):
    n = full.shape[1] // N_DEV
    return jnp.stack([full[:, j * n:(j + 1) * n] for j in range(N_DEV)])


def kernel(x, c, norm_g, ada_w, ada_b, s5_w_in, s5_lam_re, s5_lam_im, s5_log_dt, s5_b_re, s5_b_im, s5_c_re, s5_c_im, s5_d, s5_w_glu, s5_w_out, fox_w_in, fox_b_f, fox_w_out, ffn_w_up, ffn_conv_w, ffn_conv_b, ffn_w_down, final_g, loss_target, m_norm_g, m_ada_w, m_ada_b, m_s5_w_in, m_s5_lam_re, m_s5_lam_im, m_s5_log_dt, m_s5_b_re, m_s5_b_im, m_s5_c_re, m_s5_c_im, m_s5_d, m_s5_w_glu, m_s5_w_out, m_fox_w_in, m_fox_b_f, m_fox_w_out, m_ffn_w_up, m_ffn_conv_w, m_ffn_conv_b, m_ffn_w_down, m_final_g, v_norm_g, v_ada_w, v_ada_b, v_s5_w_in, v_s5_lam_re, v_s5_lam_im, v_s5_log_dt, v_s5_b_re, v_s5_b_im, v_s5_c_re, v_s5_c_im, v_s5_d, v_s5_w_glu, v_s5_w_out, v_fox_w_in, v_fox_b_f, v_fox_w_out, v_ffn_w_up, v_ffn_conv_w, v_ffn_conv_b, v_ffn_w_down, v_final_g):
    args = dict(locals())
    W = {n: args[n] for n in WEIGHTS}
    M = {n: args["m_" + n] for n in WEIGHTS}
    V = {n: args["v_" + n] for n in WEIGHTS}
    D, F = D_MODEL, D_FF
    me = _my_index()
    h0 = x[0]
    S = h0.shape[0]
    lt = min(S5_TILE, S)
    tb = min(ATTN_BLOCK, S)

    b16 = lambda a: a.astype(BF16)
    g0, g_in = _exchange_many([(_pack([c, norm_g, ffn_conv_w], F32, 8), True), (b16(s5_w_in[0]), True)], name="gather_first")
    c_all, ng_all, cw_all = _unpack(g0, [(D,), norm_g.shape, ffn_conv_w.shape])
    ng_full = ng_all.transpose(1, 2, 0, 3).reshape(2, 2, D)
    cw_full = cw_all.transpose(1, 2, 0, 3).reshape(2, 3, F)

    ncol = ada_w.shape[-1]
    modp = _adaln_fwd(c_all, ada_w.reshape(4, D, ncol), ada_b.reshape(4, 1, ncol), name="adaln_fwd")
    g1 = _exchange(_pack([modp], F32, 8), gather=True, name="gather_adaln")
    (mod_all,) = _unpack(g1, [modp.shape])
    mod = lax.dynamic_index_in_dim(mod_all, me, axis=2, keepdims=False).transpose(1, 0, 2).reshape(4, 3 * D)
    shift = [mod[k:k + 1, :D] for k in range(4)]
    scale = [mod[k:k + 1, D:2 * D] for k in range(4)]
    gate = [mod[k:k + 1, 2 * D:] for k in range(4)]
    gain = [_row(ng_full[k // 2, k % 2]) for k in range(4)]

    rows = D // N_DEV
    full = {"s5_w_in": g_in.reshape(D, D)}
    bf_pad = jnp.pad(fox_b_f, ((0, 0), (0, 128 - HEADS)))

    def ffn_fwd(h, k, layer, comm=(), down_shard=None):
        hn = _modulate(h, gain[k], shift[k], scale[k], name=f"modulate{k}")
        up = _mm(hn, full[f"ffn_w_up{layer}"], name=f"ffn_up{layer}", o_h=True, out_dtype=BF16, tn=1408, comm=comm)
        up, exchanged = up if comm else (up, ())
        (z,), gathered = _conv_gate_fwd(up, cw_full[layer], ffn_conv_b[layer:layer + 1], name=f"conv_gate{layer}",
                                        comm=() if down_shard is None else [(down_shard, True)])
        if down_shard is not None:
            full[f"ffn_w_down{layer}"] = gathered[0].reshape(F, D)
        m, h_out = _mm(z, full[f"ffn_w_down{layer}"], name=f"ffn_down{layer}", res=h, gate=gate[k])
        return h_out, (hn, up, z, m), exchanged

    lb_re, lb_im, bb_re, bb_im = _s5_disc(s5_lam_re[0], s5_lam_im[0], s5_log_dt[0], s5_b_re[0], s5_b_im[0])
    wb, wc, lam = _s5_pack(lb_re, lb_im, bb_re, bb_im, s5_c_re[0], s5_c_im[0])
    hn0 = _modulate(h0, gain[0], shift[0], scale[0], name="modulate0")
    u = _mm(hn0, full["s5_w_in"], name="s5_in", out_dtype=BF16)
    (ypre, yg, st), (g_up0, g_rows0) = _s5_fwd(
        u, wb, wc, lam, s5_d, lt=lt, name="s5_scan",
        comm=[(b16(ffn_w_up[0]), True), (_row_group([b16(s5_w_glu), b16(s5_w_out)]), True)])
    full["ffn_w_up0"] = _cols_join(g_up0)
    full["s5_w_glu"] = g_rows0[:, :rows].reshape(D, D)
    full["s5_w_out"] = g_rows0[:, rows:].reshape(D, D)
    gl = _mm(yg, full["s5_w_glu"], name="s5_glu", out_dtype=BF16)
    y2 = _glu_fwd(yg, gl, name="s5_glu_gate")
    m0, h1 = _mm(y2, full["s5_w_out"], name="s5_out", res=h0, gate=gate[0])
    h2, (hn1, up0, z0, m1), (g_fox, g_fox_out) = ffn_fwd(h1, 1, 0, comm=[(b16(fox_w_in[0]), True), (b16(fox_w_out[0]), True)],
                                                         down_shard=b16(ffn_w_down[0]))
    full["fox_w_in"] = _cols_join(g_fox)
    full["fox_w_out"] = g_fox_out.reshape(D, D)
    w_proj = jnp.pad(full["fox_w_in"], ((0, 0), (0, 3 * D + 128 - full["fox_w_in"].shape[1])))
    w_qkv, w_f = w_proj[:, :3 * D], w_proj[:, 3 * D:]

    hn2 = _modulate(h2, gain[2], shift[2], scale[2], name="modulate2")
    nb = S // tb
    w_qkv_s = jnp.concatenate([w_qkv[:, :D] * Q_SCALE, w_qkv[:, D:]], axis=1)
    qkv, qkvT = _mm(hn2, w_qkv_s, name="fox_qkv", out_dtype=BF16, with_t=True)
    fl = _mm(hn2, w_f, name="fox_f", out_dtype=F32)
    fcol = _fgate_fwd(fl, bf_pad, name="fox_fgate")
    f_heads = fcol[:, :HEADS].T.reshape(HP, 2, S)
    ka = _fox_keys(qkv, fcol, name="fox_keys")
    vtb = _blocked(qkvT[2 * D:].reshape(HP, 128, S), nb, tb)
    aug = _aug_rows(f_heads)
    (oT, lse), (g_up1, g_down1) = _foxt_fwd(qkvT, ka, vtb, aug, tb=tb, name="fox_attn",
                                            comm=[(b16(ffn_w_up[1]), True), (b16(ffn_w_down[1]), True)])
    full["ffn_w_up1"] = _cols_join(g_up1)
    full["ffn_w_down1"] = g_down1.reshape(F, D)
    m2, h3 = _mm(oT, full["fox_w_out"], name="fox_out", ta=True, res=h2, gate=gate[2])
    h4, (hn3, up1, z1, m3), _ = ffn_fwd(h3, 3, 1)

    lblk, dh, head_sums, dm = _loss_head(h4, _row(final_g), loss_target[0], m3, gate[3], name="loss_head")
    d_final_g, dgate = head_sums[0:1], head_sums[1:2]

    G = {}
    dmod = [None] * 4
    mixer_out = [m0, m1, m2, m3]

    def norm_bwd(h, dhn, dh_in, k, dgate_k):
        if k == 0:
            dh_out, sums = _modulate_bwd(h, dhn, dh_in, gain[k], scale[k], name=f"modulate_bwd{k}")
            dm_below = dgate_below = None
        else:
            dh_out, sums, dm_below = _modulate_bwd(h, dhn, dh_in, gain[k], scale[k], mixer_out[k - 1], gate[k - 1],
                                                   name=f"modulate_bwd{k}")
            dgate_below = sums[3:4]
        dmod[k] = jnp.concatenate([sums[0:1], sums[1:2], dgate_k], axis=1)
        return dh_out, sums[2], dm_below, dgate_below

    def ffn_bwd(dh_in, dm, dgate_k, h, k, layer, saved):
        hn, up, z, m = saved
        dz = _mm(dm, full[f"ffn_w_down{layer}"], name=f"ffn_down_dx{layer}", tb=True, out_dtype=BF16, tn=1408)
        dw_down = _mm(z, dm, name=f"ffn_down_dw{layer}", ta=True, tm=1408, tk=2048)
        d_up, cs = _conv_gate_bwd(up, dz, cw_full[layer], ffn_conv_b[layer:layer + 1], name=f"conv_gate_bwd{layer}")
        dhn = _mm(d_up, full[f"ffn_w_up{layer}"], name=f"ffn_up_dx{layer}", tb=True, a_h=True, tk=2816)
        dw_up = _mm(hn, d_up, name=f"ffn_up_dw{layer}", ta=True, b_h=True, tm=1024, tn=1408, tk=2048)
        dh_out, dg, dm_below, dgate_below = norm_bwd(h, dhn, dh_in, k, dgate_k)
        return dh_out, dm_below, dgate_below, dg, dw_up, dw_down, cs[0:3], cs[3]

    dh, dm, dgate, dg3, dw_up1, dw_down1, dcw1, dcb1 = ffn_bwd(dh, dm, dgate, h3, 3, 1, (hn3, up1, z1, m3))

    do, doT = _mm(dm, full["fox_w_out"], name="fox_out_dx", tb=True, out_dtype=BF16, with_t=True)
    dw_fox_out = _mm(oT, dm, name="fox_out_dw", tk=2048)
    to_rows = lambda g: b16(g).reshape(N_DEV, -1, D)
    (dqtb, dk, dv, dfk, dfq), (r_up1, r_down1) = _foxt_bwd(
        qkvT, qkv, ka, _blocked(qkvT[:D].reshape(HP, 128, S), nb, tb), _blocked(doT.reshape(HP, 128, S), nb, tb), do,
        _blocked(aug, nb, tb), _blocked(lse, nb, tb), _blocked(oT.reshape(HP, 128, S), nb, tb), tb=tb, name="fox_attn_bwd",
        comm=[(_cols_split(b16(dw_up1)), False), (to_rows(dw_down1), False)])
    dq = dqtb.transpose(1, 3, 0, 2).reshape(S, D)
    dF = dfk[:, :, :2].transpose(1, 0, 2).reshape(S, HEADS) + dfq[:, :, :2, :].transpose(1, 3, 0, 2).reshape(S, HEADS)
    dF = jnp.pad(dF, ((0, 0), (0, 128 - HEADS)))
    dfl, dbf = _fgate_bwd(dF, fl, bf_pad, name="fox_fgate_bwd")
    dproj = jnp.concatenate([dq, dk, dv, dfl.astype(BF16)], axis=1)
    dhn = _mm(dproj, w_proj, name="fox_in_dx", tb=True)
    dw_proj = _mm(hn2, dproj, name="fox_in_dw", ta=True, tn=640, tk=2048)
    dw_fox_in = dw_proj[:, :full["fox_w_in"].shape[1]]
    dh, dg2, dm, dgate = norm_bwd(h2, dhn, dh, 2, dgate)

    dh, dm, dgate, dg1, dw_up0, dw_down0, dcw0, dcb0 = ffn_bwd(dh, dm, dgate, h1, 1, 0, (hn1, up0, z0, m1))

    dy2 = _mm(dm, full["s5_w_out"], name="s5_out_dx", tb=True, out_dtype=BF16)
    G["s5_w_out"] = _mm(y2, dm, name="s5_out_dw", ta=True, tk=2048)
    da, dgl = _glu_bwd(dy2, yg, gl, name="s5_glu_bwd")
    dyg_b = _mm(dgl, full["s5_w_glu"], name="s5_glu_dx", tb=True)
    G["s5_w_glu"] = _mm(yg, dgl, name="s5_glu_dw", ta=True, tk=2048)
    dyp, dd = _gelu_bwd(da, dyg_b, ypre, u, name="s5_gelu_bwd")
    row_grads = [dw_fox_out, dw_down0, G["s5_w_out"], G["s5_w_glu"]]
    (du, dwb, dwc, dlam), (r_up0, r_rows0, r_fox) = _s5_bwd(
        u, dyp, st, wb, wc, lam, s5_d, lt=lt, name="s5_scan_bwd",
        comm=[(_cols_split(b16(dw_up0)), False), (jnp.concatenate([to_rows(g) for g in row_grads], axis=1), False),
              (_cols_split(b16(dw_fox_in)), False)])
    dhn = _mm(du, full["s5_w_in"], name="s5_in_dx", tb=True)
    G["s5_w_in"] = _mm(hn0, du, name="s5_in_dw", ta=True, tk=2048)
    dh, dg0, _, _ = norm_bwd(h0, dhn, dh, 0, dgate)
    grad_x = dh[None]

    dlb_re, dlb_im, dbb_re, dbb_im, dc_re, dc_im = _s5_unpack(dwb, dwc, dlam)
    _, disc_vjp = jax.vjp(_s5_disc, s5_lam_re[0], s5_lam_im[0], s5_log_dt[0], s5_b_re[0], s5_b_im[0])
    dlam_re, dlam_im, dlog_dt, db_re, db_im = disc_vjp((dlb_re, dlb_im, dbb_re, dbb_im))

    small_full = {
        "norm_g": jnp.stack([dg0, dg1, dg2, dg3]).reshape(2, 2, D),
        "ffn_conv_w": jnp.stack([dcw0, dcw1]),
        "s5_lam_re": dlam_re[None], "s5_lam_im": dlam_im[None], "s5_log_dt": dlog_dt[None],
        "s5_b_re": db_re[None], "s5_b_im": db_im[None], "s5_c_re": dc_re[None], "s5_c_im": dc_im[None],
        "s5_d": dd, "fox_b_f": dbf[:, :HEADS], "ffn_conv_b": jnp.stack([dcb0, dcb1]), "final_g": d_final_g[0],
        "loss": lblk[0, 0:1],
    }
    names = tuple(small_full)
    small_pack = _pack([small_full[n] for n in names], F32, 8 * N_DEV)
    r_in, r_small, g2 = _exchange_many([(to_rows(G["s5_w_in"]), False), (small_pack.reshape(N_DEV, -1, 1024), False),
                                        (_pack([jnp.concatenate(dmod, axis=0)], F32, 8), True)], name="last_exchange")
    G["s5_w_in"] = _sum8(r_in, name="sum_s5_in_grads")[None]
    (dmod_all,) = _unpack(g2, [(4, 3 * D)])
    d_ada_w, d_ada_b = _adaln_bwd(c_all, _columns_of_mod(dmod_all, me, ncol), name="adaln_bwd")
    G["ada_w"] = d_ada_w.reshape(ada_w.shape)
    G["ada_b"] = d_ada_b.reshape(ada_b.shape)
    rows0_sum = _sum8(r_rows0, name="sum_rows0_grads")
    down_rows = F // N_DEV
    G["fox_w_out"] = rows0_sum[:rows][None]
    G["s5_w_out"] = rows0_sum[rows + down_rows:2 * rows + down_rows][None]
    G["s5_w_glu"] = rows0_sum[2 * rows + down_rows:][None]
    G["ffn_w_down"] = jnp.stack([rows0_sum[rows:rows + down_rows], _sum8(r_down1, name="sum_down1_grads")])
    G["ffn_w_up"] = jnp.stack([_sum8(r_up0, name="sum_up0_grads"), _sum8(r_up1, name="sum_up1_grads")])
    G["fox_w_in"] = _sum8(r_fox, name="sum_fox_w_in")[None]

    g3 = _exchange(_sum8(r_small, name="sum_small_grads"), gather=True, name="gather_small_sums")
    summed = dict(zip(names, _unpack(g3.reshape(small_pack.shape), [small_full[n].shape for n in names])))
    G["norm_g"] = lax.dynamic_slice_in_dim(summed["norm_g"], me * norm_g.shape[-1], norm_g.shape[-1], axis=2)
    G["ffn_conv_w"] = lax.dynamic_slice_in_dim(summed["ffn_conv_w"], me * ffn_conv_w.shape[-1], ffn_conv_w.shape[-1], axis=2)
    for n in REPLICATED:
        G[n] = summed[n]
    loss = summed["loss"][0]

    delta, new_m, new_v = {}, {}, {}
    small = tuple(n for n in WEIGHTS if n not in BIG and n != "ada_w")
    for n in WEIGHTS:
        if n not in small:
            delta[n], new_m[n], new_v[n] = _adamw_nd(W[n], G[n], M[n], V[n], name=f"adamw_{n}")
    packed = [_pack([src[n] for n in small], F32, 8) for src in (W, G, M, V)]
    for dst, buf in zip((delta, new_m, new_v), _adamw(*packed, name="adamw_small")):
        dst.update(zip(small, _unpack(buf, [W[n].shape for n in small])))

    return (loss, grad_x, *[G[n] for n in WEIGHTS], *[delta[n] for n in WEIGHTS], *[new_m[n] for n in WEIGHTS],
            *[new_v[n] for n in WEIGHTS])


def _columns_of_mod(dmod_all, me, ncol):
    flat = lax.dynamic_slice_in_dim(dmod_all, me * ncol, ncol, axis=2)
    return flat.transpose(1, 0, 2)
```

```python
import functools
import math

import jax
import jax.numpy as jnp
from jax import lax
from jax.experimental import pallas as pl
from jax.experimental.pallas import tpu as pltpu

F32, BF16 = jnp.float32, jnp.bfloat16
EPS = 1e-6
N_DEV = 8
D_MODEL = 1024
D_FF = 2816
HEADS = 16
HEAD_DIM = 64
S5_G, S5_P, S5_C = 64, 64, 16
S5_SUB = 8
V7X_VMEM_LIMIT = 56 * 1024 * 1024
NEG = -1e30
ADAM_LR, ADAM_B1, ADAM_B2, ADAM_EPS, ADAM_WD, ADAM_STEP = 1e-3, 0.9, 0.999, 1e-8, 0.01, 10
GELU_K = math.sqrt(2.0 / math.pi)
MESH_ID = pl.DeviceIdType.MESH


def _cp(*sem):
    return pltpu.CompilerParams(dimension_semantics=sem, vmem_limit_bytes=V7X_VMEM_LIMIT)


def _tile(n, target, mult=128):
    if n <= target:
        return n
    t = (target // mult) * mult
    while t >= mult:
        if n % t == 0:
            return t
        t -= mult
    return n


def _row(v):
    return v.reshape(1, -1).astype(F32)


def _mm(a, b, *, name, ta=False, tb=False, out_dtype=F32, tm=1024, tn=1024, tk=None, res=None, gate=None,
        a_h=False, b_h=False, o_h=False, with_t=False, comm=()):
    if a_h:
        M, K = a.shape[1], 2 * a.shape[2]
    elif ta:
        K, M = a.shape
    else:
        M, K = a.shape
    if b_h:
        N = 2 * b.shape[2]
    else:
        N = b.shape[0] if tb else b.shape[1]
    half_n = N // 2
    tm = _tile(M, tm, 128 if ta else 8)
    tn = _tile(half_n if (b_h or o_h) else N, tn)
    tk = K if tk is None else _tile(K // 2 if a_h else K, tk)
    nk = K // tk
    nkh, nnh = (K // 2) // tk if a_h else 1, half_n // tn
    if a_h:
        a_spec = pl.BlockSpec((None, tm, tk), lambda i, j, k: (k // nkh, i, k % nkh))
    elif ta:
        a_spec = pl.BlockSpec((tk, tm), lambda i, j, k: (k, i))
    else:
        a_spec = pl.BlockSpec((tm, tk), lambda i, j, k: (i, k))
    if b_h:
        b_spec = pl.BlockSpec((None, tk, tn), lambda i, j, k: (j // nnh, k, j % nnh))
    elif tb:
        b_spec = pl.BlockSpec((tn, tk), lambda i, j, k: (j, k))
    else:
        b_spec = pl.BlockSpec((tk, tn), lambda i, j, k: (k, j))
    if o_h:
        o_spec = pl.BlockSpec((None, tm, tn), lambda i, j, k: (j // nnh, i, j % nnh))
    else:
        o_spec = pl.BlockSpec((tm, tn), lambda i, j, k: (i, j))
    dn = (((0 if ta else 1,), (1 if tb else 0,)), ((), ()))
    fused = res is not None

    def body(*refs):
        if fused:
            a_ref, b_ref, r_ref, g_ref, m_ref, o_ref, acc_ref = refs
        elif with_t:
            a_ref, b_ref, o_ref, t_ref, acc_ref = refs
        else:
            a_ref, b_ref, o_ref, acc_ref = refs
        p = lax.dot_general(a_ref[...].astype(BF16), b_ref[...].astype(BF16), dn, preferred_element_type=F32)

        def finish(acc):
            if fused:
                m_ref[...] = acc.astype(m_ref.dtype)
                o_ref[...] = r_ref[...] + g_ref[...] * acc
            else:
                o_ref[...] = acc.astype(o_ref.dtype)
                if with_t:
                    t_ref[...] = acc.T.astype(t_ref.dtype)

        if nk == 1:
            finish(p)
        else:
            k = pl.program_id(2)

            @pl.when(k == 0)
            def _():
                acc_ref[...] = p

            @pl.when(k > 0)
            def _():
                acc_ref[...] += p

            @pl.when(k == nk - 1)
            def _():
                finish(acc_ref[...])

    in_specs = [a_spec, b_spec]
    args = [a, b]
    if fused:
        in_specs += [o_spec, pl.BlockSpec((1, tn), lambda i, j, k: (0, j))]
        args += [res, gate]
        out_shape = [jax.ShapeDtypeStruct((M, N), BF16), jax.ShapeDtypeStruct((M, N), F32)]
        out_specs = [o_spec, o_spec]
    else:
        out_shape = [jax.ShapeDtypeStruct((2, M, half_n) if o_h else (M, N), out_dtype)]
        out_specs = [o_spec]
        if with_t:
            out_shape.append(jax.ShapeDtypeStruct((N, M), out_dtype))
            out_specs.append(pl.BlockSpec((tn, tm), lambda i, j, k: (j, i)))
    outs, exchanged = _call(
        body, name=name, grid=(M // tm, N // tn, nk), in_specs=in_specs, out_specs=out_specs, out_shape=out_shape,
        scratch_shapes=[pltpu.VMEM((tm, tn) if nk > 1 else (8, 128), F32)], sem=("parallel", "parallel", "arbitrary"),
        args=args, comm=comm)
    outs = tuple(outs) if (fused or with_t) else outs[0]
    return (outs, exchanged) if comm else outs


def _modulate(h, g, shift, scale, *, name):
    S, D = h.shape
    tm = _tile(S, 512, 8)
    vec = pl.BlockSpec((1, D), lambda i: (0, 0))
    blk = pl.BlockSpec((tm, D), lambda i: (i, 0))

    def body(h_ref, g_ref, sh_ref, sc_ref, o_ref):
        x = h_ref[...]
        r = lax.rsqrt(jnp.mean(x * x, axis=-1, keepdims=True) + EPS)
        o_ref[...] = ((x * r * g_ref[...]) * (1.0 + sc_ref[...]) + sh_ref[...]).astype(o_ref.dtype)

    return pl.pallas_call(body, name=name, grid=(S // tm,), in_specs=[blk, vec, vec, vec], out_specs=blk,
                          out_shape=jax.ShapeDtypeStruct((S, D), BF16), compiler_params=_cp("parallel"))(h, g, shift, scale)


def _modulate_bwd(h, dhn, dh_in, g, scale, m_prev=None, gate_prev=None, *, name):
    S, D = h.shape
    tm = _tile(S, 512, 8)
    vec = pl.BlockSpec((1, D), lambda i: (0, 0))
    blk = pl.BlockSpec((tm, D), lambda i: (i, 0))
    sums = pl.BlockSpec((8, D), lambda i: (0, 0))
    below = m_prev is not None

    def body(h_ref, dhn_ref, dhi_ref, g_ref, sc_ref, *rest):
        dh_ref, s_ref = rest[-3:-1] if below else rest[-2:]

        @pl.when(pl.program_id(0) == 0)
        def _():
            s_ref[...] = jnp.zeros_like(s_ref)

        x = h_ref[...]
        r = lax.rsqrt(jnp.mean(x * x, axis=-1, keepdims=True) + EPS)
        xhat = x * r
        gv = g_ref[...]
        dhn_v = dhn_ref[...].astype(F32)
        dn = dhn_v * (1.0 + sc_ref[...])
        s_ref[0:1, :] += jnp.sum(dhn_v, axis=0, keepdims=True)
        s_ref[1:2, :] += jnp.sum(dhn_v * (xhat * gv), axis=0, keepdims=True)
        s_ref[2:3, :] += jnp.sum(dn * xhat, axis=0, keepdims=True)
        dxh = dn * gv
        dh = dhi_ref[...] + r * (dxh - xhat * jnp.mean(dxh * xhat, axis=-1, keepdims=True))
        dh_ref[...] = dh
        if below:
            mp_ref, gp_ref, dm_ref = rest[0], rest[1], rest[-1]
            dm_ref[...] = (dh * gp_ref[...]).astype(dm_ref.dtype)
            s_ref[3:4, :] += jnp.sum(dh * mp_ref[...].astype(F32), axis=0, keepdims=True)

    extra_in, extra_args = ([blk, vec], [m_prev, gate_prev]) if below else ([], [])
    return pl.pallas_call(body, name=name, grid=(S // tm,), in_specs=[blk, blk, blk, vec, vec] + extra_in,
                          out_specs=[blk, sums] + ([blk] if below else []),
                          out_shape=[jax.ShapeDtypeStruct((S, D), F32), jax.ShapeDtypeStruct((8, D), F32)]
                          + ([jax.ShapeDtypeStruct((S, D), BF16)] if below else []),
                          compiler_params=_cp("arbitrary"))(h, dhn, dh_in, g, scale, *extra_args)


def _loss_head(h, g, tgt, m_prev, gate_prev, *, name):
    S, D = h.shape
    tm = _tile(S, 512, 8)
    vec = pl.BlockSpec((1, D), lambda i: (0, 0))
    blk = pl.BlockSpec((tm, D), lambda i: (i, 0))
    lss = pl.BlockSpec((8, 128), lambda i: (0, 0))
    sums = pl.BlockSpec((8, D), lambda i: (0, 0))

    def body(h_ref, g_ref, t_ref, mp_ref, gp_ref, l_ref, dh_ref, s_ref, dm_ref):
        @pl.when(pl.program_id(0) == 0)
        def _():
            l_ref[...] = jnp.zeros_like(l_ref)
            s_ref[...] = jnp.zeros_like(s_ref)

        x = h_ref[...]
        r = lax.rsqrt(jnp.mean(x * x, axis=-1, keepdims=True) + EPS)
        xhat = x * r
        gv = g_ref[...]
        e = xhat * gv - t_ref[...]
        l_ref[...] += 0.5 * jnp.sum(jnp.mean(e * e, axis=-1, keepdims=True))
        dy = e * (1.0 / D)
        s_ref[0:1, :] += jnp.sum(dy * xhat, axis=0, keepdims=True)
        dxh = dy * gv
        dh = r * (dxh - xhat * jnp.mean(dxh * xhat, axis=-1, keepdims=True))
        dh_ref[...] = dh
        dm_ref[...] = (dh * gp_ref[...]).astype(dm_ref.dtype)
        s_ref[1:2, :] += jnp.sum(dh * mp_ref[...].astype(F32), axis=0, keepdims=True)

    return pl.pallas_call(body, name=name, grid=(S // tm,), in_specs=[blk, vec, blk, blk, vec], out_specs=[lss, blk, sums, blk],
                          out_shape=[jax.ShapeDtypeStruct((8, 128), F32), jax.ShapeDtypeStruct((S, D), F32),
                                     jax.ShapeDtypeStruct((8, D), F32), jax.ShapeDtypeStruct((S, D), BF16)],
                          compiler_params=_cp("arbitrary"))(h, g, tgt, m_prev, gate_prev)


def _shift_down(x, k, edge):
    tm = x.shape[0]
    rows = lax.broadcasted_iota(jnp.int32, x.shape, 0)
    out = pltpu.roll(x, k, 0)
    for j in range(k):
        out = jnp.where(rows == j, edge[8 - k + j:8 - k + j + 1, :], out)
    return out


def _shift_up(x, k, edge):
    tm = x.shape[0]
    rows = lax.broadcasted_iota(jnp.int32, x.shape, 0)
    out = pltpu.roll(x, tm - k, 0)
    for j in range(k):
        out = jnp.where(rows == tm - k + j, edge[j:j + 1, :], out)
    return out


def _conv_gate_fwd(up, cw, cb, *, name, comm=()):
    _, S, F = up.shape
    tf = _tile(F, 1408)
    nf = F // tf
    tm = _tile(S, 512, 8)

    def body(a_ref, b_ref, w_ref, cb_ref, z_ref, edge_ref):
        @pl.when(pl.program_id(1) == 0)
        def _():
            edge_ref[...] = jnp.zeros_like(edge_ref)

        a = a_ref[...].astype(F32)
        edge = edge_ref[...]
        w = w_ref[...]
        ac = cb_ref[...] + w[2:3, :] * a + w[1:2, :] * _shift_down(a, 1, edge) + w[0:1, :] * _shift_down(a, 2, edge)
        edge_ref[...] = a[tm - 8:tm, :]
        z_ref[...] = (ac * jax.nn.sigmoid(ac) * b_ref[...].astype(F32)).astype(z_ref.dtype)

    return _call(
        body, name=name, grid=(nf, S // tm), comm=comm, sem=("parallel", "arbitrary"), args=(up, up, cw, cb),
        in_specs=[pl.BlockSpec((None, tm, tf), lambda j, i: (0, i, j)), pl.BlockSpec((None, tm, tf), lambda j, i: (1, i, j)),
                  pl.BlockSpec((3, tf), lambda j, i: (0, j)), pl.BlockSpec((1, tf), lambda j, i: (0, j))],
        out_specs=[pl.BlockSpec((tm, tf), lambda j, i: (i, j))],
        out_shape=[jax.ShapeDtypeStruct((S, F), BF16)], scratch_shapes=[pltpu.VMEM((8, tf), F32)])


def _conv_gate_bwd(up, dz, cw, cb, *, name):
    _, S, F = up.shape
    tf = _tile(F, 1408)
    nf = F // tf
    tm = _tile(S, 512, 8)
    nt = S // tm
    hb = tm // 8

    def body(a_ref, ah_ref, b_ref, dz_ref, w_ref, cb_ref, d_ref, s_ref, edge_ref):
        i = pl.program_id(1)

        @pl.when(i == 0)
        def _():
            edge_ref[...] = jnp.zeros_like(edge_ref)
            s_ref[...] = jnp.zeros_like(s_ref)

        a = a_ref[...].astype(F32)
        halo = jnp.where(i == nt - 1, 0.0, ah_ref[...].astype(F32))
        w = w_ref[...]
        a1 = _shift_down(a, 1, halo)
        a2 = _shift_down(a, 2, halo)
        ac = cb_ref[...] + w[2:3, :] * a + w[1:2, :] * a1 + w[0:1, :] * a2
        sg = jax.nn.sigmoid(ac)
        dzv = dz_ref[...].astype(F32)
        si = ac * sg
        d_ref[1] = (dzv * si).astype(d_ref.dtype)
        dac = (dzv * b_ref[...].astype(F32)) * (sg + si * (1.0 - sg))
        s_ref[0:1, :] += jnp.sum(dac * a2, axis=0, keepdims=True)
        s_ref[1:2, :] += jnp.sum(dac * a1, axis=0, keepdims=True)
        s_ref[2:3, :] += jnp.sum(dac * a, axis=0, keepdims=True)
        s_ref[3:4, :] += jnp.sum(dac, axis=0, keepdims=True)
        edge = edge_ref[...]
        da = w[2:3, :] * dac + w[1:2, :] * _shift_up(dac, 1, edge) + w[0:1, :] * _shift_up(dac, 2, edge)
        edge_ref[...] = dac[0:8, :]
        d_ref[0] = da.astype(d_ref.dtype)

    tile = lambda hlf: pl.BlockSpec((None, tm, tf), lambda j, i: (hlf, nt - 1 - i, j))
    d_up, sums = pl.pallas_call(
        body, name=name, grid=(nf, nt),
        in_specs=[tile(0),
                  pl.BlockSpec((None, 8, tf), lambda j, i: (0, jnp.maximum((nt - 1 - i) * hb - 1, 0), j)),
                  tile(1), pl.BlockSpec((tm, tf), lambda j, i: (nt - 1 - i, j)),
                  pl.BlockSpec((3, tf), lambda j, i: (0, j)), pl.BlockSpec((1, tf), lambda j, i: (0, j))],
        out_specs=[pl.BlockSpec((2, tm, tf), lambda j, i: (0, nt - 1 - i, j)), pl.BlockSpec((8, tf), lambda j, i: (0, j))],
        out_shape=[jax.ShapeDtypeStruct((2, S, F), BF16), jax.ShapeDtypeStruct((8, F), F32)],
        scratch_shapes=[pltpu.VMEM((8, tf), F32)],
        compiler_params=_cp("parallel", "arbitrary"))(up, up, up, dz, cw, cb)
    return d_up, sums


def _log_sigmoid(x):
    return jnp.minimum(x, 0.0) - jnp.log(1.0 + jnp.exp(-jnp.abs(x)))


def _tri_ones(n, upper):
    r = lax.broadcasted_iota(jnp.int32, (n, n), 0)
    c = lax.broadcasted_iota(jnp.int32, (n, n), 1)
    return jnp.where((c >= r) if upper else (c <= r), 1.0, 0.0).astype(F32)


def _fgate_fwd(fl, bf, *, name):
    S, W = fl.shape
    tb = _tile(S, 256, 8)

    def body(fl_ref, b_ref, o_ref, carry_ref):
        @pl.when(pl.program_id(0) == 0)
        def _():
            carry_ref[...] = jnp.zeros_like(carry_ref)

        lf = _log_sigmoid(fl_ref[...] + b_ref[...])
        cs = jnp.dot(_tri_ones(tb, False), lf, precision=lax.Precision.HIGHEST, preferred_element_type=F32) + carry_ref[0:1, :]
        o_ref[...] = cs
        carry_ref[...] = jnp.broadcast_to(cs[tb - 1:tb, :], carry_ref.shape)

    blk = pl.BlockSpec((tb, W), lambda i: (i, 0))
    return pl.pallas_call(body, name=name, grid=(S // tb,), in_specs=[blk, pl.BlockSpec((1, W), lambda i: (0, 0))], out_specs=blk,
                          out_shape=jax.ShapeDtypeStruct((S, W), F32), scratch_shapes=[pltpu.VMEM((8, W), F32)],
                          compiler_params=_cp("arbitrary"))(fl, bf)


def _fgate_bwd(dF, fl, bf, *, name):
    S, W = fl.shape
    tb = _tile(S, 256, 8)
    nb = S // tb

    def body(d_ref, fl_ref, b_ref, o_ref, s_ref, carry_ref):
        @pl.when(pl.program_id(0) == 0)
        def _():
            carry_ref[...] = jnp.zeros_like(carry_ref)
            s_ref[...] = jnp.zeros_like(s_ref)

        rc = jnp.dot(_tri_ones(tb, True), d_ref[...], precision=lax.Precision.HIGHEST, preferred_element_type=F32) + carry_ref[0:1, :]
        carry_ref[...] = jnp.broadcast_to(rc[0:1, :], carry_ref.shape)
        dfl = rc * jax.nn.sigmoid(-(fl_ref[...] + b_ref[...]))
        o_ref[...] = dfl
        s_ref[...] += jnp.sum(dfl, axis=0, keepdims=True)

    blk = pl.BlockSpec((tb, W), lambda i: (nb - 1 - i, 0))
    vec = pl.BlockSpec((1, W), lambda i: (0, 0))
    return pl.pallas_call(body, name=name, grid=(nb,), in_specs=[blk, blk, vec], out_specs=[blk, vec],
                          out_shape=[jax.ShapeDtypeStruct((S, W), F32), jax.ShapeDtypeStruct((1, W), F32)],
                          scratch_shapes=[pltpu.VMEM((8, W), F32)], compiler_params=_cp("arbitrary"))(dF, fl, bf)


NT_DIMS = (((1,), (1,)), ((), ()))
TN_DIMS = (((0,), (0,)), ((), ()))
HP = HEADS // 2
Q_SCALE = HEAD_DIM ** -0.5


def _head_mask(x, hh):
    lanes = lax.broadcasted_iota(jnp.int32, x.shape, 1)
    return jnp.where((lanes >= hh * HEAD_DIM) & (lanes < (hh + 1) * HEAD_DIM), x, jnp.zeros_like(x))


def _lanes_from(cols, shape):
    lanes = lax.broadcasted_iota(jnp.int32, shape, 1)
    out = jnp.zeros(shape, F32)
    for i, cvec in enumerate(cols):
        out = jnp.where(lanes == i, cvec, out)
    return out


AUG_ROWS = 16
DIAGONAL_TILES = ((0, 0, True), (0, 1, False), (1, 1, True))


def _qa_pair(qt, aug):
    fill = jnp.zeros((HEAD_DIM - AUG_ROWS, qt.shape[1]), qt.dtype)
    return [jnp.concatenate([qt[0:HEAD_DIM], aug[0], fill], axis=0), jnp.concatenate([aug[1], fill, qt[HEAD_DIM:]], axis=0)]


def _rows_of_head(xt, hh):
    rows = lax.broadcasted_iota(jnp.int32, xt.shape, 0)
    return jnp.where((rows >= hh * HEAD_DIM) & (rows < (hh + 1) * HEAD_DIM), xt, jnp.zeros_like(xt))


def _put_rows(ref, rows):
    for i, r in enumerate(rows):
        ref[i:i + 1, :] = r
    ref[len(rows):, :] = jnp.zeros((ref.shape[0] - len(rows), ref.shape[1]), ref.dtype)


def _diag_mask_t(s):
    rows = lax.broadcasted_iota(jnp.int32, s.shape, 0)
    cols = lax.broadcasted_iota(jnp.int32, s.shape, 1)
    return jnp.where(rows <= cols, s, NEG)


def _foxt_fwd(qkvT, ka, vtb, augq, *, tb, name, comm=()):
    S = qkvT.shape[1]
    nb = S // tb

    def body(q_ref, ka_ref, v_ref, aq_ref, o_ref, lse_ref, m0, m1, l0, l1, acc0, acc1):
        qi = pl.program_id(1)
        qa = _qa_pair(q_ref[...], aq_ref[...])
        half = tb // 2
        state = ((m0, l0, acc0), (m1, l1, acc1))
        for m_sc, l_sc, acc_sc in state:
            m_sc[...] = jnp.full_like(m_sc, NEG)
            l_sc[...] = jnp.zeros_like(l_sc)
            acc_sc[...] = jnp.zeros_like(acc_sc)

        def tile(kj, ko, qo, n, masked):
            k0 = pl.multiple_of(kj * tb + ko, n)
            cols = slice(qo, qo + n)
            logits = [jnp.dot(ka_ref[hh, pl.ds(k0, n), :], qa[hh][:, cols], preferred_element_type=F32) for hh in range(2)]
            updates = []
            for hh, (m_sc, l_sc, acc_sc) in enumerate(state):
                s = _diag_mask_t(logits[hh]) if masked else logits[hh]
                m_old = m_sc[:, cols]
                m_new = jnp.maximum(m_old, jnp.max(s, axis=0, keepdims=True))
                alpha = jnp.exp(m_old - m_new)
                p = jnp.exp(s - m_new)
                l_sc[:, cols] = alpha * l_sc[:, cols] + jnp.sum(p, axis=0, keepdims=True)
                m_sc[:, cols] = m_new
                vt = v_ref[kj, HEAD_DIM * hh:HEAD_DIM * (hh + 1), ko:ko + n]
                updates.append((alpha, jnp.dot(vt, p.astype(BF16), preferred_element_type=F32)))
            for (alpha, pv), (_, _, acc_sc) in zip(updates, state):
                acc_sc[:, cols] = alpha * acc_sc[:, cols] + pv

        def off_diagonal(kj, c):
            tile(kj, 0, 0, tb, False)
            return c

        lax.fori_loop(0, qi, off_diagonal, 0)
        for ko, qo, masked in DIAGONAL_TILES:
            tile(qi, ko * half, qo * half, half, masked)
        for hh, (m_sc, l_sc, acc_sc) in enumerate(state):
            o_ref[HEAD_DIM * hh:HEAD_DIM * (hh + 1), :] = (acc_sc[...] / l_sc[...]).astype(o_ref.dtype)
        _put_rows(lse_ref, [m_sc[...] + jnp.log(l_sc[...]) for m_sc, l_sc, _ in state])

    return _call(
        body, name=name, grid=(HP, nb), comm=comm, sem=("parallel", "parallel"), args=(qkvT, ka, vtb, augq),
        in_specs=[pl.BlockSpec((128, tb), lambda hp, qi: (hp, qi)),
                  pl.BlockSpec((None, 2, S, 128), lambda hp, qi: (hp, 0, 0, 0)),
                  pl.BlockSpec((None, nb, 128, tb), lambda hp, qi: (hp, 0, 0, 0)),
                  pl.BlockSpec((None, 2, AUG_ROWS, tb), lambda hp, qi: (hp, 0, 0, qi))],
        out_specs=[pl.BlockSpec((128, tb), lambda hp, qi: (hp, qi)), pl.BlockSpec((None, 8, tb), lambda hp, qi: (hp, 0, qi))],
        out_shape=[jax.ShapeDtypeStruct((D_MODEL, S), BF16), jax.ShapeDtypeStruct((HP, 8, S), F32)],
        scratch_shapes=[pltpu.VMEM((1, tb), F32)] * 4 + [pltpu.VMEM((HEAD_DIM, tb), F32)] * 2)


def _foxt_bwd(qkvT, qkv, ka, qtb, dotb, do, augqb, lseb, otb, *, tb, name, comm=()):
    S = qkv.shape[0]
    nb = S // tb

    def body(ka_ref, v_ref, kt_ref, qt_ref, dot_ref, qn_ref, dn_ref, aq_ref, ls_ref, ot_ref,
             dq_ref, dk_ref, dv_ref, df_ref, dr_ref, dq_sc, dk_sc, dv_sc, ds_sc, dr_sc):
        kj = pl.program_id(1)

        @pl.when(kj == 0)
        def _():
            dq_sc[...] = jnp.zeros_like(dq_sc)
            dr_sc[...] = jnp.zeros_like(dr_sc)

        dk_sc[...] = jnp.zeros_like(dk_sc)
        dv_sc[...] = jnp.zeros_like(dv_sc)
        ds_sc[...] = jnp.zeros_like(ds_sc)
        v2 = v_ref[...]
        kth = [kt_ref[HEAD_DIM * hh:HEAD_DIM * (hh + 1), :] for hh in range(2)]

        def block(qi, tiles):
            q0 = pl.multiple_of(qi * tb, tb)
            qa = _qa_pair(qt_ref[qi], aq_ref[qi])
            dot, ls = dot_ref[qi], ls_ref[qi]
            prod = ot_ref[qi].astype(F32) * dot.astype(F32)
            delta = [jnp.sum(prod[HEAD_DIM * hh:HEAD_DIM * (hh + 1)], axis=0, keepdims=True) for hh in range(2)]
            qn, dn = qn_ref[pl.ds(q0, tb), :], dn_ref[pl.ds(q0, tb), :]
            for ko, qo, n, masked in tiles:
                keys, cols = slice(ko, ko + n), slice(qo, qo + n)
                for hh in range(2):
                    s = jnp.dot(ka_ref[hh, keys, :], qa[hh][:, cols], preferred_element_type=F32)
                    if masked:
                        s = _diag_mask_t(s)
                    p = jnp.exp(s - ls[hh:hh + 1, cols])
                    dp = jnp.dot(v2[keys], _rows_of_head(dot[:, cols], hh), preferred_element_type=F32)
                    ds = p * (dp - delta[hh][:, cols])
                    dsb = ds.astype(BF16)
                    dv_sc[keys, :] += jnp.dot(p.astype(BF16), _head_mask(dn[cols], hh), preferred_element_type=F32)
                    dk_sc[keys, :] += jnp.dot(dsb, _head_mask(qn[cols], hh), preferred_element_type=F32)
                    dq_sc[qi, HEAD_DIM * hh:HEAD_DIM * (hh + 1), cols] += jnp.dot(kth[hh][:, keys], dsb, preferred_element_type=F32)
                    part = ds[:, 0:128]
                    for j in range(1, n // 128):
                        part = part + ds[:, 128 * j:128 * (j + 1)]
                    ds_sc[hh, keys, :] += part
                    dr_sc[qi, hh:hh + 1, cols] += jnp.sum(ds, axis=0, keepdims=True)

        def off_diagonal(i, c):
            block(kj + 1 + i, [(0, 0, tb, False)])
            return c

        half = tb // 2
        block(kj, [(ko * half, qo * half, half, masked) for ko, qo, masked in DIAGONAL_TILES])
        lax.fori_loop(0, nb - 1 - kj, off_diagonal, 0)
        dk_ref[...] = dk_sc[...].astype(dk_ref.dtype)
        dv_ref[...] = dv_sc[...].astype(dv_ref.dtype)
        df_ref[...] = _lanes_from([-jnp.sum(ds_sc[hh], axis=1, keepdims=True) for hh in range(2)], (tb, 128))

        @pl.when(kj == nb - 1)
        def _():
            dq_ref[...] = (dq_sc[...] * Q_SCALE).astype(dq_ref.dtype)
            dr_ref[...] = dr_sc[...]

    resident = lambda rows: pl.BlockSpec((None, nb) + rows, lambda hp, kj: (hp,) + (0,) * (len(rows) + 1))
    whole = pl.BlockSpec((S, 128), lambda hp, kj: (0, hp))
    kblk = lambda off: pl.BlockSpec((tb, 128), lambda hp, kj: (kj, off + hp))
    return _call(
        body, name=name, grid=(HP, nb), comm=comm, sem=("parallel", "arbitrary"),
        args=(ka, qkv, qkvT, qtb, dotb, qkv, do, augqb, lseb, otb),
        in_specs=[pl.BlockSpec((None, 2, tb, 128), lambda hp, kj: (hp, 0, kj, 0)), kblk(2 * HP),
                  pl.BlockSpec((128, tb), lambda hp, kj: (HP + hp, kj)),
                  resident((128, tb)), resident((128, tb)), whole, whole, resident((2, AUG_ROWS, tb)), resident((8, tb)),
                  resident((128, tb))],
        out_specs=[resident((128, tb)), kblk(0), kblk(0), pl.BlockSpec((None, tb, 128), lambda hp, kj: (hp, kj, 0)),
                   resident((8, tb))],
        out_shape=[jax.ShapeDtypeStruct((HP, nb, 128, tb), BF16), jax.ShapeDtypeStruct((S, D_MODEL), BF16),
                   jax.ShapeDtypeStruct((S, D_MODEL), BF16), jax.ShapeDtypeStruct((HP, S, 128), F32),
                   jax.ShapeDtypeStruct((HP, nb, 8, tb), F32)],
        scratch_shapes=[pltpu.VMEM((nb, 128, tb), F32), pltpu.VMEM((tb, 128), F32), pltpu.VMEM((tb, 128), F32),
                        pltpu.VMEM((2, tb, 128), F32), pltpu.VMEM((nb, 8, tb), F32)])


def _split3(x):
    rnd = lambda v: lax.reduce_precision(v, exponent_bits=8, mantissa_bits=7)
    hi = rnd(x)
    mid = rnd(x - hi)
    lo = rnd(x - hi - mid)
    return hi.astype(BF16), mid.astype(BF16), lo.astype(BF16)


def _blocked(xt, nb, tb):
    lead = xt.shape[:-1]
    x = xt.reshape(lead + (nb, tb))
    return jnp.moveaxis(x, -2, 1)


def _aug_rows(bias):
    ones = jnp.ones(bias.shape, BF16)
    zeros = jnp.zeros(bias.shape, BF16)
    return jnp.stack(list(_split3(bias)) + [ones] * 3 + [zeros] * (AUG_ROWS - 6), axis=2)


def _fox_keys(qkv, fcol, *, name):
    S = qkv.shape[0]
    tr = _tile(S, 2048, 16)

    def body(k_ref, f_ref, o_ref):
        hp = pl.program_id(1)
        k2 = k_ref[...]
        lanes = lax.broadcasted_iota(jnp.int32, (tr, 128), 1)
        for hh in range(2):
            nf = -jnp.sum(jnp.where(lanes == 2 * hp + hh, f_ref[...], 0.0), axis=1, keepdims=True)
            hi = nf.astype(BF16).astype(F32)
            mid = (nf - hi).astype(BF16).astype(F32)
            lo = nf - hi - mid
            base = HEAD_DIM * (1 - hh)
            aug = jnp.where((lanes >= base) & (lanes < base + 3), 1.0, 0.0)
            for j, piece in enumerate((hi, mid, lo)):
                aug = jnp.where(lanes == base + 3 + j, piece, aug)
            mine = (lanes >= HEAD_DIM * hh) & (lanes < HEAD_DIM * (hh + 1))
            o_ref[hh] = jnp.where(mine, k2, aug.astype(BF16))

    return pl.pallas_call(
        body, name=name, grid=(S // tr, HP),
        in_specs=[pl.BlockSpec((tr, 128), lambda i, hp: (i, HP + hp)), pl.BlockSpec((tr, 128), lambda i, hp: (i, 0))],
        out_specs=pl.BlockSpec((None, 2, tr, 128), lambda i, hp: (hp, 0, i, 0)),
        out_shape=jax.ShapeDtypeStruct((HP, 2, S, 128), BF16), compiler_params=_cp("parallel", "parallel"))(qkv, fcol)


HALF = S5_SUB * S5_P
NCB = 2 * HALF // 128
RE, IM = slice(0, NCB // 2), slice(NCB // 2, NCB)


def _gelu(x):
    return 0.5 * x * (1.0 + jnp.tanh(GELU_K * (x + 0.044715 * x * x * x)))


def _gelu_grad(x):
    t = jnp.tanh(GELU_K * (x + 0.044715 * x * x * x))
    return 0.5 * (1.0 + t) + 0.5 * x * (1.0 - t * t) * GELU_K * (1.0 + 3.0 * 0.044715 * x * x)


def _s5_put(buf, s, val, lt):
    for cb in range(NCB):
        buf[cb, pl.ds(s, lt, stride=S5_SUB), :] = val[:, 128 * cb:128 * (cb + 1)]


def _s5_get(buf, s, lt):
    return jnp.concatenate([buf[cb, pl.ds(s, lt, stride=S5_SUB), :] for cb in range(NCB)], axis=1)


def _s5_project_in(u_ref, wb_ref, buf, lt):
    for s in range(S5_SUB):
        _s5_put(buf, s, jnp.dot(u_ref[:, 128 * s:128 * (s + 1)], wb_ref[s], preferred_element_type=F32), lt)


def _s5_scan(buf, lam_ref, h0, lt):
    a_re, a_im = lam_ref[RE], lam_ref[IM]
    a2_re, a2_im = a_re * a_re - a_im * a_im, 2.0 * a_re * a_im

    def step(i, carry):
        hr, hi = carry
        r0 = pl.multiple_of(i * 2 * S5_SUB, 2 * S5_SUB)
        r1 = r0 + S5_SUB
        b0r, b0i = buf[RE, pl.ds(r0, S5_SUB), :], buf[IM, pl.ds(r0, S5_SUB), :]
        cr = a_re * b0r - a_im * b0i + buf[RE, pl.ds(r1, S5_SUB), :]
        ci = a_re * b0i + a_im * b0r + buf[IM, pl.ds(r1, S5_SUB), :]
        buf[RE, pl.ds(r0, S5_SUB), :] = a_re * hr - a_im * hi + b0r
        buf[IM, pl.ds(r0, S5_SUB), :] = a_re * hi + a_im * hr + b0i
        nr = a2_re * hr - a2_im * hi + cr
        ni = a2_re * hi + a2_im * hr + ci
        buf[RE, pl.ds(r1, S5_SUB), :] = nr
        buf[IM, pl.ds(r1, S5_SUB), :] = ni
        return nr, ni

    return lax.fori_loop(0, lt // 2, step, (h0[RE], h0[IM]))


def _s5_fwd(u, wb, wc, lam, dskip, *, lt, name, comm=()):
    S, D = u.shape
    nt = S // lt

    def body(u_ref, wb_ref, wc_ref, lam_ref, d_ref, yp_ref, yg_ref, st_ref, buf, h_sc):
        @pl.when(pl.program_id(0) == 0)
        def _():
            h_sc[...] = jnp.zeros_like(h_sc)

        st_ref[...] = h_sc[...]
        _s5_project_in(u_ref, wb_ref, buf, lt)
        hr, hi = _s5_scan(buf, lam_ref, h_sc[...], lt)
        h_sc[RE] = hr
        h_sc[IM] = hi
        for s in range(S5_SUB):
            cols = slice(128 * s, 128 * (s + 1))
            hs = _s5_get(buf, s, lt).astype(BF16)
            yp = jnp.dot(hs, wc_ref[s], preferred_element_type=F32) + d_ref[:, cols] * u_ref[:, cols].astype(F32)
            yp_ref[:, cols] = yp.astype(yp_ref.dtype)
            yg_ref[:, cols] = _gelu(yp).astype(yg_ref.dtype)

    blk = pl.BlockSpec((lt, D), lambda i: (i, 0))
    full = lambda shp: pl.BlockSpec(shp, lambda i: (0,) * len(shp))
    state = (NCB, S5_SUB, 128)
    return _call(
        body, name=name, grid=(nt,), comm=comm, sem=("arbitrary",), args=(u, wb, wc, lam, dskip),
        in_specs=[blk, full(wb.shape), full(wc.shape), full(lam.shape), full(dskip.shape)],
        out_specs=[blk, blk, pl.BlockSpec((None,) + state, lambda i: (i, 0, 0, 0))],
        out_shape=[jax.ShapeDtypeStruct((S, D), BF16), jax.ShapeDtypeStruct((S, D), BF16), jax.ShapeDtypeStruct((nt,) + state, F32)],
        scratch_shapes=[pltpu.VMEM((NCB, lt * S5_SUB, 128), F32), pltpu.VMEM(state, F32)])


def _s5_bwd(u, dyp, st, wb, wc, lam, dskip, *, lt, name, comm=()):
    S, D = u.shape
    nt = S // lt

    def body(u_ref, dy_ref, st_ref, wb_ref, wc_ref, lam_ref, d_ref, du_ref, dwb_ref, dwc_ref, dlam_ref, hbuf, gbuf, g_sc):
        @pl.when(pl.program_id(0) == 0)
        def _():
            g_sc[...] = jnp.zeros_like(g_sc)
            dwb_ref[...] = jnp.zeros_like(dwb_ref)
            dwc_ref[...] = jnp.zeros_like(dwc_ref)
            dlam_ref[...] = jnp.zeros_like(dlam_ref)

        _s5_project_in(u_ref, wb_ref, hbuf, lt)
        _s5_scan(hbuf, lam_ref, st_ref[...], lt)
        for s in range(S5_SUB):
            dys = dy_ref[:, 128 * s:128 * (s + 1)]
            _s5_put(gbuf, s, lax.dot_general(dys, wc_ref[s], NT_DIMS, preferred_element_type=F32), lt)
        a_re, a_im = lam_ref[RE], lam_ref[IM]

        def one(t, carry, hp_re, hp_im):
            gr, gi, dar, dai = carry
            r0 = pl.multiple_of(t * S5_SUB, S5_SUB)
            nr = gbuf[RE, pl.ds(r0, S5_SUB), :] + a_re * gr + a_im * gi
            ni = gbuf[IM, pl.ds(r0, S5_SUB), :] + a_re * gi - a_im * gr
            gbuf[RE, pl.ds(r0, S5_SUB), :] = nr
            gbuf[IM, pl.ds(r0, S5_SUB), :] = ni
            return nr, ni, dar + nr * hp_re + ni * hp_im, dai + ni * hp_re - nr * hp_im

        def step(k, carry):
            t = lt - 1 - k
            p0 = pl.multiple_of((t - 1) * S5_SUB, S5_SUB)
            return one(t, carry, hbuf[RE, pl.ds(p0, S5_SUB), :], hbuf[IM, pl.ds(p0, S5_SUB), :])

        init = (g_sc[RE], g_sc[IM], dlam_ref[RE], dlam_ref[IM])
        carry = lax.fori_loop(0, lt - 1, step, init, unroll=2)
        gr, gi, dar, dai = one(0, carry, st_ref[RE], st_ref[IM])
        g_sc[RE] = gr
        g_sc[IM] = gi
        dlam_ref[RE] = dar
        dlam_ref[IM] = dai
        for s in range(S5_SUB):
            cols = slice(128 * s, 128 * (s + 1))
            gs = _s5_get(gbuf, s, lt).astype(BF16)
            hs = _s5_get(hbuf, s, lt).astype(BF16)
            us, dys = u_ref[:, cols], dy_ref[:, cols]
            du = lax.dot_general(gs, wb_ref[s], NT_DIMS, preferred_element_type=F32) + d_ref[:, cols] * dys.astype(F32)
            du_ref[:, cols] = du.astype(du_ref.dtype)
            dwb_ref[s] += lax.dot_general(us, gs, TN_DIMS, preferred_element_type=F32)
            dwc_ref[s] += lax.dot_general(hs, dys, TN_DIMS, preferred_element_type=F32)

    blk = pl.BlockSpec((lt, D), lambda i: (nt - 1 - i, 0))
    full = lambda shp: pl.BlockSpec(shp, lambda i: (0,) * len(shp))
    state = (NCB, S5_SUB, 128)
    words = pltpu.VMEM((NCB, lt * S5_SUB, 128), F32)
    return _call(
        body, name=name, grid=(nt,), comm=comm, sem=("arbitrary",), args=(u, dyp, st, wb, wc, lam, dskip),
        in_specs=[blk, blk, pl.BlockSpec((None,) + state, lambda i: (nt - 1 - i, 0, 0, 0)),
                  full(wb.shape), full(wc.shape), full(lam.shape), full(dskip.shape)],
        out_specs=[blk, full(wb.shape), full(wc.shape), full(lam.shape)],
        out_shape=[jax.ShapeDtypeStruct((S, D), BF16), jax.ShapeDtypeStruct(wb.shape, F32), jax.ShapeDtypeStruct(wc.shape, F32),
                   jax.ShapeDtypeStruct(lam.shape, F32)],
        scratch_shapes=[words, words, pltpu.VMEM(state, F32)])


def _s5_disc(lam_re, lam_im, log_dt, b_re, b_im):
    dt = jnp.exp(log_dt)[:, None]
    mag = jnp.exp(lam_re * dt)
    lb_re, lb_im = mag * jnp.cos(lam_im * dt), mag * jnp.sin(lam_im * dt)
    num_re = lb_re - 1.0
    den = lam_re * lam_re + lam_im * lam_im
    k_re = (num_re * lam_re + lb_im * lam_im) / den
    k_im = (lb_im * lam_re - num_re * lam_im) / den
    bb_re = k_re[..., None] * b_re - k_im[..., None] * b_im
    bb_im = k_re[..., None] * b_im + k_im[..., None] * b_re
    return lb_re, lb_im, bb_re, bb_im


def _s5_pack(lb_re, lb_im, bb_re, bb_im, c_re, c_im):
    eye = jnp.eye(S5_SUB, dtype=F32)
    tb = lambda x: jnp.einsum("sgcp,gh->sgchp", x.reshape(S5_SUB, S5_SUB, S5_P, S5_C).transpose(0, 1, 3, 2), eye).reshape(S5_SUB, 128, HALF)
    tc = lambda x: jnp.einsum("sgpc,gh->sgphc", x.reshape(S5_SUB, S5_SUB, S5_C, S5_P).transpose(0, 1, 3, 2), eye).reshape(S5_SUB, HALF, 128)
    wb = jnp.concatenate([tb(bb_re), tb(bb_im)], axis=2).astype(BF16)
    wc = jnp.concatenate([tc(c_re), -tc(c_im)], axis=1).astype(BF16)
    lam = jnp.concatenate([lb_re.reshape(S5_SUB, HALF), lb_im.reshape(S5_SUB, HALF)], axis=1)
    return wb, wc, lam.reshape(S5_SUB, NCB, 128).transpose(1, 0, 2)


def _s5_unpack(dwb, dwc, dlam):
    db = jnp.einsum("sgcrgp->rsgpc", dwb.reshape(S5_SUB, S5_SUB, S5_C, 2, S5_SUB, S5_P)).reshape(2, S5_G, S5_P, S5_C)
    dc = jnp.einsum("srgpgc->rsgcp", dwc.reshape(S5_SUB, 2, S5_SUB, S5_P, S5_SUB, S5_C)).reshape(2, S5_G, S5_C, S5_P)
    dlam = dlam.transpose(1, 0, 2).reshape(S5_SUB, 2 * HALF)
    return (dlam[:, :HALF].reshape(S5_G, S5_P), dlam[:, HALF:].reshape(S5_G, S5_P), db[0], db[1], dc[0], -dc[1])


def _glu_fwd(yg, gl, *, name):
    S, D = yg.shape
    tm = _tile(S, 512, 8)
    blk = pl.BlockSpec((tm, D), lambda i: (i, 0))

    def body(y_ref, g_ref, o_ref):
        o_ref[...] = (y_ref[...].astype(F32) * jax.nn.sigmoid(g_ref[...].astype(F32))).astype(o_ref.dtype)

    return pl.pallas_call(body, name=name, grid=(S // tm,), in_specs=[blk, blk], out_specs=blk,
                          out_shape=jax.ShapeDtypeStruct((S, D), BF16), compiler_params=_cp("parallel"))(yg, gl)


def _glu_bwd(dy2, yg, gl, *, name):
    S, D = yg.shape
    tm = _tile(S, 512, 8)
    blk = pl.BlockSpec((tm, D), lambda i: (i, 0))

    def body(d_ref, y_ref, g_ref, da_ref, dg_ref):
        d = d_ref[...].astype(F32)
        sg = jax.nn.sigmoid(g_ref[...].astype(F32))
        da_ref[...] = d * sg
        dg_ref[...] = (d * y_ref[...].astype(F32) * sg * (1.0 - sg)).astype(dg_ref.dtype)

    return pl.pallas_call(body, name=name, grid=(S // tm,), in_specs=[blk, blk, blk], out_specs=[blk, blk],
                          out_shape=[jax.ShapeDtypeStruct((S, D), F32), jax.ShapeDtypeStruct((S, D), BF16)],
                          compiler_params=_cp("parallel"))(dy2, yg, gl)


def _gelu_bwd(da, db, ypre, u, *, name):
    S, D = ypre.shape
    tm = _tile(S, 512, 8)
    blk = pl.BlockSpec((tm, D), lambda i: (i, 0))
    vec = pl.BlockSpec((1, D), lambda i: (0, 0))

    def body(a_ref, b_ref, y_ref, u_ref, o_ref, s_ref):
        @pl.when(pl.program_id(0) == 0)
        def _():
            s_ref[...] = jnp.zeros_like(s_ref)

        dy = (a_ref[...] + b_ref[...]) * _gelu_grad(y_ref[...].astype(F32))
        o_ref[...] = dy.astype(o_ref.dtype)
        s_ref[...] += jnp.sum(dy * u_ref[...].astype(F32), axis=0, keepdims=True)

    return pl.pallas_call(body, name=name, grid=(S // tm,), in_specs=[blk, blk, blk, blk], out_specs=[blk, vec],
                          out_shape=[jax.ShapeDtypeStruct((S, D), BF16), jax.ShapeDtypeStruct((1, D), F32)],
                          compiler_params=_cp("arbitrary"))(da, db, ypre, u)


def _my_index():
    return 4 * lax.axis_index("x") + 2 * lax.axis_index("y") + lax.axis_index("c")


def _comm_plan(payloads):
    n = len(payloads)
    hbm = pl.BlockSpec(memory_space=pl.ANY)
    shapes = [jax.ShapeDtypeStruct((N_DEV,) + (x.shape if g else x.shape[1:]), x.dtype) for x, g in payloads]
    sems = [pltpu.SemaphoreType.DMA((n, N_DEV - 1)), pltpu.SemaphoreType.DMA((n, N_DEV - 1)), pltpu.SemaphoreType.DMA((n,))] if n else []
    return [hbm] * n, [hbm] * n, shapes, sems


def _comm_copies(x_refs, o_refs, gathers, send_sems, recv_sems, local_sems):
    mx, my, mc = lax.axis_index("x"), lax.axis_index("y"), lax.axis_index("c")
    me = 4 * mx + 2 * my + mc
    copies = []
    for i, (x_ref, o_ref, gather) in enumerate(zip(x_refs, o_refs, gathers)):
        src = (lambda j, r=x_ref: r) if gather else (lambda j, r=x_ref: r.at[j])
        copies.append(pltpu.make_async_copy(src(me), o_ref.at[me], local_sems.at[i]))
        for k in range(1, N_DEV):
            px, py, pc = mx ^ (k >> 2), my ^ ((k >> 1) & 1), mc ^ (k & 1)
            copies.append(pltpu.make_async_remote_copy(src(4 * px + 2 * py + pc), o_ref.at[me], send_sems.at[i, k - 1],
                                                       recv_sems.at[i, k - 1], device_id=(px, py, pc), device_id_type=MESH_ID))
    return copies


def _call(body, *, name, grid, in_specs, out_specs, out_shape, scratch_shapes, sem, args, comm=()):
    if not comm:
        outs = pl.pallas_call(body, name=name, grid=grid, in_specs=in_specs, out_specs=out_specs, out_shape=out_shape,
                              scratch_shapes=scratch_shapes, compiler_params=_cp(*sem))(*args)
        return outs, ()
    n, n_in, n_out = len(comm), len(in_specs), len(out_specs)
    c_in, c_out, c_shapes, c_sems = _comm_plan(comm)
    gathers = [g for _, g in comm]

    def wrapped(*refs):
        own_in, x_refs = refs[:n_in], refs[n_in:n_in + n]
        own_out, o_refs = refs[n_in + n:n_in + n + n_out], refs[n_in + n + n_out:n_in + 2 * n + n_out]
        own_scratch, sems = refs[n_in + 2 * n + n_out:-3], refs[-3:]
        ids = [pl.program_id(a) for a in range(len(grid))]
        first = functools.reduce(jnp.logical_and, [i == 0 for i in ids])
        last = functools.reduce(jnp.logical_and, [i == g - 1 for i, g in zip(ids, grid)])

        @pl.when(first)
        def _():
            for cp in _comm_copies(x_refs, o_refs, gathers, *sems):
                cp.start()

        body(*own_in, *own_out, *own_scratch)

        @pl.when(last)
        def _():
            for cp in _comm_copies(x_refs, o_refs, gathers, *sems):
                cp.wait()

    outs = pl.pallas_call(wrapped, name=name, grid=grid, in_specs=list(in_specs) + c_in, out_specs=list(out_specs) + c_out,
                          out_shape=list(out_shape) + c_shapes, scratch_shapes=list(scratch_shapes) + c_sems,
                          compiler_params=_cp(*["arbitrary"] * len(grid)))(*args, *[x for x, _ in comm])
    return outs[:n_out], outs[n_out:]


def _exchange_many(payloads, *, name):
    n = len(payloads)
    in_specs, out_specs, shapes, sems = _comm_plan(payloads)
    gathers = [g for _, g in payloads]

    def body(*refs):
        copies = _comm_copies(refs[:n], refs[n:2 * n], gathers, *refs[2 * n:])
        for cp in copies:
            cp.start()
        for cp in copies:
            cp.wait()

    return pl.pallas_call(body, name=name, in_specs=in_specs, out_specs=out_specs, out_shape=shapes, scratch_shapes=sems,
                          compiler_params=pltpu.CompilerParams(has_side_effects=True))(*[x for x, _ in payloads])


def _exchange(x, *, gather, name):
    return _exchange_many([(x, gather)], name=name)[0]


def _sum8(x8, *, name):
    _, R, C = x8.shape
    tr = _tile(R, 256, 16)

    def body(x_ref, o_ref):
        acc = x_ref[0].astype(F32)
        for j in range(1, N_DEV):
            acc = acc + x_ref[j].astype(F32)
        o_ref[...] = acc

    return pl.pallas_call(body, name=name, grid=(R // tr,), in_specs=[pl.BlockSpec((N_DEV, tr, C), lambda i: (0, i, 0))],
                          out_specs=pl.BlockSpec((tr, C), lambda i: (i, 0)), out_shape=jax.ShapeDtypeStruct((R, C), F32),
                          compiler_params=_cp("parallel"))(x8)


def _pack(arrs, dtype, row_mult):
    flat = jnp.concatenate([a.reshape(-1).astype(dtype) for a in arrs])
    n = flat.shape[0]
    rows = -(-n // 1024)
    rows = -(-rows // row_mult) * row_mult
    return jnp.pad(flat, (0, rows * 1024 - n)).reshape(rows, 1024)


def _unpack(buf, shapes):
    lead = buf.shape[:-2]
    flat = buf.reshape(lead + (-1,))
    out, off = [], 0
    for s in shapes:
        n = math.prod(s)
        out.append(flat[..., off:off + n].reshape(lead + tuple(s)))
        off += n
    return out


def _adaln_fwd(c_all, w, b, *, name):
    nk, D, n = w.shape

    def body(c_ref, w_ref, b_ref, o_ref):
        cv = c_ref[...]
        sc = (cv * jax.nn.sigmoid(cv)).astype(BF16)
        o_ref[...] = jnp.dot(sc, w_ref[...].astype(BF16), preferred_element_type=F32) + b_ref[...]

    return pl.pallas_call(body, name=name, grid=(nk,),
                          in_specs=[pl.BlockSpec((N_DEV, D), lambda k: (0, 0)), pl.BlockSpec((None, D, n), lambda k: (k, 0, 0)),
                                    pl.BlockSpec((None, 1, n), lambda k: (k, 0, 0))],
                          out_specs=pl.BlockSpec((None, N_DEV, n), lambda k: (k, 0, 0)),
                          out_shape=jax.ShapeDtypeStruct((nk, N_DEV, n), F32), compiler_params=_cp("parallel"))(c_all, w, b)


def _adaln_bwd(c_all, dmod, *, name):
    nk, _, n = dmod.shape
    D = c_all.shape[1]

    def body(c_ref, d_ref, dw_ref, db_ref):
        cv = c_ref[...]
        sc = (cv * jax.nn.sigmoid(cv)).astype(BF16).astype(F32)
        dm = d_ref[...]
        dw_ref[...] = lax.dot_general(sc, dm.astype(BF16).astype(F32), TN_DIMS, precision=lax.Precision.HIGHEST,
                                      preferred_element_type=F32)
        db_ref[...] = jnp.sum(dm, axis=0, keepdims=True)

    return pl.pallas_call(body, name=name, grid=(nk,),
                          in_specs=[pl.BlockSpec((N_DEV, D), lambda k: (0, 0)), pl.BlockSpec((None, N_DEV, n), lambda k: (k, 0, 0))],
                          out_specs=[pl.BlockSpec((None, D, n), lambda k: (k, 0, 0)), pl.BlockSpec((None, 1, n), lambda k: (k, 0, 0))],
                          out_shape=[jax.ShapeDtypeStruct((nk, D, n), F32), jax.ShapeDtypeStruct((nk, 1, n), F32)],
                          compiler_params=_cp("parallel"))(c_all, dmod)


def _adamw(w, g, m, v, *, name):
    R, C = w.shape
    tr = _tile(R, 256, 8)
    blk = pl.BlockSpec((tr, C), lambda i: (i, 0))
    c1 = 1.0 / (1.0 - ADAM_B1 ** ADAM_STEP)
    c2 = 1.0 / (1.0 - ADAM_B2 ** ADAM_STEP)

    def body(w_ref, g_ref, m_ref, v_ref, d_ref, nm_ref, nv_ref):
        gv = g_ref[...]
        nm = ADAM_B1 * m_ref[...] + (1.0 - ADAM_B1) * gv
        nv = ADAM_B2 * v_ref[...] + (1.0 - ADAM_B2) * (gv * gv)
        nm_ref[...] = nm
        nv_ref[...] = nv
        d_ref[...] = -ADAM_LR * ((nm * c1) / (jnp.sqrt(nv * c2) + ADAM_EPS) + ADAM_WD * w_ref[...])

    sh = jax.ShapeDtypeStruct((R, C), F32)
    return pl.pallas_call(body, name=name, grid=(R // tr,), in_specs=[blk] * 4, out_specs=[blk] * 3, out_shape=[sh] * 3,
                          compiler_params=_cp("parallel"))(w, g, m, v)


def _sum8_adamw(r8, w, m, v, *, name):
    _, R, C = r8.shape
    tr = _tile(R, 256, 16)
    blk = pl.BlockSpec((tr, C), lambda i: (i, 0))
    c1 = 1.0 / (1.0 - ADAM_B1 ** ADAM_STEP)
    c2 = 1.0 / (1.0 - ADAM_B2 ** ADAM_STEP)

    def body(r_ref, w_ref, m_ref, v_ref, g_ref, d_ref, nm_ref, nv_ref):
        gv = r_ref[0].astype(F32)
        for j in range(1, N_DEV):
            gv = gv + r_ref[j].astype(F32)
        g_ref[...] = gv
        nm = ADAM_B1 * m_ref[...] + (1.0 - ADAM_B1) * gv
        nv = ADAM_B2 * v_ref[...] + (1.0 - ADAM_B2) * (gv * gv)
        nm_ref[...] = nm
        nv_ref[...] = nv
        d_ref[...] = -ADAM_LR * ((nm * c1) / (jnp.sqrt(nv * c2) + ADAM_EPS) + ADAM_WD * w_ref[...])

    sh = jax.ShapeDtypeStruct((R, C), F32)
    return pl.pallas_call(body, name=name, grid=(R // tr,),
                          in_specs=[pl.BlockSpec((N_DEV, tr, C), lambda i: (0, i, 0))] + [blk] * 3, out_specs=[blk] * 4,
                          out_shape=[sh] * 4, compiler_params=_cp("parallel"))(r8, w, m, v)


def _adamw_nd(w, g, m, v, *, name):
    shp = w.shape
    two = (-1, shp[-1]) if w.ndim > 1 else (1, -1)
    outs = _adamw(w.reshape(two), g.reshape(two), m.reshape(two), v.reshape(two), name=name)
    return [o.reshape(shp) for o in outs]


S5_TILE = 256
ATTN_BLOCK = 1024
BIG = ("s5_w_in", "s5_w_glu", "s5_w_out", "fox_w_in", "fox_w_out", "ffn_w_up", "ffn_w_down")
WEIGHTS = ("norm_g", "ada_w", "ada_b", "s5_w_in", "s5_lam_re", "s5_lam_im", "s5_log_dt", "s5_b_re", "s5_b_im", "s5_c_re",
           "s5_c_im", "s5_d", "s5_w_glu", "s5_w_out", "fox_w_in", "fox_b_f", "fox_w_out", "ffn_w_up", "ffn_conv_w",
           "ffn_conv_b", "ffn_w_down", "final_g")
REPLICATED = ("s5_lam_re", "s5_lam_im", "s5_log_dt", "s5_b_re", "s5_b_im", "s5_c_re", "s5_c_im", "s5_d", "fox_b_f",
              "ffn_conv_b", "final_g")


def _row_group(arrs):
    return jnp.concatenate([a.reshape(-1, D_MODEL) for a in arrs], axis=0)


def _cols_join(g):
    return jnp.concatenate([g[j] for j in range(N_DEV)], axis=1)


def _cols_split(full):
    n = full.shape[1] // N_DEV
    return jnp.stack([full[:, j * n:(j + 1) * n] for j in range(N_DEV)])


def kernel(x, c, norm_g, ada_w, ada_b, s5_w_in, s5_lam_re, s5_lam_im, s5_log_dt, s5_b_re, s5_b_im, s5_c_re, s5_c_im, s5_d, s5_w_glu, s5_w_out, fox_w_in, fox_b_f, fox_w_out, ffn_w_up, ffn_conv_w, ffn_conv_b, ffn_w_down, final_g, loss_target, m_norm_g, m_ada_w, m_ada_b, m_s5_w_in, m_s5_lam_re, m_s5_lam_im, m_s5_log_dt, m_s5_b_re, m_s5_b_im, m_s5_c_re, m_s5_c_im, m_s5_d, m_s5_w_glu, m_s5_w_out, m_fox_w_in, m_fox_b_f, m_fox_w_out, m_ffn_w_up, m_ffn_conv_w, m_ffn_conv_b, m_ffn_w_down, m_final_g, v_norm_g, v_ada_w, v_ada_b, v_s5_w_in, v_s5_lam_re, v_s5_lam_im, v_s5_log_dt, v_s5_b_re, v_s5_b_im, v_s5_c_re, v_s5_c_im, v_s5_d, v_s5_w_glu, v_s5_w_out, v_fox_w_in, v_fox_b_f, v_fox_w_out, v_ffn_w_up, v_ffn_conv_w, v_ffn_conv_b, v_ffn_w_down, v_final_g):
    args = dict(locals())
    W = {n: args[n] for n in WEIGHTS}
    M = {n: args["m_" + n] for n in WEIGHTS}
    V = {n: args["v_" + n] for n in WEIGHTS}
    D, F = D_MODEL, D_FF
    me = _my_index()
    h0 = x[0]
    S = h0.shape[0]
    lt = min(S5_TILE, S)
    tb = min(ATTN_BLOCK, S)

    b16 = lambda a: a.astype(BF16)
    g0, g_in = _exchange_many([(_pack([c, norm_g, ffn_conv_w], F32, 8), True), (b16(s5_w_in[0]), True)], name="gather_first")
    c_all, ng_all, cw_all = _unpack(g0, [(D,), norm_g.shape, ffn_conv_w.shape])
    ng_full = ng_all.transpose(1, 2, 0, 3).reshape(2, 2, D)
    cw_full = cw_all.transpose(1, 2, 0, 3).reshape(2, 3, F)

    ncol = ada_w.shape[-1]
    modp = _adaln_fwd(c_all, ada_w.reshape(4, D, ncol), ada_b.reshape(4, 1, ncol), name="adaln_fwd")
    g1 = _exchange(_pack([modp], F32, 8), gather=True, name="gather_adaln")
    (mod_all,) = _unpack(g1, [modp.shape])
    mod = lax.dynamic_index_in_dim(mod_all, me, axis=2, keepdims=False).transpose(1, 0, 2).reshape(4, 3 * D)
    shift = [mod[k:k + 1, :D] for k in range(4)]
    scale = [mod[k:k + 1, D:2 * D] for k in range(4)]
    gate = [mod[k:k + 1, 2 * D:] for k in range(4)]
    gain = [_row(ng_full[k // 2, k % 2]) for k in range(4)]

    rows = D // N_DEV
    full = {"s5_w_in": g_in.reshape(D, D)}
    bf_pad = jnp.pad(fox_b_f, ((0, 0), (0, 128 - HEADS)))

    def ffn_fwd(h, k, layer, comm=(), down_shard=None):
        hn = _modulate(h, gain[k], shift[k], scale[k], name=f"modulate{k}")
        up = _mm(hn, full[f"ffn_w_up{layer}"], name=f"ffn_up{layer}", o_h=True, out_dtype=BF16, tn=1408, comm=comm)
        up, exchanged = up if comm else (up, ())
        (z,), gathered = _conv_gate_fwd(up, cw_full[layer], ffn_conv_b[layer:layer + 1], name=f"conv_gate{layer}",
                                        comm=() if down_shard is None else [(down_shard, True)])
        if down_shard is not None:
            full[f"ffn_w_down{layer}"] = gathered[0].reshape(F, D)
        m, h_out = _mm(z, full[f"ffn_w_down{layer}"], name=f"ffn_down{layer}", res=h, gate=gate[k])
        return h_out, (hn, up, z, m), exchanged

    lb_re, lb_im, bb_re, bb_im = _s5_disc(s5_lam_re[0], s5_lam_im[0], s5_log_dt[0], s5_b_re[0], s5_b_im[0])
    wb, wc, lam = _s5_pack(lb_re, lb_im, bb_re, bb_im, s5_c_re[0], s5_c_im[0])
    hn0 = _modulate(h0, gain[0], shift[0], scale[0], name="modulate0")
    u = _mm(hn0, full["s5_w_in"], name="s5_in", out_dtype=BF16)
    (ypre, yg, st), (g_up0, g_rows0) = _s5_fwd(
        u, wb, wc, lam, s5_d, lt=lt, name="s5_scan",
        comm=[(b16(ffn_w_up[0]), True), (_row_group([b16(s5_w_glu), b16(s5_w_out)]), True)])
    full["ffn_w_up0"] = _cols_join(g_up0)
    full["s5_w_glu"] = g_rows0[:, :rows].reshape(D, D)
    full["s5_w_out"] = g_rows0[:, rows:].reshape(D, D)
    gl = _mm(yg, full["s5_w_glu"], name="s5_glu", out_dtype=BF16)
    y2 = _glu_fwd(yg, gl, name="s5_glu_gate")
    m0, h1 = _mm(y2, full["s5_w_out"], name="s5_out", res=h0, gate=gate[0])
    h2, (hn1, up0, z0, m1), (g_fox, g_fox_out) = ffn_fwd(h1, 1, 0, comm=[(b16(fox_w_in[0]), True), (b16(fox_w_out[0]), True)],
                                                         down_shard=b16(ffn_w_down[0]))
    full["fox_w_in"] = _cols_join(g_fox)
    full["fox_w_out"] = g_fox_out.reshape(D, D)
    w_proj = jnp.pad(full["fox_w_in"], ((0, 0), (0, 3 * D + 128 - full["fox_w_in"].shape[1])))
    w_qkv, w_f = w_proj[:, :3 * D], w_proj[:, 3 * D:]

    hn2 = _modulate(h2, gain[2], shift[2], scale[2], name="modulate2")
    nb = S // tb
    w_qkv_s = jnp.concatenate([w_qkv[:, :D] * Q_SCALE, w_qkv[:, D:]], axis=1)
    qkv, qkvT = _mm(hn2, w_qkv_s, name="fox_qkv", out_dtype=BF16, with_t=True)
    fl = _mm(hn2, w_f, name="fox_f", out_dtype=F32)
    fcol = _fgate_fwd(fl, bf_pad, name="fox_fgate")
    f_heads = fcol[:, :HEADS].T.reshape(HP, 2, S)
    ka = _fox_keys(qkv, fcol, name="fox_keys")
    vtb = _blocked(qkvT[2 * D:].reshape(HP, 128, S), nb, tb)
    aug = _aug_rows(f_heads)
    (oT, lse), (g_up1, g_down1) = _foxt_fwd(qkvT, ka, vtb, aug, tb=tb, name="fox_attn",
                                            comm=[(b16(ffn_w_up[1]), True), (b16(ffn_w_down[1]), True)])
    full["ffn_w_up1"] = _cols_join(g_up1)
    full["ffn_w_down1"] = g_down1.reshape(F, D)
    m2, h3 = _mm(oT, full["fox_w_out"], name="fox_out", ta=True, res=h2, gate=gate[2])
    h4, (hn3, up1, z1, m3), _ = ffn_fwd(h3, 3, 1)

    lblk, dh, head_sums, dm = _loss_head(h4, _row(final_g), loss_target[0], m3, gate[3], name="loss_head")
    d_final_g, dgate = head_sums[0:1], head_sums[1:2]
    loss = lax.psum(lblk[0, 0], ("x", "y", "c"))

    G = {}
    dmod = [None] * 4
    mixer_out = [m0, m1, m2, m3]

    def norm_bwd(h, dhn, dh_in, k, dgate_k):
        if k == 0:
            dh_out, sums = _modulate_bwd(h, dhn, dh_in, gain[k], scale[k], name=f"modulate_bwd{k}")
            dm_below = dgate_below = None
        else:
            dh_out, sums, dm_below = _modulate_bwd(h, dhn, dh_in, gain[k], scale[k], mixer_out[k - 1], gate[k - 1],
                                                   name=f"modulate_bwd{k}")
            dgate_below = sums[3:4]
        dmod[k] = jnp.concatenate([sums[0:1], sums[1:2], dgate_k], axis=1)
        return dh_out, sums[2], dm_below, dgate_below

    def ffn_bwd(dh_in, dm, dgate_k, h, k, layer, saved):
        hn, up, z, m = saved
        dz = _mm(dm, full[f"ffn_w_down{layer}"], name=f"ffn_down_dx{layer}", tb=True, out_dtype=BF16, tn=1408)
        dw_down = _mm(z, dm, name=f"ffn_down_dw{layer}", ta=True, tm=1408, tk=2048)
        d_up, cs = _conv_gate_bwd(up, dz, cw_full[layer], ffn_conv_b[layer:layer + 1], name=f"conv_gate_bwd{layer}")
        dhn = _mm(d_up, full[f"ffn_w_up{layer}"], name=f"ffn_up_dx{layer}", tb=True, a_h=True, tk=2816)
        dw_up = _mm(hn, d_up, name=f"ffn_up_dw{layer}", ta=True, b_h=True, tm=1024, tn=1408, tk=2048)
        dh_out, dg, dm_below, dgate_below = norm_bwd(h, dhn, dh_in, k, dgate_k)
        return dh_out, dm_below, dgate_below, dg, dw_up, dw_down, cs[0:3], cs[3]

    dh, dm, dgate, dg3, dw_up1, dw_down1, dcw1, dcb1 = ffn_bwd(dh, dm, dgate, h3, 3, 1, (hn3, up1, z1, m3))

    do, doT = _mm(dm, full["fox_w_out"], name="fox_out_dx", tb=True, out_dtype=BF16, with_t=True)
    dw_fox_out = _mm(oT, dm, name="fox_out_dw", tk=2048)
    to_rows = lambda g: b16(g).reshape(N_DEV, -1, D)
    (dqtb, dk, dv, dfk, dfq), (r_up1, r_down1) = _foxt_bwd(
        qkvT, qkv, ka, _blocked(qkvT[:D].reshape(HP, 128, S), nb, tb), _blocked(doT.reshape(HP, 128, S), nb, tb), do,
        _blocked(aug, nb, tb), _blocked(lse, nb, tb), _blocked(oT.reshape(HP, 128, S), nb, tb), tb=tb, name="fox_attn_bwd",
        comm=[(_cols_split(b16(dw_up1)), False), (to_rows(dw_down1), False)])
    dq = dqtb.transpose(1, 3, 0, 2).reshape(S, D)
    dF = dfk[:, :, :2].transpose(1, 0, 2).reshape(S, HEADS) + dfq[:, :, :2, :].transpose(1, 3, 0, 2).reshape(S, HEADS)
    dF = jnp.pad(dF, ((0, 0), (0, 128 - HEADS)))
    dfl, dbf = _fgate_bwd(dF, fl, bf_pad, name="fox_fgate_bwd")
    dproj = jnp.concatenate([dq, dk, dv, dfl.astype(BF16)], axis=1)
    dhn = _mm(dproj, w_proj, name="fox_in_dx", tb=True)
    dw_proj = _mm(hn2, dproj, name="fox_in_dw", ta=True, tn=640, tk=2048)
    dw_fox_in = dw_proj[:, :full["fox_w_in"].shape[1]]
    dh, dg2, dm, dgate = norm_bwd(h2, dhn, dh, 2, dgate)

    dh, dm, dgate, dg1, dw_up0, dw_down0, dcw0, dcb0 = ffn_bwd(dh, dm, dgate, h1, 1, 0, (hn1, up0, z0, m1))

    dy2 = _mm(dm, full["s5_w_out"], name="s5_out_dx", tb=True, out_dtype=BF16)
    G["s5_w_out"] = _mm(y2, dm, name="s5_out_dw", ta=True, tk=2048)
    da, dgl = _glu_bwd(dy2, yg, gl, name="s5_glu_bwd")
    dyg_b = _mm(dgl, full["s5_w_glu"], name="s5_glu_dx", tb=True)
    G["s5_w_glu"] = _mm(yg, dgl, name="s5_glu_dw", ta=True, tk=2048)
    dyp, dd = _gelu_bwd(da, dyg_b, ypre, u, name="s5_gelu_bwd")
    row_grads = [dw_fox_out, dw_down0, G["s5_w_out"], G["s5_w_glu"]]
    (du, dwb, dwc, dlam), (r_up0, r_rows0, r_fox) = _s5_bwd(
        u, dyp, st, wb, wc, lam, s5_d, lt=lt, name="s5_scan_bwd",
        comm=[(_cols_split(b16(dw_up0)), False), (jnp.concatenate([to_rows(g) for g in row_grads], axis=1), False),
              (_cols_split(b16(dw_fox_in)), False)])
    dhn = _mm(du, full["s5_w_in"], name="s5_in_dx", tb=True)
    G["s5_w_in"] = _mm(hn0, du, name="s5_in_dw", ta=True, tk=2048)
    dh, dg0, _, _ = norm_bwd(h0, dhn, dh, 0, dgate)
    grad_x = dh[None]

    dlb_re, dlb_im, dbb_re, dbb_im, dc_re, dc_im = _s5_unpack(dwb, dwc, dlam)
    _, disc_vjp = jax.vjp(_s5_disc, s5_lam_re[0], s5_lam_im[0], s5_log_dt[0], s5_b_re[0], s5_b_im[0])
    dlam_re, dlam_im, dlog_dt, db_re, db_im = disc_vjp((dlb_re, dlb_im, dbb_re, dbb_im))

    small_full = {
        "norm_g": jnp.stack([dg0, dg1, dg2, dg3]).reshape(2, 2, D),
        "ffn_conv_w": jnp.stack([dcw0, dcw1]),
        "s5_lam_re": dlam_re[None], "s5_lam_im": dlam_im[None], "s5_log_dt": dlog_dt[None],
        "s5_b_re": db_re[None], "s5_b_im": db_im[None], "s5_c_re": dc_re[None], "s5_c_im": dc_im[None],
        "s5_d": dd, "fox_b_f": dbf[:, :HEADS], "ffn_conv_b": jnp.stack([dcb0, dcb1]), "final_g": d_final_g[0],
    }
    names = tuple(small_full)
    small_pack = _pack([small_full[n] for n in names], F32, 8 * N_DEV)
    r_in, r_small, g2 = _exchange_many([(to_rows(G["s5_w_in"]), False), (small_pack.reshape(N_DEV, -1, 1024), False),
                                        (_pack([jnp.concatenate(dmod, axis=0)], F32, 8), True)], name="last_exchange")
    delta, new_m, new_v = {}, {}, {}
    for n, r in (("s5_w_in", r_in), ("fox_w_in", r_fox)):
        outs = _sum8_adamw(r, W[n][0], M[n][0], V[n][0], name=f"sum_adamw_{n}")
        G[n], delta[n], new_m[n], new_v[n] = [o[None] for o in outs]
    (dmod_all,) = _unpack(g2, [(4, 3 * D)])
    d_ada_w, d_ada_b = _adaln_bwd(c_all, _columns_of_mod(dmod_all, me, ncol), name="adaln_bwd")
    G["ada_w"] = d_ada_w.reshape(ada_w.shape)
    G["ada_b"] = d_ada_b.reshape(ada_b.shape)
    rows0_sum = _sum8(r_rows0, name="sum_rows0_grads")
    down_rows = F // N_DEV
    G["fox_w_out"] = rows0_sum[:rows][None]
    G["s5_w_out"] = rows0_sum[rows + down_rows:2 * rows + down_rows][None]
    G["s5_w_glu"] = rows0_sum[2 * rows + down_rows:][None]
    G["ffn_w_down"] = jnp.stack([rows0_sum[rows:rows + down_rows], _sum8(r_down1, name="sum_down1_grads")])
    G["ffn_w_up"] = jnp.stack([_sum8(r_up0, name="sum_up0_grads"), _sum8(r_up1, name="sum_up1_grads")])

    g3 = _exchange(_sum8(r_small, name="sum_small_grads"), gather=True, name="gather_small_sums")
    summed = dict(zip(names, _unpack(g3.reshape(small_pack.shape), [small_full[n].shape for n in names])))
    G["norm_g"] = lax.dynamic_slice_in_dim(summed["norm_g"], me * norm_g.shape[-1], norm_g.shape[-1], axis=2)
    G["ffn_conv_w"] = lax.dynamic_slice_in_dim(summed["ffn_conv_w"], me * ffn_conv_w.shape[-1], ffn_conv_w.shape[-1], axis=2)
    for n in REPLICATED:
        G[n] = summed[n]

    small = tuple(n for n in WEIGHTS if n not in BIG and n != "ada_w")
    for n in WEIGHTS:
        if n not in small and n not in delta:
            delta[n], new_m[n], new_v[n] = _adamw_nd(W[n], G[n], M[n], V[n], name=f"adamw_{n}")
    packed = [_pack([src[n] for n in small], F32, 8) for src in (W, G, M, V)]
    for dst, buf in zip((delta, new_m, new_v), _adamw(*packed, name="adamw_small")):
        dst.update(zip(small, _unpack(buf, [W[n].shape for n in small])))

    return (loss, grad_x, *[G[n] for n in WEIGHTS], *[delta[n] for n in WEIGHTS], *[new_m[n] for n in WEIGHTS],
            *[new_v[n] for n in WEIGHTS])


def _columns_of_mod(dmod_all, me, ncol):
    flat = lax.dynamic_slice_in_dim(dmod_all, me * ncol, ncol, axis=2)
    return flat.transpose(1, 0, 2)
```
